```python
import math
import jax, jax.numpy as jnp
from jax import lax
import numpy as np

D_MODEL = 2048
BATCH = 8
SEQ = 8192
DEPTH = 1

HEAD_DIM = 64
N_Q_HEADS = 16
N_KV_HEADS = 2
GROUP = N_Q_HEADS // N_KV_HEADS
WINDOW = 128
BLOCK = 128
ROT_DIM = HEAD_DIM // 4
ROPE_THETA = 500000.0
Q_W = N_Q_HEADS * HEAD_DIM
KV_W = N_KV_HEADS * HEAD_DIM

SSM_W = D_MODEL // 2
SSM_GC = 16
SSM_G = SSM_W // SSM_GC
SSM_P = 64

D_FF = 4 * D_MODEL

MIX_W = Q_W + SSM_W
IN_W = Q_W + 2 * KV_W + SSM_W + 2 * D_MODEL
EPS = 1e-6

kernel_name = "hybrid_swa_sink_s5_gated_block"


def rms_norm(x, g):
    xf = x.astype(jnp.float32)
    y = xf * lax.rsqrt(jnp.mean(xf * xf, axis=-1, keepdims=True) + EPS)
    return (y * g.astype(jnp.float32)).astype(x.dtype)


def partial_rope(x, pos):
    half = ROT_DIM // 2
    inv = ROPE_THETA ** (-jnp.arange(half, dtype=jnp.float32) * 2.0 / ROT_DIM)
    ang = pos.astype(jnp.float32)[:, None] * inv[None, :]
    cos = jnp.cos(ang)[None, :, None, :]
    sin = jnp.sin(ang)[None, :, None, :]
    xr = x[..., :ROT_DIM].astype(jnp.float32)
    x1, x2 = xr[..., :half], xr[..., half:]
    rot = jnp.concatenate([x1 * cos - x2 * sin, x2 * cos + x1 * sin], axis=-1)
    return jnp.concatenate([rot.astype(x.dtype), x[..., ROT_DIM:]], axis=-1)


def sliding_window_gqa_sinks(q, k, v, sinks):
    B, L = q.shape[0], q.shape[1]
    nb = L // BLOCK
    qb = q.reshape(B, nb, BLOCK, N_KV_HEADS, GROUP, HEAD_DIM)
    pad = jnp.zeros((B, BLOCK, N_KV_HEADS, HEAD_DIM), k.dtype)
    kp = jnp.concatenate([pad, k], axis=1)
    vp = jnp.concatenate([pad, v], axis=1)
    shp = (B, nb, BLOCK, N_KV_HEADS, HEAD_DIM)
    kb = jnp.concatenate([kp[:, :-BLOCK].reshape(shp), kp[:, BLOCK:].reshape(shp)], axis=2)
    vb = jnp.concatenate([vp[:, :-BLOCK].reshape(shp), vp[:, BLOCK:].reshape(shp)], axis=2)
    s = jnp.einsum('bnqhgd,bnkhd->bnhgqk', qb, kb).astype(jnp.float32) / math.sqrt(HEAD_DIM)
    qi = jnp.arange(BLOCK)[:, None]
    kj = jnp.arange(2 * BLOCK)[None, :]
    rel = qi + BLOCK - kj
    kpos = jnp.arange(nb)[:, None, None] * BLOCK - BLOCK + kj[None]
    mask = (rel >= 0)[None] & (rel < WINDOW)[None] & (kpos >= 0)
    s = jnp.where(mask[None, :, None, None], s, jnp.finfo(jnp.float32).min)
    sink = jnp.broadcast_to(
        sinks.astype(jnp.float32).reshape(N_KV_HEADS, GROUP)[None, None, :, :, None, None],
        s.shape[:-1] + (1,))
    p = jax.nn.softmax(jnp.concatenate([s, sink], axis=-1), axis=-1)[..., :-1]
    o = jnp.einsum('bnhgqk,bnkhd->bnqhgd', p.astype(v.dtype), vb)
    return o.reshape(B, L, Q_W)


def s5_mixer(u, lam_re, lam_im, log_dt, b_re, b_im, c_re, c_im, d_skip, w_glu):
    B, L = u.shape[0], u.shape[1]
    f32 = jnp.float32
    ug = u.reshape(B, L, SSM_G, SSM_GC).astype(f32)
    lr, li = lam_re.astype(f32), lam_im.astype(f32)
    dt = jnp.exp(log_dt.astype(f32))[:, None]
    mag = jnp.exp(lr * dt)
    a_re, a_im = mag * jnp.cos(li * dt), mag * jnp.sin(li * dt)
    den = lr * lr + li * li
    nr, ni = a_re - 1.0, a_im
    coef_re = (nr * lr + ni * li) / den
    coef_im = (ni * lr - nr * li) / den
    br, bi = b_re.astype(f32), b_im.astype(f32)
    bb_re = coef_re[..., None] * br - coef_im[..., None] * bi
    bb_im = coef_re[..., None] * bi + coef_im[..., None] * br
    bu_re = jnp.einsum('blgc,gpc->blgp', ug, bb_re)
    bu_im = jnp.einsum('blgc,gpc->blgp', ug, bb_im)
    at_re = jnp.broadcast_to(a_re, bu_re.shape)
    at_im = jnp.broadcast_to(a_im, bu_im.shape)

    def combine(e1, e2):
        ar1, ai1, br1, bi1 = e1
        ar2, ai2, br2, bi2 = e2
        return (ar2 * ar1 - ai2 * ai1,
                ar2 * ai1 + ai2 * ar1,
                ar2 * br1 - ai2 * bi1 + br2,
                ar2 * bi1 + ai2 * br1 + bi2)

    _, _, xs_re, xs_im = lax.associative_scan(combine, (at_re, at_im, bu_re, bu_im), axis=1)
    y = (jnp.einsum('blgp,gcp->blgc', xs_re, c_re.astype(f32))
         - jnp.einsum('blgp,gcp->blgc', xs_im, c_im.astype(f32))
         + d_skip.astype(f32)[None, None] * ug)
    y = jax.nn.gelu(y.reshape(B, L, SSM_W)).astype(u.dtype)
    zg = y @ w_glu
    return zg[..., :SSM_W] * jax.nn.sigmoid(zg[..., SSM_W:])


def _fwd_setup_inputs(seed: int = 0) -> dict:
    key = jax.random.key(seed)
    ks = jax.random.split(key, 24)
    f32 = jnp.float32
    nrm = lambda k, shp, s: jax.random.normal(k, shp, f32) * s
    x = jax.random.normal(ks[0], (BATCH, SEQ, D_MODEL), f32)
    gain = lambda k: 1.0 + nrm(k, (DEPTH, D_MODEL), 0.02)
    n_idx = jnp.arange(SSM_P, dtype=f32)
    lam_re = -0.5 + nrm(ks[9], (DEPTH, SSM_G, SSM_P), 0.01)
    lam_im = jnp.pi * n_idx[None, None, :] + nrm(ks[10], (DEPTH, SSM_G, SSM_P), 0.01)
    log_dt = jax.random.uniform(ks[11], (DEPTH, SSM_G), f32, math.log(1e-3), math.log(1e-1))
    return {
        "x": x,
        "norm_mix_pre": gain(ks[1]),
        "norm_mix_post": gain(ks[2]),
        "norm_mlp_pre": gain(ks[3]),
        "norm_mlp_post": gain(ks[4]),
        "w_in": nrm(ks[5], (DEPTH, D_MODEL, IN_W), D_MODEL ** -0.5),
        "sinks": nrm(ks[6], (DEPTH, N_Q_HEADS), 0.5),
        "lam_re": lam_re,
        "lam_im": lam_im,
        "log_dt": log_dt,
        "b_re": nrm(ks[12], (DEPTH, SSM_G, SSM_P, SSM_GC), (2 * SSM_GC) ** -0.5),
        "b_im": nrm(ks[13], (DEPTH, SSM_G, SSM_P, SSM_GC), (2 * SSM_GC) ** -0.5),
        "c_re": nrm(ks[14], (DEPTH, SSM_G, SSM_GC, SSM_P), (2 * SSM_P) ** -0.5),
        "c_im": nrm(ks[15], (DEPTH, SSM_G, SSM_GC, SSM_P), (2 * SSM_P) ** -0.5),
        "d_skip": nrm(ks[16], (DEPTH, SSM_G, SSM_GC), 1.0),
        "w_glu": nrm(ks[17], (DEPTH, SSM_W, 2 * SSM_W), SSM_W ** -0.5),
        "w_branch": nrm(ks[18], (DEPTH, MIX_W, D_MODEL), (MIX_W // 2) ** -0.5),
        "w_out": nrm(ks[19], (DEPTH, D_MODEL, D_MODEL), D_MODEL ** -0.5),
        "w_up": nrm(ks[20], (DEPTH, D_MODEL, D_FF), D_MODEL ** -0.5),
        "w_down": nrm(ks[21], (DEPTH, D_FF, D_MODEL), D_FF ** -0.5),
    }


def _fwd_reference(x, norm_mix_pre, norm_mix_post, norm_mlp_pre, norm_mlp_post, w_in, sinks,
              lam_re, lam_im, log_dt, b_re, b_im, c_re, c_im, d_skip, w_glu,
              w_branch, w_out, w_up, w_down):
    B, L, _ = x.shape
    pos = jnp.arange(L)
    o1 = Q_W
    o2 = o1 + KV_W
    o3 = o2 + KV_W
    o4 = o3 + SSM_W
    o5 = o4 + D_MODEL
    for l in range(DEPTH):
        h = rms_norm(x, norm_mix_pre[l])
        z = h @ w_in[l]
        q = z[..., :o1].reshape(B, L, N_Q_HEADS, HEAD_DIM)
        k = z[..., o1:o2].reshape(B, L, N_KV_HEADS, HEAD_DIM)
        v = z[..., o2:o3].reshape(B, L, N_KV_HEADS, HEAD_DIM)
        u = z[..., o3:o4]
        g_attn = jax.nn.sigmoid(z[..., o4:o5].astype(jnp.float32)).astype(x.dtype)
        g_ssm = jax.nn.sigmoid(z[..., o5:].astype(jnp.float32)).astype(x.dtype)
        q = partial_rope(q, pos)
        k = partial_rope(k, pos)
        o_attn = sliding_window_gqa_sinks(q, k, v, sinks[l])
        o_ssm = s5_mixer(u, lam_re[l], lam_im[l], log_dt[l], b_re[l], b_im[l],
                         c_re[l], c_im[l], d_skip[l], w_glu[l])
        y_attn = o_attn @ w_branch[l][:Q_W]
        y_ssm = o_ssm @ w_branch[l][Q_W:]
        mixed = (g_attn * y_attn + g_ssm * y_ssm) @ w_out[l]
        x = x + rms_norm(mixed, norm_mix_post[l])
        h2 = rms_norm(x, norm_mlp_pre[l])
        a = jax.nn.relu(h2 @ w_up[l])
        x = x + rms_norm((a * a) @ w_down[l], norm_mlp_post[l])
    return x


import jax as _jax
import jax.numpy as _jnp

TWIN_FORMAT = 'train_step'
FWD_PARAMS = ['x', 'norm_mix_pre', 'norm_mix_post', 'norm_mlp_pre', 'norm_mlp_post', 'w_in', 'sinks', 'lam_re', 'lam_im', 'log_dt', 'b_re', 'b_im', 'c_re', 'c_im', 'd_skip', 'w_glu', 'w_branch', 'w_out', 'w_up', 'w_down']
TWIN_WEIGHTS = ['norm_mix_pre', 'norm_mix_post', 'norm_mlp_pre', 'norm_mlp_post', 'w_in', 'sinks', 'lam_re', 'lam_im', 'log_dt', 'b_re', 'b_im', 'c_re', 'c_im', 'd_skip', 'w_glu', 'w_branch', 'w_out', 'w_up', 'w_down']
TWIN_DIFF_INPUT = 'x'
TWIN_INPUTS = ['x', 'norm_mix_pre', 'norm_mix_post', 'norm_mlp_pre', 'norm_mlp_post', 'w_in', 'sinks', 'lam_re', 'lam_im', 'log_dt', 'b_re', 'b_im', 'c_re', 'c_im', 'd_skip', 'w_glu', 'w_branch', 'w_out', 'w_up', 'w_down', 'loss_target', 'm_norm_mix_pre', 'm_norm_mix_post', 'm_norm_mlp_pre', 'm_norm_mlp_post', 'm_w_in', 'm_sinks', 'm_lam_re', 'm_lam_im', 'm_log_dt', 'm_b_re', 'm_b_im', 'm_c_re', 'm_c_im', 'm_d_skip', 'm_w_glu', 'm_w_branch', 'm_w_out', 'm_w_up', 'm_w_down', 'v_norm_mix_pre', 'v_norm_mix_post', 'v_norm_mlp_pre', 'v_norm_mlp_post', 'v_w_in', 'v_sinks', 'v_lam_re', 'v_lam_im', 'v_log_dt', 'v_b_re', 'v_b_im', 'v_c_re', 'v_c_im', 'v_d_skip', 'v_w_glu', 'v_w_branch', 'v_w_out', 'v_w_up', 'v_w_down']
TWIN_OUTPUTS = ['loss', 'grad_x', 'grad_norm_mix_pre', 'grad_norm_mix_post', 'grad_norm_mlp_pre', 'grad_norm_mlp_post', 'grad_w_in', 'grad_sinks', 'grad_lam_re', 'grad_lam_im', 'grad_log_dt', 'grad_b_re', 'grad_b_im', 'grad_c_re', 'grad_c_im', 'grad_d_skip', 'grad_w_glu', 'grad_w_branch', 'grad_w_out', 'grad_w_up', 'grad_w_down', 'delta_norm_mix_pre', 'delta_norm_mix_post', 'delta_norm_mlp_pre', 'delta_norm_mlp_post', 'delta_w_in', 'delta_sinks', 'delta_lam_re', 'delta_lam_im', 'delta_log_dt', 'delta_b_re', 'delta_b_im', 'delta_c_re', 'delta_c_im', 'delta_d_skip', 'delta_w_glu', 'delta_w_branch', 'delta_w_out', 'delta_w_up', 'delta_w_down', 'new_m_norm_mix_pre', 'new_m_norm_mix_post', 'new_m_norm_mlp_pre', 'new_m_norm_mlp_post', 'new_m_w_in', 'new_m_sinks', 'new_m_lam_re', 'new_m_lam_im', 'new_m_log_dt', 'new_m_b_re', 'new_m_b_im', 'new_m_c_re', 'new_m_c_im', 'new_m_d_skip', 'new_m_w_glu', 'new_m_w_branch', 'new_m_w_out', 'new_m_w_up', 'new_m_w_down', 'new_v_norm_mix_pre', 'new_v_norm_mix_post', 'new_v_norm_mlp_pre', 'new_v_norm_mlp_post', 'new_v_w_in', 'new_v_sinks', 'new_v_lam_re', 'new_v_lam_im', 'new_v_log_dt', 'new_v_b_re', 'new_v_b_im', 'new_v_c_re', 'new_v_c_im', 'new_v_d_skip', 'new_v_w_glu', 'new_v_w_branch', 'new_v_w_out', 'new_v_w_up', 'new_v_w_down']
TWIN_LEAF_KINDS = {'loss': 'loss', 'grad_x': 'grad_x', 'grad_norm_mix_pre': 'grad_w', 'grad_norm_mix_post': 'grad_w', 'grad_norm_mlp_pre': 'grad_w', 'grad_norm_mlp_post': 'grad_w', 'grad_w_in': 'grad_w', 'grad_sinks': 'grad_w', 'grad_lam_re': 'grad_w', 'grad_lam_im': 'grad_w', 'grad_log_dt': 'grad_w', 'grad_b_re': 'grad_w', 'grad_b_im': 'grad_w', 'grad_c_re': 'grad_w', 'grad_c_im': 'grad_w', 'grad_d_skip': 'grad_w', 'grad_w_glu': 'grad_w', 'grad_w_branch': 'grad_w', 'grad_w_out': 'grad_w', 'grad_w_up': 'grad_w', 'grad_w_down': 'grad_w', 'delta_norm_mix_pre': 'delta_w', 'delta_norm_mix_post': 'delta_w', 'delta_norm_mlp_pre': 'delta_w', 'delta_norm_mlp_post': 'delta_w', 'delta_w_in': 'delta_w', 'delta_sinks': 'delta_w', 'delta_lam_re': 'delta_w', 'delta_lam_im': 'delta_w', 'delta_log_dt': 'delta_w', 'delta_b_re': 'delta_w', 'delta_b_im': 'delta_w', 'delta_c_re': 'delta_w', 'delta_c_im': 'delta_w', 'delta_d_skip': 'delta_w', 'delta_w_glu': 'delta_w', 'delta_w_branch': 'delta_w', 'delta_w_out': 'delta_w', 'delta_w_up': 'delta_w', 'delta_w_down': 'delta_w', 'new_m_norm_mix_pre': 'new_m', 'new_m_norm_mix_post': 'new_m', 'new_m_norm_mlp_pre': 'new_m', 'new_m_norm_mlp_post': 'new_m', 'new_m_w_in': 'new_m', 'new_m_sinks': 'new_m', 'new_m_lam_re': 'new_m', 'new_m_lam_im': 'new_m', 'new_m_log_dt': 'new_m', 'new_m_b_re': 'new_m', 'new_m_b_im': 'new_m', 'new_m_c_re': 'new_m', 'new_m_c_im': 'new_m', 'new_m_d_skip': 'new_m', 'new_m_w_glu': 'new_m', 'new_m_w_branch': 'new_m', 'new_m_w_out': 'new_m', 'new_m_w_up': 'new_m', 'new_m_w_down': 'new_m', 'new_v_norm_mix_pre': 'new_v', 'new_v_norm_mix_post': 'new_v', 'new_v_norm_mlp_pre': 'new_v', 'new_v_norm_mlp_post': 'new_v', 'new_v_w_in': 'new_v', 'new_v_sinks': 'new_v', 'new_v_lam_re': 'new_v', 'new_v_lam_im': 'new_v', 'new_v_log_dt': 'new_v', 'new_v_b_re': 'new_v', 'new_v_b_im': 'new_v', 'new_v_c_re': 'new_v', 'new_v_c_im': 'new_v', 'new_v_d_skip': 'new_v', 'new_v_w_glu': 'new_v', 'new_v_w_branch': 'new_v', 'new_v_w_out': 'new_v', 'new_v_w_up': 'new_v', 'new_v_w_down': 'new_v'}


def _forward(args):
    return _fwd_reference(*[args[k] for k in FWD_PARAMS])


def _output_shape():
    def fwd():
        inp = _fwd_setup_inputs(0)
        return _fwd_reference(*[inp[k] for k in FWD_PARAMS])
    out = _jax.eval_shape(fwd)
    return out.shape, out.dtype

N_MICROBATCH = 1
ADAM_LR = 0.001
ADAM_B1 = 0.9
ADAM_B2 = 0.999
ADAM_EPS = 1e-08
ADAM_WD = 0.01
ADAM_STEP = 10
PER_EXAMPLE_BATCH_AXIS = {'x': 0, 'loss_target': 0}
SHARED_INPUTS = []
_WEIGHT_DTYPES = {'norm_mix_pre': _jnp.float32, 'norm_mix_post': _jnp.float32, 'norm_mlp_pre': _jnp.float32, 'norm_mlp_post': _jnp.float32, 'w_in': _jnp.float32, 'sinks': _jnp.float32, 'lam_re': _jnp.float32, 'lam_im': _jnp.float32, 'log_dt': _jnp.float32, 'b_re': _jnp.float32, 'b_im': _jnp.float32, 'c_re': _jnp.float32, 'c_im': _jnp.float32, 'd_skip': _jnp.float32, 'w_glu': _jnp.float32, 'w_branch': _jnp.float32, 'w_out': _jnp.float32, 'w_up': _jnp.float32, 'w_down': _jnp.float32}
MOMENT_SCALE = {'norm_mix_pre': 5.492694e-01, 'norm_mix_post': 3.340019e+01, 'norm_mlp_pre': 2.679757e+00, 'norm_mlp_post': 3.372816e+01, 'w_in': 2.843047e-01, 'sinks': 1.196917e-01, 'lam_re': 2.802727e-02, 'lam_im': 2.512668e-02, 'log_dt': 2.171067e+01, 'b_re': 1.734578e-02, 'b_im': 1.629005e-02, 'c_re': 3.506695e-02, 'c_im': 3.296423e-02, 'd_skip': 9.376862e+00, 'w_glu': 6.124016e+00, 'w_branch': 4.477419e+00, 'w_out': 6.204850e+00, 'w_up': 1.273339e+00, 'w_down': 5.682438e+00}


def _to_microbatches(a, axis):
    t = _jnp.moveaxis(a, axis, 0)
    t = t.reshape((N_MICROBATCH, t.shape[0] // N_MICROBATCH) + t.shape[1:])
    return _jnp.moveaxis(t, 1, axis + 1)


def setup_inputs(seed: int = 0) -> dict:
    inp = _fwd_setup_inputs(seed)
    key = _jax.random.fold_in(_jax.random.key(seed), 7919)
    shape, _ = _output_shape()
    out = dict(inp)
    out["loss_target"] = _jax.random.normal(_jax.random.fold_in(key, 0), shape, _jnp.float32)
    for i, name in enumerate(TWIN_WEIGHTS):
        w = inp[name].astype(_jnp.float32)
        if MOMENT_SCALE is None:
            s = _jnp.sqrt(_jnp.mean(_jnp.square(w)) + 1e-30)
        else:
            s = MOMENT_SCALE[name]
        km, kv = _jax.random.split(_jax.random.fold_in(key, i + 1))
        out[name] = w
        out["m_" + name] = s * _jax.random.normal(km, w.shape, _jnp.float32)
        out["v_" + name] = (s * s) * _jax.random.uniform(kv, w.shape, _jnp.float32, 0.5, 1.5)
    if N_MICROBATCH > 1:
        for name, axis in PER_EXAMPLE_BATCH_AXIS.items():
            out[name] = _to_microbatches(out[name], axis)
    return {'x': out['x'], 'norm_mix_pre': out['norm_mix_pre'], 'norm_mix_post': out['norm_mix_post'], 'norm_mlp_pre': out['norm_mlp_pre'], 'norm_mlp_post': out['norm_mlp_post'], 'w_in': out['w_in'], 'sinks': out['sinks'], 'lam_re': out['lam_re'], 'lam_im': out['lam_im'], 'log_dt': out['log_dt'], 'b_re': out['b_re'], 'b_im': out['b_im'], 'c_re': out['c_re'], 'c_im': out['c_im'], 'd_skip': out['d_skip'], 'w_glu': out['w_glu'], 'w_branch': out['w_branch'], 'w_out': out['w_out'], 'w_up': out['w_up'], 'w_down': out['w_down'], 'loss_target': out['loss_target'], 'm_norm_mix_pre': out['m_norm_mix_pre'], 'm_norm_mix_post': out['m_norm_mix_post'], 'm_norm_mlp_pre': out['m_norm_mlp_pre'], 'm_norm_mlp_post': out['m_norm_mlp_post'], 'm_w_in': out['m_w_in'], 'm_sinks': out['m_sinks'], 'm_lam_re': out['m_lam_re'], 'm_lam_im': out['m_lam_im'], 'm_log_dt': out['m_log_dt'], 'm_b_re': out['m_b_re'], 'm_b_im': out['m_b_im'], 'm_c_re': out['m_c_re'], 'm_c_im': out['m_c_im'], 'm_d_skip': out['m_d_skip'], 'm_w_glu': out['m_w_glu'], 'm_w_branch': out['m_w_branch'], 'm_w_out': out['m_w_out'], 'm_w_up': out['m_w_up'], 'm_w_down': out['m_w_down'], 'v_norm_mix_pre': out['v_norm_mix_pre'], 'v_norm_mix_post': out['v_norm_mix_post'], 'v_norm_mlp_pre': out['v_norm_mlp_pre'], 'v_norm_mlp_post': out['v_norm_mlp_post'], 'v_w_in': out['v_w_in'], 'v_sinks': out['v_sinks'], 'v_lam_re': out['v_lam_re'], 'v_lam_im': out['v_lam_im'], 'v_log_dt': out['v_log_dt'], 'v_b_re': out['v_b_re'], 'v_b_im': out['v_b_im'], 'v_c_re': out['v_c_re'], 'v_c_im': out['v_c_im'], 'v_d_skip': out['v_d_skip'], 'v_w_glu': out['v_w_glu'], 'v_w_branch': out['v_w_branch'], 'v_w_out': out['v_w_out'], 'v_w_up': out['v_w_up'], 'v_w_down': out['v_w_down']}


def _loss(weights, diff, rest, loss_target):
    with _jax.named_scope("forward"):
        args = {**rest, TWIN_DIFF_INPUT: diff, **{k: w.astype(_WEIGHT_DTYPES[k]) for k, w in weights.items()}}
        y = _forward(args)
    with _jax.named_scope("loss_head"):
        err = _jnp.square(y.astype(_jnp.float32) - loss_target)
        return 0.5 * _jnp.sum(_jnp.mean(err, axis=-1)) if err.ndim else 0.5 * err


def _adamw(w, g, m, v):
    m = ADAM_B1 * m + (1.0 - ADAM_B1) * g
    v = ADAM_B2 * v + (1.0 - ADAM_B2) * _jnp.square(g)
    m_hat = m / (1.0 - ADAM_B1 ** ADAM_STEP)
    v_hat = v / (1.0 - ADAM_B2 ** ADAM_STEP)
    delta = -ADAM_LR * (m_hat / (_jnp.sqrt(v_hat) + ADAM_EPS) + ADAM_WD * w)
    return delta, m, v


def reference(x, norm_mix_pre, norm_mix_post, norm_mlp_pre, norm_mlp_post, w_in, sinks, lam_re, lam_im, log_dt, b_re, b_im, c_re, c_im, d_skip, w_glu, w_branch, w_out, w_up, w_down, loss_target, m_norm_mix_pre, m_norm_mix_post, m_norm_mlp_pre, m_norm_mlp_post, m_w_in, m_sinks, m_lam_re, m_lam_im, m_log_dt, m_b_re, m_b_im, m_c_re, m_c_im, m_d_skip, m_w_glu, m_w_branch, m_w_out, m_w_up, m_w_down, v_norm_mix_pre, v_norm_mix_post, v_norm_mlp_pre, v_norm_mlp_post, v_w_in, v_sinks, v_lam_re, v_lam_im, v_log_dt, v_b_re, v_b_im, v_c_re, v_c_im, v_d_skip, v_w_glu, v_w_branch, v_w_out, v_w_up, v_w_down):
    given = dict(x=x, norm_mix_pre=norm_mix_pre, norm_mix_post=norm_mix_post, norm_mlp_pre=norm_mlp_pre, norm_mlp_post=norm_mlp_post, w_in=w_in, sinks=sinks, lam_re=lam_re, lam_im=lam_im, log_dt=log_dt, b_re=b_re, b_im=b_im, c_re=c_re, c_im=c_im, d_skip=d_skip, w_glu=w_glu, w_branch=w_branch, w_out=w_out, w_up=w_up, w_down=w_down, loss_target=loss_target, m_norm_mix_pre=m_norm_mix_pre, m_norm_mix_post=m_norm_mix_post, m_norm_mlp_pre=m_norm_mlp_pre, m_norm_mlp_post=m_norm_mlp_post, m_w_in=m_w_in, m_sinks=m_sinks, m_lam_re=m_lam_re, m_lam_im=m_lam_im, m_log_dt=m_log_dt, m_b_re=m_b_re, m_b_im=m_b_im, m_c_re=m_c_re, m_c_im=m_c_im, m_d_skip=m_d_skip, m_w_glu=m_w_glu, m_w_branch=m_w_branch, m_w_out=m_w_out, m_w_up=m_w_up, m_w_down=m_w_down, v_norm_mix_pre=v_norm_mix_pre, v_norm_mix_post=v_norm_mix_post, v_norm_mlp_pre=v_norm_mlp_pre, v_norm_mlp_post=v_norm_mlp_post, v_w_in=v_w_in, v_sinks=v_sinks, v_lam_re=v_lam_re, v_lam_im=v_lam_im, v_log_dt=v_log_dt, v_b_re=v_b_re, v_b_im=v_b_im, v_c_re=v_c_re, v_c_im=v_c_im, v_d_skip=v_d_skip, v_w_glu=v_w_glu, v_w_branch=v_w_branch, v_w_out=v_w_out, v_w_up=v_w_up, v_w_down=v_w_down)
    weights = {n: given[n] for n in TWIN_WEIGHTS}
    shared = {n: given[n] for n in SHARED_INPUTS}
    per_example = {n: given[n] for n in ['x']}
    grad_fn = _jax.value_and_grad(_loss, argnums=(0, 1))

    def one_microbatch(ex, loss_target):
        ex = dict(ex)
        diff = ex.pop(TWIN_DIFF_INPUT)
        return grad_fn(weights, diff, {**shared, **ex}, loss_target)

    if N_MICROBATCH == 1:
        loss, (grad_w, grad_x) = one_microbatch(per_example, given["loss_target"])
    else:
        def body(carry, xs):
            loss_sum, grad_sum = carry
            l_k, (gw_k, gx_k) = one_microbatch(xs[0], xs[1])
            with _jax.named_scope("update"):
                return (loss_sum + l_k, _jax.tree.map(_jnp.add, grad_sum, gw_k)), gx_k

        init = (_jnp.zeros((), _jnp.float32), _jax.tree.map(_jnp.zeros_like, weights))
        (loss, grad_w), grad_x = _jax.lax.scan(body, init, (per_example, given["loss_target"]))
    with _jax.named_scope("update"):
        delta_w, new_m, new_v = {}, {}, {}
        for n in TWIN_WEIGHTS:
            delta_w[n], new_m[n], new_v[n] = _adamw(weights[n], grad_w[n], given["m_" + n], given["v_" + n])
    return (loss, grad_x, *[grad_w[n] for n in TWIN_WEIGHTS], *[delta_w[n] for n in TWIN_WEIGHTS],
            *[new_m[n] for n in TWIN_WEIGHTS], *[new_v[n] for n in TWIN_WEIGHTS])
```

```python
import functools
import math

import jax
import jax.numpy as jnp
from jax import lax
from jax.experimental import pallas as pl
from jax.experimental.pallas import tpu as pltpu

F32 = jnp.float32
BF16 = jnp.bfloat16
SDS = jax.ShapeDtypeStruct

D_MODEL = 2048
HEAD_DIM = 64
N_Q_HEADS = 16
ATT_BLOCK = 128
ROT_DIM = 16
ROPE_THETA = 500000.0
Q_W = 1024
KV_W = 128
SSM_W = 1024
SSM_G = 64
SSM_GC = 16
SSM_P = 64
N_STATE = SSM_G * SSM_P
LANES = 128
SUBLANES = 8
N_LG = N_STATE // LANES
N_JB = 8
LG_PER_JB = N_LG // N_JB
D_FF = 8192
ZA_W = Q_W + 2 * KV_W + SSM_W
EPS = 1e-6
S5_CHUNK = 256
S5_SEG = S5_CHUNK // SUBLANES
VMEM_LIMIT = 56 * 1024 * 1024
NEG = -1e30

ADAM_LR = 0.001
ADAM_B1 = 0.9
ADAM_B2 = 0.999
ADAM_EPS = 1e-08
ADAM_WD = 0.01
ADAM_STEP = 10

MESH = pl.DeviceIdType.MESH


def _cp(sem):
    return pltpu.CompilerParams(dimension_semantics=sem, vmem_limit_bytes=VMEM_LIMIT)


def _mm(a, b, *, mode, out_dtype, tm, tn, tk, name, a_fn=None, epi=None, extras=()):
    if mode == "nn":
        (M, K), (K2, N) = a.shape, b.shape
    elif mode == "nt":
        (M, K), (N, K2) = a.shape, b.shape
    else:
        (K, M), (K2, N) = a.shape, b.shape
    assert K == K2, (a.shape, b.shape, mode)
    tm, tn, tk = min(tm, M), min(tn, N), min(tk, K)
    assert M % tm == 0 and N % tn == 0 and K % tk == 0, (M, N, K, tm, tn, tk)
    nk = K // tk
    if mode == "tn":
        a_spec = pl.BlockSpec((tk, tm), lambda i, j, k: (k, i))
        ca = 0
    else:
        a_spec = pl.BlockSpec((tm, tk), lambda i, j, k: (i, k))
        ca = 1
    if mode == "nt":
        b_spec = pl.BlockSpec((tn, tk), lambda i, j, k: (j, k))
        cb = 1
    else:
        b_spec = pl.BlockSpec((tk, tn), lambda i, j, k: (k, j))
        cb = 0
    dims = (((ca,), (cb,)), ((), ()))
    ne = len(extras)

    def body(a_ref, b_ref, *rest):
        ex = rest[:ne]
        o_ref = rest[ne]
        av = a_ref[...]
        if a_fn is not None:
            av = a_fn(av.astype(F32))
        p = lax.dot_general(av.astype(BF16), b_ref[...].astype(BF16), dims, preferred_element_type=F32)

        def fin(v):
            if epi is not None:
                v = epi(v, *[e[...] for e in ex])
            o_ref[...] = v.astype(out_dtype)

        if nk == 1:
            fin(p)
        else:
            acc = rest[ne + 1]
            k = pl.program_id(2)

            @pl.when(k == 0)
            def _():
                acc[...] = p

            @pl.when(k > 0)
            def _():
                acc[...] += p

            @pl.when(k == nk - 1)
            def _():
                fin(acc[...])

    return pl.pallas_call(
        body,
        name=name,
        grid=(M // tm, N // tn, nk),
        in_specs=[a_spec, b_spec] + [pl.BlockSpec((tm, tn), lambda i, j, k: (i, j)) for _ in extras],
        out_specs=pl.BlockSpec((tm, tn), lambda i, j, k: (i, j)),
        out_shape=SDS((M, N), out_dtype),
        scratch_shapes=[pltpu.VMEM((tm, tn), F32)] if nk > 1 else [],
        compiler_params=_cp(("parallel", "parallel", "arbitrary")),
    )(a, b, *extras)


def _rowwise(fn, rows, bcasts, outs, accs, *, tr, name):
    T = rows[0][0].shape[0]
    tr = min(tr, T)
    assert T % tr == 0
    nr, nb, no, na = len(rows), len(bcasts), len(outs), len(accs)
    in_specs = [pl.BlockSpec((tr, w), functools.partial(lambda i, c: (i, c), c=cb)) for (_, w, cb) in rows]
    in_specs += [pl.BlockSpec(b.shape, lambda i: (0, 0)) for b in bcasts]
    out_shape = [SDS((T, w), dt) for (w, dt) in outs] + [SDS(s, F32) for s in accs]
    out_specs = [pl.BlockSpec((tr, w), lambda i: (i, 0)) for (w, _) in outs]
    out_specs += [pl.BlockSpec(s, lambda i: (0, 0)) for s in accs]

    def body(*refs):
        ins = [r[...] for r in refs[:nr + nb]]
        o_refs = refs[nr + nb:nr + nb + no]
        a_refs = refs[nr + nb + no:]
        ro, ao = fn(*ins)
        for r, v in zip(o_refs, ro):
            r[...] = v.astype(r.dtype)
        if na:
            @pl.when(pl.program_id(0) == 0)
            def _():
                for r in a_refs:
                    r[...] = jnp.zeros(r.shape, F32)

            for r, v in zip(a_refs, ao):
                r[...] += v

    res = pl.pallas_call(
        body,
        name=name,
        grid=(T // tr,),
        in_specs=in_specs,
        out_specs=out_specs,
        out_shape=out_shape,
        compiler_params=_cp(("arbitrary",) if na else ("parallel",)),
    )(*[r[0] for r in rows], *bcasts)
    return res


def _rms(v):
    r = lax.rsqrt(jnp.mean(v * v, axis=-1, keepdims=True) + EPS)
    return v * r, r


def _rms_bwd(dy, xn, r, g):
    dxn = dy * g
    dv = r * (dxn - xn * jnp.mean(dxn * xn, axis=-1, keepdims=True))
    return dv, jnp.sum(dy * xn, axis=0, keepdims=True)


def _sig(v):
    return 1.0 / (1.0 + jnp.exp(-v))


_GELU_C = math.sqrt(2.0 / math.pi)


def _gelu(v):
    return 0.5 * v * (1.0 + jnp.tanh(_GELU_C * (v + 0.044715 * v * v * v)))


def _gelu_grad(v):
    t = jnp.tanh(_GELU_C * (v + 0.044715 * v * v * v))
    return 0.5 * (1.0 + t) + 0.5 * v * (1.0 - t * t) * _GELU_C * (1.0 + 3.0 * 0.044715 * v * v)


def _rope(v, c, s, sign):
    w = v.shape[1]
    m = lax.broadcasted_iota(jnp.int32, v.shape, 1) % HEAD_DIM
    p = jnp.where(m < ROT_DIM // 2, -pltpu.roll(v, w - ROT_DIM // 2, 1), pltpu.roll(v, ROT_DIM // 2, 1))
    return v * c + sign * (p * s)


def _rope_tables(T):
    half = ROT_DIM // 2
    inv = ROPE_THETA ** (-jnp.arange(half, dtype=F32) * 2.0 / ROT_DIM)
    ang = jnp.arange(T).astype(F32)[:, None] * inv[None, :]
    cos, sin = jnp.cos(ang), jnp.sin(ang)
    one = jnp.ones((T, HEAD_DIM - ROT_DIM), F32)
    c64 = jnp.concatenate([cos, cos, one], axis=1)
    s64 = jnp.concatenate([sin, sin, 0.0 * one], axis=1)
    return jnp.tile(c64, (1, 2)), jnp.tile(s64, (1, 2))


def _dup_half(m, lo):
    lane = lax.broadcasted_iota(jnp.int32, m.shape, 1)
    sw = pltpu.roll(m, HEAD_DIM, 1)
    return jnp.where(lane < HEAD_DIM, m, sw) if lo else jnp.where(lane >= HEAD_DIM, m, sw)


def _attn_mask(i):
    qi = lax.broadcasted_iota(jnp.int32, (ATT_BLOCK, 2 * ATT_BLOCK), 0)
    kj = lax.broadcasted_iota(jnp.int32, (ATT_BLOCK, 2 * ATT_BLOCK), 1)
    rel = qi + ATT_BLOCK - kj
    return (rel >= 0) & (rel < ATT_BLOCK) & ((kj >= ATT_BLOCK) | (i > 0))


_NT = (((1,), (1,)), ((), ()))
_TN = (((0,), (0,)), ((), ()))


def _attn_fwd(za, cos, sin, sinks):
    T = za.shape[0]
    nb = T // ATT_BLOCK
    kvb = Q_W // (2 * KV_W)

    def body(sink_ref, q_ref, kvp_ref, kvc_ref, cc_ref, sc_ref, cp_ref, sp_ref, o_ref):
        i = pl.program_id(0)
        cc, sc, cp, sp = cc_ref[...], sc_ref[...], cp_ref[...], sp_ref[...]
        q = (_rope(q_ref[...], jnp.tile(cc, (1, 8)), jnp.tile(sc, (1, 8)), 1.0) * 0.125).astype(BF16)
        kvp, kvc = kvp_ref[...], kvc_ref[...]
        k = jnp.concatenate([_rope(kvp[:, :KV_W], cp, sp, 1.0), _rope(kvc[:, :KV_W], cc, sc, 1.0)], axis=0).astype(BF16)
        v = jnp.concatenate([kvp[:, KV_W:], kvc[:, KV_W:]], axis=0).astype(BF16)
        ok = _attn_mask(i)
        lane = lax.broadcasted_iota(jnp.int32, (ATT_BLOCK, LANES), 1)
        for kvh in range(2):
            k2 = _dup_half(k, kvh == 0)
            v2 = _dup_half(v, kvh == 0)
            for pair in range(4):
                c0 = (kvh * 4 + pair) * LANES
                q2 = q[:, c0:c0 + LANES]
                halves = []
                for hf in range(2):
                    sink = sink_ref[0, 2 * (kvh * 4 + pair) + hf]
                    qm = jnp.where((lane < HEAD_DIM) == (hf == 0), q2, jnp.zeros_like(q2))
                    s = lax.dot_general(qm, k2, _NT, preferred_element_type=F32)
                    s = jnp.where(ok, s, NEG)
                    m = jnp.maximum(jnp.max(s, axis=1, keepdims=True), sink)
                    e = jnp.exp(s - m)
                    den = jnp.sum(e, axis=1, keepdims=True) + jnp.exp(sink - m)
                    p = (e * (1.0 / den)).astype(BF16)
                    halves.append(jnp.dot(p, v2, preferred_element_type=F32))
                o_ref[:, c0:c0 + LANES] = jnp.where(lane < HEAD_DIM, halves[0], halves[1]).astype(BF16)

    blk = lambda w, f: pl.BlockSpec((ATT_BLOCK, w), f)
    return pl.pallas_call(
        body,
        name="attn_fwd",
        grid=(nb,),
        in_specs=[
            pl.BlockSpec(memory_space=pltpu.SMEM),
            blk(Q_W, lambda i: (i, 0)),
            blk(2 * KV_W, lambda i: (jnp.maximum(i - 1, 0), kvb)),
            blk(2 * KV_W, lambda i: (i, kvb)),
            blk(LANES, lambda i: (i, 0)),
            blk(LANES, lambda i: (i, 0)),
            blk(LANES, lambda i: (jnp.maximum(i - 1, 0), 0)),
            blk(LANES, lambda i: (jnp.maximum(i - 1, 0), 0)),
        ],
        out_specs=blk(Q_W, lambda i: (i, 0)),
        out_shape=SDS((T, Q_W), BF16),
        compiler_params=_cp(("parallel",)),
    )(sinks, za, za, za, cos, sin, cos, sin)


def _attn_bwd(za, cos, sin, sinks, o, do):
    T = za.shape[0]
    nb = T // ATT_BLOCK
    kvb = Q_W // (2 * KV_W)

    def body(sink_ref, q_ref, kvp_ref, kvc_ref, cc_ref, sc_ref, cp_ref, sp_ref, o_ref, do_ref,
             dq_ref, dkv_ref, dsk_ref, carry, dqs):
        i = pl.program_id(0)

        @pl.when(i == 0)
        def _():
            carry[...] = jnp.zeros(carry.shape, F32)
            dsk_ref[...] = jnp.zeros(dsk_ref.shape, F32)

        @pl.when(i < nb)
        def _():
            cc, sc, cp, sp = cc_ref[...], sc_ref[...], cp_ref[...], sp_ref[...]
            ccq, scq = jnp.tile(cc, (1, 8)), jnp.tile(sc, (1, 8))
            q = (_rope(q_ref[...], ccq, scq, 1.0) * 0.125).astype(BF16)
            kvp, kvc = kvp_ref[...], kvc_ref[...]
            k = jnp.concatenate([_rope(kvp[:, :KV_W], cp, sp, 1.0), _rope(kvc[:, :KV_W], cc, sc, 1.0)], axis=0).astype(BF16)
            v = jnp.concatenate([kvp[:, KV_W:], kvc[:, KV_W:]], axis=0).astype(BF16)
            ok = _attn_mask(i)
            lane = lax.broadcasted_iota(jnp.int32, (ATT_BLOCK, LANES), 1)
            lane_kv = lax.broadcasted_iota(jnp.int32, (2 * ATT_BLOCK, LANES), 1)
            lane_s = lax.broadcasted_iota(jnp.int32, (1, LANES), 1)
            dsk = jnp.zeros((1, LANES), F32)
            dk_h, dv_h = [], []
            for kvh in range(2):
                k2 = _dup_half(k, kvh == 0)
                v2 = _dup_half(v, kvh == 0)
                dk2 = jnp.zeros((2 * ATT_BLOCK, LANES), F32)
                dv2 = jnp.zeros((2 * ATT_BLOCK, LANES), F32)
                for pair in range(4):
                    c0 = (kvh * 4 + pair) * LANES
                    q2 = q[:, c0:c0 + LANES]
                    do2 = do_ref[:, c0:c0 + LANES]
                    prod = do2.astype(F32) * o_ref[:, c0:c0 + LANES].astype(F32)
                    dqh = []
                    for hf in range(2):
                        h = 2 * (kvh * 4 + pair) + hf
                        sink = sink_ref[0, h]
                        sel = (lane < HEAD_DIM) == (hf == 0)
                        qm = jnp.where(sel, q2, jnp.zeros_like(q2))
                        dom = jnp.where(sel, do2, jnp.zeros_like(do2))
                        delta = jnp.sum(jnp.where(sel, prod, 0.0), axis=1, keepdims=True)
                        s = lax.dot_general(qm, k2, _NT, preferred_element_type=F32)
                        s = jnp.where(ok, s, NEG)
                        m = jnp.maximum(jnp.max(s, axis=1, keepdims=True), sink)
                        e = jnp.exp(s - m)
                        inv = 1.0 / (jnp.sum(e, axis=1, keepdims=True) + jnp.exp(sink - m))
                        p = e * inv
                        dsk = dsk + jnp.where(lane_s == h, -jnp.sum(jnp.exp(sink - m) * inv * delta), 0.0)
                        dp = lax.dot_general(dom, v2, _NT, preferred_element_type=F32)
                        ds = (p * (dp - delta)).astype(BF16)
                        dqh.append(jnp.dot(ds, k2, preferred_element_type=F32))
                        dk2 = dk2 + lax.dot_general(ds, qm, _TN, preferred_element_type=F32)
                        dv2 = dv2 + lax.dot_general(p.astype(BF16), dom, _TN, preferred_element_type=F32)
                    dqs[:, c0:c0 + LANES] = jnp.where(lane < HEAD_DIM, dqh[0], dqh[1]) * 0.125
                dk_h.append(dk2 + pltpu.roll(dk2, HEAD_DIM, 1))
                dv_h.append(dv2 + pltpu.roll(dv2, HEAD_DIM, 1))
            dk = jnp.where(lane_kv < HEAD_DIM, dk_h[0], dk_h[1])
            dv = jnp.where(lane_kv < HEAD_DIM, dv_h[0], dv_h[1])
            dq_ref[...] = _rope(dqs[...], ccq, scq, -1.0).astype(dq_ref.dtype)
            dkp = _rope(dk[:ATT_BLOCK], cp, sp, -1.0)
            dkc = _rope(dk[ATT_BLOCK:], cc, sc, -1.0)
            dkv_ref[...] = (carry[...] + jnp.concatenate([dkp, dv[:ATT_BLOCK]], axis=1)).astype(dkv_ref.dtype)
            carry[...] = jnp.concatenate([dkc, dv[ATT_BLOCK:]], axis=1)
            dsk_ref[...] += dsk

        @pl.when(i == nb)
        def _():
            dkv_ref[...] = carry[...].astype(dkv_ref.dtype)

    blk = lambda w, f: pl.BlockSpec((ATT_BLOCK, w), f)
    cur = lambda i: jnp.minimum(i, nb - 1)
    prv = lambda i: jnp.maximum(jnp.minimum(i, nb - 1) - 1, 0)
    return pl.pallas_call(
        body,
        name="attn_bwd",
        grid=(nb + 1,),
        in_specs=[
            pl.BlockSpec(memory_space=pltpu.SMEM),
            blk(Q_W, lambda i: (cur(i), 0)),
            blk(2 * KV_W, lambda i: (prv(i), kvb)),
            blk(2 * KV_W, lambda i: (cur(i), kvb)),
            blk(LANES, lambda i: (cur(i), 0)),
            blk(LANES, lambda i: (cur(i), 0)),
            blk(LANES, lambda i: (prv(i), 0)),
            blk(LANES, lambda i: (prv(i), 0)),
            blk(Q_W, lambda i: (cur(i), 0)),
            blk(Q_W, lambda i: (cur(i), 0)),
        ],
        out_specs=[
            blk(Q_W, lambda i: (cur(i), 0)),
            blk(2 * KV_W, lambda i: (jnp.maximum(i - 1, 0), 0)),
            pl.BlockSpec((1, LANES), lambda i: (0, 0)),
        ],
        out_shape=[SDS((T, Q_W), BF16), SDS((T, 2 * KV_W), BF16), SDS((1, LANES), F32)],
        scratch_shapes=[pltpu.VMEM((ATT_BLOCK, 2 * KV_W), F32), pltpu.VMEM((ATT_BLOCK, Q_W), F32)],
        compiler_params=_cp(("arbitrary",)),
    )(sinks, za, za, za, cos, sin, cos, sin, o, do)


def _s5_discretize(lam_re, lam_im, log_dt, b_re, b_im):
    dt = jnp.exp(log_dt)[:, None]
    mag = jnp.exp(lam_re * dt)
    a_re, a_im = mag * jnp.cos(lam_im * dt), mag * jnp.sin(lam_im * dt)
    den = lam_re * lam_re + lam_im * lam_im
    nr, ni = a_re - 1.0, a_im
    coef_re = (nr * lam_re + ni * lam_im) / den
    coef_im = (ni * lam_re - nr * lam_im) / den
    bb_re = coef_re[..., None] * b_re - coef_im[..., None] * b_im
    bb_im = coef_re[..., None] * b_im + coef_im[..., None] * b_re
    return a_re, a_im, bb_re, bb_im


def _blockdiag_in(bb):
    x = bb.reshape(N_JB, 8, SSM_P, SSM_GC).transpose(0, 1, 3, 2)
    return (x[:, :, :, None, :] * jnp.eye(8, dtype=bb.dtype)[None, :, None, :, None]).reshape(N_JB, 128, 512)


def _blockdiag_in_extract(m):
    x = m.reshape(N_JB, 8, SSM_GC, 8, SSM_P)
    x = jnp.einsum('jgchp,gh->jgcp', x, jnp.eye(8, dtype=m.dtype))
    return x.transpose(0, 1, 3, 2).reshape(SSM_G, SSM_P, SSM_GC)


def _blockdiag_out(c):
    x = c.reshape(N_JB, 8, SSM_GC, SSM_P).transpose(0, 1, 3, 2)
    return (x[:, :, :, None, :] * jnp.eye(8, dtype=c.dtype)[None, :, None, :, None]).reshape(N_JB, 512, 128)


def _blockdiag_out_extract(m):
    x = m.reshape(N_JB, 8, SSM_P, 8, SSM_GC)
    x = jnp.einsum('jgphc,gh->jgpc', x, jnp.eye(8, dtype=m.dtype))
    return x.transpose(0, 1, 3, 2).reshape(SSM_G, SSM_GC, SSM_P)


def _s5_tables(a_re, a_im):
    ar, ai = a_re.reshape(N_LG, 1, LANES), a_im.reshape(N_LG, 1, LANES)
    pr, pi = [ar], [ai]
    for _ in range(S5_SEG - 1):
        pr, pi = pr + [pr[-1] * ar - pi[-1] * ai], pi + [pr[-1] * ai + pi[-1] * ar]
    p_re, p_im = jnp.concatenate(pr, axis=1), jnp.concatenate(pi, axis=1)
    bc = lambda v: jnp.broadcast_to(v, (N_LG, SUBLANES, LANES))
    return p_re, p_im, bc(ar), bc(ai)


def _s5_scan(src_re, src_im, dst_re, dst_im, ar, ai, reverse, dst_row0=0):
    def step(n, carry):
        t = (S5_SEG - 1 - n) if reverse else n
        out = []
        for ll in range(LG_PER_JB):
            xr, xi = carry[2 * ll], carry[2 * ll + 1]
            idx = (ll, pl.ds(t, SUBLANES, stride=S5_SEG), slice(None))
            odx = (ll, pl.ds(t + dst_row0, SUBLANES, stride=S5_SEG), slice(None))
            nr = ar[ll] * xr - ai[ll] * xi + src_re[idx]
            ni = ar[ll] * xi + ai[ll] * xr + src_im[idx]
            dst_re[odx] = nr
            dst_im[odx] = ni
            out += [nr, ni]
        return tuple(out)
    z = jnp.zeros((SUBLANES, LANES), F32)
    return lax.fori_loop(0, S5_SEG, step, (z,) * (2 * LG_PER_JB))


def _s5_fixup(ends, in_re, in_im, mr, mi, s_re, s_im, reverse):
    cr, ci = in_re, in_im
    order = range(SUBLANES - 1, -1, -1) if reverse else range(SUBLANES)
    for s in order:
        s_re[:, s:s + 1, :] = cr
        s_im[:, s:s + 1, :] = ci
        er = jnp.stack([ends[2 * ll][s:s + 1, :] for ll in range(LG_PER_JB)])
        ei = jnp.stack([ends[2 * ll + 1][s:s + 1, :] for ll in range(LG_PER_JB)])
        cr, ci = mr * cr - mi * ci + er, mr * ci + mi * cr + ei
    return cr, ci


def _s5_correct(x_re, x_im, s_re, s_im, p_re, p_im, row0):
    for ll in range(LG_PER_JB):
        pr, pi = p_re[ll], p_im[ll]
        for s in range(SUBLANES):
            rows = slice(row0 + s * S5_SEG, row0 + (s + 1) * S5_SEG)
            sr, si = s_re[ll, s:s + 1, :], s_im[ll, s:s + 1, :]
            x_re[ll, rows, :] = x_re[ll, rows, :] + (pr * sr - pi * si)
            x_im[ll, rows, :] = x_im[ll, rows, :] + (pr * si + pi * sr)


def _s5_specs(nc, rev):
    cidx = (lambda c: nc - 1 - c) if rev else (lambda c: c)
    jb = lambda shape: pl.BlockSpec(shape, lambda j, c: (j, 0, 0))
    return cidx, [
        jb((1, LANES, 4 * LANES)), jb((1, LANES, 4 * LANES)),
        jb((1, 4 * LANES, LANES)), jb((1, 4 * LANES, LANES)),
        pl.BlockSpec((1, LANES), lambda j, c: (0, j)),
        jb((LG_PER_JB, SUBLANES, LANES)), jb((LG_PER_JB, SUBLANES, LANES)),
        jb((LG_PER_JB, 1, LANES)), jb((LG_PER_JB, 1, LANES)),
        jb((LG_PER_JB, S5_SEG, LANES)), jb((LG_PER_JB, S5_SEG, LANES)),
    ]


def _s5_fwd(za, prm):
    T = za.shape[0]
    R = S5_CHUNK
    nc = T // R
    ub = (Q_W + 2 * KV_W) // LANES
    _, pspecs = _s5_specs(nc, False)

    def body(u_ref, bre_ref, bim_ref, cre_ref, cim_ref, d_ref, are_ref, aim_ref, alr_ref, ali_ref, pr_ref, pi_ref,
             yg_ref, x0r_ref, x0i_ref, bur, bui, xsr, xsi, sr, si, xcr, xci):
        c = pl.program_id(1)

        @pl.when(c == 0)
        def _():
            xcr[...] = jnp.zeros(xcr.shape, F32)
            xci[...] = jnp.zeros(xci.shape, F32)

        u = u_ref[...]
        ub16 = u.astype(BF16)
        b_r = jnp.dot(ub16, bre_ref[0].astype(BF16), preferred_element_type=F32)
        b_i = jnp.dot(ub16, bim_ref[0].astype(BF16), preferred_element_type=F32)
        for ll in range(LG_PER_JB):
            bur[ll] = b_r[:, ll * LANES:(ll + 1) * LANES]
            bui[ll] = b_i[:, ll * LANES:(ll + 1) * LANES]
        ar = [are_ref[ll] for ll in range(LG_PER_JB)]
        ai = [aim_ref[ll] for ll in range(LG_PER_JB)]
        ends = _s5_scan(bur, bui, xsr, xsi, ar, ai, False)
        in_r, in_i = xcr[...], xci[...]
        x0r_ref[0] = in_r
        x0i_ref[0] = in_i
        out_r, out_i = _s5_fixup(ends, in_r, in_i, alr_ref[...], ali_ref[...], sr, si, False)
        xcr[...] = out_r
        xci[...] = out_i
        _s5_correct(xsr, xsi, sr, si, pr_ref, pi_ref, 0)
        y = d_ref[...] * u
        for ll in range(LG_PER_JB):
            rows = slice(ll * LANES, (ll + 1) * LANES)
            y = y + jnp.dot(xsr[ll].astype(BF16), cre_ref[0, rows, :].astype(BF16), preferred_element_type=F32)
            y = y - jnp.dot(xsi[ll].astype(BF16), cim_ref[0, rows, :].astype(BF16), preferred_element_type=F32)
        yg_ref[...] = _gelu(y).astype(BF16)

    st = pl.BlockSpec((1, LG_PER_JB, 1, LANES), lambda j, c: (c, j, 0, 0))
    vm = lambda rows: pltpu.VMEM((LG_PER_JB, rows, LANES), F32)
    return pl.pallas_call(
        body,
        name="s5_fwd",
        grid=(N_JB, nc),
        in_specs=[pl.BlockSpec((R, LANES), lambda j, c: (c, ub + j))] + pspecs,
        out_specs=[pl.BlockSpec((R, LANES), lambda j, c: (c, j)), st, st],
        out_shape=[SDS((T, SSM_W), BF16), SDS((nc, N_LG, 1, LANES), F32), SDS((nc, N_LG, 1, LANES), F32)],
        scratch_shapes=[vm(R), vm(R), vm(R), vm(R), vm(SUBLANES), vm(SUBLANES), vm(1), vm(1)],
        compiler_params=_cp(("parallel", "arbitrary")),
    )(za, *prm)


def _s5_bwd(za, dyg, x0r, x0i, prm, prev_tables):
    T = za.shape[0]
    R = S5_CHUNK
    nc = T // R
    ub = (Q_W + 2 * KV_W) // LANES
    cidx, pspecs = _s5_specs(nc, True)
    PAD = SUBLANES

    def body(u_ref, dyg_ref, x0r_ref, x0i_ref, bre_ref, bim_ref, cre_ref, cim_ref, d_ref, are_ref, aim_ref,
             alr_ref, ali_ref, pr_ref, pi_ref, qr_ref, qi_ref,
             du_ref, dar_ref, dai_ref, dbr_ref, dbi_ref, dcr_ref, dci_ref, dd_ref,
             bur, bui, xsr, xsi, sr, si, gcr, gci):
        c = pl.program_id(1)

        @pl.when(c == 0)
        def _():
            gcr[...] = jnp.zeros(gcr.shape, F32)
            gci[...] = jnp.zeros(gci.shape, F32)
            dar_ref[...] = jnp.zeros(dar_ref.shape, F32)
            dai_ref[...] = jnp.zeros(dai_ref.shape, F32)
            dbr_ref[...] = jnp.zeros(dbr_ref.shape, F32)
            dbi_ref[...] = jnp.zeros(dbi_ref.shape, F32)
            dcr_ref[...] = jnp.zeros(dcr_ref.shape, F32)
            dci_ref[...] = jnp.zeros(dci_ref.shape, F32)
            dd_ref[...] = jnp.zeros(dd_ref.shape, F32)

        u = u_ref[...]
        ub16 = u.astype(BF16)
        bre, bim = bre_ref[0].astype(BF16), bim_ref[0].astype(BF16)
        cre, cim = cre_ref[0].astype(BF16), cim_ref[0].astype(BF16)
        b_r = jnp.dot(ub16, bre, preferred_element_type=F32)
        b_i = jnp.dot(ub16, bim, preferred_element_type=F32)
        for ll in range(LG_PER_JB):
            bur[ll] = b_r[:, ll * LANES:(ll + 1) * LANES]
            bui[ll] = b_i[:, ll * LANES:(ll + 1) * LANES]
        ar = [are_ref[ll] for ll in range(LG_PER_JB)]
        ai = [aim_ref[ll] for ll in range(LG_PER_JB)]
        ends = _s5_scan(bur, bui, xsr, xsi, ar, ai, False, dst_row0=PAD)
        in_r, in_i = x0r_ref[0], x0i_ref[0]
        _s5_fixup(ends, in_r, in_i, alr_ref[...], ali_ref[...], sr, si, False)
        _s5_correct(xsr, xsi, sr, si, pr_ref, pi_ref, PAD)
        xsr[:, PAD - 1:PAD, :] = in_r
        xsi[:, PAD - 1:PAD, :] = in_i
        y = d_ref[...] * u
        for ll in range(LG_PER_JB):
            rows = slice(ll * LANES, (ll + 1) * LANES)
            y = y + jnp.dot(xsr[ll, PAD:, :].astype(BF16), cre[rows, :], preferred_element_type=F32)
            y = y - jnp.dot(xsi[ll, PAD:, :].astype(BF16), cim[rows, :], preferred_element_type=F32)
        dy = dyg_ref[...] * _gelu_grad(y)
        dyb = dy.astype(BF16)
        dd_ref[...] += jnp.sum(dy * u, axis=0, keepdims=True)
        du = d_ref[...] * dy
        g_r = lax.dot_general(dyb, cre, _NT, preferred_element_type=F32)
        g_i = -lax.dot_general(dyb, cim, _NT, preferred_element_type=F32)
        for ll in range(LG_PER_JB):
            rows = slice(ll * LANES, (ll + 1) * LANES)
            dcr_ref[0, rows, :] += lax.dot_general(xsr[ll, PAD:, :].astype(BF16), dyb, _TN, preferred_element_type=F32)
            dci_ref[0, rows, :] -= lax.dot_general(xsi[ll, PAD:, :].astype(BF16), dyb, _TN, preferred_element_type=F32)
            bur[ll] = g_r[:, ll * LANES:(ll + 1) * LANES]
            bui[ll] = g_i[:, ll * LANES:(ll + 1) * LANES]
        ends = _s5_scan(bur, bui, bur, bui, ar, [-v for v in ai], True)
        out_r, out_i = _s5_fixup(ends, gcr[...], gci[...], alr_ref[...], -ali_ref[...], sr, si, True)
        gcr[...] = out_r
        gci[...] = out_i
        _s5_correct(bur, bui, sr, si, qr_ref, qi_ref, 0)
        for ll in range(LG_PER_JB):
            cols = slice(ll * LANES, (ll + 1) * LANES)
            gr, gi = bur[ll], bui[ll]
            xpr, xpi = xsr[ll, PAD - 1:PAD - 1 + R, :], xsi[ll, PAD - 1:PAD - 1 + R, :]
            red = lambda v: v.reshape(R // SUBLANES, SUBLANES, LANES).sum(axis=0)
            dar_ref[ll] += red(xpr * gr + xpi * gi)
            dai_ref[ll] += red(xpr * gi - xpi * gr)
            grb, gib = gr.astype(BF16), gi.astype(BF16)
            dbr_ref[0, :, cols] += lax.dot_general(ub16, grb, _TN, preferred_element_type=F32)
            dbi_ref[0, :, cols] += lax.dot_general(ub16, gib, _TN, preferred_element_type=F32)
            du = du + lax.dot_general(grb, bre[:, cols], _NT, preferred_element_type=F32)
            du = du + lax.dot_general(gib, bim[:, cols], _NT, preferred_element_type=F32)
        du_ref[...] = du.astype(du_ref.dtype)

    st = pl.BlockSpec((1, LG_PER_JB, 1, LANES), lambda j, c: (cidx(c), j, 0, 0))
    jb = lambda shape: pl.BlockSpec(shape, lambda j, c: (j, 0, 0))
    vm = lambda rows: pltpu.VMEM((LG_PER_JB, rows, LANES), F32)
    return pl.pallas_call(
        body,
        name="s5_bwd",
        grid=(N_JB, nc),
        in_specs=[pl.BlockSpec((R, LANES), lambda j, c: (cidx(c), ub + j)),
                  pl.BlockSpec((R, LANES), lambda j, c: (cidx(c), j)), st, st] + pspecs
                 + [jb((LG_PER_JB, S5_SEG, LANES)), jb((LG_PER_JB, S5_SEG, LANES))],
        out_specs=[pl.BlockSpec((R, LANES), lambda j, c: (cidx(c), j)),
                   jb((LG_PER_JB, SUBLANES, LANES)), jb((LG_PER_JB, SUBLANES, LANES)),
                   jb((1, LANES, 4 * LANES)), jb((1, LANES, 4 * LANES)),
                   jb((1, 4 * LANES, LANES)), jb((1, 4 * LANES, LANES)),
                   pl.BlockSpec((1, LANES), lambda j, c: (0, j))],
        out_shape=[SDS((T, SSM_W), BF16), SDS((N_LG, SUBLANES, LANES), F32), SDS((N_LG, SUBLANES, LANES), F32),
                   SDS((N_JB, LANES, 4 * LANES), F32), SDS((N_JB, LANES, 4 * LANES), F32),
                   SDS((N_JB, 4 * LANES, LANES), F32), SDS((N_JB, 4 * LANES, LANES), F32), SDS((1, SSM_W), F32)],
        scratch_shapes=[vm(R), vm(R), vm(R + PAD), vm(R + PAD), vm(SUBLANES), vm(SUBLANES), vm(1), vm(1)],
        compiler_params=_cp(("parallel", "arbitrary")),
    )(za, dyg, x0r, x0i, *prm, *prev_tables)


def _local_step(x, target, gains, w_a, w_g, sinks, s5w, w_glu, w_ba, w_bs, w_out, w_up, w_down):
    T = x.shape[0]
    D = D_MODEL
    g1, g2, g3, g4 = gains
    cos, sin = _rope_tables(T)
    lam_re, lam_im, log_dt, b_re, b_im, c_re, c_im, d_skip = s5w
    (a_re, a_im, bb_re, bb_im), disc_vjp = jax.vjp(_s5_discretize, lam_re, lam_im, log_dt, b_re, b_im)
    p_re, p_im, abr, abi = _s5_tables(a_re, a_im)
    prm = (_blockdiag_in(bb_re), _blockdiag_in(bb_im), _blockdiag_out(c_re), _blockdiag_out(c_im),
           d_skip.reshape(1, SSM_W), abr, abi, p_re[:, S5_SEG - 1:, :], p_im[:, S5_SEG - 1:, :], p_re, p_im)
    rev_tables = (p_re[:, ::-1, :], -p_im[:, ::-1, :])
    mm = functools.partial(_mm, tm=512, tn=1024, tk=2048)

    h = _rowwise(lambda xv, g: ((_rms(xv)[0] * g,), ()), [(x, D, 0)], [g1], [(D, BF16)], [], tr=512, name="norm1")[0]
    za = _mm(h, w_a, mode="nn", out_dtype=F32, tm=512, tn=1152, tk=2048, name="mm_za")
    zg = mm(h, w_g, mode="nn", out_dtype=F32, name="mm_zg")
    o_attn = _attn_fwd(za, cos, sin, sinks)
    yg, x0r, x0i = _s5_fwd(za, prm)
    zglu = mm(yg, w_glu, mode="nn", out_dtype=F32, name="mm_glu")
    o_ssm = _rowwise(lambda z1, z2: ((z1 * _sig(z2),), ()), [(zglu, SSM_W, 0), (zglu, SSM_W, 1)], [],
                     [(SSM_W, BF16)], [], tr=512, name="glu")[0]
    ya = mm(o_attn, w_ba, mode="nn", out_dtype=F32, name="mm_ya")
    ys = mm(o_ssm, w_bs, mode="nn", out_dtype=F32, name="mm_ys")
    mi = _rowwise(lambda ga, gs, a, s: ((_sig(ga) * a + _sig(gs) * s,), ()),
                  [(zg, D, 0), (zg, D, 1), (ya, D, 0), (ys, D, 0)], [], [(D, BF16)], [], tr=256, name="gate")[0]
    mixed = mm(mi, w_out, mode="nn", out_dtype=F32, name="mm_out")

    def f_post(xv, mv, g2v, g3v):
        x1v = xv + _rms(mv)[0] * g2v
        return (x1v, _rms(x1v)[0] * g3v), ()
    x1, h2 = _rowwise(f_post, [(x, D, 0), (mixed, D, 0)], [g2, g3], [(D, F32), (D, BF16)], [], tr=256, name="post_mix")
    act = mm(h2, w_up, mode="nn", out_dtype=BF16, name="mm_up", epi=lambda v: jnp.maximum(v, 0.0))
    f = _mm(act, w_down, mode="nn", out_dtype=F32, tm=512, tn=2048, tk=2048, name="mm_down", a_fn=lambda v: v * v)

    def f_final(x1v, fv, tv, g4v):
        fn, r = _rms(fv)
        e = x1v + fn * g4v - tv
        dx2v = e * (1.0 / D)
        dfv, dg4v = _rms_bwd(dx2v, fn, r, g4v)
        return (dfv, dx2v), (dg4v, jnp.zeros((SUBLANES, LANES), F32) + 0.5 * jnp.sum(e * e) * (1.0 / D))
    df, dx2, dg4, lossb = _rowwise(f_final, [(x1, D, 0), (f, D, 0), (target, D, 0)], [g4],
                                   [(D, BF16), (D, F32)], [(1, D), (SUBLANES, LANES)], tr=256, name="final")

    dpre = mm(df, w_down, mode="nt", out_dtype=BF16, name="mm_dact", epi=lambda v, a: v * (2.0 * a.astype(F32)), extras=(act,))
    wg = functools.partial(_mm, mode="tn", out_dtype=F32, tm=1024, tn=2048, tk=512)
    d_w_down = wg(act, df, name="wg_down", a_fn=lambda v: v * v)
    dh2 = _mm(dpre, w_up, mode="nt", out_dtype=F32, tm=512, tn=2048, tk=2048, name="mm_dh2")
    d_w_up = wg(h2, dpre, name="wg_up")

    def f_mid(dx2v, dh2v, x1v, mv, g2v, g3v):
        x1n, r3 = _rms(x1v)
        d3, dg3v = _rms_bwd(dh2v, x1n, r3, g3v)
        dx1v = dx2v + d3
        mn, r2 = _rms(mv)
        dmv, dg2v = _rms_bwd(dx1v, mn, r2, g2v)
        return (dx1v, dmv), (dg3v, dg2v)
    dx1, dmixed, dg3, dg2 = _rowwise(f_mid, [(dx2, D, 0), (dh2, D, 0), (x1, D, 0), (mixed, D, 0)], [g2, g3],
                                     [(D, F32), (D, BF16)], [(1, D), (1, D)], tr=256, name="mid")

    dmi = mm(dmixed, w_out, mode="nt", out_dtype=F32, name="mm_dmi")
    d_w_out = wg(mi, dmixed, name="wg_out")

    def f_gate(dv, ga, gs, a, s):
        sa, ss = _sig(ga), _sig(gs)
        return (dv * sa, dv * ss, jnp.concatenate([dv * a * sa * (1.0 - sa), dv * s * ss * (1.0 - ss)], axis=1)), ()
    dya, dys, dzg = _rowwise(f_gate, [(dmi, D, 0), (zg, D, 0), (zg, D, 1), (ya, D, 0), (ys, D, 0)], [],
                             [(D, BF16), (D, BF16), (2 * D, BF16)], [], tr=256, name="gate_bwd")
    do_attn = mm(dya, w_ba, mode="nt", out_dtype=BF16, name="mm_doa")
    d_w_ba = wg(o_attn, dya, name="wg_ba")
    do_ssm = mm(dys, w_bs, mode="nt", out_dtype=F32, name="mm_dos")
    d_w_bs = wg(o_ssm, dys, name="wg_bs")

    def f_glu(dv, z1, z2):
        s2 = _sig(z2)
        return (jnp.concatenate([dv * s2, dv * z1 * s2 * (1.0 - s2)], axis=1),), ()
    dzglu = _rowwise(f_glu, [(do_ssm, SSM_W, 0), (zglu, SSM_W, 0), (zglu, SSM_W, 1)], [], [(2 * SSM_W, BF16)], [],
                     tr=512, name="glu_bwd")[0]
    dyg = mm(dzglu, w_glu, mode="nt", out_dtype=F32, name="mm_dyg")
    d_w_glu = wg(yg, dzglu, name="wg_glu")
    du, dar, dai, dbr, dbi, dcr, dci, ddv = _s5_bwd(za, dyg, x0r, x0i, prm, rev_tables)
    dq, dkv, dsk = _attn_bwd(za, cos, sin, sinks, o_attn, do_attn)
    dza = jnp.concatenate([dq, dkv, du], axis=1)
    dh = mm(dza, w_a, mode="nt", out_dtype=F32, name="mm_dh_a", tk=ZA_W)
    dh = _mm(dzg, w_g, mode="nt", out_dtype=F32, tm=512, tn=1024, tk=2048, name="mm_dh_g", epi=lambda v, p: v + p, extras=(dh,))
    d_w_a = _mm(h, dza, mode="tn", out_dtype=F32, tm=1024, tn=ZA_W, tk=512, name="wg_a")
    d_w_g = wg(h, dzg, name="wg_g")

    def f_first(dx1v, dhv, xv, g1v):
        xn, r1 = _rms(xv)
        d1, dg1v = _rms_bwd(dhv, xn, r1, g1v)
        return (dx1v + d1,), (dg1v,)
    dx, dg1 = _rowwise(f_first, [(dx1, D, 0), (dh, D, 0), (x, D, 0)], [g1], [(D, F32)], [(1, D)], tr=256, name="first")

    da_re = dar.sum(axis=1).reshape(SSM_G, SSM_P)
    da_im = dai.sum(axis=1).reshape(SSM_G, SSM_P)
    d_lam_re, d_lam_im, d_log_dt, d_b_re, d_b_im = disc_vjp(
        (da_re, da_im, _blockdiag_in_extract(dbr), _blockdiag_in_extract(dbi)))
    small = dict(norm_mix_pre=dg1, norm_mix_post=dg2, norm_mlp_pre=dg3, norm_mlp_post=dg4,
                 sinks=dsk[:, :N_Q_HEADS], lam_re=d_lam_re, lam_im=d_lam_im, log_dt=d_log_dt,
                 b_re=d_b_re, b_im=d_b_im, c_re=_blockdiag_out_extract(dcr), c_im=_blockdiag_out_extract(dci),
                 d_skip=ddv.reshape(SSM_G, SSM_GC))
    big = dict(w_in=jnp.concatenate([d_w_a, d_w_g], axis=1), w_glu=d_w_glu,
               w_branch=jnp.concatenate([d_w_ba, d_w_bs], axis=0), w_out=d_w_out, w_up=d_w_up, w_down=d_w_down)
    return lossb[0, 0], dx, small, big


ANY = pl.BlockSpec(memory_space=pl.ANY)


def _place():
    x, y, c = lax.axis_index("x"), lax.axis_index("y"), lax.axis_index("c")
    others = [(1 - x, y), (x, 1 - y), (1 - x, 1 - y)]
    return x, y, c, others


def _remote(src, dst, ssem, rsem, to):
    return pltpu.make_async_remote_copy(src_ref=src, dst_ref=dst, send_sem=ssem, recv_sem=rsem,
                                        device_id=to, device_id_type=MESH)


def _gather_weights(shards):
    n = len(shards)

    def body(*refs):
        ins, outs = refs[:n], refs[n:2 * n]
        lsem, ssem, rsem, fs_sem, fr_sem = refs[2 * n:]
        x, y, c, others = _place()
        me, sib = 2 * x + y, (x, y, 1 - c)
        local = [pltpu.make_async_copy(ins[w], outs[w].at[me], lsem.at[w]) for w in range(n)]
        for cp in local:
            cp.start()

        def half(w, hc):
            hr = ins[w].shape[0] // 2
            return pl.ds(pl.multiple_of(hc * hr, 16), hr)

        sends, passes = [], []
        for w in range(n):
            for r, (ox, oy) in enumerate(others):
                cp = _remote(ins[w].at[half(w, c), :], outs[w].at[me, half(w, c), :],
                             ssem.at[3 * w + r], rsem.at[3 * w + r], (ox, oy, c))
                cp.start()
                sends.append(cp)
        for w in range(n):
            for r, (ox, oy) in enumerate(others):
                got = outs[w].at[2 * ox + oy, half(w, c), :]
                _remote(got, got, ssem.at[3 * w + r], rsem.at[3 * w + r], (ox, oy, c)).wait_recv()
                cp = _remote(got, got, fs_sem.at[3 * w + r], fr_sem.at[3 * w + r], sib)
                cp.start()
                passes.append(cp)
        for w in range(n):
            for r, (ox, oy) in enumerate(others):
                got = outs[w].at[2 * ox + oy, half(w, 1 - c), :]
                _remote(got, got, fs_sem.at[3 * w + r], fr_sem.at[3 * w + r], sib).wait_recv()
        for cp in sends + passes:
            cp.wait_send()
        for cp in local:
            cp.wait()

    dma = pltpu.SemaphoreType.DMA
    return pl.pallas_call(
        body,
        name="gather_weights",
        in_specs=[ANY] * n,
        out_specs=[ANY] * n,
        out_shape=[SDS((4,) + s.shape, s.dtype) for s in shards],
        scratch_shapes=[dma((n,)), dma((3 * n,)), dma((3 * n,)), dma((3 * n,)), dma((3 * n,))],
    )(*shards)


def _pair_exchange(grads):
    n = len(grads)

    def body(*refs):
        ins, outs = refs[:n], refs[n:2 * n]
        ssem, rsem = refs[2 * n:]
        x, y, c, _ = _place()
        cps = []
        for w in range(n):
            hr = ins[w].shape[1] // 2
            src = ins[w].at[:, pl.ds(pl.multiple_of((1 - c) * hr, 8), hr), :]
            cp = _remote(src, outs[w], ssem.at[w], rsem.at[w], (x, y, 1 - c))
            cp.start()
            cps.append(cp)
        for cp in cps:
            cp.wait()

    dma = pltpu.SemaphoreType.DMA
    return pl.pallas_call(
        body,
        name="pair_exchange",
        in_specs=[ANY] * n,
        out_specs=[ANY] * n,
        out_shape=[SDS((4, g.shape[1] // 2, g.shape[2]), g.dtype) for g in grads],
        scratch_shapes=[dma((n,)), dma((n,))],
    )(*grads)


def _pair_sum(g, r, c_arr):
    _, _, hr, cols = g.shape
    tr = min(256, hr)

    def body(c_ref, g_ref, r_ref, o_ref):
        o_ref[0] = (g_ref[0, 0] + r_ref[0]).astype(BF16)

    return pl.pallas_call(
        body,
        name="pair_sum",
        grid_spec=pltpu.PrefetchScalarGridSpec(
            num_scalar_prefetch=1,
            grid=(4, hr // tr),
            in_specs=[pl.BlockSpec((1, 1, tr, cols), lambda k, i, c_ref: (k, c_ref[0], i, 0)),
                      pl.BlockSpec((1, tr, cols), lambda k, i, c_ref: (k, i, 0))],
            out_specs=pl.BlockSpec((1, tr, cols), lambda k, i, c_ref: (k, i, 0)),
        ),
        out_shape=SDS((4, hr, cols), BF16),
        compiler_params=_cp(("parallel", "parallel")),
    )(c_arr, g, r)


def _chip_exchange(psums):
    n = len(psums)

    def body(*refs):
        ins, outs = refs[:n], refs[n:2 * n]
        ssem, rsem = refs[2 * n:]
        x, y, c, others = _place()
        cps = []
        for w in range(n):
            for r, (ox, oy) in enumerate(others):
                cp = _remote(ins[w].at[2 * ox + oy], outs[w].at[r], ssem.at[3 * w + r], rsem.at[3 * w + r], (ox, oy, c))
                cp.start()
                cps.append(cp)
        for cp in cps:
            cp.wait()

    dma = pltpu.SemaphoreType.DMA
    return pl.pallas_call(
        body,
        name="chip_exchange",
        in_specs=[ANY] * n,
        out_specs=[ANY] * n,
        out_shape=[SDS((3,) + p.shape[1:], p.dtype) for p in psums],
        scratch_shapes=[dma((3 * n,)), dma((3 * n,))],
    )(*psums)


def _chip_sum(g, r, q, kc_arr):
    _, _, hr, cols = g.shape
    tr = min(256, hr)

    def body(kc_ref, g_ref, r_ref, q_ref, o_ref):
        s = g_ref[0, 0] + r_ref[0]
        for j in range(3):
            s = s + q_ref[j].astype(F32)
        o_ref[...] = s

    return pl.pallas_call(
        body,
        name="chip_sum",
        grid_spec=pltpu.PrefetchScalarGridSpec(
            num_scalar_prefetch=1,
            grid=(hr // tr,),
            in_specs=[pl.BlockSpec((1, 1, tr, cols), lambda i, kc: (kc[0], kc[1], i, 0)),
                      pl.BlockSpec((1, tr, cols), lambda i, kc: (kc[0], i, 0)),
                      pl.BlockSpec((3, tr, cols), lambda i, kc: (0, i, 0))],
            out_specs=pl.BlockSpec((tr, cols), lambda i, kc: (i, 0)),
        ),
        out_shape=SDS((hr, cols), F32),
        compiler_params=_cp(("parallel",)),
    )(kc_arr, g, r, q)


def _pair_share(halves):
    n = len(halves)

    def body(*refs):
        ins, outs = refs[:n], refs[n:2 * n]
        lsem, ssem, rsem = refs[2 * n:]
        x, y, c, _ = _place()
        cps, loc = [], []
        for w in range(n):
            hr = ins[w].shape[0]
            rows = pl.ds(pl.multiple_of(c * hr, 8), hr)
            lc = pltpu.make_async_copy(ins[w], outs[w].at[rows, :], lsem.at[w])
            lc.start()
            loc.append(lc)
            cp = _remote(ins[w], outs[w].at[rows, :], ssem.at[w], rsem.at[w], (x, y, 1 - c))
            cp.start()
            cps.append(cp)
        for w in range(n):
            hr = ins[w].shape[0]
            other = outs[w].at[pl.ds(pl.multiple_of((1 - c) * hr, 8), hr), :]
            _remote(other, other, ssem.at[w], rsem.at[w], (x, y, 1 - c)).wait_recv()
        for cp in cps:
            cp.wait_send()
        for lc in loc:
            lc.wait()

    dma = pltpu.SemaphoreType.DMA
    return pl.pallas_call(
        body,
        name="pair_share",
        in_specs=[ANY] * n,
        out_specs=[ANY] * n,
        out_shape=[SDS((2 * h.shape[0], h.shape[1]), h.dtype) for h in halves],
        scratch_shapes=[dma((n,)), dma((n,)), dma((n,))],
    )(*halves)


def _all_reduce_small(buf):
    rows = buf.shape[0]

    def body(in_ref, o_ref, slots, ssem, rsem):
        x, y, c, _ = _place()
        me = 4 * x + 2 * y + c
        slots[me] = in_ref[...]
        cps = []
        for r in range(1, 8):
            px = 1 - x if r & 4 else x
            py = 1 - y if r & 2 else y
            pc = 1 - c if r & 1 else c
            cp = _remote(in_ref, slots.at[me], ssem.at[r - 1], rsem.at[r - 1], (px, py, pc))
            cp.start()
            cps.append((cp, 4 * px + 2 * py + pc))
        for r, (cp, peer) in enumerate(cps):
            _remote(in_ref, slots.at[peer], ssem.at[r], rsem.at[r], (x, y, c)).wait_recv()
        s = slots[0]
        for d in range(1, 8):
            s = s + slots[d]
        o_ref[...] = s
        for cp, _ in cps:
            cp.wait_send()

    dma = pltpu.SemaphoreType.DMA
    return pl.pallas_call(
        body,
        name="all_reduce_small",
        in_specs=[pl.BlockSpec(memory_space=pltpu.VMEM)],
        out_specs=pl.BlockSpec(memory_space=pltpu.VMEM),
        out_shape=SDS(buf.shape, F32),
        scratch_shapes=[pltpu.VMEM((8, rows, LANES), F32), dma((7,)), dma((7,))],
        compiler_params=pltpu.CompilerParams(vmem_limit_bytes=VMEM_LIMIT),
    )(buf)


def _adam_fn(w, g, m, v):
    m2 = ADAM_B1 * m + (1.0 - ADAM_B1) * g
    v2 = ADAM_B2 * v + (1.0 - ADAM_B2) * (g * g)
    m_hat = m2 / (1.0 - ADAM_B1 ** ADAM_STEP)
    v_hat = v2 / (1.0 - ADAM_B2 ** ADAM_STEP)
    return (-ADAM_LR * (m_hat / (jnp.sqrt(v_hat) + ADAM_EPS) + ADAM_WD * w), m2, v2), ()


def _adamw(w, g, m, v, name, tr=256):
    cols = w.shape[1]
    return _rowwise(_adam_fn, [(w, cols, 0), (g, cols, 0), (m, cols, 0), (v, cols, 0)], [],
                    [(cols, F32)] * 3, [], tr=tr, name=name)


BIG = ("w_in", "w_glu", "w_branch", "w_out", "w_up", "w_down")
COL_SHARDED = ("w_in", "w_glu", "w_up")
SMALL = ("norm_mix_pre", "norm_mix_post", "norm_mlp_pre", "norm_mlp_post", "sinks", "lam_re", "lam_im", "log_dt",
         "b_re", "b_im", "c_re", "c_im", "d_skip")
WEIGHTS = ("norm_mix_pre", "norm_mix_post", "norm_mlp_pre", "norm_mlp_post", "w_in", "sinks", "lam_re", "lam_im",
           "log_dt", "b_re", "b_im", "c_re", "c_im", "d_skip", "w_glu", "w_branch", "w_out", "w_up", "w_down")


def _flat_small(vals, extra):
    flat = jnp.concatenate([vals[k].reshape(-1) for k in SMALL] + [extra.reshape(-1)])
    rows = -(-flat.shape[0] // (SUBLANES * LANES)) * SUBLANES
    return jnp.pad(flat, (0, rows * LANES - flat.shape[0])).reshape(rows, LANES)


def kernel(x, norm_mix_pre, norm_mix_post, norm_mlp_pre, norm_mlp_post, w_in, sinks, lam_re, lam_im, log_dt, b_re, b_im, c_re, c_im, d_skip, w_glu, w_branch, w_out, w_up, w_down, loss_target, m_norm_mix_pre, m_norm_mix_post, m_norm_mlp_pre, m_norm_mlp_post, m_w_in, m_sinks, m_lam_re, m_lam_im, m_log_dt, m_b_re, m_b_im, m_c_re, m_c_im, m_d_skip, m_w_glu, m_w_branch, m_w_out, m_w_up, m_w_down, v_norm_mix_pre, v_norm_mix_post, v_norm_mlp_pre, v_norm_mlp_post, v_w_in, v_sinks, v_lam_re, v_lam_im, v_log_dt, v_b_re, v_b_im, v_c_re, v_c_im, v_d_skip, v_w_glu, v_w_branch, v_w_out, v_w_up, v_w_down):
    w = dict(norm_mix_pre=norm_mix_pre, norm_mix_post=norm_mix_post, norm_mlp_pre=norm_mlp_pre, norm_mlp_post=norm_mlp_post,
             w_in=w_in, sinks=sinks, lam_re=lam_re, lam_im=lam_im, log_dt=log_dt, b_re=b_re, b_im=b_im, c_re=c_re,
             c_im=c_im, d_skip=d_skip, w_glu=w_glu, w_branch=w_branch, w_out=w_out, w_up=w_up, w_down=w_down)
    m = dict(norm_mix_pre=m_norm_mix_pre, norm_mix_post=m_norm_mix_post, norm_mlp_pre=m_norm_mlp_pre,
             norm_mlp_post=m_norm_mlp_post, w_in=m_w_in, sinks=m_sinks, lam_re=m_lam_re, lam_im=m_lam_im,
             log_dt=m_log_dt, b_re=m_b_re, b_im=m_b_im, c_re=m_c_re, c_im=m_c_im, d_skip=m_d_skip, w_glu=m_w_glu,
             w_branch=m_w_branch, w_out=m_w_out, w_up=m_w_up, w_down=m_w_down)
    v = dict(norm_mix_pre=v_norm_mix_pre, norm_mix_post=v_norm_mix_post, norm_mlp_pre=v_norm_mlp_pre,
             norm_mlp_post=v_norm_mlp_post, w_in=v_w_in, sinks=v_sinks, lam_re=v_lam_re, lam_im=v_lam_im,
             log_dt=v_log_dt, b_re=v_b_re, b_im=v_b_im, c_re=v_c_re, c_im=v_c_im, d_skip=v_d_skip, w_glu=v_w_glu,
             w_branch=v_w_branch, w_out=v_w_out, w_up=v_w_up, w_down=v_w_down)
    xi, yi, ci = lax.axis_index("x"), lax.axis_index("y"), lax.axis_index("c")

    shards = [_rowwise(lambda t: ((t,), ()), [(w[k][0], w[k].shape[2], 0)], [], [(w[k].shape[2], BF16)], [],
                       tr=256, name="cast_" + k)[0] for k in BIG]
    full = {}
    for k, g4 in zip(BIG, _gather_weights(shards)):
        if k in COL_SHARDED:
            full[k] = jnp.concatenate([g4[j] for j in range(4)], axis=1)
        else:
            full[k] = g4.reshape(4 * g4.shape[1], g4.shape[2])

    s5w = (lam_re[0], lam_im[0], log_dt[0], b_re[0], b_im[0], c_re[0], c_im[0], d_skip[0])
    loss_part, dx, small, big = _local_step(
        x[0], loss_target[0], (norm_mix_pre, norm_mix_post, norm_mlp_pre, norm_mlp_post),
        full["w_in"][:, :ZA_W], full["w_in"][:, ZA_W:], sinks, s5w, full["w_glu"],
        full["w_branch"][:Q_W], full["w_branch"][Q_W:], full["w_out"], full["w_up"], full["w_down"])

    g4s = []
    for k in BIG:
        g = big[k]
        if k in COL_SHARDED:
            g = g.reshape(g.shape[0], 4, g.shape[1] // 4).transpose(1, 0, 2)
        else:
            g = g.reshape(4, g.shape[0] // 4, g.shape[1])
        g4s.append(g)
    from_sib = _pair_exchange(g4s)
    c_arr = jnp.stack([ci]).astype(jnp.int32)
    kc_arr = jnp.stack([2 * xi + yi, ci]).astype(jnp.int32)
    g42 = [g.reshape(4, 2, g.shape[1] // 2, g.shape[2]) for g in g4s]
    psums = [_pair_sum(g, r, c_arr) for g, r in zip(g42, from_sib)]
    from_chips = _chip_exchange(psums)
    halves = [_chip_sum(g, r, q, kc_arr) for g, r, q in zip(g42, from_sib, from_chips)]
    grads = dict(zip(BIG, _pair_share(halves)))

    red = _all_reduce_small(_flat_small(small, loss_part)).reshape(-1)
    off = 0
    for k in SMALL:
        n = math.prod(w[k].shape)
        grads[k] = red[off:off + n].reshape(w[k].shape[1:])
        off += n
    loss = red[off]

    delta, new_m, new_v = {}, {}, {}
    for k in BIG:
        delta[k], new_m[k], new_v[k] = _adamw(w[k][0], grads[k], m[k][0], v[k][0], "adamw_" + k)
    zero = jnp.zeros((), F32)
    fw, fm, fv = (_flat_small({k: t[k] for k in SMALL}, zero) for t in (w, m, v))
    fg = _flat_small(grads, zero)
    sd, sm, sv = _adamw(fw, fg, fm, fv, "adamw_small", tr=fw.shape[0])
    off = 0
    for k in SMALL:
        n = math.prod(w[k].shape)
        delta[k], new_m[k], new_v[k] = (t.reshape(-1)[off:off + n].reshape(w[k].shape[1:]) for t in (sd, sm, sv))
        off += n

    lead = lambda t: t[None]
    return (loss, lead(dx), *[lead(grads[k]) for k in WEIGHTS], *[lead(delta[k]) for k in WEIGHTS],
            *[lead(new_m[k]) for k in WEIGHTS], *[lead(new_v[k]) for k in WEIGHTS])
```

```python
import functools
import math

import jax
import jax.numpy as jnp
from jax import lax
from jax.experimental import pallas as pl
from jax.experimental.pallas import tpu as pltpu

F32 = jnp.float32
BF16 = jnp.bfloat16
SDS = jax.ShapeDtypeStruct

D_MODEL = 2048
HEAD_DIM = 64
N_Q_HEADS = 16
ATT_BLOCK = 128
ROT_DIM = 16
ROPE_THETA = 500000.0
Q_W = 1024
KV_W = 128
SSM_W = 1024
SSM_G = 64
SSM_GC = 16
SSM_P = 64
N_STATE = SSM_G * SSM_P
LANES = 128
SUBLANES = 8
N_LG = N_STATE // LANES
N_JB = 8
LG_PER_JB = N_LG // N_JB
D_FF = 8192
ZA_W = Q_W + 2 * KV_W + SSM_W
EPS = 1e-6
S5_CHUNK = 256
S5_SEG = S5_CHUNK // SUBLANES
VMEM_LIMIT = 56 * 1024 * 1024
NEG = -1e30

ADAM_LR = 0.001
ADAM_B1 = 0.9
ADAM_B2 = 0.999
ADAM_EPS = 1e-08
ADAM_WD = 0.01
ADAM_STEP = 10

MESH = pl.DeviceIdType.MESH


def _cp(sem):
    return pltpu.CompilerParams(dimension_semantics=sem, vmem_limit_bytes=VMEM_LIMIT)


def _mm(a, b, *, mode, out_dtype, tm, tn, tk, name, a_fn=None, epi=None, extras=()):
    if mode == "nn":
        (M, K), (K2, N) = a.shape, b.shape
    elif mode == "nt":
        (M, K), (N, K2) = a.shape, b.shape
    else:
        (K, M), (K2, N) = a.shape, b.shape
    assert K == K2, (a.shape, b.shape, mode)
    tm, tn, tk = min(tm, M), min(tn, N), min(tk, K)
    assert M % tm == 0 and N % tn == 0 and K % tk == 0, (M, N, K, tm, tn, tk)
    nk = K // tk
    if mode == "tn":
        a_spec = pl.BlockSpec((tk, tm), lambda i, j, k: (k, i))
        ca = 0
    else:
        a_spec = pl.BlockSpec((tm, tk), lambda i, j, k: (i, k))
        ca = 1
    if mode == "nt":
        b_spec = pl.BlockSpec((tn, tk), lambda i, j, k: (j, k))
        cb = 1
    else:
        b_spec = pl.BlockSpec((tk, tn), lambda i, j, k: (k, j))
        cb = 0
    dims = (((ca,), (cb,)), ((), ()))
    ne = len(extras)

    def body(a_ref, b_ref, *rest):
        ex = rest[:ne]
        o_ref = rest[ne]
        av = a_ref[...]
        if a_fn is not None:
            av = a_fn(av.astype(F32))
        p = lax.dot_general(av.astype(BF16), b_ref[...].astype(BF16), dims, preferred_element_type=F32)

        def fin(v):
            if epi is not None:
                v = epi(v, *[e[...] for e in ex])
            o_ref[...] = v.astype(out_dtype)

        if nk == 1:
            fin(p)
        else:
            acc = rest[ne + 1]
            k = pl.program_id(2)

            @pl.when(k == 0)
            def _():
                acc[...] = p

            @pl.when(k > 0)
            def _():
                acc[...] += p

            @pl.when(k == nk - 1)
            def _():
                fin(acc[...])

    return pl.pallas_call(
        body,
        name=name,
        grid=(M // tm, N // tn, nk),
        in_specs=[a_spec, b_spec] + [pl.BlockSpec((tm, tn), lambda i, j, k: (i, j)) for _ in extras],
        out_specs=pl.BlockSpec((tm, tn), lambda i, j, k: (i, j)),
        out_shape=SDS((M, N), out_dtype),
        scratch_shapes=[pltpu.VMEM((tm, tn), F32)] if nk > 1 else [],
        compiler_params=_cp(("parallel", "parallel", "arbitrary")),
    )(a, b, *extras)


def _rowwise(fn, rows, bcasts, outs, accs, *, tr, name):
    T = rows[0][0].shape[0]
    tr = min(tr, T)
    assert T % tr == 0
    nr, nb, no, na = len(rows), len(bcasts), len(outs), len(accs)
    in_specs = [pl.BlockSpec((tr, w), functools.partial(lambda i, c: (i, c), c=cb)) for (_, w, cb) in rows]
    in_specs += [pl.BlockSpec(b.shape, lambda i: (0, 0)) for b in bcasts]
    out_shape = [SDS((T, w), dt) for (w, dt) in outs] + [SDS(s, F32) for s in accs]
    out_specs = [pl.BlockSpec((tr, w), lambda i: (i, 0)) for (w, _) in outs]
    out_specs += [pl.BlockSpec(s, lambda i: (0, 0)) for s in accs]

    def body(*refs):
        ins = [r[...] for r in refs[:nr + nb]]
        o_refs = refs[nr + nb:nr + nb + no]
        a_refs = refs[nr + nb + no:]
        ro, ao = fn(*ins)
        for r, v in zip(o_refs, ro):
            r[...] = v.astype(r.dtype)
        if na:
            @pl.when(pl.program_id(0) == 0)
            def _():
                for r in a_refs:
                    r[...] = jnp.zeros(r.shape, F32)

            for r, v in zip(a_refs, ao):
                r[...] += v

    res = pl.pallas_call(
        body,
        name=name,
        grid=(T // tr,),
        in_specs=in_specs,
        out_specs=out_specs,
        out_shape=out_shape,
        compiler_params=_cp(("arbitrary",) if na else ("parallel",)),
    )(*[r[0] for r in rows], *bcasts)
    return res


def _rms(v):
    r = lax.rsqrt(jnp.mean(v * v, axis=-1, keepdims=True) + EPS)
    return v * r, r


def _rms_bwd(dy, xn, r, g):
    dxn = dy * g
    dv = r * (dxn - xn * jnp.mean(dxn * xn, axis=-1, keepdims=True))
    return dv, jnp.sum(dy * xn, axis=0, keepdims=True)


def _sig(v):
    return 1.0 / (1.0 + jnp.exp(-v))


_GELU_C = math.sqrt(2.0 / math.pi)


def _gelu(v):
    return 0.5 * v * (1.0 + jnp.tanh(_GELU_C * (v + 0.044715 * v * v * v)))


def _gelu_grad(v):
    t = jnp.tanh(_GELU_C * (v + 0.044715 * v * v * v))
    return 0.5 * (1.0 + t) + 0.5 * v * (1.0 - t * t) * _GELU_C * (1.0 + 3.0 * 0.044715 * v * v)


def _rope(v, c, s, sign):
    w = v.shape[1]
    m = lax.broadcasted_iota(jnp.int32, v.shape, 1) % HEAD_DIM
    p = jnp.where(m < ROT_DIM // 2, -pltpu.roll(v, w - ROT_DIM // 2, 1), pltpu.roll(v, ROT_DIM // 2, 1))
    return v * c + sign * (p * s)


def _rope_tables(T):
    half = ROT_DIM // 2
    inv = ROPE_THETA ** (-jnp.arange(half, dtype=F32) * 2.0 / ROT_DIM)
    ang = jnp.arange(T).astype(F32)[:, None] * inv[None, :]
    cos, sin = jnp.cos(ang), jnp.sin(ang)
    one = jnp.ones((T, HEAD_DIM - ROT_DIM), F32)
    c64 = jnp.concatenate([cos, cos, one], axis=1)
    s64 = jnp.concatenate([sin, sin, 0.0 * one], axis=1)
    return jnp.tile(c64, (1, 2)), jnp.tile(s64, (1, 2))


def _dup_half(m, lo):
    lane = lax.broadcasted_iota(jnp.int32, m.shape, 1)
    sw = pltpu.roll(m, HEAD_DIM, 1)
    return jnp.where(lane < HEAD_DIM, m, sw) if lo else jnp.where(lane >= HEAD_DIM, m, sw)


def _attn_mask(i):
    qi = lax.broadcasted_iota(jnp.int32, (ATT_BLOCK, 2 * ATT_BLOCK), 0)
    kj = lax.broadcasted_iota(jnp.int32, (ATT_BLOCK, 2 * ATT_BLOCK), 1)
    rel = qi + ATT_BLOCK - kj
    return (rel >= 0) & (rel < ATT_BLOCK) & ((kj >= ATT_BLOCK) | (i > 0))


_NT = (((1,), (1,)), ((), ()))
_TN = (((0,), (0,)), ((), ()))


def _attn_fwd(za, cos, sin, sinks):
    T = za.shape[0]
    nb = T // ATT_BLOCK
    kvb = Q_W // (2 * KV_W)

    def body(sink_ref, q_ref, kvp_ref, kvc_ref, cc_ref, sc_ref, cp_ref, sp_ref, o_ref):
        i = pl.program_id(0)
        cc, sc, cp, sp = cc_ref[...], sc_ref[...], cp_ref[...], sp_ref[...]
        q = (_rope(q_ref[...], jnp.tile(cc, (1, 8)), jnp.tile(sc, (1, 8)), 1.0) * 0.125).astype(BF16)
        kvp, kvc = kvp_ref[...], kvc_ref[...]
        k = jnp.concatenate([_rope(kvp[:, :KV_W], cp, sp, 1.0), _rope(kvc[:, :KV_W], cc, sc, 1.0)], axis=0).astype(BF16)
        v = jnp.concatenate([kvp[:, KV_W:], kvc[:, KV_W:]], axis=0).astype(BF16)
        ok = _attn_mask(i)
        lane = lax.broadcasted_iota(jnp.int32, (ATT_BLOCK, LANES), 1)
        for kvh in range(2):
            k2 = _dup_half(k, kvh == 0)
            v2 = _dup_half(v, kvh == 0)
            for pair in range(4):
                c0 = (kvh * 4 + pair) * LANES
                q2 = q[:, c0:c0 + LANES]
                halves = []
                for hf in range(2):
                    sink = sink_ref[0, 2 * (kvh * 4 + pair) + hf]
                    qm = jnp.where((lane < HEAD_DIM) == (hf == 0), q2, jnp.zeros_like(q2))
                    s = lax.dot_general(qm, k2, _NT, preferred_element_type=F32)
                    s = jnp.where(ok, s, NEG)
                    m = jnp.maximum(jnp.max(s, axis=1, keepdims=True), sink)
                    e = jnp.exp(s - m)
                    den = jnp.sum(e, axis=1, keepdims=True) + jnp.exp(sink - m)
                    p = (e * (1.0 / den)).astype(BF16)
                    halves.append(jnp.dot(p, v2, preferred_element_type=F32))
                o_ref[:, c0:c0 + LANES] = jnp.where(lane < HEAD_DIM, halves[0], halves[1]).astype(BF16)

    blk = lambda w, f: pl.BlockSpec((ATT_BLOCK, w), f)
    return pl.pallas_call(
        body,
        name="attn_fwd",
        grid=(nb,),
        in_specs=[
            pl.BlockSpec(memory_space=pltpu.SMEM),
            blk(Q_W, lambda i: (i, 0)),
            blk(2 * KV_W, lambda i: (jnp.maximum(i - 1, 0), kvb)),
            blk(2 * KV_W, lambda i: (i, kvb)),
            blk(LANES, lambda i: (i, 0)),
            blk(LANES, lambda i: (i, 0)),
            blk(LANES, lambda i: (jnp.maximum(i - 1, 0), 0)),
            blk(LANES, lambda i: (jnp.maximum(i - 1, 0), 0)),
        ],
        out_specs=blk(Q_W, lambda i: (i, 0)),
        out_shape=SDS((T, Q_W), BF16),
        compiler_params=_cp(("parallel",)),
    )(sinks, za, za, za, cos, sin, cos, sin)


def _attn_bwd(za, cos, sin, sinks, o, do):
    T = za.shape[0]
    nb = T // ATT_BLOCK
    kvb = Q_W // (2 * KV_W)

    def body(sink_ref, q_ref, kvp_ref, kvc_ref, cc_ref, sc_ref, cp_ref, sp_ref, o_ref, do_ref,
             dq_ref, dkv_ref, dsk_ref, carry, dqs):
        i = pl.program_id(0)

        @pl.when(i == 0)
        def _():
            carry[...] = jnp.zeros(carry.shape, F32)
            dsk_ref[...] = jnp.zeros(dsk_ref.shape, F32)

        @pl.when(i < nb)
        def _():
            cc, sc, cp, sp = cc_ref[...], sc_ref[...], cp_ref[...], sp_ref[...]
            ccq, scq = jnp.tile(cc, (1, 8)), jnp.tile(sc, (1, 8))
            q = (_rope(q_ref[...], ccq, scq, 1.0) * 0.125).astype(BF16)
            kvp, kvc = kvp_ref[...], kvc_ref[...]
            k = jnp.concatenate([_rope(kvp[:, :KV_W], cp, sp, 1.0), _rope(kvc[:, :KV_W], cc, sc, 1.0)], axis=0).astype(BF16)
            v = jnp.concatenate([kvp[:, KV_W:], kvc[:, KV_W:]], axis=0).astype(BF16)
            ok = _attn_mask(i)
            lane = lax.broadcasted_iota(jnp.int32, (ATT_BLOCK, LANES), 1)
            lane_kv = lax.broadcasted_iota(jnp.int32, (2 * ATT_BLOCK, LANES), 1)
            lane_s = lax.broadcasted_iota(jnp.int32, (1, LANES), 1)
            dsk = jnp.zeros((1, LANES), F32)
            dk_h, dv_h = [], []
            for kvh in range(2):
                k2 = _dup_half(k, kvh == 0)
                v2 = _dup_half(v, kvh == 0)
                dk2 = jnp.zeros((2 * ATT_BLOCK, LANES), F32)
                dv2 = jnp.zeros((2 * ATT_BLOCK, LANES), F32)
                for pair in range(4):
                    c0 = (kvh * 4 + pair) * LANES
                    q2 = q[:, c0:c0 + LANES]
                    do2 = do_ref[:, c0:c0 + LANES]
                    prod = do2.astype(F32) * o_ref[:, c0:c0 + LANES].astype(F32)
                    dqh = []
                    for hf in range(2):
                        h = 2 * (kvh * 4 + pair) + hf
                        sink = sink_ref[0, h]
                        sel = (lane < HEAD_DIM) == (hf == 0)
                        qm = jnp.where(sel, q2, jnp.zeros_like(q2))
                        dom = jnp.where(sel, do2, jnp.zeros_like(do2))
                        delta = jnp.sum(jnp.where(sel, prod, 0.0), axis=1, keepdims=True)
                        s = lax.dot_general(qm, k2, _NT, preferred_element_type=F32)
                        s = jnp.where(ok, s, NEG)
                        m = jnp.maximum(jnp.max(s, axis=1, keepdims=True), sink)
                        e = jnp.exp(s - m)
                        inv = 1.0 / (jnp.sum(e, axis=1, keepdims=True) + jnp.exp(sink - m))
                        p = e * inv
                        dsk = dsk + jnp.where(lane_s == h, -jnp.sum(jnp.exp(sink - m) * inv * delta), 0.0)
                        dp = lax.dot_general(dom, v2, _NT, preferred_element_type=F32)
                        ds = (p * (dp - delta)).astype(BF16)
                        dqh.append(jnp.dot(ds, k2, preferred_element_type=F32))
                        dk2 = dk2 + lax.dot_general(ds, qm, _TN, preferred_element_type=F32)
                        dv2 = dv2 + lax.dot_general(p.astype(BF16), dom, _TN, preferred_element_type=F32)
                    dqs[:, c0:c0 + LANES] = jnp.where(lane < HEAD_DIM, dqh[0], dqh[1]) * 0.125
                dk_h.append(dk2 + pltpu.roll(dk2, HEAD_DIM, 1))
                dv_h.append(dv2 + pltpu.roll(dv2, HEAD_DIM, 1))
            dk = jnp.where(lane_kv < HEAD_DIM, dk_h[0], dk_h[1])
            dv = jnp.where(lane_kv < HEAD_DIM, dv_h[0], dv_h[1])
            dq_ref[...] = _rope(dqs[...], ccq, scq, -1.0).astype(dq_ref.dtype)
            dkp = _rope(dk[:ATT_BLOCK], cp, sp, -1.0)
            dkc = _rope(dk[ATT_BLOCK:], cc, sc, -1.0)
            dkv_ref[...] = (carry[...] + jnp.concatenate([dkp, dv[:ATT_BLOCK]], axis=1)).astype(dkv_ref.dtype)
            carry[...] = jnp.concatenate([dkc, dv[ATT_BLOCK:]], axis=1)
            dsk_ref[...] += dsk

        @pl.when(i == nb)
        def _():
            dkv_ref[...] = carry[...].astype(dkv_ref.dtype)

    blk = lambda w, f: pl.BlockSpec((ATT_BLOCK, w), f)
    cur = lambda i: jnp.minimum(i, nb - 1)
    prv = lambda i: jnp.maximum(jnp.minimum(i, nb - 1) - 1, 0)
    return pl.pallas_call(
        body,
        name="attn_bwd",
        grid=(nb + 1,),
        in_specs=[
            pl.BlockSpec(memory_space=pltpu.SMEM),
            blk(Q_W, lambda i: (cur(i), 0)),
            blk(2 * KV_W, lambda i: (prv(i), kvb)),
            blk(2 * KV_W, lambda i: (cur(i), kvb)),
            blk(LANES, lambda i: (cur(i), 0)),
            blk(LANES, lambda i: (cur(i), 0)),
            blk(LANES, lambda i: (prv(i), 0)),
            blk(LANES, lambda i: (prv(i), 0)),
            blk(Q_W, lambda i: (cur(i), 0)),
            blk(Q_W, lambda i: (cur(i), 0)),
        ],
        out_specs=[
            blk(Q_W, lambda i: (cur(i), 0)),
            blk(2 * KV_W, lambda i: (jnp.maximum(i - 1, 0), 0)),
            pl.BlockSpec((1, LANES), lambda i: (0, 0)),
        ],
        out_shape=[SDS((T, Q_W), BF16), SDS((T, 2 * KV_W), BF16), SDS((1, LANES), F32)],
        scratch_shapes=[pltpu.VMEM((ATT_BLOCK, 2 * KV_W), F32), pltpu.VMEM((ATT_BLOCK, Q_W), F32)],
        compiler_params=_cp(("arbitrary",)),
    )(sinks, za, za, za, cos, sin, cos, sin, o, do)


def _s5_discretize(lam_re, lam_im, log_dt, b_re, b_im):
    dt = jnp.exp(log_dt)[:, None]
    mag = jnp.exp(lam_re * dt)
    a_re, a_im = mag * jnp.cos(lam_im * dt), mag * jnp.sin(lam_im * dt)
    den = lam_re * lam_re + lam_im * lam_im
    nr, ni = a_re - 1.0, a_im
    coef_re = (nr * lam_re + ni * lam_im) / den
    coef_im = (ni * lam_re - nr * lam_im) / den
    bb_re = coef_re[..., None] * b_re - coef_im[..., None] * b_im
    bb_im = coef_re[..., None] * b_im + coef_im[..., None] * b_re
    return a_re, a_im, bb_re, bb_im


def _blockdiag_in(bb):
    x = bb.reshape(N_JB, 8, SSM_P, SSM_GC).transpose(0, 1, 3, 2)
    return (x[:, :, :, None, :] * jnp.eye(8, dtype=bb.dtype)[None, :, None, :, None]).reshape(N_JB, 128, 512)


def _blockdiag_in_extract(m):
    x = m.reshape(N_JB, 8, SSM_GC, 8, SSM_P)
    x = jnp.einsum('jgchp,gh->jgcp', x, jnp.eye(8, dtype=m.dtype))
    return x.transpose(0, 1, 3, 2).reshape(SSM_G, SSM_P, SSM_GC)


def _blockdiag_out(c):
    x = c.reshape(N_JB, 8, SSM_GC, SSM_P).transpose(0, 1, 3, 2)
    return (x[:, :, :, None, :] * jnp.eye(8, dtype=c.dtype)[None, :, None, :, None]).reshape(N_JB, 512, 128)


def _blockdiag_out_extract(m):
    x = m.reshape(N_JB, 8, SSM_P, 8, SSM_GC)
    x = jnp.einsum('jgphc,gh->jgpc', x, jnp.eye(8, dtype=m.dtype))
    return x.transpose(0, 1, 3, 2).reshape(SSM_G, SSM_GC, SSM_P)


def _s5_tables(a_re, a_im):
    ar, ai = a_re.reshape(N_LG, 1, LANES), a_im.reshape(N_LG, 1, LANES)
    pr, pi = [ar], [ai]
    for _ in range(S5_SEG - 1):
        pr, pi = pr + [pr[-1] * ar - pi[-1] * ai], pi + [pr[-1] * ai + pi[-1] * ar]
    p_re, p_im = jnp.concatenate(pr, axis=1), jnp.concatenate(pi, axis=1)
    bc = lambda v: jnp.broadcast_to(v, (N_LG, SUBLANES, LANES))
    return p_re, p_im, bc(ar), bc(ai)


def _s5_scan(src_re, src_im, dst_re, dst_im, ar, ai, reverse, dst_row0=0):
    def step(n, carry):
        t = (S5_SEG - 1 - n) if reverse else n
        out = []
        for ll in range(LG_PER_JB):
            xr, xi = carry[2 * ll], carry[2 * ll + 1]
            idx = (ll, pl.ds(t, SUBLANES, stride=S5_SEG), slice(None))
            odx = (ll, pl.ds(t + dst_row0, SUBLANES, stride=S5_SEG), slice(None))
            nr = ar[ll] * xr - ai[ll] * xi + src_re[idx]
            ni = ar[ll] * xi + ai[ll] * xr + src_im[idx]
            dst_re[odx] = nr
            dst_im[odx] = ni
            out += [nr, ni]
        return tuple(out)
    z = jnp.zeros((SUBLANES, LANES), F32)
    return lax.fori_loop(0, S5_SEG, step, (z,) * (2 * LG_PER_JB))


def _s5_fixup(ends, in_re, in_im, mr, mi, s_re, s_im, reverse):
    cr, ci = in_re, in_im
    order = range(SUBLANES - 1, -1, -1) if reverse else range(SUBLANES)
    for s in order:
        s_re[:, s:s + 1, :] = cr
        s_im[:, s:s + 1, :] = ci
        er = jnp.stack([ends[2 * ll][s:s + 1, :] for ll in range(LG_PER_JB)])
        ei = jnp.stack([ends[2 * ll + 1][s:s + 1, :] for ll in range(LG_PER_JB)])
        cr, ci = mr * cr - mi * ci + er, mr * ci + mi * cr + ei
    return cr, ci


def _s5_correct(x_re, x_im, s_re, s_im, p_re, p_im, row0):
    for ll in range(LG_PER_JB):
        pr, pi = p_re[ll], p_im[ll]
        for s in range(SUBLANES):
            rows = slice(row0 + s * S5_SEG, row0 + (s + 1) * S5_SEG)
            sr, si = s_re[ll, s:s + 1, :], s_im[ll, s:s + 1, :]
            x_re[ll, rows, :] = x_re[ll, rows, :] + (pr * sr - pi * si)
            x_im[ll, rows, :] = x_im[ll, rows, :] + (pr * si + pi * sr)


def _s5_specs(nc, rev):
    cidx = (lambda c: nc - 1 - c) if rev else (lambda c: c)
    jb = lambda shape: pl.BlockSpec(shape, lambda j, c: (j, 0, 0))
    return cidx, [
        jb((1, LANES, 4 * LANES)), jb((1, LANES, 4 * LANES)),
        jb((1, 4 * LANES, LANES)), jb((1, 4 * LANES, LANES)),
        pl.BlockSpec((1, LANES), lambda j, c: (0, j)),
        jb((LG_PER_JB, SUBLANES, LANES)), jb((LG_PER_JB, SUBLANES, LANES)),
        jb((LG_PER_JB, 1, LANES)), jb((LG_PER_JB, 1, LANES)),
        jb((LG_PER_JB, S5_SEG, LANES)), jb((LG_PER_JB, S5_SEG, LANES)),
    ]


def _s5_fwd(za, prm):
    T = za.shape[0]
    R = S5_CHUNK
    nc = T // R
    ub = (Q_W + 2 * KV_W) // LANES
    _, pspecs = _s5_specs(nc, False)

    def body(u_ref, bre_ref, bim_ref, cre_ref, cim_ref, d_ref, are_ref, aim_ref, alr_ref, ali_ref, pr_ref, pi_ref,
             yg_ref, x0r_ref, x0i_ref, bur, bui, xsr, xsi, sr, si, xcr, xci):
        c = pl.program_id(1)

        @pl.when(c == 0)
        def _():
            xcr[...] = jnp.zeros(xcr.shape, F32)
            xci[...] = jnp.zeros(xci.shape, F32)

        u = u_ref[...]
        ub16 = u.astype(BF16)
        b_r = jnp.dot(ub16, bre_ref[0].astype(BF16), preferred_element_type=F32)
        b_i = jnp.dot(ub16, bim_ref[0].astype(BF16), preferred_element_type=F32)
        for ll in range(LG_PER_JB):
            bur[ll] = b_r[:, ll * LANES:(ll + 1) * LANES]
            bui[ll] = b_i[:, ll * LANES:(ll + 1) * LANES]
        ar = [are_ref[ll] for ll in range(LG_PER_JB)]
        ai = [aim_ref[ll] for ll in range(LG_PER_JB)]
        ends = _s5_scan(bur, bui, xsr, xsi, ar, ai, False)
        in_r, in_i = xcr[...], xci[...]
        x0r_ref[0] = in_r
        x0i_ref[0] = in_i
        out_r, out_i = _s5_fixup(ends, in_r, in_i, alr_ref[...], ali_ref[...], sr, si, False)
        xcr[...] = out_r
        xci[...] = out_i
        _s5_correct(xsr, xsi, sr, si, pr_ref, pi_ref, 0)
        y = d_ref[...] * u
        for ll in range(LG_PER_JB):
            rows = slice(ll * LANES, (ll + 1) * LANES)
            y = y + jnp.dot(xsr[ll].astype(BF16), cre_ref[0, rows, :].astype(BF16), preferred_element_type=F32)
            y = y - jnp.dot(xsi[ll].astype(BF16), cim_ref[0, rows, :].astype(BF16), preferred_element_type=F32)
        yg_ref[...] = _gelu(y).astype(BF16)

    st = pl.BlockSpec((1, LG_PER_JB, 1, LANES), lambda j, c: (c, j, 0, 0))
    vm = lambda rows: pltpu.VMEM((LG_PER_JB, rows, LANES), F32)
    return pl.pallas_call(
        body,
        name="s5_fwd",
        grid=(N_JB, nc),
        in_specs=[pl.BlockSpec((R, LANES), lambda j, c: (c, ub + j))] + pspecs,
        out_specs=[pl.BlockSpec((R, LANES), lambda j, c: (c, j)), st, st],
        out_shape=[SDS((T, SSM_W), BF16), SDS((nc, N_LG, 1, LANES), F32), SDS((nc, N_LG, 1, LANES), F32)],
        scratch_shapes=[vm(R), vm(R), vm(R), vm(R), vm(SUBLANES), vm(SUBLANES), vm(1), vm(1)],
        compiler_params=_cp(("parallel", "arbitrary")),
    )(za, *prm)


def _s5_bwd(za, dyg, x0r, x0i, prm, prev_tables):
    T = za.shape[0]
    R = S5_CHUNK
    nc = T // R
    ub = (Q_W + 2 * KV_W) // LANES
    cidx, pspecs = _s5_specs(nc, True)
    PAD = SUBLANES

    def body(u_ref, dyg_ref, x0r_ref, x0i_ref, bre_ref, bim_ref, cre_ref, cim_ref, d_ref, are_ref, aim_ref,
             alr_ref, ali_ref, pr_ref, pi_ref, qr_ref, qi_ref,
             du_ref, dar_ref, dai_ref, dbr_ref, dbi_ref, dcr_ref, dci_ref, dd_ref,
             bur, bui, xsr, xsi, sr, si, gcr, gci):
        c = pl.program_id(1)

        @pl.when(c == 0)
        def _():
            gcr[...] = jnp.zeros(gcr.shape, F32)
            gci[...] = jnp.zeros(gci.shape, F32)
            dar_ref[...] = jnp.zeros(dar_ref.shape, F32)
            dai_ref[...] = jnp.zeros(dai_ref.shape, F32)
            dbr_ref[...] = jnp.zeros(dbr_ref.shape, F32)
            dbi_ref[...] = jnp.zeros(dbi_ref.shape, F32)
            dcr_ref[...] = jnp.zeros(dcr_ref.shape, F32)
            dci_ref[...] = jnp.zeros(dci_ref.shape, F32)
            dd_ref[...] = jnp.zeros(dd_ref.shape, F32)

        u = u_ref[...]
        ub16 = u.astype(BF16)
        bre, bim = bre_ref[0].astype(BF16), bim_ref[0].astype(BF16)
        cre, cim = cre_ref[0].astype(BF16), cim_ref[0].astype(BF16)
        b_r = jnp.dot(ub16, bre, preferred_element_type=F32)
        b_i = jnp.dot(ub16, bim, preferred_element_type=F32)
        for ll in range(LG_PER_JB):
            bur[ll] = b_r[:, ll * LANES:(ll + 1) * LANES]
            bui[ll] = b_i[:, ll * LANES:(ll + 1) * LANES]
        ar = [are_ref[ll] for ll in range(LG_PER_JB)]
        ai = [aim_ref[ll] for ll in range(LG_PER_JB)]
        ends = _s5_scan(bur, bui, xsr, xsi, ar, ai, False, dst_row0=PAD)
        in_r, in_i = x0r_ref[0], x0i_ref[0]
        _s5_fixup(ends, in_r, in_i, alr_ref[...], ali_ref[...], sr, si, False)
        _s5_correct(xsr, xsi, sr, si, pr_ref, pi_ref, PAD)
        xsr[:, PAD - 1:PAD, :] = in_r
        xsi[:, PAD - 1:PAD, :] = in_i
        y = d_ref[...] * u
        for ll in range(LG_PER_JB):
            rows = slice(ll * LANES, (ll + 1) * LANES)
            y = y + jnp.dot(xsr[ll, PAD:, :].astype(BF16), cre[rows, :], preferred_element_type=F32)
            y = y - jnp.dot(xsi[ll, PAD:, :].astype(BF16), cim[rows, :], preferred_element_type=F32)
        dy = dyg_ref[...] * _gelu_grad(y)
        dyb = dy.astype(BF16)
        dd_ref[...] += jnp.sum(dy * u, axis=0, keepdims=True)
        du = d_ref[...] * dy
        g_r = lax.dot_general(dyb, cre, _NT, preferred_element_type=F32)
        g_i = -lax.dot_general(dyb, cim, _NT, preferred_element_type=F32)
        for ll in range(LG_PER_JB):
            rows = slice(ll * LANES, (ll + 1) * LANES)
            dcr_ref[0, rows, :] += lax.dot_general(xsr[ll, PAD:, :].astype(BF16), dyb, _TN, preferred_element_type=F32)
            dci_ref[0, rows, :] -= lax.dot_general(xsi[ll, PAD:, :].astype(BF16), dyb, _TN, preferred_element_type=F32)
            bur[ll] = g_r[:, ll * LANES:(ll + 1) * LANES]
            bui[ll] = g_i[:, ll * LANES:(ll + 1) * LANES]
        ends = _s5_scan(bur, bui, bur, bui, ar, [-v for v in ai], True)
        out_r, out_i = _s5_fixup(ends, gcr[...], gci[...], alr_ref[...], -ali_ref[...], sr, si, True)
        gcr[...] = out_r
        gci[...] = out_i
        _s5_correct(bur, bui, sr, si, qr_ref, qi_ref, 0)
        for ll in range(LG_PER_JB):
            cols = slice(ll * LANES, (ll + 1) * LANES)
            gr, gi = bur[ll], bui[ll]
            xpr, xpi = xsr[ll, PAD - 1:PAD - 1 + R, :], xsi[ll, PAD - 1:PAD - 1 + R, :]
            red = lambda v: v.reshape(R // SUBLANES, SUBLANES, LANES).sum(axis=0)
            dar_ref[ll] += red(xpr * gr + xpi * gi)
            dai_ref[ll] += red(xpr * gi - xpi * gr)
            grb, gib = gr.astype(BF16), gi.astype(BF16)
            dbr_ref[0, :, cols] += lax.dot_general(ub16, grb, _TN, preferred_element_type=F32)
            dbi_ref[0, :, cols] += lax.dot_general(ub16, gib, _TN, preferred_element_type=F32)
            du = du + lax.dot_general(grb, bre[:, cols], _NT, preferred_element_type=F32)
            du = du + lax.dot_general(gib, bim[:, cols], _NT, preferred_element_type=F32)
        du_ref[...] = du.astype(du_ref.dtype)

    st = pl.BlockSpec((1, LG_PER_JB, 1, LANES), lambda j, c: (cidx(c), j, 0, 0))
    jb = lambda shape: pl.BlockSpec(shape, lambda j, c: (j, 0, 0))
    vm = lambda rows: pltpu.VMEM((LG_PER_JB, rows, LANES), F32)
    return pl.pallas_call(
        body,
        name="s5_bwd",
        grid=(N_JB, nc),
        in_specs=[pl.BlockSpec((R, LANES), lambda j, c: (cidx(c), ub + j)),
                  pl.BlockSpec((R, LANES), lambda j, c: (cidx(c), j)), st, st] + pspecs
                 + [jb((LG_PER_JB, S5_SEG, LANES)), jb((LG_PER_JB, S5_SEG, LANES))],
        out_specs=[pl.BlockSpec((R, LANES), lambda j, c: (cidx(c), j)),
                   jb((LG_PER_JB, SUBLANES, LANES)), jb((LG_PER_JB, SUBLANES, LANES)),
                   jb((1, LANES, 4 * LANES)), jb((1, LANES, 4 * LANES)),
                   jb((1, 4 * LANES, LANES)), jb((1, 4 * LANES, LANES)),
                   pl.BlockSpec((1, LANES), lambda j, c: (0, j))],
        out_shape=[SDS((T, SSM_W), BF16), SDS((N_LG, SUBLANES, LANES), F32), SDS((N_LG, SUBLANES, LANES), F32),
                   SDS((N_JB, LANES, 4 * LANES), F32), SDS((N_JB, LANES, 4 * LANES), F32),
                   SDS((N_JB, 4 * LANES, LANES), F32), SDS((N_JB, 4 * LANES, LANES), F32), SDS((1, SSM_W), F32)],
        scratch_shapes=[vm(R), vm(R), vm(R + PAD), vm(R + PAD), vm(SUBLANES), vm(SUBLANES), vm(1), vm(1)],
        compiler_params=_cp(("parallel", "arbitrary")),
    )(za, dyg, x0r, x0i, *prm, *prev_tables)


def _local_step(x, target, gains, w_a, w_g, sinks, s5w, w_glu, w_ba, w_bs, w_out, w_up, w_down):
    T = x.shape[0]
    D = D_MODEL
    g1, g2, g3, g4 = gains
    cos, sin = _rope_tables(T)
    lam_re, lam_im, log_dt, b_re, b_im, c_re, c_im, d_skip = s5w
    (a_re, a_im, bb_re, bb_im), disc_vjp = jax.vjp(_s5_discretize, lam_re, lam_im, log_dt, b_re, b_im)
    p_re, p_im, abr, abi = _s5_tables(a_re, a_im)
    prm = (_blockdiag_in(bb_re), _blockdiag_in(bb_im), _blockdiag_out(c_re), _blockdiag_out(c_im),
           d_skip.reshape(1, SSM_W), abr, abi, p_re[:, S5_SEG - 1:, :], p_im[:, S5_SEG - 1:, :], p_re, p_im)
    rev_tables = (p_re[:, ::-1, :], -p_im[:, ::-1, :])
    mm = functools.partial(_mm, tm=512, tn=1024, tk=2048)

    h = _rowwise(lambda xv, g: ((_rms(xv)[0] * g,), ()), [(x, D, 0)], [g1], [(D, BF16)], [], tr=512, name="norm1")[0]
    za = _mm(h, w_a, mode="nn", out_dtype=F32, tm=512, tn=1152, tk=2048, name="mm_za")
    zg = mm(h, w_g, mode="nn", out_dtype=F32, name="mm_zg")
    o_attn = _attn_fwd(za, cos, sin, sinks)
    yg, x0r, x0i = _s5_fwd(za, prm)
    zglu = mm(yg, w_glu, mode="nn", out_dtype=F32, name="mm_glu")
    o_ssm = _rowwise(lambda z1, z2: ((z1 * _sig(z2),), ()), [(zglu, SSM_W, 0), (zglu, SSM_W, 1)], [],
                     [(SSM_W, BF16)], [], tr=512, name="glu")[0]
    ya = mm(o_attn, w_ba, mode="nn", out_dtype=F32, name="mm_ya")
    ys = mm(o_ssm, w_bs, mode="nn", out_dtype=F32, name="mm_ys")
    mi = _rowwise(lambda ga, gs, a, s: ((_sig(ga) * a + _sig(gs) * s,), ()),
                  [(zg, D, 0), (zg, D, 1), (ya, D, 0), (ys, D, 0)], [], [(D, BF16)], [], tr=256, name="gate")[0]
    mixed = mm(mi, w_out, mode="nn", out_dtype=F32, name="mm_out")

    def f_post(xv, mv, g2v, g3v):
        x1v = xv + _rms(mv)[0] * g2v
        return (x1v, _rms(x1v)[0] * g3v), ()
    x1, h2 = _rowwise(f_post, [(x, D, 0), (mixed, D, 0)], [g2, g3], [(D, F32), (D, BF16)], [], tr=256, name="post_mix")
    act = mm(h2, w_up, mode="nn", out_dtype=BF16, name="mm_up", epi=lambda v: jnp.maximum(v, 0.0))
    f = _mm(act, w_down, mode="nn", out_dtype=F32, tm=512, tn=2048, tk=2048, name="mm_down", a_fn=lambda v: v * v)

    def f_final(x1v, fv, tv, g4v):
        fn, r = _rms(fv)
        e = x1v + fn * g4v - tv
        dx2v = e * (1.0 / D)
        dfv, dg4v = _rms_bwd(dx2v, fn, r, g4v)
        return (dfv, dx2v), (dg4v, jnp.zeros((SUBLANES, LANES), F32) + 0.5 * jnp.sum(e * e) * (1.0 / D))
    df, dx2, dg4, lossb = _rowwise(f_final, [(x1, D, 0), (f, D, 0), (target, D, 0)], [g4],
                                   [(D, BF16), (D, F32)], [(1, D), (SUBLANES, LANES)], tr=256, name="final")

    dpre = mm(df, w_down, mode="nt", out_dtype=BF16, name="mm_dact", epi=lambda v, a: v * (2.0 * a.astype(F32)), extras=(act,))
    wg = functools.partial(_mm, mode="tn", out_dtype=F32, tm=1024, tn=2048, tk=512)
    d_w_down = wg(act, df, name="wg_down", a_fn=lambda v: v * v)
    dh2 = _mm(dpre, w_up, mode="nt", out_dtype=F32, tm=512, tn=2048, tk=2048, name="mm_dh2")
    d_w_up = wg(h2, dpre, name="wg_up")

    def f_mid(dx2v, dh2v, x1v, mv, g2v, g3v):
        x1n, r3 = _rms(x1v)
        d3, dg3v = _rms_bwd(dh2v, x1n, r3, g3v)
        dx1v = dx2v + d3
        mn, r2 = _rms(mv)
        dmv, dg2v = _rms_bwd(dx1v, mn, r2, g2v)
        return (dx1v, dmv), (dg3v, dg2v)
    dx1, dmixed, dg3, dg2 = _rowwise(f_mid, [(dx2, D, 0), (dh2, D, 0), (x1, D, 0), (mixed, D, 0)], [g2, g3],
                                     [(D, F32), (D, BF16)], [(1, D), (1, D)], tr=256, name="mid")

    dmi = mm(dmixed, w_out, mode="nt", out_dtype=F32, name="mm_dmi")
    d_w_out = wg(mi, dmixed, name="wg_out")

    def f_gate(dv, ga, gs, a, s):
        sa, ss = _sig(ga), _sig(gs)
        return (dv * sa, dv * ss, jnp.concatenate([dv * a * sa * (1.0 - sa), dv * s * ss * (1.0 - ss)], axis=1)), ()
    dya, dys, dzg = _rowwise(f_gate, [(dmi, D, 0), (zg, D, 0), (zg, D, 1), (ya, D, 0), (ys, D, 0)], [],
                             [(D, BF16), (D, BF16), (2 * D, BF16)], [], tr=256, name="gate_bwd")
    do_attn = mm(dya, w_ba, mode="nt", out_dtype=BF16, name="mm_doa")
    d_w_ba = wg(o_attn, dya, name="wg_ba")
    do_ssm = mm(dys, w_bs, mode="nt", out_dtype=F32, name="mm_dos")
    d_w_bs = wg(o_ssm, dys, name="wg_bs")

    def f_glu(dv, z1, z2):
        s2 = _sig(z2)
        return (jnp.concatenate([dv * s2, dv * z1 * s2 * (1.0 - s2)], axis=1),), ()
    dzglu = _rowwise(f_glu, [(do_ssm, SSM_W, 0), (zglu, SSM_W, 0), (zglu, SSM_W, 1)], [], [(2 * SSM_W, BF16)], [],
                     tr=512, name="glu_bwd")[0]
    dyg = mm(dzglu, w_glu, mode="nt", out_dtype=F32, name="mm_dyg")
    d_w_glu = wg(yg, dzglu, name="wg_glu")
    du, dar, dai, dbr, dbi, dcr, dci, ddv = _s5_bwd(za, dyg, x0r, x0i, prm, rev_tables)
    dq, dkv, dsk = _attn_bwd(za, cos, sin, sinks, o_attn, do_attn)
    dza = jnp.concatenate([dq, dkv, du], axis=1)
    dh = mm(dza, w_a, mode="nt", out_dtype=F32, name="mm_dh_a", tk=ZA_W)
    dh = _mm(dzg, w_g, mode="nt", out_dtype=F32, tm=512, tn=1024, tk=2048, name="mm_dh_g", epi=lambda v, p: v + p, extras=(dh,))
    d_w_a = _mm(h, dza, mode="tn", out_dtype=F32, tm=1024, tn=ZA_W, tk=512, name="wg_a")
    d_w_g = wg(h, dzg, name="wg_g")

    def f_first(dx1v, dhv, xv, g1v):
        xn, r1 = _rms(xv)
        d1, dg1v = _rms_bwd(dhv, xn, r1, g1v)
        return (dx1v + d1,), (dg1v,)
    dx, dg1 = _rowwise(f_first, [(dx1, D, 0), (dh, D, 0), (x, D, 0)], [g1], [(D, F32)], [(1, D)], tr=256, name="first")

    da_re = dar.sum(axis=1).reshape(SSM_G, SSM_P)
    da_im = dai.sum(axis=1).reshape(SSM_G, SSM_P)
    d_lam_re, d_lam_im, d_log_dt, d_b_re, d_b_im = disc_vjp(
        (da_re, da_im, _blockdiag_in_extract(dbr), _blockdiag_in_extract(dbi)))
    small = dict(norm_mix_pre=dg1, norm_mix_post=dg2, norm_mlp_pre=dg3, norm_mlp_post=dg4,
                 sinks=dsk[:, :N_Q_HEADS], lam_re=d_lam_re, lam_im=d_lam_im, log_dt=d_log_dt,
                 b_re=d_b_re, b_im=d_b_im, c_re=_blockdiag_out_extract(dcr), c_im=_blockdiag_out_extract(dci),
                 d_skip=ddv.reshape(SSM_G, SSM_GC))
    big = dict(w_in=jnp.concatenate([d_w_a, d_w_g], axis=1), w_glu=d_w_glu,
               w_branch=jnp.concatenate([d_w_ba, d_w_bs], axis=0), w_out=d_w_out, w_up=d_w_up, w_down=d_w_down)
    return lossb[0, 0], dx, small, big


ANY = pl.BlockSpec(memory_space=pl.ANY)


def _place():
    x, y, c = lax.axis_index("x"), lax.axis_index("y"), lax.axis_index("c")
    others = [(1 - x, y), (x, 1 - y), (1 - x, 1 - y)]
    return x, y, c, others


def _remote(src, dst, ssem, rsem, to):
    return pltpu.make_async_remote_copy(src_ref=src, dst_ref=dst, send_sem=ssem, recv_sem=rsem,
                                        device_id=to, device_id_type=MESH)


def _cast_into_slot(w, k_arr):
    rows, cols = w.shape
    tr = 256

    def body(k_ref, w_ref, o_ref):
        o_ref[0] = w_ref[...].astype(BF16)

    return pl.pallas_call(
        body,
        name="cast_into_slot",
        grid_spec=pltpu.PrefetchScalarGridSpec(
            num_scalar_prefetch=1,
            grid=(rows // tr,),
            in_specs=[pl.BlockSpec((tr, cols), lambda i, k: (i, 0))],
            out_specs=pl.BlockSpec((1, tr, cols), lambda i, k: (k[0], i, 0)),
        ),
        out_shape=SDS((4, rows, cols), BF16),
        compiler_params=_cp(("parallel",)),
    )(k_arr, w)


def _gather_weights(slotted):
    n = len(slotted)

    def body(*refs):
        ins, outs = refs[:n], refs[n:2 * n]
        ssem, rsem, fs_sem, fr_sem = refs[2 * n:]
        x, y, c, others = _place()
        me, sib = 2 * x + y, (x, y, 1 - c)

        def half(w, hc):
            hr = ins[w].shape[1] // 2
            return pl.ds(pl.multiple_of(hc * hr, 16), hr)

        sends, passes = [], []
        for w in range(n):
            for r, (ox, oy) in enumerate(others):
                cp = _remote(ins[w].at[me, half(w, c), :], outs[w].at[me, half(w, c), :],
                             ssem.at[3 * w + r], rsem.at[3 * w + r], (ox, oy, c))
                cp.start()
                sends.append(cp)
        for w in range(n):
            for r, (ox, oy) in enumerate(others):
                got = outs[w].at[2 * ox + oy, half(w, c), :]
                _remote(got, got, ssem.at[3 * w + r], rsem.at[3 * w + r], (ox, oy, c)).wait_recv()
                cp = _remote(got, got, fs_sem.at[3 * w + r], fr_sem.at[3 * w + r], sib)
                cp.start()
                passes.append(cp)
        for w in range(n):
            for r, (ox, oy) in enumerate(others):
                got = outs[w].at[2 * ox + oy, half(w, 1 - c), :]
                _remote(got, got, fs_sem.at[3 * w + r], fr_sem.at[3 * w + r], sib).wait_recv()
        for cp in sends + passes:
            cp.wait_send()

    dma = pltpu.SemaphoreType.DMA
    return pl.pallas_call(
        body,
        name="gather_weights",
        in_specs=[ANY] * n,
        out_specs=[ANY] * n,
        out_shape=[SDS(s.shape, s.dtype) for s in slotted],
        input_output_aliases={w: w for w in range(n)},
        scratch_shapes=[dma((3 * n,)), dma((3 * n,)), dma((3 * n,)), dma((3 * n,))],
    )(*slotted)


def _pair_exchange(grads):
    n = len(grads)

    def body(*refs):
        ins, outs = refs[:n], refs[n:2 * n]
        ssem, rsem = refs[2 * n:]
        x, y, c, _ = _place()
        cps = []
        for w in range(n):
            hr = ins[w].shape[1] // 2
            src = ins[w].at[:, pl.ds(pl.multiple_of((1 - c) * hr, 8), hr), :]
            cp = _remote(src, outs[w], ssem.at[w], rsem.at[w], (x, y, 1 - c))
            cp.start()
            cps.append(cp)
        for cp in cps:
            cp.wait()

    dma = pltpu.SemaphoreType.DMA
    return pl.pallas_call(
        body,
        name="pair_exchange",
        in_specs=[ANY] * n,
        out_specs=[ANY] * n,
        out_shape=[SDS((4, g.shape[1] // 2, g.shape[2]), g.dtype) for g in grads],
        scratch_shapes=[dma((n,)), dma((n,))],
    )(*grads)


def _pair_sum(g, r, c_arr):
    _, _, hr, cols = g.shape
    tr = min(256, hr)

    def body(c_ref, g_ref, r_ref, o_ref):
        o_ref[0] = (g_ref[0, 0] + r_ref[0]).astype(BF16)

    return pl.pallas_call(
        body,
        name="pair_sum",
        grid_spec=pltpu.PrefetchScalarGridSpec(
            num_scalar_prefetch=1,
            grid=(4, hr // tr),
            in_specs=[pl.BlockSpec((1, 1, tr, cols), lambda k, i, c_ref: (k, c_ref[0], i, 0)),
                      pl.BlockSpec((1, tr, cols), lambda k, i, c_ref: (k, i, 0))],
            out_specs=pl.BlockSpec((1, tr, cols), lambda k, i, c_ref: (k, i, 0)),
        ),
        out_shape=SDS((4, hr, cols), BF16),
        compiler_params=_cp(("parallel", "parallel")),
    )(c_arr, g, r)


def _chip_exchange(psums):
    n = len(psums)

    def body(*refs):
        ins, outs = refs[:n], refs[n:2 * n]
        ssem, rsem = refs[2 * n:]
        x, y, c, others = _place()
        cps = []
        for w in range(n):
            for r, (ox, oy) in enumerate(others):
                cp = _remote(ins[w].at[2 * ox + oy], outs[w].at[r], ssem.at[3 * w + r], rsem.at[3 * w + r], (ox, oy, c))
                cp.start()
                cps.append(cp)
        for cp in cps:
            cp.wait()

    dma = pltpu.SemaphoreType.DMA
    return pl.pallas_call(
        body,
        name="chip_exchange",
        in_specs=[ANY] * n,
        out_specs=[ANY] * n,
        out_shape=[SDS((3,) + p.shape[1:], p.dtype) for p in psums],
        scratch_shapes=[dma((3 * n,)), dma((3 * n,))],
    )(*psums)


def _chip_sum(g, r, q, kc_arr):
    _, _, hr, cols = g.shape
    tr = min(256, hr)

    def body(kc_ref, g_ref, r_ref, q_ref, o_ref):
        s = g_ref[0, 0] + r_ref[0]
        for j in range(3):
            s = s + q_ref[j].astype(F32)
        o_ref[...] = s

    return pl.pallas_call(
        body,
        name="chip_sum",
        grid_spec=pltpu.PrefetchScalarGridSpec(
            num_scalar_prefetch=1,
            grid=(hr // tr,),
            in_specs=[pl.BlockSpec((1, 1, tr, cols), lambda i, kc: (kc[0], kc[1], i, 0)),
                      pl.BlockSpec((1, tr, cols), lambda i, kc: (kc[0], i, 0)),
                      pl.BlockSpec((3, tr, cols), lambda i, kc: (0, i, 0))],
            out_specs=pl.BlockSpec((tr, cols), lambda i, kc: (kc[1] * (hr // tr) + i, 0)),
        ),
        out_shape=SDS((2 * hr, cols), F32),
        compiler_params=_cp(("parallel",)),
    )(kc_arr, g, r, q)


def _pair_share(blocks):
    n = len(blocks)

    def body(*refs):
        ins, outs = refs[:n], refs[n:2 * n]
        ssem, rsem = refs[2 * n:]
        x, y, c, _ = _place()
        cps = []
        for w in range(n):
            hr = ins[w].shape[0] // 2
            rows = pl.ds(pl.multiple_of(c * hr, 8), hr)
            cp = _remote(ins[w].at[rows, :], outs[w].at[rows, :], ssem.at[w], rsem.at[w], (x, y, 1 - c))
            cp.start()
            cps.append(cp)
        for w in range(n):
            hr = ins[w].shape[0] // 2
            other = outs[w].at[pl.ds(pl.multiple_of((1 - c) * hr, 8), hr), :]
            _remote(other, other, ssem.at[w], rsem.at[w], (x, y, 1 - c)).wait_recv()
        for cp in cps:
            cp.wait_send()

    dma = pltpu.SemaphoreType.DMA
    return pl.pallas_call(
        body,
        name="pair_share",
        in_specs=[ANY] * n,
        out_specs=[ANY] * n,
        out_shape=[SDS(b.shape, b.dtype) for b in blocks],
        input_output_aliases={w: w for w in range(n)},
        scratch_shapes=[dma((n,)), dma((n,))],
    )(*blocks)


def _all_reduce_small(buf):
    rows = buf.shape[0]

    def body(in_ref, o_ref, slots, ssem, rsem):
        x, y, c, _ = _place()
        me = 4 * x + 2 * y + c
        slots[me] = in_ref[...]
        cps = []
        for r in range(1, 8):
            px = 1 - x if r & 4 else x
            py = 1 - y if r & 2 else y
            pc = 1 - c if r & 1 else c
            cp = _remote(in_ref, slots.at[me], ssem.at[r - 1], rsem.at[r - 1], (px, py, pc))
            cp.start()
            cps.append((cp, 4 * px + 2 * py + pc))
        for r, (cp, peer) in enumerate(cps):
            _remote(in_ref, slots.at[peer], ssem.at[r], rsem.at[r], (x, y, c)).wait_recv()
        s = slots[0]
        for d in range(1, 8):
            s = s + slots[d]
        o_ref[...] = s
        for cp, _ in cps:
            cp.wait_send()

    dma = pltpu.SemaphoreType.DMA
    return pl.pallas_call(
        body,
        name="all_reduce_small",
        in_specs=[pl.BlockSpec(memory_space=pltpu.VMEM)],
        out_specs=pl.BlockSpec(memory_space=pltpu.VMEM),
        out_shape=SDS(buf.shape, F32),
        scratch_shapes=[pltpu.VMEM((8, rows, LANES), F32), dma((7,)), dma((7,))],
        compiler_params=pltpu.CompilerParams(vmem_limit_bytes=VMEM_LIMIT),
    )(buf)


def _adam_fn(w, g, m, v):
    m2 = ADAM_B1 * m + (1.0 - ADAM_B1) * g
    v2 = ADAM_B2 * v + (1.0 - ADAM_B2) * (g * g)
    m_hat = m2 / (1.0 - ADAM_B1 ** ADAM_STEP)
    v_hat = v2 / (1.0 - ADAM_B2 ** ADAM_STEP)
    return (-ADAM_LR * (m_hat / (jnp.sqrt(v_hat) + ADAM_EPS) + ADAM_WD * w), m2, v2), ()


def _adamw(w, g, m, v, name, tr=256):
    cols = w.shape[1]
    return _rowwise(_adam_fn, [(w, cols, 0), (g, cols, 0), (m, cols, 0), (v, cols, 0)], [],
                    [(cols, F32)] * 3, [], tr=tr, name=name)


BIG = ("w_in", "w_glu", "w_branch", "w_out", "w_up", "w_down")
COL_SHARDED = ("w_in", "w_glu", "w_up")
SMALL = ("norm_mix_pre", "norm_mix_post", "norm_mlp_pre", "norm_mlp_post", "sinks", "lam_re", "lam_im", "log_dt",
         "b_re", "b_im", "c_re", "c_im", "d_skip")
WEIGHTS = ("norm_mix_pre", "norm_mix_post", "norm_mlp_pre", "norm_mlp_post", "w_in", "sinks", "lam_re", "lam_im",
           "log_dt", "b_re", "b_im", "c_re", "c_im", "d_skip", "w_glu", "w_branch", "w_out", "w_up", "w_down")


def _flat_small(vals, extra):
    flat = jnp.concatenate([vals[k].reshape(-1) for k in SMALL] + [extra.reshape(-1)])
    rows = -(-flat.shape[0] // (SUBLANES * LANES)) * SUBLANES
    return jnp.pad(flat, (0, rows * LANES - flat.shape[0])).reshape(rows, LANES)


def kernel(x, norm_mix_pre, norm_mix_post, norm_mlp_pre, norm_mlp_post, w_in, sinks, lam_re, lam_im, log_dt, b_re, b_im, c_re, c_im, d_skip, w_glu, w_branch, w_out, w_up, w_down, loss_target, m_norm_mix_pre, m_norm_mix_post, m_norm_mlp_pre, m_norm_mlp_post, m_w_in, m_sinks, m_lam_re, m_lam_im, m_log_dt, m_b_re, m_b_im, m_c_re, m_c_im, m_d_skip, m_w_glu, m_w_branch, m_w_out, m_w_up, m_w_down, v_norm_mix_pre, v_norm_mix_post, v_norm_mlp_pre, v_norm_mlp_post, v_w_in, v_sinks, v_lam_re, v_lam_im, v_log_dt, v_b_re, v_b_im, v_c_re, v_c_im, v_d_skip, v_w_glu, v_w_branch, v_w_out, v_w_up, v_w_down):
    w = dict(norm_mix_pre=norm_mix_pre, norm_mix_post=norm_mix_post, norm_mlp_pre=norm_mlp_pre, norm_mlp_post=norm_mlp_post,
             w_in=w_in, sinks=sinks, lam_re=lam_re, lam_im=lam_im, log_dt=log_dt, b_re=b_re, b_im=b_im, c_re=c_re,
             c_im=c_im, d_skip=d_skip, w_glu=w_glu, w_branch=w_branch, w_out=w_out, w_up=w_up, w_down=w_down)
    m = dict(norm_mix_pre=m_norm_mix_pre, norm_mix_post=m_norm_mix_post, norm_mlp_pre=m_norm_mlp_pre,
             norm_mlp_post=m_norm_mlp_post, w_in=m_w_in, sinks=m_sinks, lam_re=m_lam_re, lam_im=m_lam_im,
             log_dt=m_log_dt, b_re=m_b_re, b_im=m_b_im, c_re=m_c_re, c_im=m_c_im, d_skip=m_d_skip, w_glu=m_w_glu,
             w_branch=m_w_branch, w_out=m_w_out, w_up=m_w_up, w_down=m_w_down)
    v = dict(norm_mix_pre=v_norm_mix_pre, norm_mix_post=v_norm_mix_post, norm_mlp_pre=v_norm_mlp_pre,
             norm_mlp_post=v_norm_mlp_post, w_in=v_w_in, sinks=v_sinks, lam_re=v_lam_re, lam_im=v_lam_im,
             log_dt=v_log_dt, b_re=v_b_re, b_im=v_b_im, c_re=v_c_re, c_im=v_c_im, d_skip=v_d_skip, w_glu=v_w_glu,
             w_branch=v_w_branch, w_out=v_w_out, w_up=v_w_up, w_down=v_w_down)
    xi, yi, ci = lax.axis_index("x"), lax.axis_index("y"), lax.axis_index("c")

    k_arr = jnp.stack([2 * xi + yi]).astype(jnp.int32)
    full = {}
    for k, g4 in zip(BIG, _gather_weights([_cast_into_slot(w[k][0], k_arr) for k in BIG])):
        if k in COL_SHARDED:
            full[k] = jnp.concatenate([g4[j] for j in range(4)], axis=1)
        else:
            full[k] = g4.reshape(4 * g4.shape[1], g4.shape[2])

    s5w = (lam_re[0], lam_im[0], log_dt[0], b_re[0], b_im[0], c_re[0], c_im[0], d_skip[0])
    loss_part, dx, small, big = _local_step(
        x[0], loss_target[0], (norm_mix_pre, norm_mix_post, norm_mlp_pre, norm_mlp_post),
        full["w_in"][:, :ZA_W], full["w_in"][:, ZA_W:], sinks, s5w, full["w_glu"],
        full["w_branch"][:Q_W], full["w_branch"][Q_W:], full["w_out"], full["w_up"], full["w_down"])

    g4s = []
    for k in BIG:
        g = big[k]
        if k in COL_SHARDED:
            g = g.reshape(g.shape[0], 4, g.shape[1] // 4).transpose(1, 0, 2)
        else:
            g = g.reshape(4, g.shape[0] // 4, g.shape[1])
        g4s.append(g)
    from_sib = _pair_exchange(g4s)
    c_arr = jnp.stack([ci]).astype(jnp.int32)
    kc_arr = jnp.stack([2 * xi + yi, ci]).astype(jnp.int32)
    g42 = [g.reshape(4, 2, g.shape[1] // 2, g.shape[2]) for g in g4s]
    psums = [_pair_sum(g, r, c_arr) for g, r in zip(g42, from_sib)]
    from_chips = _chip_exchange(psums)
    halves = [_chip_sum(g, r, q, kc_arr) for g, r, q in zip(g42, from_sib, from_chips)]
    grads = dict(zip(BIG, _pair_share(halves)))

    red = _all_reduce_small(_flat_small(small, loss_part)).reshape(-1)
    off = 0
    for k in SMALL:
        n = math.prod(w[k].shape)
        grads[k] = red[off:off + n].reshape(w[k].shape[1:])
        off += n
    loss = red[off]

    delta, new_m, new_v = {}, {}, {}
    for k in BIG:
        delta[k], new_m[k], new_v[k] = _adamw(w[k][0], grads[k], m[k][0], v[k][0], "adamw_" + k)
    zero = jnp.zeros((), F32)
    fw, fm, fv = (_flat_small({k: t[k] for k in SMALL}, zero) for t in (w, m, v))
    fg = _flat_small(grads, zero)
    sd, sm, sv = _adamw(fw, fg, fm, fv, "adamw_small", tr=fw.shape[0])
    off = 0
    for k in SMALL:
        n = math.prod(w[k].shape)
        delta[k], new_m[k], new_v[k] = (t.reshape(-1)[off:off + n].reshape(w[k].shape[1:]) for t in (sd, sm, sv))
        off += n

    lead = lambda t: t[None]
    return (loss, lead(dx), *[lead(grads[k]) for k in WEIGHTS], *[lead(delta[k]) for k in WEIGHTS],
            *[lead(new_m[k]) for k in WEIGHTS], *[lead(new_v[k]) for k in WEIGHTS])
```

```python
import functools
import math

import jax
import jax.numpy as jnp
from jax import lax
from jax.experimental import pallas as pl
from jax.experimental.pallas import tpu as pltpu

F32 = jnp.float32
BF16 = jnp.bfloat16
SDS = jax.ShapeDtypeStruct

D_MODEL = 2048
HEAD_DIM = 64
N_Q_HEADS = 16
ATT_BLOCK = 128
ROT_DIM = 16
ROPE_THETA = 500000.0
Q_W = 1024
KV_W = 128
SSM_W = 1024
SSM_G = 64
SSM_GC = 16
SSM_P = 64
N_STATE = SSM_G * SSM_P
LANES = 128
SUBLANES = 8
N_LG = N_STATE // LANES
N_JB = 8
LG_PER_JB = N_LG // N_JB
D_FF = 8192
ZA_W = Q_W + 2 * KV_W + SSM_W
EPS = 1e-6
S5_CHUNK = 512
S5_SEG = S5_CHUNK // SUBLANES
VMEM_LIMIT = 56 * 1024 * 1024
NEG = -1e30

ADAM_LR = 0.001
ADAM_B1 = 0.9
ADAM_B2 = 0.999
ADAM_EPS = 1e-08
ADAM_WD = 0.01
ADAM_STEP = 10

MESH = pl.DeviceIdType.MESH


def _cp(sem):
    return pltpu.CompilerParams(dimension_semantics=sem, vmem_limit_bytes=VMEM_LIMIT)


def _mm(a, b, *, mode, out_dtype, tm, tn, tk, name, a_fn=None, epi=None, extras=()):
    if mode == "nn":
        (M, K), (K2, N) = a.shape, b.shape
    elif mode == "nt":
        (M, K), (N, K2) = a.shape, b.shape
    else:
        (K, M), (K2, N) = a.shape, b.shape
    assert K == K2, (a.shape, b.shape, mode)
    tm, tn, tk = min(tm, M), min(tn, N), min(tk, K)
    assert M % tm == 0 and N % tn == 0 and K % tk == 0, (M, N, K, tm, tn, tk)
    nk = K // tk
    if mode == "tn":
        a_spec = pl.BlockSpec((tk, tm), lambda i, j, k: (k, i))
        ca = 0
    else:
        a_spec = pl.BlockSpec((tm, tk), lambda i, j, k: (i, k))
        ca = 1
    if mode == "nt":
        b_spec = pl.BlockSpec((tn, tk), lambda i, j, k: (j, k))
        cb = 1
    else:
        b_spec = pl.BlockSpec((tk, tn), lambda i, j, k: (k, j))
        cb = 0
    dims = (((ca,), (cb,)), ((), ()))
    ne = len(extras)

    def body(a_ref, b_ref, *rest):
        ex = rest[:ne]
        o_ref = rest[ne]
        av = a_ref[...]
        if a_fn is not None:
            av = a_fn(av.astype(F32))
        p = lax.dot_general(av.astype(BF16), b_ref[...].astype(BF16), dims, preferred_element_type=F32)

        def fin(v):
            if epi is not None:
                v = epi(v, *[e[...] for e in ex])
            o_ref[...] = v.astype(out_dtype)

        if nk == 1:
            fin(p)
        else:
            acc = rest[ne + 1]
            k = pl.program_id(2)

            @pl.when(k == 0)
            def _():
                acc[...] = p

            @pl.when(k > 0)
            def _():
                acc[...] += p

            @pl.when(k == nk - 1)
            def _():
                fin(acc[...])

    return pl.pallas_call(
        body,
        name=name,
        grid=(M // tm, N // tn, nk),
        in_specs=[a_spec, b_spec] + [pl.BlockSpec((tm, tn), lambda i, j, k: (i, j)) for _ in extras],
        out_specs=pl.BlockSpec((tm, tn), lambda i, j, k: (i, j)),
        out_shape=SDS((M, N), out_dtype),
        scratch_shapes=[pltpu.VMEM((tm, tn), F32)] if nk > 1 else [],
        compiler_params=_cp(("parallel", "parallel", "arbitrary")),
    )(a, b, *extras)


def _rowwise(fn, rows, bcasts, outs, accs, *, tr, name):
    T = rows[0][0].shape[0]
    tr = min(tr, T)
    assert T % tr == 0
    nr, nb, no, na = len(rows), len(bcasts), len(outs), len(accs)
    in_specs = [pl.BlockSpec((tr, w), functools.partial(lambda i, c: (i, c), c=cb)) for (_, w, cb) in rows]
    in_specs += [pl.BlockSpec(b.shape, lambda i: (0, 0)) for b in bcasts]
    out_shape = [SDS((T, w), dt) for (w, dt) in outs] + [SDS(s, F32) for s in accs]
    out_specs = [pl.BlockSpec((tr, w), lambda i: (i, 0)) for (w, _) in outs]
    out_specs += [pl.BlockSpec(s, lambda i: (0, 0)) for s in accs]

    def body(*refs):
        ins = [r[...] for r in refs[:nr + nb]]
        o_refs = refs[nr + nb:nr + nb + no]
        a_refs = refs[nr + nb + no:]
        ro, ao = fn(*ins)
        for r, v in zip(o_refs, ro):
            r[...] = v.astype(r.dtype)
        if na:
            @pl.when(pl.program_id(0) == 0)
            def _():
                for r in a_refs:
                    r[...] = jnp.zeros(r.shape, F32)

            for r, v in zip(a_refs, ao):
                r[...] += v

    res = pl.pallas_call(
        body,
        name=name,
        grid=(T // tr,),
        in_specs=in_specs,
        out_specs=out_specs,
        out_shape=out_shape,
        compiler_params=_cp(("arbitrary",) if na else ("parallel",)),
    )(*[r[0] for r in rows], *bcasts)
    return res


def _rms(v):
    r = lax.rsqrt(jnp.mean(v * v, axis=-1, keepdims=True) + EPS)
    return v * r, r


def _rms_bwd(dy, xn, r, g):
    dxn = dy * g
    dv = r * (dxn - xn * jnp.mean(dxn * xn, axis=-1, keepdims=True))
    return dv, jnp.sum(dy * xn, axis=0, keepdims=True)


def _sig(v):
    return 1.0 / (1.0 + jnp.exp(-v))


_GELU_C = math.sqrt(2.0 / math.pi)


def _gelu(v):
    return 0.5 * v * (1.0 + jnp.tanh(_GELU_C * (v + 0.044715 * v * v * v)))


def _gelu_grad(v):
    t = jnp.tanh(_GELU_C * (v + 0.044715 * v * v * v))
    return 0.5 * (1.0 + t) + 0.5 * v * (1.0 - t * t) * _GELU_C * (1.0 + 3.0 * 0.044715 * v * v)


def _rope(v, c, s, sign):
    w = v.shape[1]
    m = lax.broadcasted_iota(jnp.int32, v.shape, 1) % HEAD_DIM
    p = jnp.where(m < ROT_DIM // 2, -pltpu.roll(v, w - ROT_DIM // 2, 1), pltpu.roll(v, ROT_DIM // 2, 1))
    return v * c + sign * (p * s)


def _rope_tables(T):
    half = ROT_DIM // 2
    inv = ROPE_THETA ** (-jnp.arange(half, dtype=F32) * 2.0 / ROT_DIM)
    ang = jnp.arange(T).astype(F32)[:, None] * inv[None, :]
    cos, sin = jnp.cos(ang), jnp.sin(ang)
    one = jnp.ones((T, HEAD_DIM - ROT_DIM), F32)
    c64 = jnp.concatenate([cos, cos, one], axis=1)
    s64 = jnp.concatenate([sin, sin, 0.0 * one], axis=1)
    return jnp.tile(c64, (1, 2)), jnp.tile(s64, (1, 2))


def _dup_half(m, lo):
    lane = lax.broadcasted_iota(jnp.int32, m.shape, 1)
    sw = pltpu.roll(m, HEAD_DIM, 1)
    return jnp.where(lane < HEAD_DIM, m, sw) if lo else jnp.where(lane >= HEAD_DIM, m, sw)


def _attn_mask(i):
    qi = lax.broadcasted_iota(jnp.int32, (ATT_BLOCK, 2 * ATT_BLOCK), 0)
    kj = lax.broadcasted_iota(jnp.int32, (ATT_BLOCK, 2 * ATT_BLOCK), 1)
    rel = qi + ATT_BLOCK - kj
    return (rel >= 0) & (rel < ATT_BLOCK) & ((kj >= ATT_BLOCK) | (i > 0))


_NT = (((1,), (1,)), ((), ()))
_TN = (((0,), (0,)), ((), ()))


def _attn_fwd(za, cos, sin, sinks):
    T = za.shape[0]
    nb = T // ATT_BLOCK
    kvb = Q_W // (2 * KV_W)

    def body(sink_ref, q_ref, kvp_ref, kvc_ref, cc_ref, sc_ref, cp_ref, sp_ref, o_ref):
        i = pl.program_id(0)
        cc, sc, cp, sp = cc_ref[...], sc_ref[...], cp_ref[...], sp_ref[...]
        q = (_rope(q_ref[...], jnp.tile(cc, (1, 8)), jnp.tile(sc, (1, 8)), 1.0) * 0.125).astype(BF16)
        kvp, kvc = kvp_ref[...], kvc_ref[...]
        k = jnp.concatenate([_rope(kvp[:, :KV_W], cp, sp, 1.0), _rope(kvc[:, :KV_W], cc, sc, 1.0)], axis=0).astype(BF16)
        v = jnp.concatenate([kvp[:, KV_W:], kvc[:, KV_W:]], axis=0).astype(BF16)
        ok = _attn_mask(i)
        lane = lax.broadcasted_iota(jnp.int32, (ATT_BLOCK, LANES), 1)
        for kvh in range(2):
            k2 = _dup_half(k, kvh == 0)
            v2 = _dup_half(v, kvh == 0)
            for pair in range(4):
                c0 = (kvh * 4 + pair) * LANES
                q2 = q[:, c0:c0 + LANES]
                halves = []
                for hf in range(2):
                    sink = sink_ref[0, 2 * (kvh * 4 + pair) + hf]
                    qm = jnp.where((lane < HEAD_DIM) == (hf == 0), q2, jnp.zeros_like(q2))
                    s = lax.dot_general(qm, k2, _NT, preferred_element_type=F32)
                    s = jnp.where(ok, s, NEG)
                    m = jnp.maximum(jnp.max(s, axis=1, keepdims=True), sink)
                    e = jnp.exp(s - m)
                    den = jnp.sum(e, axis=1, keepdims=True) + jnp.exp(sink - m)
                    p = (e * (1.0 / den)).astype(BF16)
                    halves.append(jnp.dot(p, v2, preferred_element_type=F32))
                o_ref[:, c0:c0 + LANES] = jnp.where(lane < HEAD_DIM, halves[0], halves[1]).astype(BF16)

    blk = lambda w, f: pl.BlockSpec((ATT_BLOCK, w), f)
    return pl.pallas_call(
        body,
        name="attn_fwd",
        grid=(nb,),
        in_specs=[
            pl.BlockSpec(memory_space=pltpu.SMEM),
            blk(Q_W, lambda i: (i, 0)),
            blk(2 * KV_W, lambda i: (jnp.maximum(i - 1, 0), kvb)),
            blk(2 * KV_W, lambda i: (i, kvb)),
            blk(LANES, lambda i: (i, 0)),
            blk(LANES, lambda i: (i, 0)),
            blk(LANES, lambda i: (jnp.maximum(i - 1, 0), 0)),
            blk(LANES, lambda i: (jnp.maximum(i - 1, 0), 0)),
        ],
        out_specs=blk(Q_W, lambda i: (i, 0)),
        out_shape=SDS((T, Q_W), BF16),
        compiler_params=_cp(("parallel",)),
    )(sinks, za, za, za, cos, sin, cos, sin)


def _attn_bwd(za, cos, sin, sinks, o, do):
    T = za.shape[0]
    nb = T // ATT_BLOCK
    kvb = Q_W // (2 * KV_W)

    def body(sink_ref, q_ref, kvp_ref, kvc_ref, cc_ref, sc_ref, cp_ref, sp_ref, o_ref, do_ref,
             dq_ref, dkv_ref, dsk_ref, carry, dqs):
        i = pl.program_id(0)

        @pl.when(i == 0)
        def _():
            carry[...] = jnp.zeros(carry.shape, F32)
            dsk_ref[...] = jnp.zeros(dsk_ref.shape, F32)

        @pl.when(i < nb)
        def _():
            cc, sc, cp, sp = cc_ref[...], sc_ref[...], cp_ref[...], sp_ref[...]
            ccq, scq = jnp.tile(cc, (1, 8)), jnp.tile(sc, (1, 8))
            q = (_rope(q_ref[...], ccq, scq, 1.0) * 0.125).astype(BF16)
            kvp, kvc = kvp_ref[...], kvc_ref[...]
            k = jnp.concatenate([_rope(kvp[:, :KV_W], cp, sp, 1.0), _rope(kvc[:, :KV_W], cc, sc, 1.0)], axis=0).astype(BF16)
            v = jnp.concatenate([kvp[:, KV_W:], kvc[:, KV_W:]], axis=0).astype(BF16)
            ok = _attn_mask(i)
            lane = lax.broadcasted_iota(jnp.int32, (ATT_BLOCK, LANES), 1)
            lane_kv = lax.broadcasted_iota(jnp.int32, (2 * ATT_BLOCK, LANES), 1)
            lane_s = lax.broadcasted_iota(jnp.int32, (1, LANES), 1)
            dsk = jnp.zeros((1, LANES), F32)
            dk_h, dv_h = [], []
            for kvh in range(2):
                k2 = _dup_half(k, kvh == 0)
                v2 = _dup_half(v, kvh == 0)
                dk2 = jnp.zeros((2 * ATT_BLOCK, LANES), F32)
                dv2 = jnp.zeros((2 * ATT_BLOCK, LANES), F32)
                for pair in range(4):
                    c0 = (kvh * 4 + pair) * LANES
                    q2 = q[:, c0:c0 + LANES]
                    do2 = do_ref[:, c0:c0 + LANES]
                    prod = do2.astype(F32) * o_ref[:, c0:c0 + LANES].astype(F32)
                    dqh = []
                    for hf in range(2):
                        h = 2 * (kvh * 4 + pair) + hf
                        sink = sink_ref[0, h]
                        sel = (lane < HEAD_DIM) == (hf == 0)
                        qm = jnp.where(sel, q2, jnp.zeros_like(q2))
                        dom = jnp.where(sel, do2, jnp.zeros_like(do2))
                        delta = jnp.sum(jnp.where(sel, prod, 0.0), axis=1, keepdims=True)
                        s = lax.dot_general(qm, k2, _NT, preferred_element_type=F32)
                        s = jnp.where(ok, s, NEG)
                        m = jnp.maximum(jnp.max(s, axis=1, keepdims=True), sink)
                        e = jnp.exp(s - m)
                        inv = 1.0 / (jnp.sum(e, axis=1, keepdims=True) + jnp.exp(sink - m))
                        p = e * inv
                        dsk = dsk + jnp.where(lane_s == h, -jnp.sum(jnp.exp(sink - m) * inv * delta), 0.0)
                        dp = lax.dot_general(dom, v2, _NT, preferred_element_type=F32)
                        ds = (p * (dp - delta)).astype(BF16)
                        dqh.append(jnp.dot(ds, k2, preferred_element_type=F32))
                        dk2 = dk2 + lax.dot_general(ds, qm, _TN, preferred_element_type=F32)
                        dv2 = dv2 + lax.dot_general(p.astype(BF16), dom, _TN, preferred_element_type=F32)
                    dqs[:, c0:c0 + LANES] = jnp.where(lane < HEAD_DIM, dqh[0], dqh[1]) * 0.125
                dk_h.append(dk2 + pltpu.roll(dk2, HEAD_DIM, 1))
                dv_h.append(dv2 + pltpu.roll(dv2, HEAD_DIM, 1))
            dk = jnp.where(lane_kv < HEAD_DIM, dk_h[0], dk_h[1])
            dv = jnp.where(lane_kv < HEAD_DIM, dv_h[0], dv_h[1])
            dq_ref[...] = _rope(dqs[...], ccq, scq, -1.0).astype(dq_ref.dtype)
            dkp = _rope(dk[:ATT_BLOCK], cp, sp, -1.0)
            dkc = _rope(dk[ATT_BLOCK:], cc, sc, -1.0)
            dkv_ref[...] = (carry[...] + jnp.concatenate([dkp, dv[:ATT_BLOCK]], axis=1)).astype(dkv_ref.dtype)
            carry[...] = jnp.concatenate([dkc, dv[ATT_BLOCK:]], axis=1)
            dsk_ref[...] += dsk

        @pl.when(i == nb)
        def _():
            dkv_ref[...] = carry[...].astype(dkv_ref.dtype)

    blk = lambda w, f: pl.BlockSpec((ATT_BLOCK, w), f)
    cur = lambda i: jnp.minimum(i, nb - 1)
    prv = lambda i: jnp.maximum(jnp.minimum(i, nb - 1) - 1, 0)
    return pl.pallas_call(
        body,
        name="attn_bwd",
        grid=(nb + 1,),
        in_specs=[
            pl.BlockSpec(memory_space=pltpu.SMEM),
            blk(Q_W, lambda i: (cur(i), 0)),
            blk(2 * KV_W, lambda i: (prv(i), kvb)),
            blk(2 * KV_W, lambda i: (cur(i), kvb)),
            blk(LANES, lambda i: (cur(i), 0)),
            blk(LANES, lambda i: (cur(i), 0)),
            blk(LANES, lambda i: (prv(i), 0)),
            blk(LANES, lambda i: (prv(i), 0)),
            blk(Q_W, lambda i: (cur(i), 0)),
            blk(Q_W, lambda i: (cur(i), 0)),
        ],
        out_specs=[
            blk(Q_W, lambda i: (cur(i), 0)),
            blk(2 * KV_W, lambda i: (jnp.maximum(i - 1, 0), 0)),
            pl.BlockSpec((1, LANES), lambda i: (0, 0)),
        ],
        out_shape=[SDS((T, Q_W), BF16), SDS((T, 2 * KV_W), BF16), SDS((1, LANES), F32)],
        scratch_shapes=[pltpu.VMEM((ATT_BLOCK, 2 * KV_W), F32), pltpu.VMEM((ATT_BLOCK, Q_W), F32)],
        compiler_params=_cp(("arbitrary",)),
    )(sinks, za, za, za, cos, sin, cos, sin, o, do)


def _s5_discretize(lam_re, lam_im, log_dt, b_re, b_im):
    dt = jnp.exp(log_dt)[:, None]
    mag = jnp.exp(lam_re * dt)
    a_re, a_im = mag * jnp.cos(lam_im * dt), mag * jnp.sin(lam_im * dt)
    den = lam_re * lam_re + lam_im * lam_im
    nr, ni = a_re - 1.0, a_im
    coef_re = (nr * lam_re + ni * lam_im) / den
    coef_im = (ni * lam_re - nr * lam_im) / den
    bb_re = coef_re[..., None] * b_re - coef_im[..., None] * b_im
    bb_im = coef_re[..., None] * b_im + coef_im[..., None] * b_re
    return a_re, a_im, bb_re, bb_im


def _blockdiag_in(bb):
    x = bb.reshape(N_JB, 8, SSM_P, SSM_GC).transpose(0, 1, 3, 2)
    return (x[:, :, :, None, :] * jnp.eye(8, dtype=bb.dtype)[None, :, None, :, None]).reshape(N_JB, 128, 512)


def _blockdiag_in_extract(m):
    x = m.reshape(N_JB, 8, SSM_GC, 8, SSM_P)
    x = jnp.einsum('jgchp,gh->jgcp', x, jnp.eye(8, dtype=m.dtype))
    return x.transpose(0, 1, 3, 2).reshape(SSM_G, SSM_P, SSM_GC)


def _blockdiag_out(c):
    x = c.reshape(N_JB, 8, SSM_GC, SSM_P).transpose(0, 1, 3, 2)
    return (x[:, :, :, None, :] * jnp.eye(8, dtype=c.dtype)[None, :, None, :, None]).reshape(N_JB, 512, 128)


def _blockdiag_out_extract(m):
    x = m.reshape(N_JB, 8, SSM_P, 8, SSM_GC)
    x = jnp.einsum('jgphc,gh->jgpc', x, jnp.eye(8, dtype=m.dtype))
    return x.transpose(0, 1, 3, 2).reshape(SSM_G, SSM_GC, SSM_P)


def _s5_tables(a_re, a_im):
    ar, ai = a_re.reshape(N_LG, 1, LANES), a_im.reshape(N_LG, 1, LANES)
    pr, pi = [ar], [ai]
    for _ in range(S5_SEG - 1):
        pr, pi = pr + [pr[-1] * ar - pi[-1] * ai], pi + [pr[-1] * ai + pi[-1] * ar]
    p_re, p_im = jnp.concatenate(pr, axis=1), jnp.concatenate(pi, axis=1)
    bc = lambda v: jnp.broadcast_to(v, (N_LG, SUBLANES, LANES))
    return p_re, p_im, bc(ar), bc(ai)


def _s5_to_time_major(src_ref, dst_ref):
    for t in range(S5_SEG):
        dst_ref[t * SUBLANES:(t + 1) * SUBLANES, :] = src_ref[pl.ds(t, SUBLANES, stride=S5_SEG), :]


def _s5_from_time_major(val, dst_ref):
    for t in range(S5_SEG):
        dst_ref[pl.ds(t, SUBLANES, stride=S5_SEG), :] = val[t * SUBLANES:(t + 1) * SUBLANES, :]


def _tm_rows(t, row0=0):
    return pl.ds(pl.multiple_of(t * SUBLANES + row0, SUBLANES), SUBLANES)


def _s5_scan(src_re, src_im, dst_re, dst_im, ar, ai, reverse, dst_row0=0):
    def step(n, carry):
        t = (S5_SEG - 1 - n) if reverse else n
        out = []
        for ll in range(LG_PER_JB):
            xr, xi = carry[2 * ll], carry[2 * ll + 1]
            idx = (ll, _tm_rows(t), slice(None))
            odx = (ll, _tm_rows(t, dst_row0), slice(None))
            nr = ar[ll] * xr - ai[ll] * xi + src_re[idx]
            ni = ar[ll] * xi + ai[ll] * xr + src_im[idx]
            dst_re[odx] = nr
            dst_im[odx] = ni
            out += [nr, ni]
        return tuple(out)
    z = jnp.zeros((SUBLANES, LANES), F32)
    return lax.fori_loop(0, S5_SEG, step, (z,) * (2 * LG_PER_JB))


def _s5_fixup(ends, in_re, in_im, mr, mi, s_re, s_im, reverse):
    cr, ci = in_re, in_im
    order = range(SUBLANES - 1, -1, -1) if reverse else range(SUBLANES)
    for s in order:
        s_re[:, s:s + 1, :] = cr
        s_im[:, s:s + 1, :] = ci
        er = jnp.stack([ends[2 * ll][s:s + 1, :] for ll in range(LG_PER_JB)])
        ei = jnp.stack([ends[2 * ll + 1][s:s + 1, :] for ll in range(LG_PER_JB)])
        cr, ci = mr * cr - mi * ci + er, mr * ci + mi * cr + ei
    return cr, ci


def _s5_correct(x_re, x_im, s_re, s_im, p_re, p_im, row0=0):
    sr = [s_re[ll] for ll in range(LG_PER_JB)]
    si = [s_im[ll] for ll in range(LG_PER_JB)]

    def step(t, carry):
        for ll in range(LG_PER_JB):
            idx = (ll, _tm_rows(t, row0), slice(None))
            pr, pi = p_re[ll, pl.ds(t, 1), :], p_im[ll, pl.ds(t, 1), :]
            x_re[idx] = x_re[idx] + (pr * sr[ll] - pi * si[ll])
            x_im[idx] = x_im[idx] + (pr * si[ll] + pi * sr[ll])
        return carry
    lax.fori_loop(0, S5_SEG, step, 0)


def _s5_specs(nc, rev):
    cidx = (lambda c: nc - 1 - c) if rev else (lambda c: c)
    jb = lambda shape: pl.BlockSpec(shape, lambda j, c: (j, 0, 0))
    return cidx, [
        jb((1, LANES, 8 * LANES)),
        jb((1, 8 * LANES, LANES)),
        pl.BlockSpec((1, LANES), lambda j, c: (0, j)),
        jb((LG_PER_JB, SUBLANES, LANES)), jb((LG_PER_JB, SUBLANES, LANES)),
        jb((LG_PER_JB, 1, LANES)), jb((LG_PER_JB, 1, LANES)),
        jb((LG_PER_JB, S5_SEG, LANES)), jb((LG_PER_JB, S5_SEG, LANES)),
    ]


def _s5_fwd(za, prm):
    T = za.shape[0]
    R = S5_CHUNK
    nc = T // R
    ub = (Q_W + 2 * KV_W) // LANES
    _, pspecs = _s5_specs(nc, False)

    def body(u_ref, b_ref, c_ref, d_ref, are_ref, aim_ref, alr_ref, ali_ref, pr_ref, pi_ref,
             yg_ref, x0r_ref, x0i_ref, bur, bui, xsr, xsi, sr, si, xcr, xci, utm, ynat):
        c = pl.program_id(1)

        @pl.when(c == 0)
        def _():
            xcr[...] = jnp.zeros(xcr.shape, F32)
            xci[...] = jnp.zeros(xci.shape, F32)

        _s5_to_time_major(u_ref, utm)
        u = utm[...]
        ub16 = u.astype(BF16)
        bu = jnp.dot(ub16, b_ref[0].astype(BF16), preferred_element_type=F32)
        for ll in range(LG_PER_JB):
            bur[ll] = bu[:, ll * LANES:(ll + 1) * LANES]
            bui[ll] = bu[:, (LG_PER_JB + ll) * LANES:(LG_PER_JB + ll + 1) * LANES]
        ar = [are_ref[ll] for ll in range(LG_PER_JB)]
        ai = [aim_ref[ll] for ll in range(LG_PER_JB)]
        ends = _s5_scan(bur, bui, xsr, xsi, ar, ai, False)
        in_r, in_i = xcr[...], xci[...]
        x0r_ref[0] = in_r
        x0i_ref[0] = in_i
        out_r, out_i = _s5_fixup(ends, in_r, in_i, alr_ref[...], ali_ref[...], sr, si, False)
        xcr[...] = out_r
        xci[...] = out_i
        _s5_correct(xsr, xsi, sr, si, pr_ref, pi_ref)
        xcat = jnp.concatenate([xsr[ll].astype(BF16) for ll in range(LG_PER_JB)]
                               + [xsi[ll].astype(BF16) for ll in range(LG_PER_JB)], axis=1)
        y = d_ref[...] * u + jnp.dot(xcat, c_ref[0].astype(BF16), preferred_element_type=F32)
        _s5_from_time_major(_gelu(y), ynat)
        yg_ref[...] = ynat[...].astype(BF16)

    st = pl.BlockSpec((1, LG_PER_JB, 1, LANES), lambda j, c: (c, j, 0, 0))
    vm = lambda rows: pltpu.VMEM((LG_PER_JB, rows, LANES), F32)
    return pl.pallas_call(
        body,
        name="s5_fwd",
        grid=(N_JB, nc),
        in_specs=[pl.BlockSpec((R, LANES), lambda j, c: (c, ub + j))] + pspecs,
        out_specs=[pl.BlockSpec((R, LANES), lambda j, c: (c, j)), st, st],
        out_shape=[SDS((T, SSM_W), BF16), SDS((nc, N_LG, 1, LANES), F32), SDS((nc, N_LG, 1, LANES), F32)],
        scratch_shapes=[vm(R), vm(R), vm(R), vm(R), vm(SUBLANES), vm(SUBLANES), vm(1), vm(1),
                        pltpu.VMEM((R, LANES), F32), pltpu.VMEM((R, LANES), F32)],
        compiler_params=_cp(("parallel", "arbitrary")),
    )(za, *prm)


def _s5_bwd(za, dyg, x0r, x0i, prm, prev_tables):
    T = za.shape[0]
    R = S5_CHUNK
    nc = T // R
    ub = (Q_W + 2 * KV_W) // LANES
    cidx, pspecs = _s5_specs(nc, True)
    PAD = SUBLANES

    def body(u_ref, dyg_ref, x0r_ref, x0i_ref, b_ref, c_ref, d_ref, are_ref, aim_ref,
             alr_ref, ali_ref, pr_ref, pi_ref, qr_ref, qi_ref,
             du_ref, dar_ref, dai_ref, db_ref, dc_ref, dd_ref,
             bur, bui, xsr, xsi, sr, si, gcr, gci, utm, dtm, dunat):
        c = pl.program_id(1)

        @pl.when(c == 0)
        def _():
            gcr[...] = jnp.zeros(gcr.shape, F32)
            gci[...] = jnp.zeros(gci.shape, F32)
            dar_ref[...] = jnp.zeros(dar_ref.shape, F32)
            dai_ref[...] = jnp.zeros(dai_ref.shape, F32)
            db_ref[...] = jnp.zeros(db_ref.shape, F32)
            dc_ref[...] = jnp.zeros(dc_ref.shape, F32)
            dd_ref[...] = jnp.zeros(dd_ref.shape, F32)

        _s5_to_time_major(u_ref, utm)
        _s5_to_time_major(dyg_ref, dtm)
        u = utm[...]
        ub16 = u.astype(BF16)
        bcat, ccat = b_ref[0].astype(BF16), c_ref[0].astype(BF16)
        lanes = lambda v, ll: v[:, ll * LANES:(ll + 1) * LANES]
        bu = jnp.dot(ub16, bcat, preferred_element_type=F32)
        for ll in range(LG_PER_JB):
            bur[ll] = lanes(bu, ll)
            bui[ll] = lanes(bu, LG_PER_JB + ll)
        ar = [are_ref[ll] for ll in range(LG_PER_JB)]
        ai = [aim_ref[ll] for ll in range(LG_PER_JB)]
        ends = _s5_scan(bur, bui, xsr, xsi, ar, ai, False, dst_row0=PAD)
        in_r, in_i = x0r_ref[0], x0i_ref[0]
        _s5_fixup(ends, in_r, in_i, alr_ref[...], ali_ref[...], sr, si, False)
        _s5_correct(xsr, xsi, sr, si, pr_ref, pi_ref, PAD)
        xsr[:, 0:PAD, :] = sr[...]
        xsi[:, 0:PAD, :] = si[...]
        xcat = jnp.concatenate([xsr[ll, PAD:, :].astype(BF16) for ll in range(LG_PER_JB)]
                               + [xsi[ll, PAD:, :].astype(BF16) for ll in range(LG_PER_JB)], axis=1)
        y = d_ref[...] * u + jnp.dot(xcat, ccat, preferred_element_type=F32)
        dy = dtm[...] * _gelu_grad(y)
        dyb = dy.astype(BF16)
        dd_ref[...] += jnp.sum(dy * u, axis=0, keepdims=True)
        du = d_ref[...] * dy
        dc_ref[0] += lax.dot_general(xcat, dyb, _TN, preferred_element_type=F32)
        g = lax.dot_general(dyb, ccat, _NT, preferred_element_type=F32)
        for ll in range(LG_PER_JB):
            bur[ll] = lanes(g, ll)
            bui[ll] = lanes(g, LG_PER_JB + ll)
        ends = _s5_scan(bur, bui, bur, bui, ar, [-v for v in ai], True)
        out_r, out_i = _s5_fixup(ends, gcr[...], gci[...], alr_ref[...], -ali_ref[...], sr, si, True)
        gcr[...] = out_r
        gci[...] = out_i
        _s5_correct(bur, bui, sr, si, qr_ref, qi_ref)
        for ll in range(LG_PER_JB):
            gr, gi = bur[ll], bui[ll]
            xpr, xpi = xsr[ll, 0:R, :], xsi[ll, 0:R, :]
            red = lambda v: v.reshape(R // SUBLANES, SUBLANES, LANES).sum(axis=0)
            dar_ref[ll] += red(xpr * gr + xpi * gi)
            dai_ref[ll] += red(xpr * gi - xpi * gr)
        gcat = jnp.concatenate([bur[ll].astype(BF16) for ll in range(LG_PER_JB)]
                               + [bui[ll].astype(BF16) for ll in range(LG_PER_JB)], axis=1)
        db_ref[0] += lax.dot_general(ub16, gcat, _TN, preferred_element_type=F32)
        du = du + lax.dot_general(gcat, bcat, _NT, preferred_element_type=F32)
        _s5_from_time_major(du, dunat)
        du_ref[...] = dunat[...].astype(du_ref.dtype)

    st = pl.BlockSpec((1, LG_PER_JB, 1, LANES), lambda j, c: (cidx(c), j, 0, 0))
    jb = lambda shape: pl.BlockSpec(shape, lambda j, c: (j, 0, 0))
    vm = lambda rows: pltpu.VMEM((LG_PER_JB, rows, LANES), F32)
    return pl.pallas_call(
        body,
        name="s5_bwd",
        grid=(N_JB, nc),
        in_specs=[pl.BlockSpec((R, LANES), lambda j, c: (cidx(c), ub + j)),
                  pl.BlockSpec((R, LANES), lambda j, c: (cidx(c), j)), st, st] + pspecs
                 + [jb((LG_PER_JB, S5_SEG, LANES)), jb((LG_PER_JB, S5_SEG, LANES))],
        out_specs=[pl.BlockSpec((R, LANES), lambda j, c: (cidx(c), j)),
                   jb((LG_PER_JB, SUBLANES, LANES)), jb((LG_PER_JB, SUBLANES, LANES)),
                   jb((1, LANES, 8 * LANES)), jb((1, 8 * LANES, LANES)),
                   pl.BlockSpec((1, LANES), lambda j, c: (0, j))],
        out_shape=[SDS((T, SSM_W), BF16), SDS((N_LG, SUBLANES, LANES), F32), SDS((N_LG, SUBLANES, LANES), F32),
                   SDS((N_JB, LANES, 8 * LANES), F32), SDS((N_JB, 8 * LANES, LANES), F32), SDS((1, SSM_W), F32)],
        scratch_shapes=[vm(R), vm(R), vm(R + PAD), vm(R + PAD), vm(SUBLANES), vm(SUBLANES), vm(1), vm(1)]
                       + [pltpu.VMEM((R, LANES), F32)] * 3,
        compiler_params=_cp(("parallel", "arbitrary")),
    )(za, dyg, x0r, x0i, *prm, *prev_tables)


def _local_step(x, target, gains, w_a, w_g, sinks, s5w, w_glu, w_ba, w_bs, w_out, w_up, w_down):
    T = x.shape[0]
    D = D_MODEL
    g1, g2, g3, g4 = gains
    cos, sin = _rope_tables(T)
    lam_re, lam_im, log_dt, b_re, b_im, c_re, c_im, d_skip = s5w
    (a_re, a_im, bb_re, bb_im), disc_vjp = jax.vjp(_s5_discretize, lam_re, lam_im, log_dt, b_re, b_im)
    p_re, p_im, abr, abi = _s5_tables(a_re, a_im)
    prm = (jnp.concatenate([_blockdiag_in(bb_re), _blockdiag_in(bb_im)], axis=2),
           jnp.concatenate([_blockdiag_out(c_re), -_blockdiag_out(c_im)], axis=1),
           d_skip.reshape(1, SSM_W), abr, abi, p_re[:, S5_SEG - 1:, :], p_im[:, S5_SEG - 1:, :], p_re, p_im)
    rev_tables = (p_re[:, ::-1, :], -p_im[:, ::-1, :])
    mm = functools.partial(_mm, tm=512, tn=1024, tk=2048)

    h = _rowwise(lambda xv, g: ((_rms(xv)[0] * g,), ()), [(x, D, 0)], [g1], [(D, BF16)], [], tr=512, name="norm1")[0]
    za = _mm(h, w_a, mode="nn", out_dtype=F32, tm=512, tn=1152, tk=2048, name="mm_za")
    zg = mm(h, w_g, mode="nn", out_dtype=F32, name="mm_zg")
    o_attn = _attn_fwd(za, cos, sin, sinks)
    yg, x0r, x0i = _s5_fwd(za, prm)
    zglu = mm(yg, w_glu, mode="nn", out_dtype=F32, name="mm_glu")
    o_ssm = _rowwise(lambda z1, z2: ((z1 * _sig(z2),), ()), [(zglu, SSM_W, 0), (zglu, SSM_W, 1)], [],
                     [(SSM_W, BF16)], [], tr=512, name="glu")[0]
    ya = mm(o_attn, w_ba, mode="nn", out_dtype=F32, name="mm_ya")
    ys = mm(o_ssm, w_bs, mode="nn", out_dtype=F32, name="mm_ys")
    mi = _rowwise(lambda ga, gs, a, s: ((_sig(ga) * a + _sig(gs) * s,), ()),
                  [(zg, D, 0), (zg, D, 1), (ya, D, 0), (ys, D, 0)], [], [(D, BF16)], [], tr=256, name="gate")[0]
    mixed = mm(mi, w_out, mode="nn", out_dtype=F32, name="mm_out")

    def f_post(xv, mv, g2v, g3v):
        x1v = xv + _rms(mv)[0] * g2v
        return (x1v, _rms(x1v)[0] * g3v), ()
    x1, h2 = _rowwise(f_post, [(x, D, 0), (mixed, D, 0)], [g2, g3], [(D, F32), (D, BF16)], [], tr=256, name="post_mix")
    act = mm(h2, w_up, mode="nn", out_dtype=BF16, name="mm_up", epi=lambda v: jnp.maximum(v, 0.0))
    f = _mm(act, w_down, mode="nn", out_dtype=F32, tm=512, tn=2048, tk=2048, name="mm_down", a_fn=lambda v: v * v)

    def f_final(x1v, fv, tv, g4v):
        fn, r = _rms(fv)
        e = x1v + fn * g4v - tv
        dx2v = e * (1.0 / D)
        dfv, dg4v = _rms_bwd(dx2v, fn, r, g4v)
        return (dfv, dx2v), (dg4v, jnp.zeros((SUBLANES, LANES), F32) + 0.5 * jnp.sum(e * e) * (1.0 / D))
    df, dx2, dg4, lossb = _rowwise(f_final, [(x1, D, 0), (f, D, 0), (target, D, 0)], [g4],
                                   [(D, BF16), (D, F32)], [(1, D), (SUBLANES, LANES)], tr=256, name="final")

    dpre = mm(df, w_down, mode="nt", out_dtype=BF16, name="mm_dact", epi=lambda v, a: v * (2.0 * a.astype(F32)), extras=(act,))
    wg = functools.partial(_mm, mode="tn", out_dtype=F32, tm=1024, tn=2048, tk=512)
    d_w_down = wg(act, df, name="wg_down", a_fn=lambda v: v * v)
    dh2 = _mm(dpre, w_up, mode="nt", out_dtype=F32, tm=512, tn=2048, tk=2048, name="mm_dh2")
    d_w_up = wg(h2, dpre, name="wg_up")

    def f_mid(dx2v, dh2v, x1v, mv, g2v, g3v):
        x1n, r3 = _rms(x1v)
        d3, dg3v = _rms_bwd(dh2v, x1n, r3, g3v)
        dx1v = dx2v + d3
        mn, r2 = _rms(mv)
        dmv, dg2v = _rms_bwd(dx1v, mn, r2, g2v)
        return (dx1v, dmv), (dg3v, dg2v)
    dx1, dmixed, dg3, dg2 = _rowwise(f_mid, [(dx2, D, 0), (dh2, D, 0), (x1, D, 0), (mixed, D, 0)], [g2, g3],
                                     [(D, F32), (D, BF16)], [(1, D), (1, D)], tr=256, name="mid")

    dmi = mm(dmixed, w_out, mode="nt", out_dtype=F32, name="mm_dmi")
    d_w_out = wg(mi, dmixed, name="wg_out")

    def f_gate(dv, ga, gs, a, s):
        sa, ss = _sig(ga), _sig(gs)
        return (dv * sa, dv * ss, jnp.concatenate([dv * a * sa * (1.0 - sa), dv * s * ss * (1.0 - ss)], axis=1)), ()
    dya, dys, dzg = _rowwise(f_gate, [(dmi, D, 0), (zg, D, 0), (zg, D, 1), (ya, D, 0), (ys, D, 0)], [],
                             [(D, BF16), (D, BF16), (2 * D, BF16)], [], tr=256, name="gate_bwd")
    do_attn = mm(dya, w_ba, mode="nt", out_dtype=BF16, name="mm_doa")
    d_w_ba = wg(o_attn, dya, name="wg_ba")
    do_ssm = mm(dys, w_bs, mode="nt", out_dtype=F32, name="mm_dos")
    d_w_bs = wg(o_ssm, dys, name="wg_bs")

    def f_glu(dv, z1, z2):
        s2 = _sig(z2)
        return (jnp.concatenate([dv * s2, dv * z1 * s2 * (1.0 - s2)], axis=1),), ()
    dzglu = _rowwise(f_glu, [(do_ssm, SSM_W, 0), (zglu, SSM_W, 0), (zglu, SSM_W, 1)], [], [(2 * SSM_W, BF16)], [],
                     tr=512, name="glu_bwd")[0]
    dyg = mm(dzglu, w_glu, mode="nt", out_dtype=F32, name="mm_dyg")
    d_w_glu = wg(yg, dzglu, name="wg_glu")
    du, dar, dai, dbc, dcc, ddv = _s5_bwd(za, dyg, x0r, x0i, prm, rev_tables)
    dbr, dbi = dbc[:, :, :4 * LANES], dbc[:, :, 4 * LANES:]
    dcr, dci = dcc[:, :4 * LANES, :], -dcc[:, 4 * LANES:, :]
    dq, dkv, dsk = _attn_bwd(za, cos, sin, sinks, o_attn, do_attn)
    dza = jnp.concatenate([dq, dkv, du], axis=1)
    dh = mm(dza, w_a, mode="nt", out_dtype=F32, name="mm_dh_a", tk=ZA_W)
    dh = _mm(dzg, w_g, mode="nt", out_dtype=F32, tm=512, tn=1024, tk=2048, name="mm_dh_g", epi=lambda v, p: v + p, extras=(dh,))
    d_w_a = _mm(h, dza, mode="tn", out_dtype=F32, tm=1024, tn=ZA_W, tk=512, name="wg_a")
    d_w_g = wg(h, dzg, name="wg_g")

    def f_first(dx1v, dhv, xv, g1v):
        xn, r1 = _rms(xv)
        d1, dg1v = _rms_bwd(dhv, xn, r1, g1v)
        return (dx1v + d1,), (dg1v,)
    dx, dg1 = _rowwise(f_first, [(dx1, D, 0), (dh, D, 0), (x, D, 0)], [g1], [(D, F32)], [(1, D)], tr=256, name="first")

    da_re = dar.sum(axis=1).reshape(SSM_G, SSM_P)
    da_im = dai.sum(axis=1).reshape(SSM_G, SSM_P)
    d_lam_re, d_lam_im, d_log_dt, d_b_re, d_b_im = disc_vjp(
        (da_re, da_im, _blockdiag_in_extract(dbr), _blockdiag_in_extract(dbi)))
    small = dict(norm_mix_pre=dg1, norm_mix_post=dg2, norm_mlp_pre=dg3, norm_mlp_post=dg4,
                 sinks=dsk[:, :N_Q_HEADS], lam_re=d_lam_re, lam_im=d_lam_im, log_dt=d_log_dt,
                 b_re=d_b_re, b_im=d_b_im, c_re=_blockdiag_out_extract(dcr), c_im=_blockdiag_out_extract(dci),
                 d_skip=ddv.reshape(SSM_G, SSM_GC))
    big = dict(w_in=jnp.concatenate([d_w_a, d_w_g], axis=1), w_glu=d_w_glu,
               w_branch=jnp.concatenate([d_w_ba, d_w_bs], axis=0), w_out=d_w_out, w_up=d_w_up, w_down=d_w_down)
    return lossb[0, 0], dx, small, big


ANY = pl.BlockSpec(memory_space=pl.ANY)


def _place():
    x, y, c = lax.axis_index("x"), lax.axis_index("y"), lax.axis_index("c")
    others = [(1 - x, y), (x, 1 - y), (1 - x, 1 - y)]
    return x, y, c, others


def _remote(src, dst, ssem, rsem, to):
    return pltpu.make_async_remote_copy(src_ref=src, dst_ref=dst, send_sem=ssem, recv_sem=rsem,
                                        device_id=to, device_id_type=MESH)


def _cast_into_slot(w, k_arr):
    rows, cols = w.shape
    tr = 256

    def body(k_ref, w_ref, o_ref):
        o_ref[0] = w_ref[...].astype(BF16)

    return pl.pallas_call(
        body,
        name="cast_into_slot",
        grid_spec=pltpu.PrefetchScalarGridSpec(
            num_scalar_prefetch=1,
            grid=(rows // tr,),
            in_specs=[pl.BlockSpec((tr, cols), lambda i, k: (i, 0))],
            out_specs=pl.BlockSpec((1, tr, cols), lambda i, k: (k[0], i, 0)),
        ),
        out_shape=SDS((4, rows, cols), BF16),
        compiler_params=_cp(("parallel",)),
    )(k_arr, w)


def _gather_weights(slotted):
    n = len(slotted)

    def body(*refs):
        ins, outs = refs[:n], refs[n:2 * n]
        ssem, rsem, fs_sem, fr_sem = refs[2 * n:]
        x, y, c, others = _place()
        me, sib = 2 * x + y, (x, y, 1 - c)

        def half(w, hc):
            hr = ins[w].shape[1] // 2
            return pl.ds(pl.multiple_of(hc * hr, 16), hr)

        sends, passes = [], []
        for w in range(n):
            for r, (ox, oy) in enumerate(others):
                cp = _remote(ins[w].at[me, half(w, c), :], outs[w].at[me, half(w, c), :],
                             ssem.at[3 * w + r], rsem.at[3 * w + r], (ox, oy, c))
                cp.start()
                sends.append(cp)
        for w in range(n):
            for r, (ox, oy) in enumerate(others):
                got = outs[w].at[2 * ox + oy, half(w, c), :]
                _remote(got, got, ssem.at[3 * w + r], rsem.at[3 * w + r], (ox, oy, c)).wait_recv()
                cp = _remote(got, got, fs_sem.at[3 * w + r], fr_sem.at[3 * w + r], sib)
                cp.start()
                passes.append(cp)
        for w in range(n):
            for r, (ox, oy) in enumerate(others):
                got = outs[w].at[2 * ox + oy, half(w, 1 - c), :]
                _remote(got, got, fs_sem.at[3 * w + r], fr_sem.at[3 * w + r], sib).wait_recv()
        for cp in sends + passes:
            cp.wait_send()

    dma = pltpu.SemaphoreType.DMA
    return pl.pallas_call(
        body,
        name="gather_weights",
        in_specs=[ANY] * n,
        out_specs=[ANY] * n,
        out_shape=[SDS(s.shape, s.dtype) for s in slotted],
        input_output_aliases={w: w for w in range(n)},
        scratch_shapes=[dma((3 * n,)), dma((3 * n,)), dma((3 * n,)), dma((3 * n,))],
    )(*slotted)


def _pair_exchange(grads):
    n = len(grads)

    def body(*refs):
        ins, outs = refs[:n], refs[n:2 * n]
        ssem, rsem = refs[2 * n:]
        x, y, c, _ = _place()
        cps = []
        for w in range(n):
            hr = ins[w].shape[1] // 2
            src = ins[w].at[:, pl.ds(pl.multiple_of((1 - c) * hr, 8), hr), :]
            cp = _remote(src, outs[w], ssem.at[w], rsem.at[w], (x, y, 1 - c))
            cp.start()
            cps.append(cp)
        for cp in cps:
            cp.wait()

    dma = pltpu.SemaphoreType.DMA
    return pl.pallas_call(
        body,
        name="pair_exchange",
        in_specs=[ANY] * n,
        out_specs=[ANY] * n,
        out_shape=[SDS((4, g.shape[1] // 2, g.shape[2]), g.dtype) for g in grads],
        scratch_shapes=[dma((n,)), dma((n,))],
    )(*grads)


def _pair_sum(g, r, c_arr):
    _, _, hr, cols = g.shape
    tr = min(256, hr)

    def body(c_ref, g_ref, r_ref, o_ref):
        o_ref[0] = (g_ref[0, 0] + r_ref[0]).astype(BF16)

    return pl.pallas_call(
        body,
        name="pair_sum",
        grid_spec=pltpu.PrefetchScalarGridSpec(
            num_scalar_prefetch=1,
            grid=(4, hr // tr),
            in_specs=[pl.BlockSpec((1, 1, tr, cols), lambda k, i, c_ref: (k, c_ref[0], i, 0)),
                      pl.BlockSpec((1, tr, cols), lambda k, i, c_ref: (k, i, 0))],
            out_specs=pl.BlockSpec((1, tr, cols), lambda k, i, c_ref: (k, i, 0)),
        ),
        out_shape=SDS((4, hr, cols), BF16),
        compiler_params=_cp(("parallel", "parallel")),
    )(c_arr, g, r)


def _chip_exchange(psums):
    n = len(psums)

    def body(*refs):
        ins, outs = refs[:n], refs[n:2 * n]
        ssem, rsem = refs[2 * n:]
        x, y, c, others = _place()
        cps = []
        for w in range(n):
            for r, (ox, oy) in enumerate(others):
                cp = _remote(ins[w].at[2 * ox + oy], outs[w].at[r], ssem.at[3 * w + r], rsem.at[3 * w + r], (ox, oy, c))
                cp.start()
                cps.append(cp)
        for cp in cps:
            cp.wait()

    dma = pltpu.SemaphoreType.DMA
    return pl.pallas_call(
        body,
        name="chip_exchange",
        in_specs=[ANY] * n,
        out_specs=[ANY] * n,
        out_shape=[SDS((3,) + p.shape[1:], p.dtype) for p in psums],
        scratch_shapes=[dma((3 * n,)), dma((3 * n,))],
    )(*psums)


def _chip_sum(g, r, q, kc_arr):
    _, _, hr, cols = g.shape
    tr = min(256, hr)

    def body(kc_ref, g_ref, r_ref, q_ref, o_ref):
        s = g_ref[0, 0] + r_ref[0]
        for j in range(3):
            s = s + q_ref[j].astype(F32)
        o_ref[...] = s

    return pl.pallas_call(
        body,
        name="chip_sum",
        grid_spec=pltpu.PrefetchScalarGridSpec(
            num_scalar_prefetch=1,
            grid=(hr // tr,),
            in_specs=[pl.BlockSpec((1, 1, tr, cols), lambda i, kc: (kc[0], kc[1], i, 0)),
                      pl.BlockSpec((1, tr, cols), lambda i, kc: (kc[0], i, 0)),
                      pl.BlockSpec((3, tr, cols), lambda i, kc: (0, i, 0))],
            out_specs=pl.BlockSpec((tr, cols), lambda i, kc: (kc[1] * (hr // tr) + i, 0)),
        ),
        out_shape=SDS((2 * hr, cols), F32),
        compiler_params=_cp(("parallel",)),
    )(kc_arr, g, r, q)


def _pair_share(blocks):
    n = len(blocks)

    def body(*refs):
        ins, outs = refs[:n], refs[n:2 * n]
        ssem, rsem = refs[2 * n:]
        x, y, c, _ = _place()
        cps = []
        for w in range(n):
            hr = ins[w].shape[0] // 2
            rows = pl.ds(pl.multiple_of(c * hr, 8), hr)
            cp = _remote(ins[w].at[rows, :], outs[w].at[rows, :], ssem.at[w], rsem.at[w], (x, y, 1 - c))
            cp.start()
            cps.append(cp)
        for w in range(n):
            hr = ins[w].shape[0] // 2
            other = outs[w].at[pl.ds(pl.multiple_of((1 - c) * hr, 8), hr), :]
            _remote(other, other, ssem.at[w], rsem.at[w], (x, y, 1 - c)).wait_recv()
        for cp in cps:
            cp.wait_send()

    dma = pltpu.SemaphoreType.DMA
    return pl.pallas_call(
        body,
        name="pair_share",
        in_specs=[ANY] * n,
        out_specs=[ANY] * n,
        out_shape=[SDS(b.shape, b.dtype) for b in blocks],
        input_output_aliases={w: w for w in range(n)},
        scratch_shapes=[dma((n,)), dma((n,))],
    )(*blocks)


def _all_reduce_small(buf):
    rows = buf.shape[0]

    def body(in_ref, o_ref, slots, ssem, rsem):
        x, y, c, _ = _place()
        me = 4 * x + 2 * y + c
        slots[me] = in_ref[...]
        cps = []
        for r in range(1, 8):
            px = 1 - x if r & 4 else x
            py = 1 - y if r & 2 else y
            pc = 1 - c if r & 1 else c
            cp = _remote(in_ref, slots.at[me], ssem.at[r - 1], rsem.at[r - 1], (px, py, pc))
            cp.start()
            cps.append((cp, 4 * px + 2 * py + pc))
        for r, (cp, peer) in enumerate(cps):
            _remote(in_ref, slots.at[peer], ssem.at[r], rsem.at[r], (x, y, c)).wait_recv()
        s = slots[0]
        for d in range(1, 8):
            s = s + slots[d]
        o_ref[...] = s
        for cp, _ in cps:
            cp.wait_send()

    dma = pltpu.SemaphoreType.DMA
    return pl.pallas_call(
        body,
        name="all_reduce_small",
        in_specs=[pl.BlockSpec(memory_space=pltpu.VMEM)],
        out_specs=pl.BlockSpec(memory_space=pltpu.VMEM),
        out_shape=SDS(buf.shape, F32),
        scratch_shapes=[pltpu.VMEM((8, rows, LANES), F32), dma((7,)), dma((7,))],
        compiler_params=pltpu.CompilerParams(vmem_limit_bytes=VMEM_LIMIT),
    )(buf)


def _adam_fn(w, g, m, v):
    m2 = ADAM_B1 * m + (1.0 - ADAM_B1) * g
    v2 = ADAM_B2 * v + (1.0 - ADAM_B2) * (g * g)
    m_hat = m2 / (1.0 - ADAM_B1 ** ADAM_STEP)
    v_hat = v2 / (1.0 - ADAM_B2 ** ADAM_STEP)
    return (-ADAM_LR * (m_hat / (jnp.sqrt(v_hat) + ADAM_EPS) + ADAM_WD * w), m2, v2), ()


def _adamw(w, g, m, v, name, tr=256):
    cols = w.shape[1]
    return _rowwise(_adam_fn, [(w, cols, 0), (g, cols, 0), (m, cols, 0), (v, cols, 0)], [],
                    [(cols, F32)] * 3, [], tr=tr, name=name)


BIG = ("w_in", "w_glu", "w_branch", "w_out", "w_up", "w_down")
COL_SHARDED = ("w_in", "w_glu", "w_up")
SMALL = ("norm_mix_pre", "norm_mix_post", "norm_mlp_pre", "norm_mlp_post", "sinks", "lam_re", "lam_im", "log_dt",
         "b_re", "b_im", "c_re", "c_im", "d_skip")
WEIGHTS = ("norm_mix_pre", "norm_mix_post", "norm_mlp_pre", "norm_mlp_post", "w_in", "sinks", "lam_re", "lam_im",
           "log_dt", "b_re", "b_im", "c_re", "c_im", "d_skip", "w_glu", "w_branch", "w_out", "w_up", "w_down")


def _flat_small(vals, extra):
    flat = jnp.concatenate([vals[k].reshape(-1) for k in SMALL] + [extra.reshape(-1)])
    rows = -(-flat.shape[0] // (SUBLANES * LANES)) * SUBLANES
    return jnp.pad(flat, (0, rows * LANES - flat.shape[0])).reshape(rows, LANES)


def kernel(x, norm_mix_pre, norm_mix_post, norm_mlp_pre, norm_mlp_post, w_in, sinks, lam_re, lam_im, log_dt, b_re, b_im, c_re, c_im, d_skip, w_glu, w_branch, w_out, w_up, w_down, loss_target, m_norm_mix_pre, m_norm_mix_post, m_norm_mlp_pre, m_norm_mlp_post, m_w_in, m_sinks, m_lam_re, m_lam_im, m_log_dt, m_b_re, m_b_im, m_c_re, m_c_im, m_d_skip, m_w_glu, m_w_branch, m_w_out, m_w_up, m_w_down, v_norm_mix_pre, v_norm_mix_post, v_norm_mlp_pre, v_norm_mlp_post, v_w_in, v_sinks, v_lam_re, v_lam_im, v_log_dt, v_b_re, v_b_im, v_c_re, v_c_im, v_d_skip, v_w_glu, v_w_branch, v_w_out, v_w_up, v_w_down):
    w = dict(norm_mix_pre=norm_mix_pre, norm_mix_post=norm_mix_post, norm_mlp_pre=norm_mlp_pre, norm_mlp_post=norm_mlp_post,
             w_in=w_in, sinks=sinks, lam_re=lam_re, lam_im=lam_im, log_dt=log_dt, b_re=b_re, b_im=b_im, c_re=c_re,
             c_im=c_im, d_skip=d_skip, w_glu=w_glu, w_branch=w_branch, w_out=w_out, w_up=w_up, w_down=w_down)
    m = dict(norm_mix_pre=m_norm_mix_pre, norm_mix_post=m_norm_mix_post, norm_mlp_pre=m_norm_mlp_pre,
             norm_mlp_post=m_norm_mlp_post, w_in=m_w_in, sinks=m_sinks, lam_re=m_lam_re, lam_im=m_lam_im,
             log_dt=m_log_dt, b_re=m_b_re, b_im=m_b_im, c_re=m_c_re, c_im=m_c_im, d_skip=m_d_skip, w_glu=m_w_glu,
             w_branch=m_w_branch, w_out=m_w_out, w_up=m_w_up, w_down=m_w_down)
    v = dict(norm_mix_pre=v_norm_mix_pre, norm_mix_post=v_norm_mix_post, norm_mlp_pre=v_norm_mlp_pre,
             norm_mlp_post=v_norm_mlp_post, w_in=v_w_in, sinks=v_sinks, lam_re=v_lam_re, lam_im=v_lam_im,
             log_dt=v_log_dt, b_re=v_b_re, b_im=v_b_im, c_re=v_c_re, c_im=v_c_im, d_skip=v_d_skip, w_glu=v_w_glu,
             w_branch=v_w_branch, w_out=v_w_out, w_up=v_w_up, w_down=v_w_down)
    xi, yi, ci = lax.axis_index("x"), lax.axis_index("y"), lax.axis_index("c")

    k_arr = jnp.stack([2 * xi + yi]).astype(jnp.int32)
    full = {}
    for k, g4 in zip(BIG, _gather_weights([_cast_into_slot(w[k][0], k_arr) for k in BIG])):
        if k in COL_SHARDED:
            full[k] = jnp.concatenate([g4[j] for j in range(4)], axis=1)
        else:
            full[k] = g4.reshape(4 * g4.shape[1], g4.shape[2])

    s5w = (lam_re[0], lam_im[0], log_dt[0], b_re[0], b_im[0], c_re[0], c_im[0], d_skip[0])
    loss_part, dx, small, big = _local_step(
        x[0], loss_target[0], (norm_mix_pre, norm_mix_post, norm_mlp_pre, norm_mlp_post),
        full["w_in"][:, :ZA_W], full["w_in"][:, ZA_W:], sinks, s5w, full["w_glu"],
        full["w_branch"][:Q_W], full["w_branch"][Q_W:], full["w_out"], full["w_up"], full["w_down"])

    g4s = []
    for k in BIG:
        g = big[k]
        if k in COL_SHARDED:
            g = g.reshape(g.shape[0], 4, g.shape[1] // 4).transpose(1, 0, 2)
        else:
            g = g.reshape(4, g.shape[0] // 4, g.shape[1])
        g4s.append(g)
    from_sib = _pair_exchange(g4s)
    c_arr = jnp.stack([ci]).astype(jnp.int32)
    kc_arr = jnp.stack([2 * xi + yi, ci]).astype(jnp.int32)
    g42 = [g.reshape(4, 2, g.shape[1] // 2, g.shape[2]) for g in g4s]
    psums = [_pair_sum(g, r, c_arr) for g, r in zip(g42, from_sib)]
    from_chips = _chip_exchange(psums)
    halves = [_chip_sum(g, r, q, kc_arr) for g, r, q in zip(g42, from_sib, from_chips)]
    grads = dict(zip(BIG, _pair_share(halves)))

    red = _all_reduce_small(_flat_small(small, loss_part)).reshape(-1)
    off = 0
    for k in SMALL:
        n = math.prod(w[k].shape)
        grads[k] = red[off:off + n].reshape(w[k].shape[1:])
        off += n
    loss = red[off]

    delta, new_m, new_v = {}, {}, {}
    for k in BIG:
        delta[k], new_m[k], new_v[k] = _adamw(w[k][0], grads[k], m[k][0], v[k][0], "adamw_" + k)
    zero = jnp.zeros((), F32)
    fw, fm, fv = (_flat_small({k: t[k] for k in SMALL}, zero) for t in (w, m, v))
    fg = _flat_small(grads, zero)
    sd, sm, sv = _adamw(fw, fg, fm, fv, "adamw_small", tr=fw.shape[0])
    off = 0
    for k in SMALL:
        n = math.prod(w[k].shape)
        delta[k], new_m[k], new_v[k] = (t.reshape(-1)[off:off + n].reshape(w[k].shape[1:]) for t in (sd, sm, sv))
        off += n

    lead = lambda t: t[None]
    return (loss, lead(dx), *[lead(grads[k]) for k in WEIGHTS], *[lead(delta[k]) for k in WEIGHTS],
            *[lead(new_m[k]) for k in WEIGHTS], *[lead(new_v[k]) for k in WEIGHTS])
```

```python
import functools
import math

import jax
import jax.numpy as jnp
from jax import lax
from jax.experimental import pallas as pl
from jax.experimental.pallas import tpu as pltpu

F32 = jnp.float32
BF16 = jnp.bfloat16
SDS = jax.ShapeDtypeStruct

D_MODEL = 2048
HEAD_DIM = 64
N_Q_HEADS = 16
ATT_BLOCK = 128
ROT_DIM = 16
ROPE_THETA = 500000.0
Q_W = 1024
KV_W = 128
SSM_W = 1024
SSM_G = 64
SSM_GC = 16
SSM_P = 64
N_STATE = SSM_G * SSM_P
LANES = 128
SUBLANES = 8
N_LG = N_STATE // LANES
N_JB = 8
LG_PER_JB = N_LG // N_JB
D_FF = 8192
ZA_W = Q_W + 2 * KV_W + SSM_W
EPS = 1e-6
S5_CHUNK = 512
S5_SEG = S5_CHUNK // SUBLANES
VMEM_LIMIT = 56 * 1024 * 1024
NEG = -1e30

ADAM_LR = 0.001
ADAM_B1 = 0.9
ADAM_B2 = 0.999
ADAM_EPS = 1e-08
ADAM_WD = 0.01
ADAM_STEP = 10

MESH = pl.DeviceIdType.MESH


def _cp(sem):
    return pltpu.CompilerParams(dimension_semantics=sem, vmem_limit_bytes=VMEM_LIMIT)


ANY = pl.BlockSpec(memory_space=pl.ANY)


def _place():
    x, y, c = lax.axis_index("x"), lax.axis_index("y"), lax.axis_index("c")
    others = [(1 - x, y), (x, 1 - y), (1 - x, 1 - y)]
    return x, y, c, others


def _remote(src, dst, ssem, rsem, to):
    return pltpu.make_async_remote_copy(src_ref=src, dst_ref=dst, send_sem=ssem, recv_sem=rsem,
                                        device_id=to, device_id_type=MESH)


class _GatherComm:
    def __init__(self, slotted):
        self.arrs = list(slotted)
        self.n = len(self.arrs)
        dma = pltpu.SemaphoreType.DMA
        self.scratch = [dma((3 * self.n,)) for _ in range(4)]
        self.out_shape = [SDS(s.shape, s.dtype) for s in self.arrs]

    @staticmethod
    def _half(ref, hc):
        hr = ref.shape[1] // 2
        return pl.ds(pl.multiple_of(hc * hr, 16), hr)

    def start(self, ins, outs, sems):
        ssem, rsem, _, _ = sems
        x, y, c, others = _place()
        me = 2 * x + y
        for w in range(self.n):
            for r, (ox, oy) in enumerate(others):
                _remote(ins[w].at[me, self._half(ins[w], c), :], outs[w].at[me, self._half(ins[w], c), :],
                        ssem.at[3 * w + r], rsem.at[3 * w + r], (ox, oy, c)).start()

    def finish(self, ins, outs, sems):
        ssem, rsem, fs_sem, fr_sem = sems
        x, y, c, others = _place()
        me, sib = 2 * x + y, (x, y, 1 - c)
        passes = []
        for w in range(self.n):
            for r, (ox, oy) in enumerate(others):
                got = outs[w].at[2 * ox + oy, self._half(ins[w], c), :]
                _remote(got, got, ssem.at[3 * w + r], rsem.at[3 * w + r], (ox, oy, c)).wait_recv()
                cp = _remote(got, got, fs_sem.at[3 * w + r], fr_sem.at[3 * w + r], sib)
                cp.start()
                passes.append(cp)
        for w in range(self.n):
            for r, (ox, oy) in enumerate(others):
                got = outs[w].at[2 * ox + oy, self._half(ins[w], 1 - c), :]
                _remote(got, got, fs_sem.at[3 * w + r], fr_sem.at[3 * w + r], sib).wait_recv()
        for w in range(self.n):
            for r, (ox, oy) in enumerate(others):
                mine = ins[w].at[me, self._half(ins[w], c), :]
                _remote(mine, mine, ssem.at[3 * w + r], rsem.at[3 * w + r], (ox, oy, c)).wait_send()
        for cp in passes:
            cp.wait_send()


def _call(name, body, grid, in_specs, out_specs, out_shape, scratch, dims, args, comm=None):
    if comm is None:
        return pl.pallas_call(body, name=name, grid=grid, in_specs=in_specs, out_specs=out_specs, out_shape=out_shape,
                              scratch_shapes=scratch, compiler_params=_cp(dims))(*args)
    ni, no, ns, n = len(in_specs), len(out_shape), len(scratch), comm.n

    def hosted(*refs):
        ins, cin = refs[:ni], refs[ni:ni + n]
        outs, cout = refs[ni + n:ni + n + no], refs[ni + n + no:ni + 2 * n + no]
        scr, sems = refs[ni + 2 * n + no:ni + 2 * n + no + ns], refs[ni + 2 * n + no + ns:]
        ids = [pl.program_id(d) for d in range(len(grid))]
        first = functools.reduce(jnp.logical_and, [i == 0 for i in ids])
        last = functools.reduce(jnp.logical_and, [i == g - 1 for i, g in zip(ids, grid)])

        @pl.when(first)
        def _():
            comm.start(cin, cout, sems)

        body(*ins, *outs, *scr)

        @pl.when(last)
        def _():
            comm.finish(cin, cout, sems)

    return pl.pallas_call(
        hosted, name=name, grid=grid, in_specs=list(in_specs) + [ANY] * n, out_specs=list(out_specs) + [ANY] * n,
        out_shape=list(out_shape) + comm.out_shape, input_output_aliases={ni + w: no + w for w in range(n)},
        scratch_shapes=list(scratch) + comm.scratch, compiler_params=_cp(("arbitrary",) * len(grid)))(*args, *comm.arrs)


def _mm(a, b, *, mode, out_dtype, tm, tn, tk, name, a_fn=None, epi=None, extras=(), comm=None):
    if mode == "nn":
        (M, K), (K2, N) = a.shape, b.shape
    elif mode == "nt":
        (M, K), (N, K2) = a.shape, b.shape
    else:
        (K, M), (K2, N) = a.shape, b.shape
    assert K == K2, (a.shape, b.shape, mode)
    tm, tn, tk = min(tm, M), min(tn, N), min(tk, K)
    assert M % tm == 0 and N % tn == 0 and K % tk == 0, (M, N, K, tm, tn, tk)
    nk = K // tk
    if mode == "tn":
        a_spec = pl.BlockSpec((tk, tm), lambda i, j, k: (k, i))
        ca = 0
    else:
        a_spec = pl.BlockSpec((tm, tk), lambda i, j, k: (i, k))
        ca = 1
    if mode == "nt":
        b_spec = pl.BlockSpec((tn, tk), lambda i, j, k: (j, k))
        cb = 1
    else:
        b_spec = pl.BlockSpec((tk, tn), lambda i, j, k: (k, j))
        cb = 0
    dims = (((ca,), (cb,)), ((), ()))
    ne = len(extras)

    def body(a_ref, b_ref, *rest):
        ex = rest[:ne]
        o_ref = rest[ne]
        av = a_ref[...]
        if a_fn is not None:
            av = a_fn(av.astype(F32))
        p = lax.dot_general(av.astype(BF16), b_ref[...].astype(BF16), dims, preferred_element_type=F32)

        def fin(v):
            if epi is not None:
                v = epi(v, *[e[...] for e in ex])
            o_ref[...] = v.astype(out_dtype)

        if nk == 1:
            fin(p)
        else:
            acc = rest[ne + 1]
            k = pl.program_id(2)

            @pl.when(k == 0)
            def _():
                acc[...] = p

            @pl.when(k > 0)
            def _():
                acc[...] += p

            @pl.when(k == nk - 1)
            def _():
                fin(acc[...])

    res = _call(name, body, (M // tm, N // tn, nk),
                [a_spec, b_spec] + [pl.BlockSpec((tm, tn), lambda i, j, k: (i, j)) for _ in extras],
                [pl.BlockSpec((tm, tn), lambda i, j, k: (i, j))], [SDS((M, N), out_dtype)],
                [pltpu.VMEM((tm, tn), F32)] if nk > 1 else [], ("parallel", "parallel", "arbitrary"),
                (a, b, *extras), comm)
    return res[0] if comm is None else (res[0], res[1:])


def _rowwise(fn, rows, bcasts, outs, accs, *, tr, name):
    T = rows[0][0].shape[0]
    tr = min(tr, T)
    assert T % tr == 0
    nr, nb, no, na = len(rows), len(bcasts), len(outs), len(accs)
    in_specs = [pl.BlockSpec((tr, w), functools.partial(lambda i, c: (i, c), c=cb)) for (_, w, cb) in rows]
    in_specs += [pl.BlockSpec(b.shape, lambda i: (0, 0)) for b in bcasts]
    out_shape = [SDS((T, w), dt) for (w, dt) in outs] + [SDS(s, F32) for s in accs]
    out_specs = [pl.BlockSpec((tr, w), lambda i: (i, 0)) for (w, _) in outs]
    out_specs += [pl.BlockSpec(s, lambda i: (0, 0)) for s in accs]

    def body(*refs):
        ins = [r[...] for r in refs[:nr + nb]]
        o_refs = refs[nr + nb:nr + nb + no]
        a_refs = refs[nr + nb + no:]
        ro, ao = fn(*ins)
        for r, v in zip(o_refs, ro):
            r[...] = v.astype(r.dtype)
        if na:
            @pl.when(pl.program_id(0) == 0)
            def _():
                for r in a_refs:
                    r[...] = jnp.zeros(r.shape, F32)

            for r, v in zip(a_refs, ao):
                r[...] += v

    res = pl.pallas_call(
        body,
        name=name,
        grid=(T // tr,),
        in_specs=in_specs,
        out_specs=out_specs,
        out_shape=out_shape,
        compiler_params=_cp(("arbitrary",) if na else ("parallel",)),
    )(*[r[0] for r in rows], *bcasts)
    return res


def _rms(v):
    r = lax.rsqrt(jnp.mean(v * v, axis=-1, keepdims=True) + EPS)
    return v * r, r


def _rms_bwd(dy, xn, r, g):
    dxn = dy * g
    dv = r * (dxn - xn * jnp.mean(dxn * xn, axis=-1, keepdims=True))
    return dv, jnp.sum(dy * xn, axis=0, keepdims=True)


def _sig(v):
    return 1.0 / (1.0 + jnp.exp(-v))


_GELU_C = math.sqrt(2.0 / math.pi)


def _gelu(v):
    return 0.5 * v * (1.0 + jnp.tanh(_GELU_C * (v + 0.044715 * v * v * v)))


def _gelu_grad(v):
    t = jnp.tanh(_GELU_C * (v + 0.044715 * v * v * v))
    return 0.5 * (1.0 + t) + 0.5 * v * (1.0 - t * t) * _GELU_C * (1.0 + 3.0 * 0.044715 * v * v)


def _rope(v, c, s, sign):
    w = v.shape[1]
    m = lax.broadcasted_iota(jnp.int32, v.shape, 1) % HEAD_DIM
    p = jnp.where(m < ROT_DIM // 2, -pltpu.roll(v, w - ROT_DIM // 2, 1), pltpu.roll(v, ROT_DIM // 2, 1))
    return v * c + sign * (p * s)


def _rope_tables(T):
    half = ROT_DIM // 2
    inv = ROPE_THETA ** (-jnp.arange(half, dtype=F32) * 2.0 / ROT_DIM)
    ang = jnp.arange(T).astype(F32)[:, None] * inv[None, :]
    cos, sin = jnp.cos(ang), jnp.sin(ang)
    one = jnp.ones((T, HEAD_DIM - ROT_DIM), F32)
    c64 = jnp.concatenate([cos, cos, one], axis=1)
    s64 = jnp.concatenate([sin, sin, 0.0 * one], axis=1)
    return jnp.tile(c64, (1, 2)), jnp.tile(s64, (1, 2))


def _dup_half(m, lo):
    lane = lax.broadcasted_iota(jnp.int32, m.shape, 1)
    sw = pltpu.roll(m, HEAD_DIM, 1)
    return jnp.where(lane < HEAD_DIM, m, sw) if lo else jnp.where(lane >= HEAD_DIM, m, sw)


def _attn_mask(i):
    qi = lax.broadcasted_iota(jnp.int32, (ATT_BLOCK, 2 * ATT_BLOCK), 0)
    kj = lax.broadcasted_iota(jnp.int32, (ATT_BLOCK, 2 * ATT_BLOCK), 1)
    rel = qi + ATT_BLOCK - kj
    return (rel >= 0) & (rel < ATT_BLOCK) & ((kj >= ATT_BLOCK) | (i > 0))


_NT = (((1,), (1,)), ((), ()))
_TN = (((0,), (0,)), ((), ()))


def _attn_fwd(za, cos, sin, sinks, comm=None):
    T = za.shape[0]
    nb = T // ATT_BLOCK
    kvb = Q_W // (2 * KV_W)

    def body(sink_ref, q_ref, kvp_ref, kvc_ref, cc_ref, sc_ref, cp_ref, sp_ref, o_ref):
        i = pl.program_id(0)
        cc, sc, cp, sp = cc_ref[...], sc_ref[...], cp_ref[...], sp_ref[...]
        q = (_rope(q_ref[...], jnp.tile(cc, (1, 8)), jnp.tile(sc, (1, 8)), 1.0) * 0.125).astype(BF16)
        kvp, kvc = kvp_ref[...], kvc_ref[...]
        k = jnp.concatenate([_rope(kvp[:, :KV_W], cp, sp, 1.0), _rope(kvc[:, :KV_W], cc, sc, 1.0)], axis=0).astype(BF16)
        v = jnp.concatenate([kvp[:, KV_W:], kvc[:, KV_W:]], axis=0).astype(BF16)
        ok = _attn_mask(i)
        lane = lax.broadcasted_iota(jnp.int32, (ATT_BLOCK, LANES), 1)
        for kvh in range(2):
            k2 = _dup_half(k, kvh == 0)
            v2 = _dup_half(v, kvh == 0)
            for pair in range(4):
                c0 = (kvh * 4 + pair) * LANES
                q2 = q[:, c0:c0 + LANES]
                halves = []
                for hf in range(2):
                    sink = sink_ref[0, 2 * (kvh * 4 + pair) + hf]
                    qm = jnp.where((lane < HEAD_DIM) == (hf == 0), q2, jnp.zeros_like(q2))
                    s = lax.dot_general(qm, k2, _NT, preferred_element_type=F32)
                    s = jnp.where(ok, s, NEG)
                    m = jnp.maximum(jnp.max(s, axis=1, keepdims=True), sink)
                    e = jnp.exp(s - m)
                    den = jnp.sum(e, axis=1, keepdims=True) + jnp.exp(sink - m)
                    p = (e * (1.0 / den)).astype(BF16)
                    halves.append(jnp.dot(p, v2, preferred_element_type=F32))
                o_ref[:, c0:c0 + LANES] = jnp.where(lane < HEAD_DIM, halves[0], halves[1]).astype(BF16)

    blk = lambda w, f: pl.BlockSpec((ATT_BLOCK, w), f)
    res = _call(
        "attn_fwd", body, (nb,),
        [
            pl.BlockSpec(memory_space=pltpu.SMEM),
            blk(Q_W, lambda i: (i, 0)),
            blk(2 * KV_W, lambda i: (jnp.maximum(i - 1, 0), kvb)),
            blk(2 * KV_W, lambda i: (i, kvb)),
            blk(LANES, lambda i: (i, 0)),
            blk(LANES, lambda i: (i, 0)),
            blk(LANES, lambda i: (jnp.maximum(i - 1, 0), 0)),
            blk(LANES, lambda i: (jnp.maximum(i - 1, 0), 0)),
        ],
        [blk(Q_W, lambda i: (i, 0))], [SDS((T, Q_W), BF16)], [], ("parallel",),
        (sinks, za, za, za, cos, sin, cos, sin), comm)
    return res[0] if comm is None else (res[0], res[1:])


def _attn_bwd(za, cos, sin, sinks, o, do):
    T = za.shape[0]
    nb = T // ATT_BLOCK
    kvb = Q_W // (2 * KV_W)

    def body(sink_ref, q_ref, kvp_ref, kvc_ref, cc_ref, sc_ref, cp_ref, sp_ref, o_ref, do_ref,
             dq_ref, dkv_ref, dsk_ref, carry, dqs):
        i = pl.program_id(0)

        @pl.when(i == 0)
        def _():
            carry[...] = jnp.zeros(carry.shape, F32)
            dsk_ref[...] = jnp.zeros(dsk_ref.shape, F32)

        @pl.when(i < nb)
        def _():
            cc, sc, cp, sp = cc_ref[...], sc_ref[...], cp_ref[...], sp_ref[...]
            ccq, scq = jnp.tile(cc, (1, 8)), jnp.tile(sc, (1, 8))
            q = (_rope(q_ref[...], ccq, scq, 1.0) * 0.125).astype(BF16)
            kvp, kvc = kvp_ref[...], kvc_ref[...]
            k = jnp.concatenate([_rope(kvp[:, :KV_W], cp, sp, 1.0), _rope(kvc[:, :KV_W], cc, sc, 1.0)], axis=0).astype(BF16)
            v = jnp.concatenate([kvp[:, KV_W:], kvc[:, KV_W:]], axis=0).astype(BF16)
            ok = _attn_mask(i)
            lane = lax.broadcasted_iota(jnp.int32, (ATT_BLOCK, LANES), 1)
            lane_kv = lax.broadcasted_iota(jnp.int32, (2 * ATT_BLOCK, LANES), 1)
            lane_s = lax.broadcasted_iota(jnp.int32, (1, LANES), 1)
            dsk = jnp.zeros((1, LANES), F32)
            dk_h, dv_h = [], []
            for kvh in range(2):
                k2 = _dup_half(k, kvh == 0)
                v2 = _dup_half(v, kvh == 0)
                dk2 = jnp.zeros((2 * ATT_BLOCK, LANES), F32)
                dv2 = jnp.zeros((2 * ATT_BLOCK, LANES), F32)
                for pair in range(4):
                    c0 = (kvh * 4 + pair) * LANES
                    q2 = q[:, c0:c0 + LANES]
                    do2 = do_ref[:, c0:c0 + LANES]
                    prod = do2.astype(F32) * o_ref[:, c0:c0 + LANES].astype(F32)
                    dqh = []
                    for hf in range(2):
                        h = 2 * (kvh * 4 + pair) + hf
                        sink = sink_ref[0, h]
                        sel = (lane < HEAD_DIM) == (hf == 0)
                        qm = jnp.where(sel, q2, jnp.zeros_like(q2))
                        dom = jnp.where(sel, do2, jnp.zeros_like(do2))
                        delta = jnp.sum(jnp.where(sel, prod, 0.0), axis=1, keepdims=True)
                        s = lax.dot_general(qm, k2, _NT, preferred_element_type=F32)
                        s = jnp.where(ok, s, NEG)
                        m = jnp.maximum(jnp.max(s, axis=1, keepdims=True), sink)
                        e = jnp.exp(s - m)
                        inv = 1.0 / (jnp.sum(e, axis=1, keepdims=True) + jnp.exp(sink - m))
                        p = e * inv
                        dsk = dsk + jnp.where(lane_s == h, -jnp.sum(jnp.exp(sink - m) * inv * delta), 0.0)
                        dp = lax.dot_general(dom, v2, _NT, preferred_element_type=F32)
                        ds = (p * (dp - delta)).astype(BF16)
                        dqh.append(jnp.dot(ds, k2, preferred_element_type=F32))
                        dk2 = dk2 + lax.dot_general(ds, qm, _TN, preferred_element_type=F32)
                        dv2 = dv2 + lax.dot_general(p.astype(BF16), dom, _TN, preferred_element_type=F32)
                    dqs[:, c0:c0 + LANES] = jnp.where(lane < HEAD_DIM, dqh[0], dqh[1]) * 0.125
                dk_h.append(dk2 + pltpu.roll(dk2, HEAD_DIM, 1))
                dv_h.append(dv2 + pltpu.roll(dv2, HEAD_DIM, 1))
            dk = jnp.where(lane_kv < HEAD_DIM, dk_h[0], dk_h[1])
            dv = jnp.where(lane_kv < HEAD_DIM, dv_h[0], dv_h[1])
            dq_ref[...] = _rope(dqs[...], ccq, scq, -1.0).astype(dq_ref.dtype)
            dkp = _rope(dk[:ATT_BLOCK], cp, sp, -1.0)
            dkc = _rope(dk[ATT_BLOCK:], cc, sc, -1.0)
            dkv_ref[...] = (carry[...] + jnp.concatenate([dkp, dv[:ATT_BLOCK]], axis=1)).astype(dkv_ref.dtype)
            carry[...] = jnp.concatenate([dkc, dv[ATT_BLOCK:]], axis=1)
            dsk_ref[...] += dsk

        @pl.when(i == nb)
        def _():
            dkv_ref[...] = carry[...].astype(dkv_ref.dtype)

    blk = lambda w, f: pl.BlockSpec((ATT_BLOCK, w), f)
    cur = lambda i: jnp.minimum(i, nb - 1)
    prv = lambda i: jnp.maximum(jnp.minimum(i, nb - 1) - 1, 0)
    return pl.pallas_call(
        body,
        name="attn_bwd",
        grid=(nb + 1,),
        in_specs=[
            pl.BlockSpec(memory_space=pltpu.SMEM),
            blk(Q_W, lambda i: (cur(i), 0)),
            blk(2 * KV_W, lambda i: (prv(i), kvb)),
            blk(2 * KV_W, lambda i: (cur(i), kvb)),
            blk(LANES, lambda i: (cur(i), 0)),
            blk(LANES, lambda i: (cur(i), 0)),
            blk(LANES, lambda i: (prv(i), 0)),
            blk(LANES, lambda i: (prv(i), 0)),
            blk(Q_W, lambda i: (cur(i), 0)),
            blk(Q_W, lambda i: (cur(i), 0)),
        ],
        out_specs=[
            blk(Q_W, lambda i: (cur(i), 0)),
            blk(2 * KV_W, lambda i: (jnp.maximum(i - 1, 0), 0)),
            pl.BlockSpec((1, LANES), lambda i: (0, 0)),
        ],
        out_shape=[SDS((T, Q_W), BF16), SDS((T, 2 * KV_W), BF16), SDS((1, LANES), F32)],
        scratch_shapes=[pltpu.VMEM((ATT_BLOCK, 2 * KV_W), F32), pltpu.VMEM((ATT_BLOCK, Q_W), F32)],
        compiler_params=_cp(("arbitrary",)),
    )(sinks, za, za, za, cos, sin, cos, sin, o, do)


def _s5_discretize(lam_re, lam_im, log_dt, b_re, b_im):
    dt = jnp.exp(log_dt)[:, None]
    mag = jnp.exp(lam_re * dt)
    a_re, a_im = mag * jnp.cos(lam_im * dt), mag * jnp.sin(lam_im * dt)
    den = lam_re * lam_re + lam_im * lam_im
    nr, ni = a_re - 1.0, a_im
    coef_re = (nr * lam_re + ni * lam_im) / den
    coef_im = (ni * lam_re - nr * lam_im) / den
    bb_re = coef_re[..., None] * b_re - coef_im[..., None] * b_im
    bb_im = coef_re[..., None] * b_im + coef_im[..., None] * b_re
    return a_re, a_im, bb_re, bb_im


def _blockdiag_in(bb):
    x = bb.reshape(N_JB, 8, SSM_P, SSM_GC).transpose(0, 1, 3, 2)
    return (x[:, :, :, None, :] * jnp.eye(8, dtype=bb.dtype)[None, :, None, :, None]).reshape(N_JB, 128, 512)


def _blockdiag_in_extract(m):
    x = m.reshape(N_JB, 8, SSM_GC, 8, SSM_P)
    x = jnp.einsum('jgchp,gh->jgcp', x, jnp.eye(8, dtype=m.dtype))
    return x.transpose(0, 1, 3, 2).reshape(SSM_G, SSM_P, SSM_GC)


def _blockdiag_out(c):
    x = c.reshape(N_JB, 8, SSM_GC, SSM_P).transpose(0, 1, 3, 2)
    return (x[:, :, :, None, :] * jnp.eye(8, dtype=c.dtype)[None, :, None, :, None]).reshape(N_JB, 512, 128)


def _blockdiag_out_extract(m):
    x = m.reshape(N_JB, 8, SSM_P, 8, SSM_GC)
    x = jnp.einsum('jgphc,gh->jgpc', x, jnp.eye(8, dtype=m.dtype))
    return x.transpose(0, 1, 3, 2).reshape(SSM_G, SSM_GC, SSM_P)


def _s5_tables(a_re, a_im):
    ar, ai = a_re.reshape(N_LG, 1, LANES), a_im.reshape(N_LG, 1, LANES)
    p_re, p_im, n = ar, ai, 1
    while n < S5_SEG:
        tr, ti = p_re[:, n - 1:n], p_im[:, n - 1:n]
        p_re, p_im = (jnp.concatenate([p_re, p_re * tr - p_im * ti], axis=1),
                      jnp.concatenate([p_im, p_re * ti + p_im * tr], axis=1))
        n *= 2
    bc = lambda v: jnp.broadcast_to(v, (N_LG, SUBLANES, LANES))
    return p_re, p_im, bc(ar), bc(ai)


def _s5_to_time_major(src_ref, dst_ref):
    for t in range(S5_SEG):
        dst_ref[t * SUBLANES:(t + 1) * SUBLANES, :] = src_ref[pl.ds(t, SUBLANES, stride=S5_SEG), :]


def _s5_from_time_major(val, dst_ref):
    for t in range(S5_SEG):
        dst_ref[pl.ds(t, SUBLANES, stride=S5_SEG), :] = val[t * SUBLANES:(t + 1) * SUBLANES, :]


def _tm_rows(t, row0=0):
    return pl.ds(pl.multiple_of(t * SUBLANES + row0, SUBLANES), SUBLANES)


def _s5_scan(src_re, src_im, dst_re, dst_im, ar, ai, reverse, dst_row0=0):
    def step(n, carry):
        t = (S5_SEG - 1 - n) if reverse else n
        out = []
        for ll in range(LG_PER_JB):
            xr, xi = carry[2 * ll], carry[2 * ll + 1]
            idx = (ll, _tm_rows(t), slice(None))
            odx = (ll, _tm_rows(t, dst_row0), slice(None))
            nr = ar[ll] * xr - ai[ll] * xi + src_re[idx]
            ni = ar[ll] * xi + ai[ll] * xr + src_im[idx]
            dst_re[odx] = nr
            dst_im[odx] = ni
            out += [nr, ni]
        return tuple(out)
    z = jnp.zeros((SUBLANES, LANES), F32)
    return lax.fori_loop(0, S5_SEG, step, (z,) * (2 * LG_PER_JB))


def _s5_fixup(ends, in_re, in_im, mr, mi, s_re, s_im, reverse):
    cr, ci = in_re, in_im
    order = range(SUBLANES - 1, -1, -1) if reverse else range(SUBLANES)
    for s in order:
        s_re[:, s:s + 1, :] = cr
        s_im[:, s:s + 1, :] = ci
        er = jnp.stack([ends[2 * ll][s:s + 1, :] for ll in range(LG_PER_JB)])
        ei = jnp.stack([ends[2 * ll + 1][s:s + 1, :] for ll in range(LG_PER_JB)])
        cr, ci = mr * cr - mi * ci + er, mr * ci + mi * cr + ei
    return cr, ci


def _s5_correct(x_re, x_im, s_re, s_im, p_re, p_im, row0=0):
    sr = [s_re[ll] for ll in range(LG_PER_JB)]
    si = [s_im[ll] for ll in range(LG_PER_JB)]

    def step(t, carry):
        for ll in range(LG_PER_JB):
            idx = (ll, _tm_rows(t, row0), slice(None))
            pr, pi = p_re[ll, pl.ds(t, 1), :], p_im[ll, pl.ds(t, 1), :]
            x_re[idx] = x_re[idx] + (pr * sr[ll] - pi * si[ll])
            x_im[idx] = x_im[idx] + (pr * si[ll] + pi * sr[ll])
        return carry
    lax.fori_loop(0, S5_SEG, step, 0)


def _s5_specs(nc, rev):
    cidx = (lambda c: nc - 1 - c) if rev else (lambda c: c)
    jb = lambda shape: pl.BlockSpec(shape, lambda j, c: (j, 0, 0))
    return cidx, [
        jb((1, LANES, 8 * LANES)),
        jb((1, 8 * LANES, LANES)),
        pl.BlockSpec((1, LANES), lambda j, c: (0, j)),
        jb((LG_PER_JB, SUBLANES, LANES)), jb((LG_PER_JB, SUBLANES, LANES)),
        jb((LG_PER_JB, 1, LANES)), jb((LG_PER_JB, 1, LANES)),
        jb((LG_PER_JB, S5_SEG, LANES)), jb((LG_PER_JB, S5_SEG, LANES)),
    ]


def _s5_fwd(za, prm, comm=None):
    T = za.shape[0]
    R = S5_CHUNK
    nc = T // R
    ub = (Q_W + 2 * KV_W) // LANES
    _, pspecs = _s5_specs(nc, False)

    def body(u_ref, b_ref, c_ref, d_ref, are_ref, aim_ref, alr_ref, ali_ref, pr_ref, pi_ref,
             yg_ref, x0r_ref, x0i_ref, bur, bui, xsr, xsi, sr, si, xcr, xci, utm, ynat):
        c = pl.program_id(1)

        @pl.when(c == 0)
        def _():
            xcr[...] = jnp.zeros(xcr.shape, F32)
            xci[...] = jnp.zeros(xci.shape, F32)

        _s5_to_time_major(u_ref, utm)
        u = utm[...]
        ub16 = u.astype(BF16)
        bu = jnp.dot(ub16, b_ref[0].astype(BF16), preferred_element_type=F32)
        for ll in range(LG_PER_JB):
            bur[ll] = bu[:, ll * LANES:(ll + 1) * LANES]
            bui[ll] = bu[:, (LG_PER_JB + ll) * LANES:(LG_PER_JB + ll + 1) * LANES]
        ar = [are_ref[ll] for ll in range(LG_PER_JB)]
        ai = [aim_ref[ll] for ll in range(LG_PER_JB)]
        ends = _s5_scan(bur, bui, xsr, xsi, ar, ai, False)
        in_r, in_i = xcr[...], xci[...]
        x0r_ref[0] = in_r
        x0i_ref[0] = in_i
        out_r, out_i = _s5_fixup(ends, in_r, in_i, alr_ref[...], ali_ref[...], sr, si, False)
        xcr[...] = out_r
        xci[...] = out_i
        _s5_correct(xsr, xsi, sr, si, pr_ref, pi_ref)
        xcat = jnp.concatenate([xsr[ll].astype(BF16) for ll in range(LG_PER_JB)]
                               + [xsi[ll].astype(BF16) for ll in range(LG_PER_JB)], axis=1)
        y = d_ref[...] * u + jnp.dot(xcat, c_ref[0].astype(BF16), preferred_element_type=F32)
        _s5_from_time_major(_gelu(y), ynat)
        yg_ref[...] = ynat[...].astype(BF16)

    st = pl.BlockSpec((1, LG_PER_JB, 1, LANES), lambda j, c: (c, j, 0, 0))
    vm = lambda rows: pltpu.VMEM((LG_PER_JB, rows, LANES), F32)
    res = _call(
        "s5_fwd", body, (N_JB, nc),
        [pl.BlockSpec((R, LANES), lambda j, c: (c, ub + j))] + pspecs,
        [pl.BlockSpec((R, LANES), lambda j, c: (c, j)), st, st],
        [SDS((T, SSM_W), BF16), SDS((nc, N_LG, 1, LANES), F32), SDS((nc, N_LG, 1, LANES), F32)],
        [vm(R), vm(R), vm(R), vm(R), vm(SUBLANES), vm(SUBLANES), vm(1), vm(1),
         pltpu.VMEM((R, LANES), F32), pltpu.VMEM((R, LANES), F32)],
        ("parallel", "arbitrary"), (za, *prm), comm)
    return res if comm is None else (res[:3], res[3:])


def _s5_bwd(za, dyg, x0r, x0i, prm, prev_tables):
    T = za.shape[0]
    R = S5_CHUNK
    nc = T // R
    ub = (Q_W + 2 * KV_W) // LANES
    cidx, pspecs = _s5_specs(nc, True)
    PAD = SUBLANES

    def body(u_ref, dyg_ref, x0r_ref, x0i_ref, b_ref, c_ref, d_ref, are_ref, aim_ref,
             alr_ref, ali_ref, pr_ref, pi_ref, qr_ref, qi_ref,
             du_ref, dar_ref, dai_ref, db_ref, dc_ref, dd_ref,
             bur, bui, xsr, xsi, sr, si, gcr, gci, utm, dtm, dunat):
        c = pl.program_id(1)

        @pl.when(c == 0)
        def _():
            gcr[...] = jnp.zeros(gcr.shape, F32)
            gci[...] = jnp.zeros(gci.shape, F32)
            dar_ref[...] = jnp.zeros(dar_ref.shape, F32)
            dai_ref[...] = jnp.zeros(dai_ref.shape, F32)
            db_ref[...] = jnp.zeros(db_ref.shape, F32)
            dc_ref[...] = jnp.zeros(dc_ref.shape, F32)
            dd_ref[...] = jnp.zeros(dd_ref.shape, F32)

        _s5_to_time_major(u_ref, utm)
        _s5_to_time_major(dyg_ref, dtm)
        u = utm[...]
        ub16 = u.astype(BF16)
        bcat, ccat = b_ref[0].astype(BF16), c_ref[0].astype(BF16)
        lanes = lambda v, ll: v[:, ll * LANES:(ll + 1) * LANES]
        bu = jnp.dot(ub16, bcat, preferred_element_type=F32)
        for ll in range(LG_PER_JB):
            bur[ll] = lanes(bu, ll)
            bui[ll] = lanes(bu, LG_PER_JB + ll)
        ar = [are_ref[ll] for ll in range(LG_PER_JB)]
        ai = [aim_ref[ll] for ll in range(LG_PER_JB)]
        ends = _s5_scan(bur, bui, xsr, xsi, ar, ai, False, dst_row0=PAD)
        in_r, in_i = x0r_ref[0], x0i_ref[0]
        _s5_fixup(ends, in_r, in_i, alr_ref[...], ali_ref[...], sr, si, False)
        _s5_correct(xsr, xsi, sr, si, pr_ref, pi_ref, PAD)
        xsr[:, 0:PAD, :] = sr[...]
        xsi[:, 0:PAD, :] = si[...]
        xcat = jnp.concatenate([xsr[ll, PAD:, :].astype(BF16) for ll in range(LG_PER_JB)]
                               + [xsi[ll, PAD:, :].astype(BF16) for ll in range(LG_PER_JB)], axis=1)
        y = d_ref[...] * u + jnp.dot(xcat, ccat, preferred_element_type=F32)
        dy = dtm[...] * _gelu_grad(y)
        dyb = dy.astype(BF16)
        dd_ref[...] += jnp.sum(dy * u, axis=0, keepdims=True)
        du = d_ref[...] * dy
        dc_ref[0] += lax.dot_general(xcat, dyb, _TN, preferred_element_type=F32)
        g = lax.dot_general(dyb, ccat, _NT, preferred_element_type=F32)
        for ll in range(LG_PER_JB):
            bur[ll] = lanes(g, ll)
            bui[ll] = lanes(g, LG_PER_JB + ll)
        ends = _s5_scan(bur, bui, bur, bui, ar, [-v for v in ai], True)
        out_r, out_i = _s5_fixup(ends, gcr[...], gci[...], alr_ref[...], -ali_ref[...], sr, si, True)
        gcr[...] = out_r
        gci[...] = out_i
        _s5_correct(bur, bui, sr, si, qr_ref, qi_ref)
        for ll in range(LG_PER_JB):
            gr, gi = bur[ll], bui[ll]
            xpr, xpi = xsr[ll, 0:R, :], xsi[ll, 0:R, :]
            red = lambda v: v.reshape(R // SUBLANES, SUBLANES, LANES).sum(axis=0)
            dar_ref[ll] += red(xpr * gr + xpi * gi)
            dai_ref[ll] += red(xpr * gi - xpi * gr)
        gcat = jnp.concatenate([bur[ll].astype(BF16) for ll in range(LG_PER_JB)]
                               + [bui[ll].astype(BF16) for ll in range(LG_PER_JB)], axis=1)
        db_ref[0] += lax.dot_general(ub16, gcat, _TN, preferred_element_type=F32)
        du = du + lax.dot_general(gcat, bcat, _NT, preferred_element_type=F32)
        _s5_from_time_major(du, dunat)
        du_ref[...] = dunat[...].astype(du_ref.dtype)

    st = pl.BlockSpec((1, LG_PER_JB, 1, LANES), lambda j, c: (cidx(c), j, 0, 0))
    jb = lambda shape: pl.BlockSpec(shape, lambda j, c: (j, 0, 0))
    vm = lambda rows: pltpu.VMEM((LG_PER_JB, rows, LANES), F32)
    return pl.pallas_call(
        body,
        name="s5_bwd",
        grid=(N_JB, nc),
        in_specs=[pl.BlockSpec((R, LANES), lambda j, c: (cidx(c), ub + j)),
                  pl.BlockSpec((R, LANES), lambda j, c: (cidx(c), j)), st, st] + pspecs
                 + [jb((LG_PER_JB, S5_SEG, LANES)), jb((LG_PER_JB, S5_SEG, LANES))],
        out_specs=[pl.BlockSpec((R, LANES), lambda j, c: (cidx(c), j)),
                   jb((LG_PER_JB, SUBLANES, LANES)), jb((LG_PER_JB, SUBLANES, LANES)),
                   jb((1, LANES, 8 * LANES)), jb((1, 8 * LANES, LANES)),
                   pl.BlockSpec((1, LANES), lambda j, c: (0, j))],
        out_shape=[SDS((T, SSM_W), BF16), SDS((N_LG, SUBLANES, LANES), F32), SDS((N_LG, SUBLANES, LANES), F32),
                   SDS((N_JB, LANES, 8 * LANES), F32), SDS((N_JB, 8 * LANES, LANES), F32), SDS((1, SSM_W), F32)],
        scratch_shapes=[vm(R), vm(R), vm(R + PAD), vm(R + PAD), vm(SUBLANES), vm(SUBLANES), vm(1), vm(1)]
                       + [pltpu.VMEM((R, LANES), F32)] * 3,
        compiler_params=_cp(("parallel", "arbitrary")),
    )(za, dyg, x0r, x0i, *prm, *prev_tables)


def _local_step(x, target, gains, w_a, w_g, sinks, s5w, comms, late):
    T = x.shape[0]
    D = D_MODEL
    g1, g2, g3, g4 = gains
    cos, sin = _rope_tables(T)
    lam_re, lam_im, log_dt, b_re, b_im, c_re, c_im, d_skip = s5w
    (a_re, a_im, bb_re, bb_im), disc_vjp = jax.vjp(_s5_discretize, lam_re, lam_im, log_dt, b_re, b_im)
    p_re, p_im, abr, abi = _s5_tables(a_re, a_im)
    prm = (jnp.concatenate([_blockdiag_in(bb_re), _blockdiag_in(bb_im)], axis=2),
           jnp.concatenate([_blockdiag_out(c_re), -_blockdiag_out(c_im)], axis=1),
           d_skip.reshape(1, SSM_W), abr, abi, p_re[:, S5_SEG - 1:, :], p_im[:, S5_SEG - 1:, :], p_re, p_im)
    rev_tables = (p_re[:, ::-1, :], -p_im[:, ::-1, :])
    mm = functools.partial(_mm, tm=512, tn=1024, tk=2048)

    h = _rowwise(lambda xv, g: ((_rms(xv)[0] * g,), ()), [(x, D, 0)], [g1], [(D, BF16)], [], tr=512, name="norm1")[0]
    za = _mm(h, w_a, mode="nn", out_dtype=F32, tm=512, tn=1152, tk=2048, name="mm_za")
    unpack = lambda res, comm: (res, ()) if comm is None else res
    zg, got0 = unpack(mm(h, w_g, mode="nn", out_dtype=F32, name="mm_zg", comm=comms[0]), comms[0])
    o_attn, got1 = unpack(_attn_fwd(za, cos, sin, sinks, comm=comms[1]), comms[1])
    (yg, x0r, x0i), got2 = unpack(_s5_fwd(za, prm, comm=comms[2]), comms[2])
    w_glu, w_ba, w_bs, w_out, w_up, w_down = late(got0, got1, got2)
    zglu = mm(yg, w_glu, mode="nn", out_dtype=F32, name="mm_glu")
    o_ssm = _rowwise(lambda z1, z2: ((z1 * _sig(z2),), ()), [(zglu, SSM_W, 0), (zglu, SSM_W, 1)], [],
                     [(SSM_W, BF16)], [], tr=512, name="glu")[0]
    ya = mm(o_attn, w_ba, mode="nn", out_dtype=F32, name="mm_ya")
    ys = mm(o_ssm, w_bs, mode="nn", out_dtype=F32, name="mm_ys")
    mi = _rowwise(lambda ga, gs, a, s: ((_sig(ga) * a + _sig(gs) * s,), ()),
                  [(zg, D, 0), (zg, D, 1), (ya, D, 0), (ys, D, 0)], [], [(D, BF16)], [], tr=256, name="gate")[0]
    mixed = mm(mi, w_out, mode="nn", out_dtype=F32, name="mm_out")

    def f_post(xv, mv, g2v, g3v):
        x1v = xv + _rms(mv)[0] * g2v
        return (x1v, _rms(x1v)[0] * g3v), ()
    x1, h2 = _rowwise(f_post, [(x, D, 0), (mixed, D, 0)], [g2, g3], [(D, F32), (D, BF16)], [], tr=256, name="post_mix")
    act = mm(h2, w_up, mode="nn", out_dtype=BF16, name="mm_up", epi=lambda v: jnp.maximum(v, 0.0))
    f = _mm(act, w_down, mode="nn", out_dtype=F32, tm=512, tn=2048, tk=2048, name="mm_down", a_fn=lambda v: v * v)

    def f_final(x1v, fv, tv, g4v):
        fn, r = _rms(fv)
        e = x1v + fn * g4v - tv
        dx2v = e * (1.0 / D)
        dfv, dg4v = _rms_bwd(dx2v, fn, r, g4v)
        return (dfv, dx2v), (dg4v, jnp.zeros((SUBLANES, LANES), F32) + 0.5 * jnp.sum(e * e) * (1.0 / D))
    df, dx2, dg4, lossb = _rowwise(f_final, [(x1, D, 0), (f, D, 0), (target, D, 0)], [g4],
                                   [(D, BF16), (D, F32)], [(1, D), (SUBLANES, LANES)], tr=256, name="final")

    dpre = mm(df, w_down, mode="nt", out_dtype=BF16, name="mm_dact", epi=lambda v, a: v * (2.0 * a.astype(F32)), extras=(act,))
    wg = functools.partial(_mm, mode="tn", out_dtype=F32, tm=1024, tn=1024, tk=2048)
    d_w_down = wg(act, df, name="wg_down", a_fn=lambda v: v * v)
    dh2 = _mm(dpre, w_up, mode="nt", out_dtype=F32, tm=512, tn=2048, tk=2048, name="mm_dh2")
    d_w_up = wg(h2, dpre, name="wg_up")

    def f_mid(dx2v, dh2v, x1v, mv, g2v, g3v):
        x1n, r3 = _rms(x1v)
        d3, dg3v = _rms_bwd(dh2v, x1n, r3, g3v)
        dx1v = dx2v + d3
        mn, r2 = _rms(mv)
        dmv, dg2v = _rms_bwd(dx1v, mn, r2, g2v)
        return (dx1v, dmv), (dg3v, dg2v)
    dx1, dmixed, dg3, dg2 = _rowwise(f_mid, [(dx2, D, 0), (dh2, D, 0), (x1, D, 0), (mixed, D, 0)], [g2, g3],
                                     [(D, F32), (D, BF16)], [(1, D), (1, D)], tr=256, name="mid")

    dmi = mm(dmixed, w_out, mode="nt", out_dtype=F32, name="mm_dmi")
    d_w_out = wg(mi, dmixed, name="wg_out")

    def f_gate(dv, ga, gs, a, s):
        sa, ss = _sig(ga), _sig(gs)
        return (dv * sa, dv * ss, jnp.concatenate([dv * a * sa * (1.0 - sa), dv * s * ss * (1.0 - ss)], axis=1)), ()
    dya, dys, dzg = _rowwise(f_gate, [(dmi, D, 0), (zg, D, 0), (zg, D, 1), (ya, D, 0), (ys, D, 0)], [],
                             [(D, BF16), (D, BF16), (2 * D, BF16)], [], tr=256, name="gate_bwd")
    do_attn = mm(dya, w_ba, mode="nt", out_dtype=BF16, name="mm_doa")
    d_w_ba = wg(o_attn, dya, name="wg_ba")
    do_ssm = mm(dys, w_bs, mode="nt", out_dtype=F32, name="mm_dos")
    d_w_bs = wg(o_ssm, dys, name="wg_bs")

    def f_glu(dv, z1, z2):
        s2 = _sig(z2)
        return (jnp.concatenate([dv * s2, dv * z1 * s2 * (1.0 - s2)], axis=1),), ()
    dzglu = _rowwise(f_glu, [(do_ssm, SSM_W, 0), (zglu, SSM_W, 0), (zglu, SSM_W, 1)], [], [(2 * SSM_W, BF16)], [],
                     tr=512, name="glu_bwd")[0]
    dyg = mm(dzglu, w_glu, mode="nt", out_dtype=F32, name="mm_dyg")
    d_w_glu = wg(yg, dzglu, name="wg_glu")
    du, dar, dai, dbc, dcc, ddv = _s5_bwd(za, dyg, x0r, x0i, prm, rev_tables)
    dbr, dbi = dbc[:, :, :4 * LANES], dbc[:, :, 4 * LANES:]
    dcr, dci = dcc[:, :4 * LANES, :], -dcc[:, 4 * LANES:, :]
    dq, dkv, dsk = _attn_bwd(za, cos, sin, sinks, o_attn, do_attn)
    dza = jnp.concatenate([dq, dkv, du], axis=1)
    dh = mm(dza, w_a, mode="nt", out_dtype=F32, name="mm_dh_a", tk=ZA_W)
    dh = _mm(dzg, w_g, mode="nt", out_dtype=F32, tm=512, tn=1024, tk=2048, name="mm_dh_g", epi=lambda v, p: v + p, extras=(dh,))
    d_w_a = _mm(h, dza, mode="tn", out_dtype=F32, tm=1024, tn=ZA_W // 2, tk=2048, name="wg_a")
    d_w_g = wg(h, dzg, name="wg_g")

    def f_first(dx1v, dhv, xv, g1v):
        xn, r1 = _rms(xv)
        d1, dg1v = _rms_bwd(dhv, xn, r1, g1v)
        return (dx1v + d1,), (dg1v,)
    dx, dg1 = _rowwise(f_first, [(dx1, D, 0), (dh, D, 0), (x, D, 0)], [g1], [(D, F32)], [(1, D)], tr=256, name="first")

    da_re = dar.sum(axis=1).reshape(SSM_G, SSM_P)
    da_im = dai.sum(axis=1).reshape(SSM_G, SSM_P)
    d_lam_re, d_lam_im, d_log_dt, d_b_re, d_b_im = disc_vjp(
        (da_re, da_im, _blockdiag_in_extract(dbr), _blockdiag_in_extract(dbi)))
    small = dict(norm_mix_pre=dg1, norm_mix_post=dg2, norm_mlp_pre=dg3, norm_mlp_post=dg4,
                 sinks=dsk[:, :N_Q_HEADS], lam_re=d_lam_re, lam_im=d_lam_im, log_dt=d_log_dt,
                 b_re=d_b_re, b_im=d_b_im, c_re=_blockdiag_out_extract(dcr), c_im=_blockdiag_out_extract(dci),
                 d_skip=ddv.reshape(SSM_G, SSM_GC))
    big = dict(w_in=jnp.concatenate([d_w_a, d_w_g], axis=1), w_glu=d_w_glu,
               w_branch=jnp.concatenate([d_w_ba, d_w_bs], axis=0), w_out=d_w_out, w_up=d_w_up, w_down=d_w_down)
    return lossb[0, 0], dx, small, big


def _cast_into_slot(w, k_arr):
    rows, cols = w.shape
    tr = 256

    def body(k_ref, w_ref, o_ref):
        o_ref[0] = w_ref[...].astype(BF16)

    return pl.pallas_call(
        body,
        name="cast_into_slot",
        grid_spec=pltpu.PrefetchScalarGridSpec(
            num_scalar_prefetch=1,
            grid=(rows // tr,),
            in_specs=[pl.BlockSpec((tr, cols), lambda i, k: (i, 0))],
            out_specs=pl.BlockSpec((1, tr, cols), lambda i, k: (k[0], i, 0)),
        ),
        out_shape=SDS((4, rows, cols), BF16),
        compiler_params=_cp(("parallel",)),
    )(k_arr, w)


def _gather_weights(slotted):
    comm = _GatherComm(slotted)
    n = comm.n

    def body(*refs):
        ins, outs, sems = refs[:n], refs[n:2 * n], refs[2 * n:]
        comm.start(ins, outs, sems)
        comm.finish(ins, outs, sems)

    return pl.pallas_call(
        body,
        name="gather_weights",
        in_specs=[ANY] * n,
        out_specs=[ANY] * n,
        out_shape=comm.out_shape,
        input_output_aliases={w: w for w in range(n)},
        scratch_shapes=comm.scratch,
    )(*slotted)


def _pair_exchange(grads):
    n = len(grads)

    def body(*refs):
        ins, outs = refs[:n], refs[n:2 * n]
        ssem, rsem = refs[2 * n:]
        x, y, c, _ = _place()
        cps = []
        for w in range(n):
            hr = ins[w].shape[1] // 2
            src = ins[w].at[:, pl.ds(pl.multiple_of((1 - c) * hr, 8), hr), :]
            cp = _remote(src, outs[w], ssem.at[w], rsem.at[w], (x, y, 1 - c))
            cp.start()
            cps.append(cp)
        for cp in cps:
            cp.wait()

    dma = pltpu.SemaphoreType.DMA
    return pl.pallas_call(
        body,
        name="pair_exchange",
        in_specs=[ANY] * n,
        out_specs=[ANY] * n,
        out_shape=[SDS((4, g.shape[1] // 2, g.shape[2]), g.dtype) for g in grads],
        scratch_shapes=[dma((n,)), dma((n,))],
    )(*grads)


def _pair_sum(g, r, c_arr):
    _, _, hr, cols = g.shape
    tr = min(256, hr)

    def body(c_ref, g_ref, r_ref, o_ref):
        o_ref[0] = (g_ref[0, 0] + r_ref[0]).astype(BF16)

    return pl.pallas_call(
        body,
        name="pair_sum",
        grid_spec=pltpu.PrefetchScalarGridSpec(
            num_scalar_prefetch=1,
            grid=(4, hr // tr),
            in_specs=[pl.BlockSpec((1, 1, tr, cols), lambda k, i, c_ref: (k, c_ref[0], i, 0)),
                      pl.BlockSpec((1, tr, cols), lambda k, i, c_ref: (k, i, 0))],
            out_specs=pl.BlockSpec((1, tr, cols), lambda k, i, c_ref: (k, i, 0)),
        ),
        out_shape=SDS((4, hr, cols), BF16),
        compiler_params=_cp(("parallel", "parallel")),
    )(c_arr, g, r)


def _chip_exchange(psums):
    n = len(psums)

    def body(*refs):
        ins, outs = refs[:n], refs[n:2 * n]
        ssem, rsem = refs[2 * n:]
        x, y, c, others = _place()
        cps = []
        for w in range(n):
            for r, (ox, oy) in enumerate(others):
                cp = _remote(ins[w].at[2 * ox + oy], outs[w].at[r], ssem.at[3 * w + r], rsem.at[3 * w + r], (ox, oy, c))
                cp.start()
                cps.append(cp)
        for cp in cps:
            cp.wait()

    dma = pltpu.SemaphoreType.DMA
    return pl.pallas_call(
        body,
        name="chip_exchange",
        in_specs=[ANY] * n,
        out_specs=[ANY] * n,
        out_shape=[SDS((3,) + p.shape[1:], p.dtype) for p in psums],
        scratch_shapes=[dma((3 * n,)), dma((3 * n,))],
    )(*psums)


def _chip_sum(g, r, q, kc_arr):
    _, _, hr, cols = g.shape
    tr = min(256, hr)

    def body(kc_ref, g_ref, r_ref, q_ref, o_ref):
        s = g_ref[0, 0] + r_ref[0]
        for j in range(3):
            s = s + q_ref[j].astype(F32)
        o_ref[...] = s

    return pl.pallas_call(
        body,
        name="chip_sum",
        grid_spec=pltpu.PrefetchScalarGridSpec(
            num_scalar_prefetch=1,
            grid=(hr // tr,),
            in_specs=[pl.BlockSpec((1, 1, tr, cols), lambda i, kc: (kc[0], kc[1], i, 0)),
                      pl.BlockSpec((1, tr, cols), lambda i, kc: (kc[0], i, 0)),
                      pl.BlockSpec((3, tr, cols), lambda i, kc: (0, i, 0))],
            out_specs=pl.BlockSpec((tr, cols), lambda i, kc: (kc[1] * (hr // tr) + i, 0)),
        ),
        out_shape=SDS((2 * hr, cols), F32),
        compiler_params=_cp(("parallel",)),
    )(kc_arr, g, r, q)


def _pair_share(blocks):
    n = len(blocks)

    def body(*refs):
        ins, outs = refs[:n], refs[n:2 * n]
        ssem, rsem = refs[2 * n:]
        x, y, c, _ = _place()
        cps = []
        for w in range(n):
            hr = ins[w].shape[0] // 2
            rows = pl.ds(pl.multiple_of(c * hr, 8), hr)
            cp = _remote(ins[w].at[rows, :], outs[w].at[rows, :], ssem.at[w], rsem.at[w], (x, y, 1 - c))
            cp.start()
            cps.append(cp)
        for w in range(n):
            hr = ins[w].shape[0] // 2
            other = outs[w].at[pl.ds(pl.multiple_of((1 - c) * hr, 8), hr), :]
            _remote(other, other, ssem.at[w], rsem.at[w], (x, y, 1 - c)).wait_recv()
        for cp in cps:
            cp.wait_send()

    dma = pltpu.SemaphoreType.DMA
    return pl.pallas_call(
        body,
        name="pair_share",
        in_specs=[ANY] * n,
        out_specs=[ANY] * n,
        out_shape=[SDS(b.shape, b.dtype) for b in blocks],
        input_output_aliases={w: w for w in range(n)},
        scratch_shapes=[dma((n,)), dma((n,))],
    )(*blocks)


def _all_reduce_small(buf):
    rows = buf.shape[0]

    def body(in_ref, o_ref, slots, ssem, rsem):
        x, y, c, _ = _place()
        me = 4 * x + 2 * y + c
        slots[me] = in_ref[...]
        cps = []
        for r in range(1, 8):
            px = 1 - x if r & 4 else x
            py = 1 - y if r & 2 else y
            pc = 1 - c if r & 1 else c
            cp = _remote(in_ref, slots.at[me], ssem.at[r - 1], rsem.at[r - 1], (px, py, pc))
            cp.start()
            cps.append((cp, 4 * px + 2 * py + pc))
        for r, (cp, peer) in enumerate(cps):
            _remote(in_ref, slots.at[peer], ssem.at[r], rsem.at[r], (x, y, c)).wait_recv()
        s = slots[0]
        for d in range(1, 8):
            s = s + slots[d]
        o_ref[...] = s
        for cp, _ in cps:
            cp.wait_send()

    dma = pltpu.SemaphoreType.DMA
    return pl.pallas_call(
        body,
        name="all_reduce_small",
        in_specs=[pl.BlockSpec(memory_space=pltpu.VMEM)],
        out_specs=pl.BlockSpec(memory_space=pltpu.VMEM),
        out_shape=SDS(buf.shape, F32),
        scratch_shapes=[pltpu.VMEM((8, rows, LANES), F32), dma((7,)), dma((7,))],
        compiler_params=pltpu.CompilerParams(vmem_limit_bytes=VMEM_LIMIT),
    )(buf)


def _adam_fn(w, g, m, v):
    m2 = ADAM_B1 * m + (1.0 - ADAM_B1) * g
    v2 = ADAM_B2 * v + (1.0 - ADAM_B2) * (g * g)
    m_hat = m2 / (1.0 - ADAM_B1 ** ADAM_STEP)
    v_hat = v2 / (1.0 - ADAM_B2 ** ADAM_STEP)
    return (-ADAM_LR * (m_hat / (jnp.sqrt(v_hat) + ADAM_EPS) + ADAM_WD * w), m2, v2), ()


def _adamw(w, g, m, v, name, tr=256):
    cols = w.shape[1]
    return _rowwise(_adam_fn, [(w, cols, 0), (g, cols, 0), (m, cols, 0), (v, cols, 0)], [],
                    [(cols, F32)] * 3, [], tr=tr, name=name)


BIG = ("w_in", "w_glu", "w_branch", "w_out", "w_up", "w_down")
COL_SHARDED = ("w_in", "w_glu", "w_up")
SMALL = ("norm_mix_pre", "norm_mix_post", "norm_mlp_pre", "norm_mlp_post", "sinks", "lam_re", "lam_im", "log_dt",
         "b_re", "b_im", "c_re", "c_im", "d_skip")
WEIGHTS = ("norm_mix_pre", "norm_mix_post", "norm_mlp_pre", "norm_mlp_post", "w_in", "sinks", "lam_re", "lam_im",
           "log_dt", "b_re", "b_im", "c_re", "c_im", "d_skip", "w_glu", "w_branch", "w_out", "w_up", "w_down")


def _flat_small(vals, extra):
    flat = jnp.concatenate([vals[k].reshape(-1) for k in SMALL] + [extra.reshape(-1)])
    rows = -(-flat.shape[0] // (SUBLANES * LANES)) * SUBLANES
    return jnp.pad(flat, (0, rows * LANES - flat.shape[0])).reshape(rows, LANES)


def kernel(x, norm_mix_pre, norm_mix_post, norm_mlp_pre, norm_mlp_post, w_in, sinks, lam_re, lam_im, log_dt, b_re, b_im, c_re, c_im, d_skip, w_glu, w_branch, w_out, w_up, w_down, loss_target, m_norm_mix_pre, m_norm_mix_post, m_norm_mlp_pre, m_norm_mlp_post, m_w_in, m_sinks, m_lam_re, m_lam_im, m_log_dt, m_b_re, m_b_im, m_c_re, m_c_im, m_d_skip, m_w_glu, m_w_branch, m_w_out, m_w_up, m_w_down, v_norm_mix_pre, v_norm_mix_post, v_norm_mlp_pre, v_norm_mlp_post, v_w_in, v_sinks, v_lam_re, v_lam_im, v_log_dt, v_b_re, v_b_im, v_c_re, v_c_im, v_d_skip, v_w_glu, v_w_branch, v_w_out, v_w_up, v_w_down):
    w = dict(norm_mix_pre=norm_mix_pre, norm_mix_post=norm_mix_post, norm_mlp_pre=norm_mlp_pre, norm_mlp_post=norm_mlp_post,
             w_in=w_in, sinks=sinks, lam_re=lam_re, lam_im=lam_im, log_dt=log_dt, b_re=b_re, b_im=b_im, c_re=c_re,
             c_im=c_im, d_skip=d_skip, w_glu=w_glu, w_branch=w_branch, w_out=w_out, w_up=w_up, w_down=w_down)
    m = dict(norm_mix_pre=m_norm_mix_pre, norm_mix_post=m_norm_mix_post, norm_mlp_pre=m_norm_mlp_pre,
             norm_mlp_post=m_norm_mlp_post, w_in=m_w_in, sinks=m_sinks, lam_re=m_lam_re, lam_im=m_lam_im,
             log_dt=m_log_dt, b_re=m_b_re, b_im=m_b_im, c_re=m_c_re, c_im=m_c_im, d_skip=m_d_skip, w_glu=m_w_glu,
             w_branch=m_w_branch, w_out=m_w_out, w_up=m_w_up, w_down=m_w_down)
    v = dict(norm_mix_pre=v_norm_mix_pre, norm_mix_post=v_norm_mix_post, norm_mlp_pre=v_norm_mlp_pre,
             norm_mlp_post=v_norm_mlp_post, w_in=v_w_in, sinks=v_sinks, lam_re=v_lam_re, lam_im=v_lam_im,
             log_dt=v_log_dt, b_re=v_b_re, b_im=v_b_im, c_re=v_c_re, c_im=v_c_im, d_skip=v_d_skip, w_glu=v_w_glu,
             w_branch=v_w_branch, w_out=v_w_out, w_up=v_w_up, w_down=v_w_down)
    xi, yi, ci = lax.axis_index("x"), lax.axis_index("y"), lax.axis_index("c")

    k_arr = jnp.stack([2 * xi + yi]).astype(jnp.int32)
    slot = {k: _cast_into_slot(w[k][0], k_arr) for k in BIG}

    def whole(k, g4):
        if k in COL_SHARDED:
            return jnp.concatenate([g4[j] for j in range(4)], axis=1)
        return g4.reshape(4 * g4.shape[1], g4.shape[2])

    w_in_b = whole("w_in", _gather_weights([slot["w_in"]])[0])
    hosted = (("w_glu", "w_branch", "w_out"), ("w_up",), ("w_down",))
    comms = [_GatherComm([slot[k] for k in names]) for names in hosted]

    def late(*got):
        f = {k: whole(k, g4) for names, res in zip(hosted, got) for k, g4 in zip(names, res)}
        return f["w_glu"], f["w_branch"][:Q_W], f["w_branch"][Q_W:], f["w_out"], f["w_up"], f["w_down"]

    s5w = (lam_re[0], lam_im[0], log_dt[0], b_re[0], b_im[0], c_re[0], c_im[0], d_skip[0])
    loss_part, dx, small, big = _local_step(
        x[0], loss_target[0], (norm_mix_pre, norm_mix_post, norm_mlp_pre, norm_mlp_post),
        w_in_b[:, :ZA_W], w_in_b[:, ZA_W:], sinks, s5w, comms, late)

    g4s = []
    for k in BIG:
        g = big[k]
        if k in COL_SHARDED:
            g = g.reshape(g.shape[0], 4, g.shape[1] // 4).transpose(1, 0, 2)
        else:
            g = g.reshape(4, g.shape[0] // 4, g.shape[1])
        g4s.append(g)
    from_sib = _pair_exchange(g4s)
    c_arr = jnp.stack([ci]).astype(jnp.int32)
    kc_arr = jnp.stack([2 * xi + yi, ci]).astype(jnp.int32)
    g42 = [g.reshape(4, 2, g.shape[1] // 2, g.shape[2]) for g in g4s]
    psums = [_pair_sum(g, r, c_arr) for g, r in zip(g42, from_sib)]
    from_chips = _chip_exchange(psums)
    halves = [_chip_sum(g, r, q, kc_arr) for g, r, q in zip(g42, from_sib, from_chips)]
    grads = dict(zip(BIG, _pair_share(halves)))

    red = _all_reduce_small(_flat_small(small, loss_part)).reshape(-1)
    off = 0
    for k in SMALL:
        n = math.prod(w[k].shape)
        grads[k] = red[off:off + n].reshape(w[k].shape[1:])
        off += n
    loss = red[off]

    delta, new_m, new_v = {}, {}, {}
    for k in BIG:
        delta[k], new_m[k], new_v[k] = _adamw(w[k][0], grads[k], m[k][0], v[k][0], "adamw_" + k)
    zero = jnp.zeros((), F32)
    fw, fm, fv = (_flat_small({k: t[k] for k in SMALL}, zero) for t in (w, m, v))
    fg = _flat_small(grads, zero)
    sd, sm, sv = _adamw(fw, fg, fm, fv, "adamw_small", tr=fw.shape[0])
    off = 0
    for k in SMALL:
        n = math.prod(w[k].shape)
        delta[k], new_m[k], new_v[k] = (t.reshape(-1)[off:off + n].reshape(w[k].shape[1:]) for t in (sd, sm, sv))
        off += n

    lead = lambda t: t[None]
    return (loss, lead(dx), *[lead(grads[k]) for k in WEIGHTS], *[lead(delta[k]) for k in WEIGHTS],
            *[lead(new_m[k]) for k in WEIGHTS], *[lead(new_v[k]) for k in WEIGHTS])
```

```python
import functools
import math

import jax
import jax.numpy as jnp
from jax import lax
from jax.experimental import pallas as pl
from jax.experimental.pallas import tpu as pltpu

F32 = jnp.float32
BF16 = jnp.bfloat16
SDS = jax.ShapeDtypeStruct

D_MODEL = 2048
HEAD_DIM = 64
N_Q_HEADS = 16
ATT_BLOCK = 128
ROT_DIM = 16
ROPE_THETA = 500000.0
Q_W = 1024
KV_W = 128
SSM_W = 1024
SSM_G = 64
SSM_GC = 16
SSM_P = 64
N_STATE = SSM_G * SSM_P
LANES = 128
SUBLANES = 8
N_LG = N_STATE // LANES
N_JB = 8
LG_PER_JB = N_LG // N_JB
D_FF = 8192
ZA_W = Q_W + 2 * KV_W + SSM_W
EPS = 1e-6
S5_CHUNK = 512
S5_SEG = S5_CHUNK // SUBLANES
VMEM_LIMIT = 56 * 1024 * 1024
NEG = -1e30

ADAM_LR = 0.001
ADAM_B1 = 0.9
ADAM_B2 = 0.999
ADAM_EPS = 1e-08
ADAM_WD = 0.01
ADAM_STEP = 10

MESH = pl.DeviceIdType.MESH


def _cp(sem):
    return pltpu.CompilerParams(dimension_semantics=sem, vmem_limit_bytes=VMEM_LIMIT)


ANY = pl.BlockSpec(memory_space=pl.ANY)


def _place():
    x, y, c = lax.axis_index("x"), lax.axis_index("y"), lax.axis_index("c")
    others = [(1 - x, y), (x, 1 - y), (1 - x, 1 - y)]
    return x, y, c, others


def _remote(src, dst, ssem, rsem, to):
    return pltpu.make_async_remote_copy(src_ref=src, dst_ref=dst, send_sem=ssem, recv_sem=rsem,
                                        device_id=to, device_id_type=MESH)


class _GatherComm:
    aliased = True

    def __init__(self, slotted):
        self.arrs = list(slotted)
        self.n = len(self.arrs)
        dma = pltpu.SemaphoreType.DMA
        self.scratch = [dma((3 * self.n,)) for _ in range(4)]
        self.out_shape = [SDS(s.shape, s.dtype) for s in self.arrs]

    @staticmethod
    def _half(ref, hc):
        hr = ref.shape[1] // 2
        return pl.ds(pl.multiple_of(hc * hr, 16), hr)

    def start(self, ins, outs, sems):
        ssem, rsem, _, _ = sems
        x, y, c, others = _place()
        me = 2 * x + y
        for w in range(self.n):
            for r, (ox, oy) in enumerate(others):
                _remote(ins[w].at[me, self._half(ins[w], c), :], outs[w].at[me, self._half(ins[w], c), :],
                        ssem.at[3 * w + r], rsem.at[3 * w + r], (ox, oy, c)).start()

    def finish(self, ins, outs, sems):
        ssem, rsem, fs_sem, fr_sem = sems
        x, y, c, others = _place()
        me, sib = 2 * x + y, (x, y, 1 - c)
        passes = []
        for w in range(self.n):
            for r, (ox, oy) in enumerate(others):
                got = outs[w].at[2 * ox + oy, self._half(ins[w], c), :]
                _remote(got, got, ssem.at[3 * w + r], rsem.at[3 * w + r], (ox, oy, c)).wait_recv()
                cp = _remote(got, got, fs_sem.at[3 * w + r], fr_sem.at[3 * w + r], sib)
                cp.start()
                passes.append(cp)
        for w in range(self.n):
            for r, (ox, oy) in enumerate(others):
                got = outs[w].at[2 * ox + oy, self._half(ins[w], 1 - c), :]
                _remote(got, got, fs_sem.at[3 * w + r], fr_sem.at[3 * w + r], sib).wait_recv()
        for w in range(self.n):
            for r, (ox, oy) in enumerate(others):
                mine = ins[w].at[me, self._half(ins[w], c), :]
                _remote(mine, mine, ssem.at[3 * w + r], rsem.at[3 * w + r], (ox, oy, c)).wait_send()
        for cp in passes:
            cp.wait_send()


class _PairExchangeComm:
    aliased = False

    def __init__(self, grads):
        self.arrs = list(grads)
        self.n = len(self.arrs)
        dma = pltpu.SemaphoreType.DMA
        self.scratch = [dma((self.n,)), dma((self.n,))]
        self.out_shape = [SDS((4, g.shape[1] // 2, g.shape[2]), g.dtype) for g in self.arrs]

    def _copies(self, ins, outs, sems):
        ssem, rsem = sems
        x, y, c, _ = _place()
        cps = []
        for w in range(self.n):
            hr = ins[w].shape[1] // 2
            src = ins[w].at[:, pl.ds(pl.multiple_of((1 - c) * hr, 8), hr), :]
            cps.append(_remote(src, outs[w], ssem.at[w], rsem.at[w], (x, y, 1 - c)))
        return cps

    def start(self, ins, outs, sems):
        for cp in self._copies(ins, outs, sems):
            cp.start()

    def finish(self, ins, outs, sems):
        for cp in self._copies(ins, outs, sems):
            cp.wait()


class _ChipExchangeComm:
    aliased = False

    def __init__(self, psums):
        self.arrs = list(psums)
        self.n = len(self.arrs)
        dma = pltpu.SemaphoreType.DMA
        self.scratch = [dma((3 * self.n,)), dma((3 * self.n,))]
        self.out_shape = [SDS((3,) + p.shape[1:], p.dtype) for p in self.arrs]

    def _copies(self, ins, outs, sems):
        ssem, rsem = sems
        x, y, c, others = _place()
        return [_remote(ins[w].at[2 * ox + oy], outs[w].at[r], ssem.at[3 * w + r], rsem.at[3 * w + r], (ox, oy, c))
                for w in range(self.n) for r, (ox, oy) in enumerate(others)]

    def start(self, ins, outs, sems):
        for cp in self._copies(ins, outs, sems):
            cp.start()

    def finish(self, ins, outs, sems):
        for cp in self._copies(ins, outs, sems):
            cp.wait()


def _comm_only(name, comm):
    n = comm.n

    def body(*refs):
        ins, outs, sems = refs[:n], refs[n:2 * n], refs[2 * n:]
        comm.start(ins, outs, sems)
        comm.finish(ins, outs, sems)

    return pl.pallas_call(
        body, name=name, in_specs=[ANY] * n, out_specs=[ANY] * n, out_shape=comm.out_shape,
        input_output_aliases={w: w for w in range(n)} if comm.aliased else {},
        scratch_shapes=comm.scratch)(*comm.arrs)


def _call(name, body, grid, in_specs, out_specs, out_shape, scratch, dims, args, comm=None):
    if comm is None:
        return pl.pallas_call(body, name=name, grid=grid, in_specs=in_specs, out_specs=out_specs, out_shape=out_shape,
                              scratch_shapes=scratch, compiler_params=_cp(dims))(*args)
    ni, no, ns, n = len(in_specs), len(out_shape), len(scratch), comm.n

    def hosted(*refs):
        ins, cin = refs[:ni], refs[ni:ni + n]
        outs, cout = refs[ni + n:ni + n + no], refs[ni + n + no:ni + 2 * n + no]
        scr, sems = refs[ni + 2 * n + no:ni + 2 * n + no + ns], refs[ni + 2 * n + no + ns:]
        ids = [pl.program_id(d) for d in range(len(grid))]
        first = functools.reduce(jnp.logical_and, [i == 0 for i in ids])
        last = functools.reduce(jnp.logical_and, [i == g - 1 for i, g in zip(ids, grid)])

        @pl.when(first)
        def _():
            comm.start(cin, cout, sems)

        body(*ins, *outs, *scr)

        @pl.when(last)
        def _():
            comm.finish(cin, cout, sems)

    return pl.pallas_call(
        hosted, name=name, grid=grid, in_specs=list(in_specs) + [ANY] * n, out_specs=list(out_specs) + [ANY] * n,
        out_shape=list(out_shape) + comm.out_shape,
        input_output_aliases={ni + w: no + w for w in range(n)} if comm.aliased else {},
        scratch_shapes=list(scratch) + comm.scratch, compiler_params=_cp(("arbitrary",) * len(grid)))(*args, *comm.arrs)


def _mm(a, b, *, mode, out_dtype, tm, tn, tk, name, a_fn=None, epi=None, extras=(), comm=None, shard_cols=None):
    if mode == "nn":
        (M, K), (K2, N) = a.shape, b.shape
    elif mode == "nt":
        (M, K), (N, K2) = a.shape, b.shape
    else:
        (K, M), (K2, N) = a.shape, b.shape
    assert K == K2, (a.shape, b.shape, mode)
    tm, tn, tk = min(tm, M), min(tn, N), min(tk, K)
    assert M % tm == 0 and N % tn == 0 and K % tk == 0, (M, N, K, tm, tn, tk)
    nk = K // tk
    if mode == "tn":
        a_spec = pl.BlockSpec((tk, tm), lambda i, j, k: (k, i))
        ca = 0
    else:
        a_spec = pl.BlockSpec((tm, tk), lambda i, j, k: (i, k))
        ca = 1
    if mode == "nt":
        b_spec = pl.BlockSpec((tn, tk), lambda i, j, k: (j, k))
        cb = 1
    else:
        b_spec = pl.BlockSpec((tk, tn), lambda i, j, k: (k, j))
        cb = 0
    dims = (((ca,), (cb,)), ((), ()))
    ne = len(extras)

    def body(a_ref, b_ref, *rest):
        ex = rest[:ne]
        o_ref = rest[ne]
        av = a_ref[...]
        if a_fn is not None:
            av = a_fn(av.astype(F32))
        p = lax.dot_general(av.astype(BF16), b_ref[...].astype(BF16), dims, preferred_element_type=F32)

        def fin(v):
            if epi is not None:
                v = epi(v, *[e[...] for e in ex])
            o_ref[...] = v.astype(out_dtype).reshape(o_ref.shape)

        if nk == 1:
            fin(p)
        else:
            acc = rest[ne + 1]
            k = pl.program_id(2)

            @pl.when(k == 0)
            def _():
                acc[...] = p

            @pl.when(k > 0)
            def _():
                acc[...] += p

            @pl.when(k == nk - 1)
            def _():
                fin(acc[...])

    if shard_cols is None:
        o_spec, o_shape = pl.BlockSpec((tm, tn), lambda i, j, k: (i, j)), SDS((M, N), out_dtype)
    else:
        per = shard_cols // tn
        assert shard_cols % tn == 0 and N % shard_cols == 0
        o_spec = pl.BlockSpec((1, tm, tn), lambda i, j, k: (lax.div(j, per), i, lax.rem(j, per)))
        o_shape = SDS((N // shard_cols, M, shard_cols), out_dtype)
    res = _call(name, body, (M // tm, N // tn, nk),
                [a_spec, b_spec] + [pl.BlockSpec((tm, tn), lambda i, j, k: (i, j)) for _ in extras],
                [o_spec], [o_shape],
                [pltpu.VMEM((tm, tn), F32)] if nk > 1 else [], ("parallel", "parallel", "arbitrary"),
                (a, b, *extras), comm)
    return res[0] if comm is None else (res[0], res[1:])


def _rowwise(fn, rows, bcasts, outs, accs, *, tr, name):
    T = rows[0][0].shape[0]
    tr = min(tr, T)
    assert T % tr == 0
    nr, nb, no, na = len(rows), len(bcasts), len(outs), len(accs)
    in_specs = [pl.BlockSpec((tr, w), functools.partial(lambda i, c: (i, c), c=cb)) for (_, w, cb) in rows]
    in_specs += [pl.BlockSpec(b.shape, lambda i: (0, 0)) for b in bcasts]
    out_shape = [SDS((T, w), dt) for (w, dt) in outs] + [SDS(s, F32) for s in accs]
    out_specs = [pl.BlockSpec((tr, w), lambda i: (i, 0)) for (w, _) in outs]
    out_specs += [pl.BlockSpec(s, lambda i: (0, 0)) for s in accs]

    def body(*refs):
        ins = [r[...] for r in refs[:nr + nb]]
        o_refs = refs[nr + nb:nr + nb + no]
        a_refs = refs[nr + nb + no:]
        ro, ao = fn(*ins)
        for r, v in zip(o_refs, ro):
            r[...] = v.astype(r.dtype)
        if na:
            @pl.when(pl.program_id(0) == 0)
            def _():
                for r in a_refs:
                    r[...] = jnp.zeros(r.shape, F32)

            for r, v in zip(a_refs, ao):
                r[...] += v

    res = pl.pallas_call(
        body,
        name=name,
        grid=(T // tr,),
        in_specs=in_specs,
        out_specs=out_specs,
        out_shape=out_shape,
        compiler_params=_cp(("arbitrary",) if na else ("parallel",)),
    )(*[r[0] for r in rows], *bcasts)
    return res


def _rms(v):
    r = lax.rsqrt(jnp.mean(v * v, axis=-1, keepdims=True) + EPS)
    return v * r, r


def _rms_bwd(dy, xn, r, g):
    dxn = dy * g
    dv = r * (dxn - xn * jnp.mean(dxn * xn, axis=-1, keepdims=True))
    return dv, jnp.sum(dy * xn, axis=0, keepdims=True)


def _sig(v):
    return 1.0 / (1.0 + jnp.exp(-v))


_GELU_C = math.sqrt(2.0 / math.pi)


def _gelu(v):
    return 0.5 * v * (1.0 + jnp.tanh(_GELU_C * (v + 0.044715 * v * v * v)))


def _gelu_grad(v):
    t = jnp.tanh(_GELU_C * (v + 0.044715 * v * v * v))
    return 0.5 * (1.0 + t) + 0.5 * v * (1.0 - t * t) * _GELU_C * (1.0 + 3.0 * 0.044715 * v * v)


def _rope(v, c, s, sign):
    w = v.shape[1]
    m = lax.broadcasted_iota(jnp.int32, v.shape, 1) % HEAD_DIM
    p = jnp.where(m < ROT_DIM // 2, -pltpu.roll(v, w - ROT_DIM // 2, 1), pltpu.roll(v, ROT_DIM // 2, 1))
    return v * c + sign * (p * s)


def _rope_tables(T):
    half = ROT_DIM // 2
    inv = ROPE_THETA ** (-jnp.arange(half, dtype=F32) * 2.0 / ROT_DIM)
    ang = jnp.arange(T).astype(F32)[:, None] * inv[None, :]
    cos, sin = jnp.cos(ang), jnp.sin(ang)
    one = jnp.ones((T, HEAD_DIM - ROT_DIM), F32)
    c64 = jnp.concatenate([cos, cos, one], axis=1)
    s64 = jnp.concatenate([sin, sin, 0.0 * one], axis=1)
    return jnp.tile(c64, (1, 2)), jnp.tile(s64, (1, 2))


def _dup_half(m, lo):
    lane = lax.broadcasted_iota(jnp.int32, m.shape, 1)
    sw = pltpu.roll(m, HEAD_DIM, 1)
    return jnp.where(lane < HEAD_DIM, m, sw) if lo else jnp.where(lane >= HEAD_DIM, m, sw)


def _attn_mask(i):
    qi = lax.broadcasted_iota(jnp.int32, (ATT_BLOCK, 2 * ATT_BLOCK), 0)
    kj = lax.broadcasted_iota(jnp.int32, (ATT_BLOCK, 2 * ATT_BLOCK), 1)
    rel = qi + ATT_BLOCK - kj
    return (rel >= 0) & (rel < ATT_BLOCK) & ((kj >= ATT_BLOCK) | (i > 0))


_NT = (((1,), (1,)), ((), ()))
_TN = (((0,), (0,)), ((), ()))


def _attn_fwd(za, cos, sin, sinks, comm=None):
    T = za.shape[0]
    nb = T // ATT_BLOCK
    kvb = Q_W // (2 * KV_W)

    def body(sink_ref, q_ref, kvp_ref, kvc_ref, cc_ref, sc_ref, cp_ref, sp_ref, o_ref):
        i = pl.program_id(0)
        cc, sc, cp, sp = cc_ref[...], sc_ref[...], cp_ref[...], sp_ref[...]
        q = (_rope(q_ref[...], jnp.tile(cc, (1, 8)), jnp.tile(sc, (1, 8)), 1.0) * 0.125).astype(BF16)
        kvp, kvc = kvp_ref[...], kvc_ref[...]
        k = jnp.concatenate([_rope(kvp[:, :KV_W], cp, sp, 1.0), _rope(kvc[:, :KV_W], cc, sc, 1.0)], axis=0).astype(BF16)
        v = jnp.concatenate([kvp[:, KV_W:], kvc[:, KV_W:]], axis=0).astype(BF16)
        ok = _attn_mask(i)
        lane = lax.broadcasted_iota(jnp.int32, (ATT_BLOCK, LANES), 1)
        for kvh in range(2):
            k2 = _dup_half(k, kvh == 0)
            v2 = _dup_half(v, kvh == 0)
            for pair in range(4):
                c0 = (kvh * 4 + pair) * LANES
                q2 = q[:, c0:c0 + LANES]
                halves = []
                for hf in range(2):
                    sink = sink_ref[0, 2 * (kvh * 4 + pair) + hf]
                    qm = jnp.where((lane < HEAD_DIM) == (hf == 0), q2, jnp.zeros_like(q2))
                    s = lax.dot_general(qm, k2, _NT, preferred_element_type=F32)
                    s = jnp.where(ok, s, NEG)
                    m = jnp.maximum(jnp.max(s, axis=1, keepdims=True), sink)
                    e = jnp.exp(s - m)
                    den = jnp.sum(e, axis=1, keepdims=True) + jnp.exp(sink - m)
                    p = (e * (1.0 / den)).astype(BF16)
                    halves.append(jnp.dot(p, v2, preferred_element_type=F32))
                o_ref[:, c0:c0 + LANES] = jnp.where(lane < HEAD_DIM, halves[0], halves[1]).astype(BF16)

    blk = lambda w, f: pl.BlockSpec((ATT_BLOCK, w), f)
    res = _call(
        "attn_fwd", body, (nb,),
        [
            pl.BlockSpec(memory_space=pltpu.SMEM),
            blk(Q_W, lambda i: (i, 0)),
            blk(2 * KV_W, lambda i: (jnp.maximum(i - 1, 0), kvb)),
            blk(2 * KV_W, lambda i: (i, kvb)),
            blk(LANES, lambda i: (i, 0)),
            blk(LANES, lambda i: (i, 0)),
            blk(LANES, lambda i: (jnp.maximum(i - 1, 0), 0)),
            blk(LANES, lambda i: (jnp.maximum(i - 1, 0), 0)),
        ],
        [blk(Q_W, lambda i: (i, 0))], [SDS((T, Q_W), BF16)], [], ("parallel",),
        (sinks, za, za, za, cos, sin, cos, sin), comm)
    return res[0] if comm is None else (res[0], res[1:])


def _attn_bwd(za, cos, sin, sinks, o, do):
    T = za.shape[0]
    nb = T // ATT_BLOCK
    kvb = Q_W // (2 * KV_W)

    def body(sink_ref, q_ref, kvp_ref, kvc_ref, cc_ref, sc_ref, cp_ref, sp_ref, o_ref, do_ref,
             dq_ref, dkv_ref, dsk_ref, carry, dqs):
        i = pl.program_id(0)

        @pl.when(i == 0)
        def _():
            carry[...] = jnp.zeros(carry.shape, F32)
            dsk_ref[...] = jnp.zeros(dsk_ref.shape, F32)

        @pl.when(i < nb)
        def _():
            cc, sc, cp, sp = cc_ref[...], sc_ref[...], cp_ref[...], sp_ref[...]
            ccq, scq = jnp.tile(cc, (1, 8)), jnp.tile(sc, (1, 8))
            q = (_rope(q_ref[...], ccq, scq, 1.0) * 0.125).astype(BF16)
            kvp, kvc = kvp_ref[...], kvc_ref[...]
            k = jnp.concatenate([_rope(kvp[:, :KV_W], cp, sp, 1.0), _rope(kvc[:, :KV_W], cc, sc, 1.0)], axis=0).astype(BF16)
            v = jnp.concatenate([kvp[:, KV_W:], kvc[:, KV_W:]], axis=0).astype(BF16)
            ok = _attn_mask(i)
            lane = lax.broadcasted_iota(jnp.int32, (ATT_BLOCK, LANES), 1)
            lane_kv = lax.broadcasted_iota(jnp.int32, (2 * ATT_BLOCK, LANES), 1)
            lane_s = lax.broadcasted_iota(jnp.int32, (1, LANES), 1)
            dsk = jnp.zeros((1, LANES), F32)
            dk_h, dv_h = [], []
            for kvh in range(2):
                k2 = _dup_half(k, kvh == 0)
                v2 = _dup_half(v, kvh == 0)
                dk2 = jnp.zeros((2 * ATT_BLOCK, LANES), F32)
                dv2 = jnp.zeros((2 * ATT_BLOCK, LANES), F32)
                for pair in range(4):
                    c0 = (kvh * 4 + pair) * LANES
                    q2 = q[:, c0:c0 + LANES]
                    do2 = do_ref[:, c0:c0 + LANES]
                    prod = do2.astype(F32) * o_ref[:, c0:c0 + LANES].astype(F32)
                    dqh = []
                    for hf in range(2):
                        h = 2 * (kvh * 4 + pair) + hf
                        sink = sink_ref[0, h]
                        sel = (lane < HEAD_DIM) == (hf == 0)
                        qm = jnp.where(sel, q2, jnp.zeros_like(q2))
                        dom = jnp.where(sel, do2, jnp.zeros_like(do2))
                        delta = jnp.sum(jnp.where(sel, prod, 0.0), axis=1, keepdims=True)
                        s = lax.dot_general(qm, k2, _NT, preferred_element_type=F32)
                        s = jnp.where(ok, s, NEG)
                        m = jnp.maximum(jnp.max(s, axis=1, keepdims=True), sink)
                        e = jnp.exp(s - m)
                        inv = 1.0 / (jnp.sum(e, axis=1, keepdims=True) + jnp.exp(sink - m))
                        p = e * inv
                        dsk = dsk + jnp.where(lane_s == h, -jnp.sum(jnp.exp(sink - m) * inv * delta), 0.0)
                        dp = lax.dot_general(dom, v2, _NT, preferred_element_type=F32)
                        ds = (p * (dp - delta)).astype(BF16)
                        dqh.append(jnp.dot(ds, k2, preferred_element_type=F32))
                        dk2 = dk2 + lax.dot_general(ds, qm, _TN, preferred_element_type=F32)
                        dv2 = dv2 + lax.dot_general(p.astype(BF16), dom, _TN, preferred_element_type=F32)
                    dqs[:, c0:c0 + LANES] = jnp.where(lane < HEAD_DIM, dqh[0], dqh[1]) * 0.125
                dk_h.append(dk2 + pltpu.roll(dk2, HEAD_DIM, 1))
                dv_h.append(dv2 + pltpu.roll(dv2, HEAD_DIM, 1))
            dk = jnp.where(lane_kv < HEAD_DIM, dk_h[0], dk_h[1])
            dv = jnp.where(lane_kv < HEAD_DIM, dv_h[0], dv_h[1])
            dq_ref[...] = _rope(dqs[...], ccq, scq, -1.0).astype(dq_ref.dtype)
            dkp = _rope(dk[:ATT_BLOCK], cp, sp, -1.0)
            dkc = _rope(dk[ATT_BLOCK:], cc, sc, -1.0)
            dkv_ref[...] = (carry[...] + jnp.concatenate([dkp, dv[:ATT_BLOCK]], axis=1)).astype(dkv_ref.dtype)
            carry[...] = jnp.concatenate([dkc, dv[ATT_BLOCK:]], axis=1)
            dsk_ref[...] += dsk

        @pl.when(i == nb)
        def _():
            dkv_ref[...] = carry[...].astype(dkv_ref.dtype)

    blk = lambda w, f: pl.BlockSpec((ATT_BLOCK, w), f)
    cur = lambda i: jnp.minimum(i, nb - 1)
    prv = lambda i: jnp.maximum(jnp.minimum(i, nb - 1) - 1, 0)
    return pl.pallas_call(
        body,
        name="attn_bwd",
        grid=(nb + 1,),
        in_specs=[
            pl.BlockSpec(memory_space=pltpu.SMEM),
            blk(Q_W, lambda i: (cur(i), 0)),
            blk(2 * KV_W, lambda i: (prv(i), kvb)),
            blk(2 * KV_W, lambda i: (cur(i), kvb)),
            blk(LANES, lambda i: (cur(i), 0)),
            blk(LANES, lambda i: (cur(i), 0)),
            blk(LANES, lambda i: (prv(i), 0)),
            blk(LANES, lambda i: (prv(i), 0)),
            blk(Q_W, lambda i: (cur(i), 0)),
            blk(Q_W, lambda i: (cur(i), 0)),
        ],
        out_specs=[
            blk(Q_W, lambda i: (cur(i), 0)),
            blk(2 * KV_W, lambda i: (jnp.maximum(i - 1, 0), 0)),
            pl.BlockSpec((1, LANES), lambda i: (0, 0)),
        ],
        out_shape=[SDS((T, Q_W), BF16), SDS((T, 2 * KV_W), BF16), SDS((1, LANES), F32)],
        scratch_shapes=[pltpu.VMEM((ATT_BLOCK, 2 * KV_W), F32), pltpu.VMEM((ATT_BLOCK, Q_W), F32)],
        compiler_params=_cp(("arbitrary",)),
    )(sinks, za, za, za, cos, sin, cos, sin, o, do)


def _s5_discretize(lam_re, lam_im, log_dt, b_re, b_im):
    dt = jnp.exp(log_dt)[:, None]
    mag = jnp.exp(lam_re * dt)
    a_re, a_im = mag * jnp.cos(lam_im * dt), mag * jnp.sin(lam_im * dt)
    den = lam_re * lam_re + lam_im * lam_im
    nr, ni = a_re - 1.0, a_im
    coef_re = (nr * lam_re + ni * lam_im) / den
    coef_im = (ni * lam_re - nr * lam_im) / den
    bb_re = coef_re[..., None] * b_re - coef_im[..., None] * b_im
    bb_im = coef_re[..., None] * b_im + coef_im[..., None] * b_re
    return a_re, a_im, bb_re, bb_im


def _blockdiag_in(bb):
    x = bb.reshape(N_JB, 8, SSM_P, SSM_GC).transpose(0, 1, 3, 2)
    return (x[:, :, :, None, :] * jnp.eye(8, dtype=bb.dtype)[None, :, None, :, None]).reshape(N_JB, 128, 512)


def _blockdiag_in_extract(m):
    x = m.reshape(N_JB, 8, SSM_GC, 8, SSM_P)
    x = jnp.einsum('jgchp,gh->jgcp', x, jnp.eye(8, dtype=m.dtype))
    return x.transpose(0, 1, 3, 2).reshape(SSM_G, SSM_P, SSM_GC)


def _blockdiag_out(c):
    x = c.reshape(N_JB, 8, SSM_GC, SSM_P).transpose(0, 1, 3, 2)
    return (x[:, :, :, None, :] * jnp.eye(8, dtype=c.dtype)[None, :, None, :, None]).reshape(N_JB, 512, 128)


def _blockdiag_out_extract(m):
    x = m.reshape(N_JB, 8, SSM_P, 8, SSM_GC)
    x = jnp.einsum('jgphc,gh->jgpc', x, jnp.eye(8, dtype=m.dtype))
    return x.transpose(0, 1, 3, 2).reshape(SSM_G, SSM_GC, SSM_P)


def _s5_tables(a_re, a_im):
    ar, ai = a_re.reshape(N_LG, 1, LANES), a_im.reshape(N_LG, 1, LANES)
    p_re, p_im, n = ar, ai, 1
    while n < S5_SEG:
        tr, ti = p_re[:, n - 1:n], p_im[:, n - 1:n]
        p_re, p_im = (jnp.concatenate([p_re, p_re * tr - p_im * ti], axis=1),
                      jnp.concatenate([p_im, p_re * ti + p_im * tr], axis=1))
        n *= 2
    bc = lambda v: jnp.broadcast_to(v, (N_LG, SUBLANES, LANES))
    return p_re, p_im, bc(ar), bc(ai)


def _s5_to_time_major(src_ref, dst_ref):
    for t in range(S5_SEG):
        dst_ref[t * SUBLANES:(t + 1) * SUBLANES, :] = src_ref[pl.ds(t, SUBLANES, stride=S5_SEG), :]


def _s5_from_time_major(val, dst_ref):
    for t in range(S5_SEG):
        dst_ref[pl.ds(t, SUBLANES, stride=S5_SEG), :] = val[t * SUBLANES:(t + 1) * SUBLANES, :]


def _tm_rows(t, row0=0):
    return pl.ds(pl.multiple_of(t * SUBLANES + row0, SUBLANES), SUBLANES)


def _s5_scan(src_re, src_im, dst_re, dst_im, ar, ai, reverse, dst_row0=0):
    def step(n, carry):
        t = (S5_SEG - 1 - n) if reverse else n
        out = []
        for ll in range(LG_PER_JB):
            xr, xi = carry[2 * ll], carry[2 * ll + 1]
            idx = (ll, _tm_rows(t), slice(None))
            odx = (ll, _tm_rows(t, dst_row0), slice(None))
            nr = ar[ll] * xr - ai[ll] * xi + src_re[idx]
            ni = ar[ll] * xi + ai[ll] * xr + src_im[idx]
            dst_re[odx] = nr
            dst_im[odx] = ni
            out += [nr, ni]
        return tuple(out)
    z = jnp.zeros((SUBLANES, LANES), F32)
    return lax.fori_loop(0, S5_SEG, step, (z,) * (2 * LG_PER_JB))


def _s5_fixup(ends, in_re, in_im, mr, mi, s_re, s_im, reverse):
    cr, ci = in_re, in_im
    order = range(SUBLANES - 1, -1, -1) if reverse else range(SUBLANES)
    for s in order:
        s_re[:, s:s + 1, :] = cr
        s_im[:, s:s + 1, :] = ci
        er = jnp.stack([ends[2 * ll][s:s + 1, :] for ll in range(LG_PER_JB)])
        ei = jnp.stack([ends[2 * ll + 1][s:s + 1, :] for ll in range(LG_PER_JB)])
        cr, ci = mr * cr - mi * ci + er, mr * ci + mi * cr + ei
    return cr, ci


def _s5_correct(x_re, x_im, s_re, s_im, p_re, p_im, row0=0):
    sr = [s_re[ll] for ll in range(LG_PER_JB)]
    si = [s_im[ll] for ll in range(LG_PER_JB)]

    def step(t, carry):
        for ll in range(LG_PER_JB):
            idx = (ll, _tm_rows(t, row0), slice(None))
            pr, pi = p_re[ll, pl.ds(t, 1), :], p_im[ll, pl.ds(t, 1), :]
            x_re[idx] = x_re[idx] + (pr * sr[ll] - pi * si[ll])
            x_im[idx] = x_im[idx] + (pr * si[ll] + pi * sr[ll])
        return carry
    lax.fori_loop(0, S5_SEG, step, 0)


def _s5_specs(nc, rev):
    cidx = (lambda c: nc - 1 - c) if rev else (lambda c: c)
    jb = lambda shape: pl.BlockSpec(shape, lambda j, c: (j, 0, 0))
    return cidx, [
        jb((1, LANES, 8 * LANES)),
        jb((1, 8 * LANES, LANES)),
        pl.BlockSpec((1, LANES), lambda j, c: (0, j)),
        jb((LG_PER_JB, SUBLANES, LANES)), jb((LG_PER_JB, SUBLANES, LANES)),
        jb((LG_PER_JB, 1, LANES)), jb((LG_PER_JB, 1, LANES)),
        jb((LG_PER_JB, S5_SEG, LANES)), jb((LG_PER_JB, S5_SEG, LANES)),
    ]


def _s5_fwd(za, prm, comm=None):
    T = za.shape[0]
    R = S5_CHUNK
    nc = T // R
    ub = (Q_W + 2 * KV_W) // LANES
    _, pspecs = _s5_specs(nc, False)

    def body(u_ref, b_ref, c_ref, d_ref, are_ref, aim_ref, alr_ref, ali_ref, pr_ref, pi_ref,
             yg_ref, x0r_ref, x0i_ref, bur, bui, xsr, xsi, sr, si, xcr, xci, utm, ynat):
        c = pl.program_id(1)

        @pl.when(c == 0)
        def _():
            xcr[...] = jnp.zeros(xcr.shape, F32)
            xci[...] = jnp.zeros(xci.shape, F32)

        _s5_to_time_major(u_ref, utm)
        u = utm[...]
        ub16 = u.astype(BF16)
        bu = jnp.dot(ub16, b_ref[0].astype(BF16), preferred_element_type=F32)
        for ll in range(LG_PER_JB):
            bur[ll] = bu[:, ll * LANES:(ll + 1) * LANES]
            bui[ll] = bu[:, (LG_PER_JB + ll) * LANES:(LG_PER_JB + ll + 1) * LANES]
        ar = [are_ref[ll] for ll in range(LG_PER_JB)]
        ai = [aim_ref[ll] for ll in range(LG_PER_JB)]
        ends = _s5_scan(bur, bui, xsr, xsi, ar, ai, False)
        in_r, in_i = xcr[...], xci[...]
        x0r_ref[0] = in_r
        x0i_ref[0] = in_i
        out_r, out_i = _s5_fixup(ends, in_r, in_i, alr_ref[...], ali_ref[...], sr, si, False)
        xcr[...] = out_r
        xci[...] = out_i
        _s5_correct(xsr, xsi, sr, si, pr_ref, pi_ref)
        xcat = jnp.concatenate([xsr[ll].astype(BF16) for ll in range(LG_PER_JB)]
                               + [xsi[ll].astype(BF16) for ll in range(LG_PER_JB)], axis=1)
        y = d_ref[...] * u + jnp.dot(xcat, c_ref[0].astype(BF16), preferred_element_type=F32)
        _s5_from_time_major(_gelu(y), ynat)
        yg_ref[...] = ynat[...].astype(BF16)

    st = pl.BlockSpec((1, LG_PER_JB, 1, LANES), lambda j, c: (c, j, 0, 0))
    vm = lambda rows: pltpu.VMEM((LG_PER_JB, rows, LANES), F32)
    res = _call(
        "s5_fwd", body, (N_JB, nc),
        [pl.BlockSpec((R, LANES), lambda j, c: (c, ub + j))] + pspecs,
        [pl.BlockSpec((R, LANES), lambda j, c: (c, j)), st, st],
        [SDS((T, SSM_W), BF16), SDS((nc, N_LG, 1, LANES), F32), SDS((nc, N_LG, 1, LANES), F32)],
        [vm(R), vm(R), vm(R), vm(R), vm(SUBLANES), vm(SUBLANES), vm(1), vm(1),
         pltpu.VMEM((R, LANES), F32), pltpu.VMEM((R, LANES), F32)],
        ("parallel", "arbitrary"), (za, *prm), comm)
    return res if comm is None else (res[:3], res[3:])


def _s5_bwd(za, dyg, x0r, x0i, prm, prev_tables, comm=None):
    T = za.shape[0]
    R = S5_CHUNK
    nc = T // R
    ub = (Q_W + 2 * KV_W) // LANES
    cidx, pspecs = _s5_specs(nc, True)
    PAD = SUBLANES

    def body(u_ref, dyg_ref, x0r_ref, x0i_ref, b_ref, c_ref, d_ref, are_ref, aim_ref,
             alr_ref, ali_ref, pr_ref, pi_ref, qr_ref, qi_ref,
             du_ref, dar_ref, dai_ref, db_ref, dc_ref, dd_ref,
             bur, bui, xsr, xsi, sr, si, gcr, gci, utm, dtm, dunat):
        c = pl.program_id(1)

        @pl.when(c == 0)
        def _():
            gcr[...] = jnp.zeros(gcr.shape, F32)
            gci[...] = jnp.zeros(gci.shape, F32)
            dar_ref[...] = jnp.zeros(dar_ref.shape, F32)
            dai_ref[...] = jnp.zeros(dai_ref.shape, F32)
            db_ref[...] = jnp.zeros(db_ref.shape, F32)
            dc_ref[...] = jnp.zeros(dc_ref.shape, F32)
            dd_ref[...] = jnp.zeros(dd_ref.shape, F32)

        _s5_to_time_major(u_ref, utm)
        _s5_to_time_major(dyg_ref, dtm)
        u = utm[...]
        ub16 = u.astype(BF16)
        bcat, ccat = b_ref[0].astype(BF16), c_ref[0].astype(BF16)
        lanes = lambda v, ll: v[:, ll * LANES:(ll + 1) * LANES]
        bu = jnp.dot(ub16, bcat, preferred_element_type=F32)
        for ll in range(LG_PER_JB):
            bur[ll] = lanes(bu, ll)
            bui[ll] = lanes(bu, LG_PER_JB + ll)
        ar = [are_ref[ll] for ll in range(LG_PER_JB)]
        ai = [aim_ref[ll] for ll in range(LG_PER_JB)]
        ends = _s5_scan(bur, bui, xsr, xsi, ar, ai, False, dst_row0=PAD)
        in_r, in_i = x0r_ref[0], x0i_ref[0]
        _s5_fixup(ends, in_r, in_i, alr_ref[...], ali_ref[...], sr, si, False)
        _s5_correct(xsr, xsi, sr, si, pr_ref, pi_ref, PAD)
        xsr[:, 0:PAD, :] = sr[...]
        xsi[:, 0:PAD, :] = si[...]
        xcat = jnp.concatenate([xsr[ll, PAD:, :].astype(BF16) for ll in range(LG_PER_JB)]
                               + [xsi[ll, PAD:, :].astype(BF16) for ll in range(LG_PER_JB)], axis=1)
        y = d_ref[...] * u + jnp.dot(xcat, ccat, preferred_element_type=F32)
        dy = dtm[...] * _gelu_grad(y)
        dyb = dy.astype(BF16)
        dd_ref[...] += jnp.sum(dy * u, axis=0, keepdims=True)
        du = d_ref[...] * dy
        dc_ref[0] += lax.dot_general(xcat, dyb, _TN, preferred_element_type=F32)
        g = lax.dot_general(dyb, ccat, _NT, preferred_element_type=F32)
        for ll in range(LG_PER_JB):
            bur[ll] = lanes(g, ll)
            bui[ll] = lanes(g, LG_PER_JB + ll)
        ends = _s5_scan(bur, bui, bur, bui, ar, [-v for v in ai], True)
        out_r, out_i = _s5_fixup(ends, gcr[...], gci[...], alr_ref[...], -ali_ref[...], sr, si, True)
        gcr[...] = out_r
        gci[...] = out_i
        _s5_correct(bur, bui, sr, si, qr_ref, qi_ref)
        for ll in range(LG_PER_JB):
            gr, gi = bur[ll], bui[ll]
            xpr, xpi = xsr[ll, 0:R, :], xsi[ll, 0:R, :]
            red = lambda v: v.reshape(R // SUBLANES, SUBLANES, LANES).sum(axis=0)
            dar_ref[ll] += red(xpr * gr + xpi * gi)
            dai_ref[ll] += red(xpr * gi - xpi * gr)
        gcat = jnp.concatenate([bur[ll].astype(BF16) for ll in range(LG_PER_JB)]
                               + [bui[ll].astype(BF16) for ll in range(LG_PER_JB)], axis=1)
        db_ref[0] += lax.dot_general(ub16, gcat, _TN, preferred_element_type=F32)
        du = du + lax.dot_general(gcat, bcat, _NT, preferred_element_type=F32)
        _s5_from_time_major(du, dunat)
        du_ref[...] = dunat[...].astype(du_ref.dtype)

    st = pl.BlockSpec((1, LG_PER_JB, 1, LANES), lambda j, c: (cidx(c), j, 0, 0))
    jb = lambda shape: pl.BlockSpec(shape, lambda j, c: (j, 0, 0))
    vm = lambda rows: pltpu.VMEM((LG_PER_JB, rows, LANES), F32)
    res = _call(
        "s5_bwd", body, (N_JB, nc),
        [pl.BlockSpec((R, LANES), lambda j, c: (cidx(c), ub + j)),
         pl.BlockSpec((R, LANES), lambda j, c: (cidx(c), j)), st, st] + pspecs
        + [jb((LG_PER_JB, S5_SEG, LANES)), jb((LG_PER_JB, S5_SEG, LANES))],
        [pl.BlockSpec((R, LANES), lambda j, c: (cidx(c), j)),
         jb((LG_PER_JB, SUBLANES, LANES)), jb((LG_PER_JB, SUBLANES, LANES)),
         jb((1, LANES, 8 * LANES)), jb((1, 8 * LANES, LANES)),
         pl.BlockSpec((1, LANES), lambda j, c: (0, j))],
        [SDS((T, SSM_W), BF16), SDS((N_LG, SUBLANES, LANES), F32), SDS((N_LG, SUBLANES, LANES), F32),
         SDS((N_JB, LANES, 8 * LANES), F32), SDS((N_JB, 8 * LANES, LANES), F32), SDS((1, SSM_W), F32)],
        [vm(R), vm(R), vm(R + PAD), vm(R + PAD), vm(SUBLANES), vm(SUBLANES), vm(1), vm(1)]
        + [pltpu.VMEM((R, LANES), F32)] * 3,
        ("parallel", "arbitrary"), (za, dyg, x0r, x0i, *prm, *prev_tables), comm)
    return res if comm is None else (res[:6], res[6:])


def _local_step(x, target, gains, w_a, w_g, sinks, s5w, comms, late, red=None):
    T = x.shape[0]
    D = D_MODEL
    g1, g2, g3, g4 = gains
    cos, sin = _rope_tables(T)
    lam_re, lam_im, log_dt, b_re, b_im, c_re, c_im, d_skip = s5w
    (a_re, a_im, bb_re, bb_im), disc_vjp = jax.vjp(_s5_discretize, lam_re, lam_im, log_dt, b_re, b_im)
    p_re, p_im, abr, abi = _s5_tables(a_re, a_im)
    prm = (jnp.concatenate([_blockdiag_in(bb_re), _blockdiag_in(bb_im)], axis=2),
           jnp.concatenate([_blockdiag_out(c_re), -_blockdiag_out(c_im)], axis=1),
           d_skip.reshape(1, SSM_W), abr, abi, p_re[:, S5_SEG - 1:, :], p_im[:, S5_SEG - 1:, :], p_re, p_im)
    rev_tables = (p_re[:, ::-1, :], -p_im[:, ::-1, :])
    mm = functools.partial(_mm, tm=512, tn=1024, tk=2048)

    h = _rowwise(lambda xv, g: ((_rms(xv)[0] * g,), ()), [(x, D, 0)], [g1], [(D, BF16)], [], tr=512, name="norm1")[0]
    za = _mm(h, w_a, mode="nn", out_dtype=F32, tm=512, tn=1152, tk=2048, name="mm_za")
    unpack = lambda res, comm: (res, ()) if comm is None else res
    zg, got0 = unpack(mm(h, w_g, mode="nn", out_dtype=F32, name="mm_zg", comm=comms[0]), comms[0])
    o_attn, got1 = unpack(_attn_fwd(za, cos, sin, sinks, comm=comms[1]), comms[1])
    (yg, x0r, x0i), got2 = unpack(_s5_fwd(za, prm, comm=comms[2]), comms[2])
    w_glu, w_ba, w_bs, w_out, w_up, w_down = late(got0, got1, got2)
    zglu = mm(yg, w_glu, mode="nn", out_dtype=F32, name="mm_glu")
    o_ssm = _rowwise(lambda z1, z2: ((z1 * _sig(z2),), ()), [(zglu, SSM_W, 0), (zglu, SSM_W, 1)], [],
                     [(SSM_W, BF16)], [], tr=512, name="glu")[0]
    ya = mm(o_attn, w_ba, mode="nn", out_dtype=F32, name="mm_ya")
    ys = mm(o_ssm, w_bs, mode="nn", out_dtype=F32, name="mm_ys")
    mi = _rowwise(lambda ga, gs, a, s: ((_sig(ga) * a + _sig(gs) * s,), ()),
                  [(zg, D, 0), (zg, D, 1), (ya, D, 0), (ys, D, 0)], [], [(D, BF16)], [], tr=256, name="gate")[0]
    mixed = mm(mi, w_out, mode="nn", out_dtype=F32, name="mm_out")

    def f_post(xv, mv, g2v, g3v):
        x1v = xv + _rms(mv)[0] * g2v
        return (x1v, _rms(x1v)[0] * g3v), ()
    x1, h2 = _rowwise(f_post, [(x, D, 0), (mixed, D, 0)], [g2, g3], [(D, F32), (D, BF16)], [], tr=256, name="post_mix")
    act = mm(h2, w_up, mode="nn", out_dtype=BF16, name="mm_up", epi=lambda v: jnp.maximum(v, 0.0))
    f = _mm(act, w_down, mode="nn", out_dtype=F32, tm=512, tn=2048, tk=2048, name="mm_down", a_fn=lambda v: v * v)

    def f_final(x1v, fv, tv, g4v):
        fn, r = _rms(fv)
        e = x1v + fn * g4v - tv
        dx2v = e * (1.0 / D)
        dfv, dg4v = _rms_bwd(dx2v, fn, r, g4v)
        return (dfv, dx2v), (dg4v, jnp.zeros((SUBLANES, LANES), F32) + 0.5 * jnp.sum(e * e) * (1.0 / D))
    df, dx2, dg4, lossb = _rowwise(f_final, [(x1, D, 0), (f, D, 0), (target, D, 0)], [g4],
                                   [(D, BF16), (D, F32)], [(1, D), (SUBLANES, LANES)], tr=256, name="final")

    big = {}

    def add(k, g4):
        big[k] = g4
        if red is not None:
            red.add(k, g4)

    def hosted(fn, stage, names):
        if red is None:
            return fn(comm=None)
        out, got = fn(comm=getattr(red, stage)(names))
        getattr(red, stage + "_done")(names, got)
        return out

    dpre = mm(df, w_down, mode="nt", out_dtype=BF16, name="mm_dact", epi=lambda v, a: v * (2.0 * a.astype(F32)), extras=(act,))
    wg = functools.partial(_mm, mode="tn", out_dtype=F32, tm=1024, tn=1024, tk=2048)
    add("w_down", wg(act, df, name="wg_down", a_fn=lambda v: v * v).reshape(4, D_FF // 4, D))
    dh2 = hosted(functools.partial(_mm, dpre, w_up, mode="nt", out_dtype=F32, tm=512, tn=2048, tk=2048, name="mm_dh2"),
                 "s1", ["w_down"])
    add("w_up", hosted(functools.partial(wg, h2, dpre, name="wg_up", shard_cols=D_FF // 4), "s3", ["w_down"]))

    def f_mid(dx2v, dh2v, x1v, mv, g2v, g3v):
        x1n, r3 = _rms(x1v)
        d3, dg3v = _rms_bwd(dh2v, x1n, r3, g3v)
        dx1v = dx2v + d3
        mn, r2 = _rms(mv)
        dmv, dg2v = _rms_bwd(dx1v, mn, r2, g2v)
        return (dx1v, dmv), (dg3v, dg2v)
    dx1, dmixed, dg3, dg2 = _rowwise(f_mid, [(dx2, D, 0), (dh2, D, 0), (x1, D, 0), (mixed, D, 0)], [g2, g3],
                                     [(D, F32), (D, BF16)], [(1, D), (1, D)], tr=256, name="mid")

    dmi = hosted(functools.partial(mm, dmixed, w_out, mode="nt", out_dtype=F32, name="mm_dmi"), "s1", ["w_up"])
    add("w_out", wg(mi, dmixed, name="wg_out").reshape(4, D // 4, D))

    def f_gate(dv, ga, gs, a, s):
        sa, ss = _sig(ga), _sig(gs)
        return (dv * sa, dv * ss, jnp.concatenate([dv * a * sa * (1.0 - sa), dv * s * ss * (1.0 - ss)], axis=1)), ()
    dya, dys, dzg = _rowwise(f_gate, [(dmi, D, 0), (zg, D, 0), (zg, D, 1), (ya, D, 0), (ys, D, 0)], [],
                             [(D, BF16), (D, BF16), (2 * D, BF16)], [], tr=256, name="gate_bwd")
    do_attn = hosted(functools.partial(mm, dya, w_ba, mode="nt", out_dtype=BF16, name="mm_doa"), "s1", ["w_out"])
    d_w_ba = wg(o_attn, dya, name="wg_ba")
    do_ssm = mm(dys, w_bs, mode="nt", out_dtype=F32, name="mm_dos")
    d_w_bs = wg(o_ssm, dys, name="wg_bs")
    add("w_branch", jnp.concatenate([d_w_ba.reshape(2, D // 4, D), d_w_bs.reshape(2, D // 4, D)], axis=0))

    def f_glu(dv, z1, z2):
        s2 = _sig(z2)
        return (jnp.concatenate([dv * s2, dv * z1 * s2 * (1.0 - s2)], axis=1),), ()
    dzglu = _rowwise(f_glu, [(do_ssm, SSM_W, 0), (zglu, SSM_W, 0), (zglu, SSM_W, 1)], [], [(2 * SSM_W, BF16)], [],
                     tr=512, name="glu_bwd")[0]
    dyg = hosted(functools.partial(mm, dzglu, w_glu, mode="nt", out_dtype=F32, name="mm_dyg"), "s1", ["w_branch"])
    add("w_glu", wg(yg, dzglu, name="wg_glu", tn=SSM_W // 2, shard_cols=SSM_W // 2))
    du, dar, dai, dbc, dcc, ddv = hosted(functools.partial(_s5_bwd, za, dyg, x0r, x0i, prm, rev_tables),
                                         "s3", ["w_up", "w_out", "w_branch"])
    dbr, dbi = dbc[:, :, :4 * LANES], dbc[:, :, 4 * LANES:]
    dcr, dci = dcc[:, :4 * LANES, :], -dcc[:, 4 * LANES:, :]
    dq, dkv, dsk = _attn_bwd(za, cos, sin, sinks, o_attn, do_attn)
    dza = jnp.concatenate([dq, dkv, du], axis=1)
    dh = mm(dza, w_a, mode="nt", out_dtype=F32, name="mm_dh_a", tk=ZA_W)
    dh = _mm(dzg, w_g, mode="nt", out_dtype=F32, tm=512, tn=1024, tk=2048, name="mm_dh_g", epi=lambda v, p: v + p, extras=(dh,))
    d_w_a = _mm(h, dza, mode="tn", out_dtype=F32, tm=1024, tn=ZA_W // 2, tk=2048, name="wg_a")
    d_w_g = wg(h, dzg, name="wg_g")
    d_w_in = jnp.concatenate([d_w_a, d_w_g], axis=1)
    add("w_in", d_w_in.reshape(D, 4, d_w_in.shape[1] // 4).transpose(1, 0, 2))

    def f_first(dx1v, dhv, xv, g1v):
        xn, r1 = _rms(xv)
        d1, dg1v = _rms_bwd(dhv, xn, r1, g1v)
        return (dx1v + d1,), (dg1v,)
    dx, dg1 = _rowwise(f_first, [(dx1, D, 0), (dh, D, 0), (x, D, 0)], [g1], [(D, F32)], [(1, D)], tr=256, name="first")

    da_re = dar.sum(axis=1).reshape(SSM_G, SSM_P)
    da_im = dai.sum(axis=1).reshape(SSM_G, SSM_P)
    d_lam_re, d_lam_im, d_log_dt, d_b_re, d_b_im = disc_vjp(
        (da_re, da_im, _blockdiag_in_extract(dbr), _blockdiag_in_extract(dbi)))
    small = dict(norm_mix_pre=dg1, norm_mix_post=dg2, norm_mlp_pre=dg3, norm_mlp_post=dg4,
                 sinks=dsk[:, :N_Q_HEADS], lam_re=d_lam_re, lam_im=d_lam_im, log_dt=d_log_dt,
                 b_re=d_b_re, b_im=d_b_im, c_re=_blockdiag_out_extract(dcr), c_im=_blockdiag_out_extract(dci),
                 d_skip=ddv.reshape(SSM_G, SSM_GC))
    return lossb[0, 0], dx, small, big


def _cast_into_slot(w, k_arr):
    rows, cols = w.shape
    tr = 256

    def body(k_ref, w_ref, o_ref):
        o_ref[0] = w_ref[...].astype(BF16)

    return pl.pallas_call(
        body,
        name="cast_into_slot",
        grid_spec=pltpu.PrefetchScalarGridSpec(
            num_scalar_prefetch=1,
            grid=(rows // tr,),
            in_specs=[pl.BlockSpec((tr, cols), lambda i, k: (i, 0))],
            out_specs=pl.BlockSpec((1, tr, cols), lambda i, k: (k[0], i, 0)),
        ),
        out_shape=SDS((4, rows, cols), BF16),
        compiler_params=_cp(("parallel",)),
    )(k_arr, w)


def _pair_sum(g, r, c_arr):
    _, _, hr, cols = g.shape
    tr = min(256, hr)

    def body(c_ref, g_ref, r_ref, o_ref):
        o_ref[0] = (g_ref[0, 0] + r_ref[0]).astype(BF16)

    return pl.pallas_call(
        body,
        name="pair_sum",
        grid_spec=pltpu.PrefetchScalarGridSpec(
            num_scalar_prefetch=1,
            grid=(4, hr // tr),
            in_specs=[pl.BlockSpec((1, 1, tr, cols), lambda k, i, c_ref: (k, c_ref[0], i, 0)),
                      pl.BlockSpec((1, tr, cols), lambda k, i, c_ref: (k, i, 0))],
            out_specs=pl.BlockSpec((1, tr, cols), lambda k, i, c_ref: (k, i, 0)),
        ),
        out_shape=SDS((4, hr, cols), BF16),
        compiler_params=_cp(("parallel", "parallel")),
    )(c_arr, g, r)


def _chip_sum(g, r, q, kc_arr):
    _, _, hr, cols = g.shape
    tr = min(256, hr)

    def body(kc_ref, g_ref, r_ref, q_ref, o_ref):
        s = g_ref[0, 0] + r_ref[0]
        for j in range(3):
            s = s + q_ref[j].astype(F32)
        o_ref[...] = s

    return pl.pallas_call(
        body,
        name="chip_sum",
        grid_spec=pltpu.PrefetchScalarGridSpec(
            num_scalar_prefetch=1,
            grid=(hr // tr,),
            in_specs=[pl.BlockSpec((1, 1, tr, cols), lambda i, kc: (kc[0], kc[1], i, 0)),
                      pl.BlockSpec((1, tr, cols), lambda i, kc: (kc[0], i, 0)),
                      pl.BlockSpec((3, tr, cols), lambda i, kc: (0, i, 0))],
            out_specs=pl.BlockSpec((tr, cols), lambda i, kc: (kc[1] * (hr // tr) + i, 0)),
        ),
        out_shape=SDS((2 * hr, cols), F32),
        compiler_params=_cp(("parallel",)),
    )(kc_arr, g, r, q)


def _pair_share(blocks):
    n = len(blocks)

    def body(*refs):
        ins, outs = refs[:n], refs[n:2 * n]
        ssem, rsem = refs[2 * n:]
        x, y, c, _ = _place()
        cps = []
        for w in range(n):
            hr = ins[w].shape[0] // 2
            rows = pl.ds(pl.multiple_of(c * hr, 8), hr)
            cp = _remote(ins[w].at[rows, :], outs[w].at[rows, :], ssem.at[w], rsem.at[w], (x, y, 1 - c))
            cp.start()
            cps.append(cp)
        for w in range(n):
            hr = ins[w].shape[0] // 2
            other = outs[w].at[pl.ds(pl.multiple_of((1 - c) * hr, 8), hr), :]
            _remote(other, other, ssem.at[w], rsem.at[w], (x, y, 1 - c)).wait_recv()
        for cp in cps:
            cp.wait_send()

    dma = pltpu.SemaphoreType.DMA
    return pl.pallas_call(
        body,
        name="pair_share",
        in_specs=[ANY] * n,
        out_specs=[ANY] * n,
        out_shape=[SDS(b.shape, b.dtype) for b in blocks],
        input_output_aliases={w: w for w in range(n)},
        scratch_shapes=[dma((n,)), dma((n,))],
    )(*blocks)


class _GradReducer:
    def __init__(self, c_arr, kc_arr):
        self.c_arr, self.kc_arr = c_arr, kc_arr
        self.g, self.r, self.ps, self.q = {}, {}, {}, {}

    def add(self, k, g4):
        self.g[k] = g4.reshape(4, 2, g4.shape[1] // 2, g4.shape[2])

    def s1(self, names):
        return _PairExchangeComm([self.g[k].reshape(4, -1, self.g[k].shape[3]) for k in names])

    def s1_done(self, names, got):
        for k, r in zip(names, got):
            self.r[k] = r
            self.ps[k] = _pair_sum(self.g[k], r, self.c_arr)

    def s3(self, names):
        return _ChipExchangeComm([self.ps[k] for k in names])

    def s3_done(self, names, got):
        self.q.update(zip(names, got))

    def finish(self, order):
        rest = [k for k in order if k not in self.r]
        if rest:
            self.s1_done(rest, _comm_only("pair_exchange", self.s1(rest)))
        rest = [k for k in order if k not in self.q]
        if rest:
            self.s3_done(rest, _comm_only("chip_exchange", self.s3(rest)))
        blocks = [_chip_sum(self.g[k], self.r[k], self.q[k], self.kc_arr) for k in order]
        return dict(zip(order, _pair_share(blocks)))


def _all_reduce_small(buf):
    rows = buf.shape[0]

    def body(in_ref, o_ref, slots, ssem, rsem):
        x, y, c, _ = _place()
        me = 4 * x + 2 * y + c
        slots[me] = in_ref[...]
        cps = []
        for r in range(1, 8):
            px = 1 - x if r & 4 else x
            py = 1 - y if r & 2 else y
            pc = 1 - c if r & 1 else c
            cp = _remote(in_ref, slots.at[me], ssem.at[r - 1], rsem.at[r - 1], (px, py, pc))
            cp.start()
            cps.append((cp, 4 * px + 2 * py + pc))
        for r, (cp, peer) in enumerate(cps):
            _remote(in_ref, slots.at[peer], ssem.at[r], rsem.at[r], (x, y, c)).wait_recv()
        s = slots[0]
        for d in range(1, 8):
            s = s + slots[d]
        o_ref[...] = s
        for cp, _ in cps:
            cp.wait_send()

    dma = pltpu.SemaphoreType.DMA
    return pl.pallas_call(
        body,
        name="all_reduce_small",
        in_specs=[pl.BlockSpec(memory_space=pltpu.VMEM)],
        out_specs=pl.BlockSpec(memory_space=pltpu.VMEM),
        out_shape=SDS(buf.shape, F32),
        scratch_shapes=[pltpu.VMEM((8, rows, LANES), F32), dma((7,)), dma((7,))],
        compiler_params=pltpu.CompilerParams(vmem_limit_bytes=VMEM_LIMIT),
    )(buf)


def _adam_fn(w, g, m, v):
    m2 = ADAM_B1 * m + (1.0 - ADAM_B1) * g
    v2 = ADAM_B2 * v + (1.0 - ADAM_B2) * (g * g)
    m_hat = m2 / (1.0 - ADAM_B1 ** ADAM_STEP)
    v_hat = v2 / (1.0 - ADAM_B2 ** ADAM_STEP)
    return (-ADAM_LR * (m_hat / (jnp.sqrt(v_hat) + ADAM_EPS) + ADAM_WD * w), m2, v2), ()


def _adamw(w, g, m, v, name, tr=256):
    cols = w.shape[1]
    return _rowwise(_adam_fn, [(w, cols, 0), (g, cols, 0), (m, cols, 0), (v, cols, 0)], [],
                    [(cols, F32)] * 3, [], tr=tr, name=name)


BIG = ("w_in", "w_glu", "w_branch", "w_out", "w_up", "w_down")
COL_SHARDED = ("w_in", "w_glu", "w_up")
SMALL = ("norm_mix_pre", "norm_mix_post", "norm_mlp_pre", "norm_mlp_post", "sinks", "lam_re", "lam_im", "log_dt",
         "b_re", "b_im", "c_re", "c_im", "d_skip")
WEIGHTS = ("norm_mix_pre", "norm_mix_post", "norm_mlp_pre", "norm_mlp_post", "w_in", "sinks", "lam_re", "lam_im",
           "log_dt", "b_re", "b_im", "c_re", "c_im", "d_skip", "w_glu", "w_branch", "w_out", "w_up", "w_down")


def _flat_small(vals, extra):
    flat = jnp.concatenate([vals[k].reshape(-1) for k in SMALL] + [extra.reshape(-1)])
    rows = -(-flat.shape[0] // (SUBLANES * LANES)) * SUBLANES
    return jnp.pad(flat, (0, rows * LANES - flat.shape[0])).reshape(rows, LANES)


def kernel(x, norm_mix_pre, norm_mix_post, norm_mlp_pre, norm_mlp_post, w_in, sinks, lam_re, lam_im, log_dt, b_re, b_im, c_re, c_im, d_skip, w_glu, w_branch, w_out, w_up, w_down, loss_target, m_norm_mix_pre, m_norm_mix_post, m_norm_mlp_pre, m_norm_mlp_post, m_w_in, m_sinks, m_lam_re, m_lam_im, m_log_dt, m_b_re, m_b_im, m_c_re, m_c_im, m_d_skip, m_w_glu, m_w_branch, m_w_out, m_w_up, m_w_down, v_norm_mix_pre, v_norm_mix_post, v_norm_mlp_pre, v_norm_mlp_post, v_w_in, v_sinks, v_lam_re, v_lam_im, v_log_dt, v_b_re, v_b_im, v_c_re, v_c_im, v_d_skip, v_w_glu, v_w_branch, v_w_out, v_w_up, v_w_down):
    w = dict(norm_mix_pre=norm_mix_pre, norm_mix_post=norm_mix_post, norm_mlp_pre=norm_mlp_pre, norm_mlp_post=norm_mlp_post,
             w_in=w_in, sinks=sinks, lam_re=lam_re, lam_im=lam_im, log_dt=log_dt, b_re=b_re, b_im=b_im, c_re=c_re,
             c_im=c_im, d_skip=d_skip, w_glu=w_glu, w_branch=w_branch, w_out=w_out, w_up=w_up, w_down=w_down)
    m = dict(norm_mix_pre=m_norm_mix_pre, norm_mix_post=m_norm_mix_post, norm_mlp_pre=m_norm_mlp_pre,
             norm_mlp_post=m_norm_mlp_post, w_in=m_w_in, sinks=m_sinks, lam_re=m_lam_re, lam_im=m_lam_im,
             log_dt=m_log_dt, b_re=m_b_re, b_im=m_b_im, c_re=m_c_re, c_im=m_c_im, d_skip=m_d_skip, w_glu=m_w_glu,
             w_branch=m_w_branch, w_out=m_w_out, w_up=m_w_up, w_down=m_w_down)
    v = dict(norm_mix_pre=v_norm_mix_pre, norm_mix_post=v_norm_mix_post, norm_mlp_pre=v_norm_mlp_pre,
             norm_mlp_post=v_norm_mlp_post, w_in=v_w_in, sinks=v_sinks, lam_re=v_lam_re, lam_im=v_lam_im,
             log_dt=v_log_dt, b_re=v_b_re, b_im=v_b_im, c_re=v_c_re, c_im=v_c_im, d_skip=v_d_skip, w_glu=v_w_glu,
             w_branch=v_w_branch, w_out=v_w_out, w_up=v_w_up, w_down=v_w_down)
    xi, yi, ci = lax.axis_index("x"), lax.axis_index("y"), lax.axis_index("c")

    k_arr = jnp.stack([2 * xi + yi]).astype(jnp.int32)
    slot = {k: _cast_into_slot(w[k][0], k_arr) for k in BIG}

    def whole(k, g4):
        if k in COL_SHARDED:
            return jnp.concatenate([g4[j] for j in range(4)], axis=1)
        return g4.reshape(4 * g4.shape[1], g4.shape[2])

    w_in_b = whole("w_in", _comm_only("gather_w_in", _GatherComm([slot["w_in"]]))[0])
    hosted = (("w_glu", "w_branch", "w_out"), ("w_up",), ("w_down",))
    comms = [_GatherComm([slot[k] for k in names]) for names in hosted]

    def late(*got):
        f = {k: whole(k, g4) for names, res in zip(hosted, got) for k, g4 in zip(names, res)}
        return f["w_glu"], f["w_branch"][:Q_W], f["w_branch"][Q_W:], f["w_out"], f["w_up"], f["w_down"]

    s5w = (lam_re[0], lam_im[0], log_dt[0], b_re[0], b_im[0], c_re[0], c_im[0], d_skip[0])
    reducer = _GradReducer(jnp.stack([ci]).astype(jnp.int32), jnp.stack([2 * xi + yi, ci]).astype(jnp.int32))
    loss_part, dx, small, _ = _local_step(
        x[0], loss_target[0], (norm_mix_pre, norm_mix_post, norm_mlp_pre, norm_mlp_post),
        w_in_b[:, :ZA_W], w_in_b[:, ZA_W:], sinks, s5w, comms, late, reducer)
    grads = reducer.finish(BIG)

    red = _all_reduce_small(_flat_small(small, loss_part)).reshape(-1)
    off = 0
    for k in SMALL:
        n = math.prod(w[k].shape)
        grads[k] = red[off:off + n].reshape(w[k].shape[1:])
        off += n
    loss = red[off]

    delta, new_m, new_v = {}, {}, {}
    for k in BIG:
        delta[k], new_m[k], new_v[k] = _adamw(w[k][0], grads[k], m[k][0], v[k][0], "adamw_" + k)
    zero = jnp.zeros((), F32)
    fw, fm, fv = (_flat_small({k: t[k] for k in SMALL}, zero) for t in (w, m, v))
    fg = _flat_small(grads, zero)
    sd, sm, sv = _adamw(fw, fg, fm, fv, "adamw_small", tr=fw.shape[0])
    off = 0
    for k in SMALL:
        n = math.prod(w[k].shape)
        delta[k], new_m[k], new_v[k] = (t.reshape(-1)[off:off + n].reshape(w[k].shape[1:]) for t in (sd, sm, sv))
        off += n

    lead = lambda t: t[None]
    return (loss, lead(dx), *[lead(grads[k]) for k in WEIGHTS], *[lead(delta[k]) for k in WEIGHTS],
            *[lead(new_m[k]) for k in WEIGHTS], *[lead(new_v[k]) for k in WEIGHTS])
```

```python
import functools
import math

import jax
import jax.numpy as jnp
from jax import lax
from jax.experimental import pallas as pl
from jax.experimental.pallas import tpu as pltpu

F32 = jnp.float32
BF16 = jnp.bfloat16
SDS = jax.ShapeDtypeStruct

D_MODEL = 2048
HEAD_DIM = 64
N_Q_HEADS = 16
ATT_BLOCK = 128
ROT_DIM = 16
ROPE_THETA = 500000.0
Q_W = 1024
KV_W = 128
SSM_W = 1024
SSM_G = 64
SSM_GC = 16
SSM_P = 64
N_STATE = SSM_G * SSM_P
LANES = 128
SUBLANES = 8
N_LG = N_STATE // LANES
N_JB = 8
LG_PER_JB = N_LG // N_JB
D_FF = 8192
ZA_W = Q_W + 2 * KV_W + SSM_W
EPS = 1e-6
S5_CHUNK = 512
S5_SEG = S5_CHUNK // SUBLANES
VMEM_LIMIT = 56 * 1024 * 1024
NEG = -1e30

ADAM_LR = 0.001
ADAM_B1 = 0.9
ADAM_B2 = 0.999
ADAM_EPS = 1e-08
ADAM_WD = 0.01
ADAM_STEP = 10

MESH = pl.DeviceIdType.MESH


def _cp(sem):
    return pltpu.CompilerParams(dimension_semantics=sem, vmem_limit_bytes=VMEM_LIMIT)


ANY = pl.BlockSpec(memory_space=pl.ANY)


def _place():
    x, y, c = lax.axis_index("x"), lax.axis_index("y"), lax.axis_index("c")
    others = [(1 - x, y), (x, 1 - y), (1 - x, 1 - y)]
    return x, y, c, others


def _remote(src, dst, ssem, rsem, to):
    return pltpu.make_async_remote_copy(src_ref=src, dst_ref=dst, send_sem=ssem, recv_sem=rsem,
                                        device_id=to, device_id_type=MESH)


class _GatherComm:
    aliased = True

    def __init__(self, slotted):
        self.arrs = list(slotted)
        self.n = len(self.arrs)
        dma = pltpu.SemaphoreType.DMA
        self.scratch = [dma((3 * self.n,)) for _ in range(4)]
        self.out_shape = [SDS(s.shape, s.dtype) for s in self.arrs]

    @staticmethod
    def _half(ref, hc):
        hr = ref.shape[1] // 2
        return pl.ds(pl.multiple_of(hc * hr, 16), hr)

    def start(self, ins, outs, sems):
        ssem, rsem, _, _ = sems
        x, y, c, others = _place()
        me = 2 * x + y
        for w in range(self.n):
            for r, (ox, oy) in enumerate(others):
                _remote(ins[w].at[me, self._half(ins[w], c), :], outs[w].at[me, self._half(ins[w], c), :],
                        ssem.at[3 * w + r], rsem.at[3 * w + r], (ox, oy, c)).start()

    def finish(self, ins, outs, sems):
        ssem, rsem, fs_sem, fr_sem = sems
        x, y, c, others = _place()
        me, sib = 2 * x + y, (x, y, 1 - c)
        passes = []
        for w in range(self.n):
            for r, (ox, oy) in enumerate(others):
                got = outs[w].at[2 * ox + oy, self._half(ins[w], c), :]
                _remote(got, got, ssem.at[3 * w + r], rsem.at[3 * w + r], (ox, oy, c)).wait_recv()
                cp = _remote(got, got, fs_sem.at[3 * w + r], fr_sem.at[3 * w + r], sib)
                cp.start()
                passes.append(cp)
        for w in range(self.n):
            for r, (ox, oy) in enumerate(others):
                got = outs[w].at[2 * ox + oy, self._half(ins[w], 1 - c), :]
                _remote(got, got, fs_sem.at[3 * w + r], fr_sem.at[3 * w + r], sib).wait_recv()
        for w in range(self.n):
            for r, (ox, oy) in enumerate(others):
                mine = ins[w].at[me, self._half(ins[w], c), :]
                _remote(mine, mine, ssem.at[3 * w + r], rsem.at[3 * w + r], (ox, oy, c)).wait_send()
        for cp in passes:
            cp.wait_send()


class _PairExchangeComm:
    aliased = False

    def __init__(self, grads):
        self.arrs = list(grads)
        self.n = len(self.arrs)
        dma = pltpu.SemaphoreType.DMA
        self.scratch = [dma((self.n,)), dma((self.n,))]
        self.out_shape = [SDS((4, g.shape[1] // 2, g.shape[2]), g.dtype) for g in self.arrs]

    def _copies(self, ins, outs, sems):
        ssem, rsem = sems
        x, y, c, _ = _place()
        cps = []
        for w in range(self.n):
            hr = ins[w].shape[1] // 2
            src = ins[w].at[:, pl.ds(pl.multiple_of((1 - c) * hr, 8), hr), :]
            cps.append(_remote(src, outs[w], ssem.at[w], rsem.at[w], (x, y, 1 - c)))
        return cps

    def start(self, ins, outs, sems):
        for cp in self._copies(ins, outs, sems):
            cp.start()

    def finish(self, ins, outs, sems):
        for cp in self._copies(ins, outs, sems):
            cp.wait()


class _ChipExchangeComm:
    aliased = False

    def __init__(self, psums):
        self.arrs = list(psums)
        self.n = len(self.arrs)
        dma = pltpu.SemaphoreType.DMA
        self.scratch = [dma((3 * self.n,)), dma((3 * self.n,))]
        self.out_shape = [SDS((3,) + p.shape[1:], p.dtype) for p in self.arrs]

    def _copies(self, ins, outs, sems):
        ssem, rsem = sems
        x, y, c, others = _place()
        return [_remote(ins[w].at[2 * ox + oy], outs[w].at[r], ssem.at[3 * w + r], rsem.at[3 * w + r], (ox, oy, c))
                for w in range(self.n) for r, (ox, oy) in enumerate(others)]

    def start(self, ins, outs, sems):
        for cp in self._copies(ins, outs, sems):
            cp.start()

    def finish(self, ins, outs, sems):
        for cp in self._copies(ins, outs, sems):
            cp.wait()


def _comm_only(name, comm):
    n = comm.n

    def body(*refs):
        ins, outs, sems = refs[:n], refs[n:2 * n], refs[2 * n:]
        comm.start(ins, outs, sems)
        comm.finish(ins, outs, sems)

    return pl.pallas_call(
        body, name=name, in_specs=[ANY] * n, out_specs=[ANY] * n, out_shape=comm.out_shape,
        input_output_aliases={w: w for w in range(n)} if comm.aliased else {},
        scratch_shapes=comm.scratch)(*comm.arrs)


def _call(name, body, grid, in_specs, out_specs, out_shape, scratch, dims, args, comm=None):
    if comm is None:
        return pl.pallas_call(body, name=name, grid=grid, in_specs=in_specs, out_specs=out_specs, out_shape=out_shape,
                              scratch_shapes=scratch, compiler_params=_cp(dims))(*args)
    ni, no, ns, n = len(in_specs), len(out_shape), len(scratch), comm.n

    def hosted(*refs):
        ins, cin = refs[:ni], refs[ni:ni + n]
        outs, cout = refs[ni + n:ni + n + no], refs[ni + n + no:ni + 2 * n + no]
        scr, sems = refs[ni + 2 * n + no:ni + 2 * n + no + ns], refs[ni + 2 * n + no + ns:]
        ids = [pl.program_id(d) for d in range(len(grid))]
        first = functools.reduce(jnp.logical_and, [i == 0 for i in ids])
        last = functools.reduce(jnp.logical_and, [i == g - 1 for i, g in zip(ids, grid)])

        @pl.when(first)
        def _():
            comm.start(cin, cout, sems)

        body(*ins, *outs, *scr)

        @pl.when(last)
        def _():
            comm.finish(cin, cout, sems)

    return pl.pallas_call(
        hosted, name=name, grid=grid, in_specs=list(in_specs) + [ANY] * n, out_specs=list(out_specs) + [ANY] * n,
        out_shape=list(out_shape) + comm.out_shape,
        input_output_aliases={ni + w: no + w for w in range(n)} if comm.aliased else {},
        scratch_shapes=list(scratch) + comm.scratch, compiler_params=_cp(("arbitrary",) * len(grid)))(*args, *comm.arrs)


def _mm(a, b, *, mode, out_dtype, tm, tn, tk, name, a_fn=None, epi=None, extras=(), comm=None, shard_cols=None):
    if mode == "nn":
        (M, K), (K2, N) = a.shape, b.shape
    elif mode == "nt":
        (M, K), (N, K2) = a.shape, b.shape
    else:
        (K, M), (K2, N) = a.shape, b.shape
    assert K == K2, (a.shape, b.shape, mode)
    tm, tn, tk = min(tm, M), min(tn, N), min(tk, K)
    assert M % tm == 0 and N % tn == 0 and K % tk == 0, (M, N, K, tm, tn, tk)
    nk = K // tk
    if mode == "tn":
        a_spec = pl.BlockSpec((tk, tm), lambda i, j, k: (k, i))
        ca = 0
    else:
        a_spec = pl.BlockSpec((tm, tk), lambda i, j, k: (i, k))
        ca = 1
    if mode == "nt":
        b_spec = pl.BlockSpec((tn, tk), lambda i, j, k: (j, k))
        cb = 1
    else:
        b_spec = pl.BlockSpec((tk, tn), lambda i, j, k: (k, j))
        cb = 0
    dims = (((ca,), (cb,)), ((), ()))
    ne = len(extras)

    def body(a_ref, b_ref, *rest):
        ex = rest[:ne]
        o_ref = rest[ne]
        av = a_ref[...]
        if a_fn is not None:
            av = a_fn(av.astype(F32))
        p = lax.dot_general(av.astype(BF16), b_ref[...].astype(BF16), dims, preferred_element_type=F32)

        def fin(v):
            if epi is not None:
                v = epi(v, *[e[...] for e in ex])
            o_ref[...] = v.astype(out_dtype).reshape(o_ref.shape)

        if nk == 1:
            fin(p)
        else:
            acc = rest[ne + 1]
            k = pl.program_id(2)

            @pl.when(k == 0)
            def _():
                acc[...] = p

            @pl.when(k > 0)
            def _():
                acc[...] += p

            @pl.when(k == nk - 1)
            def _():
                fin(acc[...])

    if shard_cols is None:
        o_spec, o_shape = pl.BlockSpec((tm, tn), lambda i, j, k: (i, j)), SDS((M, N), out_dtype)
    else:
        per = shard_cols // tn
        assert shard_cols % tn == 0 and N % shard_cols == 0
        o_spec = pl.BlockSpec((1, tm, tn), lambda i, j, k: (lax.div(j, per), i, lax.rem(j, per)))
        o_shape = SDS((N // shard_cols, M, shard_cols), out_dtype)
    res = _call(name, body, (M // tm, N // tn, nk),
                [a_spec, b_spec] + [pl.BlockSpec((tm, tn), lambda i, j, k: (i, j)) for _ in extras],
                [o_spec], [o_shape],
                [pltpu.VMEM((tm, tn), F32)] if nk > 1 else [], ("parallel", "parallel", "arbitrary"),
                (a, b, *extras), comm)
    return res[0] if comm is None else (res[0], res[1:])


def _rowwise(fn, rows, bcasts, outs, accs, *, tr, name):
    T = rows[0][0].shape[0]
    tr = min(tr, T)
    assert T % tr == 0
    nr, nb, no, na = len(rows), len(bcasts), len(outs), len(accs)
    in_specs = [pl.BlockSpec((tr, w), functools.partial(lambda i, c: (i, c), c=cb)) for (_, w, cb) in rows]
    in_specs += [pl.BlockSpec(b.shape, lambda i: (0, 0)) for b in bcasts]
    out_shape = [SDS((T, w), dt) for (w, dt) in outs] + [SDS(s, F32) for s in accs]
    out_specs = [pl.BlockSpec((tr, w), lambda i: (i, 0)) for (w, _) in outs]
    out_specs += [pl.BlockSpec(s, lambda i: (0, 0)) for s in accs]

    def body(*refs):
        ins = [r[...] for r in refs[:nr + nb]]
        o_refs = refs[nr + nb:nr + nb + no]
        a_refs = refs[nr + nb + no:]
        ro, ao = fn(*ins)
        for r, v in zip(o_refs, ro):
            r[...] = v.astype(r.dtype)
        if na:
            @pl.when(pl.program_id(0) == 0)
            def _():
                for r in a_refs:
                    r[...] = jnp.zeros(r.shape, F32)

            for r, v in zip(a_refs, ao):
                r[...] += v

    res = pl.pallas_call(
        body,
        name=name,
        grid=(T // tr,),
        in_specs=in_specs,
        out_specs=out_specs,
        out_shape=out_shape,
        compiler_params=_cp(("arbitrary",) if na else ("parallel",)),
    )(*[r[0] for r in rows], *bcasts)
    return res


def _rms(v):
    r = lax.rsqrt(jnp.mean(v * v, axis=-1, keepdims=True) + EPS)
    return v * r, r


def _rms_bwd(dy, xn, r, g):
    dxn = dy * g
    dv = r * (dxn - xn * jnp.mean(dxn * xn, axis=-1, keepdims=True))
    return dv, jnp.sum(dy * xn, axis=0, keepdims=True)


def _sig(v):
    return 1.0 / (1.0 + jnp.exp(-v))


_GELU_C = math.sqrt(2.0 / math.pi)


def _gelu(v):
    return 0.5 * v * (1.0 + jnp.tanh(_GELU_C * (v + 0.044715 * v * v * v)))


def _gelu_grad(v):
    t = jnp.tanh(_GELU_C * (v + 0.044715 * v * v * v))
    return 0.5 * (1.0 + t) + 0.5 * v * (1.0 - t * t) * _GELU_C * (1.0 + 3.0 * 0.044715 * v * v)


def _rope(v, c, s, sign):
    w = v.shape[1]
    m = lax.broadcasted_iota(jnp.int32, v.shape, 1) % HEAD_DIM
    p = jnp.where(m < ROT_DIM // 2, -pltpu.roll(v, w - ROT_DIM // 2, 1), pltpu.roll(v, ROT_DIM // 2, 1))
    return v * c + sign * (p * s)


def _rope_tables(T):
    half = ROT_DIM // 2
    inv = ROPE_THETA ** (-jnp.arange(half, dtype=F32) * 2.0 / ROT_DIM)
    ang = jnp.arange(T).astype(F32)[:, None] * inv[None, :]
    cos, sin = jnp.cos(ang), jnp.sin(ang)
    one = jnp.ones((T, HEAD_DIM - ROT_DIM), F32)
    c64 = jnp.concatenate([cos, cos, one], axis=1)
    s64 = jnp.concatenate([sin, sin, 0.0 * one], axis=1)
    return jnp.tile(c64, (1, 2)), jnp.tile(s64, (1, 2))


def _dup_half(m, lo):
    lane = lax.broadcasted_iota(jnp.int32, m.shape, 1)
    sw = pltpu.roll(m, HEAD_DIM, 1)
    return jnp.where(lane < HEAD_DIM, m, sw) if lo else jnp.where(lane >= HEAD_DIM, m, sw)


def _attn_mask(i):
    qi = lax.broadcasted_iota(jnp.int32, (ATT_BLOCK, 2 * ATT_BLOCK), 0)
    kj = lax.broadcasted_iota(jnp.int32, (ATT_BLOCK, 2 * ATT_BLOCK), 1)
    rel = qi + ATT_BLOCK - kj
    return (rel >= 0) & (rel < ATT_BLOCK) & ((kj >= ATT_BLOCK) | (i > 0))


_NT = (((1,), (1,)), ((), ()))
_TN = (((0,), (0,)), ((), ()))


def _stack_heads(m):
    lane = lax.broadcasted_iota(jnp.int32, m.shape, 1)
    zero = jnp.zeros_like(m)
    return jnp.concatenate([jnp.where(lane < HEAD_DIM, m, zero), jnp.where(lane >= HEAD_DIM, m, zero)], axis=0)


def _pair_probs(q2, k2, ok2, sink_lo, sink_hi):
    qs = _stack_heads(q2)
    s = lax.dot_general(qs, k2, _NT, preferred_element_type=F32)
    s = jnp.where(ok2, s, NEG)
    row = lax.broadcasted_iota(jnp.int32, (2 * ATT_BLOCK, 1), 0)
    sink = jnp.where(row < ATT_BLOCK, sink_lo, sink_hi)
    m = jnp.maximum(jnp.max(s, axis=1, keepdims=True), sink)
    e = jnp.exp(s - m)
    es = jnp.exp(sink - m)
    inv = 1.0 / (jnp.sum(e, axis=1, keepdims=True) + es)
    return e * inv, es * inv, qs


def _attn_fwd(za, cos, sin, sinks, comm=None):
    T = za.shape[0]
    nb = T // ATT_BLOCK
    kvb = Q_W // (2 * KV_W)

    def body(sink_ref, q_ref, kvp_ref, kvc_ref, cc_ref, sc_ref, cp_ref, sp_ref, o_ref):
        i = pl.program_id(0)
        cc, sc, cp, sp = cc_ref[...], sc_ref[...], cp_ref[...], sp_ref[...]
        q = (_rope(q_ref[...], jnp.tile(cc, (1, 8)), jnp.tile(sc, (1, 8)), 1.0) * 0.125).astype(BF16)
        kvp, kvc = kvp_ref[...], kvc_ref[...]
        k = jnp.concatenate([_rope(kvp[:, :KV_W], cp, sp, 1.0), _rope(kvc[:, :KV_W], cc, sc, 1.0)], axis=0).astype(BF16)
        v = jnp.concatenate([kvp[:, KV_W:], kvc[:, KV_W:]], axis=0).astype(BF16)
        ok = _attn_mask(i)
        ok2 = jnp.concatenate([ok, ok], axis=0)
        lane = lax.broadcasted_iota(jnp.int32, (ATT_BLOCK, LANES), 1)
        for kvh in range(2):
            k2 = _dup_half(k, kvh == 0)
            v2 = _dup_half(v, kvh == 0)
            for pair in range(4):
                c0 = (kvh * 4 + pair) * LANES
                q2 = q[:, c0:c0 + LANES]
                p, _, _ = _pair_probs(q2, k2, ok2, sink_ref[0, 2 * (kvh * 4 + pair)], sink_ref[0, 2 * (kvh * 4 + pair) + 1])
                o = jnp.dot(p.astype(BF16), v2, preferred_element_type=F32)
                o_ref[:, c0:c0 + LANES] = jnp.where(lane < HEAD_DIM, o[:ATT_BLOCK], o[ATT_BLOCK:]).astype(BF16)

    blk = lambda w, f: pl.BlockSpec((ATT_BLOCK, w), f)
    res = _call(
        "attn_fwd", body, (nb,),
        [
            pl.BlockSpec(memory_space=pltpu.SMEM),
            blk(Q_W, lambda i: (i, 0)),
            blk(2 * KV_W, lambda i: (jnp.maximum(i - 1, 0), kvb)),
            blk(2 * KV_W, lambda i: (i, kvb)),
            blk(LANES, lambda i: (i, 0)),
            blk(LANES, lambda i: (i, 0)),
            blk(LANES, lambda i: (jnp.maximum(i - 1, 0), 0)),
            blk(LANES, lambda i: (jnp.maximum(i - 1, 0), 0)),
        ],
        [blk(Q_W, lambda i: (i, 0))], [SDS((T, Q_W), BF16)], [], ("parallel",),
        (sinks, za, za, za, cos, sin, cos, sin), comm)
    return res[0] if comm is None else (res[0], res[1:])


def _attn_bwd(za, cos, sin, sinks, o, do):
    T = za.shape[0]
    nb = T // ATT_BLOCK
    kvb = Q_W // (2 * KV_W)

    def body(sink_ref, q_ref, kvp_ref, kvc_ref, cc_ref, sc_ref, cp_ref, sp_ref, o_ref, do_ref,
             dq_ref, dkv_ref, dsk_ref, carry, dqs):
        i = pl.program_id(0)

        @pl.when(i == 0)
        def _():
            carry[...] = jnp.zeros(carry.shape, F32)
            dsk_ref[...] = jnp.zeros(dsk_ref.shape, F32)

        @pl.when(i < nb)
        def _():
            cc, sc, cp, sp = cc_ref[...], sc_ref[...], cp_ref[...], sp_ref[...]
            ccq, scq = jnp.tile(cc, (1, 8)), jnp.tile(sc, (1, 8))
            q = (_rope(q_ref[...], ccq, scq, 1.0) * 0.125).astype(BF16)
            kvp, kvc = kvp_ref[...], kvc_ref[...]
            k = jnp.concatenate([_rope(kvp[:, :KV_W], cp, sp, 1.0), _rope(kvc[:, :KV_W], cc, sc, 1.0)], axis=0).astype(BF16)
            v = jnp.concatenate([kvp[:, KV_W:], kvc[:, KV_W:]], axis=0).astype(BF16)
            ok = _attn_mask(i)
            ok2 = jnp.concatenate([ok, ok], axis=0)
            lane = lax.broadcasted_iota(jnp.int32, (ATT_BLOCK, LANES), 1)
            lane_s = lax.broadcasted_iota(jnp.int32, (1, LANES), 1)
            dsk = jnp.zeros((1, LANES), F32)
            dkt_h, dvt_h = [], []
            for kvh in range(2):
                k2 = _dup_half(k, kvh == 0)
                v2 = _dup_half(v, kvh == 0)
                dkt = jnp.zeros((LANES, 2 * ATT_BLOCK), F32)
                dvt = jnp.zeros((LANES, 2 * ATT_BLOCK), F32)
                for pair in range(4):
                    h = 2 * (kvh * 4 + pair)
                    c0 = (kvh * 4 + pair) * LANES
                    do2 = do_ref[:, c0:c0 + LANES]
                    prod = do2.astype(F32) * o_ref[:, c0:c0 + LANES].astype(F32)
                    d_lo = jnp.sum(jnp.where(lane < HEAD_DIM, prod, 0.0), axis=1, keepdims=True)
                    d_hi = jnp.sum(jnp.where(lane >= HEAD_DIM, prod, 0.0), axis=1, keepdims=True)
                    delta = jnp.concatenate([d_lo, d_hi], axis=0)
                    p, p_sink, qs = _pair_probs(q[:, c0:c0 + LANES], k2, ok2, sink_ref[0, h], sink_ref[0, h + 1])
                    dos = _stack_heads(do2)
                    t = p_sink * delta
                    dsk = dsk - jnp.where(lane_s == h, jnp.sum(t[:ATT_BLOCK]), 0.0) \
                              - jnp.where(lane_s == h + 1, jnp.sum(t[ATT_BLOCK:]), 0.0)
                    dp = lax.dot_general(dos, v2, _NT, preferred_element_type=F32)
                    ds = (p * (dp - delta)).astype(BF16)
                    dqp = jnp.dot(ds, k2, preferred_element_type=F32)
                    dqs[:, c0:c0 + LANES] = jnp.where(lane < HEAD_DIM, dqp[:ATT_BLOCK], dqp[ATT_BLOCK:]) * 0.125
                    dkt = dkt + lax.dot_general(qs, ds, _TN, preferred_element_type=F32)
                    dvt = dvt + lax.dot_general(dos, p.astype(BF16), _TN, preferred_element_type=F32)
                dkt_h.append(dkt[:HEAD_DIM] + dkt[HEAD_DIM:])
                dvt_h.append(dvt[:HEAD_DIM] + dvt[HEAD_DIM:])
            dk = jnp.concatenate(dkt_h, axis=0).T
            dv = jnp.concatenate(dvt_h, axis=0).T
            dq_ref[...] = _rope(dqs[...], ccq, scq, -1.0).astype(dq_ref.dtype)
            dkp = _rope(dk[:ATT_BLOCK], cp, sp, -1.0)
            dkc = _rope(dk[ATT_BLOCK:], cc, sc, -1.0)
            dkv_ref[...] = (carry[...] + jnp.concatenate([dkp, dv[:ATT_BLOCK]], axis=1)).astype(dkv_ref.dtype)
            carry[...] = jnp.concatenate([dkc, dv[ATT_BLOCK:]], axis=1)
            dsk_ref[...] += dsk

        @pl.when(i == nb)
        def _():
            dkv_ref[...] = carry[...].astype(dkv_ref.dtype)

    blk = lambda w, f: pl.BlockSpec((ATT_BLOCK, w), f)
    cur = lambda i: jnp.minimum(i, nb - 1)
    prv = lambda i: jnp.maximum(jnp.minimum(i, nb - 1) - 1, 0)
    return pl.pallas_call(
        body,
        name="attn_bwd",
        grid=(nb + 1,),
        in_specs=[
            pl.BlockSpec(memory_space=pltpu.SMEM),
            blk(Q_W, lambda i: (cur(i), 0)),
            blk(2 * KV_W, lambda i: (prv(i), kvb)),
            blk(2 * KV_W, lambda i: (cur(i), kvb)),
            blk(LANES, lambda i: (cur(i), 0)),
            blk(LANES, lambda i: (cur(i), 0)),
            blk(LANES, lambda i: (prv(i), 0)),
            blk(LANES, lambda i: (prv(i), 0)),
            blk(Q_W, lambda i: (cur(i), 0)),
            blk(Q_W, lambda i: (cur(i), 0)),
        ],
        out_specs=[
            blk(Q_W, lambda i: (cur(i), 0)),
            blk(2 * KV_W, lambda i: (jnp.maximum(i - 1, 0), 0)),
            pl.BlockSpec((1, LANES), lambda i: (0, 0)),
        ],
        out_shape=[SDS((T, Q_W), BF16), SDS((T, 2 * KV_W), BF16), SDS((1, LANES), F32)],
        scratch_shapes=[pltpu.VMEM((ATT_BLOCK, 2 * KV_W), F32), pltpu.VMEM((ATT_BLOCK, Q_W), F32)],
        compiler_params=_cp(("arbitrary",)),
    )(sinks, za, za, za, cos, sin, cos, sin, o, do)


def _s5_discretize(lam_re, lam_im, log_dt, b_re, b_im):
    dt = jnp.exp(log_dt)[:, None]
    mag = jnp.exp(lam_re * dt)
    a_re, a_im = mag * jnp.cos(lam_im * dt), mag * jnp.sin(lam_im * dt)
    den = lam_re * lam_re + lam_im * lam_im
    nr, ni = a_re - 1.0, a_im
    coef_re = (nr * lam_re + ni * lam_im) / den
    coef_im = (ni * lam_re - nr * lam_im) / den
    bb_re = coef_re[..., None] * b_re - coef_im[..., None] * b_im
    bb_im = coef_re[..., None] * b_im + coef_im[..., None] * b_re
    return a_re, a_im, bb_re, bb_im


def _blockdiag_in(bb):
    x = bb.reshape(N_JB, 8, SSM_P, SSM_GC).transpose(0, 1, 3, 2)
    return (x[:, :, :, None, :] * jnp.eye(8, dtype=bb.dtype)[None, :, None, :, None]).reshape(N_JB, 128, 512)


def _blockdiag_in_extract(m):
    x = m.reshape(N_JB, 8, SSM_GC, 8, SSM_P)
    x = jnp.einsum('jgchp,gh->jgcp', x, jnp.eye(8, dtype=m.dtype))
    return x.transpose(0, 1, 3, 2).reshape(SSM_G, SSM_P, SSM_GC)


def _blockdiag_out(c):
    x = c.reshape(N_JB, 8, SSM_GC, SSM_P).transpose(0, 1, 3, 2)
    return (x[:, :, :, None, :] * jnp.eye(8, dtype=c.dtype)[None, :, None, :, None]).reshape(N_JB, 512, 128)


def _blockdiag_out_extract(m):
    x = m.reshape(N_JB, 8, SSM_P, 8, SSM_GC)
    x = jnp.einsum('jgphc,gh->jgpc', x, jnp.eye(8, dtype=m.dtype))
    return x.transpose(0, 1, 3, 2).reshape(SSM_G, SSM_GC, SSM_P)


def _s5_tables(a_re, a_im):
    ar, ai = a_re.reshape(N_LG, 1, LANES), a_im.reshape(N_LG, 1, LANES)
    p_re, p_im, n = ar, ai, 1
    while n < S5_SEG:
        tr, ti = p_re[:, n - 1:n], p_im[:, n - 1:n]
        p_re, p_im = (jnp.concatenate([p_re, p_re * tr - p_im * ti], axis=1),
                      jnp.concatenate([p_im, p_re * ti + p_im * tr], axis=1))
        n *= 2
    bc = lambda v: jnp.broadcast_to(v, (N_LG, SUBLANES, LANES))
    return p_re, p_im, bc(ar), bc(ai)


def _s5_to_time_major(src_ref, dst_ref):
    for t in range(S5_SEG):
        dst_ref[t * SUBLANES:(t + 1) * SUBLANES, :] = src_ref[pl.ds(t, SUBLANES, stride=S5_SEG), :]


def _s5_from_time_major(val, dst_ref):
    for t in range(S5_SEG):
        dst_ref[pl.ds(t, SUBLANES, stride=S5_SEG), :] = val[t * SUBLANES:(t + 1) * SUBLANES, :]


def _tm_rows(t, row0=0):
    return pl.ds(pl.multiple_of(t * SUBLANES + row0, SUBLANES), SUBLANES)


def _s5_scan(src_re, src_im, dst_re, dst_im, ar, ai, reverse, dst_row0=0):
    def step(n, carry):
        t = (S5_SEG - 1 - n) if reverse else n
        out = []
        for ll in range(LG_PER_JB):
            xr, xi = carry[2 * ll], carry[2 * ll + 1]
            idx = (ll, _tm_rows(t), slice(None))
            odx = (ll, _tm_rows(t, dst_row0), slice(None))
            nr = ar[ll] * xr - ai[ll] * xi + src_re[idx]
            ni = ar[ll] * xi + ai[ll] * xr + src_im[idx]
            dst_re[odx] = nr
            dst_im[odx] = ni
            out += [nr, ni]
        return tuple(out)
    z = jnp.zeros((SUBLANES, LANES), F32)
    return lax.fori_loop(0, S5_SEG, step, (z,) * (2 * LG_PER_JB))


def _s5_fixup(ends, in_re, in_im, mr, mi, s_re, s_im, reverse):
    cr, ci = in_re, in_im
    order = range(SUBLANES - 1, -1, -1) if reverse else range(SUBLANES)
    for s in order:
        s_re[:, s:s + 1, :] = cr
        s_im[:, s:s + 1, :] = ci
        er = jnp.stack([ends[2 * ll][s:s + 1, :] for ll in range(LG_PER_JB)])
        ei = jnp.stack([ends[2 * ll + 1][s:s + 1, :] for ll in range(LG_PER_JB)])
        cr, ci = mr * cr - mi * ci + er, mr * ci + mi * cr + ei
    return cr, ci


def _s5_correct(x_re, x_im, s_re, s_im, p_re, p_im, row0=0):
    sr = [s_re[ll] for ll in range(LG_PER_JB)]
    si = [s_im[ll] for ll in range(LG_PER_JB)]

    def step(t, carry):
        for ll in range(LG_PER_JB):
            idx = (ll, _tm_rows(t, row0), slice(None))
            pr, pi = p_re[ll, pl.ds(t, 1), :], p_im[ll, pl.ds(t, 1), :]
            x_re[idx] = x_re[idx] + (pr * sr[ll] - pi * si[ll])
            x_im[idx] = x_im[idx] + (pr * si[ll] + pi * sr[ll])
        return carry
    lax.fori_loop(0, S5_SEG, step, 0)


def _s5_specs(nc, rev):
    cidx = (lambda c: nc - 1 - c) if rev else (lambda c: c)
    jb = lambda shape: pl.BlockSpec(shape, lambda j, c: (j, 0, 0))
    return cidx, [
        jb((1, LANES, 8 * LANES)),
        jb((1, 8 * LANES, LANES)),
        pl.BlockSpec((1, LANES), lambda j, c: (0, j)),
        jb((LG_PER_JB, SUBLANES, LANES)), jb((LG_PER_JB, SUBLANES, LANES)),
        jb((LG_PER_JB, 1, LANES)), jb((LG_PER_JB, 1, LANES)),
        jb((LG_PER_JB, S5_SEG, LANES)), jb((LG_PER_JB, S5_SEG, LANES)),
    ]


def _s5_fwd(za, prm, comm=None):
    T = za.shape[0]
    R = S5_CHUNK
    nc = T // R
    ub = (Q_W + 2 * KV_W) // LANES
    _, pspecs = _s5_specs(nc, False)

    def body(u_ref, b_ref, c_ref, d_ref, are_ref, aim_ref, alr_ref, ali_ref, pr_ref, pi_ref,
             yg_ref, x0r_ref, x0i_ref, bur, bui, xsr, xsi, sr, si, xcr, xci, utm, ynat):
        c = pl.program_id(1)

        @pl.when(c == 0)
        def _():
            xcr[...] = jnp.zeros(xcr.shape, F32)
            xci[...] = jnp.zeros(xci.shape, F32)

        _s5_to_time_major(u_ref, utm)
        u = utm[...]
        ub16 = u.astype(BF16)
        bu = jnp.dot(ub16, b_ref[0].astype(BF16), preferred_element_type=F32)
        for ll in range(LG_PER_JB):
            bur[ll] = bu[:, ll * LANES:(ll + 1) * LANES]
            bui[ll] = bu[:, (LG_PER_JB + ll) * LANES:(LG_PER_JB + ll + 1) * LANES]
        ar = [are_ref[ll] for ll in range(LG_PER_JB)]
        ai = [aim_ref[ll] for ll in range(LG_PER_JB)]
        ends = _s5_scan(bur, bui, xsr, xsi, ar, ai, False)
        in_r, in_i = xcr[...], xci[...]
        x0r_ref[0] = in_r
        x0i_ref[0] = in_i
        out_r, out_i = _s5_fixup(ends, in_r, in_i, alr_ref[...], ali_ref[...], sr, si, False)
        xcr[...] = out_r
        xci[...] = out_i
        _s5_correct(xsr, xsi, sr, si, pr_ref, pi_ref)
        xcat = jnp.concatenate([xsr[ll].astype(BF16) for ll in range(LG_PER_JB)]
                               + [xsi[ll].astype(BF16) for ll in range(LG_PER_JB)], axis=1)
        y = d_ref[...] * u + jnp.dot(xcat, c_ref[0].astype(BF16), preferred_element_type=F32)
        _s5_from_time_major(_gelu(y), ynat)
        yg_ref[...] = ynat[...].astype(BF16)

    st = pl.BlockSpec((1, LG_PER_JB, 1, LANES), lambda j, c: (c, j, 0, 0))
    vm = lambda rows: pltpu.VMEM((LG_PER_JB, rows, LANES), F32)
    res = _call(
        "s5_fwd", body, (N_JB, nc),
        [pl.BlockSpec((R, LANES), lambda j, c: (c, ub + j))] + pspecs,
        [pl.BlockSpec((R, LANES), lambda j, c: (c, j)), st, st],
        [SDS((T, SSM_W), BF16), SDS((nc, N_LG, 1, LANES), F32), SDS((nc, N_LG, 1, LANES), F32)],
        [vm(R), vm(R), vm(R), vm(R), vm(SUBLANES), vm(SUBLANES), vm(1), vm(1),
         pltpu.VMEM((R, LANES), F32), pltpu.VMEM((R, LANES), F32)],
        ("parallel", "arbitrary"), (za, *prm), comm)
    return res if comm is None else (res[:3], res[3:])


def _s5_bwd(za, dyg, x0r, x0i, prm, prev_tables, comm=None):
    T = za.shape[0]
    R = S5_CHUNK
    nc = T // R
    ub = (Q_W + 2 * KV_W) // LANES
    cidx, pspecs = _s5_specs(nc, True)
    PAD = SUBLANES

    def body(u_ref, dyg_ref, x0r_ref, x0i_ref, b_ref, c_ref, d_ref, are_ref, aim_ref,
             alr_ref, ali_ref, pr_ref, pi_ref, qr_ref, qi_ref,
             du_ref, dar_ref, dai_ref, db_ref, dc_ref, dd_ref,
             bur, bui, xsr, xsi, sr, si, gcr, gci, utm, dtm, dunat):
        c = pl.program_id(1)

        @pl.when(c == 0)
        def _():
            gcr[...] = jnp.zeros(gcr.shape, F32)
            gci[...] = jnp.zeros(gci.shape, F32)
            dar_ref[...] = jnp.zeros(dar_ref.shape, F32)
            dai_ref[...] = jnp.zeros(dai_ref.shape, F32)
            db_ref[...] = jnp.zeros(db_ref.shape, F32)
            dc_ref[...] = jnp.zeros(dc_ref.shape, F32)
            dd_ref[...] = jnp.zeros(dd_ref.shape, F32)

        _s5_to_time_major(u_ref, utm)
        _s5_to_time_major(dyg_ref, dtm)
        u = utm[...]
        ub16 = u.astype(BF16)
        bcat, ccat = b_ref[0].astype(BF16), c_ref[0].astype(BF16)
        lanes = lambda v, ll: v[:, ll * LANES:(ll + 1) * LANES]
        bu = jnp.dot(ub16, bcat, preferred_element_type=F32)
        for ll in range(LG_PER_JB):
            bur[ll] = lanes(bu, ll)
            bui[ll] = lanes(bu, LG_PER_JB + ll)
        ar = [are_ref[ll] for ll in range(LG_PER_JB)]
        ai = [aim_ref[ll] for ll in range(LG_PER_JB)]
        ends = _s5_scan(bur, bui, xsr, xsi, ar, ai, False, dst_row0=PAD)
        in_r, in_i = x0r_ref[0], x0i_ref[0]
        _s5_fixup(ends, in_r, in_i, alr_ref[...], ali_ref[...], sr, si, False)
        _s5_correct(xsr, xsi, sr, si, pr_ref, pi_ref, PAD)
        xsr[:, 0:PAD, :] = sr[...]
        xsi[:, 0:PAD, :] = si[...]
        xcat = jnp.concatenate([xsr[ll, PAD:, :].astype(BF16) for ll in range(LG_PER_JB)]
                               + [xsi[ll, PAD:, :].astype(BF16) for ll in range(LG_PER_JB)], axis=1)
        y = d_ref[...] * u + jnp.dot(xcat, ccat, preferred_element_type=F32)
        dy = dtm[...] * _gelu_grad(y)
        dyb = dy.astype(BF16)
        dd_ref[...] += jnp.sum(dy * u, axis=0, keepdims=True)
        du = d_ref[...] * dy
        dc_ref[0] += lax.dot_general(xcat, dyb, _TN, preferred_element_type=F32)
        g = lax.dot_general(dyb, ccat, _NT, preferred_element_type=F32)
        for ll in range(LG_PER_JB):
            bur[ll] = lanes(g, ll)
            bui[ll] = lanes(g, LG_PER_JB + ll)
        ends = _s5_scan(bur, bui, bur, bui, ar, [-v for v in ai], True)
        out_r, out_i = _s5_fixup(ends, gcr[...], gci[...], alr_ref[...], -ali_ref[...], sr, si, True)
        gcr[...] = out_r
        gci[...] = out_i
        _s5_correct(bur, bui, sr, si, qr_ref, qi_ref)
        for ll in range(LG_PER_JB):
            gr, gi = bur[ll], bui[ll]
            xpr, xpi = xsr[ll, 0:R, :], xsi[ll, 0:R, :]
            red = lambda v: v.reshape(R // SUBLANES, SUBLANES, LANES).sum(axis=0)
            dar_ref[ll] += red(xpr * gr + xpi * gi)
            dai_ref[ll] += red(xpr * gi - xpi * gr)
        gcat = jnp.concatenate([bur[ll].astype(BF16) for ll in range(LG_PER_JB)]
                               + [bui[ll].astype(BF16) for ll in range(LG_PER_JB)], axis=1)
        db_ref[0] += lax.dot_general(ub16, gcat, _TN, preferred_element_type=F32)
        du = du + lax.dot_general(gcat, bcat, _NT, preferred_element_type=F32)
        _s5_from_time_major(du, dunat)
        du_ref[...] = dunat[...].astype(du_ref.dtype)

    st = pl.BlockSpec((1, LG_PER_JB, 1, LANES), lambda j, c: (cidx(c), j, 0, 0))
    jb = lambda shape: pl.BlockSpec(shape, lambda j, c: (j, 0, 0))
    vm = lambda rows: pltpu.VMEM((LG_PER_JB, rows, LANES), F32)
    res = _call(
        "s5_bwd", body, (N_JB, nc),
        [pl.BlockSpec((R, LANES), lambda j, c: (cidx(c), ub + j)),
         pl.BlockSpec((R, LANES), lambda j, c: (cidx(c), j)), st, st] + pspecs
        + [jb((LG_PER_JB, S5_SEG, LANES)), jb((LG_PER_JB, S5_SEG, LANES))],
        [pl.BlockSpec((R, LANES), lambda j, c: (cidx(c), j)),
         jb((LG_PER_JB, SUBLANES, LANES)), jb((LG_PER_JB, SUBLANES, LANES)),
         jb((1, LANES, 8 * LANES)), jb((1, 8 * LANES, LANES)),
         pl.BlockSpec((1, LANES), lambda j, c: (0, j))],
        [SDS((T, SSM_W), BF16), SDS((N_LG, SUBLANES, LANES), F32), SDS((N_LG, SUBLANES, LANES), F32),
         SDS((N_JB, LANES, 8 * LANES), F32), SDS((N_JB, 8 * LANES, LANES), F32), SDS((1, SSM_W), F32)],
        [vm(R), vm(R), vm(R + PAD), vm(R + PAD), vm(SUBLANES), vm(SUBLANES), vm(1), vm(1)]
        + [pltpu.VMEM((R, LANES), F32)] * 3,
        ("parallel", "arbitrary"), (za, dyg, x0r, x0i, *prm, *prev_tables), comm)
    return res if comm is None else (res[:6], res[6:])


def _local_step(x, target, gains, w_a, w_g, sinks, s5w, comms, late, red=None):
    T = x.shape[0]
    D = D_MODEL
    g1, g2, g3, g4 = gains
    cos, sin = _rope_tables(T)
    lam_re, lam_im, log_dt, b_re, b_im, c_re, c_im, d_skip = s5w
    (a_re, a_im, bb_re, bb_im), disc_vjp = jax.vjp(_s5_discretize, lam_re, lam_im, log_dt, b_re, b_im)
    p_re, p_im, abr, abi = _s5_tables(a_re, a_im)
    prm = (jnp.concatenate([_blockdiag_in(bb_re), _blockdiag_in(bb_im)], axis=2),
           jnp.concatenate([_blockdiag_out(c_re), -_blockdiag_out(c_im)], axis=1),
           d_skip.reshape(1, SSM_W), abr, abi, p_re[:, S5_SEG - 1:, :], p_im[:, S5_SEG - 1:, :], p_re, p_im)
    rev_tables = (p_re[:, ::-1, :], -p_im[:, ::-1, :])
    mm = functools.partial(_mm, tm=1024, tn=1024, tk=2048)

    h = _rowwise(lambda xv, g: ((_rms(xv)[0] * g,), ()), [(x, D, 0)], [g1], [(D, BF16)], [], tr=512, name="norm1")[0]
    za = _mm(h, w_a, mode="nn", out_dtype=F32, tm=1024, tn=1152, tk=2048, name="mm_za")
    unpack = lambda res, comm: (res, ()) if comm is None else res
    zg, got0 = unpack(mm(h, w_g, mode="nn", out_dtype=F32, name="mm_zg", comm=comms[0]), comms[0])
    o_attn, got1 = unpack(_attn_fwd(za, cos, sin, sinks, comm=comms[1]), comms[1])
    (yg, x0r, x0i), got2 = unpack(_s5_fwd(za, prm, comm=comms[2]), comms[2])
    w_glu, w_ba, w_bs, w_out, w_up, w_down = late(got0, got1, got2)
    zglu = mm(yg, w_glu, mode="nn", out_dtype=F32, name="mm_glu")
    o_ssm = _rowwise(lambda z1, z2: ((z1 * _sig(z2),), ()), [(zglu, SSM_W, 0), (zglu, SSM_W, 1)], [],
                     [(SSM_W, BF16)], [], tr=512, name="glu")[0]
    ya = mm(o_attn, w_ba, mode="nn", out_dtype=F32, name="mm_ya")
    ys = mm(o_ssm, w_bs, mode="nn", out_dtype=F32, name="mm_ys")
    mi = _rowwise(lambda ga, gs, a, s: ((_sig(ga) * a + _sig(gs) * s,), ()),
                  [(zg, D, 0), (zg, D, 1), (ya, D, 0), (ys, D, 0)], [], [(D, BF16)], [], tr=256, name="gate")[0]
    mixed = mm(mi, w_out, mode="nn", out_dtype=F32, name="mm_out")

    def f_post(xv, mv, g2v, g3v):
        x1v = xv + _rms(mv)[0] * g2v
        return (x1v, _rms(x1v)[0] * g3v), ()
    x1, h2 = _rowwise(f_post, [(x, D, 0), (mixed, D, 0)], [g2, g3], [(D, F32), (D, BF16)], [], tr=256, name="post_mix")
    act = mm(h2, w_up, mode="nn", out_dtype=BF16, name="mm_up", epi=lambda v: jnp.maximum(v, 0.0))
    f = mm(act, w_down, mode="nn", out_dtype=F32, name="mm_down", a_fn=lambda v: v * v)

    def f_final(x1v, fv, tv, g4v):
        fn, r = _rms(fv)
        e = x1v + fn * g4v - tv
        dx2v = e * (1.0 / D)
        dfv, dg4v = _rms_bwd(dx2v, fn, r, g4v)
        return (dfv, dx2v), (dg4v, jnp.zeros((SUBLANES, LANES), F32) + 0.5 * jnp.sum(e * e) * (1.0 / D))
    df, dx2, dg4, lossb = _rowwise(f_final, [(x1, D, 0), (f, D, 0), (target, D, 0)], [g4],
                                   [(D, BF16), (D, F32)], [(1, D), (SUBLANES, LANES)], tr=256, name="final")

    big = {}

    def add(k, g4):
        big[k] = g4
        if red is not None:
            red.add(k, g4)

    def hosted(fn, stage, names):
        if red is None:
            return fn(comm=None)
        out, got = fn(comm=getattr(red, stage)(names))
        getattr(red, stage + "_done")(names, got)
        return out

    dpre = mm(df, w_down, mode="nt", out_dtype=BF16, name="mm_dact", epi=lambda v, a: v * (2.0 * a.astype(F32)), extras=(act,))
    wg = functools.partial(_mm, mode="tn", out_dtype=F32, tm=1024, tn=1024, tk=2048)
    add("w_down", wg(act, df, name="wg_down", a_fn=lambda v: v * v).reshape(4, D_FF // 4, D))
    dh2 = hosted(functools.partial(mm, dpre, w_up, mode="nt", out_dtype=F32, name="mm_dh2"),
                 "s1", ["w_down"])
    add("w_up", hosted(functools.partial(wg, h2, dpre, name="wg_up", shard_cols=D_FF // 4), "s3", ["w_down"]))

    def f_mid(dx2v, dh2v, x1v, mv, g2v, g3v):
        x1n, r3 = _rms(x1v)
        d3, dg3v = _rms_bwd(dh2v, x1n, r3, g3v)
        dx1v = dx2v + d3
        mn, r2 = _rms(mv)
        dmv, dg2v = _rms_bwd(dx1v, mn, r2, g2v)
        return (dx1v, dmv), (dg3v, dg2v)
    dx1, dmixed, dg3, dg2 = _rowwise(f_mid, [(dx2, D, 0), (dh2, D, 0), (x1, D, 0), (mixed, D, 0)], [g2, g3],
                                     [(D, F32), (D, BF16)], [(1, D), (1, D)], tr=256, name="mid")

    dmi = hosted(functools.partial(mm, dmixed, w_out, mode="nt", out_dtype=F32, name="mm_dmi"), "s1", ["w_up"])
    add("w_out", wg(mi, dmixed, name="wg_out").reshape(4, D // 4, D))

    def f_gate(dv, ga, gs, a, s):
        sa, ss = _sig(ga), _sig(gs)
        return (dv * sa, dv * ss, jnp.concatenate([dv * a * sa * (1.0 - sa), dv * s * ss * (1.0 - ss)], axis=1)), ()
    dya, dys, dzg = _rowwise(f_gate, [(dmi, D, 0), (zg, D, 0), (zg, D, 1), (ya, D, 0), (ys, D, 0)], [],
                             [(D, BF16), (D, BF16), (2 * D, BF16)], [], tr=256, name="gate_bwd")
    do_attn = hosted(functools.partial(mm, dya, w_ba, mode="nt", out_dtype=BF16, name="mm_doa"), "s1", ["w_out"])
    d_w_ba = wg(o_attn, dya, name="wg_ba")
    do_ssm = mm(dys, w_bs, mode="nt", out_dtype=F32, name="mm_dos")
    d_w_bs = wg(o_ssm, dys, name="wg_bs")
    add("w_branch", jnp.concatenate([d_w_ba.reshape(2, D // 4, D), d_w_bs.reshape(2, D // 4, D)], axis=0))

    def f_glu(dv, z1, z2):
        s2 = _sig(z2)
        return (jnp.concatenate([dv * s2, dv * z1 * s2 * (1.0 - s2)], axis=1),), ()
    dzglu = _rowwise(f_glu, [(do_ssm, SSM_W, 0), (zglu, SSM_W, 0), (zglu, SSM_W, 1)], [], [(2 * SSM_W, BF16)], [],
                     tr=512, name="glu_bwd")[0]
    dyg = hosted(functools.partial(mm, dzglu, w_glu, mode="nt", out_dtype=F32, name="mm_dyg"), "s1", ["w_branch"])
    add("w_glu", wg(yg, dzglu, name="wg_glu", tn=SSM_W // 2, shard_cols=SSM_W // 2))
    du, dar, dai, dbc, dcc, ddv = hosted(functools.partial(_s5_bwd, za, dyg, x0r, x0i, prm, rev_tables),
                                         "s3", ["w_up", "w_out", "w_branch"])
    dbr, dbi = dbc[:, :, :4 * LANES], dbc[:, :, 4 * LANES:]
    dcr, dci = dcc[:, :4 * LANES, :], -dcc[:, 4 * LANES:, :]
    dq, dkv, dsk = _attn_bwd(za, cos, sin, sinks, o_attn, do_attn)
    dza = jnp.concatenate([dq, dkv, du], axis=1)
    d_w_a = _mm(h, dza, mode="tn", out_dtype=F32, tm=1024, tn=ZA_W // 2, tk=2048, name="wg_a")
    d_w_g = wg(h, dzg, name="wg_g")
    d_w_in = jnp.concatenate([d_w_a, d_w_g], axis=1)
    add("w_in", d_w_in.reshape(D, 4, d_w_in.shape[1] // 4).transpose(1, 0, 2))
    dh = hosted(functools.partial(mm, dza, w_a, mode="nt", out_dtype=F32, name="mm_dh_a", tk=ZA_W), "s1", ["w_in", "w_glu"])
    dh = hosted(functools.partial(mm, dzg, w_g, mode="nt", out_dtype=F32, name="mm_dh_g",
                                  epi=lambda v, p: v + p, extras=(dh,)), "s3", ["w_in", "w_glu"])

    def f_first(dx1v, dhv, xv, g1v):
        xn, r1 = _rms(xv)
        d1, dg1v = _rms_bwd(dhv, xn, r1, g1v)
        return (dx1v + d1,), (dg1v,)
    dx, dg1 = _rowwise(f_first, [(dx1, D, 0), (dh, D, 0), (x, D, 0)], [g1], [(D, F32)], [(1, D)], tr=256, name="first")

    da_re = dar.sum(axis=1).reshape(SSM_G, SSM_P)
    da_im = dai.sum(axis=1).reshape(SSM_G, SSM_P)
    d_lam_re, d_lam_im, d_log_dt, d_b_re, d_b_im = disc_vjp(
        (da_re, da_im, _blockdiag_in_extract(dbr), _blockdiag_in_extract(dbi)))
    small = dict(norm_mix_pre=dg1, norm_mix_post=dg2, norm_mlp_pre=dg3, norm_mlp_post=dg4,
                 sinks=dsk[:, :N_Q_HEADS], lam_re=d_lam_re, lam_im=d_lam_im, log_dt=d_log_dt,
                 b_re=d_b_re, b_im=d_b_im, c_re=_blockdiag_out_extract(dcr), c_im=_blockdiag_out_extract(dci),
                 d_skip=ddv.reshape(SSM_G, SSM_GC))
    return lossb[0, 0], dx, small, big


def _cast_into_slot(w, k_arr):
    rows, cols = w.shape
    tr = 256

    def body(k_ref, w_ref, o_ref):
        o_ref[0] = w_ref[...].astype(BF16)

    return pl.pallas_call(
        body,
        name="cast_into_slot",
        grid_spec=pltpu.PrefetchScalarGridSpec(
            num_scalar_prefetch=1,
            grid=(rows // tr,),
            in_specs=[pl.BlockSpec((tr, cols), lambda i, k: (i, 0))],
            out_specs=pl.BlockSpec((1, tr, cols), lambda i, k: (k[0], i, 0)),
        ),
        out_shape=SDS((4, rows, cols), BF16),
        compiler_params=_cp(("parallel",)),
    )(k_arr, w)


def _pair_sum(g, r, c_arr):
    _, _, hr, cols = g.shape
    tr = min(256, hr)

    def body(c_ref, g_ref, r_ref, o_ref):
        o_ref[0] = (g_ref[0, 0] + r_ref[0]).astype(BF16)

    return pl.pallas_call(
        body,
        name="pair_sum",
        grid_spec=pltpu.PrefetchScalarGridSpec(
            num_scalar_prefetch=1,
            grid=(4, hr // tr),
            in_specs=[pl.BlockSpec((1, 1, tr, cols), lambda k, i, c_ref: (k, c_ref[0], i, 0)),
                      pl.BlockSpec((1, tr, cols), lambda k, i, c_ref: (k, i, 0))],
            out_specs=pl.BlockSpec((1, tr, cols), lambda k, i, c_ref: (k, i, 0)),
        ),
        out_shape=SDS((4, hr, cols), BF16),
        compiler_params=_cp(("parallel", "parallel")),
    )(c_arr, g, r)


def _chip_sum(g, r, q, kc_arr):
    _, _, hr, cols = g.shape
    tr = min(256, hr)

    def body(kc_ref, g_ref, r_ref, q_ref, o_ref):
        s = g_ref[0, 0] + r_ref[0]
        for j in range(3):
            s = s + q_ref[j].astype(F32)
        o_ref[...] = s

    return pl.pallas_call(
        body,
        name="chip_sum",
        grid_spec=pltpu.PrefetchScalarGridSpec(
            num_scalar_prefetch=1,
            grid=(hr // tr,),
            in_specs=[pl.BlockSpec((1, 1, tr, cols), lambda i, kc: (kc[0], kc[1], i, 0)),
                      pl.BlockSpec((1, tr, cols), lambda i, kc: (kc[0], i, 0)),
                      pl.BlockSpec((3, tr, cols), lambda i, kc: (0, i, 0))],
            out_specs=pl.BlockSpec((tr, cols), lambda i, kc: (kc[1] * (hr // tr) + i, 0)),
        ),
        out_shape=SDS((2 * hr, cols), F32),
        compiler_params=_cp(("parallel",)),
    )(kc_arr, g, r, q)


def _pair_share(blocks):
    n = len(blocks)

    def body(*refs):
        ins, outs = refs[:n], refs[n:2 * n]
        ssem, rsem = refs[2 * n:]
        x, y, c, _ = _place()
        cps = []
        for w in range(n):
            hr = ins[w].shape[0] // 2
            rows = pl.ds(pl.multiple_of(c * hr, 8), hr)
            cp = _remote(ins[w].at[rows, :], outs[w].at[rows, :], ssem.at[w], rsem.at[w], (x, y, 1 - c))
            cp.start()
            cps.append(cp)
        for w in range(n):
            hr = ins[w].shape[0] // 2
            other = outs[w].at[pl.ds(pl.multiple_of((1 - c) * hr, 8), hr), :]
            _remote(other, other, ssem.at[w], rsem.at[w], (x, y, 1 - c)).wait_recv()
        for cp in cps:
            cp.wait_send()

    dma = pltpu.SemaphoreType.DMA
    return pl.pallas_call(
        body,
        name="pair_share",
        in_specs=[ANY] * n,
        out_specs=[ANY] * n,
        out_shape=[SDS(b.shape, b.dtype) for b in blocks],
        input_output_aliases={w: w for w in range(n)},
        scratch_shapes=[dma((n,)), dma((n,))],
    )(*blocks)


class _GradReducer:
    def __init__(self, c_arr, kc_arr):
        self.c_arr, self.kc_arr = c_arr, kc_arr
        self.g, self.r, self.ps, self.q = {}, {}, {}, {}

    def add(self, k, g4):
        self.g[k] = g4.reshape(4, 2, g4.shape[1] // 2, g4.shape[2])

    def s1(self, names):
        return _PairExchangeComm([self.g[k].reshape(4, -1, self.g[k].shape[3]) for k in names])

    def s1_done(self, names, got):
        for k, r in zip(names, got):
            self.r[k] = r
            self.ps[k] = _pair_sum(self.g[k], r, self.c_arr)

    def s3(self, names):
        return _ChipExchangeComm([self.ps[k] for k in names])

    def s3_done(self, names, got):
        self.q.update(zip(names, got))

    def finish(self, order):
        rest = [k for k in order if k not in self.r]
        if rest:
            self.s1_done(rest, _comm_only("pair_exchange", self.s1(rest)))
        rest = [k for k in order if k not in self.q]
        if rest:
            self.s3_done(rest, _comm_only("chip_exchange", self.s3(rest)))
        blocks = [_chip_sum(self.g[k], self.r[k], self.q[k], self.kc_arr) for k in order]
        return dict(zip(order, _pair_share(blocks)))


def _all_reduce_small(buf):
    rows = buf.shape[0]
    hr = rows // 2
    assert hr % SUBLANES == 0

    def body(in_ref, o_ref, sib, pair, slots, ssem, rsem):
        x, y, c, others = _place()
        me, sibling = 2 * x + y, (x, y, 1 - c)
        mine = pl.ds(pl.multiple_of(c * hr, SUBLANES), hr)
        theirs = pl.ds(pl.multiple_of((1 - c) * hr, SUBLANES), hr)
        first = _remote(in_ref, sib, ssem.at[0], rsem.at[0], sibling)
        first.start()
        first.wait()
        pair[...] = in_ref[...] + sib[...]
        slots[me] = pair[mine, :]
        cps = [_remote(pair.at[mine, :], slots.at[me], ssem.at[1 + r], rsem.at[1 + r], (ox, oy, c))
               for r, (ox, oy) in enumerate(others)]
        for cp in cps:
            cp.start()
        for r, (ox, oy) in enumerate(others):
            _remote(pair.at[mine, :], slots.at[2 * ox + oy], ssem.at[1 + r], rsem.at[1 + r], (ox, oy, c)).wait_recv()
        o_ref[mine, :] = (slots[0] + slots[1]) + (slots[2] + slots[3])
        last = _remote(o_ref.at[mine, :], o_ref.at[mine, :], ssem.at[4], rsem.at[4], sibling)
        last.start()
        _remote(o_ref.at[theirs, :], o_ref.at[theirs, :], ssem.at[4], rsem.at[4], sibling).wait_recv()
        last.wait_send()
        for cp in cps:
            cp.wait_send()

    dma = pltpu.SemaphoreType.DMA
    return pl.pallas_call(
        body,
        name="all_reduce_small",
        in_specs=[pl.BlockSpec(memory_space=pltpu.VMEM)],
        out_specs=pl.BlockSpec(memory_space=pltpu.VMEM),
        out_shape=SDS(buf.shape, F32),
        scratch_shapes=[pltpu.VMEM((rows, LANES), F32), pltpu.VMEM((rows, LANES), F32),
                        pltpu.VMEM((4, hr, LANES), F32), dma((5,)), dma((5,))],
        compiler_params=pltpu.CompilerParams(vmem_limit_bytes=VMEM_LIMIT),
    )(buf)


def _adam_fn(w, g, m, v):
    m2 = ADAM_B1 * m + (1.0 - ADAM_B1) * g
    v2 = ADAM_B2 * v + (1.0 - ADAM_B2) * (g * g)
    m_hat = m2 / (1.0 - ADAM_B1 ** ADAM_STEP)
    v_hat = v2 / (1.0 - ADAM_B2 ** ADAM_STEP)
    return (-ADAM_LR * (m_hat / (jnp.sqrt(v_hat) + ADAM_EPS) + ADAM_WD * w), m2, v2), ()


def _adamw(w, g, m, v, name, tr=256):
    cols = w.shape[1]
    return _rowwise(_adam_fn, [(w, cols, 0), (g, cols, 0), (m, cols, 0), (v, cols, 0)], [],
                    [(cols, F32)] * 3, [], tr=tr, name=name)


BIG = ("w_in", "w_glu", "w_branch", "w_out", "w_up", "w_down")
COL_SHARDED = ("w_in", "w_glu", "w_up")
SMALL = ("norm_mix_pre", "norm_mix_post", "norm_mlp_pre", "norm_mlp_post", "sinks", "lam_re", "lam_im", "log_dt",
         "b_re", "b_im", "c_re", "c_im", "d_skip")
WEIGHTS = ("norm_mix_pre", "norm_mix_post", "norm_mlp_pre", "norm_mlp_post", "w_in", "sinks", "lam_re", "lam_im",
           "log_dt", "b_re", "b_im", "c_re", "c_im", "d_skip", "w_glu", "w_branch", "w_out", "w_up", "w_down")


def _flat_small(vals, extra):
    flat = jnp.concatenate([vals[k].reshape(-1) for k in SMALL] + [extra.reshape(-1)])
    rows = -(-flat.shape[0] // (SUBLANES * LANES)) * SUBLANES
    return jnp.pad(flat, (0, rows * LANES - flat.shape[0])).reshape(rows, LANES)


def kernel(x, norm_mix_pre, norm_mix_post, norm_mlp_pre, norm_mlp_post, w_in, sinks, lam_re, lam_im, log_dt, b_re, b_im, c_re, c_im, d_skip, w_glu, w_branch, w_out, w_up, w_down, loss_target, m_norm_mix_pre, m_norm_mix_post, m_norm_mlp_pre, m_norm_mlp_post, m_w_in, m_sinks, m_lam_re, m_lam_im, m_log_dt, m_b_re, m_b_im, m_c_re, m_c_im, m_d_skip, m_w_glu, m_w_branch, m_w_out, m_w_up, m_w_down, v_norm_mix_pre, v_norm_mix_post, v_norm_mlp_pre, v_norm_mlp_post, v_w_in, v_sinks, v_lam_re, v_lam_im, v_log_dt, v_b_re, v_b_im, v_c_re, v_c_im, v_d_skip, v_w_glu, v_w_branch, v_w_out, v_w_up, v_w_down):
    w = dict(norm_mix_pre=norm_mix_pre, norm_mix_post=norm_mix_post, norm_mlp_pre=norm_mlp_pre, norm_mlp_post=norm_mlp_post,
             w_in=w_in, sinks=sinks, lam_re=lam_re, lam_im=lam_im, log_dt=log_dt, b_re=b_re, b_im=b_im, c_re=c_re,
             c_im=c_im, d_skip=d_skip, w_glu=w_glu, w_branch=w_branch, w_out=w_out, w_up=w_up, w_down=w_down)
    m = dict(norm_mix_pre=m_norm_mix_pre, norm_mix_post=m_norm_mix_post, norm_mlp_pre=m_norm_mlp_pre,
             norm_mlp_post=m_norm_mlp_post, w_in=m_w_in, sinks=m_sinks, lam_re=m_lam_re, lam_im=m_lam_im,
             log_dt=m_log_dt, b_re=m_b_re, b_im=m_b_im, c_re=m_c_re, c_im=m_c_im, d_skip=m_d_skip, w_glu=m_w_glu,
             w_branch=m_w_branch, w_out=m_w_out, w_up=m_w_up, w_down=m_w_down)
    v = dict(norm_mix_pre=v_norm_mix_pre, norm_mix_post=v_norm_mix_post, norm_mlp_pre=v_norm_mlp_pre,
             norm_mlp_post=v_norm_mlp_post, w_in=v_w_in, sinks=v_sinks, lam_re=v_lam_re, lam_im=v_lam_im,
             log_dt=v_log_dt, b_re=v_b_re, b_im=v_b_im, c_re=v_c_re, c_im=v_c_im, d_skip=v_d_skip, w_glu=v_w_glu,
             w_branch=v_w_branch, w_out=v_w_out, w_up=v_w_up, w_down=v_w_down)
    xi, yi, ci = lax.axis_index("x"), lax.axis_index("y"), lax.axis_index("c")

    k_arr = jnp.stack([2 * xi + yi]).astype(jnp.int32)
    slot = {k: _cast_into_slot(w[k][0], k_arr) for k in BIG}

    def whole(k, g4):
        if k in COL_SHARDED:
            return jnp.concatenate([g4[j] for j in range(4)], axis=1)
        return g4.reshape(4 * g4.shape[1], g4.shape[2])

    w_in_b = whole("w_in", _comm_only("gather_w_in", _GatherComm([slot["w_in"]]))[0])
    hosted = (("w_glu", "w_branch", "w_out"), ("w_up",), ("w_down",))
    comms = [_GatherComm([slot[k] for k in names]) for names in hosted]

    def late(*got):
        f = {k: whole(k, g4) for names, res in zip(hosted, got) for k, g4 in zip(names, res)}
        return f["w_glu"], f["w_branch"][:Q_W], f["w_branch"][Q_W:], f["w_out"], f["w_up"], f["w_down"]

    s5w = (lam_re[0], lam_im[0], log_dt[0], b_re[0], b_im[0], c_re[0], c_im[0], d_skip[0])
    reducer = _GradReducer(jnp.stack([ci]).astype(jnp.int32), jnp.stack([2 * xi + yi, ci]).astype(jnp.int32))
    loss_part, dx, small, _ = _local_step(
        x[0], loss_target[0], (norm_mix_pre, norm_mix_post, norm_mlp_pre, norm_mlp_post),
        w_in_b[:, :ZA_W], w_in_b[:, ZA_W:], sinks, s5w, comms, late, reducer)
    grads = reducer.finish(BIG)

    red = _all_reduce_small(_flat_small(small, loss_part)).reshape(-1)
    off = 0
    for k in SMALL:
        n = math.prod(w[k].shape)
        grads[k] = red[off:off + n].reshape(w[k].shape[1:])
        off += n
    loss = red[off]

    delta, new_m, new_v = {}, {}, {}
    for k in BIG:
        delta[k], new_m[k], new_v[k] = _adamw(w[k][0], grads[k], m[k][0], v[k][0], "adamw_" + k)
    zero = jnp.zeros((), F32)
    fw, fm, fv = (_flat_small({k: t[k] for k in SMALL}, zero) for t in (w, m, v))
    fg = _flat_small(grads, zero)
    sd, sm, sv = _adamw(fw, fg, fm, fv, "adamw_small", tr=fw.shape[0])
    off = 0
    for k in SMALL:
        n = math.prod(w[k].shape)
        delta[k], new_m[k], new_v[k] = (t.reshape(-1)[off:off + n].reshape(w[k].shape[1:]) for t in (sd, sm, sv))
        off += n

    lead = lambda t: t[None]
    return (loss, lead(dx), *[lead(grads[k]) for k in WEIGHTS], *[lead(delta[k]) for k in WEIGHTS],
            *[lead(new_m[k]) for k in WEIGHTS], *[lead(new_v[k]) for k in WEIGHTS])
```

```python
import functools
import math

import jax
import jax.numpy as jnp
from jax import lax
from jax.experimental import pallas as pl
from jax.experimental.pallas import tpu as pltpu

F32 = jnp.float32
BF16 = jnp.bfloat16
SDS = jax.ShapeDtypeStruct

D_MODEL = 2048
HEAD_DIM = 64
N_Q_HEADS = 16
ATT_BLOCK = 128
ROT_DIM = 16
ROPE_THETA = 500000.0
Q_W = 1024
KV_W = 128
SSM_W = 1024
SSM_G = 64
SSM_GC = 16
SSM_P = 64
N_STATE = SSM_G * SSM_P
LANES = 128
SUBLANES = 8
N_LG = N_STATE // LANES
N_JB = 8
LG_PER_JB = N_LG // N_JB
D_FF = 8192
ZA_W = Q_W + 2 * KV_W + SSM_W
EPS = 1e-6
S5_CHUNK = 512
S5_SEG = S5_CHUNK // SUBLANES
VMEM_LIMIT = 56 * 1024 * 1024
NEG = -1e30

ADAM_LR = 0.001
ADAM_B1 = 0.9
ADAM_B2 = 0.999
ADAM_EPS = 1e-08
ADAM_WD = 0.01
ADAM_STEP = 10

MESH = pl.DeviceIdType.MESH


def _cp(sem):
    return pltpu.CompilerParams(dimension_semantics=sem, vmem_limit_bytes=VMEM_LIMIT)


ANY = pl.BlockSpec(memory_space=pl.ANY)


def _place():
    x, y, c = lax.axis_index("x"), lax.axis_index("y"), lax.axis_index("c")
    others = [(1 - x, y), (x, 1 - y), (1 - x, 1 - y)]
    return x, y, c, others


def _remote(src, dst, ssem, rsem, to):
    return pltpu.make_async_remote_copy(src_ref=src, dst_ref=dst, send_sem=ssem, recv_sem=rsem,
                                        device_id=to, device_id_type=MESH)


class _GatherComm:
    aliased = True

    def __init__(self, slotted):
        self.arrs = list(slotted)
        self.n = len(self.arrs)
        dma = pltpu.SemaphoreType.DMA
        self.scratch = [dma((3 * self.n,)) for _ in range(4)]
        self.out_shape = [SDS(s.shape, s.dtype) for s in self.arrs]

    @staticmethod
    def _half(ref, hc):
        hr = ref.shape[1] // 2
        return pl.ds(pl.multiple_of(hc * hr, 16), hr)

    def start(self, ins, outs, sems):
        ssem, rsem, _, _ = sems
        x, y, c, others = _place()
        me = 2 * x + y
        for w in range(self.n):
            for r, (ox, oy) in enumerate(others):
                _remote(ins[w].at[me, self._half(ins[w], c), :], outs[w].at[me, self._half(ins[w], c), :],
                        ssem.at[3 * w + r], rsem.at[3 * w + r], (ox, oy, c)).start()

    def finish(self, ins, outs, sems):
        ssem, rsem, fs_sem, fr_sem = sems
        x, y, c, others = _place()
        me, sib = 2 * x + y, (x, y, 1 - c)
        passes = []
        for w in range(self.n):
            for r, (ox, oy) in enumerate(others):
                got = outs[w].at[2 * ox + oy, self._half(ins[w], c), :]
                _remote(got, got, ssem.at[3 * w + r], rsem.at[3 * w + r], (ox, oy, c)).wait_recv()
                cp = _remote(got, got, fs_sem.at[3 * w + r], fr_sem.at[3 * w + r], sib)
                cp.start()
                passes.append(cp)
        for w in range(self.n):
            for r, (ox, oy) in enumerate(others):
                got = outs[w].at[2 * ox + oy, self._half(ins[w], 1 - c), :]
                _remote(got, got, fs_sem.at[3 * w + r], fr_sem.at[3 * w + r], sib).wait_recv()
        for w in range(self.n):
            for r, (ox, oy) in enumerate(others):
                mine = ins[w].at[me, self._half(ins[w], c), :]
                _remote(mine, mine, ssem.at[3 * w + r], rsem.at[3 * w + r], (ox, oy, c)).wait_send()
        for cp in passes:
            cp.wait_send()


class _PairExchangeComm:
    aliased = False

    def __init__(self, grads):
        self.arrs = list(grads)
        self.n = len(self.arrs)
        dma = pltpu.SemaphoreType.DMA
        self.scratch = [dma((self.n,)), dma((self.n,))]
        self.out_shape = [SDS((4, g.shape[1] // 2, g.shape[2]), g.dtype) for g in self.arrs]

    def _copies(self, ins, outs, sems):
        ssem, rsem = sems
        x, y, c, _ = _place()
        cps = []
        for w in range(self.n):
            hr = ins[w].shape[1] // 2
            src = ins[w].at[:, pl.ds(pl.multiple_of((1 - c) * hr, 8), hr), :]
            cps.append(_remote(src, outs[w], ssem.at[w], rsem.at[w], (x, y, 1 - c)))
        return cps

    def start(self, ins, outs, sems):
        for cp in self._copies(ins, outs, sems):
            cp.start()

    def finish(self, ins, outs, sems):
        for cp in self._copies(ins, outs, sems):
            cp.wait()


class _ChipExchangeComm:
    aliased = False

    def __init__(self, psums):
        self.arrs = list(psums)
        self.n = len(self.arrs)
        dma = pltpu.SemaphoreType.DMA
        self.scratch = [dma((3 * self.n,)), dma((3 * self.n,))]
        self.out_shape = [SDS((3,) + p.shape[1:], p.dtype) for p in self.arrs]

    def _copies(self, ins, outs, sems):
        ssem, rsem = sems
        x, y, c, others = _place()
        return [_remote(ins[w].at[2 * ox + oy], outs[w].at[r], ssem.at[3 * w + r], rsem.at[3 * w + r], (ox, oy, c))
                for w in range(self.n) for r, (ox, oy) in enumerate(others)]

    def start(self, ins, outs, sems):
        for cp in self._copies(ins, outs, sems):
            cp.start()

    def finish(self, ins, outs, sems):
        for cp in self._copies(ins, outs, sems):
            cp.wait()


def _comm_only(name, comm):
    n = comm.n

    def body(*refs):
        ins, outs, sems = refs[:n], refs[n:2 * n], refs[2 * n:]
        comm.start(ins, outs, sems)
        comm.finish(ins, outs, sems)

    return pl.pallas_call(
        body, name=name, in_specs=[ANY] * n, out_specs=[ANY] * n, out_shape=comm.out_shape,
        input_output_aliases={w: w for w in range(n)} if comm.aliased else {},
        scratch_shapes=comm.scratch)(*comm.arrs)


def _call(name, body, grid, in_specs, out_specs, out_shape, scratch, dims, args, comm=None):
    if comm is None:
        return pl.pallas_call(body, name=name, grid=grid, in_specs=in_specs, out_specs=out_specs, out_shape=out_shape,
                              scratch_shapes=scratch, compiler_params=_cp(dims))(*args)
    ni, no, ns, n = len(in_specs), len(out_shape), len(scratch), comm.n

    def hosted(*refs):
        ins, cin = refs[:ni], refs[ni:ni + n]
        outs, cout = refs[ni + n:ni + n + no], refs[ni + n + no:ni + 2 * n + no]
        scr, sems = refs[ni + 2 * n + no:ni + 2 * n + no + ns], refs[ni + 2 * n + no + ns:]
        ids = [pl.program_id(d) for d in range(len(grid))]
        first = functools.reduce(jnp.logical_and, [i == 0 for i in ids])
        last = functools.reduce(jnp.logical_and, [i == g - 1 for i, g in zip(ids, grid)])

        @pl.when(first)
        def _():
            comm.start(cin, cout, sems)

        body(*ins, *outs, *scr)

        @pl.when(last)
        def _():
            comm.finish(cin, cout, sems)

    return pl.pallas_call(
        hosted, name=name, grid=grid, in_specs=list(in_specs) + [ANY] * n, out_specs=list(out_specs) + [ANY] * n,
        out_shape=list(out_shape) + comm.out_shape,
        input_output_aliases={ni + w: no + w for w in range(n)} if comm.aliased else {},
        scratch_shapes=list(scratch) + comm.scratch, compiler_params=_cp(("arbitrary",) * len(grid)))(*args, *comm.arrs)


def _mm(a, b, *, mode, out_dtype, tm, tn, tk, name, a_fn=None, epi=None, extras=(), comm=None, shard_cols=None):
    if mode == "nn":
        (M, K), (K2, N) = a.shape, b.shape
    elif mode == "nt":
        (M, K), (N, K2) = a.shape, b.shape
    else:
        (K, M), (K2, N) = a.shape, b.shape
    assert K == K2, (a.shape, b.shape, mode)
    tm, tn, tk = min(tm, M), min(tn, N), min(tk, K)
    assert M % tm == 0 and N % tn == 0 and K % tk == 0, (M, N, K, tm, tn, tk)
    nk = K // tk
    if mode == "tn":
        a_spec = pl.BlockSpec((tk, tm), lambda i, j, k: (k, i))
        ca = 0
    else:
        a_spec = pl.BlockSpec((tm, tk), lambda i, j, k: (i, k))
        ca = 1
    if mode == "nt":
        b_spec = pl.BlockSpec((tn, tk), lambda i, j, k: (j, k))
        cb = 1
    else:
        b_spec = pl.BlockSpec((tk, tn), lambda i, j, k: (k, j))
        cb = 0
    dims = (((ca,), (cb,)), ((), ()))
    ne = len(extras)

    def body(a_ref, b_ref, *rest):
        ex = rest[:ne]
        o_ref = rest[ne]
        av = a_ref[...]
        if a_fn is not None:
            av = a_fn(av.astype(F32))
        p = lax.dot_general(av.astype(BF16), b_ref[...].astype(BF16), dims, preferred_element_type=F32)

        def fin(v):
            if epi is not None:
                v = epi(v, *[e[...] for e in ex])
            o_ref[...] = v.astype(out_dtype).reshape(o_ref.shape)

        if nk == 1:
            fin(p)
        else:
            acc = rest[ne + 1]
            k = pl.program_id(2)

            @pl.when(k == 0)
            def _():
                acc[...] = p

            @pl.when(k > 0)
            def _():
                acc[...] += p

            @pl.when(k == nk - 1)
            def _():
                fin(acc[...])

    if shard_cols is None:
        o_spec, o_shape = pl.BlockSpec((tm, tn), lambda i, j, k: (i, j)), SDS((M, N), out_dtype)
    else:
        per = shard_cols // tn
        assert shard_cols % tn == 0 and N % shard_cols == 0
        o_spec = pl.BlockSpec((1, tm, tn), lambda i, j, k: (lax.div(j, per), i, lax.rem(j, per)))
        o_shape = SDS((N // shard_cols, M, shard_cols), out_dtype)
    res = _call(name, body, (M // tm, N // tn, nk),
                [a_spec, b_spec] + [pl.BlockSpec((tm, tn), lambda i, j, k: (i, j)) for _ in extras],
                [o_spec], [o_shape],
                [pltpu.VMEM((tm, tn), F32)] if nk > 1 else [], ("parallel", "parallel", "arbitrary"),
                (a, b, *extras), comm)
    return res[0] if comm is None else (res[0], res[1:])


def _rowwise(fn, rows, bcasts, outs, accs, *, tr, name):
    T = rows[0][0].shape[0]
    tr = min(tr, T)
    assert T % tr == 0
    nr, nb, no, na = len(rows), len(bcasts), len(outs), len(accs)
    in_specs = [pl.BlockSpec((tr, w), functools.partial(lambda i, c: (i, c), c=cb)) for (_, w, cb) in rows]
    in_specs += [pl.BlockSpec(b.shape, lambda i: (0, 0)) for b in bcasts]
    out_shape = [SDS((T, w), dt) for (w, dt) in outs] + [SDS(s, F32) for s in accs]
    out_specs = [pl.BlockSpec((tr, w), lambda i: (i, 0)) for (w, _) in outs]
    out_specs += [pl.BlockSpec(s, lambda i: (0, 0)) for s in accs]

    def body(*refs):
        ins = [r[...].astype(F32) for r in refs[:nr + nb]]
        o_refs = refs[nr + nb:nr + nb + no]
        a_refs = refs[nr + nb + no:]
        ro, ao = fn(*ins)
        for r, v in zip(o_refs, ro):
            r[...] = v.astype(r.dtype)
        if na:
            @pl.when(pl.program_id(0) == 0)
            def _():
                for r in a_refs:
                    r[...] = jnp.zeros(r.shape, F32)

            for r, v in zip(a_refs, ao):
                r[...] += v

    res = pl.pallas_call(
        body,
        name=name,
        grid=(T // tr,),
        in_specs=in_specs,
        out_specs=out_specs,
        out_shape=out_shape,
        compiler_params=_cp(("arbitrary",) if na else ("parallel",)),
    )(*[r[0] for r in rows], *bcasts)
    return res


def _rms(v):
    r = lax.rsqrt(jnp.mean(v * v, axis=-1, keepdims=True) + EPS)
    return v * r, r


def _rms_bwd(dy, xn, r, g):
    dxn = dy * g
    dv = r * (dxn - xn * jnp.mean(dxn * xn, axis=-1, keepdims=True))
    return dv, jnp.sum(dy * xn, axis=0, keepdims=True)


def _sig(v):
    return 1.0 / (1.0 + jnp.exp(-v))


_GELU_C = math.sqrt(2.0 / math.pi)


def _gelu(v):
    return 0.5 * v * (1.0 + jnp.tanh(_GELU_C * (v + 0.044715 * v * v * v)))


def _gelu_grad(v):
    t = jnp.tanh(_GELU_C * (v + 0.044715 * v * v * v))
    return 0.5 * (1.0 + t) + 0.5 * v * (1.0 - t * t) * _GELU_C * (1.0 + 3.0 * 0.044715 * v * v)


def _rope(v, c, s, sign):
    w = v.shape[1]
    m = lax.broadcasted_iota(jnp.int32, v.shape, 1) % HEAD_DIM
    p = jnp.where(m < ROT_DIM // 2, -pltpu.roll(v, w - ROT_DIM // 2, 1), pltpu.roll(v, ROT_DIM // 2, 1))
    return v * c + sign * (p * s)


def _rope_tables(T):
    half = ROT_DIM // 2
    inv = ROPE_THETA ** (-jnp.arange(half, dtype=F32) * 2.0 / ROT_DIM)
    ang = jnp.arange(T).astype(F32)[:, None] * inv[None, :]
    cos, sin = jnp.cos(ang), jnp.sin(ang)
    one = jnp.ones((T, HEAD_DIM - ROT_DIM), F32)
    c64 = jnp.concatenate([cos, cos, one], axis=1)
    s64 = jnp.concatenate([sin, sin, 0.0 * one], axis=1)
    return jnp.tile(c64, (1, 2)), jnp.tile(s64, (1, 2))


def _dup_half(m, lo):
    lane = lax.broadcasted_iota(jnp.int32, m.shape, 1)
    sw = pltpu.roll(m, HEAD_DIM, 1)
    return jnp.where(lane < HEAD_DIM, m, sw) if lo else jnp.where(lane >= HEAD_DIM, m, sw)


def _attn_mask(i):
    qi = lax.broadcasted_iota(jnp.int32, (ATT_BLOCK, 2 * ATT_BLOCK), 0)
    kj = lax.broadcasted_iota(jnp.int32, (ATT_BLOCK, 2 * ATT_BLOCK), 1)
    rel = qi + ATT_BLOCK - kj
    return (rel >= 0) & (rel < ATT_BLOCK) & ((kj >= ATT_BLOCK) | (i > 0))


_NT = (((1,), (1,)), ((), ()))
_TN = (((0,), (0,)), ((), ()))


def _stack_heads(m):
    lane = lax.broadcasted_iota(jnp.int32, m.shape, 1)
    zero = jnp.zeros_like(m)
    return jnp.concatenate([jnp.where(lane < HEAD_DIM, m, zero), jnp.where(lane >= HEAD_DIM, m, zero)], axis=0)


def _pair_probs(q2, k2, ok2, sink_lo, sink_hi):
    qs = _stack_heads(q2)
    s = lax.dot_general(qs, k2, _NT, preferred_element_type=F32)
    s = jnp.where(ok2, s, NEG)
    row = lax.broadcasted_iota(jnp.int32, (2 * ATT_BLOCK, 1), 0)
    sink = jnp.where(row < ATT_BLOCK, sink_lo, sink_hi)
    m = jnp.maximum(jnp.max(s, axis=1, keepdims=True), sink)
    e = jnp.exp(s - m)
    es = jnp.exp(sink - m)
    inv = 1.0 / (jnp.sum(e, axis=1, keepdims=True) + es)
    return e * inv, es * inv, qs


def _attn_fwd(za, cos, sin, sinks, comm=None):
    T = za.shape[0]
    nb = T // ATT_BLOCK
    kvb = Q_W // (2 * KV_W)

    def body(sink_ref, q_ref, kvp_ref, kvc_ref, cc_ref, sc_ref, cp_ref, sp_ref, o_ref):
        i = pl.program_id(0)
        cc, sc, cp, sp = cc_ref[...], sc_ref[...], cp_ref[...], sp_ref[...]
        q = (_rope(q_ref[...], jnp.tile(cc, (1, 8)), jnp.tile(sc, (1, 8)), 1.0) * 0.125).astype(BF16)
        kvp, kvc = kvp_ref[...], kvc_ref[...]
        k = jnp.concatenate([_rope(kvp[:, :KV_W], cp, sp, 1.0), _rope(kvc[:, :KV_W], cc, sc, 1.0)], axis=0).astype(BF16)
        v = jnp.concatenate([kvp[:, KV_W:], kvc[:, KV_W:]], axis=0).astype(BF16)
        ok = _attn_mask(i)
        ok2 = jnp.concatenate([ok, ok], axis=0)
        lane = lax.broadcasted_iota(jnp.int32, (ATT_BLOCK, LANES), 1)
        for kvh in range(2):
            k2 = _dup_half(k, kvh == 0)
            v2 = _dup_half(v, kvh == 0)
            for pair in range(4):
                c0 = (kvh * 4 + pair) * LANES
                q2 = q[:, c0:c0 + LANES]
                p, _, _ = _pair_probs(q2, k2, ok2, sink_ref[0, 2 * (kvh * 4 + pair)], sink_ref[0, 2 * (kvh * 4 + pair) + 1])
                o = jnp.dot(p.astype(BF16), v2, preferred_element_type=F32)
                o_ref[:, c0:c0 + LANES] = jnp.where(lane < HEAD_DIM, o[:ATT_BLOCK], o[ATT_BLOCK:]).astype(BF16)

    blk = lambda w, f: pl.BlockSpec((ATT_BLOCK, w), f)
    res = _call(
        "attn_fwd", body, (nb,),
        [
            pl.BlockSpec(memory_space=pltpu.SMEM),
            blk(Q_W, lambda i: (i, 0)),
            blk(2 * KV_W, lambda i: (jnp.maximum(i - 1, 0), kvb)),
            blk(2 * KV_W, lambda i: (i, kvb)),
            blk(LANES, lambda i: (i, 0)),
            blk(LANES, lambda i: (i, 0)),
            blk(LANES, lambda i: (jnp.maximum(i - 1, 0), 0)),
            blk(LANES, lambda i: (jnp.maximum(i - 1, 0), 0)),
        ],
        [blk(Q_W, lambda i: (i, 0))], [SDS((T, Q_W), BF16)], [], ("parallel",),
        (sinks, za, za, za, cos, sin, cos, sin), comm)
    return res[0] if comm is None else (res[0], res[1:])


def _attn_bwd(za, cos, sin, sinks, o, do):
    T = za.shape[0]
    nb = T // ATT_BLOCK
    kvb = Q_W // (2 * KV_W)

    def body(sink_ref, q_ref, kvp_ref, kvc_ref, cc_ref, sc_ref, cp_ref, sp_ref, o_ref, do_ref,
             dq_ref, dkv_ref, dsk_ref, carry, dqs):
        i = pl.program_id(0)

        @pl.when(i == 0)
        def _():
            carry[...] = jnp.zeros(carry.shape, F32)
            dsk_ref[...] = jnp.zeros(dsk_ref.shape, F32)

        @pl.when(i < nb)
        def _():
            cc, sc, cp, sp = cc_ref[...], sc_ref[...], cp_ref[...], sp_ref[...]
            ccq, scq = jnp.tile(cc, (1, 8)), jnp.tile(sc, (1, 8))
            q = (_rope(q_ref[...], ccq, scq, 1.0) * 0.125).astype(BF16)
            kvp, kvc = kvp_ref[...], kvc_ref[...]
            k = jnp.concatenate([_rope(kvp[:, :KV_W], cp, sp, 1.0), _rope(kvc[:, :KV_W], cc, sc, 1.0)], axis=0).astype(BF16)
            v = jnp.concatenate([kvp[:, KV_W:], kvc[:, KV_W:]], axis=0).astype(BF16)
            ok = _attn_mask(i)
            ok2 = jnp.concatenate([ok, ok], axis=0)
            lane = lax.broadcasted_iota(jnp.int32, (ATT_BLOCK, LANES), 1)
            lane_s = lax.broadcasted_iota(jnp.int32, (1, LANES), 1)
            dsk = jnp.zeros((1, LANES), F32)
            dkt_h, dvt_h = [], []
            for kvh in range(2):
                k2 = _dup_half(k, kvh == 0)
                v2 = _dup_half(v, kvh == 0)
                dkt = jnp.zeros((LANES, 2 * ATT_BLOCK), F32)
                dvt = jnp.zeros((LANES, 2 * ATT_BLOCK), F32)
                for pair in range(4):
                    h = 2 * (kvh * 4 + pair)
                    c0 = (kvh * 4 + pair) * LANES
                    do2 = do_ref[:, c0:c0 + LANES]
                    prod = do2.astype(F32) * o_ref[:, c0:c0 + LANES].astype(F32)
                    d_lo = jnp.sum(jnp.where(lane < HEAD_DIM, prod, 0.0), axis=1, keepdims=True)
                    d_hi = jnp.sum(jnp.where(lane >= HEAD_DIM, prod, 0.0), axis=1, keepdims=True)
                    delta = jnp.concatenate([d_lo, d_hi], axis=0)
                    p, p_sink, qs = _pair_probs(q[:, c0:c0 + LANES], k2, ok2, sink_ref[0, h], sink_ref[0, h + 1])
                    dos = _stack_heads(do2)
                    t = p_sink * delta
                    dsk = dsk - jnp.where(lane_s == h, jnp.sum(t[:ATT_BLOCK]), 0.0) \
                              - jnp.where(lane_s == h + 1, jnp.sum(t[ATT_BLOCK:]), 0.0)
                    dp = lax.dot_general(dos, v2, _NT, preferred_element_type=F32)
                    ds = (p * (dp - delta)).astype(BF16)
                    dqp = jnp.dot(ds, k2, preferred_element_type=F32)
                    dqs[:, c0:c0 + LANES] = jnp.where(lane < HEAD_DIM, dqp[:ATT_BLOCK], dqp[ATT_BLOCK:]) * 0.125
                    dkt = dkt + lax.dot_general(qs, ds, _TN, preferred_element_type=F32)
                    dvt = dvt + lax.dot_general(dos, p.astype(BF16), _TN, preferred_element_type=F32)
                dkt_h.append(dkt[:HEAD_DIM] + dkt[HEAD_DIM:])
                dvt_h.append(dvt[:HEAD_DIM] + dvt[HEAD_DIM:])
            dk = jnp.concatenate(dkt_h, axis=0).T
            dv = jnp.concatenate(dvt_h, axis=0).T
            dq_ref[...] = _rope(dqs[...], ccq, scq, -1.0).astype(dq_ref.dtype)
            dkp = _rope(dk[:ATT_BLOCK], cp, sp, -1.0)
            dkc = _rope(dk[ATT_BLOCK:], cc, sc, -1.0)
            dkv_ref[...] = (carry[...] + jnp.concatenate([dkp, dv[:ATT_BLOCK]], axis=1)).astype(dkv_ref.dtype)
            carry[...] = jnp.concatenate([dkc, dv[ATT_BLOCK:]], axis=1)
            dsk_ref[...] += dsk

        @pl.when(i == nb)
        def _():
            dkv_ref[...] = carry[...].astype(dkv_ref.dtype)

    blk = lambda w, f: pl.BlockSpec((ATT_BLOCK, w), f)
    cur = lambda i: jnp.minimum(i, nb - 1)
    prv = lambda i: jnp.maximum(jnp.minimum(i, nb - 1) - 1, 0)
    return pl.pallas_call(
        body,
        name="attn_bwd",
        grid=(nb + 1,),
        in_specs=[
            pl.BlockSpec(memory_space=pltpu.SMEM),
            blk(Q_W, lambda i: (cur(i), 0)),
            blk(2 * KV_W, lambda i: (prv(i), kvb)),
            blk(2 * KV_W, lambda i: (cur(i), kvb)),
            blk(LANES, lambda i: (cur(i), 0)),
            blk(LANES, lambda i: (cur(i), 0)),
            blk(LANES, lambda i: (prv(i), 0)),
            blk(LANES, lambda i: (prv(i), 0)),
            blk(Q_W, lambda i: (cur(i), 0)),
            blk(Q_W, lambda i: (cur(i), 0)),
        ],
        out_specs=[
            blk(Q_W, lambda i: (cur(i), 0)),
            blk(2 * KV_W, lambda i: (jnp.maximum(i - 1, 0), 0)),
            pl.BlockSpec((1, LANES), lambda i: (0, 0)),
        ],
        out_shape=[SDS((T, Q_W), BF16), SDS((T, 2 * KV_W), BF16), SDS((1, LANES), F32)],
        scratch_shapes=[pltpu.VMEM((ATT_BLOCK, 2 * KV_W), F32), pltpu.VMEM((ATT_BLOCK, Q_W), F32)],
        compiler_params=_cp(("arbitrary",)),
    )(sinks, za, za, za, cos, sin, cos, sin, o, do)


def _s5_discretize(lam_re, lam_im, log_dt, b_re, b_im):
    dt = jnp.exp(log_dt)[:, None]
    mag = jnp.exp(lam_re * dt)
    a_re, a_im = mag * jnp.cos(lam_im * dt), mag * jnp.sin(lam_im * dt)
    den = lam_re * lam_re + lam_im * lam_im
    nr, ni = a_re - 1.0, a_im
    coef_re = (nr * lam_re + ni * lam_im) / den
    coef_im = (ni * lam_re - nr * lam_im) / den
    bb_re = coef_re[..., None] * b_re - coef_im[..., None] * b_im
    bb_im = coef_re[..., None] * b_im + coef_im[..., None] * b_re
    return a_re, a_im, bb_re, bb_im


def _blockdiag_in(bb):
    x = bb.reshape(N_JB, 8, SSM_P, SSM_GC).transpose(0, 1, 3, 2)
    return (x[:, :, :, None, :] * jnp.eye(8, dtype=bb.dtype)[None, :, None, :, None]).reshape(N_JB, 128, 512)


def _blockdiag_in_extract(m):
    x = m.reshape(N_JB, 8, SSM_GC, 8, SSM_P)
    x = jnp.einsum('jgchp,gh->jgcp', x, jnp.eye(8, dtype=m.dtype))
    return x.transpose(0, 1, 3, 2).reshape(SSM_G, SSM_P, SSM_GC)


def _blockdiag_out(c):
    x = c.reshape(N_JB, 8, SSM_GC, SSM_P).transpose(0, 1, 3, 2)
    return (x[:, :, :, None, :] * jnp.eye(8, dtype=c.dtype)[None, :, None, :, None]).reshape(N_JB, 512, 128)


def _blockdiag_out_extract(m):
    x = m.reshape(N_JB, 8, SSM_P, 8, SSM_GC)
    x = jnp.einsum('jgphc,gh->jgpc', x, jnp.eye(8, dtype=m.dtype))
    return x.transpose(0, 1, 3, 2).reshape(SSM_G, SSM_GC, SSM_P)


def _s5_tables(a_re, a_im):
    ar, ai = a_re.reshape(N_LG, 1, LANES), a_im.reshape(N_LG, 1, LANES)
    p_re, p_im, n = ar, ai, 1
    while n < S5_SEG:
        tr, ti = p_re[:, n - 1:n], p_im[:, n - 1:n]
        p_re, p_im = (jnp.concatenate([p_re, p_re * tr - p_im * ti], axis=1),
                      jnp.concatenate([p_im, p_re * ti + p_im * tr], axis=1))
        n *= 2
    bc = lambda v: jnp.broadcast_to(v, (N_LG, SUBLANES, LANES))
    return p_re, p_im, bc(ar), bc(ai)


def _s5_to_time_major(src_ref, dst_ref):
    for t in range(S5_SEG):
        dst_ref[t * SUBLANES:(t + 1) * SUBLANES, :] = src_ref[pl.ds(t, SUBLANES, stride=S5_SEG), :]


def _s5_from_time_major(val, dst_ref):
    for t in range(S5_SEG):
        dst_ref[pl.ds(t, SUBLANES, stride=S5_SEG), :] = val[t * SUBLANES:(t + 1) * SUBLANES, :]


def _tm_rows(t, row0=0):
    return pl.ds(pl.multiple_of(t * SUBLANES + row0, SUBLANES), SUBLANES)


def _s5_scan(src_re, src_im, dst_re, dst_im, ar, ai, reverse, dst_row0=0):
    def step(n, carry):
        t = (S5_SEG - 1 - n) if reverse else n
        out = []
        for ll in range(LG_PER_JB):
            xr, xi = carry[2 * ll], carry[2 * ll + 1]
            idx = (ll, _tm_rows(t), slice(None))
            odx = (ll, _tm_rows(t, dst_row0), slice(None))
            nr = ar[ll] * xr - ai[ll] * xi + src_re[idx]
            ni = ar[ll] * xi + ai[ll] * xr + src_im[idx]
            dst_re[odx] = nr
            dst_im[odx] = ni
            out += [nr, ni]
        return tuple(out)
    z = jnp.zeros((SUBLANES, LANES), F32)
    return lax.fori_loop(0, S5_SEG, step, (z,) * (2 * LG_PER_JB))


def _s5_fixup(ends, in_re, in_im, mr, mi, s_re, s_im, reverse):
    cr, ci = in_re, in_im
    order = range(SUBLANES - 1, -1, -1) if reverse else range(SUBLANES)
    for s in order:
        s_re[:, s:s + 1, :] = cr
        s_im[:, s:s + 1, :] = ci
        er = jnp.stack([ends[2 * ll][s:s + 1, :] for ll in range(LG_PER_JB)])
        ei = jnp.stack([ends[2 * ll + 1][s:s + 1, :] for ll in range(LG_PER_JB)])
        cr, ci = mr * cr - mi * ci + er, mr * ci + mi * cr + ei
    return cr, ci


def _s5_correct(x_re, x_im, s_re, s_im, p_re, p_im, row0=0):
    sr = [s_re[ll] for ll in range(LG_PER_JB)]
    si = [s_im[ll] for ll in range(LG_PER_JB)]

    def step(t, carry):
        for ll in range(LG_PER_JB):
            idx = (ll, _tm_rows(t, row0), slice(None))
            pr, pi = p_re[ll, pl.ds(t, 1), :], p_im[ll, pl.ds(t, 1), :]
            x_re[idx] = x_re[idx] + (pr * sr[ll] - pi * si[ll])
            x_im[idx] = x_im[idx] + (pr * si[ll] + pi * sr[ll])
        return carry
    lax.fori_loop(0, S5_SEG, step, 0)


def _s5_specs(nc, rev):
    cidx = (lambda c: nc - 1 - c) if rev else (lambda c: c)
    jb = lambda shape: pl.BlockSpec(shape, lambda j, c: (j, 0, 0))
    return cidx, [
        jb((1, LANES, 8 * LANES)),
        jb((1, 8 * LANES, LANES)),
        pl.BlockSpec((1, LANES), lambda j, c: (0, j)),
        jb((LG_PER_JB, SUBLANES, LANES)), jb((LG_PER_JB, SUBLANES, LANES)),
        jb((LG_PER_JB, 1, LANES)), jb((LG_PER_JB, 1, LANES)),
        jb((LG_PER_JB, S5_SEG, LANES)), jb((LG_PER_JB, S5_SEG, LANES)),
    ]


def _s5_fwd(za, prm, comm=None):
    T = za.shape[0]
    R = S5_CHUNK
    nc = T // R
    ub = (Q_W + 2 * KV_W) // LANES
    _, pspecs = _s5_specs(nc, False)

    def body(u_ref, b_ref, c_ref, d_ref, are_ref, aim_ref, alr_ref, ali_ref, pr_ref, pi_ref,
             yg_ref, x0r_ref, x0i_ref, bur, bui, xsr, xsi, sr, si, xcr, xci, utm, ynat):
        c = pl.program_id(1)

        @pl.when(c == 0)
        def _():
            xcr[...] = jnp.zeros(xcr.shape, F32)
            xci[...] = jnp.zeros(xci.shape, F32)

        _s5_to_time_major(u_ref, utm)
        u = utm[...]
        ub16 = u.astype(BF16)
        bu = jnp.dot(ub16, b_ref[0].astype(BF16), preferred_element_type=F32)
        for ll in range(LG_PER_JB):
            bur[ll] = bu[:, ll * LANES:(ll + 1) * LANES]
            bui[ll] = bu[:, (LG_PER_JB + ll) * LANES:(LG_PER_JB + ll + 1) * LANES]
        ar = [are_ref[ll] for ll in range(LG_PER_JB)]
        ai = [aim_ref[ll] for ll in range(LG_PER_JB)]
        ends = _s5_scan(bur, bui, xsr, xsi, ar, ai, False)
        in_r, in_i = xcr[...], xci[...]
        x0r_ref[0] = in_r
        x0i_ref[0] = in_i
        out_r, out_i = _s5_fixup(ends, in_r, in_i, alr_ref[...], ali_ref[...], sr, si, False)
        xcr[...] = out_r
        xci[...] = out_i
        _s5_correct(xsr, xsi, sr, si, pr_ref, pi_ref)
        xcat = jnp.concatenate([xsr[ll].astype(BF16) for ll in range(LG_PER_JB)]
                               + [xsi[ll].astype(BF16) for ll in range(LG_PER_JB)], axis=1)
        y = d_ref[...] * u + jnp.dot(xcat, c_ref[0].astype(BF16), preferred_element_type=F32)
        _s5_from_time_major(_gelu(y), ynat)
        yg_ref[...] = ynat[...].astype(BF16)

    st = pl.BlockSpec((1, LG_PER_JB, 1, LANES), lambda j, c: (c, j, 0, 0))
    vm = lambda rows: pltpu.VMEM((LG_PER_JB, rows, LANES), F32)
    res = _call(
        "s5_fwd", body, (N_JB, nc),
        [pl.BlockSpec((R, LANES), lambda j, c: (c, ub + j))] + pspecs,
        [pl.BlockSpec((R, LANES), lambda j, c: (c, j)), st, st],
        [SDS((T, SSM_W), BF16), SDS((nc, N_LG, 1, LANES), F32), SDS((nc, N_LG, 1, LANES), F32)],
        [vm(R), vm(R), vm(R), vm(R), vm(SUBLANES), vm(SUBLANES), vm(1), vm(1),
         pltpu.VMEM((R, LANES), F32), pltpu.VMEM((R, LANES), F32)],
        ("parallel", "arbitrary"), (za, *prm), comm)
    return res if comm is None else (res[:3], res[3:])


def _s5_bwd(za, dyg, x0r, x0i, prm, prev_tables, comm=None):
    T = za.shape[0]
    R = S5_CHUNK
    nc = T // R
    ub = (Q_W + 2 * KV_W) // LANES
    cidx, pspecs = _s5_specs(nc, True)
    PAD = SUBLANES

    def body(u_ref, dyg_ref, x0r_ref, x0i_ref, b_ref, c_ref, d_ref, are_ref, aim_ref,
             alr_ref, ali_ref, pr_ref, pi_ref, qr_ref, qi_ref,
             du_ref, dar_ref, dai_ref, db_ref, dc_ref, dd_ref,
             bur, bui, xsr, xsi, sr, si, gcr, gci, utm, dtm, dunat):
        c = pl.program_id(1)

        @pl.when(c == 0)
        def _():
            gcr[...] = jnp.zeros(gcr.shape, F32)
            gci[...] = jnp.zeros(gci.shape, F32)
            dar_ref[...] = jnp.zeros(dar_ref.shape, F32)
            dai_ref[...] = jnp.zeros(dai_ref.shape, F32)
            db_ref[...] = jnp.zeros(db_ref.shape, F32)
            dc_ref[...] = jnp.zeros(dc_ref.shape, F32)
            dd_ref[...] = jnp.zeros(dd_ref.shape, F32)

        _s5_to_time_major(u_ref, utm)
        _s5_to_time_major(dyg_ref, dtm)
        u = utm[...]
        ub16 = u.astype(BF16)
        bcat, ccat = b_ref[0].astype(BF16), c_ref[0].astype(BF16)
        lanes = lambda v, ll: v[:, ll * LANES:(ll + 1) * LANES]
        bu = jnp.dot(ub16, bcat, preferred_element_type=F32)
        for ll in range(LG_PER_JB):
            bur[ll] = lanes(bu, ll)
            bui[ll] = lanes(bu, LG_PER_JB + ll)
        ar = [are_ref[ll] for ll in range(LG_PER_JB)]
        ai = [aim_ref[ll] for ll in range(LG_PER_JB)]
        ends = _s5_scan(bur, bui, xsr, xsi, ar, ai, False, dst_row0=PAD)
        in_r, in_i = x0r_ref[0], x0i_ref[0]
        _s5_fixup(ends, in_r, in_i, alr_ref[...], ali_ref[...], sr, si, False)
        _s5_correct(xsr, xsi, sr, si, pr_ref, pi_ref, PAD)
        xsr[:, 0:PAD, :] = sr[...]
        xsi[:, 0:PAD, :] = si[...]
        xcat = jnp.concatenate([xsr[ll, PAD:, :].astype(BF16) for ll in range(LG_PER_JB)]
                               + [xsi[ll, PAD:, :].astype(BF16) for ll in range(LG_PER_JB)], axis=1)
        y = d_ref[...] * u + jnp.dot(xcat, ccat, preferred_element_type=F32)
        dy = dtm[...] * _gelu_grad(y)
        dyb = dy.astype(BF16)
        dd_ref[...] += jnp.sum(dy * u, axis=0, keepdims=True)
        du = d_ref[...] * dy
        dc_ref[0] += lax.dot_general(xcat, dyb, _TN, preferred_element_type=F32)
        g = lax.dot_general(dyb, ccat, _NT, preferred_element_type=F32)
        for ll in range(LG_PER_JB):
            bur[ll] = lanes(g, ll)
            bui[ll] = lanes(g, LG_PER_JB + ll)
        ends = _s5_scan(bur, bui, bur, bui, ar, [-v for v in ai], True)
        out_r, out_i = _s5_fixup(ends, gcr[...], gci[...], alr_ref[...], -ali_ref[...], sr, si, True)
        gcr[...] = out_r
        gci[...] = out_i
        _s5_correct(bur, bui, sr, si, qr_ref, qi_ref)
        for ll in range(LG_PER_JB):
            gr, gi = bur[ll], bui[ll]
            xpr, xpi = xsr[ll, 0:R, :], xsi[ll, 0:R, :]
            red = lambda v: v.reshape(R // SUBLANES, SUBLANES, LANES).sum(axis=0)
            dar_ref[ll] += red(xpr * gr + xpi * gi)
            dai_ref[ll] += red(xpr * gi - xpi * gr)
        gcat = jnp.concatenate([bur[ll].astype(BF16) for ll in range(LG_PER_JB)]
                               + [bui[ll].astype(BF16) for ll in range(LG_PER_JB)], axis=1)
        db_ref[0] += lax.dot_general(ub16, gcat, _TN, preferred_element_type=F32)
        du = du + lax.dot_general(gcat, bcat, _NT, preferred_element_type=F32)
        _s5_from_time_major(du, dunat)
        du_ref[...] = dunat[...].astype(du_ref.dtype)

    st = pl.BlockSpec((1, LG_PER_JB, 1, LANES), lambda j, c: (cidx(c), j, 0, 0))
    jb = lambda shape: pl.BlockSpec(shape, lambda j, c: (j, 0, 0))
    vm = lambda rows: pltpu.VMEM((LG_PER_JB, rows, LANES), F32)
    res = _call(
        "s5_bwd", body, (N_JB, nc),
        [pl.BlockSpec((R, LANES), lambda j, c: (cidx(c), ub + j)),
         pl.BlockSpec((R, LANES), lambda j, c: (cidx(c), j)), st, st] + pspecs
        + [jb((LG_PER_JB, S5_SEG, LANES)), jb((LG_PER_JB, S5_SEG, LANES))],
        [pl.BlockSpec((R, LANES), lambda j, c: (cidx(c), j)),
         jb((LG_PER_JB, SUBLANES, LANES)), jb((LG_PER_JB, SUBLANES, LANES)),
         jb((1, LANES, 8 * LANES)), jb((1, 8 * LANES, LANES)),
         pl.BlockSpec((1, LANES), lambda j, c: (0, j))],
        [SDS((T, SSM_W), BF16), SDS((N_LG, SUBLANES, LANES), F32), SDS((N_LG, SUBLANES, LANES), F32),
         SDS((N_JB, LANES, 8 * LANES), F32), SDS((N_JB, 8 * LANES, LANES), F32), SDS((1, SSM_W), F32)],
        [vm(R), vm(R), vm(R + PAD), vm(R + PAD), vm(SUBLANES), vm(SUBLANES), vm(1), vm(1)]
        + [pltpu.VMEM((R, LANES), F32)] * 3,
        ("parallel", "arbitrary"), (za, dyg, x0r, x0i, *prm, *prev_tables), comm)
    return res if comm is None else (res[:6], res[6:])


def _local_step(x, target, gains, w_a, w_g, sinks, s5w, comms, late, red=None):
    T = x.shape[0]
    D = D_MODEL
    g1, g2, g3, g4 = gains
    cos, sin = _rope_tables(T)
    lam_re, lam_im, log_dt, b_re, b_im, c_re, c_im, d_skip = s5w
    (a_re, a_im, bb_re, bb_im), disc_vjp = jax.vjp(_s5_discretize, lam_re, lam_im, log_dt, b_re, b_im)
    p_re, p_im, abr, abi = _s5_tables(a_re, a_im)
    prm = (jnp.concatenate([_blockdiag_in(bb_re), _blockdiag_in(bb_im)], axis=2),
           jnp.concatenate([_blockdiag_out(c_re), -_blockdiag_out(c_im)], axis=1),
           d_skip.reshape(1, SSM_W), abr, abi, p_re[:, S5_SEG - 1:, :], p_im[:, S5_SEG - 1:, :], p_re, p_im)
    rev_tables = (p_re[:, ::-1, :], -p_im[:, ::-1, :])
    mm = functools.partial(_mm, tm=1024, tn=1024, tk=2048)

    h = _rowwise(lambda xv, g: ((_rms(xv)[0] * g,), ()), [(x, D, 0)], [g1], [(D, BF16)], [], tr=512, name="norm1")[0]
    za = _mm(h, w_a, mode="nn", out_dtype=F32, tm=1024, tn=1152, tk=2048, name="mm_za")
    unpack = lambda res, comm: (res, ()) if comm is None else res
    zg, got0 = unpack(mm(h, w_g, mode="nn", out_dtype=BF16, name="mm_zg", comm=comms[0]), comms[0])
    o_attn, got1 = unpack(_attn_fwd(za, cos, sin, sinks, comm=comms[1]), comms[1])
    (yg, x0r, x0i), got2 = unpack(_s5_fwd(za, prm, comm=comms[2]), comms[2])
    w_glu, w_ba, w_bs, w_out, w_up, w_down = late(got0, got1, got2)
    zglu = mm(yg, w_glu, mode="nn", out_dtype=BF16, name="mm_glu")
    o_ssm = _rowwise(lambda z1, z2: ((z1 * _sig(z2),), ()), [(zglu, SSM_W, 0), (zglu, SSM_W, 1)], [],
                     [(SSM_W, BF16)], [], tr=512, name="glu")[0]
    ya = mm(o_attn, w_ba, mode="nn", out_dtype=BF16, name="mm_ya")
    ys = mm(o_ssm, w_bs, mode="nn", out_dtype=BF16, name="mm_ys")
    mi = _rowwise(lambda ga, gs, a, s: ((_sig(ga) * a + _sig(gs) * s,), ()),
                  [(zg, D, 0), (zg, D, 1), (ya, D, 0), (ys, D, 0)], [], [(D, BF16)], [], tr=256, name="gate")[0]
    mixed = mm(mi, w_out, mode="nn", out_dtype=F32, name="mm_out")

    def f_post(xv, mv, g2v, g3v):
        x1v = xv + _rms(mv)[0] * g2v
        return (x1v, _rms(x1v)[0] * g3v), ()
    x1, h2 = _rowwise(f_post, [(x, D, 0), (mixed, D, 0)], [g2, g3], [(D, F32), (D, BF16)], [], tr=256, name="post_mix")
    act = mm(h2, w_up, mode="nn", out_dtype=BF16, name="mm_up", epi=lambda v: jnp.maximum(v, 0.0))
    f = mm(act, w_down, mode="nn", out_dtype=F32, name="mm_down", a_fn=lambda v: v * v, tk=4096)

    def f_final(x1v, fv, tv, g4v):
        fn, r = _rms(fv)
        e = x1v + fn * g4v - tv
        dx2v = e * (1.0 / D)
        dfv, dg4v = _rms_bwd(dx2v, fn, r, g4v)
        return (dfv, dx2v), (dg4v, jnp.zeros((SUBLANES, LANES), F32) + 0.5 * jnp.sum(e * e) * (1.0 / D))
    df, dx2, dg4, lossb = _rowwise(f_final, [(x1, D, 0), (f, D, 0), (target, D, 0)], [g4],
                                   [(D, BF16), (D, F32)], [(1, D), (SUBLANES, LANES)], tr=256, name="final")

    big = {}

    def add(k, g4):
        big[k] = g4
        if red is not None:
            red.add(k, g4)

    def hosted(fn, stage, names):
        if red is None:
            return fn(comm=None)
        out, got = fn(comm=getattr(red, stage)(names))
        getattr(red, stage + "_done")(names, got)
        return out

    dpre = mm(df, w_down, mode="nt", out_dtype=BF16, name="mm_dact", epi=lambda v, a: v * (2.0 * a.astype(F32)), extras=(act,))
    wg = functools.partial(_mm, mode="tn", out_dtype=F32, tm=1024, tn=1024, tk=4096)
    add("w_down", wg(act, df, name="wg_down", a_fn=lambda v: v * v).reshape(4, D_FF // 4, D))
    dh2 = hosted(functools.partial(mm, dpre, w_up, mode="nt", out_dtype=F32, name="mm_dh2", tk=4096),
                 "s1", ["w_down"])
    add("w_up", hosted(functools.partial(wg, h2, dpre, name="wg_up", shard_cols=D_FF // 4), "s3", ["w_down"]))

    def f_mid(dx2v, dh2v, x1v, mv, g2v, g3v):
        x1n, r3 = _rms(x1v)
        d3, dg3v = _rms_bwd(dh2v, x1n, r3, g3v)
        dx1v = dx2v + d3
        mn, r2 = _rms(mv)
        dmv, dg2v = _rms_bwd(dx1v, mn, r2, g2v)
        return (dx1v, dmv), (dg3v, dg2v)
    dx1, dmixed, dg3, dg2 = _rowwise(f_mid, [(dx2, D, 0), (dh2, D, 0), (x1, D, 0), (mixed, D, 0)], [g2, g3],
                                     [(D, F32), (D, BF16)], [(1, D), (1, D)], tr=256, name="mid")

    dmi = hosted(functools.partial(mm, dmixed, w_out, mode="nt", out_dtype=BF16, name="mm_dmi"), "s1", ["w_up"])
    add("w_out", wg(mi, dmixed, name="wg_out").reshape(4, D // 4, D))

    def f_gate(dv, ga, gs, a, s):
        sa, ss = _sig(ga), _sig(gs)
        return (dv * sa, dv * ss, jnp.concatenate([dv * a * sa * (1.0 - sa), dv * s * ss * (1.0 - ss)], axis=1)), ()
    dya, dys, dzg = _rowwise(f_gate, [(dmi, D, 0), (zg, D, 0), (zg, D, 1), (ya, D, 0), (ys, D, 0)], [],
                             [(D, BF16), (D, BF16), (2 * D, BF16)], [], tr=256, name="gate_bwd")
    do_attn = hosted(functools.partial(mm, dya, w_ba, mode="nt", out_dtype=BF16, name="mm_doa"), "s1", ["w_out"])
    d_w_ba = wg(o_attn, dya, name="wg_ba")
    do_ssm = mm(dys, w_bs, mode="nt", out_dtype=BF16, name="mm_dos")
    d_w_bs = wg(o_ssm, dys, name="wg_bs")
    add("w_branch", jnp.concatenate([d_w_ba.reshape(2, D // 4, D), d_w_bs.reshape(2, D // 4, D)], axis=0))

    def f_glu(dv, z1, z2):
        s2 = _sig(z2)
        return (jnp.concatenate([dv * s2, dv * z1 * s2 * (1.0 - s2)], axis=1),), ()
    dzglu = _rowwise(f_glu, [(do_ssm, SSM_W, 0), (zglu, SSM_W, 0), (zglu, SSM_W, 1)], [], [(2 * SSM_W, BF16)], [],
                     tr=512, name="glu_bwd")[0]
    dyg = hosted(functools.partial(mm, dzglu, w_glu, mode="nt", out_dtype=F32, name="mm_dyg"), "s1", ["w_branch"])
    add("w_glu", wg(yg, dzglu, name="wg_glu", tn=SSM_W // 2, shard_cols=SSM_W // 2))
    du, dar, dai, dbc, dcc, ddv = hosted(functools.partial(_s5_bwd, za, dyg, x0r, x0i, prm, rev_tables),
                                         "s3", ["w_up", "w_out", "w_branch"])
    dbr, dbi = dbc[:, :, :4 * LANES], dbc[:, :, 4 * LANES:]
    dcr, dci = dcc[:, :4 * LANES, :], -dcc[:, 4 * LANES:, :]
    dq, dkv, dsk = _attn_bwd(za, cos, sin, sinks, o_attn, do_attn)
    dza = jnp.concatenate([dq, dkv, du], axis=1)
    d_w_a = _mm(h, dza, mode="tn", out_dtype=F32, tm=1024, tn=ZA_W // 2, tk=2048, name="wg_a")
    d_w_g = wg(h, dzg, name="wg_g")
    d_w_in = jnp.concatenate([d_w_a, d_w_g], axis=1)
    add("w_in", d_w_in.reshape(D, 4, d_w_in.shape[1] // 4).transpose(1, 0, 2))
    dh = hosted(functools.partial(mm, dza, w_a, mode="nt", out_dtype=F32, name="mm_dh_a", tk=ZA_W), "s1", ["w_in", "w_glu"])
    dh = hosted(functools.partial(mm, dzg, w_g, mode="nt", out_dtype=F32, name="mm_dh_g",
                                  epi=lambda v, p: v + p, extras=(dh,)), "s3", ["w_in", "w_glu"])

    def f_first(dx1v, dhv, xv, g1v):
        xn, r1 = _rms(xv)
        d1, dg1v = _rms_bwd(dhv, xn, r1, g1v)
        return (dx1v + d1,), (dg1v,)
    dx, dg1 = _rowwise(f_first, [(dx1, D, 0), (dh, D, 0), (x, D, 0)], [g1], [(D, F32)], [(1, D)], tr=256, name="first")

    da_re = dar.sum(axis=1).reshape(SSM_G, SSM_P)
    da_im = dai.sum(axis=1).reshape(SSM_G, SSM_P)
    d_lam_re, d_lam_im, d_log_dt, d_b_re, d_b_im = disc_vjp(
        (da_re, da_im, _blockdiag_in_extract(dbr), _blockdiag_in_extract(dbi)))
    small = dict(norm_mix_pre=dg1, norm_mix_post=dg2, norm_mlp_pre=dg3, norm_mlp_post=dg4,
                 sinks=dsk[:, :N_Q_HEADS], lam_re=d_lam_re, lam_im=d_lam_im, log_dt=d_log_dt,
                 b_re=d_b_re, b_im=d_b_im, c_re=_blockdiag_out_extract(dcr), c_im=_blockdiag_out_extract(dci),
                 d_skip=ddv.reshape(SSM_G, SSM_GC))
    return lossb[0, 0], dx, small, big


def _cast_into_slot(w, k_arr):
    rows, cols = w.shape
    tr = 256

    def body(k_ref, w_ref, o_ref):
        o_ref[0] = w_ref[...].astype(BF16)

    return pl.pallas_call(
        body,
        name="cast_into_slot",
        grid_spec=pltpu.PrefetchScalarGridSpec(
            num_scalar_prefetch=1,
            grid=(rows // tr,),
            in_specs=[pl.BlockSpec((tr, cols), lambda i, k: (i, 0))],
            out_specs=pl.BlockSpec((1, tr, cols), lambda i, k: (k[0], i, 0)),
        ),
        out_shape=SDS((4, rows, cols), BF16),
        compiler_params=_cp(("parallel",)),
    )(k_arr, w)


def _pair_sum(g, r, c_arr):
    _, _, hr, cols = g.shape
    tr = min(256, hr)

    def body(c_ref, g_ref, r_ref, o_ref):
        o_ref[0] = (g_ref[0, 0] + r_ref[0]).astype(BF16)

    return pl.pallas_call(
        body,
        name="pair_sum",
        grid_spec=pltpu.PrefetchScalarGridSpec(
            num_scalar_prefetch=1,
            grid=(4, hr // tr),
            in_specs=[pl.BlockSpec((1, 1, tr, cols), lambda k, i, c_ref: (k, c_ref[0], i, 0)),
                      pl.BlockSpec((1, tr, cols), lambda k, i, c_ref: (k, i, 0))],
            out_specs=pl.BlockSpec((1, tr, cols), lambda k, i, c_ref: (k, i, 0)),
        ),
        out_shape=SDS((4, hr, cols), BF16),
        compiler_params=_cp(("parallel", "parallel")),
    )(c_arr, g, r)


def _chip_sum(g, r, q, kc_arr):
    _, _, hr, cols = g.shape
    tr = min(256, hr)

    def body(kc_ref, g_ref, r_ref, q_ref, o_ref):
        s = g_ref[0, 0] + r_ref[0]
        for j in range(3):
            s = s + q_ref[j].astype(F32)
        o_ref[...] = s

    return pl.pallas_call(
        body,
        name="chip_sum",
        grid_spec=pltpu.PrefetchScalarGridSpec(
            num_scalar_prefetch=1,
            grid=(hr // tr,),
            in_specs=[pl.BlockSpec((1, 1, tr, cols), lambda i, kc: (kc[0], kc[1], i, 0)),
                      pl.BlockSpec((1, tr, cols), lambda i, kc: (kc[0], i, 0)),
                      pl.BlockSpec((3, tr, cols), lambda i, kc: (0, i, 0))],
            out_specs=pl.BlockSpec((tr, cols), lambda i, kc: (kc[1] * (hr // tr) + i, 0)),
        ),
        out_shape=SDS((2 * hr, cols), F32),
        compiler_params=_cp(("parallel",)),
    )(kc_arr, g, r, q)


def _pair_share(blocks):
    n = len(blocks)

    def body(*refs):
        ins, outs = refs[:n], refs[n:2 * n]
        ssem, rsem = refs[2 * n:]
        x, y, c, _ = _place()
        cps = []
        for w in range(n):
            hr = ins[w].shape[0] // 2
            rows = pl.ds(pl.multiple_of(c * hr, 8), hr)
            cp = _remote(ins[w].at[rows, :], outs[w].at[rows, :], ssem.at[w], rsem.at[w], (x, y, 1 - c))
            cp.start()
            cps.append(cp)
        for w in range(n):
            hr = ins[w].shape[0] // 2
            other = outs[w].at[pl.ds(pl.multiple_of((1 - c) * hr, 8), hr), :]
            _remote(other, other, ssem.at[w], rsem.at[w], (x, y, 1 - c)).wait_recv()
        for cp in cps:
            cp.wait_send()

    dma = pltpu.SemaphoreType.DMA
    return pl.pallas_call(
        body,
        name="pair_share",
        in_specs=[ANY] * n,
        out_specs=[ANY] * n,
        out_shape=[SDS(b.shape, b.dtype) for b in blocks],
        input_output_aliases={w: w for w in range(n)},
        scratch_shapes=[dma((n,)), dma((n,))],
    )(*blocks)


class _GradReducer:
    def __init__(self, c_arr, kc_arr):
        self.c_arr, self.kc_arr = c_arr, kc_arr
        self.g, self.r, self.ps, self.q = {}, {}, {}, {}

    def add(self, k, g4):
        self.g[k] = g4.reshape(4, 2, g4.shape[1] // 2, g4.shape[2])

    def s1(self, names):
        return _PairExchangeComm([self.g[k].reshape(4, -1, self.g[k].shape[3]) for k in names])

    def s1_done(self, names, got):
        for k, r in zip(names, got):
            self.r[k] = r
            self.ps[k] = _pair_sum(self.g[k], r, self.c_arr)

    def s3(self, names):
        return _ChipExchangeComm([self.ps[k] for k in names])

    def s3_done(self, names, got):
        self.q.update(zip(names, got))

    def finish(self, order):
        rest = [k for k in order if k not in self.r]
        if rest:
            self.s1_done(rest, _comm_only("pair_exchange", self.s1(rest)))
        rest = [k for k in order if k not in self.q]
        if rest:
            self.s3_done(rest, _comm_only("chip_exchange", self.s3(rest)))
        blocks = [_chip_sum(self.g[k], self.r[k], self.q[k], self.kc_arr) for k in order]
        return dict(zip(order, _pair_share(blocks)))


def _all_reduce_small(buf):
    rows = buf.shape[0]
    hr = rows // 2
    assert hr % SUBLANES == 0

    def body(in_ref, o_ref, sib, pair, slots, ssem, rsem):
        x, y, c, others = _place()
        me, sibling = 2 * x + y, (x, y, 1 - c)
        mine = pl.ds(pl.multiple_of(c * hr, SUBLANES), hr)
        theirs = pl.ds(pl.multiple_of((1 - c) * hr, SUBLANES), hr)
        first = _remote(in_ref, sib, ssem.at[0], rsem.at[0], sibling)
        first.start()
        first.wait()
        pair[...] = in_ref[...] + sib[...]
        slots[me] = pair[mine, :]
        cps = [_remote(pair.at[mine, :], slots.at[me], ssem.at[1 + r], rsem.at[1 + r], (ox, oy, c))
               for r, (ox, oy) in enumerate(others)]
        for cp in cps:
            cp.start()
        for r, (ox, oy) in enumerate(others):
            _remote(pair.at[mine, :], slots.at[2 * ox + oy], ssem.at[1 + r], rsem.at[1 + r], (ox, oy, c)).wait_recv()
        o_ref[mine, :] = (slots[0] + slots[1]) + (slots[2] + slots[3])
        last = _remote(o_ref.at[mine, :], o_ref.at[mine, :], ssem.at[4], rsem.at[4], sibling)
        last.start()
        _remote(o_ref.at[theirs, :], o_ref.at[theirs, :], ssem.at[4], rsem.at[4], sibling).wait_recv()
        last.wait_send()
        for cp in cps:
            cp.wait_send()

    dma = pltpu.SemaphoreType.DMA
    return pl.pallas_call(
        body,
        name="all_reduce_small",
        in_specs=[pl.BlockSpec(memory_space=pltpu.VMEM)],
        out_specs=pl.BlockSpec(memory_space=pltpu.VMEM),
        out_shape=SDS(buf.shape, F32),
        scratch_shapes=[pltpu.VMEM((rows, LANES), F32), pltpu.VMEM((rows, LANES), F32),
                        pltpu.VMEM((4, hr, LANES), F32), dma((5,)), dma((5,))],
        compiler_params=pltpu.CompilerParams(vmem_limit_bytes=VMEM_LIMIT),
    )(buf)


def _adam_fn(w, g, m, v):
    m2 = ADAM_B1 * m + (1.0 - ADAM_B1) * g
    v2 = ADAM_B2 * v + (1.0 - ADAM_B2) * (g * g)
    m_hat = m2 / (1.0 - ADAM_B1 ** ADAM_STEP)
    v_hat = v2 / (1.0 - ADAM_B2 ** ADAM_STEP)
    return (-ADAM_LR * (m_hat / (jnp.sqrt(v_hat) + ADAM_EPS) + ADAM_WD * w), m2, v2), ()


def _adamw(w, g, m, v, name, tr=256):
    cols = w.shape[1]
    return _rowwise(_adam_fn, [(w, cols, 0), (g, cols, 0), (m, cols, 0), (v, cols, 0)], [],
                    [(cols, F32)] * 3, [], tr=tr, name=name)


BIG = ("w_in", "w_glu", "w_branch", "w_out", "w_up", "w_down")
COL_SHARDED = ("w_in", "w_glu", "w_up")
SMALL = ("norm_mix_pre", "norm_mix_post", "norm_mlp_pre", "norm_mlp_post", "sinks", "lam_re", "lam_im", "log_dt",
         "b_re", "b_im", "c_re", "c_im", "d_skip")
WEIGHTS = ("norm_mix_pre", "norm_mix_post", "norm_mlp_pre", "norm_mlp_post", "w_in", "sinks", "lam_re", "lam_im",
           "log_dt", "b_re", "b_im", "c_re", "c_im", "d_skip", "w_glu", "w_branch", "w_out", "w_up", "w_down")


def _flat_small(vals, extra):
    flat = jnp.concatenate([vals[k].reshape(-1) for k in SMALL] + [extra.reshape(-1)])
    rows = -(-flat.shape[0] // (SUBLANES * LANES)) * SUBLANES
    return jnp.pad(flat, (0, rows * LANES - flat.shape[0])).reshape(rows, LANES)


def kernel(x, norm_mix_pre, norm_mix_post, norm_mlp_pre, norm_mlp_post, w_in, sinks, lam_re, lam_im, log_dt, b_re, b_im, c_re, c_im, d_skip, w_glu, w_branch, w_out, w_up, w_down, loss_target, m_norm_mix_pre, m_norm_mix_post, m_norm_mlp_pre, m_norm_mlp_post, m_w_in, m_sinks, m_lam_re, m_lam_im, m_log_dt, m_b_re, m_b_im, m_c_re, m_c_im, m_d_skip, m_w_glu, m_w_branch, m_w_out, m_w_up, m_w_down, v_norm_mix_pre, v_norm_mix_post, v_norm_mlp_pre, v_norm_mlp_post, v_w_in, v_sinks, v_lam_re, v_lam_im, v_log_dt, v_b_re, v_b_im, v_c_re, v_c_im, v_d_skip, v_w_glu, v_w_branch, v_w_out, v_w_up, v_w_down):
    w = dict(norm_mix_pre=norm_mix_pre, norm_mix_post=norm_mix_post, norm_mlp_pre=norm_mlp_pre, norm_mlp_post=norm_mlp_post,
             w_in=w_in, sinks=sinks, lam_re=lam_re, lam_im=lam_im, log_dt=log_dt, b_re=b_re, b_im=b_im, c_re=c_re,
             c_im=c_im, d_skip=d_skip, w_glu=w_glu, w_branch=w_branch, w_out=w_out, w_up=w_up, w_down=w_down)
    m = dict(norm_mix_pre=m_norm_mix_pre, norm_mix_post=m_norm_mix_post, norm_mlp_pre=m_norm_mlp_pre,
             norm_mlp_post=m_norm_mlp_post, w_in=m_w_in, sinks=m_sinks, lam_re=m_lam_re, lam_im=m_lam_im,
             log_dt=m_log_dt, b_re=m_b_re, b_im=m_b_im, c_re=m_c_re, c_im=m_c_im, d_skip=m_d_skip, w_glu=m_w_glu,
             w_branch=m_w_branch, w_out=m_w_out, w_up=m_w_up, w_down=m_w_down)
    v = dict(norm_mix_pre=v_norm_mix_pre, norm_mix_post=v_norm_mix_post, norm_mlp_pre=v_norm_mlp_pre,
             norm_mlp_post=v_norm_mlp_post, w_in=v_w_in, sinks=v_sinks, lam_re=v_lam_re, lam_im=v_lam_im,
             log_dt=v_log_dt, b_re=v_b_re, b_im=v_b_im, c_re=v_c_re, c_im=v_c_im, d_skip=v_d_skip, w_glu=v_w_glu,
             w_branch=v_w_branch, w_out=v_w_out, w_up=v_w_up, w_down=v_w_down)
    xi, yi, ci = lax.axis_index("x"), lax.axis_index("y"), lax.axis_index("c")

    k_arr = jnp.stack([2 * xi + yi]).astype(jnp.int32)
    slot = {k: _cast_into_slot(w[k][0], k_arr) for k in BIG}

    def whole(k, g4):
        if k in COL_SHARDED:
            return jnp.concatenate([g4[j] for j in range(4)], axis=1)
        return g4.reshape(4 * g4.shape[1], g4.shape[2])

    w_in_b = whole("w_in", _comm_only("gather_w_in", _GatherComm([slot["w_in"]]))[0])
    hosted = (("w_glu", "w_branch", "w_out"), ("w_up",), ("w_down",))
    comms = [_GatherComm([slot[k] for k in names]) for names in hosted]

    def late(*got):
        f = {k: whole(k, g4) for names, res in zip(hosted, got) for k, g4 in zip(names, res)}
        return f["w_glu"], f["w_branch"][:Q_W], f["w_branch"][Q_W:], f["w_out"], f["w_up"], f["w_down"]

    s5w = (lam_re[0], lam_im[0], log_dt[0], b_re[0], b_im[0], c_re[0], c_im[0], d_skip[0])
    reducer = _GradReducer(jnp.stack([ci]).astype(jnp.int32), jnp.stack([2 * xi + yi, ci]).astype(jnp.int32))
    loss_part, dx, small, _ = _local_step(
        x[0], loss_target[0], (norm_mix_pre, norm_mix_post, norm_mlp_pre, norm_mlp_post),
        w_in_b[:, :ZA_W], w_in_b[:, ZA_W:], sinks, s5w, comms, late, reducer)
    grads = reducer.finish(BIG)

    red = _all_reduce_small(_flat_small(small, loss_part)).reshape(-1)
    off = 0
    for k in SMALL:
        n = math.prod(w[k].shape)
        grads[k] = red[off:off + n].reshape(w[k].shape[1:])
        off += n
    loss = red[off]

    delta, new_m, new_v = {}, {}, {}
    for k in BIG:
        delta[k], new_m[k], new_v[k] = _adamw(w[k][0], grads[k], m[k][0], v[k][0], "adamw_" + k)
    zero = jnp.zeros((), F32)
    fw, fm, fv = (_flat_small({k: t[k] for k in SMALL}, zero) for t in (w, m, v))
    fg = _flat_small(grads, zero)
    sd, sm, sv = _adamw(fw, fg, fm, fv, "adamw_small", tr=fw.shape[0])
    off = 0
    for k in SMALL:
        n = math.prod(w[k].shape)
        delta[k], new_m[k], new_v[k] = (t.reshape(-1)[off:off + n].reshape(w[k].shape[1:]) for t in (sd, sm, sv))
        off += n

    lead = lambda t: t[None]
    return (loss, lead(dx), *[lead(grads[k]) for k in WEIGHTS], *[lead(delta[k]) for k in WEIGHTS],
            *[lead(new_m[k]) for k in WEIGHTS], *[lead(new_v[k]) for k in WEIGHTS])
```

```python
import functools
import math

import jax
import jax.numpy as jnp
from jax import lax
from jax.experimental import pallas as pl
from jax.experimental.pallas import tpu as pltpu

F32 = jnp.float32
BF16 = jnp.bfloat16
SDS = jax.ShapeDtypeStruct

D_MODEL = 2048
HEAD_DIM = 64
N_Q_HEADS = 16
ATT_BLOCK = 128
ROT_DIM = 16
ROPE_THETA = 500000.0
Q_W = 1024
KV_W = 128
SSM_W = 1024
SSM_G = 64
SSM_GC = 16
SSM_P = 64
N_STATE = SSM_G * SSM_P
LANES = 128
SUBLANES = 8
N_LG = N_STATE // LANES
N_JB = 8
LG_PER_JB = N_LG // N_JB
D_FF = 8192
ZA_W = Q_W + 2 * KV_W + SSM_W
EPS = 1e-6
S5_CHUNK = 512
S5_SEG = S5_CHUNK // SUBLANES
VMEM_LIMIT = 56 * 1024 * 1024
NEG = -1e30

ADAM_LR = 0.001
ADAM_B1 = 0.9
ADAM_B2 = 0.999
ADAM_EPS = 1e-08
ADAM_WD = 0.01
ADAM_STEP = 10

MESH = pl.DeviceIdType.MESH


def _cp(sem):
    return pltpu.CompilerParams(dimension_semantics=sem, vmem_limit_bytes=VMEM_LIMIT)


ANY = pl.BlockSpec(memory_space=pl.ANY)


def _place():
    x, y, c = lax.axis_index("x"), lax.axis_index("y"), lax.axis_index("c")
    others = [(1 - x, y), (x, 1 - y), (1 - x, 1 - y)]
    return x, y, c, others


def _remote(src, dst, ssem, rsem, to):
    return pltpu.make_async_remote_copy(src_ref=src, dst_ref=dst, send_sem=ssem, recv_sem=rsem,
                                        device_id=to, device_id_type=MESH)


class _GatherComm:
    aliased = True

    def __init__(self, slotted):
        self.arrs = list(slotted)
        self.n = len(self.arrs)
        dma = pltpu.SemaphoreType.DMA
        self.scratch = [dma((3 * self.n,)) for _ in range(4)]
        self.out_shape = [SDS(s.shape, s.dtype) for s in self.arrs]

    @staticmethod
    def _half(ref, hc):
        hr = ref.shape[1] // 2
        return pl.ds(pl.multiple_of(hc * hr, 16), hr)

    def start(self, ins, outs, sems):
        ssem, rsem, _, _ = sems
        x, y, c, others = _place()
        me = 2 * x + y
        for w in range(self.n):
            for r, (ox, oy) in enumerate(others):
                _remote(ins[w].at[me, self._half(ins[w], c), :], outs[w].at[me, self._half(ins[w], c), :],
                        ssem.at[3 * w + r], rsem.at[3 * w + r], (ox, oy, c)).start()

    def finish(self, ins, outs, sems):
        ssem, rsem, fs_sem, fr_sem = sems
        x, y, c, others = _place()
        me, sib = 2 * x + y, (x, y, 1 - c)
        passes = []
        for w in range(self.n):
            for r, (ox, oy) in enumerate(others):
                got = outs[w].at[2 * ox + oy, self._half(ins[w], c), :]
                _remote(got, got, ssem.at[3 * w + r], rsem.at[3 * w + r], (ox, oy, c)).wait_recv()
                cp = _remote(got, got, fs_sem.at[3 * w + r], fr_sem.at[3 * w + r], sib)
                cp.start()
                passes.append(cp)
        for w in range(self.n):
            for r, (ox, oy) in enumerate(others):
                got = outs[w].at[2 * ox + oy, self._half(ins[w], 1 - c), :]
                _remote(got, got, fs_sem.at[3 * w + r], fr_sem.at[3 * w + r], sib).wait_recv()
        for w in range(self.n):
            for r, (ox, oy) in enumerate(others):
                mine = ins[w].at[me, self._half(ins[w], c), :]
                _remote(mine, mine, ssem.at[3 * w + r], rsem.at[3 * w + r], (ox, oy, c)).wait_send()
        for cp in passes:
            cp.wait_send()


class _PairExchangeComm:
    aliased = False

    def __init__(self, grads):
        self.arrs = list(grads)
        self.n = len(self.arrs)
        dma = pltpu.SemaphoreType.DMA
        self.scratch = [dma((self.n,)), dma((self.n,))]
        self.out_shape = [SDS((4, g.shape[1] // 2, g.shape[2]), g.dtype) for g in self.arrs]

    def _copies(self, ins, outs, sems):
        ssem, rsem = sems
        x, y, c, _ = _place()
        cps = []
        for w in range(self.n):
            hr = ins[w].shape[1] // 2
            src = ins[w].at[:, pl.ds(pl.multiple_of((1 - c) * hr, 8), hr), :]
            cps.append(_remote(src, outs[w], ssem.at[w], rsem.at[w], (x, y, 1 - c)))
        return cps

    def start(self, ins, outs, sems):
        for cp in self._copies(ins, outs, sems):
            cp.start()

    def finish(self, ins, outs, sems):
        for cp in self._copies(ins, outs, sems):
            cp.wait()


class _ChipExchangeComm:
    aliased = False

    def __init__(self, psums):
        self.arrs = list(psums)
        self.n = len(self.arrs)
        dma = pltpu.SemaphoreType.DMA
        self.scratch = [dma((3 * self.n,)), dma((3 * self.n,))]
        self.out_shape = [SDS((3,) + p.shape[1:], p.dtype) for p in self.arrs]

    def _copies(self, ins, outs, sems):
        ssem, rsem = sems
        x, y, c, others = _place()
        return [_remote(ins[w].at[2 * ox + oy], outs[w].at[r], ssem.at[3 * w + r], rsem.at[3 * w + r], (ox, oy, c))
                for w in range(self.n) for r, (ox, oy) in enumerate(others)]

    def start(self, ins, outs, sems):
        for cp in self._copies(ins, outs, sems):
            cp.start()

    def finish(self, ins, outs, sems):
        for cp in self._copies(ins, outs, sems):
            cp.wait()


def _comm_only(name, comm):
    n = comm.n

    def body(*refs):
        ins, outs, sems = refs[:n], refs[n:2 * n], refs[2 * n:]
        comm.start(ins, outs, sems)
        comm.finish(ins, outs, sems)

    return pl.pallas_call(
        body, name=name, in_specs=[ANY] * n, out_specs=[ANY] * n, out_shape=comm.out_shape,
        input_output_aliases={w: w for w in range(n)} if comm.aliased else {},
        scratch_shapes=comm.scratch)(*comm.arrs)


def _call(name, body, grid, in_specs, out_specs, out_shape, scratch, dims, args, comm=None):
    if comm is None:
        return pl.pallas_call(body, name=name, grid=grid, in_specs=in_specs, out_specs=out_specs, out_shape=out_shape,
                              scratch_shapes=scratch, compiler_params=_cp(dims))(*args)
    ni, no, ns, n = len(in_specs), len(out_shape), len(scratch), comm.n

    def hosted(*refs):
        ins, cin = refs[:ni], refs[ni:ni + n]
        outs, cout = refs[ni + n:ni + n + no], refs[ni + n + no:ni + 2 * n + no]
        scr, sems = refs[ni + 2 * n + no:ni + 2 * n + no + ns], refs[ni + 2 * n + no + ns:]
        ids = [pl.program_id(d) for d in range(len(grid))]
        first = functools.reduce(jnp.logical_and, [i == 0 for i in ids])
        last = functools.reduce(jnp.logical_and, [i == g - 1 for i, g in zip(ids, grid)])

        @pl.when(first)
        def _():
            comm.start(cin, cout, sems)

        body(*ins, *outs, *scr)

        @pl.when(last)
        def _():
            comm.finish(cin, cout, sems)

    return pl.pallas_call(
        hosted, name=name, grid=grid, in_specs=list(in_specs) + [ANY] * n, out_specs=list(out_specs) + [ANY] * n,
        out_shape=list(out_shape) + comm.out_shape,
        input_output_aliases={ni + w: no + w for w in range(n)} if comm.aliased else {},
        scratch_shapes=list(scratch) + comm.scratch, compiler_params=_cp(("arbitrary",) * len(grid)))(*args, *comm.arrs)


def _mm(a, b, *, mode, out_dtype, tm, tn, tk, name, a_fn=None, epi=None, extras=(), comm=None, shard_cols=None):
    if mode == "nn":
        (M, K), (K2, N) = a.shape, b.shape
    elif mode == "nt":
        (M, K), (N, K2) = a.shape, b.shape
    else:
        (K, M), (K2, N) = a.shape, b.shape
    assert K == K2, (a.shape, b.shape, mode)
    tm, tn, tk = min(tm, M), min(tn, N), min(tk, K)
    assert M % tm == 0 and N % tn == 0 and K % tk == 0, (M, N, K, tm, tn, tk)
    nk = K // tk
    if mode == "tn":
        a_spec = pl.BlockSpec((tk, tm), lambda i, j, k: (k, i))
        ca = 0
    else:
        a_spec = pl.BlockSpec((tm, tk), lambda i, j, k: (i, k))
        ca = 1
    if mode == "nt":
        b_spec = pl.BlockSpec((tn, tk), lambda i, j, k: (j, k))
        cb = 1
    else:
        b_spec = pl.BlockSpec((tk, tn), lambda i, j, k: (k, j))
        cb = 0
    dims = (((ca,), (cb,)), ((), ()))
    ne = len(extras)

    def body(a_ref, b_ref, *rest):
        ex = rest[:ne]
        o_ref = rest[ne]
        av = a_ref[...]
        if a_fn is not None:
            av = a_fn(av.astype(F32))
        p = lax.dot_general(av.astype(BF16), b_ref[...].astype(BF16), dims, preferred_element_type=F32)

        def fin(v):
            if epi is not None:
                v = epi(v, *[e[...] for e in ex])
            o_ref[...] = v.astype(out_dtype).reshape(o_ref.shape)

        if nk == 1:
            fin(p)
        else:
            acc = rest[ne + 1]
            k = pl.program_id(2)

            @pl.when(k == 0)
            def _():
                acc[...] = p

            @pl.when(k > 0)
            def _():
                acc[...] += p

            @pl.when(k == nk - 1)
            def _():
                fin(acc[...])

    if shard_cols is None:
        o_spec, o_shape = pl.BlockSpec((tm, tn), lambda i, j, k: (i, j)), SDS((M, N), out_dtype)
    else:
        per = shard_cols // tn
        assert shard_cols % tn == 0 and N % shard_cols == 0
        o_spec = pl.BlockSpec((1, tm, tn), lambda i, j, k: (lax.div(j, per), i, lax.rem(j, per)))
        o_shape = SDS((N // shard_cols, M, shard_cols), out_dtype)
    res = _call(name, body, (M // tm, N // tn, nk),
                [a_spec, b_spec] + [pl.BlockSpec((tm, tn), lambda i, j, k: (i, j)) for _ in extras],
                [o_spec], [o_shape],
                [pltpu.VMEM((tm, tn), F32)] if nk > 1 else [], ("parallel", "parallel", "arbitrary"),
                (a, b, *extras), comm)
    return res[0] if comm is None else (res[0], res[1:])


def _rowwise(fn, rows, bcasts, outs, accs, *, tr, name):
    T = rows[0][0].shape[0]
    tr = min(tr, T)
    assert T % tr == 0
    nr, nb, no, na = len(rows), len(bcasts), len(outs), len(accs)
    in_specs = [pl.BlockSpec((tr, w), functools.partial(lambda i, c: (i, c), c=cb)) for (_, w, cb) in rows]
    in_specs += [pl.BlockSpec(b.shape, lambda i: (0, 0)) for b in bcasts]
    out_shape = [SDS((T, w), dt) for (w, dt) in outs] + [SDS(s, F32) for s in accs]
    out_specs = [pl.BlockSpec((tr, w), lambda i: (i, 0)) for (w, _) in outs]
    out_specs += [pl.BlockSpec(s, lambda i: (0, 0)) for s in accs]

    def body(*refs):
        ins = [r[...].astype(F32) for r in refs[:nr + nb]]
        o_refs = refs[nr + nb:nr + nb + no]
        a_refs = refs[nr + nb + no:]
        ro, ao = fn(*ins)
        for r, v in zip(o_refs, ro):
            r[...] = v.astype(r.dtype)
        if na:
            @pl.when(pl.program_id(0) == 0)
            def _():
                for r in a_refs:
                    r[...] = jnp.zeros(r.shape, F32)

            for r, v in zip(a_refs, ao):
                r[...] += v

    res = pl.pallas_call(
        body,
        name=name,
        grid=(T // tr,),
        in_specs=in_specs,
        out_specs=out_specs,
        out_shape=out_shape,
        compiler_params=_cp(("arbitrary",) if na else ("parallel",)),
    )(*[r[0] for r in rows], *bcasts)
    return res


def _rms(v):
    r = lax.rsqrt(jnp.mean(v * v, axis=-1, keepdims=True) + EPS)
    return v * r, r


def _rms_bwd(dy, xn, r, g):
    dxn = dy * g
    dv = r * (dxn - xn * jnp.mean(dxn * xn, axis=-1, keepdims=True))
    return dv, jnp.sum(dy * xn, axis=0, keepdims=True)


def _sig(v):
    return 1.0 / (1.0 + jnp.exp(-v))


_GELU_C = math.sqrt(2.0 / math.pi)


def _gelu(v):
    return 0.5 * v * (1.0 + jnp.tanh(_GELU_C * (v + 0.044715 * v * v * v)))


def _gelu_grad(v):
    t = jnp.tanh(_GELU_C * (v + 0.044715 * v * v * v))
    return 0.5 * (1.0 + t) + 0.5 * v * (1.0 - t * t) * _GELU_C * (1.0 + 3.0 * 0.044715 * v * v)


def _rope(v, c, s, sign):
    w = v.shape[1]
    m = lax.broadcasted_iota(jnp.int32, v.shape, 1) % HEAD_DIM
    p = jnp.where(m < ROT_DIM // 2, -pltpu.roll(v, w - ROT_DIM // 2, 1), pltpu.roll(v, ROT_DIM // 2, 1))
    return v * c + sign * (p * s)


def _rope_tables(T):
    half = ROT_DIM // 2
    inv = ROPE_THETA ** (-jnp.arange(half, dtype=F32) * 2.0 / ROT_DIM)
    ang = jnp.arange(T).astype(F32)[:, None] * inv[None, :]
    cos, sin = jnp.cos(ang), jnp.sin(ang)
    one = jnp.ones((T, HEAD_DIM - ROT_DIM), F32)
    c64 = jnp.concatenate([cos, cos, one], axis=1)
    s64 = jnp.concatenate([sin, sin, 0.0 * one], axis=1)
    return jnp.tile(c64, (1, 2)), jnp.tile(s64, (1, 2))


def _dup_half(m, lo):
    lane = lax.broadcasted_iota(jnp.int32, m.shape, 1)
    sw = pltpu.roll(m, HEAD_DIM, 1)
    return jnp.where(lane < HEAD_DIM, m, sw) if lo else jnp.where(lane >= HEAD_DIM, m, sw)


def _attn_mask(i):
    qi = lax.broadcasted_iota(jnp.int32, (ATT_BLOCK, 2 * ATT_BLOCK), 0)
    kj = lax.broadcasted_iota(jnp.int32, (ATT_BLOCK, 2 * ATT_BLOCK), 1)
    rel = qi + ATT_BLOCK - kj
    return (rel >= 0) & (rel < ATT_BLOCK) & ((kj >= ATT_BLOCK) | (i > 0))


_NT = (((1,), (1,)), ((), ()))
_TN = (((0,), (0,)), ((), ()))


def _stack_heads(m):
    lane = lax.broadcasted_iota(jnp.int32, m.shape, 1)
    zero = jnp.zeros_like(m)
    return jnp.concatenate([jnp.where(lane < HEAD_DIM, m, zero), jnp.where(lane >= HEAD_DIM, m, zero)], axis=0)


def _pair_probs(q2, k2, ok2, sink_lo, sink_hi):
    qs = _stack_heads(q2)
    s = lax.dot_general(qs, k2, _NT, preferred_element_type=F32)
    s = jnp.where(ok2, s, NEG)
    row = lax.broadcasted_iota(jnp.int32, (2 * ATT_BLOCK, 1), 0)
    sink = jnp.where(row < ATT_BLOCK, sink_lo, sink_hi)
    m = jnp.maximum(jnp.max(s, axis=1, keepdims=True), sink)
    e = jnp.exp(s - m)
    es = jnp.exp(sink - m)
    inv = 1.0 / (jnp.sum(e, axis=1, keepdims=True) + es)
    return e * inv, es * inv, qs


def _attn_fwd(za, cos, sin, sinks, comm=None):
    T = za.shape[0]
    nb = T // ATT_BLOCK
    kvb = Q_W // (2 * KV_W)

    def body(sink_ref, q_ref, kvp_ref, kvc_ref, cc_ref, sc_ref, cp_ref, sp_ref, o_ref):
        i = pl.program_id(0)
        cc, sc, cp, sp = cc_ref[...], sc_ref[...], cp_ref[...], sp_ref[...]
        q = (_rope(q_ref[...], jnp.tile(cc, (1, 8)), jnp.tile(sc, (1, 8)), 1.0) * 0.125).astype(BF16)
        kvp, kvc = kvp_ref[...], kvc_ref[...]
        k = jnp.concatenate([_rope(kvp[:, :KV_W], cp, sp, 1.0), _rope(kvc[:, :KV_W], cc, sc, 1.0)], axis=0).astype(BF16)
        v = jnp.concatenate([kvp[:, KV_W:], kvc[:, KV_W:]], axis=0).astype(BF16)
        ok = _attn_mask(i)
        ok2 = jnp.concatenate([ok, ok], axis=0)
        lane = lax.broadcasted_iota(jnp.int32, (ATT_BLOCK, LANES), 1)
        for kvh in range(2):
            k2 = _dup_half(k, kvh == 0)
            v2 = _dup_half(v, kvh == 0)
            for pair in range(4):
                c0 = (kvh * 4 + pair) * LANES
                q2 = q[:, c0:c0 + LANES]
                p, _, _ = _pair_probs(q2, k2, ok2, sink_ref[0, 2 * (kvh * 4 + pair)], sink_ref[0, 2 * (kvh * 4 + pair) + 1])
                o = jnp.dot(p.astype(BF16), v2, preferred_element_type=F32)
                o_ref[:, c0:c0 + LANES] = jnp.where(lane < HEAD_DIM, o[:ATT_BLOCK], o[ATT_BLOCK:]).astype(BF16)

    blk = lambda w, f: pl.BlockSpec((ATT_BLOCK, w), f)
    res = _call(
        "attn_fwd", body, (nb,),
        [
            pl.BlockSpec(memory_space=pltpu.SMEM),
            blk(Q_W, lambda i: (i, 0)),
            blk(2 * KV_W, lambda i: (jnp.maximum(i - 1, 0), kvb)),
            blk(2 * KV_W, lambda i: (i, kvb)),
            blk(LANES, lambda i: (i, 0)),
            blk(LANES, lambda i: (i, 0)),
            blk(LANES, lambda i: (jnp.maximum(i - 1, 0), 0)),
            blk(LANES, lambda i: (jnp.maximum(i - 1, 0), 0)),
        ],
        [blk(Q_W, lambda i: (i, 0))], [SDS((T, Q_W), BF16)], [], ("parallel",),
        (sinks, za, za, za, cos, sin, cos, sin), comm)
    return res[0] if comm is None else (res[0], res[1:])


def _attn_bwd(za, cos, sin, sinks, o, do):
    T = za.shape[0]
    nb = T // ATT_BLOCK
    kvb = Q_W // (2 * KV_W)

    def body(sink_ref, q_ref, kvp_ref, kvc_ref, cc_ref, sc_ref, cp_ref, sp_ref, o_ref, do_ref,
             dq_ref, dkv_ref, dsk_ref, carry, dqs):
        i = pl.program_id(0)

        @pl.when(i == 0)
        def _():
            carry[...] = jnp.zeros(carry.shape, F32)
            dsk_ref[...] = jnp.zeros(dsk_ref.shape, F32)

        @pl.when(i < nb)
        def _():
            cc, sc, cp, sp = cc_ref[...], sc_ref[...], cp_ref[...], sp_ref[...]
            ccq, scq = jnp.tile(cc, (1, 8)), jnp.tile(sc, (1, 8))
            q = (_rope(q_ref[...], ccq, scq, 1.0) * 0.125).astype(BF16)
            kvp, kvc = kvp_ref[...], kvc_ref[...]
            k = jnp.concatenate([_rope(kvp[:, :KV_W], cp, sp, 1.0), _rope(kvc[:, :KV_W], cc, sc, 1.0)], axis=0).astype(BF16)
            v = jnp.concatenate([kvp[:, KV_W:], kvc[:, KV_W:]], axis=0).astype(BF16)
            ok = _attn_mask(i)
            ok2 = jnp.concatenate([ok, ok], axis=0)
            lane = lax.broadcasted_iota(jnp.int32, (ATT_BLOCK, LANES), 1)
            lane_s = lax.broadcasted_iota(jnp.int32, (1, LANES), 1)
            dsk = jnp.zeros((1, LANES), F32)
            dkt_h, dvt_h = [], []
            for kvh in range(2):
                k2 = _dup_half(k, kvh == 0)
                v2 = _dup_half(v, kvh == 0)
                dkt = jnp.zeros((LANES, 2 * ATT_BLOCK), F32)
                dvt = jnp.zeros((LANES, 2 * ATT_BLOCK), F32)
                for pair in range(4):
                    h = 2 * (kvh * 4 + pair)
                    c0 = (kvh * 4 + pair) * LANES
                    do2 = do_ref[:, c0:c0 + LANES]
                    prod = do2.astype(F32) * o_ref[:, c0:c0 + LANES].astype(F32)
                    d_lo = jnp.sum(jnp.where(lane < HEAD_DIM, prod, 0.0), axis=1, keepdims=True)
                    d_hi = jnp.sum(jnp.where(lane >= HEAD_DIM, prod, 0.0), axis=1, keepdims=True)
                    delta = jnp.concatenate([d_lo, d_hi], axis=0)
                    p, p_sink, qs = _pair_probs(q[:, c0:c0 + LANES], k2, ok2, sink_ref[0, h], sink_ref[0, h + 1])
                    dos = _stack_heads(do2)
                    t = p_sink * delta
                    dsk = dsk - jnp.where(lane_s == h, jnp.sum(t[:ATT_BLOCK]), 0.0) \
                              - jnp.where(lane_s == h + 1, jnp.sum(t[ATT_BLOCK:]), 0.0)
                    dp = lax.dot_general(dos, v2, _NT, preferred_element_type=F32)
                    ds = (p * (dp - delta)).astype(BF16)
                    dqp = jnp.dot(ds, k2, preferred_element_type=F32)
                    dqs[:, c0:c0 + LANES] = jnp.where(lane < HEAD_DIM, dqp[:ATT_BLOCK], dqp[ATT_BLOCK:]) * 0.125
                    dkt = dkt + lax.dot_general(qs, ds, _TN, preferred_element_type=F32)
                    dvt = dvt + lax.dot_general(dos, p.astype(BF16), _TN, preferred_element_type=F32)
                dkt_h.append(dkt[:HEAD_DIM] + dkt[HEAD_DIM:])
                dvt_h.append(dvt[:HEAD_DIM] + dvt[HEAD_DIM:])
            dk = jnp.concatenate(dkt_h, axis=0).T
            dv = jnp.concatenate(dvt_h, axis=0).T
            dq_ref[...] = _rope(dqs[...], ccq, scq, -1.0).astype(dq_ref.dtype)
            dkp = _rope(dk[:ATT_BLOCK], cp, sp, -1.0)
            dkc = _rope(dk[ATT_BLOCK:], cc, sc, -1.0)
            dkv_ref[...] = (carry[...] + jnp.concatenate([dkp, dv[:ATT_BLOCK]], axis=1)).astype(dkv_ref.dtype)
            carry[...] = jnp.concatenate([dkc, dv[ATT_BLOCK:]], axis=1)
            dsk_ref[...] += dsk

        @pl.when(i == nb)
        def _():
            dkv_ref[...] = carry[...].astype(dkv_ref.dtype)

    blk = lambda w, f: pl.BlockSpec((ATT_BLOCK, w), f)
    cur = lambda i: jnp.minimum(i, nb - 1)
    prv = lambda i: jnp.maximum(jnp.minimum(i, nb - 1) - 1, 0)
    return pl.pallas_call(
        body,
        name="attn_bwd",
        grid=(nb + 1,),
        in_specs=[
            pl.BlockSpec(memory_space=pltpu.SMEM),
            blk(Q_W, lambda i: (cur(i), 0)),
            blk(2 * KV_W, lambda i: (prv(i), kvb)),
            blk(2 * KV_W, lambda i: (cur(i), kvb)),
            blk(LANES, lambda i: (cur(i), 0)),
            blk(LANES, lambda i: (cur(i), 0)),
            blk(LANES, lambda i: (prv(i), 0)),
            blk(LANES, lambda i: (prv(i), 0)),
            blk(Q_W, lambda i: (cur(i), 0)),
            blk(Q_W, lambda i: (cur(i), 0)),
        ],
        out_specs=[
            blk(Q_W, lambda i: (cur(i), 0)),
            blk(2 * KV_W, lambda i: (jnp.maximum(i - 1, 0), 0)),
            pl.BlockSpec((1, LANES), lambda i: (0, 0)),
        ],
        out_shape=[SDS((T, Q_W), BF16), SDS((T, 2 * KV_W), BF16), SDS((1, LANES), F32)],
        scratch_shapes=[pltpu.VMEM((ATT_BLOCK, 2 * KV_W), F32), pltpu.VMEM((ATT_BLOCK, Q_W), F32)],
        compiler_params=_cp(("arbitrary",)),
    )(sinks, za, za, za, cos, sin, cos, sin, o, do)


def _s5_discretize(lam_re, lam_im, log_dt, b_re, b_im):
    dt = jnp.exp(log_dt)[:, None]
    mag = jnp.exp(lam_re * dt)
    a_re, a_im = mag * jnp.cos(lam_im * dt), mag * jnp.sin(lam_im * dt)
    den = lam_re * lam_re + lam_im * lam_im
    nr, ni = a_re - 1.0, a_im
    coef_re = (nr * lam_re + ni * lam_im) / den
    coef_im = (ni * lam_re - nr * lam_im) / den
    bb_re = coef_re[..., None] * b_re - coef_im[..., None] * b_im
    bb_im = coef_re[..., None] * b_im + coef_im[..., None] * b_re
    return a_re, a_im, bb_re, bb_im


def _blockdiag_in(bb):
    x = bb.reshape(N_JB, 8, SSM_P, SSM_GC).transpose(0, 1, 3, 2)
    return (x[:, :, :, None, :] * jnp.eye(8, dtype=bb.dtype)[None, :, None, :, None]).reshape(N_JB, 128, 512)


def _blockdiag_in_extract(m):
    x = m.reshape(N_JB, 8, SSM_GC, 8, SSM_P)
    x = jnp.einsum('jgchp,gh->jgcp', x, jnp.eye(8, dtype=m.dtype))
    return x.transpose(0, 1, 3, 2).reshape(SSM_G, SSM_P, SSM_GC)


def _blockdiag_out(c):
    x = c.reshape(N_JB, 8, SSM_GC, SSM_P).transpose(0, 1, 3, 2)
    return (x[:, :, :, None, :] * jnp.eye(8, dtype=c.dtype)[None, :, None, :, None]).reshape(N_JB, 512, 128)


def _blockdiag_out_extract(m):
    x = m.reshape(N_JB, 8, SSM_P, 8, SSM_GC)
    x = jnp.einsum('jgphc,gh->jgpc', x, jnp.eye(8, dtype=m.dtype))
    return x.transpose(0, 1, 3, 2).reshape(SSM_G, SSM_GC, SSM_P)


def _s5_tables(a_re, a_im):
    ar, ai = a_re.reshape(N_LG, 1, LANES), a_im.reshape(N_LG, 1, LANES)
    pr, pi, n = ar, ai, 1
    while n < S5_SEG:
        pr, pi, n = pr * pr - pi * pi, 2.0 * pr * pi, 2 * n
    assert n == S5_SEG
    bc = lambda v: jnp.broadcast_to(v, (N_LG, SUBLANES, LANES))
    return bc(ar), bc(ai), pr, pi


def _s5_to_time_major(src_ref, dst_ref):
    for t in range(S5_SEG):
        dst_ref[t * SUBLANES:(t + 1) * SUBLANES, :] = src_ref[pl.ds(t, SUBLANES, stride=S5_SEG), :]


def _s5_from_time_major(val, dst_ref):
    for t in range(S5_SEG):
        dst_ref[pl.ds(t, SUBLANES, stride=S5_SEG), :] = val[t * SUBLANES:(t + 1) * SUBLANES, :]


def _tm_rows(t, row0=0):
    return pl.ds(pl.multiple_of(t * SUBLANES + row0, SUBLANES), SUBLANES)


def _s5_scan(src_re, src_im, ar, ai, reverse, start=None, dst=None, dst_row0=0):
    def step(n, carry):
        t = (S5_SEG - 1 - n) if reverse else n
        out = []
        for ll in range(LG_PER_JB):
            xr, xi = carry[2 * ll], carry[2 * ll + 1]
            idx = (ll, _tm_rows(t), slice(None))
            nr = ar[ll] * xr - ai[ll] * xi + src_re[idx]
            ni = ar[ll] * xi + ai[ll] * xr + src_im[idx]
            if dst is not None:
                odx = (ll, _tm_rows(t, dst_row0), slice(None))
                dst[0][odx] = nr
                dst[1][odx] = ni
            out += [nr, ni]
        return tuple(out)
    if start is None:
        init = (jnp.zeros((SUBLANES, LANES), F32),) * (2 * LG_PER_JB)
    else:
        init = tuple(s[ll] for ll in range(LG_PER_JB) for s in start)
    return lax.fori_loop(0, S5_SEG, step, init)


def _s5_fixup(ends, in_re, in_im, mr, mi, s_re, s_im, reverse):
    cr, ci = in_re, in_im
    order = range(SUBLANES - 1, -1, -1) if reverse else range(SUBLANES)
    for s in order:
        s_re[:, s:s + 1, :] = cr
        s_im[:, s:s + 1, :] = ci
        er = jnp.stack([ends[2 * ll][s:s + 1, :] for ll in range(LG_PER_JB)])
        ei = jnp.stack([ends[2 * ll + 1][s:s + 1, :] for ll in range(LG_PER_JB)])
        cr, ci = mr * cr - mi * ci + er, mr * ci + mi * cr + ei
    return cr, ci


def _s5_specs(nc, rev):
    cidx = (lambda c: nc - 1 - c) if rev else (lambda c: c)
    jb = lambda shape: pl.BlockSpec(shape, lambda j, c: (j, 0, 0))
    return cidx, [
        jb((1, LANES, 8 * LANES)),
        jb((1, 8 * LANES, LANES)),
        pl.BlockSpec((1, LANES), lambda j, c: (0, j)),
        jb((LG_PER_JB, SUBLANES, LANES)), jb((LG_PER_JB, SUBLANES, LANES)),
        jb((LG_PER_JB, 1, LANES)), jb((LG_PER_JB, 1, LANES)),
    ]


def _s5_fwd(za, prm, comm=None):
    T = za.shape[0]
    R = S5_CHUNK
    nc = T // R
    ub = (Q_W + 2 * KV_W) // LANES
    _, pspecs = _s5_specs(nc, False)

    def body(u_ref, b_ref, c_ref, d_ref, are_ref, aim_ref, alr_ref, ali_ref,
             yg_ref, x0r_ref, x0i_ref, bur, bui, xsr, xsi, sr, si, xcr, xci, utm, ynat):
        c = pl.program_id(1)

        @pl.when(c == 0)
        def _():
            xcr[...] = jnp.zeros(xcr.shape, F32)
            xci[...] = jnp.zeros(xci.shape, F32)

        _s5_to_time_major(u_ref, utm)
        u = utm[...]
        ub16 = u.astype(BF16)
        bu = jnp.dot(ub16, b_ref[0].astype(BF16), preferred_element_type=F32)
        for ll in range(LG_PER_JB):
            bur[ll] = bu[:, ll * LANES:(ll + 1) * LANES]
            bui[ll] = bu[:, (LG_PER_JB + ll) * LANES:(LG_PER_JB + ll + 1) * LANES]
        ar = [are_ref[ll] for ll in range(LG_PER_JB)]
        ai = [aim_ref[ll] for ll in range(LG_PER_JB)]
        ends = _s5_scan(bur, bui, ar, ai, False)
        in_r, in_i = xcr[...], xci[...]
        x0r_ref[0] = in_r
        x0i_ref[0] = in_i
        out_r, out_i = _s5_fixup(ends, in_r, in_i, alr_ref[...], ali_ref[...], sr, si, False)
        xcr[...] = out_r
        xci[...] = out_i
        _s5_scan(bur, bui, ar, ai, False, start=(sr, si), dst=(xsr, xsi))
        xcat =jnp.concatenate([xsr[ll].astype(BF16) for ll in range(LG_PER_JB)]
                               + [xsi[ll].astype(BF16) for ll in range(LG_PER_JB)], axis=1)
        y = d_ref[...] * u + jnp.dot(xcat, c_ref[0].astype(BF16), preferred_element_type=F32)
        _s5_from_time_major(_gelu(y), ynat)
        yg_ref[...] = ynat[...].astype(BF16)

    st = pl.BlockSpec((1, LG_PER_JB, 1, LANES), lambda j, c: (c, j, 0, 0))
    vm = lambda rows: pltpu.VMEM((LG_PER_JB, rows, LANES), F32)
    res = _call(
        "s5_fwd", body, (N_JB, nc),
        [pl.BlockSpec((R, LANES), lambda j, c: (c, ub + j))] + pspecs,
        [pl.BlockSpec((R, LANES), lambda j, c: (c, j)), st, st],
        [SDS((T, SSM_W), BF16), SDS((nc, N_LG, 1, LANES), F32), SDS((nc, N_LG, 1, LANES), F32)],
        [vm(R), vm(R), vm(R), vm(R), vm(SUBLANES), vm(SUBLANES), vm(1), vm(1),
         pltpu.VMEM((R, LANES), F32), pltpu.VMEM((R, LANES), F32)],
        ("parallel", "arbitrary"), (za, *prm), comm)
    return res if comm is None else (res[:3], res[3:])


def _s5_bwd(za, dyg, x0r, x0i, prm, comm=None):
    T = za.shape[0]
    R = S5_CHUNK
    nc = T // R
    ub = (Q_W + 2 * KV_W) // LANES
    cidx, pspecs = _s5_specs(nc, True)
    PAD = SUBLANES

    def body(u_ref, dyg_ref, x0r_ref, x0i_ref, b_ref, c_ref, d_ref, are_ref, aim_ref,
             alr_ref, ali_ref,
             du_ref, dar_ref, dai_ref, db_ref, dc_ref, dd_ref,
             bur, bui, xsr, xsi, sr, si, gcr, gci, utm, dtm, dunat):
        c = pl.program_id(1)

        @pl.when(c == 0)
        def _():
            gcr[...] = jnp.zeros(gcr.shape, F32)
            gci[...] = jnp.zeros(gci.shape, F32)
            dar_ref[...] = jnp.zeros(dar_ref.shape, F32)
            dai_ref[...] = jnp.zeros(dai_ref.shape, F32)
            db_ref[...] = jnp.zeros(db_ref.shape, F32)
            dc_ref[...] = jnp.zeros(dc_ref.shape, F32)
            dd_ref[...] = jnp.zeros(dd_ref.shape, F32)

        _s5_to_time_major(u_ref, utm)
        _s5_to_time_major(dyg_ref, dtm)
        u = utm[...]
        ub16 = u.astype(BF16)
        bcat, ccat = b_ref[0].astype(BF16), c_ref[0].astype(BF16)
        lanes = lambda v, ll: v[:, ll * LANES:(ll + 1) * LANES]
        bu = jnp.dot(ub16, bcat, preferred_element_type=F32)
        for ll in range(LG_PER_JB):
            bur[ll] = lanes(bu, ll)
            bui[ll] = lanes(bu, LG_PER_JB + ll)
        ar = [are_ref[ll] for ll in range(LG_PER_JB)]
        ai = [aim_ref[ll] for ll in range(LG_PER_JB)]
        ends = _s5_scan(bur, bui, ar, ai, False)
        in_r, in_i = x0r_ref[0], x0i_ref[0]
        _s5_fixup(ends, in_r, in_i, alr_ref[...], ali_ref[...], sr, si, False)
        _s5_scan(bur, bui, ar, ai, False, start=(sr, si), dst=(xsr, xsi), dst_row0=PAD)
        xsr[:, 0:PAD, :] = sr[...]
        xsi[:, 0:PAD, :] = si[...]
        xcat = jnp.concatenate([xsr[ll, PAD:, :].astype(BF16) for ll in range(LG_PER_JB)]
                               + [xsi[ll, PAD:, :].astype(BF16) for ll in range(LG_PER_JB)], axis=1)
        y = d_ref[...] * u + jnp.dot(xcat, ccat, preferred_element_type=F32)
        dy = dtm[...] * _gelu_grad(y)
        dyb = dy.astype(BF16)
        dd_ref[...] += jnp.sum(dy * u, axis=0, keepdims=True)
        du = d_ref[...] * dy
        dc_ref[0] += lax.dot_general(dyb, xcat, _TN, preferred_element_type=F32)
        g = lax.dot_general(dyb, ccat, _NT, preferred_element_type=F32)
        for ll in range(LG_PER_JB):
            bur[ll] = lanes(g, ll)
            bui[ll] = lanes(g, LG_PER_JB + ll)
        aic = [-v for v in ai]
        ends = _s5_scan(bur, bui, ar, aic, True)
        out_r, out_i = _s5_fixup(ends, gcr[...], gci[...], alr_ref[...], -ali_ref[...], sr, si, True)
        gcr[...] = out_r
        gci[...] = out_i
        _s5_scan(bur, bui, ar, aic, True, start=(sr, si), dst=(bur, bui))
        for ll in range(LG_PER_JB):
            gr, gi = bur[ll], bui[ll]
            xpr, xpi = xsr[ll, 0:R, :], xsi[ll, 0:R, :]
            red = lambda v: v.reshape(R // SUBLANES, SUBLANES, LANES).sum(axis=0)
            dar_ref[ll] += red(xpr * gr + xpi * gi)
            dai_ref[ll] += red(xpr * gi - xpi * gr)
        gcat = jnp.concatenate([bur[ll].astype(BF16) for ll in range(LG_PER_JB)]
                               + [bui[ll].astype(BF16) for ll in range(LG_PER_JB)], axis=1)
        db_ref[0] += lax.dot_general(ub16, gcat, _TN, preferred_element_type=F32)
        du = du + lax.dot_general(gcat, bcat, _NT, preferred_element_type=F32)
        _s5_from_time_major(du, dunat)
        du_ref[...] = dunat[...].astype(du_ref.dtype)

    st = pl.BlockSpec((1, LG_PER_JB, 1, LANES), lambda j, c: (cidx(c), j, 0, 0))
    jb = lambda shape: pl.BlockSpec(shape, lambda j, c: (j, 0, 0))
    vm = lambda rows: pltpu.VMEM((LG_PER_JB, rows, LANES), F32)
    res = _call(
        "s5_bwd", body, (N_JB, nc),
        [pl.BlockSpec((R, LANES), lambda j, c: (cidx(c), ub + j)),
         pl.BlockSpec((R, LANES), lambda j, c: (cidx(c), j)), st, st] + pspecs,
        [pl.BlockSpec((R, LANES), lambda j, c: (cidx(c), j)),
         jb((LG_PER_JB, SUBLANES, LANES)), jb((LG_PER_JB, SUBLANES, LANES)),
         jb((1, LANES, 8 * LANES)), jb((1, LANES, 8 * LANES)),
         pl.BlockSpec((1, LANES), lambda j, c: (0, j))],
        [SDS((T, SSM_W), BF16), SDS((N_LG, SUBLANES, LANES), F32), SDS((N_LG, SUBLANES, LANES), F32),
         SDS((N_JB, LANES, 8 * LANES), F32), SDS((N_JB, LANES, 8 * LANES), F32), SDS((1, SSM_W), F32)],
        [vm(R), vm(R), vm(R + PAD), vm(R + PAD), vm(SUBLANES), vm(SUBLANES), vm(1), vm(1)]
        + [pltpu.VMEM((R, LANES), F32)] * 3,
        ("parallel", "arbitrary"), (za, dyg, x0r, x0i, *prm), comm)
    return res if comm is None else (res[:6], res[6:])


def _local_step(x, target, gains, w_a, w_g, sinks, s5w, comms, late, red=None):
    T = x.shape[0]
    D = D_MODEL
    g1, g2, g3, g4 = gains
    cos, sin = _rope_tables(T)
    lam_re, lam_im, log_dt, b_re, b_im, c_re, c_im, d_skip = s5w
    (a_re, a_im, bb_re, bb_im), disc_vjp = jax.vjp(_s5_discretize, lam_re, lam_im, log_dt, b_re, b_im)
    abr, abi, al_re, al_im = _s5_tables(a_re, a_im)
    prm = (jnp.concatenate([_blockdiag_in(bb_re), _blockdiag_in(bb_im)], axis=2),
           jnp.concatenate([_blockdiag_out(c_re), -_blockdiag_out(c_im)], axis=1),
           d_skip.reshape(1, SSM_W), abr, abi, al_re, al_im)
    mm = functools.partial(_mm, tm=1024, tn=1024, tk=2048)

    h = _rowwise(lambda xv, g: ((_rms(xv)[0] * g,), ()), [(x, D, 0)], [g1], [(D, BF16)], [], tr=512, name="norm1")[0]
    za = _mm(h, w_a, mode="nn", out_dtype=F32, tm=1024, tn=1152, tk=2048, name="mm_za")
    unpack = lambda res, comm: (res, ()) if comm is None else res
    zg, got0 = unpack(mm(h, w_g, mode="nn", out_dtype=BF16, name="mm_zg", comm=comms[0]), comms[0])
    o_attn, got1 = unpack(_attn_fwd(za, cos, sin, sinks, comm=comms[1]), comms[1])
    (yg, x0r, x0i), got2 = unpack(_s5_fwd(za, prm, comm=comms[2]), comms[2])
    w_glu, w_ba, w_bs, w_out, w_up, w_down = late(got0, got1, got2)
    zglu = mm(yg, w_glu, mode="nn", out_dtype=BF16, name="mm_glu")
    o_ssm = _rowwise(lambda z1, z2: ((z1 * _sig(z2),), ()), [(zglu, SSM_W, 0), (zglu, SSM_W, 1)], [],
                     [(SSM_W, BF16)], [], tr=512, name="glu")[0]
    ya = mm(o_attn, w_ba, mode="nn", out_dtype=BF16, name="mm_ya")
    ys = mm(o_ssm, w_bs, mode="nn", out_dtype=BF16, name="mm_ys")
    mi = _rowwise(lambda ga, gs, a, s: ((_sig(ga) * a + _sig(gs) * s,), ()),
                  [(zg, D, 0), (zg, D, 1), (ya, D, 0), (ys, D, 0)], [], [(D, BF16)], [], tr=256, name="gate")[0]
    mixed = mm(mi, w_out, mode="nn", out_dtype=F32, name="mm_out")

    def f_post(xv, mv, g2v, g3v):
        x1v = xv + _rms(mv)[0] * g2v
        return (x1v, _rms(x1v)[0] * g3v), ()
    x1, h2 = _rowwise(f_post, [(x, D, 0), (mixed, D, 0)], [g2, g3], [(D, F32), (D, BF16)], [], tr=256, name="post_mix")
    act = mm(h2, w_up, mode="nn", out_dtype=BF16, name="mm_up", epi=lambda v: jnp.maximum(v, 0.0))
    f = mm(act, w_down, mode="nn", out_dtype=F32, name="mm_down", a_fn=lambda v: v * v, tk=4096)

    def f_final(x1v, fv, tv, g4v):
        fn, r = _rms(fv)
        e = x1v + fn * g4v - tv
        dx2v = e * (1.0 / D)
        dfv, dg4v = _rms_bwd(dx2v, fn, r, g4v)
        return (dfv, dx2v), (dg4v, jnp.zeros((SUBLANES, LANES), F32) + 0.5 * jnp.sum(e * e) * (1.0 / D))
    df, dx2, dg4, lossb = _rowwise(f_final, [(x1, D, 0), (f, D, 0), (target, D, 0)], [g4],
                                   [(D, BF16), (D, F32)], [(1, D), (SUBLANES, LANES)], tr=256, name="final")

    big = {}

    def add(k, g4):
        big[k] = g4
        if red is not None:
            red.add(k, g4)

    def hosted(fn, stage, names):
        if red is None:
            return fn(comm=None)
        out, got = fn(comm=getattr(red, stage)(names))
        getattr(red, stage + "_done")(names, got)
        return out

    dpre = mm(df, w_down, mode="nt", out_dtype=BF16, name="mm_dact", epi=lambda v, a: v * (2.0 * a.astype(F32)), extras=(act,))
    wg = functools.partial(_mm, mode="tn", out_dtype=F32, tm=1024, tn=1024, tk=4096)
    add("w_down", wg(act, df, name="wg_down", a_fn=lambda v: v * v).reshape(4, D_FF // 4, D))
    dh2 = hosted(functools.partial(mm, dpre, w_up, mode="nt", out_dtype=F32, name="mm_dh2", tk=4096),
                 "s1", ["w_down"])
    add("w_up", hosted(functools.partial(wg, h2, dpre, name="wg_up", shard_cols=D_FF // 4), "s3", ["w_down"]))

    def f_mid(dx2v, dh2v, x1v, mv, g2v, g3v):
        x1n, r3 = _rms(x1v)
        d3, dg3v = _rms_bwd(dh2v, x1n, r3, g3v)
        dx1v = dx2v + d3
        mn, r2 = _rms(mv)
        dmv, dg2v = _rms_bwd(dx1v, mn, r2, g2v)
        return (dx1v, dmv), (dg3v, dg2v)
    dx1, dmixed, dg3, dg2 = _rowwise(f_mid, [(dx2, D, 0), (dh2, D, 0), (x1, D, 0), (mixed, D, 0)], [g2, g3],
                                     [(D, F32), (D, BF16)], [(1, D), (1, D)], tr=256, name="mid")

    dmi = hosted(functools.partial(mm, dmixed, w_out, mode="nt", out_dtype=BF16, name="mm_dmi"), "s1", ["w_up"])
    add("w_out", wg(mi, dmixed, name="wg_out").reshape(4, D // 4, D))

    def f_gate(dv, ga, gs, a, s):
        sa, ss = _sig(ga), _sig(gs)
        return (dv * sa, dv * ss, jnp.concatenate([dv * a * sa * (1.0 - sa), dv * s * ss * (1.0 - ss)], axis=1)), ()
    dya, dys, dzg = _rowwise(f_gate, [(dmi, D, 0), (zg, D, 0), (zg, D, 1), (ya, D, 0), (ys, D, 0)], [],
                             [(D, BF16), (D, BF16), (2 * D, BF16)], [], tr=256, name="gate_bwd")
    do_attn = hosted(functools.partial(mm, dya, w_ba, mode="nt", out_dtype=BF16, name="mm_doa"), "s1", ["w_out"])
    d_w_ba = wg(o_attn, dya, name="wg_ba")
    do_ssm = mm(dys, w_bs, mode="nt", out_dtype=BF16, name="mm_dos")
    d_w_bs = wg(o_ssm, dys, name="wg_bs")
    add("w_branch", jnp.concatenate([d_w_ba.reshape(2, D // 4, D), d_w_bs.reshape(2, D // 4, D)], axis=0))

    def f_glu(dv, z1, z2):
        s2 = _sig(z2)
        return (jnp.concatenate([dv * s2, dv * z1 * s2 * (1.0 - s2)], axis=1),), ()
    dzglu = _rowwise(f_glu, [(do_ssm, SSM_W, 0), (zglu, SSM_W, 0), (zglu, SSM_W, 1)], [], [(2 * SSM_W, BF16)], [],
                     tr=512, name="glu_bwd")[0]
    dyg = hosted(functools.partial(mm, dzglu, w_glu, mode="nt", out_dtype=F32, name="mm_dyg"), "s1", ["w_branch"])
    add("w_glu", wg(yg, dzglu, name="wg_glu", tn=SSM_W // 2, shard_cols=SSM_W // 2))
    du, dar, dai, dbc, dcc, ddv = hosted(functools.partial(_s5_bwd, za, dyg, x0r, x0i, prm),
                                         "s3", ["w_up", "w_out", "w_branch"])
    dbr, dbi = dbc[:, :, :4 * LANES], dbc[:, :, 4 * LANES:]
    dcc = dcc.transpose(0, 2, 1)
    dcr, dci = dcc[:, :4 * LANES, :], -dcc[:, 4 * LANES:, :]
    dq, dkv, dsk = _attn_bwd(za, cos, sin, sinks, o_attn, do_attn)
    dza = jnp.concatenate([dq, dkv, du], axis=1)
    d_w_a = _mm(h, dza, mode="tn", out_dtype=F32, tm=1024, tn=ZA_W // 2, tk=2048, name="wg_a")
    d_w_g = wg(h, dzg, name="wg_g")
    cb = (ZA_W + d_w_g.shape[1]) // 4
    add("w_in", jnp.stack([d_w_a[:, :cb], jnp.concatenate([d_w_a[:, cb:], d_w_g[:, :2 * cb - ZA_W]], axis=1),
                           d_w_g[:, 2 * cb - ZA_W:3 * cb - ZA_W], d_w_g[:, 3 * cb - ZA_W:]]))
    dh = hosted(functools.partial(mm, dza, w_a, mode="nt", out_dtype=F32, name="mm_dh_a", tk=ZA_W), "s1", ["w_in", "w_glu"])
    dh = hosted(functools.partial(mm, dzg, w_g, mode="nt", out_dtype=F32, name="mm_dh_g",
                                  epi=lambda v, p: v + p, extras=(dh,)), "s3", ["w_in", "w_glu"])

    def f_first(dx1v, dhv, xv, g1v):
        xn, r1 = _rms(xv)
        d1, dg1v = _rms_bwd(dhv, xn, r1, g1v)
        return (dx1v + d1,), (dg1v,)
    dx, dg1 = _rowwise(f_first, [(dx1, D, 0), (dh, D, 0), (x, D, 0)], [g1], [(D, F32)], [(1, D)], tr=256, name="first")

    da_re = dar.sum(axis=1).reshape(SSM_G, SSM_P)
    da_im = dai.sum(axis=1).reshape(SSM_G, SSM_P)
    d_lam_re, d_lam_im, d_log_dt, d_b_re, d_b_im = disc_vjp(
        (da_re, da_im, _blockdiag_in_extract(dbr), _blockdiag_in_extract(dbi)))
    small = dict(norm_mix_pre=dg1, norm_mix_post=dg2, norm_mlp_pre=dg3, norm_mlp_post=dg4,
                 sinks=dsk[:, :N_Q_HEADS], lam_re=d_lam_re, lam_im=d_lam_im, log_dt=d_log_dt,
                 b_re=d_b_re, b_im=d_b_im, c_re=_blockdiag_out_extract(dcr), c_im=_blockdiag_out_extract(dci),
                 d_skip=ddv.reshape(SSM_G, SSM_GC))
    return lossb[0, 0], dx, small, big


def _cast_into_slot(w, k_arr):
    rows, cols = w.shape
    tr = 256

    def body(k_ref, w_ref, o_ref):
        o_ref[0] = w_ref[...].astype(BF16)

    return pl.pallas_call(
        body,
        name="cast_into_slot",
        grid_spec=pltpu.PrefetchScalarGridSpec(
            num_scalar_prefetch=1,
            grid=(rows // tr,),
            in_specs=[pl.BlockSpec((tr, cols), lambda i, k: (i, 0))],
            out_specs=pl.BlockSpec((1, tr, cols), lambda i, k: (k[0], i, 0)),
        ),
        out_shape=SDS((4, rows, cols), BF16),
        compiler_params=_cp(("parallel",)),
    )(k_arr, w)


def _pair_sum(g, r, c_arr):
    _, _, hr, cols = g.shape
    tr = min(256, hr)

    def body(c_ref, g_ref, r_ref, o_ref):
        o_ref[0] = (g_ref[0, 0] + r_ref[0]).astype(BF16)

    return pl.pallas_call(
        body,
        name="pair_sum",
        grid_spec=pltpu.PrefetchScalarGridSpec(
            num_scalar_prefetch=1,
            grid=(4, hr // tr),
            in_specs=[pl.BlockSpec((1, 1, tr, cols), lambda k, i, c_ref: (k, c_ref[0], i, 0)),
                      pl.BlockSpec((1, tr, cols), lambda k, i, c_ref: (k, i, 0))],
            out_specs=pl.BlockSpec((1, tr, cols), lambda k, i, c_ref: (k, i, 0)),
        ),
        out_shape=SDS((4, hr, cols), BF16),
        compiler_params=_cp(("parallel", "parallel")),
    )(c_arr, g, r)


def _chip_sum(g, r, q, kc_arr):
    _, _, hr, cols = g.shape
    tr = min(256, hr)

    def body(kc_ref, g_ref, r_ref, q_ref, o_ref):
        s = g_ref[0, 0] + r_ref[0]
        for j in range(3):
            s = s + q_ref[j].astype(F32)
        o_ref[...] = s

    return pl.pallas_call(
        body,
        name="chip_sum",
        grid_spec=pltpu.PrefetchScalarGridSpec(
            num_scalar_prefetch=1,
            grid=(hr // tr,),
            in_specs=[pl.BlockSpec((1, 1, tr, cols), lambda i, kc: (kc[0], kc[1], i, 0)),
                      pl.BlockSpec((1, tr, cols), lambda i, kc: (kc[0], i, 0)),
                      pl.BlockSpec((3, tr, cols), lambda i, kc: (0, i, 0))],
            out_specs=pl.BlockSpec((tr, cols), lambda i, kc: (kc[1] * (hr // tr) + i, 0)),
        ),
        out_shape=SDS((2 * hr, cols), F32),
        compiler_params=_cp(("parallel",)),
    )(kc_arr, g, r, q)


def _pair_share(blocks):
    n = len(blocks)

    def body(*refs):
        ins, outs = refs[:n], refs[n:2 * n]
        ssem, rsem = refs[2 * n:]
        x, y, c, _ = _place()
        cps = []
        for w in range(n):
            hr = ins[w].shape[0] // 2
            rows = pl.ds(pl.multiple_of(c * hr, 8), hr)
            cp = _remote(ins[w].at[rows, :], outs[w].at[rows, :], ssem.at[w], rsem.at[w], (x, y, 1 - c))
            cp.start()
            cps.append(cp)
        for w in range(n):
            hr = ins[w].shape[0] // 2
            other = outs[w].at[pl.ds(pl.multiple_of((1 - c) * hr, 8), hr), :]
            _remote(other, other, ssem.at[w], rsem.at[w], (x, y, 1 - c)).wait_recv()
        for cp in cps:
            cp.wait_send()

    dma = pltpu.SemaphoreType.DMA
    return pl.pallas_call(
        body,
        name="pair_share",
        in_specs=[ANY] * n,
        out_specs=[ANY] * n,
        out_shape=[SDS(b.shape, b.dtype) for b in blocks],
        input_output_aliases={w: w for w in range(n)},
        scratch_shapes=[dma((n,)), dma((n,))],
    )(*blocks)


class _GradReducer:
    def __init__(self, c_arr, kc_arr):
        self.c_arr, self.kc_arr = c_arr, kc_arr
        self.g, self.r, self.ps, self.q = {}, {}, {}, {}

    def add(self, k, g4):
        self.g[k] = g4.reshape(4, 2, g4.shape[1] // 2, g4.shape[2])

    def s1(self, names):
        return _PairExchangeComm([self.g[k].reshape(4, -1, self.g[k].shape[3]) for k in names])

    def s1_done(self, names, got):
        for k, r in zip(names, got):
            self.r[k] = r
            self.ps[k] = _pair_sum(self.g[k], r, self.c_arr)

    def s3(self, names):
        return _ChipExchangeComm([self.ps[k] for k in names])

    def s3_done(self, names, got):
        self.q.update(zip(names, got))

    def finish(self, order):
        rest = [k for k in order if k not in self.r]
        if rest:
            self.s1_done(rest, _comm_only("pair_exchange", self.s1(rest)))
        rest = [k for k in order if k not in self.q]
        if rest:
            self.s3_done(rest, _comm_only("chip_exchange", self.s3(rest)))
        blocks = [_chip_sum(self.g[k], self.r[k], self.q[k], self.kc_arr) for k in order]
        return dict(zip(order, _pair_share(blocks)))


def _all_reduce_small(buf):
    rows = buf.shape[0]
    hr = rows // 2
    assert hr % SUBLANES == 0

    def body(in_ref, o_ref, sib, pair, slots, ssem, rsem):
        x, y, c, others = _place()
        me, sibling = 2 * x + y, (x, y, 1 - c)
        mine = pl.ds(pl.multiple_of(c * hr, SUBLANES), hr)
        theirs = pl.ds(pl.multiple_of((1 - c) * hr, SUBLANES), hr)
        first = _remote(in_ref, sib, ssem.at[0], rsem.at[0], sibling)
        first.start()
        first.wait()
        pair[...] = in_ref[...] + sib[...]
        slots[me] = pair[mine, :]
        cps = [_remote(pair.at[mine, :], slots.at[me], ssem.at[1 + r], rsem.at[1 + r], (ox, oy, c))
               for r, (ox, oy) in enumerate(others)]
        for cp in cps:
            cp.start()
        for r, (ox, oy) in enumerate(others):
            _remote(pair.at[mine, :], slots.at[2 * ox + oy], ssem.at[1 + r], rsem.at[1 + r], (ox, oy, c)).wait_recv()
        o_ref[mine, :] = (slots[0] + slots[1]) + (slots[2] + slots[3])
        last = _remote(o_ref.at[mine, :], o_ref.at[mine, :], ssem.at[4], rsem.at[4], sibling)
        last.start()
        _remote(o_ref.at[theirs, :], o_ref.at[theirs, :], ssem.at[4], rsem.at[4], sibling).wait_recv()
        last.wait_send()
        for cp in cps:
            cp.wait_send()

    dma = pltpu.SemaphoreType.DMA
    return pl.pallas_call(
        body,
        name="all_reduce_small",
        in_specs=[pl.BlockSpec(memory_space=pltpu.VMEM)],
        out_specs=pl.BlockSpec(memory_space=pltpu.VMEM),
        out_shape=SDS(buf.shape, F32),
        scratch_shapes=[pltpu.VMEM((rows, LANES), F32), pltpu.VMEM((rows, LANES), F32),
                        pltpu.VMEM((4, hr, LANES), F32), dma((5,)), dma((5,))],
        compiler_params=pltpu.CompilerParams(vmem_limit_bytes=VMEM_LIMIT),
    )(buf)


def _adam_fn(w, g, m, v):
    m2 = ADAM_B1 * m + (1.0 - ADAM_B1) * g
    v2 = ADAM_B2 * v + (1.0 - ADAM_B2) * (g * g)
    m_hat = m2 / (1.0 - ADAM_B1 ** ADAM_STEP)
    v_hat = v2 / (1.0 - ADAM_B2 ** ADAM_STEP)
    return (-ADAM_LR * (m_hat / (jnp.sqrt(v_hat) + ADAM_EPS) + ADAM_WD * w), m2, v2), ()


def _adamw(w, g, m, v, name, tr=256):
    cols = w.shape[1]
    return _rowwise(_adam_fn, [(w, cols, 0), (g, cols, 0), (m, cols, 0), (v, cols, 0)], [],
                    [(cols, F32)] * 3, [], tr=tr, name=name)


BIG = ("w_in", "w_glu", "w_branch", "w_out", "w_up", "w_down")
COL_SHARDED = ("w_in", "w_glu", "w_up")
SMALL = ("norm_mix_pre", "norm_mix_post", "norm_mlp_pre", "norm_mlp_post", "sinks", "lam_re", "lam_im", "log_dt",
         "b_re", "b_im", "c_re", "c_im", "d_skip")
WEIGHTS = ("norm_mix_pre", "norm_mix_post", "norm_mlp_pre", "norm_mlp_post", "w_in", "sinks", "lam_re", "lam_im",
           "log_dt", "b_re", "b_im", "c_re", "c_im", "d_skip", "w_glu", "w_branch", "w_out", "w_up", "w_down")


def _flat_small(vals, extra):
    flat = jnp.concatenate([vals[k].reshape(-1) for k in SMALL] + [extra.reshape(-1)])
    rows = -(-flat.shape[0] // (SUBLANES * LANES)) * SUBLANES
    return jnp.pad(flat, (0, rows * LANES - flat.shape[0])).reshape(rows, LANES)


def kernel(x, norm_mix_pre, norm_mix_post, norm_mlp_pre, norm_mlp_post, w_in, sinks, lam_re, lam_im, log_dt, b_re, b_im, c_re, c_im, d_skip, w_glu, w_branch, w_out, w_up, w_down, loss_target, m_norm_mix_pre, m_norm_mix_post, m_norm_mlp_pre, m_norm_mlp_post, m_w_in, m_sinks, m_lam_re, m_lam_im, m_log_dt, m_b_re, m_b_im, m_c_re, m_c_im, m_d_skip, m_w_glu, m_w_branch, m_w_out, m_w_up, m_w_down, v_norm_mix_pre, v_norm_mix_post, v_norm_mlp_pre, v_norm_mlp_post, v_w_in, v_sinks, v_lam_re, v_lam_im, v_log_dt, v_b_re, v_b_im, v_c_re, v_c_im, v_d_skip, v_w_glu, v_w_branch, v_w_out, v_w_up, v_w_down):
    w = dict(norm_mix_pre=norm_mix_pre, norm_mix_post=norm_mix_post, norm_mlp_pre=norm_mlp_pre, norm_mlp_post=norm_mlp_post,
             w_in=w_in, sinks=sinks, lam_re=lam_re, lam_im=lam_im, log_dt=log_dt, b_re=b_re, b_im=b_im, c_re=c_re,
             c_im=c_im, d_skip=d_skip, w_glu=w_glu, w_branch=w_branch, w_out=w_out, w_up=w_up, w_down=w_down)
    m = dict(norm_mix_pre=m_norm_mix_pre, norm_mix_post=m_norm_mix_post, norm_mlp_pre=m_norm_mlp_pre,
             norm_mlp_post=m_norm_mlp_post, w_in=m_w_in, sinks=m_sinks, lam_re=m_lam_re, lam_im=m_lam_im,
             log_dt=m_log_dt, b_re=m_b_re, b_im=m_b_im, c_re=m_c_re, c_im=m_c_im, d_skip=m_d_skip, w_glu=m_w_glu,
             w_branch=m_w_branch, w_out=m_w_out, w_up=m_w_up, w_down=m_w_down)
    v = dict(norm_mix_pre=v_norm_mix_pre, norm_mix_post=v_norm_mix_post, norm_mlp_pre=v_norm_mlp_pre,
             norm_mlp_post=v_norm_mlp_post, w_in=v_w_in, sinks=v_sinks, lam_re=v_lam_re, lam_im=v_lam_im,
             log_dt=v_log_dt, b_re=v_b_re, b_im=v_b_im, c_re=v_c_re, c_im=v_c_im, d_skip=v_d_skip, w_glu=v_w_glu,
             w_branch=v_w_branch, w_out=v_w_out, w_up=v_w_up, w_down=v_w_down)
    xi, yi, ci = lax.axis_index("x"), lax.axis_index("y"), lax.axis_index("c")

    k_arr = jnp.stack([2 * xi + yi]).astype(jnp.int32)
    slot = {k: _cast_into_slot(w[k][0], k_arr) for k in BIG}

    def whole(k, g4):
        if k in COL_SHARDED:
            return jnp.concatenate([g4[j] for j in range(4)], axis=1)
        return g4.reshape(4 * g4.shape[1], g4.shape[2])

    wi = _comm_only("gather_w_in", _GatherComm([slot["w_in"]]))[0]
    cut = ZA_W - wi.shape[2]
    w_a = jnp.concatenate([wi[0], wi[1][:, :cut]], axis=1)
    w_g = jnp.concatenate([wi[1][:, cut:], wi[2], wi[3]], axis=1)
    hosted = (("w_glu", "w_branch", "w_out"), ("w_up",), ("w_down",))
    comms = [_GatherComm([slot[k] for k in names]) for names in hosted]

    def late(*got):
        f = {k: whole(k, g4) for names, res in zip(hosted, got) for k, g4 in zip(names, res)}
        return f["w_glu"], f["w_branch"][:Q_W], f["w_branch"][Q_W:], f["w_out"], f["w_up"], f["w_down"]

    s5w = (lam_re[0], lam_im[0], log_dt[0], b_re[0], b_im[0], c_re[0], c_im[0], d_skip[0])
    reducer = _GradReducer(jnp.stack([ci]).astype(jnp.int32), jnp.stack([2 * xi + yi, ci]).astype(jnp.int32))
    loss_part, dx, small, _ = _local_step(
        x[0], loss_target[0], (norm_mix_pre, norm_mix_post, norm_mlp_pre, norm_mlp_post),
        w_a, w_g, sinks, s5w, comms, late, reducer)
    grads = reducer.finish(BIG)

    red = _all_reduce_small(_flat_small(small, loss_part)).reshape(-1)
    off = 0
    for k in SMALL:
        n = math.prod(w[k].shape)
        grads[k] = red[off:off + n].reshape(w[k].shape[1:])
        off += n
    loss = red[off]

    delta, new_m, new_v = {}, {}, {}
    for k in BIG:
        delta[k], new_m[k], new_v[k] = _adamw(w[k][0], grads[k], m[k][0], v[k][0], "adamw_" + k)
    zero = jnp.zeros((), F32)
    fw, fm, fv = (_flat_small({k: t[k] for k in SMALL}, zero) for t in (w, m, v))
    fg = _flat_small(grads, zero)
    sd, sm, sv = _adamw(fw, fg, fm, fv, "adamw_small", tr=fw.shape[0])
    off = 0
    for k in SMALL:
        n = math.prod(w[k].shape)
        delta[k], new_m[k], new_v[k] = (t.reshape(-1)[off:off + n].reshape(w[k].shape[1:]) for t in (sd, sm, sv))
        off += n

    lead = lambda t: t[None]
    return (loss, lead(dx), *[lead(grads[k]) for k in WEIGHTS], *[lead(delta[k]) for k in WEIGHTS],
            *[lead(new_m[k]) for k in WEIGHTS], *[lead(new_v[k]) for k in WEIGHTS])
```

```python
import functools
import math

import jax
import jax.numpy as jnp
from jax import lax
from jax.experimental import pallas as pl
from jax.experimental.pallas import tpu as pltpu

F32 = jnp.float32
BF16 = jnp.bfloat16
SDS = jax.ShapeDtypeStruct

D_MODEL = 2048
HEAD_DIM = 64
N_Q_HEADS = 16
ATT_BLOCK = 128
ROT_DIM = 16
ROPE_THETA = 500000.0
Q_W = 1024
KV_W = 128
SSM_W = 1024
SSM_G = 64
SSM_GC = 16
SSM_P = 64
N_STATE = SSM_G * SSM_P
LANES = 128
SUBLANES = 8
N_LG = N_STATE // LANES
N_JB = 8
LG_PER_JB = N_LG // N_JB
D_FF = 8192
ZA_W = Q_W + 2 * KV_W + SSM_W
EPS = 1e-6
S5_CHUNK = 512
S5_SEG = S5_CHUNK // SUBLANES
VMEM_LIMIT = 56 * 1024 * 1024
NEG = -1e30

ADAM_LR = 0.001
ADAM_B1 = 0.9
ADAM_B2 = 0.999
ADAM_EPS = 1e-08
ADAM_WD = 0.01
ADAM_STEP = 10

MESH = pl.DeviceIdType.MESH


def _cp(sem):
    return pltpu.CompilerParams(dimension_semantics=sem, vmem_limit_bytes=VMEM_LIMIT)


ANY = pl.BlockSpec(memory_space=pl.ANY)


def _place():
    x, y, c = lax.axis_index("x"), lax.axis_index("y"), lax.axis_index("c")
    others = [(1 - x, y), (x, 1 - y), (1 - x, 1 - y)]
    return x, y, c, others


def _remote(src, dst, ssem, rsem, to):
    return pltpu.make_async_remote_copy(src_ref=src, dst_ref=dst, send_sem=ssem, recv_sem=rsem,
                                        device_id=to, device_id_type=MESH)


class _GatherComm:
    aliased = True

    def __init__(self, slotted, sender_x=None):
        self.arrs = list(slotted)
        self.n = len(self.arrs)
        self.sender_x = sender_x
        dma = pltpu.SemaphoreType.DMA
        self.scratch = [dma((3 * self.n,)) for _ in range(4)]
        self.out_shape = [SDS(s.shape, s.dtype) for s in self.arrs]

    @staticmethod
    def _half(ref, hc):
        hr = ref.shape[1] // 2
        return pl.ds(pl.multiple_of(hc * hr, 16), hr)

    def _if_sends(self, chip_x, fn):
        if self.sender_x is None:
            fn()
        else:
            pl.when(chip_x == self.sender_x)(fn)

    def _each(self, ins, outs, sems, fn):
        x, y, c, others = _place()
        for w in range(self.n):
            for r, (ox, oy) in enumerate(others):
                fn(3 * w + r, (ox, oy), outs[w].at[2 * ox + oy, self._half(ins[w], c), :],
                   outs[w].at[2 * ox + oy, self._half(ins[w], 1 - c), :])

    def start(self, ins, outs, sems):
        ssem, rsem, _, _ = sems
        x, y, c, others = _place()
        me = 2 * x + y

        def send():
            for w in range(self.n):
                for r, (ox, oy) in enumerate(others):
                    _remote(ins[w].at[me, self._half(ins[w], c), :], outs[w].at[me, self._half(ins[w], c), :],
                            ssem.at[3 * w + r], rsem.at[3 * w + r], (ox, oy, c)).start()
        self._if_sends(x, send)

    def finish(self, ins, outs, sems):
        ssem, rsem, fs_sem, fr_sem = sems
        x, y, c, others = _place()
        me, sib = 2 * x + y, (x, y, 1 - c)

        def arrived(i, chip, got, _):
            def fn():
                _remote(got, got, ssem.at[i], rsem.at[i], (*chip, c)).wait_recv()
                _remote(got, got, fs_sem.at[i], fr_sem.at[i], sib).start()
            self._if_sends(chip[0], fn)

        def passed(i, chip, _, got):
            self._if_sends(chip[0], lambda: _remote(got, got, fs_sem.at[i], fr_sem.at[i], sib).wait_recv())

        def pass_sent(i, chip, got, _):
            self._if_sends(chip[0], lambda: _remote(got, got, fs_sem.at[i], fr_sem.at[i], sib).wait_send())

        def sent():
            for w in range(self.n):
                for r, (ox, oy) in enumerate(others):
                    mine = ins[w].at[me, self._half(ins[w], c), :]
                    _remote(mine, mine, ssem.at[3 * w + r], rsem.at[3 * w + r], (ox, oy, c)).wait_send()

        self._each(ins, outs, sems, arrived)
        self._each(ins, outs, sems, passed)
        self._if_sends(x, sent)
        self._each(ins, outs, sems, pass_sent)


class _PairExchangeComm:
    aliased = False

    def __init__(self, grads):
        self.arrs = list(grads)
        self.n = len(self.arrs)
        dma = pltpu.SemaphoreType.DMA
        self.scratch = [dma((self.n,)), dma((self.n,))]
        self.out_shape = [SDS((4, g.shape[1] // 2, g.shape[2]), g.dtype) for g in self.arrs]

    def _copies(self, ins, outs, sems):
        ssem, rsem = sems
        x, y, c, _ = _place()
        cps = []
        for w in range(self.n):
            hr = ins[w].shape[1] // 2
            src = ins[w].at[:, pl.ds(pl.multiple_of((1 - c) * hr, 8), hr), :]
            cps.append(_remote(src, outs[w], ssem.at[w], rsem.at[w], (x, y, 1 - c)))
        return cps

    def start(self, ins, outs, sems):
        for cp in self._copies(ins, outs, sems):
            cp.start()

    def finish(self, ins, outs, sems):
        for cp in self._copies(ins, outs, sems):
            cp.wait()


class _ChipExchangeComm:
    aliased = False

    def __init__(self, psums):
        self.arrs = list(psums)
        self.n = len(self.arrs)
        dma = pltpu.SemaphoreType.DMA
        self.scratch = [dma((3 * self.n,)), dma((3 * self.n,))]
        self.out_shape = [SDS((3,) + p.shape[1:], p.dtype) for p in self.arrs]

    def _copies(self, ins, outs, sems):
        ssem, rsem = sems
        x, y, c, others = _place()
        return [_remote(ins[w].at[2 * ox + oy], outs[w].at[r], ssem.at[3 * w + r], rsem.at[3 * w + r], (ox, oy, c))
                for w in range(self.n) for r, (ox, oy) in enumerate(others)]

    def start(self, ins, outs, sems):
        for cp in self._copies(ins, outs, sems):
            cp.start()

    def finish(self, ins, outs, sems):
        for cp in self._copies(ins, outs, sems):
            cp.wait()


def _comm_only(name, comm):
    n = comm.n

    def body(*refs):
        ins, outs, sems = refs[:n], refs[n:2 * n], refs[2 * n:]
        comm.start(ins, outs, sems)
        comm.finish(ins, outs, sems)

    return pl.pallas_call(
        body, name=name, in_specs=[ANY] * n, out_specs=[ANY] * n, out_shape=comm.out_shape,
        input_output_aliases={w: w for w in range(n)} if comm.aliased else {},
        scratch_shapes=comm.scratch)(*comm.arrs)


def _call(name, body, grid, in_specs, out_specs, out_shape, scratch, dims, args, comm=None):
    if comm is None:
        return pl.pallas_call(body, name=name, grid=grid, in_specs=in_specs, out_specs=out_specs, out_shape=out_shape,
                              scratch_shapes=scratch, compiler_params=_cp(dims))(*args)
    ni, no, ns, n = len(in_specs), len(out_shape), len(scratch), comm.n

    def hosted(*refs):
        ins, cin = refs[:ni], refs[ni:ni + n]
        outs, cout = refs[ni + n:ni + n + no], refs[ni + n + no:ni + 2 * n + no]
        scr, sems = refs[ni + 2 * n + no:ni + 2 * n + no + ns], refs[ni + 2 * n + no + ns:]
        ids = [pl.program_id(d) for d in range(len(grid))]
        first = functools.reduce(jnp.logical_and, [i == 0 for i in ids])
        last = functools.reduce(jnp.logical_and, [i == g - 1 for i, g in zip(ids, grid)])

        @pl.when(first)
        def _():
            comm.start(cin, cout, sems)

        body(*ins, *outs, *scr)

        @pl.when(last)
        def _():
            comm.finish(cin, cout, sems)

    return pl.pallas_call(
        hosted, name=name, grid=grid, in_specs=list(in_specs) + [ANY] * n, out_specs=list(out_specs) + [ANY] * n,
        out_shape=list(out_shape) + comm.out_shape,
        input_output_aliases={ni + w: no + w for w in range(n)} if comm.aliased else {},
        scratch_shapes=list(scratch) + comm.scratch, compiler_params=_cp(("arbitrary",) * len(grid)))(*args, *comm.arrs)


def _mm(a, b, *, mode, out_dtype, tm, tn, tk, name, a_fn=None, epi=None, extras=(), comm=None, shard_cols=None):
    if mode == "nn":
        (M, K), (K2, N) = a.shape, b.shape
    elif mode == "nt":
        (M, K), (N, K2) = a.shape, b.shape
    else:
        (K, M), (K2, N) = a.shape, b.shape
    assert K == K2, (a.shape, b.shape, mode)
    tm, tn, tk = min(tm, M), min(tn, N), min(tk, K)
    assert M % tm == 0 and N % tn == 0 and K % tk == 0, (M, N, K, tm, tn, tk)
    nk = K // tk
    if mode == "tn":
        a_spec = pl.BlockSpec((tk, tm), lambda i, j, k: (k, i))
        ca = 0
    else:
        a_spec = pl.BlockSpec((tm, tk), lambda i, j, k: (i, k))
        ca = 1
    if mode == "nt":
        b_spec = pl.BlockSpec((tn, tk), lambda i, j, k: (j, k))
        cb = 1
    else:
        b_spec = pl.BlockSpec((tk, tn), lambda i, j, k: (k, j))
        cb = 0
    dims = (((ca,), (cb,)), ((), ()))
    ne = len(extras)

    def body(a_ref, b_ref, *rest):
        ex = rest[:ne]
        o_ref = rest[ne]
        av = a_ref[...]
        if a_fn is not None:
            av = a_fn(av.astype(F32))
        p = lax.dot_general(av.astype(BF16), b_ref[...].astype(BF16), dims, preferred_element_type=F32)

        def fin(v):
            if epi is not None:
                v = epi(v, *[e[...] for e in ex])
            o_ref[...] = v.astype(out_dtype).reshape(o_ref.shape)

        if nk == 1:
            fin(p)
        else:
            acc = rest[ne + 1]
            k = pl.program_id(2)

            @pl.when(k == 0)
            def _():
                acc[...] = p

            @pl.when(k > 0)
            def _():
                acc[...] += p

            @pl.when(k == nk - 1)
            def _():
                fin(acc[...])

    if shard_cols is None:
        o_spec, o_shape = pl.BlockSpec((tm, tn), lambda i, j, k: (i, j)), SDS((M, N), out_dtype)
    else:
        per = shard_cols // tn
        assert shard_cols % tn == 0 and N % shard_cols == 0
        o_spec = pl.BlockSpec((1, tm, tn), lambda i, j, k: (lax.div(j, per), i, lax.rem(j, per)))
        o_shape = SDS((N // shard_cols, M, shard_cols), out_dtype)
    res = _call(name, body, (M // tm, N // tn, nk),
                [a_spec, b_spec] + [pl.BlockSpec((tm, tn), lambda i, j, k: (i, j)) for _ in extras],
                [o_spec], [o_shape],
                [pltpu.VMEM((tm, tn), F32)] if nk > 1 else [], ("parallel", "parallel", "arbitrary"),
                (a, b, *extras), comm)
    return res[0] if comm is None else (res[0], res[1:])


def _rowwise(fn, rows, bcasts, outs, accs, *, tr, name):
    T = rows[0][0].shape[0]
    tr = min(tr, T)
    assert T % tr == 0
    nr, nb, no, na = len(rows), len(bcasts), len(outs), len(accs)
    in_specs = [pl.BlockSpec((tr, w), functools.partial(lambda i, c: (i, c), c=cb)) for (_, w, cb) in rows]
    in_specs += [pl.BlockSpec(b.shape, lambda i: (0, 0)) for b in bcasts]
    out_shape = [SDS((T, w), dt) for (w, dt) in outs] + [SDS(s, F32) for s in accs]
    out_specs = [pl.BlockSpec((tr, w), lambda i: (i, 0)) for (w, _) in outs]
    out_specs += [pl.BlockSpec(s, lambda i: (0, 0)) for s in accs]

    def body(*refs):
        ins = [r[...].astype(F32) for r in refs[:nr + nb]]
        o_refs = refs[nr + nb:nr + nb + no]
        a_refs = refs[nr + nb + no:]
        ro, ao = fn(*ins)
        for r, v in zip(o_refs, ro):
            r[...] = v.astype(r.dtype)
        if na:
            @pl.when(pl.program_id(0) == 0)
            def _():
                for r in a_refs:
                    r[...] = jnp.zeros(r.shape, F32)

            for r, v in zip(a_refs, ao):
                r[...] += v

    res = pl.pallas_call(
        body,
        name=name,
        grid=(T // tr,),
        in_specs=in_specs,
        out_specs=out_specs,
        out_shape=out_shape,
        compiler_params=_cp(("arbitrary",) if na else ("parallel",)),
    )(*[r[0] for r in rows], *bcasts)
    return res


def _rms(v):
    r = lax.rsqrt(jnp.mean(v * v, axis=-1, keepdims=True) + EPS)
    return v * r, r


def _rms_bwd(dy, xn, r, g):
    dxn = dy * g
    dv = r * (dxn - xn * jnp.mean(dxn * xn, axis=-1, keepdims=True))
    return dv, jnp.sum(dy * xn, axis=0, keepdims=True)


def _sig(v):
    return 1.0 / (1.0 + jnp.exp(-v))


_GELU_C = math.sqrt(2.0 / math.pi)


def _gelu(v):
    return 0.5 * v * (1.0 + jnp.tanh(_GELU_C * (v + 0.044715 * v * v * v)))


def _gelu_grad(v):
    t = jnp.tanh(_GELU_C * (v + 0.044715 * v * v * v))
    return 0.5 * (1.0 + t) + 0.5 * v * (1.0 - t * t) * _GELU_C * (1.0 + 3.0 * 0.044715 * v * v)


def _rope(v, c, s, sign):
    w = v.shape[1]
    m = lax.broadcasted_iota(jnp.int32, v.shape, 1) % HEAD_DIM
    p = jnp.where(m < ROT_DIM // 2, -pltpu.roll(v, w - ROT_DIM // 2, 1), pltpu.roll(v, ROT_DIM // 2, 1))
    return v * c + sign * (p * s)


def _rope_tables(T):
    half = ROT_DIM // 2
    inv = ROPE_THETA ** (-jnp.arange(half, dtype=F32) * 2.0 / ROT_DIM)
    ang = jnp.arange(T).astype(F32)[:, None] * inv[None, :]
    cos, sin = jnp.cos(ang), jnp.sin(ang)
    one = jnp.ones((T, HEAD_DIM - ROT_DIM), F32)
    c64 = jnp.concatenate([cos, cos, one], axis=1)
    s64 = jnp.concatenate([sin, sin, 0.0 * one], axis=1)
    return jnp.tile(c64, (1, 2)), jnp.tile(s64, (1, 2))


def _dup_half(m, lo):
    lane = lax.broadcasted_iota(jnp.int32, m.shape, 1)
    sw = pltpu.roll(m, HEAD_DIM, 1)
    return jnp.where(lane < HEAD_DIM, m, sw) if lo else jnp.where(lane >= HEAD_DIM, m, sw)


def _attn_mask(i):
    qi = lax.broadcasted_iota(jnp.int32, (ATT_BLOCK, 2 * ATT_BLOCK), 0)
    kj = lax.broadcasted_iota(jnp.int32, (ATT_BLOCK, 2 * ATT_BLOCK), 1)
    rel = qi + ATT_BLOCK - kj
    return (rel >= 0) & (rel < ATT_BLOCK) & ((kj >= ATT_BLOCK) | (i > 0))


_NT = (((1,), (1,)), ((), ()))
_TN = (((0,), (0,)), ((), ()))


def _stack_heads(m):
    lane = lax.broadcasted_iota(jnp.int32, m.shape, 1)
    zero = jnp.zeros_like(m)
    return jnp.concatenate([jnp.where(lane < HEAD_DIM, m, zero), jnp.where(lane >= HEAD_DIM, m, zero)], axis=0)


def _pair_probs(q2, k2, ok2, sink_lo, sink_hi):
    qs = _stack_heads(q2)
    s = lax.dot_general(qs, k2, _NT, preferred_element_type=F32)
    s = jnp.where(ok2, s, NEG)
    row = lax.broadcasted_iota(jnp.int32, (2 * ATT_BLOCK, 1), 0)
    sink = jnp.where(row < ATT_BLOCK, sink_lo, sink_hi)
    m = jnp.maximum(jnp.max(s, axis=1, keepdims=True), sink)
    e = jnp.exp(s - m)
    es = jnp.exp(sink - m)
    inv = 1.0 / (jnp.sum(e, axis=1, keepdims=True) + es)
    return e * inv, es * inv, qs


def _attn_fwd(za, cos, sin, sinks, comm=None):
    T = za.shape[0]
    nb = T // ATT_BLOCK
    kvb = Q_W // (2 * KV_W)

    def body(sink_ref, q_ref, kvp_ref, kvc_ref, cc_ref, sc_ref, cp_ref, sp_ref, o_ref):
        i = pl.program_id(0)
        cc, sc, cp, sp = cc_ref[...], sc_ref[...], cp_ref[...], sp_ref[...]
        q = (_rope(q_ref[...], jnp.tile(cc, (1, 8)), jnp.tile(sc, (1, 8)), 1.0) * 0.125).astype(BF16)
        kvp, kvc = kvp_ref[...], kvc_ref[...]
        k = jnp.concatenate([_rope(kvp[:, :KV_W], cp, sp, 1.0), _rope(kvc[:, :KV_W], cc, sc, 1.0)], axis=0).astype(BF16)
        v = jnp.concatenate([kvp[:, KV_W:], kvc[:, KV_W:]], axis=0).astype(BF16)
        ok = _attn_mask(i)
        ok2 = jnp.concatenate([ok, ok], axis=0)
        lane = lax.broadcasted_iota(jnp.int32, (ATT_BLOCK, LANES), 1)
        for kvh in range(2):
            k2 = _dup_half(k, kvh == 0)
            v2 = _dup_half(v, kvh == 0)
            for pair in range(4):
                c0 = (kvh * 4 + pair) * LANES
                q2 = q[:, c0:c0 + LANES]
                p, _, _ = _pair_probs(q2, k2, ok2, sink_ref[0, 2 * (kvh * 4 + pair)], sink_ref[0, 2 * (kvh * 4 + pair) + 1])
                o = jnp.dot(p.astype(BF16), v2, preferred_element_type=F32)
                o_ref[:, c0:c0 + LANES] = jnp.where(lane < HEAD_DIM, o[:ATT_BLOCK], o[ATT_BLOCK:]).astype(BF16)

    blk = lambda w, f: pl.BlockSpec((ATT_BLOCK, w), f)
    res = _call(
        "attn_fwd", body, (nb,),
        [
            pl.BlockSpec(memory_space=pltpu.SMEM),
            blk(Q_W, lambda i: (i, 0)),
            blk(2 * KV_W, lambda i: (jnp.maximum(i - 1, 0), kvb)),
            blk(2 * KV_W, lambda i: (i, kvb)),
            blk(LANES, lambda i: (i, 0)),
            blk(LANES, lambda i: (i, 0)),
            blk(LANES, lambda i: (jnp.maximum(i - 1, 0), 0)),
            blk(LANES, lambda i: (jnp.maximum(i - 1, 0), 0)),
        ],
        [blk(Q_W, lambda i: (i, 0))], [SDS((T, Q_W), BF16)], [], ("parallel",),
        (sinks, za, za, za, cos, sin, cos, sin), comm)
    return res[0] if comm is None else (res[0], res[1:])


def _attn_bwd(za, cos, sin, sinks, o, do):
    T = za.shape[0]
    nb = T // ATT_BLOCK
    kvb = Q_W // (2 * KV_W)

    def body(sink_ref, q_ref, kvp_ref, kvc_ref, cc_ref, sc_ref, cp_ref, sp_ref, o_ref, do_ref,
             dq_ref, dkv_ref, dsk_ref, carry, dqs):
        i = pl.program_id(0)

        @pl.when(i == 0)
        def _():
            carry[...] = jnp.zeros(carry.shape, F32)
            dsk_ref[...] = jnp.zeros(dsk_ref.shape, F32)

        @pl.when(i < nb)
        def _():
            cc, sc, cp, sp = cc_ref[...], sc_ref[...], cp_ref[...], sp_ref[...]
            ccq, scq = jnp.tile(cc, (1, 8)), jnp.tile(sc, (1, 8))
            q = (_rope(q_ref[...], ccq, scq, 1.0) * 0.125).astype(BF16)
            kvp, kvc = kvp_ref[...], kvc_ref[...]
            k = jnp.concatenate([_rope(kvp[:, :KV_W], cp, sp, 1.0), _rope(kvc[:, :KV_W], cc, sc, 1.0)], axis=0).astype(BF16)
            v = jnp.concatenate([kvp[:, KV_W:], kvc[:, KV_W:]], axis=0).astype(BF16)
            ok = _attn_mask(i)
            ok2 = jnp.concatenate([ok, ok], axis=0)
            lane = lax.broadcasted_iota(jnp.int32, (ATT_BLOCK, LANES), 1)
            lane_s = lax.broadcasted_iota(jnp.int32, (1, LANES), 1)
            dsk = jnp.zeros((1, LANES), F32)
            dkt_h, dvt_h = [], []
            for kvh in range(2):
                k2 = _dup_half(k, kvh == 0)
                v2 = _dup_half(v, kvh == 0)
                dkt = jnp.zeros((LANES, 2 * ATT_BLOCK), F32)
                dvt = jnp.zeros((LANES, 2 * ATT_BLOCK), F32)
                for pair in range(4):
                    h = 2 * (kvh * 4 + pair)
                    c0 = (kvh * 4 + pair) * LANES
                    do2 = do_ref[:, c0:c0 + LANES]
                    prod = do2.astype(F32) * o_ref[:, c0:c0 + LANES].astype(F32)
                    d_lo = jnp.sum(jnp.where(lane < HEAD_DIM, prod, 0.0), axis=1, keepdims=True)
                    d_hi = jnp.sum(jnp.where(lane >= HEAD_DIM, prod, 0.0), axis=1, keepdims=True)
                    delta = jnp.concatenate([d_lo, d_hi], axis=0)
                    p, p_sink, qs = _pair_probs(q[:, c0:c0 + LANES], k2, ok2, sink_ref[0, h], sink_ref[0, h + 1])
                    dos = _stack_heads(do2)
                    t = p_sink * delta
                    dsk = dsk - jnp.where(lane_s == h, jnp.sum(t[:ATT_BLOCK]), 0.0) \
                              - jnp.where(lane_s == h + 1, jnp.sum(t[ATT_BLOCK:]), 0.0)
                    dp = lax.dot_general(dos, v2, _NT, preferred_element_type=F32)
                    ds = (p * (dp - delta)).astype(BF16)
                    dqp = jnp.dot(ds, k2, preferred_element_type=F32)
                    dqs[:, c0:c0 + LANES] = jnp.where(lane < HEAD_DIM, dqp[:ATT_BLOCK], dqp[ATT_BLOCK:]) * 0.125
                    dkt = dkt + lax.dot_general(qs, ds, _TN, preferred_element_type=F32)
                    dvt = dvt + lax.dot_general(dos, p.astype(BF16), _TN, preferred_element_type=F32)
                dkt_h.append(dkt[:HEAD_DIM] + dkt[HEAD_DIM:])
                dvt_h.append(dvt[:HEAD_DIM] + dvt[HEAD_DIM:])
            dk = jnp.concatenate(dkt_h, axis=0).T
            dv = jnp.concatenate(dvt_h, axis=0).T
            dq_ref[...] = _rope(dqs[...], ccq, scq, -1.0).astype(dq_ref.dtype)
            dkp = _rope(dk[:ATT_BLOCK], cp, sp, -1.0)
            dkc = _rope(dk[ATT_BLOCK:], cc, sc, -1.0)
            dkv_ref[...] = (carry[...] + jnp.concatenate([dkp, dv[:ATT_BLOCK]], axis=1)).astype(dkv_ref.dtype)
            carry[...] = jnp.concatenate([dkc, dv[ATT_BLOCK:]], axis=1)
            dsk_ref[...] += dsk

        @pl.when(i == nb)
        def _():
            dkv_ref[...] = carry[...].astype(dkv_ref.dtype)

    blk = lambda w, f: pl.BlockSpec((ATT_BLOCK, w), f)
    cur = lambda i: jnp.minimum(i, nb - 1)
    prv = lambda i: jnp.maximum(jnp.minimum(i, nb - 1) - 1, 0)
    return pl.pallas_call(
        body,
        name="attn_bwd",
        grid=(nb + 1,),
        in_specs=[
            pl.BlockSpec(memory_space=pltpu.SMEM),
            blk(Q_W, lambda i: (cur(i), 0)),
            blk(2 * KV_W, lambda i: (prv(i), kvb)),
            blk(2 * KV_W, lambda i: (cur(i), kvb)),
            blk(LANES, lambda i: (cur(i), 0)),
            blk(LANES, lambda i: (cur(i), 0)),
            blk(LANES, lambda i: (prv(i), 0)),
            blk(LANES, lambda i: (prv(i), 0)),
            blk(Q_W, lambda i: (cur(i), 0)),
            blk(Q_W, lambda i: (cur(i), 0)),
        ],
        out_specs=[
            blk(Q_W, lambda i: (cur(i), 0)),
            blk(2 * KV_W, lambda i: (jnp.maximum(i - 1, 0), 0)),
            pl.BlockSpec((1, LANES), lambda i: (0, 0)),
        ],
        out_shape=[SDS((T, Q_W), BF16), SDS((T, 2 * KV_W), BF16), SDS((1, LANES), F32)],
        scratch_shapes=[pltpu.VMEM((ATT_BLOCK, 2 * KV_W), F32), pltpu.VMEM((ATT_BLOCK, Q_W), F32)],
        compiler_params=_cp(("arbitrary",)),
    )(sinks, za, za, za, cos, sin, cos, sin, o, do)


def _s5_discretize(lam_re, lam_im, log_dt, b_re, b_im):
    dt = jnp.exp(log_dt)[:, None]
    mag = jnp.exp(lam_re * dt)
    a_re, a_im = mag * jnp.cos(lam_im * dt), mag * jnp.sin(lam_im * dt)
    den = lam_re * lam_re + lam_im * lam_im
    nr, ni = a_re - 1.0, a_im
    coef_re = (nr * lam_re + ni * lam_im) / den
    coef_im = (ni * lam_re - nr * lam_im) / den
    bb_re = coef_re[..., None] * b_re - coef_im[..., None] * b_im
    bb_im = coef_re[..., None] * b_im + coef_im[..., None] * b_re
    return a_re, a_im, bb_re, bb_im


def _blockdiag_in(bb):
    x = bb.reshape(N_JB, 8, SSM_P, SSM_GC).transpose(0, 1, 3, 2)
    return (x[:, :, :, None, :] * jnp.eye(8, dtype=bb.dtype)[None, :, None, :, None]).reshape(N_JB, 128, 512)


def _blockdiag_in_extract(m):
    x = m.reshape(N_JB, 8, SSM_GC, 8, SSM_P)
    x = jnp.einsum('jgchp,gh->jgcp', x, jnp.eye(8, dtype=m.dtype))
    return x.transpose(0, 1, 3, 2).reshape(SSM_G, SSM_P, SSM_GC)


def _blockdiag_out(c):
    x = c.reshape(N_JB, 8, SSM_GC, SSM_P).transpose(0, 1, 3, 2)
    return (x[:, :, :, None, :] * jnp.eye(8, dtype=c.dtype)[None, :, None, :, None]).reshape(N_JB, 512, 128)


def _blockdiag_out_extract(m):
    x = m.reshape(N_JB, 8, SSM_P, 8, SSM_GC)
    x = jnp.einsum('jgphc,gh->jgpc', x, jnp.eye(8, dtype=m.dtype))
    return x.transpose(0, 1, 3, 2).reshape(SSM_G, SSM_GC, SSM_P)


def _s5_tables(a_re, a_im):
    ar, ai = a_re.reshape(N_LG, 1, LANES), a_im.reshape(N_LG, 1, LANES)
    pr, pi, n = ar, ai, 1
    while n < S5_SEG:
        pr, pi, n = pr * pr - pi * pi, 2.0 * pr * pi, 2 * n
    assert n == S5_SEG
    bc = lambda v: jnp.broadcast_to(v, (N_LG, SUBLANES, LANES))
    return bc(ar), bc(ai), pr, pi


def _s5_to_time_major(src_ref, dst_ref):
    for t in range(S5_SEG):
        dst_ref[t * SUBLANES:(t + 1) * SUBLANES, :] = src_ref[pl.ds(t, SUBLANES, stride=S5_SEG), :]


def _s5_from_time_major(val, dst_ref):
    for t in range(S5_SEG):
        dst_ref[pl.ds(t, SUBLANES, stride=S5_SEG), :] = val[t * SUBLANES:(t + 1) * SUBLANES, :]


def _tm_rows(t, row0=0):
    return pl.ds(pl.multiple_of(t * SUBLANES + row0, SUBLANES), SUBLANES)


def _s5_scan(src_re, src_im, ar, ai, reverse, start=None, dst=None, dst_row0=0):
    def step(n, carry):
        t = (S5_SEG - 1 - n) if reverse else n
        out = []
        for ll in range(LG_PER_JB):
            xr, xi = carry[2 * ll], carry[2 * ll + 1]
            idx = (ll, _tm_rows(t), slice(None))
            nr = ar[ll] * xr - ai[ll] * xi + src_re[idx]
            ni = ar[ll] * xi + ai[ll] * xr + src_im[idx]
            if dst is not None:
                odx = (ll, _tm_rows(t, dst_row0), slice(None))
                dst[0][odx] = nr
                dst[1][odx] = ni
            out += [nr, ni]
        return tuple(out)
    if start is None:
        init = (jnp.zeros((SUBLANES, LANES), F32),) * (2 * LG_PER_JB)
    else:
        init = tuple(s[ll] for ll in range(LG_PER_JB) for s in start)
    return lax.fori_loop(0, S5_SEG, step, init)


def _s5_fixup(ends, in_re, in_im, mr, mi, s_re, s_im, reverse):
    cr, ci = in_re, in_im
    order = range(SUBLANES - 1, -1, -1) if reverse else range(SUBLANES)
    for s in order:
        s_re[:, s:s + 1, :] = cr
        s_im[:, s:s + 1, :] = ci
        er = jnp.stack([ends[2 * ll][s:s + 1, :] for ll in range(LG_PER_JB)])
        ei = jnp.stack([ends[2 * ll + 1][s:s + 1, :] for ll in range(LG_PER_JB)])
        cr, ci = mr * cr - mi * ci + er, mr * ci + mi * cr + ei
    return cr, ci


def _s5_specs(nc, rev):
    cidx = (lambda c: nc - 1 - c) if rev else (lambda c: c)
    jb = lambda shape: pl.BlockSpec(shape, lambda j, c: (j, 0, 0))
    return cidx, [
        jb((1, LANES, 8 * LANES)),
        jb((1, 8 * LANES, LANES)),
        pl.BlockSpec((1, LANES), lambda j, c: (0, j)),
        jb((LG_PER_JB, SUBLANES, LANES)), jb((LG_PER_JB, SUBLANES, LANES)),
        jb((LG_PER_JB, 1, LANES)), jb((LG_PER_JB, 1, LANES)),
    ]


def _s5_fwd(za, prm, comm=None):
    T = za.shape[0]
    R = S5_CHUNK
    nc = T // R
    ub = (Q_W + 2 * KV_W) // LANES
    _, pspecs = _s5_specs(nc, False)

    def body(u_ref, b_ref, c_ref, d_ref, are_ref, aim_ref, alr_ref, ali_ref,
             yg_ref, x0r_ref, x0i_ref, bur, bui, xsr, xsi, sr, si, xcr, xci, utm, ynat):
        c = pl.program_id(1)

        @pl.when(c == 0)
        def _():
            xcr[...] = jnp.zeros(xcr.shape, F32)
            xci[...] = jnp.zeros(xci.shape, F32)

        _s5_to_time_major(u_ref, utm)
        u = utm[...]
        ub16 = u.astype(BF16)
        bu = jnp.dot(ub16, b_ref[0].astype(BF16), preferred_element_type=F32)
        for ll in range(LG_PER_JB):
            bur[ll] = bu[:, ll * LANES:(ll + 1) * LANES]
            bui[ll] = bu[:, (LG_PER_JB + ll) * LANES:(LG_PER_JB + ll + 1) * LANES]
        ar = [are_ref[ll] for ll in range(LG_PER_JB)]
        ai = [aim_ref[ll] for ll in range(LG_PER_JB)]
        ends = _s5_scan(bur, bui, ar, ai, False)
        in_r, in_i = xcr[...], xci[...]
        x0r_ref[0] = in_r
        x0i_ref[0] = in_i
        out_r, out_i = _s5_fixup(ends, in_r, in_i, alr_ref[...], ali_ref[...], sr, si, False)
        xcr[...] = out_r
        xci[...] = out_i
        _s5_scan(bur, bui, ar, ai, False, start=(sr, si), dst=(xsr, xsi))
        xcat =jnp.concatenate([xsr[ll].astype(BF16) for ll in range(LG_PER_JB)]
                               + [xsi[ll].astype(BF16) for ll in range(LG_PER_JB)], axis=1)
        y = d_ref[...] * u + jnp.dot(xcat, c_ref[0].astype(BF16), preferred_element_type=F32)
        _s5_from_time_major(_gelu(y), ynat)
        yg_ref[...] = ynat[...].astype(BF16)

    st = pl.BlockSpec((1, LG_PER_JB, 1, LANES), lambda j, c: (c, j, 0, 0))
    vm = lambda rows: pltpu.VMEM((LG_PER_JB, rows, LANES), F32)
    res = _call(
        "s5_fwd", body, (N_JB, nc),
        [pl.BlockSpec((R, LANES), lambda j, c: (c, ub + j))] + pspecs,
        [pl.BlockSpec((R, LANES), lambda j, c: (c, j)), st, st],
        [SDS((T, SSM_W), BF16), SDS((nc, N_LG, 1, LANES), F32), SDS((nc, N_LG, 1, LANES), F32)],
        [vm(R), vm(R), vm(R), vm(R), vm(SUBLANES), vm(SUBLANES), vm(1), vm(1),
         pltpu.VMEM((R, LANES), F32), pltpu.VMEM((R, LANES), F32)],
        ("parallel", "arbitrary"), (za, *prm), comm)
    return res if comm is None else (res[:3], res[3:])


def _s5_bwd(za, dyg, x0r, x0i, prm, comm=None):
    T = za.shape[0]
    R = S5_CHUNK
    nc = T // R
    ub = (Q_W + 2 * KV_W) // LANES
    cidx, pspecs = _s5_specs(nc, True)
    PAD = SUBLANES

    def body(u_ref, dyg_ref, x0r_ref, x0i_ref, b_ref, c_ref, d_ref, are_ref, aim_ref,
             alr_ref, ali_ref,
             du_ref, dar_ref, dai_ref, db_ref, dc_ref, dd_ref,
             bur, bui, xsr, xsi, sr, si, gcr, gci, utm, dtm, dunat):
        c = pl.program_id(1)

        @pl.when(c == 0)
        def _():
            gcr[...] = jnp.zeros(gcr.shape, F32)
            gci[...] = jnp.zeros(gci.shape, F32)
            dar_ref[...] = jnp.zeros(dar_ref.shape, F32)
            dai_ref[...] = jnp.zeros(dai_ref.shape, F32)
            db_ref[...] = jnp.zeros(db_ref.shape, F32)
            dc_ref[...] = jnp.zeros(dc_ref.shape, F32)
            dd_ref[...] = jnp.zeros(dd_ref.shape, F32)

        _s5_to_time_major(u_ref, utm)
        _s5_to_time_major(dyg_ref, dtm)
        u = utm[...]
        ub16 = u.astype(BF16)
        bcat, ccat = b_ref[0].astype(BF16), c_ref[0].astype(BF16)
        lanes = lambda v, ll: v[:, ll * LANES:(ll + 1) * LANES]
        bu = jnp.dot(ub16, bcat, preferred_element_type=F32)
        for ll in range(LG_PER_JB):
            bur[ll] = lanes(bu, ll)
            bui[ll] = lanes(bu, LG_PER_JB + ll)
        ar = [are_ref[ll] for ll in range(LG_PER_JB)]
        ai = [aim_ref[ll] for ll in range(LG_PER_JB)]
        ends = _s5_scan(bur, bui, ar, ai, False)
        in_r, in_i = x0r_ref[0], x0i_ref[0]
        _s5_fixup(ends, in_r, in_i, alr_ref[...], ali_ref[...], sr, si, False)
        _s5_scan(bur, bui, ar, ai, False, start=(sr, si), dst=(xsr, xsi), dst_row0=PAD)
        xsr[:, 0:PAD, :] = sr[...]
        xsi[:, 0:PAD, :] = si[...]
        xcat = jnp.concatenate([xsr[ll, PAD:, :].astype(BF16) for ll in range(LG_PER_JB)]
                               + [xsi[ll, PAD:, :].astype(BF16) for ll in range(LG_PER_JB)], axis=1)
        y = d_ref[...] * u + jnp.dot(xcat, ccat, preferred_element_type=F32)
        dy = dtm[...] * _gelu_grad(y)
        dyb = dy.astype(BF16)
        dd_ref[...] += jnp.sum(dy * u, axis=0, keepdims=True)
        du = d_ref[...] * dy
        dc_ref[0] += lax.dot_general(dyb, xcat, _TN, preferred_element_type=F32)
        g = lax.dot_general(dyb, ccat, _NT, preferred_element_type=F32)
        for ll in range(LG_PER_JB):
            bur[ll] = lanes(g, ll)
            bui[ll] = lanes(g, LG_PER_JB + ll)
        aic = [-v for v in ai]
        ends = _s5_scan(bur, bui, ar, aic, True)
        out_r, out_i = _s5_fixup(ends, gcr[...], gci[...], alr_ref[...], -ali_ref[...], sr, si, True)
        gcr[...] = out_r
        gci[...] = out_i
        _s5_scan(bur, bui, ar, aic, True, start=(sr, si), dst=(bur, bui))
        for ll in range(LG_PER_JB):
            gr, gi = bur[ll], bui[ll]
            xpr, xpi = xsr[ll, 0:R, :], xsi[ll, 0:R, :]
            red = lambda v: v.reshape(R // SUBLANES, SUBLANES, LANES).sum(axis=0)
            dar_ref[ll] += red(xpr * gr + xpi * gi)
            dai_ref[ll] += red(xpr * gi - xpi * gr)
        gcat = jnp.concatenate([bur[ll].astype(BF16) for ll in range(LG_PER_JB)]
                               + [bui[ll].astype(BF16) for ll in range(LG_PER_JB)], axis=1)
        db_ref[0] += lax.dot_general(ub16, gcat, _TN, preferred_element_type=F32)
        du = du + lax.dot_general(gcat, bcat, _NT, preferred_element_type=F32)
        _s5_from_time_major(du, dunat)
        du_ref[...] = dunat[...].astype(du_ref.dtype)

    st = pl.BlockSpec((1, LG_PER_JB, 1, LANES), lambda j, c: (cidx(c), j, 0, 0))
    jb = lambda shape: pl.BlockSpec(shape, lambda j, c: (j, 0, 0))
    vm = lambda rows: pltpu.VMEM((LG_PER_JB, rows, LANES), F32)
    res = _call(
        "s5_bwd", body, (N_JB, nc),
        [pl.BlockSpec((R, LANES), lambda j, c: (cidx(c), ub + j)),
         pl.BlockSpec((R, LANES), lambda j, c: (cidx(c), j)), st, st] + pspecs,
        [pl.BlockSpec((R, LANES), lambda j, c: (cidx(c), j)),
         jb((LG_PER_JB, SUBLANES, LANES)), jb((LG_PER_JB, SUBLANES, LANES)),
         jb((1, LANES, 8 * LANES)), jb((1, LANES, 8 * LANES)),
         pl.BlockSpec((1, LANES), lambda j, c: (0, j))],
        [SDS((T, SSM_W), BF16), SDS((N_LG, SUBLANES, LANES), F32), SDS((N_LG, SUBLANES, LANES), F32),
         SDS((N_JB, LANES, 8 * LANES), F32), SDS((N_JB, LANES, 8 * LANES), F32), SDS((1, SSM_W), F32)],
        [vm(R), vm(R), vm(R + PAD), vm(R + PAD), vm(SUBLANES), vm(SUBLANES), vm(1), vm(1)]
        + [pltpu.VMEM((R, LANES), F32)] * 3,
        ("parallel", "arbitrary"), (za, dyg, x0r, x0i, *prm), comm)
    return res if comm is None else (res[:6], res[6:])


def _assemble_w_a(wi):
    _, rows, cb = wi.shape
    tr = 256

    def body(w_ref, a_ref):
        a_ref[:, :cb] = w_ref[0]
        a_ref[:, cb:] = w_ref[1, :, :ZA_W - cb]

    return pl.pallas_call(
        body, name="assemble_w_a", grid=(rows // tr,),
        in_specs=[pl.BlockSpec((2, tr, cb), lambda i: (0, i, 0))],
        out_specs=pl.BlockSpec((tr, ZA_W), lambda i: (i, 0)),
        out_shape=SDS((rows, ZA_W), wi.dtype), compiler_params=_cp(("parallel",)))(wi)


def _assemble_w_g(wi):
    _, rows, cb = wi.shape
    tr = 256
    cut = ZA_W - cb

    def body(w_ref, g_ref):
        g_ref[:, :cb - cut] = w_ref[1, :, cut:]
        g_ref[:, cb - cut:2 * cb - cut] = w_ref[2]
        g_ref[:, 2 * cb - cut:] = w_ref[3]

    return pl.pallas_call(
        body, name="assemble_w_g", grid=(rows // tr,),
        in_specs=[pl.BlockSpec((4, tr, cb), lambda i: (0, i, 0))],
        out_specs=pl.BlockSpec((tr, 4 * cb - ZA_W), lambda i: (i, 0)),
        out_shape=SDS((rows, 4 * cb - ZA_W), wi.dtype), compiler_params=_cp(("parallel",)))(wi)


def _stack_w_in_grad(d_w_a, d_w_g):
    rows = d_w_a.shape[0]
    cb = (ZA_W + d_w_g.shape[1]) // 4
    cut = ZA_W - cb
    tr = 256

    def body(a_ref, g_ref, o_ref):
        o_ref[0] = a_ref[:, :cb]
        o_ref[1, :, :cut] = a_ref[:, cb:]
        o_ref[1, :, cut:] = g_ref[:, :cb - cut]
        o_ref[2] = g_ref[:, cb - cut:2 * cb - cut]
        o_ref[3] = g_ref[:, 2 * cb - cut:]

    return pl.pallas_call(
        body, name="stack_w_in_grad", grid=(rows // tr,),
        in_specs=[pl.BlockSpec((tr, ZA_W), lambda i: (i, 0)), pl.BlockSpec((tr, d_w_g.shape[1]), lambda i: (i, 0))],
        out_specs=pl.BlockSpec((4, tr, cb), lambda i: (0, i, 0)),
        out_shape=SDS((4, rows, cb), d_w_a.dtype), compiler_params=_cp(("parallel",)))(d_w_a, d_w_g)


def _local_step(x, target, gains, w_a, sinks, s5w, comms, late_g, late, red=None):
    T = x.shape[0]
    D = D_MODEL
    g1, g2, g3, g4 = gains
    cos, sin = _rope_tables(T)
    lam_re, lam_im, log_dt, b_re, b_im, c_re, c_im, d_skip = s5w
    (a_re, a_im, bb_re, bb_im), disc_vjp = jax.vjp(_s5_discretize, lam_re, lam_im, log_dt, b_re, b_im)
    abr, abi, al_re, al_im = _s5_tables(a_re, a_im)
    prm = (jnp.concatenate([_blockdiag_in(bb_re), _blockdiag_in(bb_im)], axis=2),
           jnp.concatenate([_blockdiag_out(c_re), -_blockdiag_out(c_im)], axis=1),
           d_skip.reshape(1, SSM_W), abr, abi, al_re, al_im)
    mm = functools.partial(_mm, tm=1024, tn=1024, tk=2048)

    h = _rowwise(lambda xv, g: ((_rms(xv)[0] * g,), ()), [(x, D, 0)], [g1], [(D, BF16)], [], tr=512, name="norm1")[0]
    unpack = lambda res, comm: (res, ()) if comm is None else res
    za, got_a = unpack(_mm(h, w_a, mode="nn", out_dtype=F32, tm=1024, tn=1152, tk=2048, name="mm_za", comm=comms[0]), comms[0])
    w_g = late_g(got_a)
    zg, got0 = unpack(mm(h, w_g, mode="nn", out_dtype=BF16, name="mm_zg", comm=comms[1]), comms[1])
    o_attn, got1 = unpack(_attn_fwd(za, cos, sin, sinks, comm=comms[2]), comms[2])
    (yg, x0r, x0i), got2 = unpack(_s5_fwd(za, prm, comm=comms[3]), comms[3])
    w_glu, w_ba, w_bs, w_out, w_up, w_down = late(got0, got1, got2)
    zglu = mm(yg, w_glu, mode="nn", out_dtype=BF16, name="mm_glu")
    o_ssm = _rowwise(lambda z1, z2: ((z1 * _sig(z2),), ()), [(zglu, SSM_W, 0), (zglu, SSM_W, 1)], [],
                     [(SSM_W, BF16)], [], tr=512, name="glu")[0]
    ya = mm(o_attn, w_ba, mode="nn", out_dtype=BF16, name="mm_ya")
    ys = mm(o_ssm, w_bs, mode="nn", out_dtype=BF16, name="mm_ys")
    mi = _rowwise(lambda ga, gs, a, s: ((_sig(ga) * a + _sig(gs) * s,), ()),
                  [(zg, D, 0), (zg, D, 1), (ya, D, 0), (ys, D, 0)], [], [(D, BF16)], [], tr=256, name="gate")[0]
    mixed = mm(mi, w_out, mode="nn", out_dtype=F32, name="mm_out")

    def f_post(xv, mv, g2v, g3v):
        x1v = xv + _rms(mv)[0] * g2v
        return (x1v, _rms(x1v)[0] * g3v), ()
    x1, h2 = _rowwise(f_post, [(x, D, 0), (mixed, D, 0)], [g2, g3], [(D, F32), (D, BF16)], [], tr=256, name="post_mix")
    act = mm(h2, w_up, mode="nn", out_dtype=BF16, name="mm_up", epi=lambda v: jnp.maximum(v, 0.0))
    f = mm(act, w_down, mode="nn", out_dtype=F32, name="mm_down", a_fn=lambda v: v * v, tk=4096)

    def f_final(x1v, fv, tv, g4v):
        fn, r = _rms(fv)
        e = x1v + fn * g4v - tv
        dx2v = e * (1.0 / D)
        dfv, dg4v = _rms_bwd(dx2v, fn, r, g4v)
        return (dfv, dx2v), (dg4v, jnp.zeros((SUBLANES, LANES), F32) + 0.5 * jnp.sum(e * e) * (1.0 / D))
    df, dx2, dg4, lossb = _rowwise(f_final, [(x1, D, 0), (f, D, 0), (target, D, 0)], [g4],
                                   [(D, BF16), (D, F32)], [(1, D), (SUBLANES, LANES)], tr=256, name="final")

    big = {}

    def add(k, g4):
        big[k] = g4
        if red is not None:
            red.add(k, g4)

    def hosted(fn, stage, names):
        if red is None:
            return fn(comm=None)
        out, got = fn(comm=getattr(red, stage)(names))
        getattr(red, stage + "_done")(names, got)
        return out

    dpre = mm(df, w_down, mode="nt", out_dtype=BF16, name="mm_dact", epi=lambda v, a: v * (2.0 * a.astype(F32)), extras=(act,))
    wg = functools.partial(_mm, mode="tn", out_dtype=F32, tm=1024, tn=1024, tk=4096)
    add("w_down", wg(act, df, name="wg_down", a_fn=lambda v: v * v).reshape(4, D_FF // 4, D))
    dh2 = hosted(functools.partial(mm, dpre, w_up, mode="nt", out_dtype=F32, name="mm_dh2", tk=4096),
                 "s1", ["w_down"])
    add("w_up", hosted(functools.partial(wg, h2, dpre, name="wg_up", shard_cols=D_FF // 4), "s3", ["w_down"]))

    def f_mid(dx2v, dh2v, x1v, mv, g2v, g3v):
        x1n, r3 = _rms(x1v)
        d3, dg3v = _rms_bwd(dh2v, x1n, r3, g3v)
        dx1v = dx2v + d3
        mn, r2 = _rms(mv)
        dmv, dg2v = _rms_bwd(dx1v, mn, r2, g2v)
        return (dx1v, dmv), (dg3v, dg2v)
    dx1, dmixed, dg3, dg2 = _rowwise(f_mid, [(dx2, D, 0), (dh2, D, 0), (x1, D, 0), (mixed, D, 0)], [g2, g3],
                                     [(D, F32), (D, BF16)], [(1, D), (1, D)], tr=256, name="mid")

    dmi = hosted(functools.partial(mm, dmixed, w_out, mode="nt", out_dtype=BF16, name="mm_dmi"), "s1", ["w_up"])
    add("w_out", wg(mi, dmixed, name="wg_out").reshape(4, D // 4, D))

    def f_gate(dv, ga, gs, a, s):
        sa, ss = _sig(ga), _sig(gs)
        return (dv * sa, dv * ss, jnp.concatenate([dv * a * sa * (1.0 - sa), dv * s * ss * (1.0 - ss)], axis=1)), ()
    dya, dys, dzg = _rowwise(f_gate, [(dmi, D, 0), (zg, D, 0), (zg, D, 1), (ya, D, 0), (ys, D, 0)], [],
                             [(D, BF16), (D, BF16), (2 * D, BF16)], [], tr=256, name="gate_bwd")
    do_attn = hosted(functools.partial(mm, dya, w_ba, mode="nt", out_dtype=BF16, name="mm_doa"), "s1", ["w_out"])
    d_w_ba = wg(o_attn, dya, name="wg_ba")
    do_ssm = mm(dys, w_bs, mode="nt", out_dtype=BF16, name="mm_dos")
    d_w_bs = wg(o_ssm, dys, name="wg_bs")
    add("w_branch", jnp.concatenate([d_w_ba.reshape(2, D // 4, D), d_w_bs.reshape(2, D // 4, D)], axis=0))

    def f_glu(dv, z1, z2):
        s2 = _sig(z2)
        return (jnp.concatenate([dv * s2, dv * z1 * s2 * (1.0 - s2)], axis=1),), ()
    dzglu = _rowwise(f_glu, [(do_ssm, SSM_W, 0), (zglu, SSM_W, 0), (zglu, SSM_W, 1)], [], [(2 * SSM_W, BF16)], [],
                     tr=512, name="glu_bwd")[0]
    dyg = hosted(functools.partial(mm, dzglu, w_glu, mode="nt", out_dtype=F32, name="mm_dyg"), "s1", ["w_branch"])
    add("w_glu", wg(yg, dzglu, name="wg_glu", tn=SSM_W // 2, shard_cols=SSM_W // 2))
    du, dar, dai, dbc, dcc, ddv = hosted(functools.partial(_s5_bwd, za, dyg, x0r, x0i, prm),
                                         "s3", ["w_up", "w_out", "w_branch"])
    dbr, dbi = dbc[:, :, :4 * LANES], dbc[:, :, 4 * LANES:]
    dcc = dcc.transpose(0, 2, 1)
    dcr, dci = dcc[:, :4 * LANES, :], -dcc[:, 4 * LANES:, :]
    dq, dkv, dsk = _attn_bwd(za, cos, sin, sinks, o_attn, do_attn)
    dza = jnp.concatenate([dq, dkv, du], axis=1)
    d_w_a = _mm(h, dza, mode="tn", out_dtype=F32, tm=1024, tn=ZA_W // 2, tk=2048, name="wg_a")
    d_w_g = wg(h, dzg, name="wg_g")
    add("w_in", _stack_w_in_grad(d_w_a, d_w_g))
    dh = hosted(functools.partial(mm, dza, w_a, mode="nt", out_dtype=F32, name="mm_dh_a", tk=ZA_W), "s1", ["w_in", "w_glu"])
    dh = hosted(functools.partial(mm, dzg, w_g, mode="nt", out_dtype=F32, name="mm_dh_g",
                                  epi=lambda v, p: v + p, extras=(dh,)), "s3", ["w_in", "w_glu"])

    def f_first(dx1v, dhv, xv, g1v):
        xn, r1 = _rms(xv)
        d1, dg1v = _rms_bwd(dhv, xn, r1, g1v)
        return (dx1v + d1,), (dg1v,)
    dx, dg1 = _rowwise(f_first, [(dx1, D, 0), (dh, D, 0), (x, D, 0)], [g1], [(D, F32)], [(1, D)], tr=256, name="first")

    da_re = dar.sum(axis=1).reshape(SSM_G, SSM_P)
    da_im = dai.sum(axis=1).reshape(SSM_G, SSM_P)
    d_lam_re, d_lam_im, d_log_dt, d_b_re, d_b_im = disc_vjp(
        (da_re, da_im, _blockdiag_in_extract(dbr), _blockdiag_in_extract(dbi)))
    small = dict(norm_mix_pre=dg1, norm_mix_post=dg2, norm_mlp_pre=dg3, norm_mlp_post=dg4,
                 sinks=dsk[:, :N_Q_HEADS], lam_re=d_lam_re, lam_im=d_lam_im, log_dt=d_log_dt,
                 b_re=d_b_re, b_im=d_b_im, c_re=_blockdiag_out_extract(dcr), c_im=_blockdiag_out_extract(dci),
                 d_skip=ddv.reshape(SSM_G, SSM_GC))
    return lossb[0, 0], dx, small, big


def _cast_into_slot(w, k_arr):
    rows, cols = w.shape
    tr = 256

    def body(k_ref, w_ref, o_ref):
        o_ref[0] = w_ref[...].astype(BF16)

    return pl.pallas_call(
        body,
        name="cast_into_slot",
        grid_spec=pltpu.PrefetchScalarGridSpec(
            num_scalar_prefetch=1,
            grid=(rows // tr,),
            in_specs=[pl.BlockSpec((tr, cols), lambda i, k: (i, 0))],
            out_specs=pl.BlockSpec((1, tr, cols), lambda i, k: (k[0], i, 0)),
        ),
        out_shape=SDS((4, rows, cols), BF16),
        compiler_params=_cp(("parallel",)),
    )(k_arr, w)


def _pair_sum(g, r, c_arr):
    _, _, hr, cols = g.shape
    tr = min(256, hr)

    def body(c_ref, g_ref, r_ref, o_ref):
        o_ref[0] = (g_ref[0, 0] + r_ref[0]).astype(BF16)

    return pl.pallas_call(
        body,
        name="pair_sum",
        grid_spec=pltpu.PrefetchScalarGridSpec(
            num_scalar_prefetch=1,
            grid=(3, hr // tr),
            in_specs=[pl.BlockSpec((1, 1, tr, cols), lambda k, i, c_ref: (c_ref[1 + k], c_ref[0], i, 0)),
                      pl.BlockSpec((1, tr, cols), lambda k, i, c_ref: (c_ref[1 + k], i, 0))],
            out_specs=pl.BlockSpec((1, tr, cols), lambda k, i, c_ref: (c_ref[1 + k], i, 0)),
        ),
        out_shape=SDS((4, hr, cols), BF16),
        compiler_params=_cp(("parallel", "parallel")),
    )(c_arr, g, r)


def _chip_sum(g, r, q, kc_arr):
    _, _, hr, cols = g.shape
    tr = min(256, hr)

    def body(kc_ref, g_ref, r_ref, q_ref, o_ref):
        s = g_ref[0, 0] + r_ref[0]
        for j in range(3):
            s = s + q_ref[j].astype(F32)
        o_ref[...] = s

    return pl.pallas_call(
        body,
        name="chip_sum",
        grid_spec=pltpu.PrefetchScalarGridSpec(
            num_scalar_prefetch=1,
            grid=(hr // tr,),
            in_specs=[pl.BlockSpec((1, 1, tr, cols), lambda i, kc: (kc[0], kc[1], i, 0)),
                      pl.BlockSpec((1, tr, cols), lambda i, kc: (kc[0], i, 0)),
                      pl.BlockSpec((3, tr, cols), lambda i, kc: (0, i, 0))],
            out_specs=pl.BlockSpec((tr, cols), lambda i, kc: (kc[1] * (hr // tr) + i, 0)),
        ),
        out_shape=SDS((2 * hr, cols), F32),
        compiler_params=_cp(("parallel",)),
    )(kc_arr, g, r, q)


def _pair_share(blocks):
    n = len(blocks)

    def body(*refs):
        ins, outs = refs[:n], refs[n:2 * n]
        ssem, rsem = refs[2 * n:]
        x, y, c, _ = _place()
        cps = []
        for w in range(n):
            hr = ins[w].shape[0] // 2
            rows = pl.ds(pl.multiple_of(c * hr, 8), hr)
            cp = _remote(ins[w].at[rows, :], outs[w].at[rows, :], ssem.at[w], rsem.at[w], (x, y, 1 - c))
            cp.start()
            cps.append(cp)
        for w in range(n):
            hr = ins[w].shape[0] // 2
            other = outs[w].at[pl.ds(pl.multiple_of((1 - c) * hr, 8), hr), :]
            _remote(other, other, ssem.at[w], rsem.at[w], (x, y, 1 - c)).wait_recv()
        for cp in cps:
            cp.wait_send()

    dma = pltpu.SemaphoreType.DMA
    return pl.pallas_call(
        body,
        name="pair_share",
        in_specs=[ANY] * n,
        out_specs=[ANY] * n,
        out_shape=[SDS(b.shape, b.dtype) for b in blocks],
        input_output_aliases={w: w for w in range(n)},
        scratch_shapes=[dma((n,)), dma((n,))],
    )(*blocks)


class _GradReducer:
    def __init__(self, c_arr, kc_arr):
        self.c_arr, self.kc_arr = c_arr, kc_arr
        self.g, self.r, self.ps, self.q = {}, {}, {}, {}

    def add(self, k, g4):
        self.g[k] = g4.reshape(4, 2, g4.shape[1] // 2, g4.shape[2])

    def s1(self, names):
        return _PairExchangeComm([self.g[k].reshape(4, -1, self.g[k].shape[3]) for k in names])

    def s1_done(self, names, got):
        for k, r in zip(names, got):
            self.r[k] = r
            self.ps[k] = _pair_sum(self.g[k], r, self.c_arr)

    def s3(self, names):
        return _ChipExchangeComm([self.ps[k] for k in names])

    def s3_done(self, names, got):
        self.q.update(zip(names, got))

    def finish(self, order):
        rest = [k for k in order if k not in self.r]
        if rest:
            self.s1_done(rest, _comm_only("pair_exchange", self.s1(rest)))
        rest = [k for k in order if k not in self.q]
        if rest:
            self.s3_done(rest, _comm_only("chip_exchange", self.s3(rest)))
        blocks = [_chip_sum(self.g[k], self.r[k], self.q[k], self.kc_arr) for k in order]
        return dict(zip(order, _pair_share(blocks)))


def _all_reduce_small(buf):
    rows = buf.shape[0]
    hr = rows // 2
    assert hr % SUBLANES == 0

    def body(in_ref, o_ref, sib, pair, slots, ssem, rsem):
        x, y, c, others = _place()
        me, sibling = 2 * x + y, (x, y, 1 - c)
        mine = pl.ds(pl.multiple_of(c * hr, SUBLANES), hr)
        theirs = pl.ds(pl.multiple_of((1 - c) * hr, SUBLANES), hr)
        first = _remote(in_ref, sib, ssem.at[0], rsem.at[0], sibling)
        first.start()
        first.wait()
        pair[...] = in_ref[...] + sib[...]
        slots[me] = pair[mine, :]
        cps = [_remote(pair.at[mine, :], slots.at[me], ssem.at[1 + r], rsem.at[1 + r], (ox, oy, c))
               for r, (ox, oy) in enumerate(others)]
        for cp in cps:
            cp.start()
        for r, (ox, oy) in enumerate(others):
            _remote(pair.at[mine, :], slots.at[2 * ox + oy], ssem.at[1 + r], rsem.at[1 + r], (ox, oy, c)).wait_recv()
        o_ref[mine, :] = (slots[0] + slots[1]) + (slots[2] + slots[3])
        last = _remote(o_ref.at[mine, :], o_ref.at[mine, :], ssem.at[4], rsem.at[4], sibling)
        last.start()
        _remote(o_ref.at[theirs, :], o_ref.at[theirs, :], ssem.at[4], rsem.at[4], sibling).wait_recv()
        last.wait_send()
        for cp in cps:
            cp.wait_send()

    dma = pltpu.SemaphoreType.DMA
    return pl.pallas_call(
        body,
        name="all_reduce_small",
        in_specs=[pl.BlockSpec(memory_space=pltpu.VMEM)],
        out_specs=pl.BlockSpec(memory_space=pltpu.VMEM),
        out_shape=SDS(buf.shape, F32),
        scratch_shapes=[pltpu.VMEM((rows, LANES), F32), pltpu.VMEM((rows, LANES), F32),
                        pltpu.VMEM((4, hr, LANES), F32), dma((5,)), dma((5,))],
        compiler_params=pltpu.CompilerParams(vmem_limit_bytes=VMEM_LIMIT),
    )(buf)


def _adam_fn(w, g, m, v):
    m2 = ADAM_B1 * m + (1.0 - ADAM_B1) * g
    v2 = ADAM_B2 * v + (1.0 - ADAM_B2) * (g * g)
    m_hat = m2 / (1.0 - ADAM_B1 ** ADAM_STEP)
    v_hat = v2 / (1.0 - ADAM_B2 ** ADAM_STEP)
    return (-ADAM_LR * (m_hat / (jnp.sqrt(v_hat) + ADAM_EPS) + ADAM_WD * w), m2, v2), ()


def _adamw(w, g, m, v, name, tr=256):
    cols = w.shape[1]
    return _rowwise(_adam_fn, [(w, cols, 0), (g, cols, 0), (m, cols, 0), (v, cols, 0)], [],
                    [(cols, F32)] * 3, [], tr=tr, name=name)


BIG = ("w_in", "w_glu", "w_branch", "w_out", "w_up", "w_down")
COL_SHARDED = ("w_in", "w_glu", "w_up")
SMALL = ("norm_mix_pre", "norm_mix_post", "norm_mlp_pre", "norm_mlp_post", "sinks", "lam_re", "lam_im", "log_dt",
         "b_re", "b_im", "c_re", "c_im", "d_skip")
WEIGHTS = ("norm_mix_pre", "norm_mix_post", "norm_mlp_pre", "norm_mlp_post", "w_in", "sinks", "lam_re", "lam_im",
           "log_dt", "b_re", "b_im", "c_re", "c_im", "d_skip", "w_glu", "w_branch", "w_out", "w_up", "w_down")


def _flat_small(vals, extra):
    flat = jnp.concatenate([vals[k].reshape(-1) for k in SMALL] + [extra.reshape(-1)])
    rows = -(-flat.shape[0] // (SUBLANES * LANES)) * SUBLANES
    return jnp.pad(flat, (0, rows * LANES - flat.shape[0])).reshape(rows, LANES)


def kernel(x, norm_mix_pre, norm_mix_post, norm_mlp_pre, norm_mlp_post, w_in, sinks, lam_re, lam_im, log_dt, b_re, b_im, c_re, c_im, d_skip, w_glu, w_branch, w_out, w_up, w_down, loss_target, m_norm_mix_pre, m_norm_mix_post, m_norm_mlp_pre, m_norm_mlp_post, m_w_in, m_sinks, m_lam_re, m_lam_im, m_log_dt, m_b_re, m_b_im, m_c_re, m_c_im, m_d_skip, m_w_glu, m_w_branch, m_w_out, m_w_up, m_w_down, v_norm_mix_pre, v_norm_mix_post, v_norm_mlp_pre, v_norm_mlp_post, v_w_in, v_sinks, v_lam_re, v_lam_im, v_log_dt, v_b_re, v_b_im, v_c_re, v_c_im, v_d_skip, v_w_glu, v_w_branch, v_w_out, v_w_up, v_w_down):
    w = dict(norm_mix_pre=norm_mix_pre, norm_mix_post=norm_mix_post, norm_mlp_pre=norm_mlp_pre, norm_mlp_post=norm_mlp_post,
             w_in=w_in, sinks=sinks, lam_re=lam_re, lam_im=lam_im, log_dt=log_dt, b_re=b_re, b_im=b_im, c_re=c_re,
             c_im=c_im, d_skip=d_skip, w_glu=w_glu, w_branch=w_branch, w_out=w_out, w_up=w_up, w_down=w_down)
    m = dict(norm_mix_pre=m_norm_mix_pre, norm_mix_post=m_norm_mix_post, norm_mlp_pre=m_norm_mlp_pre,
             norm_mlp_post=m_norm_mlp_post, w_in=m_w_in, sinks=m_sinks, lam_re=m_lam_re, lam_im=m_lam_im,
             log_dt=m_log_dt, b_re=m_b_re, b_im=m_b_im, c_re=m_c_re, c_im=m_c_im, d_skip=m_d_skip, w_glu=m_w_glu,
             w_branch=m_w_branch, w_out=m_w_out, w_up=m_w_up, w_down=m_w_down)
    v = dict(norm_mix_pre=v_norm_mix_pre, norm_mix_post=v_norm_mix_post, norm_mlp_pre=v_norm_mlp_pre,
             norm_mlp_post=v_norm_mlp_post, w_in=v_w_in, sinks=v_sinks, lam_re=v_lam_re, lam_im=v_lam_im,
             log_dt=v_log_dt, b_re=v_b_re, b_im=v_b_im, c_re=v_c_re, c_im=v_c_im, d_skip=v_d_skip, w_glu=v_w_glu,
             w_branch=v_w_branch, w_out=v_w_out, w_up=v_w_up, w_down=v_w_down)
    xi, yi, ci = lax.axis_index("x"), lax.axis_index("y"), lax.axis_index("c")

    k_arr = jnp.stack([2 * xi + yi]).astype(jnp.int32)
    slot = {k: _cast_into_slot(w[k][0], k_arr) for k in BIG}

    def whole(k, g4):
        if k in COL_SHARDED:
            return jnp.concatenate([g4[j] for j in range(4)], axis=1)
        return g4.reshape(4 * g4.shape[1], g4.shape[2])

    wi = _comm_only("gather_w_in_a", _GatherComm([slot["w_in"]], sender_x=0))[0]
    w_a = _assemble_w_a(wi)
    hosted = (("w_glu", "w_branch", "w_out"), ("w_up",), ("w_down",))
    comms = [_GatherComm([wi], sender_x=1)] + [_GatherComm([slot[k] for k in names]) for names in hosted]

    def late(*got):
        f = {k: whole(k, g4) for names, res in zip(hosted, got) for k, g4 in zip(names, res)}
        return f["w_glu"], f["w_branch"][:Q_W], f["w_branch"][Q_W:], f["w_out"], f["w_up"], f["w_down"]

    s5w = (lam_re[0], lam_im[0], log_dt[0], b_re[0], b_im[0], c_re[0], c_im[0], d_skip[0])
    reducer = _GradReducer(
        jnp.stack([ci, 2 * (1 - xi) + yi, 2 * xi + (1 - yi), 2 * (1 - xi) + (1 - yi)]).astype(jnp.int32),
        jnp.stack([2 * xi + yi, ci]).astype(jnp.int32))
    loss_part, dx, small, _ = _local_step(
        x[0], loss_target[0], (norm_mix_pre, norm_mix_post, norm_mlp_pre, norm_mlp_post),
        w_a, sinks, s5w, comms, lambda got: _assemble_w_g(got[0]), late, reducer)
    grads = reducer.finish(BIG)

    red = _all_reduce_small(_flat_small(small, loss_part)).reshape(-1)
    off = 0
    for k in SMALL:
        n = math.prod(w[k].shape)
        grads[k] = red[off:off + n].reshape(w[k].shape[1:])
        off += n
    loss = red[off]

    delta, new_m, new_v = {}, {}, {}
    for k in BIG:
        delta[k], new_m[k], new_v[k] = _adamw(w[k][0], grads[k], m[k][0], v[k][0], "adamw_" + k)
    zero = jnp.zeros((), F32)
    fw, fm, fv = (_flat_small({k: t[k] for k in SMALL}, zero) for t in (w, m, v))
    fg = _flat_small(grads, zero)
    sd, sm, sv = _adamw(fw, fg, fm, fv, "adamw_small", tr=fw.shape[0])
    off = 0
    for k in SMALL:
        n = math.prod(w[k].shape)
        delta[k], new_m[k], new_v[k] = (t.reshape(-1)[off:off + n].reshape(w[k].shape[1:]) for t in (sd, sm, sv))
        off += n

    lead = lambda t: t[None]
    return (loss, lead(dx), *[lead(grads[k]) for k in WEIGHTS], *[lead(delta[k]) for k in WEIGHTS],
            *[lead(new_m[k]) for k in WEIGHTS], *[lead(new_v[k]) for k in WEIGHTS])
```

```python
import functools
import math

import jax
import jax.numpy as jnp
from jax import lax
from jax.experimental import pallas as pl
from jax.experimental.pallas import tpu as pltpu

F32 = jnp.float32
BF16 = jnp.bfloat16
SDS = jax.ShapeDtypeStruct

D_MODEL = 2048
HEAD_DIM = 64
N_Q_HEADS = 16
ATT_BLOCK = 128
ROT_DIM = 16
ROPE_THETA = 500000.0
Q_W = 1024
KV_W = 128
SSM_W = 1024
SSM_G = 64
SSM_GC = 16
SSM_P = 64
N_STATE = SSM_G * SSM_P
LANES = 128
SUBLANES = 8
N_LG = N_STATE // LANES
N_JB = 8
LG_PER_JB = N_LG // N_JB
D_FF = 8192
ZA_W = Q_W + 2 * KV_W + SSM_W
EPS = 1e-6
S5_CHUNK = 1024
S5_SEG = S5_CHUNK // SUBLANES
VMEM_LIMIT = 56 * 1024 * 1024
NEG = -1e30

ADAM_LR = 0.001
ADAM_B1 = 0.9
ADAM_B2 = 0.999
ADAM_EPS = 1e-08
ADAM_WD = 0.01
ADAM_STEP = 10

MESH = pl.DeviceIdType.MESH


def _cp(sem):
    return pltpu.CompilerParams(dimension_semantics=sem, vmem_limit_bytes=VMEM_LIMIT)


ANY = pl.BlockSpec(memory_space=pl.ANY)


def _place():
    x, y, c = lax.axis_index("x"), lax.axis_index("y"), lax.axis_index("c")
    others = [(1 - x, y), (x, 1 - y), (1 - x, 1 - y)]
    return x, y, c, others


def _remote(src, dst, ssem, rsem, to):
    return pltpu.make_async_remote_copy(src_ref=src, dst_ref=dst, send_sem=ssem, recv_sem=rsem,
                                        device_id=to, device_id_type=MESH)


class _GatherComm:
    aliased = True

    def __init__(self, slotted):
        self.arrs = list(slotted)
        self.n = len(self.arrs)
        dma = pltpu.SemaphoreType.DMA
        self.scratch = [dma((3 * self.n,)) for _ in range(4)]
        self.out_shape = [SDS(s.shape, s.dtype) for s in self.arrs]

    @staticmethod
    def _half(ref, hc):
        hr = ref.shape[1] // 2
        return pl.ds(pl.multiple_of(hc * hr, 16), hr)

    def _sends(self, ins, outs, sems):
        ssem, rsem, _, _ = sems
        x, y, c, others = _place()
        me = 2 * x + y
        return [_remote(ins[w].at[me, self._half(ins[w], c), :], outs[w].at[me, self._half(ins[w], c), :],
                        ssem.at[3 * w + r], rsem.at[3 * w + r], (ox, oy, c))
                for w in range(self.n) for r, (ox, oy) in enumerate(others)]

    def start(self, ins, outs, sems):
        for cp in self._sends(ins, outs, sems):
            cp.start()

    def finish(self, ins, outs, sems):
        ssem, rsem, fs_sem, fr_sem = sems
        x, y, c, others = _place()
        sib = (x, y, 1 - c)
        passes = []
        for w in range(self.n):
            for r, (ox, oy) in enumerate(others):
                got = outs[w].at[2 * ox + oy, self._half(ins[w], c), :]
                _remote(got, got, ssem.at[3 * w + r], rsem.at[3 * w + r], (ox, oy, c)).wait_recv()
                cp = _remote(got, got, fs_sem.at[3 * w + r], fr_sem.at[3 * w + r], sib)
                cp.start()
                passes.append(cp)
        for w in range(self.n):
            for r, (ox, oy) in enumerate(others):
                got = outs[w].at[2 * ox + oy, self._half(ins[w], 1 - c), :]
                _remote(got, got, fs_sem.at[3 * w + r], fr_sem.at[3 * w + r], sib).wait_recv()
        for cp in self._sends(ins, outs, sems) + passes:
            cp.wait_send()


class _PairExchangeComm:
    aliased = False

    def __init__(self, grads):
        self.arrs = list(grads)
        self.n = len(self.arrs)
        dma = pltpu.SemaphoreType.DMA
        self.scratch = [dma((self.n,)), dma((self.n,))]
        self.out_shape = [SDS((4, g.shape[1] // 2, g.shape[2]), g.dtype) for g in self.arrs]

    def _copies(self, ins, outs, sems):
        ssem, rsem = sems
        x, y, c, _ = _place()
        cps = []
        for w in range(self.n):
            hr = ins[w].shape[1] // 2
            src = ins[w].at[:, pl.ds(pl.multiple_of((1 - c) * hr, 8), hr), :]
            cps.append(_remote(src, outs[w], ssem.at[w], rsem.at[w], (x, y, 1 - c)))
        return cps

    def start(self, ins, outs, sems):
        for cp in self._copies(ins, outs, sems):
            cp.start()

    def finish(self, ins, outs, sems):
        for cp in self._copies(ins, outs, sems):
            cp.wait()


class _ChipExchangeComm:
    aliased = False

    def __init__(self, psums):
        self.arrs = list(psums)
        self.n = len(self.arrs)
        dma = pltpu.SemaphoreType.DMA
        self.scratch = [dma((3 * self.n,)), dma((3 * self.n,))]
        self.out_shape = [SDS((3,) + p.shape[1:], p.dtype) for p in self.arrs]

    def _copies(self, ins, outs, sems):
        ssem, rsem = sems
        x, y, c, others = _place()
        return [_remote(ins[w].at[2 * ox + oy], outs[w].at[r], ssem.at[3 * w + r], rsem.at[3 * w + r], (ox, oy, c))
                for w in range(self.n) for r, (ox, oy) in enumerate(others)]

    def start(self, ins, outs, sems):
        for cp in self._copies(ins, outs, sems):
            cp.start()

    def finish(self, ins, outs, sems):
        for cp in self._copies(ins, outs, sems):
            cp.wait()


def _comm_only(name, comm):
    n = comm.n

    def body(*refs):
        ins, outs, sems = refs[:n], refs[n:2 * n], refs[2 * n:]
        comm.start(ins, outs, sems)
        comm.finish(ins, outs, sems)

    return pl.pallas_call(
        body, name=name, in_specs=[ANY] * n, out_specs=[ANY] * n, out_shape=comm.out_shape,
        input_output_aliases={w: w for w in range(n)} if comm.aliased else {},
        scratch_shapes=comm.scratch)(*comm.arrs)


def _call(name, body, grid, in_specs, out_specs, out_shape, scratch, dims, args, comm=None):
    if comm is None:
        return pl.pallas_call(body, name=name, grid=grid, in_specs=in_specs, out_specs=out_specs, out_shape=out_shape,
                              scratch_shapes=scratch, compiler_params=_cp(dims))(*args)
    ni, no, ns, n = len(in_specs), len(out_shape), len(scratch), comm.n

    def hosted(*refs):
        ins, cin = refs[:ni], refs[ni:ni + n]
        outs, cout = refs[ni + n:ni + n + no], refs[ni + n + no:ni + 2 * n + no]
        scr, sems = refs[ni + 2 * n + no:ni + 2 * n + no + ns], refs[ni + 2 * n + no + ns:]
        ids = [pl.program_id(d) for d in range(len(grid))]
        first = functools.reduce(jnp.logical_and, [i == 0 for i in ids])
        last = functools.reduce(jnp.logical_and, [i == g - 1 for i, g in zip(ids, grid)])

        @pl.when(first)
        def _():
            comm.start(cin, cout, sems)

        body(*ins, *outs, *scr)

        @pl.when(last)
        def _():
            comm.finish(cin, cout, sems)

    return pl.pallas_call(
        hosted, name=name, grid=grid, in_specs=list(in_specs) + [ANY] * n, out_specs=list(out_specs) + [ANY] * n,
        out_shape=list(out_shape) + comm.out_shape,
        input_output_aliases={ni + w: no + w for w in range(n)} if comm.aliased else {},
        scratch_shapes=list(scratch) + comm.scratch, compiler_params=_cp(("arbitrary",) * len(grid)))(*args, *comm.arrs)


def _mm(a, b, *, mode, out_dtype, tm, tn, tk, name, a_fn=None, epi=None, extras=(), comm=None, shard_cols=None):
    if mode == "nn":
        (M, K), (K2, N) = a.shape, b.shape
    elif mode == "nt":
        (M, K), (N, K2) = a.shape, b.shape
    else:
        (K, M), (K2, N) = a.shape, b.shape
    assert K == K2, (a.shape, b.shape, mode)
    tm, tn, tk = min(tm, M), min(tn, N), min(tk, K)
    assert M % tm == 0 and N % tn == 0 and K % tk == 0, (M, N, K, tm, tn, tk)
    nk = K // tk
    if mode == "tn":
        a_spec = pl.BlockSpec((tk, tm), lambda i, j, k: (k, i))
        ca = 0
    else:
        a_spec = pl.BlockSpec((tm, tk), lambda i, j, k: (i, k))
        ca = 1
    if mode == "nt":
        b_spec = pl.BlockSpec((tn, tk), lambda i, j, k: (j, k))
        cb = 1
    else:
        b_spec = pl.BlockSpec((tk, tn), lambda i, j, k: (k, j))
        cb = 0
    dims = (((ca,), (cb,)), ((), ()))
    ne = len(extras)

    def body(a_ref, b_ref, *rest):
        ex = rest[:ne]
        o_ref = rest[ne]
        av = a_ref[...]
        if a_fn is not None:
            av = a_fn(av.astype(F32))
        p = lax.dot_general(av.astype(BF16), b_ref[...].astype(BF16), dims, preferred_element_type=F32)

        def fin(v):
            if epi is not None:
                v = epi(v, *[e[...] for e in ex])
            o_ref[...] = v.astype(out_dtype).reshape(o_ref.shape)

        if nk == 1:
            fin(p)
        else:
            acc = rest[ne + 1]
            k = pl.program_id(2)

            @pl.when(k == 0)
            def _():
                acc[...] = p

            @pl.when(k > 0)
            def _():
                acc[...] += p

            @pl.when(k == nk - 1)
            def _():
                fin(acc[...])

    if shard_cols is None:
        o_spec, o_shape = pl.BlockSpec((tm, tn), lambda i, j, k: (i, j)), SDS((M, N), out_dtype)
    else:
        per = shard_cols // tn
        assert shard_cols % tn == 0 and N % shard_cols == 0
        o_spec = pl.BlockSpec((1, tm, tn), lambda i, j, k: (lax.div(j, per), i, lax.rem(j, per)))
        o_shape = SDS((N // shard_cols, M, shard_cols), out_dtype)
    res = _call(name, body, (M // tm, N // tn, nk),
                [a_spec, b_spec] + [pl.BlockSpec((tm, tn), lambda i, j, k: (i, j)) for _ in extras],
                [o_spec], [o_shape],
                [pltpu.VMEM((tm, tn), F32)] if nk > 1 else [], ("parallel", "parallel", "arbitrary"),
                (a, b, *extras), comm)
    return res[0] if comm is None else (res[0], res[1:])


def _rowwise(fn, rows, bcasts, outs, accs, *, tr, name):
    T = rows[0][0].shape[0]
    tr = min(tr, T)
    assert T % tr == 0
    nr, nb, no, na = len(rows), len(bcasts), len(outs), len(accs)
    in_specs = [pl.BlockSpec((tr, w), functools.partial(lambda i, c: (i, c), c=cb)) for (_, w, cb) in rows]
    in_specs += [pl.BlockSpec(b.shape, lambda i: (0, 0)) for b in bcasts]
    out_shape = [SDS((T, w), dt) for (w, dt) in outs] + [SDS(s, F32) for s in accs]
    out_specs = [pl.BlockSpec((tr, w), lambda i: (i, 0)) for (w, _) in outs]
    out_specs += [pl.BlockSpec(s, lambda i: (0, 0)) for s in accs]

    def body(*refs):
        ins = [r[...].astype(F32) for r in refs[:nr + nb]]
        o_refs = refs[nr + nb:nr + nb + no]
        a_refs = refs[nr + nb + no:]
        ro, ao = fn(*ins)
        for r, v in zip(o_refs, ro):
            r[...] = v.astype(r.dtype)
        if na:
            @pl.when(pl.program_id(0) == 0)
            def _():
                for r in a_refs:
                    r[...] = jnp.zeros(r.shape, F32)

            for r, v in zip(a_refs, ao):
                r[...] += v

    res = pl.pallas_call(
        body,
        name=name,
        grid=(T // tr,),
        in_specs=in_specs,
        out_specs=out_specs,
        out_shape=out_shape,
        compiler_params=_cp(("arbitrary",) if na else ("parallel",)),
    )(*[r[0] for r in rows], *bcasts)
    return res


def _rms(v):
    r = lax.rsqrt(jnp.mean(v * v, axis=-1, keepdims=True) + EPS)
    return v * r, r


def _rms_bwd(dy, xn, r, g):
    dxn = dy * g
    dv = r * (dxn - xn * jnp.mean(dxn * xn, axis=-1, keepdims=True))
    return dv, jnp.sum(dy * xn, axis=0, keepdims=True)


def _sig(v):
    return 1.0 / (1.0 + jnp.exp(-v))


_GELU_C = math.sqrt(2.0 / math.pi)


def _gelu(v):
    return 0.5 * v * (1.0 + jnp.tanh(_GELU_C * (v + 0.044715 * v * v * v)))


def _gelu_grad(v):
    t = jnp.tanh(_GELU_C * (v + 0.044715 * v * v * v))
    return 0.5 * (1.0 + t) + 0.5 * v * (1.0 - t * t) * _GELU_C * (1.0 + 3.0 * 0.044715 * v * v)


def _rope(v, c, s, sign):
    w = v.shape[1]
    m = lax.broadcasted_iota(jnp.int32, v.shape, 1) % HEAD_DIM
    p = jnp.where(m < ROT_DIM // 2, -pltpu.roll(v, w - ROT_DIM // 2, 1), pltpu.roll(v, ROT_DIM // 2, 1))
    return v * c + sign * (p * s)


def _rope_tables(T):
    half = ROT_DIM // 2
    inv = ROPE_THETA ** (-jnp.arange(half, dtype=F32) * 2.0 / ROT_DIM)
    ang = jnp.arange(T).astype(F32)[:, None] * inv[None, :]
    cos, sin = jnp.cos(ang), jnp.sin(ang)
    one = jnp.ones((T, HEAD_DIM - ROT_DIM), F32)
    c64 = jnp.concatenate([cos, cos, one], axis=1)
    s64 = jnp.concatenate([sin, sin, 0.0 * one], axis=1)
    return jnp.tile(c64, (1, 2)), jnp.tile(s64, (1, 2))


def _dup_half(m, lo):
    lane = lax.broadcasted_iota(jnp.int32, m.shape, 1)
    sw = pltpu.roll(m, HEAD_DIM, 1)
    return jnp.where(lane < HEAD_DIM, m, sw) if lo else jnp.where(lane >= HEAD_DIM, m, sw)


def _attn_mask(i):
    qi = lax.broadcasted_iota(jnp.int32, (ATT_BLOCK, 2 * ATT_BLOCK), 0)
    kj = lax.broadcasted_iota(jnp.int32, (ATT_BLOCK, 2 * ATT_BLOCK), 1)
    rel = qi + ATT_BLOCK - kj
    return (rel >= 0) & (rel < ATT_BLOCK) & ((kj >= ATT_BLOCK) | (i > 0))


_NT = (((1,), (1,)), ((), ()))
_TN = (((0,), (0,)), ((), ()))


def _stack_heads(m):
    lane = lax.broadcasted_iota(jnp.int32, m.shape, 1)
    zero = jnp.zeros_like(m)
    return jnp.concatenate([jnp.where(lane < HEAD_DIM, m, zero), jnp.where(lane >= HEAD_DIM, m, zero)], axis=0)


def _pair_probs(q2, k2, ok2, sink_lo, sink_hi):
    qs = _stack_heads(q2)
    s = lax.dot_general(qs, k2, _NT, preferred_element_type=F32)
    s = jnp.where(ok2, s, NEG)
    row = lax.broadcasted_iota(jnp.int32, (2 * ATT_BLOCK, 1), 0)
    sink = jnp.where(row < ATT_BLOCK, sink_lo, sink_hi)
    m = jnp.maximum(jnp.max(s, axis=1, keepdims=True), sink)
    e = jnp.exp(s - m)
    es = jnp.exp(sink - m)
    inv = 1.0 / (jnp.sum(e, axis=1, keepdims=True) + es)
    return e * inv, es * inv, qs


def _attn_fwd(za, cos, sin, sinks, comm=None):
    T = za.shape[0]
    nb = T // ATT_BLOCK
    kvb = Q_W // (2 * KV_W)

    def body(sink_ref, q_ref, kvp_ref, kvc_ref, cc_ref, sc_ref, cp_ref, sp_ref, o_ref):
        i = pl.program_id(0)
        cc, sc, cp, sp = cc_ref[...], sc_ref[...], cp_ref[...], sp_ref[...]
        q = (_rope(q_ref[...], jnp.tile(cc, (1, 8)), jnp.tile(sc, (1, 8)), 1.0) * 0.125).astype(BF16)
        kvp, kvc = kvp_ref[...], kvc_ref[...]
        k = jnp.concatenate([_rope(kvp[:, :KV_W], cp, sp, 1.0), _rope(kvc[:, :KV_W], cc, sc, 1.0)], axis=0).astype(BF16)
        v = jnp.concatenate([kvp[:, KV_W:], kvc[:, KV_W:]], axis=0).astype(BF16)
        ok = _attn_mask(i)
        ok2 = jnp.concatenate([ok, ok], axis=0)
        lane = lax.broadcasted_iota(jnp.int32, (ATT_BLOCK, LANES), 1)
        for kvh in range(2):
            k2 = _dup_half(k, kvh == 0)
            v2 = _dup_half(v, kvh == 0)
            for pair in range(4):
                c0 = (kvh * 4 + pair) * LANES
                q2 = q[:, c0:c0 + LANES]
                p, _, _ = _pair_probs(q2, k2, ok2, sink_ref[0, 2 * (kvh * 4 + pair)], sink_ref[0, 2 * (kvh * 4 + pair) + 1])
                o = jnp.dot(p.astype(BF16), v2, preferred_element_type=F32)
                o_ref[:, c0:c0 + LANES] = jnp.where(lane < HEAD_DIM, o[:ATT_BLOCK], o[ATT_BLOCK:]).astype(BF16)

    blk = lambda w, f: pl.BlockSpec((ATT_BLOCK, w), f)
    res = _call(
        "attn_fwd", body, (nb,),
        [
            pl.BlockSpec(memory_space=pltpu.SMEM),
            blk(Q_W, lambda i: (i, 0)),
            blk(2 * KV_W, lambda i: (jnp.maximum(i - 1, 0), kvb)),
            blk(2 * KV_W, lambda i: (i, kvb)),
            blk(LANES, lambda i: (i, 0)),
            blk(LANES, lambda i: (i, 0)),
            blk(LANES, lambda i: (jnp.maximum(i - 1, 0), 0)),
            blk(LANES, lambda i: (jnp.maximum(i - 1, 0), 0)),
        ],
        [blk(Q_W, lambda i: (i, 0))], [SDS((T, Q_W), BF16)], [], ("parallel",),
        (sinks, za, za, za, cos, sin, cos, sin), comm)
    return res[0] if comm is None else (res[0], res[1:])


def _attn_bwd(za, cos, sin, sinks, o, do):
    T = za.shape[0]
    nb = T // ATT_BLOCK
    kvb = Q_W // (2 * KV_W)

    def body(sink_ref, q_ref, kvp_ref, kvc_ref, cc_ref, sc_ref, cp_ref, sp_ref, o_ref, do_ref,
             dq_ref, dkv_ref, dsk_ref, carry, dqs):
        i = pl.program_id(0)

        @pl.when(i == 0)
        def _():
            carry[...] = jnp.zeros(carry.shape, F32)
            dsk_ref[...] = jnp.zeros(dsk_ref.shape, F32)

        @pl.when(i < nb)
        def _():
            cc, sc, cp, sp = cc_ref[...], sc_ref[...], cp_ref[...], sp_ref[...]
            ccq, scq = jnp.tile(cc, (1, 8)), jnp.tile(sc, (1, 8))
            q = (_rope(q_ref[...], ccq, scq, 1.0) * 0.125).astype(BF16)
            kvp, kvc = kvp_ref[...], kvc_ref[...]
            k = jnp.concatenate([_rope(kvp[:, :KV_W], cp, sp, 1.0), _rope(kvc[:, :KV_W], cc, sc, 1.0)], axis=0).astype(BF16)
            v = jnp.concatenate([kvp[:, KV_W:], kvc[:, KV_W:]], axis=0).astype(BF16)
            ok = _attn_mask(i)
            ok2 = jnp.concatenate([ok, ok], axis=0)
            lane = lax.broadcasted_iota(jnp.int32, (ATT_BLOCK, LANES), 1)
            lane_s = lax.broadcasted_iota(jnp.int32, (1, LANES), 1)
            dsk = jnp.zeros((1, LANES), F32)
            dkt_h, dvt_h = [], []
            for kvh in range(2):
                k2 = _dup_half(k, kvh == 0)
                v2 = _dup_half(v, kvh == 0)
                dkt = jnp.zeros((LANES, 2 * ATT_BLOCK), F32)
                dvt = jnp.zeros((LANES, 2 * ATT_BLOCK), F32)
                for pair in range(4):
                    h = 2 * (kvh * 4 + pair)
                    c0 = (kvh * 4 + pair) * LANES
                    do2 = do_ref[:, c0:c0 + LANES]
                    prod = do2.astype(F32) * o_ref[:, c0:c0 + LANES].astype(F32)
                    d_lo = jnp.sum(jnp.where(lane < HEAD_DIM, prod, 0.0), axis=1, keepdims=True)
                    d_hi = jnp.sum(jnp.where(lane >= HEAD_DIM, prod, 0.0), axis=1, keepdims=True)
                    delta = jnp.concatenate([d_lo, d_hi], axis=0)
                    p, p_sink, qs = _pair_probs(q[:, c0:c0 + LANES], k2, ok2, sink_ref[0, h], sink_ref[0, h + 1])
                    dos = _stack_heads(do2)
                    t = p_sink * delta
                    dsk = dsk - jnp.where(lane_s == h, jnp.sum(t[:ATT_BLOCK]), 0.0) \
                              - jnp.where(lane_s == h + 1, jnp.sum(t[ATT_BLOCK:]), 0.0)
                    dp = lax.dot_general(dos, v2, _NT, preferred_element_type=F32)
                    ds = (p * (dp - delta)).astype(BF16)
                    dqp = jnp.dot(ds, k2, preferred_element_type=F32)
                    dqs[:, c0:c0 + LANES] = jnp.where(lane < HEAD_DIM, dqp[:ATT_BLOCK], dqp[ATT_BLOCK:]) * 0.125
                    dkt = dkt + lax.dot_general(qs, ds, _TN, preferred_element_type=F32)
                    dvt = dvt + lax.dot_general(dos, p.astype(BF16), _TN, preferred_element_type=F32)
                dkt_h.append(dkt[:HEAD_DIM] + dkt[HEAD_DIM:])
                dvt_h.append(dvt[:HEAD_DIM] + dvt[HEAD_DIM:])
            dk = jnp.concatenate(dkt_h, axis=0).T
            dv = jnp.concatenate(dvt_h, axis=0).T
            dq_ref[...] = _rope(dqs[...], ccq, scq, -1.0).astype(dq_ref.dtype)
            dkp = _rope(dk[:ATT_BLOCK], cp, sp, -1.0)
            dkc = _rope(dk[ATT_BLOCK:], cc, sc, -1.0)
            dkv_ref[...] = (carry[...] + jnp.concatenate([dkp, dv[:ATT_BLOCK]], axis=1)).astype(dkv_ref.dtype)
            carry[...] = jnp.concatenate([dkc, dv[ATT_BLOCK:]], axis=1)
            dsk_ref[...] += dsk

        @pl.when(i == nb)
        def _():
            dkv_ref[...] = carry[...].astype(dkv_ref.dtype)

    blk = lambda w, f: pl.BlockSpec((ATT_BLOCK, w), f)
    cur = lambda i: jnp.minimum(i, nb - 1)
    prv = lambda i: jnp.maximum(jnp.minimum(i, nb - 1) - 1, 0)
    return pl.pallas_call(
        body,
        name="attn_bwd",
        grid=(nb + 1,),
        in_specs=[
            pl.BlockSpec(memory_space=pltpu.SMEM),
            blk(Q_W, lambda i: (cur(i), 0)),
            blk(2 * KV_W, lambda i: (prv(i), kvb)),
            blk(2 * KV_W, lambda i: (cur(i), kvb)),
            blk(LANES, lambda i: (cur(i), 0)),
            blk(LANES, lambda i: (cur(i), 0)),
            blk(LANES, lambda i: (prv(i), 0)),
            blk(LANES, lambda i: (prv(i), 0)),
            blk(Q_W, lambda i: (cur(i), 0)),
            blk(Q_W, lambda i: (cur(i), 0)),
        ],
        out_specs=[
            blk(Q_W, lambda i: (cur(i), 0)),
            blk(2 * KV_W, lambda i: (jnp.maximum(i - 1, 0), 0)),
            pl.BlockSpec((1, LANES), lambda i: (0, 0)),
        ],
        out_shape=[SDS((T, Q_W), BF16), SDS((T, 2 * KV_W), BF16), SDS((1, LANES), F32)],
        scratch_shapes=[pltpu.VMEM((ATT_BLOCK, 2 * KV_W), F32), pltpu.VMEM((ATT_BLOCK, Q_W), F32)],
        compiler_params=_cp(("arbitrary",)),
    )(sinks, za, za, za, cos, sin, cos, sin, o, do)


def _s5_discretize(lam_re, lam_im, log_dt, b_re, b_im):
    dt = jnp.exp(log_dt)[:, None]
    mag = jnp.exp(lam_re * dt)
    a_re, a_im = mag * jnp.cos(lam_im * dt), mag * jnp.sin(lam_im * dt)
    den = lam_re * lam_re + lam_im * lam_im
    nr, ni = a_re - 1.0, a_im
    coef_re = (nr * lam_re + ni * lam_im) / den
    coef_im = (ni * lam_re - nr * lam_im) / den
    bb_re = coef_re[..., None] * b_re - coef_im[..., None] * b_im
    bb_im = coef_re[..., None] * b_im + coef_im[..., None] * b_re
    return a_re, a_im, bb_re, bb_im


def _blockdiag_in(bb):
    x = bb.reshape(N_JB, 8, SSM_P, SSM_GC).transpose(0, 1, 3, 2)
    return (x[:, :, :, None, :] * jnp.eye(8, dtype=bb.dtype)[None, :, None, :, None]).reshape(N_JB, 128, 512)


def _blockdiag_in_extract(m):
    x = m.reshape(N_JB, 8, SSM_GC, 8, SSM_P)
    x = jnp.einsum('jgchp,gh->jgcp', x, jnp.eye(8, dtype=m.dtype))
    return x.transpose(0, 1, 3, 2).reshape(SSM_G, SSM_P, SSM_GC)


def _blockdiag_out(c):
    x = c.reshape(N_JB, 8, SSM_GC, SSM_P).transpose(0, 1, 3, 2)
    return (x[:, :, :, None, :] * jnp.eye(8, dtype=c.dtype)[None, :, None, :, None]).reshape(N_JB, 512, 128)


def _blockdiag_out_extract(m):
    x = m.reshape(N_JB, 8, SSM_P, 8, SSM_GC)
    x = jnp.einsum('jgphc,gh->jgpc', x, jnp.eye(8, dtype=m.dtype))
    return x.transpose(0, 1, 3, 2).reshape(SSM_G, SSM_GC, SSM_P)


def _s5_tables(a_re, a_im):
    ar, ai = a_re.reshape(N_LG, 1, LANES), a_im.reshape(N_LG, 1, LANES)
    pr, pi, n = ar, ai, 1
    while n < S5_SEG:
        pr, pi, n = pr * pr - pi * pi, 2.0 * pr * pi, 2 * n
    assert n == S5_SEG
    bc = lambda v: jnp.broadcast_to(v, (N_LG, SUBLANES, LANES))
    return bc(ar), bc(ai), pr, pi


def _s5_to_time_major(src_ref, dst_ref):
    for t in range(S5_SEG):
        dst_ref[t * SUBLANES:(t + 1) * SUBLANES, :] = src_ref[pl.ds(t, SUBLANES, stride=S5_SEG), :]


def _s5_from_time_major(val, dst_ref):
    for t in range(S5_SEG):
        dst_ref[pl.ds(t, SUBLANES, stride=S5_SEG), :] = val[t * SUBLANES:(t + 1) * SUBLANES, :]


def _tm_rows(t, row0=0):
    return pl.ds(pl.multiple_of(t * SUBLANES + row0, SUBLANES), SUBLANES)


def _s5_scan(src_re, src_im, ar, ai, reverse, start=None, dst=None, dst_row0=0):
    def step(n, carry):
        t = (S5_SEG - 1 - n) if reverse else n
        out = []
        for ll in range(LG_PER_JB):
            xr, xi = carry[2 * ll], carry[2 * ll + 1]
            idx = (ll, _tm_rows(t), slice(None))
            nr = ar[ll] * xr - ai[ll] * xi + src_re[idx]
            ni = ar[ll] * xi + ai[ll] * xr + src_im[idx]
            if dst is not None:
                odx = (ll, _tm_rows(t, dst_row0), slice(None))
                dst[0][odx] = nr
                dst[1][odx] = ni
            out += [nr, ni]
        return tuple(out)
    if start is None:
        init = (jnp.zeros((SUBLANES, LANES), F32),) * (2 * LG_PER_JB)
    else:
        init = tuple(s[ll] for ll in range(LG_PER_JB) for s in start)
    return lax.fori_loop(0, S5_SEG, step, init)


def _s5_fixup(ends, in_re, in_im, mr, mi, s_re, s_im, reverse):
    cr, ci = in_re, in_im
    order = range(SUBLANES - 1, -1, -1) if reverse else range(SUBLANES)
    for s in order:
        s_re[:, s:s + 1, :] = cr
        s_im[:, s:s + 1, :] = ci
        er = jnp.stack([ends[2 * ll][s:s + 1, :] for ll in range(LG_PER_JB)])
        ei = jnp.stack([ends[2 * ll + 1][s:s + 1, :] for ll in range(LG_PER_JB)])
        cr, ci = mr * cr - mi * ci + er, mr * ci + mi * cr + ei
    return cr, ci


def _s5_specs(nc, rev):
    cidx = (lambda c: nc - 1 - c) if rev else (lambda c: c)
    jb = lambda shape: pl.BlockSpec(shape, lambda j, c: (j, 0, 0))
    return cidx, [
        jb((1, LANES, 8 * LANES)),
        jb((1, 8 * LANES, LANES)),
        pl.BlockSpec((1, LANES), lambda j, c: (0, j)),
        jb((LG_PER_JB, SUBLANES, LANES)), jb((LG_PER_JB, SUBLANES, LANES)),
        jb((LG_PER_JB, 1, LANES)), jb((LG_PER_JB, 1, LANES)),
    ]


def _s5_fwd(za, prm, comm=None):
    T = za.shape[0]
    R = S5_CHUNK
    nc = T // R
    ub = (Q_W + 2 * KV_W) // LANES
    _, pspecs = _s5_specs(nc, False)

    def body(u_ref, b_ref, c_ref, d_ref, are_ref, aim_ref, alr_ref, ali_ref,
             yg_ref, x0r_ref, x0i_ref, bur, bui, xsr, xsi, sr, si, xcr, xci, utm, ynat):
        c = pl.program_id(1)

        @pl.when(c == 0)
        def _():
            xcr[...] = jnp.zeros(xcr.shape, F32)
            xci[...] = jnp.zeros(xci.shape, F32)

        _s5_to_time_major(u_ref, utm)
        u = utm[...]
        ub16 = u.astype(BF16)
        bu = jnp.dot(ub16, b_ref[0].astype(BF16), preferred_element_type=F32)
        for ll in range(LG_PER_JB):
            bur[ll] = bu[:, ll * LANES:(ll + 1) * LANES]
            bui[ll] = bu[:, (LG_PER_JB + ll) * LANES:(LG_PER_JB + ll + 1) * LANES]
        ar = [are_ref[ll] for ll in range(LG_PER_JB)]
        ai = [aim_ref[ll] for ll in range(LG_PER_JB)]
        ends = _s5_scan(bur, bui, ar, ai, False)
        in_r, in_i = xcr[...], xci[...]
        x0r_ref[0] = in_r
        x0i_ref[0] = in_i
        out_r, out_i = _s5_fixup(ends, in_r, in_i, alr_ref[...], ali_ref[...], sr, si, False)
        xcr[...] = out_r
        xci[...] = out_i
        _s5_scan(bur, bui, ar, ai, False, start=(sr, si), dst=(xsr, xsi))
        xcat =jnp.concatenate([xsr[ll].astype(BF16) for ll in range(LG_PER_JB)]
                               + [xsi[ll].astype(BF16) for ll in range(LG_PER_JB)], axis=1)
        y = d_ref[...] * u + jnp.dot(xcat, c_ref[0].astype(BF16), preferred_element_type=F32)
        _s5_from_time_major(_gelu(y), ynat)
        yg_ref[...] = ynat[...].astype(BF16)

    st = pl.BlockSpec((1, LG_PER_JB, 1, LANES), lambda j, c: (c, j, 0, 0))
    vm = lambda rows: pltpu.VMEM((LG_PER_JB, rows, LANES), F32)
    res = _call(
        "s5_fwd", body, (N_JB, nc),
        [pl.BlockSpec((R, LANES), lambda j, c: (c, ub + j))] + pspecs,
        [pl.BlockSpec((R, LANES), lambda j, c: (c, j)), st, st],
        [SDS((T, SSM_W), BF16), SDS((nc, N_LG, 1, LANES), F32), SDS((nc, N_LG, 1, LANES), F32)],
        [vm(R), vm(R), vm(R), vm(R), vm(SUBLANES), vm(SUBLANES), vm(1), vm(1),
         pltpu.VMEM((R, LANES), F32), pltpu.VMEM((R, LANES), F32)],
        ("parallel", "arbitrary"), (za, *prm), comm)
    return res if comm is None else (res[:3], res[3:])


def _s5_bwd(za, dyg, x0r, x0i, prm, comm=None):
    T = za.shape[0]
    R = S5_CHUNK
    nc = T // R
    ub = (Q_W + 2 * KV_W) // LANES
    cidx, pspecs = _s5_specs(nc, True)
    PAD = SUBLANES

    def body(u_ref, dyg_ref, x0r_ref, x0i_ref, b_ref, c_ref, d_ref, are_ref, aim_ref,
             alr_ref, ali_ref,
             du_ref, dar_ref, dai_ref, db_ref, dc_ref, dd_ref,
             bur, bui, xsr, xsi, sr, si, gcr, gci, utm, dtm, dunat):
        c = pl.program_id(1)

        @pl.when(c == 0)
        def _():
            gcr[...] = jnp.zeros(gcr.shape, F32)
            gci[...] = jnp.zeros(gci.shape, F32)
            dar_ref[...] = jnp.zeros(dar_ref.shape, F32)
            dai_ref[...] = jnp.zeros(dai_ref.shape, F32)
            db_ref[...] = jnp.zeros(db_ref.shape, F32)
            dc_ref[...] = jnp.zeros(dc_ref.shape, F32)
            dd_ref[...] = jnp.zeros(dd_ref.shape, F32)

        _s5_to_time_major(u_ref, utm)
        _s5_to_time_major(dyg_ref, dtm)
        u = utm[...]
        ub16 = u.astype(BF16)
        bcat, ccat = b_ref[0].astype(BF16), c_ref[0].astype(BF16)
        lanes = lambda v, ll: v[:, ll * LANES:(ll + 1) * LANES]
        bu = jnp.dot(ub16, bcat, preferred_element_type=F32)
        for ll in range(LG_PER_JB):
            bur[ll] = lanes(bu, ll)
            bui[ll] = lanes(bu, LG_PER_JB + ll)
        ar = [are_ref[ll] for ll in range(LG_PER_JB)]
        ai = [aim_ref[ll] for ll in range(LG_PER_JB)]
        ends = _s5_scan(bur, bui, ar, ai, False)
        in_r, in_i = x0r_ref[0], x0i_ref[0]
        _s5_fixup(ends, in_r, in_i, alr_ref[...], ali_ref[...], sr, si, False)
        _s5_scan(bur, bui, ar, ai, False, start=(sr, si), dst=(xsr, xsi), dst_row0=PAD)
        xsr[:, 0:PAD, :] = sr[...]
        xsi[:, 0:PAD, :] = si[...]
        xcat = jnp.concatenate([xsr[ll, PAD:, :].astype(BF16) for ll in range(LG_PER_JB)]
                               + [xsi[ll, PAD:, :].astype(BF16) for ll in range(LG_PER_JB)], axis=1)
        y = d_ref[...] * u + jnp.dot(xcat, ccat, preferred_element_type=F32)
        dy = dtm[...] * _gelu_grad(y)
        dyb = dy.astype(BF16)
        dd_ref[...] += jnp.sum(dy * u, axis=0, keepdims=True)
        du = d_ref[...] * dy
        dc_ref[0] += lax.dot_general(dyb, xcat, _TN, preferred_element_type=F32)
        g = lax.dot_general(dyb, ccat, _NT, preferred_element_type=F32)
        for ll in range(LG_PER_JB):
            bur[ll] = lanes(g, ll)
            bui[ll] = lanes(g, LG_PER_JB + ll)
        aic = [-v for v in ai]
        ends = _s5_scan(bur, bui, ar, aic, True)
        out_r, out_i = _s5_fixup(ends, gcr[...], gci[...], alr_ref[...], -ali_ref[...], sr, si, True)
        gcr[...] = out_r
        gci[...] = out_i
        _s5_scan(bur, bui, ar, aic, True, start=(sr, si), dst=(bur, bui))
        for ll in range(LG_PER_JB):
            gr, gi = bur[ll], bui[ll]
            xpr, xpi = xsr[ll, 0:R, :], xsi[ll, 0:R, :]
            red = lambda v: v.reshape(R // SUBLANES, SUBLANES, LANES).sum(axis=0)
            dar_ref[ll] += red(xpr * gr + xpi * gi)
            dai_ref[ll] += red(xpr * gi - xpi * gr)
        gcat = jnp.concatenate([bur[ll].astype(BF16) for ll in range(LG_PER_JB)]
                               + [bui[ll].astype(BF16) for ll in range(LG_PER_JB)], axis=1)
        db_ref[0] += lax.dot_general(ub16, gcat, _TN, preferred_element_type=F32)
        du = du + lax.dot_general(gcat, bcat, _NT, preferred_element_type=F32)
        _s5_from_time_major(du, dunat)
        du_ref[...] = dunat[...].astype(du_ref.dtype)

    st = pl.BlockSpec((1, LG_PER_JB, 1, LANES), lambda j, c: (cidx(c), j, 0, 0))
    jb = lambda shape: pl.BlockSpec(shape, lambda j, c: (j, 0, 0))
    vm = lambda rows: pltpu.VMEM((LG_PER_JB, rows, LANES), F32)
    res = _call(
        "s5_bwd", body, (N_JB, nc),
        [pl.BlockSpec((R, LANES), lambda j, c: (cidx(c), ub + j)),
         pl.BlockSpec((R, LANES), lambda j, c: (cidx(c), j)), st, st] + pspecs,
        [pl.BlockSpec((R, LANES), lambda j, c: (cidx(c), j)),
         jb((LG_PER_JB, SUBLANES, LANES)), jb((LG_PER_JB, SUBLANES, LANES)),
         jb((1, LANES, 8 * LANES)), jb((1, LANES, 8 * LANES)),
         pl.BlockSpec((1, LANES), lambda j, c: (0, j))],
        [SDS((T, SSM_W), BF16), SDS((N_LG, SUBLANES, LANES), F32), SDS((N_LG, SUBLANES, LANES), F32),
         SDS((N_JB, LANES, 8 * LANES), F32), SDS((N_JB, LANES, 8 * LANES), F32), SDS((1, SSM_W), F32)],
        [vm(R), vm(R), vm(R + PAD), vm(R + PAD), vm(SUBLANES), vm(SUBLANES), vm(1), vm(1)]
        + [pltpu.VMEM((R, LANES), F32)] * 3,
        ("parallel", "arbitrary"), (za, dyg, x0r, x0i, *prm), comm)
    return res if comm is None else (res[:6], res[6:])


def _assemble_w_a(wi):
    _, rows, cb = wi.shape
    tr = 256

    def body(w_ref, a_ref):
        a_ref[:, :cb] = w_ref[0]
        a_ref[:, cb:] = w_ref[1, :, :ZA_W - cb]

    return pl.pallas_call(
        body, name="assemble_w_a", grid=(rows // tr,),
        in_specs=[pl.BlockSpec((2, tr, cb), lambda i: (0, i, 0))],
        out_specs=pl.BlockSpec((tr, ZA_W), lambda i: (i, 0)),
        out_shape=SDS((rows, ZA_W), wi.dtype), compiler_params=_cp(("parallel",)))(wi)


def _assemble_w_g(wi):
    _, rows, cb = wi.shape
    tr = 256
    cut = ZA_W - cb

    def body(w_ref, g_ref):
        g_ref[:, :cb - cut] = w_ref[1, :, cut:]
        g_ref[:, cb - cut:2 * cb - cut] = w_ref[2]
        g_ref[:, 2 * cb - cut:] = w_ref[3]

    return pl.pallas_call(
        body, name="assemble_w_g", grid=(rows // tr,),
        in_specs=[pl.BlockSpec((4, tr, cb), lambda i: (0, i, 0))],
        out_specs=pl.BlockSpec((tr, 4 * cb - ZA_W), lambda i: (i, 0)),
        out_shape=SDS((rows, 4 * cb - ZA_W), wi.dtype), compiler_params=_cp(("parallel",)))(wi)


def _stack_w_in_grad(d_w_a, d_w_g):
    rows = d_w_a.shape[0]
    cb = (ZA_W + d_w_g.shape[1]) // 4
    cut = ZA_W - cb
    tr = 256

    def body(a_ref, g_ref, o_ref):
        o_ref[0] = a_ref[:, :cb]
        o_ref[1, :, :cut] = a_ref[:, cb:]
        o_ref[1, :, cut:] = g_ref[:, :cb - cut]
        o_ref[2] = g_ref[:, cb - cut:2 * cb - cut]
        o_ref[3] = g_ref[:, 2 * cb - cut:]

    return pl.pallas_call(
        body, name="stack_w_in_grad", grid=(rows // tr,),
        in_specs=[pl.BlockSpec((tr, ZA_W), lambda i: (i, 0)), pl.BlockSpec((tr, d_w_g.shape[1]), lambda i: (i, 0))],
        out_specs=pl.BlockSpec((4, tr, cb), lambda i: (0, i, 0)),
        out_shape=SDS((4, rows, cb), d_w_a.dtype), compiler_params=_cp(("parallel",)))(d_w_a, d_w_g)


def _local_step(x, target, gains, w_a, sinks, s5w, comms, late_g, late, red=None):
    T = x.shape[0]
    D = D_MODEL
    g1, g2, g3, g4 = gains
    cos, sin = _rope_tables(T)
    lam_re, lam_im, log_dt, b_re, b_im, c_re, c_im, d_skip = s5w
    (a_re, a_im, bb_re, bb_im), disc_vjp = jax.vjp(_s5_discretize, lam_re, lam_im, log_dt, b_re, b_im)
    abr, abi, al_re, al_im = _s5_tables(a_re, a_im)
    prm = (jnp.concatenate([_blockdiag_in(bb_re), _blockdiag_in(bb_im)], axis=2),
           jnp.concatenate([_blockdiag_out(c_re), -_blockdiag_out(c_im)], axis=1),
           d_skip.reshape(1, SSM_W), abr, abi, al_re, al_im)
    mm = functools.partial(_mm, tm=1024, tn=1024, tk=2048)

    h = _rowwise(lambda xv, g: ((_rms(xv)[0] * g,), ()), [(x, D, 0)], [g1], [(D, BF16)], [], tr=512, name="norm1")[0]
    unpack = lambda res, comm: (res, ()) if comm is None else res
    za, got_a = unpack(_mm(h, w_a, mode="nn", out_dtype=F32, tm=1024, tn=1152, tk=2048, name="mm_za", comm=comms[0]), comms[0])
    w_g = late_g(got_a)
    zg, got0 = unpack(mm(h, w_g, mode="nn", out_dtype=BF16, name="mm_zg", comm=comms[1]), comms[1])
    o_attn, got1 = unpack(_attn_fwd(za, cos, sin, sinks, comm=comms[2]), comms[2])
    (yg, x0r, x0i), got2 = unpack(_s5_fwd(za, prm, comm=comms[3]), comms[3])
    w_glu, w_ba, w_bs, w_out, w_up, w_down = late(got0, got1, got2)
    zglu = mm(yg, w_glu, mode="nn", out_dtype=BF16, name="mm_glu")
    o_ssm = _rowwise(lambda z1, z2: ((z1 * _sig(z2),), ()), [(zglu, SSM_W, 0), (zglu, SSM_W, 1)], [],
                     [(SSM_W, BF16)], [], tr=512, name="glu")[0]
    ya = mm(o_attn, w_ba, mode="nn", out_dtype=BF16, name="mm_ya")
    ys = mm(o_ssm, w_bs, mode="nn", out_dtype=BF16, name="mm_ys")
    mi = _rowwise(lambda ga, gs, a, s: ((_sig(ga) * a + _sig(gs) * s,), ()),
                  [(zg, D, 0), (zg, D, 1), (ya, D, 0), (ys, D, 0)], [], [(D, BF16)], [], tr=256, name="gate")[0]
    mixed = mm(mi, w_out, mode="nn", out_dtype=F32, name="mm_out")

    def f_post(xv, mv, g2v, g3v):
        x1v = xv + _rms(mv)[0] * g2v
        return (x1v, _rms(x1v)[0] * g3v), ()
    x1, h2 = _rowwise(f_post, [(x, D, 0), (mixed, D, 0)], [g2, g3], [(D, F32), (D, BF16)], [], tr=256, name="post_mix")
    act = mm(h2, w_up, mode="nn", out_dtype=BF16, name="mm_up", epi=lambda v: jnp.maximum(v, 0.0))
    f = mm(act, w_down, mode="nn", out_dtype=F32, name="mm_down", a_fn=lambda v: v * v, tk=4096)

    def f_final(x1v, fv, tv, g4v):
        fn, r = _rms(fv)
        e = x1v + fn * g4v - tv
        dx2v = e * (1.0 / D)
        dfv, dg4v = _rms_bwd(dx2v, fn, r, g4v)
        return (dfv, dx2v), (dg4v, jnp.zeros((SUBLANES, LANES), F32) + 0.5 * jnp.sum(e * e) * (1.0 / D))
    df, dx2, dg4, lossb = _rowwise(f_final, [(x1, D, 0), (f, D, 0), (target, D, 0)], [g4],
                                   [(D, BF16), (D, F32)], [(1, D), (SUBLANES, LANES)], tr=256, name="final")

    big = {}

    def add(k, g4):
        big[k] = g4
        if red is not None:
            red.add(k, g4)

    def hosted(fn, stage, names):
        if red is None:
            return fn(comm=None)
        out, got = fn(comm=getattr(red, stage)(names))
        getattr(red, stage + "_done")(names, got)
        return out

    dpre = mm(df, w_down, mode="nt", out_dtype=BF16, name="mm_dact", epi=lambda v, a: v * (2.0 * a.astype(F32)), extras=(act,))
    wg = functools.partial(_mm, mode="tn", out_dtype=F32, tm=1024, tn=1024, tk=4096)
    add("w_down", wg(act, df, name="wg_down", a_fn=lambda v: v * v).reshape(4, D_FF // 4, D))
    dh2 = hosted(functools.partial(mm, dpre, w_up, mode="nt", out_dtype=F32, name="mm_dh2", tk=4096),
                 "s1", ["w_down"])
    add("w_up", hosted(functools.partial(wg, h2, dpre, name="wg_up", shard_cols=D_FF // 4), "s3", ["w_down"]))

    def f_mid(dx2v, dh2v, x1v, mv, g2v, g3v):
        x1n, r3 = _rms(x1v)
        d3, dg3v = _rms_bwd(dh2v, x1n, r3, g3v)
        dx1v = dx2v + d3
        mn, r2 = _rms(mv)
        dmv, dg2v = _rms_bwd(dx1v, mn, r2, g2v)
        return (dx1v, dmv), (dg3v, dg2v)
    dx1, dmixed, dg3, dg2 = _rowwise(f_mid, [(dx2, D, 0), (dh2, D, 0), (x1, D, 0), (mixed, D, 0)], [g2, g3],
                                     [(D, F32), (D, BF16)], [(1, D), (1, D)], tr=256, name="mid")

    dmi = hosted(functools.partial(mm, dmixed, w_out, mode="nt", out_dtype=BF16, name="mm_dmi"), "s1", ["w_up"])
    add("w_out", wg(mi, dmixed, name="wg_out").reshape(4, D // 4, D))

    def f_gate(dv, ga, gs, a, s):
        sa, ss = _sig(ga), _sig(gs)
        return (dv * sa, dv * ss, jnp.concatenate([dv * a * sa * (1.0 - sa), dv * s * ss * (1.0 - ss)], axis=1)), ()
    dya, dys, dzg = _rowwise(f_gate, [(dmi, D, 0), (zg, D, 0), (zg, D, 1), (ya, D, 0), (ys, D, 0)], [],
                             [(D, BF16), (D, BF16), (2 * D, BF16)], [], tr=256, name="gate_bwd")
    do_attn = hosted(functools.partial(mm, dya, w_ba, mode="nt", out_dtype=BF16, name="mm_doa"), "s1", ["w_out"])
    d_w_ba = wg(o_attn, dya, name="wg_ba")
    do_ssm = mm(dys, w_bs, mode="nt", out_dtype=BF16, name="mm_dos")
    d_w_bs = wg(o_ssm, dys, name="wg_bs")
    add("w_branch", jnp.concatenate([d_w_ba.reshape(2, D // 4, D), d_w_bs.reshape(2, D // 4, D)], axis=0))

    def f_glu(dv, z1, z2):
        s2 = _sig(z2)
        return (jnp.concatenate([dv * s2, dv * z1 * s2 * (1.0 - s2)], axis=1),), ()
    dzglu = _rowwise(f_glu, [(do_ssm, SSM_W, 0), (zglu, SSM_W, 0), (zglu, SSM_W, 1)], [], [(2 * SSM_W, BF16)], [],
                     tr=512, name="glu_bwd")[0]
    dyg = hosted(functools.partial(mm, dzglu, w_glu, mode="nt", out_dtype=F32, name="mm_dyg"), "s1", ["w_branch"])
    add("w_glu", wg(yg, dzglu, name="wg_glu", tn=SSM_W // 2, shard_cols=SSM_W // 2))
    du, dar, dai, dbc, dcc, ddv = hosted(functools.partial(_s5_bwd, za, dyg, x0r, x0i, prm),
                                         "s3", ["w_up", "w_out", "w_branch"])
    dbr, dbi = dbc[:, :, :4 * LANES], dbc[:, :, 4 * LANES:]
    dcc = dcc.transpose(0, 2, 1)
    dcr, dci = dcc[:, :4 * LANES, :], -dcc[:, 4 * LANES:, :]
    dq, dkv, dsk = _attn_bwd(za, cos, sin, sinks, o_attn, do_attn)
    dza = jnp.concatenate([dq, dkv, du], axis=1)
    d_w_a = _mm(h, dza, mode="tn", out_dtype=F32, tm=1024, tn=ZA_W // 2, tk=2048, name="wg_a")
    d_w_g = wg(h, dzg, name="wg_g")
    add("w_in", _stack_w_in_grad(d_w_a, d_w_g))
    dh = hosted(functools.partial(mm, dza, w_a, mode="nt", out_dtype=F32, name="mm_dh_a", tk=ZA_W), "s1", ["w_in", "w_glu"])
    dh = hosted(functools.partial(mm, dzg, w_g, mode="nt", out_dtype=F32, name="mm_dh_g",
                                  epi=lambda v, p: v + p, extras=(dh,)), "s3", ["w_in", "w_glu"])

    def f_first(dx1v, dhv, xv, g1v):
        xn, r1 = _rms(xv)
        d1, dg1v = _rms_bwd(dhv, xn, r1, g1v)
        return (dx1v + d1,), (dg1v,)
    dx, dg1 = _rowwise(f_first, [(dx1, D, 0), (dh, D, 0), (x, D, 0)], [g1], [(D, F32)], [(1, D)], tr=256, name="first")

    da_re = dar.sum(axis=1).reshape(SSM_G, SSM_P)
    da_im = dai.sum(axis=1).reshape(SSM_G, SSM_P)
    d_lam_re, d_lam_im, d_log_dt, d_b_re, d_b_im = disc_vjp(
        (da_re, da_im, _blockdiag_in_extract(dbr), _blockdiag_in_extract(dbi)))
    small = dict(norm_mix_pre=dg1, norm_mix_post=dg2, norm_mlp_pre=dg3, norm_mlp_post=dg4,
                 sinks=dsk[:, :N_Q_HEADS], lam_re=d_lam_re, lam_im=d_lam_im, log_dt=d_log_dt,
                 b_re=d_b_re, b_im=d_b_im, c_re=_blockdiag_out_extract(dcr), c_im=_blockdiag_out_extract(dci),
                 d_skip=ddv.reshape(SSM_G, SSM_GC))
    return lossb[0, 0], dx, small, big


def _cast_into_slot(w, k_arr):
    rows, cols = w.shape
    tr = 256

    def body(k_ref, w_ref, o_ref):
        o_ref[0] = w_ref[...].astype(BF16)

    return pl.pallas_call(
        body,
        name="cast_into_slot",
        grid_spec=pltpu.PrefetchScalarGridSpec(
            num_scalar_prefetch=1,
            grid=(rows // tr,),
            in_specs=[pl.BlockSpec((tr, cols), lambda i, k: (i, 0))],
            out_specs=pl.BlockSpec((1, tr, cols), lambda i, k: (k[0], i, 0)),
        ),
        out_shape=SDS((4, rows, cols), BF16),
        compiler_params=_cp(("parallel",)),
    )(k_arr, w)


def _pair_sum(g, r, c_arr):
    _, _, hr, cols = g.shape
    tr = min(256, hr)

    def body(c_ref, g_ref, r_ref, o_ref):
        o_ref[0] = (g_ref[0, 0] + r_ref[0]).astype(BF16)

    return pl.pallas_call(
        body,
        name="pair_sum",
        grid_spec=pltpu.PrefetchScalarGridSpec(
            num_scalar_prefetch=1,
            grid=(3, hr // tr),
            in_specs=[pl.BlockSpec((1, 1, tr, cols), lambda k, i, c_ref: (c_ref[1 + k], c_ref[0], i, 0)),
                      pl.BlockSpec((1, tr, cols), lambda k, i, c_ref: (c_ref[1 + k], i, 0))],
            out_specs=pl.BlockSpec((1, tr, cols), lambda k, i, c_ref: (c_ref[1 + k], i, 0)),
        ),
        out_shape=SDS((4, hr, cols), BF16),
        compiler_params=_cp(("parallel", "parallel")),
    )(c_arr, g, r)


def _chip_sum(g, r, q, kc_arr):
    _, _, hr, cols = g.shape
    tr = min(256, hr)

    def body(kc_ref, g_ref, r_ref, q_ref, o_ref):
        s = g_ref[0, 0] + r_ref[0]
        for j in range(3):
            s = s + q_ref[j].astype(F32)
        o_ref[...] = s

    return pl.pallas_call(
        body,
        name="chip_sum",
        grid_spec=pltpu.PrefetchScalarGridSpec(
            num_scalar_prefetch=1,
            grid=(hr // tr,),
            in_specs=[pl.BlockSpec((1, 1, tr, cols), lambda i, kc: (kc[0], kc[1], i, 0)),
                      pl.BlockSpec((1, tr, cols), lambda i, kc: (kc[0], i, 0)),
                      pl.BlockSpec((3, tr, cols), lambda i, kc: (0, i, 0))],
            out_specs=pl.BlockSpec((tr, cols), lambda i, kc: (kc[1] * (hr // tr) + i, 0)),
        ),
        out_shape=SDS((2 * hr, cols), F32),
        compiler_params=_cp(("parallel",)),
    )(kc_arr, g, r, q)


def _pair_share(blocks):
    n = len(blocks)

    def body(*refs):
        ins, outs = refs[:n], refs[n:2 * n]
        ssem, rsem = refs[2 * n:]
        x, y, c, _ = _place()
        cps = []
        for w in range(n):
            hr = ins[w].shape[0] // 2
            rows = pl.ds(pl.multiple_of(c * hr, 8), hr)
            cp = _remote(ins[w].at[rows, :], outs[w].at[rows, :], ssem.at[w], rsem.at[w], (x, y, 1 - c))
            cp.start()
            cps.append(cp)
        for w in range(n):
            hr = ins[w].shape[0] // 2
            other = outs[w].at[pl.ds(pl.multiple_of((1 - c) * hr, 8), hr), :]
            _remote(other, other, ssem.at[w], rsem.at[w], (x, y, 1 - c)).wait_recv()
        for cp in cps:
            cp.wait_send()

    dma = pltpu.SemaphoreType.DMA
    return pl.pallas_call(
        body,
        name="pair_share",
        in_specs=[ANY] * n,
        out_specs=[ANY] * n,
        out_shape=[SDS(b.shape, b.dtype) for b in blocks],
        input_output_aliases={w: w for w in range(n)},
        scratch_shapes=[dma((n,)), dma((n,))],
    )(*blocks)


class _GradReducer:
    def __init__(self, c_arr, kc_arr):
        self.c_arr, self.kc_arr = c_arr, kc_arr
        self.g, self.r, self.ps, self.q = {}, {}, {}, {}

    def add(self, k, g4):
        self.g[k] = g4.reshape(4, 2, g4.shape[1] // 2, g4.shape[2])

    def s1(self, names):
        return _PairExchangeComm([self.g[k].reshape(4, -1, self.g[k].shape[3]) for k in names])

    def s1_done(self, names, got):
        for k, r in zip(names, got):
            self.r[k] = r
            self.ps[k] = _pair_sum(self.g[k], r, self.c_arr)

    def s3(self, names):
        return _ChipExchangeComm([self.ps[k] for k in names])

    def s3_done(self, names, got):
        self.q.update(zip(names, got))

    def finish(self, order):
        rest = [k for k in order if k not in self.r]
        if rest:
            self.s1_done(rest, _comm_only("pair_exchange", self.s1(rest)))
        rest = [k for k in order if k not in self.q]
        if rest:
            self.s3_done(rest, _comm_only("chip_exchange", self.s3(rest)))
        blocks = [_chip_sum(self.g[k], self.r[k], self.q[k], self.kc_arr) for k in order]
        return dict(zip(order, _pair_share(blocks)))


def _all_reduce_small(buf):
    rows = buf.shape[0]
    hr = rows // 2
    assert hr % SUBLANES == 0

    def body(in_ref, o_ref, sib, pair, slots, ssem, rsem):
        x, y, c, others = _place()
        me, sibling = 2 * x + y, (x, y, 1 - c)
        mine = pl.ds(pl.multiple_of(c * hr, SUBLANES), hr)
        theirs = pl.ds(pl.multiple_of((1 - c) * hr, SUBLANES), hr)
        first = _remote(in_ref, sib, ssem.at[0], rsem.at[0], sibling)
        first.start()
        first.wait()
        pair[...] = in_ref[...] + sib[...]
        slots[me] = pair[mine, :]
        cps = [_remote(pair.at[mine, :], slots.at[me], ssem.at[1 + r], rsem.at[1 + r], (ox, oy, c))
               for r, (ox, oy) in enumerate(others)]
        for cp in cps:
            cp.start()
        for r, (ox, oy) in enumerate(others):
            _remote(pair.at[mine, :], slots.at[2 * ox + oy], ssem.at[1 + r], rsem.at[1 + r], (ox, oy, c)).wait_recv()
        o_ref[mine, :] = (slots[0] + slots[1]) + (slots[2] + slots[3])
        last = _remote(o_ref.at[mine, :], o_ref.at[mine, :], ssem.at[4], rsem.at[4], sibling)
        last.start()
        _remote(o_ref.at[theirs, :], o_ref.at[theirs, :], ssem.at[4], rsem.at[4], sibling).wait_recv()
        last.wait_send()
        for cp in cps:
            cp.wait_send()

    dma = pltpu.SemaphoreType.DMA
    return pl.pallas_call(
        body,
        name="all_reduce_small",
        in_specs=[pl.BlockSpec(memory_space=pltpu.VMEM)],
        out_specs=pl.BlockSpec(memory_space=pltpu.VMEM),
        out_shape=SDS(buf.shape, F32),
        scratch_shapes=[pltpu.VMEM((rows, LANES), F32), pltpu.VMEM((rows, LANES), F32),
                        pltpu.VMEM((4, hr, LANES), F32), dma((5,)), dma((5,))],
        compiler_params=pltpu.CompilerParams(vmem_limit_bytes=VMEM_LIMIT),
    )(buf)


def _adam_fn(w, g, m, v):
    m2 = ADAM_B1 * m + (1.0 - ADAM_B1) * g
    v2 = ADAM_B2 * v + (1.0 - ADAM_B2) * (g * g)
    m_hat = m2 / (1.0 - ADAM_B1 ** ADAM_STEP)
    v_hat = v2 / (1.0 - ADAM_B2 ** ADAM_STEP)
    return (-ADAM_LR * (m_hat / (jnp.sqrt(v_hat) + ADAM_EPS) + ADAM_WD * w), m2, v2), ()


def _adamw(w, g, m, v, name, tr=256):
    cols = w.shape[1]
    return _rowwise(_adam_fn, [(w, cols, 0), (g, cols, 0), (m, cols, 0), (v, cols, 0)], [],
                    [(cols, F32)] * 3, [], tr=tr, name=name)


BIG = ("w_in", "w_glu", "w_branch", "w_out", "w_up", "w_down")
COL_SHARDED = ("w_in", "w_glu", "w_up")
SMALL = ("norm_mix_pre", "norm_mix_post", "norm_mlp_pre", "norm_mlp_post", "sinks", "lam_re", "lam_im", "log_dt",
         "b_re", "b_im", "c_re", "c_im", "d_skip")
WEIGHTS = ("norm_mix_pre", "norm_mix_post", "norm_mlp_pre", "norm_mlp_post", "w_in", "sinks", "lam_re", "lam_im",
           "log_dt", "b_re", "b_im", "c_re", "c_im", "d_skip", "w_glu", "w_branch", "w_out", "w_up", "w_down")


def _flat_small(vals, extra):
    flat = jnp.concatenate([vals[k].reshape(-1) for k in SMALL] + [extra.reshape(-1)])
    rows = -(-flat.shape[0] // (SUBLANES * LANES)) * SUBLANES
    return jnp.pad(flat, (0, rows * LANES - flat.shape[0])).reshape(rows, LANES)


def kernel(x, norm_mix_pre, norm_mix_post, norm_mlp_pre, norm_mlp_post, w_in, sinks, lam_re, lam_im, log_dt, b_re, b_im, c_re, c_im, d_skip, w_glu, w_branch, w_out, w_up, w_down, loss_target, m_norm_mix_pre, m_norm_mix_post, m_norm_mlp_pre, m_norm_mlp_post, m_w_in, m_sinks, m_lam_re, m_lam_im, m_log_dt, m_b_re, m_b_im, m_c_re, m_c_im, m_d_skip, m_w_glu, m_w_branch, m_w_out, m_w_up, m_w_down, v_norm_mix_pre, v_norm_mix_post, v_norm_mlp_pre, v_norm_mlp_post, v_w_in, v_sinks, v_lam_re, v_lam_im, v_log_dt, v_b_re, v_b_im, v_c_re, v_c_im, v_d_skip, v_w_glu, v_w_branch, v_w_out, v_w_up, v_w_down):
    w = dict(norm_mix_pre=norm_mix_pre, norm_mix_post=norm_mix_post, norm_mlp_pre=norm_mlp_pre, norm_mlp_post=norm_mlp_post,
             w_in=w_in, sinks=sinks, lam_re=lam_re, lam_im=lam_im, log_dt=log_dt, b_re=b_re, b_im=b_im, c_re=c_re,
             c_im=c_im, d_skip=d_skip, w_glu=w_glu, w_branch=w_branch, w_out=w_out, w_up=w_up, w_down=w_down)
    m = dict(norm_mix_pre=m_norm_mix_pre, norm_mix_post=m_norm_mix_post, norm_mlp_pre=m_norm_mlp_pre,
             norm_mlp_post=m_norm_mlp_post, w_in=m_w_in, sinks=m_sinks, lam_re=m_lam_re, lam_im=m_lam_im,
             log_dt=m_log_dt, b_re=m_b_re, b_im=m_b_im, c_re=m_c_re, c_im=m_c_im, d_skip=m_d_skip, w_glu=m_w_glu,
             w_branch=m_w_branch, w_out=m_w_out, w_up=m_w_up, w_down=m_w_down)
    v = dict(norm_mix_pre=v_norm_mix_pre, norm_mix_post=v_norm_mix_post, norm_mlp_pre=v_norm_mlp_pre,
             norm_mlp_post=v_norm_mlp_post, w_in=v_w_in, sinks=v_sinks, lam_re=v_lam_re, lam_im=v_lam_im,
             log_dt=v_log_dt, b_re=v_b_re, b_im=v_b_im, c_re=v_c_re, c_im=v_c_im, d_skip=v_d_skip, w_glu=v_w_glu,
             w_branch=v_w_branch, w_out=v_w_out, w_up=v_w_up, w_down=v_w_down)
    xi, yi, ci = lax.axis_index("x"), lax.axis_index("y"), lax.axis_index("c")

    k_arr = jnp.stack([2 * xi + yi]).astype(jnp.int32)
    slot = {k: _cast_into_slot(w[k][0], k_arr) for k in BIG}

    def whole(k, g4):
        if k in COL_SHARDED:
            return jnp.concatenate([g4[j] for j in range(4)], axis=1)
        return g4.reshape(4 * g4.shape[1], g4.shape[2])

    wi = _comm_only("gather_w_in", _GatherComm([slot["w_in"]]))[0]
    w_a = _assemble_w_a(wi)
    hosted = (("w_glu", "w_branch", "w_out"), ("w_up",), ("w_down",))
    comms = [None] + [_GatherComm([slot[k] for k in names]) for names in hosted]

    def late(*got):
        f = {k: whole(k, g4) for names, res in zip(hosted, got) for k, g4 in zip(names, res)}
        return f["w_glu"], f["w_branch"][:Q_W], f["w_branch"][Q_W:], f["w_out"], f["w_up"], f["w_down"]

    s5w = (lam_re[0], lam_im[0], log_dt[0], b_re[0], b_im[0], c_re[0], c_im[0], d_skip[0])
    reducer = _GradReducer(
        jnp.stack([ci, 2 * (1 - xi) + yi, 2 * xi + (1 - yi), 2 * (1 - xi) + (1 - yi)]).astype(jnp.int32),
        jnp.stack([2 * xi + yi, ci]).astype(jnp.int32))
    loss_part, dx, small, _ = _local_step(
        x[0], loss_target[0], (norm_mix_pre, norm_mix_post, norm_mlp_pre, norm_mlp_post),
        w_a, sinks, s5w, comms, lambda _: _assemble_w_g(wi), late, reducer)
    grads = reducer.finish(BIG)

    red = _all_reduce_small(_flat_small(small, loss_part)).reshape(-1)
    off = 0
    for k in SMALL:
        n = math.prod(w[k].shape)
        grads[k] = red[off:off + n].reshape(w[k].shape[1:])
        off += n
    loss = red[off]

    delta, new_m, new_v = {}, {}, {}
    for k in BIG:
        delta[k], new_m[k], new_v[k] = _adamw(w[k][0], grads[k], m[k][0], v[k][0], "adamw_" + k)
    zero = jnp.zeros((), F32)
    fw, fm, fv = (_flat_small({k: t[k] for k in SMALL}, zero) for t in (w, m, v))
    fg = _flat_small(grads, zero)
    sd, sm, sv = _adamw(fw, fg, fm, fv, "adamw_small", tr=fw.shape[0])
    off = 0
    for k in SMALL:
        n = math.prod(w[k].shape)
        delta[k], new_m[k], new_v[k] = (t.reshape(-1)[off:off + n].reshape(w[k].shape[1:]) for t in (sd, sm, sv))
        off += n

    lead = lambda t: t[None]
    return (loss, lead(dx), *[lead(grads[k]) for k in WEIGHTS], *[lead(delta[k]) for k in WEIGHTS],
            *[lead(new_m[k]) for k in WEIGHTS], *[lead(new_v[k]) for k in WEIGHTS])
```

```python
import functools
import math

import jax
import jax.numpy as jnp
from jax import lax
from jax.experimental import pallas as pl
from jax.experimental.pallas import tpu as pltpu

F32 = jnp.float32
BF16 = jnp.bfloat16
SDS = jax.ShapeDtypeStruct

D_MODEL = 2048
HEAD_DIM = 64
N_Q_HEADS = 16
ATT_BLOCK = 128
ROT_DIM = 16
ROPE_THETA = 500000.0
Q_W = 1024
KV_W = 128
SSM_W = 1024
SSM_G = 64
SSM_GC = 16
SSM_P = 64
N_STATE = SSM_G * SSM_P
LANES = 128
SUBLANES = 8
N_LG = N_STATE // LANES
N_JB = 8
LG_PER_JB = N_LG // N_JB
D_FF = 8192
ZA_W = Q_W + 2 * KV_W + SSM_W
EPS = 1e-6
S5_CHUNK = 2048
S5_SEG = S5_CHUNK // SUBLANES
VMEM_LIMIT = 56 * 1024 * 1024
NEG = -1e30

ADAM_LR = 0.001
ADAM_B1 = 0.9
ADAM_B2 = 0.999
ADAM_EPS = 1e-08
ADAM_WD = 0.01
ADAM_STEP = 10

MESH = pl.DeviceIdType.MESH


def _cp(sem):
    return pltpu.CompilerParams(dimension_semantics=sem, vmem_limit_bytes=VMEM_LIMIT)


ANY = pl.BlockSpec(memory_space=pl.ANY)


def _place():
    x, y, c = lax.axis_index("x"), lax.axis_index("y"), lax.axis_index("c")
    others = [(1 - x, y), (x, 1 - y), (1 - x, 1 - y)]
    return x, y, c, others


def _remote(src, dst, ssem, rsem, to):
    return pltpu.make_async_remote_copy(src_ref=src, dst_ref=dst, send_sem=ssem, recv_sem=rsem,
                                        device_id=to, device_id_type=MESH)


class _GatherComm:
    aliased = True

    def __init__(self, slotted):
        self.arrs = list(slotted)
        self.n = len(self.arrs)
        dma = pltpu.SemaphoreType.DMA
        self.scratch = [dma((3 * self.n,)) for _ in range(4)]
        self.out_shape = [SDS(s.shape, s.dtype) for s in self.arrs]

    @staticmethod
    def _half(ref, hc):
        hr = ref.shape[1] // 2
        return pl.ds(pl.multiple_of(hc * hr, 16), hr)

    def _sends(self, ins, outs, sems):
        ssem, rsem, _, _ = sems
        x, y, c, others = _place()
        me = 2 * x + y
        return [_remote(ins[w].at[me, self._half(ins[w], c), :], outs[w].at[me, self._half(ins[w], c), :],
                        ssem.at[3 * w + r], rsem.at[3 * w + r], (ox, oy, c))
                for w in range(self.n) for r, (ox, oy) in enumerate(others)]

    def start(self, ins, outs, sems):
        for cp in self._sends(ins, outs, sems):
            cp.start()

    def finish(self, ins, outs, sems):
        ssem, rsem, fs_sem, fr_sem = sems
        x, y, c, others = _place()
        sib = (x, y, 1 - c)
        passes = []
        for w in range(self.n):
            for r, (ox, oy) in enumerate(others):
                got = outs[w].at[2 * ox + oy, self._half(ins[w], c), :]
                _remote(got, got, ssem.at[3 * w + r], rsem.at[3 * w + r], (ox, oy, c)).wait_recv()
                cp = _remote(got, got, fs_sem.at[3 * w + r], fr_sem.at[3 * w + r], sib)
                cp.start()
                passes.append(cp)
        for w in range(self.n):
            for r, (ox, oy) in enumerate(others):
                got = outs[w].at[2 * ox + oy, self._half(ins[w], 1 - c), :]
                _remote(got, got, fs_sem.at[3 * w + r], fr_sem.at[3 * w + r], sib).wait_recv()
        for cp in self._sends(ins, outs, sems) + passes:
            cp.wait_send()


class _PairExchangeComm:
    aliased = False

    def __init__(self, grads):
        self.arrs = list(grads)
        self.n = len(self.arrs)
        dma = pltpu.SemaphoreType.DMA
        self.scratch = [dma((self.n,)), dma((self.n,))]
        self.out_shape = [SDS((4, g.shape[1] // 2, g.shape[2]), g.dtype) for g in self.arrs]

    def _copies(self, ins, outs, sems):
        ssem, rsem = sems
        x, y, c, _ = _place()
        cps = []
        for w in range(self.n):
            hr = ins[w].shape[1] // 2
            src = ins[w].at[:, pl.ds(pl.multiple_of((1 - c) * hr, 8), hr), :]
            cps.append(_remote(src, outs[w], ssem.at[w], rsem.at[w], (x, y, 1 - c)))
        return cps

    def start(self, ins, outs, sems):
        for cp in self._copies(ins, outs, sems):
            cp.start()

    def finish(self, ins, outs, sems):
        for cp in self._copies(ins, outs, sems):
            cp.wait()


class _ChipExchangeComm:
    aliased = False

    def __init__(self, psums):
        self.arrs = list(psums)
        self.n = len(self.arrs)
        dma = pltpu.SemaphoreType.DMA
        self.scratch = [dma((3 * self.n,)), dma((3 * self.n,))]
        self.out_shape = [SDS((3,) + p.shape[1:], p.dtype) for p in self.arrs]

    def _copies(self, ins, outs, sems):
        ssem, rsem = sems
        x, y, c, others = _place()
        return [_remote(ins[w].at[2 * ox + oy], outs[w].at[r], ssem.at[3 * w + r], rsem.at[3 * w + r], (ox, oy, c))
                for w in range(self.n) for r, (ox, oy) in enumerate(others)]

    def start(self, ins, outs, sems):
        for cp in self._copies(ins, outs, sems):
            cp.start()

    def finish(self, ins, outs, sems):
        for cp in self._copies(ins, outs, sems):
            cp.wait()


def _comm_only(name, comm):
    n = comm.n

    def body(*refs):
        ins, outs, sems = refs[:n], refs[n:2 * n], refs[2 * n:]
        comm.start(ins, outs, sems)
        comm.finish(ins, outs, sems)

    return pl.pallas_call(
        body, name=name, in_specs=[ANY] * n, out_specs=[ANY] * n, out_shape=comm.out_shape,
        input_output_aliases={w: w for w in range(n)} if comm.aliased else {},
        scratch_shapes=comm.scratch)(*comm.arrs)


def _call(name, body, grid, in_specs, out_specs, out_shape, scratch, dims, args, comm=None):
    if comm is None:
        return pl.pallas_call(body, name=name, grid=grid, in_specs=in_specs, out_specs=out_specs, out_shape=out_shape,
                              scratch_shapes=scratch, compiler_params=_cp(dims))(*args)
    ni, no, ns, n = len(in_specs), len(out_shape), len(scratch), comm.n

    def hosted(*refs):
        ins, cin = refs[:ni], refs[ni:ni + n]
        outs, cout = refs[ni + n:ni + n + no], refs[ni + n + no:ni + 2 * n + no]
        scr, sems = refs[ni + 2 * n + no:ni + 2 * n + no + ns], refs[ni + 2 * n + no + ns:]
        ids = [pl.program_id(d) for d in range(len(grid))]
        first = functools.reduce(jnp.logical_and, [i == 0 for i in ids])
        last = functools.reduce(jnp.logical_and, [i == g - 1 for i, g in zip(ids, grid)])

        @pl.when(first)
        def _():
            comm.start(cin, cout, sems)

        body(*ins, *outs, *scr)

        @pl.when(last)
        def _():
            comm.finish(cin, cout, sems)

    return pl.pallas_call(
        hosted, name=name, grid=grid, in_specs=list(in_specs) + [ANY] * n, out_specs=list(out_specs) + [ANY] * n,
        out_shape=list(out_shape) + comm.out_shape,
        input_output_aliases={ni + w: no + w for w in range(n)} if comm.aliased else {},
        scratch_shapes=list(scratch) + comm.scratch, compiler_params=_cp(("arbitrary",) * len(grid)))(*args, *comm.arrs)


def _mm(a, b, *, mode, out_dtype, tm, tn, tk, name, a_fn=None, epi=None, extras=(), comm=None, shard_cols=None):
    if mode == "nn":
        (M, K), (K2, N) = a.shape, b.shape
    elif mode == "nt":
        (M, K), (N, K2) = a.shape, b.shape
    else:
        (K, M), (K2, N) = a.shape, b.shape
    assert K == K2, (a.shape, b.shape, mode)
    tm, tn, tk = min(tm, M), min(tn, N), min(tk, K)
    assert M % tm == 0 and N % tn == 0 and K % tk == 0, (M, N, K, tm, tn, tk)
    nk = K // tk
    if mode == "tn":
        a_spec = pl.BlockSpec((tk, tm), lambda i, j, k: (k, i))
        ca = 0
    else:
        a_spec = pl.BlockSpec((tm, tk), lambda i, j, k: (i, k))
        ca = 1
    if mode == "nt":
        b_spec = pl.BlockSpec((tn, tk), lambda i, j, k: (j, k))
        cb = 1
    else:
        b_spec = pl.BlockSpec((tk, tn), lambda i, j, k: (k, j))
        cb = 0
    dims = (((ca,), (cb,)), ((), ()))
    ne = len(extras)

    def body(a_ref, b_ref, *rest):
        ex = rest[:ne]
        o_ref = rest[ne]
        av = a_ref[...]
        if a_fn is not None:
            av = a_fn(av.astype(F32))
        p = lax.dot_general(av.astype(BF16), b_ref[...].astype(BF16), dims, preferred_element_type=F32)

        def fin(v):
            if epi is not None:
                v = epi(v, *[e[...] for e in ex])
            o_ref[...] = v.astype(out_dtype).reshape(o_ref.shape)

        if nk == 1:
            fin(p)
        else:
            acc = rest[ne + 1]
            k = pl.program_id(2)

            @pl.when(k == 0)
            def _():
                acc[...] = p

            @pl.when(k > 0)
            def _():
                acc[...] += p

            @pl.when(k == nk - 1)
            def _():
                fin(acc[...])

    if shard_cols is None:
        o_spec, o_shape = pl.BlockSpec((tm, tn), lambda i, j, k: (i, j)), SDS((M, N), out_dtype)
    else:
        per = shard_cols // tn
        assert shard_cols % tn == 0 and N % shard_cols == 0
        o_spec = pl.BlockSpec((1, tm, tn), lambda i, j, k: (lax.div(j, per), i, lax.rem(j, per)))
        o_shape = SDS((N // shard_cols, M, shard_cols), out_dtype)
    res = _call(name, body, (M // tm, N // tn, nk),
                [a_spec, b_spec] + [pl.BlockSpec((tm, tn), lambda i, j, k: (i, j)) for _ in extras],
                [o_spec], [o_shape],
                [pltpu.VMEM((tm, tn), F32)] if nk > 1 else [], ("parallel", "parallel", "arbitrary"),
                (a, b, *extras), comm)
    return res[0] if comm is None else (res[0], res[1:])


def _rowwise(fn, rows, bcasts, outs, accs, *, tr, name):
    T = rows[0][0].shape[0]
    tr = min(tr, T)
    assert T % tr == 0
    nr, nb, no, na = len(rows), len(bcasts), len(outs), len(accs)
    in_specs = [pl.BlockSpec((tr, w), functools.partial(lambda i, c: (i, c), c=cb)) for (_, w, cb) in rows]
    in_specs += [pl.BlockSpec(b.shape, lambda i: (0, 0)) for b in bcasts]
    out_shape = [SDS((T, w), dt) for (w, dt) in outs] + [SDS(s, F32) for s in accs]
    out_specs = [pl.BlockSpec((tr, w), lambda i: (i, 0)) for (w, _) in outs]
    out_specs += [pl.BlockSpec(s, lambda i: (0, 0)) for s in accs]

    def body(*refs):
        ins = [r[...].astype(F32) for r in refs[:nr + nb]]
        o_refs = refs[nr + nb:nr + nb + no]
        a_refs = refs[nr + nb + no:]
        ro, ao = fn(*ins)
        for r, v in zip(o_refs, ro):
            r[...] = v.astype(r.dtype)
        if na:
            @pl.when(pl.program_id(0) == 0)
            def _():
                for r in a_refs:
                    r[...] = jnp.zeros(r.shape, F32)

            for r, v in zip(a_refs, ao):
                r[...] += v

    res = pl.pallas_call(
        body,
        name=name,
        grid=(T // tr,),
        in_specs=in_specs,
        out_specs=out_specs,
        out_shape=out_shape,
        compiler_params=_cp(("arbitrary",) if na else ("parallel",)),
    )(*[r[0] for r in rows], *bcasts)
    return res


def _rms(v):
    r = lax.rsqrt(jnp.mean(v * v, axis=-1, keepdims=True) + EPS)
    return v * r, r


def _rms_bwd(dy, xn, r, g):
    dxn = dy * g
    dv = r * (dxn - xn * jnp.mean(dxn * xn, axis=-1, keepdims=True))
    return dv, jnp.sum(dy * xn, axis=0, keepdims=True)


def _sig(v):
    return 1.0 / (1.0 + jnp.exp(-v))


_GELU_C = math.sqrt(2.0 / math.pi)


def _gelu(v):
    return 0.5 * v * (1.0 + jnp.tanh(_GELU_C * (v + 0.044715 * v * v * v)))


def _gelu_grad(v):
    t = jnp.tanh(_GELU_C * (v + 0.044715 * v * v * v))
    return 0.5 * (1.0 + t) + 0.5 * v * (1.0 - t * t) * _GELU_C * (1.0 + 3.0 * 0.044715 * v * v)


def _rope(v, c, s, sign):
    w = v.shape[1]
    m = lax.broadcasted_iota(jnp.int32, v.shape, 1) % HEAD_DIM
    p = jnp.where(m < ROT_DIM // 2, -pltpu.roll(v, w - ROT_DIM // 2, 1), pltpu.roll(v, ROT_DIM // 2, 1))
    return v * c + sign * (p * s)


def _rope_tables(T):
    half = ROT_DIM // 2
    inv = ROPE_THETA ** (-jnp.arange(half, dtype=F32) * 2.0 / ROT_DIM)
    ang = jnp.arange(T).astype(F32)[:, None] * inv[None, :]
    cos, sin = jnp.cos(ang), jnp.sin(ang)
    one = jnp.ones((T, HEAD_DIM - ROT_DIM), F32)
    c64 = jnp.concatenate([cos, cos, one], axis=1)
    s64 = jnp.concatenate([sin, sin, 0.0 * one], axis=1)
    return jnp.tile(c64, (1, 2)), jnp.tile(s64, (1, 2))


def _dup_half(m, lo):
    lane = lax.broadcasted_iota(jnp.int32, m.shape, 1)
    sw = pltpu.roll(m, HEAD_DIM, 1)
    return jnp.where(lane < HEAD_DIM, m, sw) if lo else jnp.where(lane >= HEAD_DIM, m, sw)


def _attn_mask(i):
    qi = lax.broadcasted_iota(jnp.int32, (ATT_BLOCK, 2 * ATT_BLOCK), 0)
    kj = lax.broadcasted_iota(jnp.int32, (ATT_BLOCK, 2 * ATT_BLOCK), 1)
    rel = qi + ATT_BLOCK - kj
    return (rel >= 0) & (rel < ATT_BLOCK) & ((kj >= ATT_BLOCK) | (i > 0))


_NT = (((1,), (1,)), ((), ()))
_TN = (((0,), (0,)), ((), ()))


def _stack_heads(m):
    lane = lax.broadcasted_iota(jnp.int32, m.shape, 1)
    zero = jnp.zeros_like(m)
    return jnp.concatenate([jnp.where(lane < HEAD_DIM, m, zero), jnp.where(lane >= HEAD_DIM, m, zero)], axis=0)


def _pair_probs(q2, k2, ok2, sink_lo, sink_hi):
    qs = _stack_heads(q2)
    s = lax.dot_general(qs, k2, _NT, preferred_element_type=F32)
    s = jnp.where(ok2, s, NEG)
    row = lax.broadcasted_iota(jnp.int32, (2 * ATT_BLOCK, 1), 0)
    sink = jnp.where(row < ATT_BLOCK, sink_lo, sink_hi)
    m = jnp.maximum(jnp.max(s, axis=1, keepdims=True), sink)
    e = jnp.exp(s - m)
    es = jnp.exp(sink - m)
    inv = 1.0 / (jnp.sum(e, axis=1, keepdims=True) + es)
    return e * inv, es * inv, qs


def _attn_fwd(za, cos, sin, sinks, comm=None):
    T = za.shape[0]
    nb = T // ATT_BLOCK
    kvb = Q_W // (2 * KV_W)

    def body(sink_ref, q_ref, kvp_ref, kvc_ref, cc_ref, sc_ref, cp_ref, sp_ref, o_ref):
        i = pl.program_id(0)
        cc, sc, cp, sp = cc_ref[...], sc_ref[...], cp_ref[...], sp_ref[...]
        q = (_rope(q_ref[...], jnp.tile(cc, (1, 8)), jnp.tile(sc, (1, 8)), 1.0) * 0.125).astype(BF16)
        kvp, kvc = kvp_ref[...], kvc_ref[...]
        k = jnp.concatenate([_rope(kvp[:, :KV_W], cp, sp, 1.0), _rope(kvc[:, :KV_W], cc, sc, 1.0)], axis=0).astype(BF16)
        v = jnp.concatenate([kvp[:, KV_W:], kvc[:, KV_W:]], axis=0).astype(BF16)
        ok = _attn_mask(i)
        ok2 = jnp.concatenate([ok, ok], axis=0)
        lane = lax.broadcasted_iota(jnp.int32, (ATT_BLOCK, LANES), 1)
        for kvh in range(2):
            k2 = _dup_half(k, kvh == 0)
            v2 = _dup_half(v, kvh == 0)
            for pair in range(4):
                c0 = (kvh * 4 + pair) * LANES
                q2 = q[:, c0:c0 + LANES]
                p, _, _ = _pair_probs(q2, k2, ok2, sink_ref[0, 2 * (kvh * 4 + pair)], sink_ref[0, 2 * (kvh * 4 + pair) + 1])
                o = jnp.dot(p.astype(BF16), v2, preferred_element_type=F32)
                o_ref[:, c0:c0 + LANES] = jnp.where(lane < HEAD_DIM, o[:ATT_BLOCK], o[ATT_BLOCK:]).astype(BF16)

    blk = lambda w, f: pl.BlockSpec((ATT_BLOCK, w), f)
    res = _call(
        "attn_fwd", body, (nb,),
        [
            pl.BlockSpec(memory_space=pltpu.SMEM),
            blk(Q_W, lambda i: (i, 0)),
            blk(2 * KV_W, lambda i: (jnp.maximum(i - 1, 0), kvb)),
            blk(2 * KV_W, lambda i: (i, kvb)),
            blk(LANES, lambda i: (i, 0)),
            blk(LANES, lambda i: (i, 0)),
            blk(LANES, lambda i: (jnp.maximum(i - 1, 0), 0)),
            blk(LANES, lambda i: (jnp.maximum(i - 1, 0), 0)),
        ],
        [blk(Q_W, lambda i: (i, 0))], [SDS((T, Q_W), BF16)], [], ("parallel",),
        (sinks, za, za, za, cos, sin, cos, sin), comm)
    return res[0] if comm is None else (res[0], res[1:])


def _attn_bwd(za, cos, sin, sinks, o, do):
    T = za.shape[0]
    nb = T // ATT_BLOCK
    kvb = Q_W // (2 * KV_W)

    def body(sink_ref, q_ref, kvp_ref, kvc_ref, cc_ref, sc_ref, cp_ref, sp_ref, o_ref, do_ref,
             dq_ref, dkv_ref, dsk_ref, carry, dqs):
        i = pl.program_id(0)

        @pl.when(i == 0)
        def _():
            carry[...] = jnp.zeros(carry.shape, F32)
            dsk_ref[...] = jnp.zeros(dsk_ref.shape, F32)

        @pl.when(i < nb)
        def _():
            cc, sc, cp, sp = cc_ref[...], sc_ref[...], cp_ref[...], sp_ref[...]
            ccq, scq = jnp.tile(cc, (1, 8)), jnp.tile(sc, (1, 8))
            q = (_rope(q_ref[...], ccq, scq, 1.0) * 0.125).astype(BF16)
            kvp, kvc = kvp_ref[...], kvc_ref[...]
            k = jnp.concatenate([_rope(kvp[:, :KV_W], cp, sp, 1.0), _rope(kvc[:, :KV_W], cc, sc, 1.0)], axis=0).astype(BF16)
            v = jnp.concatenate([kvp[:, KV_W:], kvc[:, KV_W:]], axis=0).astype(BF16)
            ok = _attn_mask(i)
            ok2 = jnp.concatenate([ok, ok], axis=0)
            lane = lax.broadcasted_iota(jnp.int32, (ATT_BLOCK, LANES), 1)
            lane_s = lax.broadcasted_iota(jnp.int32, (1, LANES), 1)
            dsk = jnp.zeros((1, LANES), F32)
            dkt_h, dvt_h = [], []
            for kvh in range(2):
                k2 = _dup_half(k, kvh == 0)
                v2 = _dup_half(v, kvh == 0)
                dkt = jnp.zeros((LANES, 2 * ATT_BLOCK), F32)
                dvt = jnp.zeros((LANES, 2 * ATT_BLOCK), F32)
                for pair in range(4):
                    h = 2 * (kvh * 4 + pair)
                    c0 = (kvh * 4 + pair) * LANES
                    do2 = do_ref[:, c0:c0 + LANES]
                    prod = do2.astype(F32) * o_ref[:, c0:c0 + LANES].astype(F32)
                    d_lo = jnp.sum(jnp.where(lane < HEAD_DIM, prod, 0.0), axis=1, keepdims=True)
                    d_hi = jnp.sum(jnp.where(lane >= HEAD_DIM, prod, 0.0), axis=1, keepdims=True)
                    delta = jnp.concatenate([d_lo, d_hi], axis=0)
                    p, p_sink, qs = _pair_probs(q[:, c0:c0 + LANES], k2, ok2, sink_ref[0, h], sink_ref[0, h + 1])
                    dos = _stack_heads(do2)
                    t = p_sink * delta
                    dsk = dsk - jnp.where(lane_s == h, jnp.sum(t[:ATT_BLOCK]), 0.0) \
                              - jnp.where(lane_s == h + 1, jnp.sum(t[ATT_BLOCK:]), 0.0)
                    dp = lax.dot_general(dos, v2, _NT, preferred_element_type=F32)
                    ds = (p * (dp - delta)).astype(BF16)
                    dqp = jnp.dot(ds, k2, preferred_element_type=F32)
                    dqs[:, c0:c0 + LANES] = jnp.where(lane < HEAD_DIM, dqp[:ATT_BLOCK], dqp[ATT_BLOCK:]) * 0.125
                    dkt = dkt + lax.dot_general(qs, ds, _TN, preferred_element_type=F32)
                    dvt = dvt + lax.dot_general(dos, p.astype(BF16), _TN, preferred_element_type=F32)
                dkt_h.append(dkt[:HEAD_DIM] + dkt[HEAD_DIM:])
                dvt_h.append(dvt[:HEAD_DIM] + dvt[HEAD_DIM:])
            dk = jnp.concatenate(dkt_h, axis=0).T
            dv = jnp.concatenate(dvt_h, axis=0).T
            dq_ref[...] = _rope(dqs[...], ccq, scq, -1.0).astype(dq_ref.dtype)
            dkp = _rope(dk[:ATT_BLOCK], cp, sp, -1.0)
            dkc = _rope(dk[ATT_BLOCK:], cc, sc, -1.0)
            dkv_ref[...] = (carry[...] + jnp.concatenate([dkp, dv[:ATT_BLOCK]], axis=1)).astype(dkv_ref.dtype)
            carry[...] = jnp.concatenate([dkc, dv[ATT_BLOCK:]], axis=1)
            dsk_ref[...] += dsk

        @pl.when(i == nb)
        def _():
            dkv_ref[...] = carry[...].astype(dkv_ref.dtype)

    blk = lambda w, f: pl.BlockSpec((ATT_BLOCK, w), f)
    cur = lambda i: jnp.minimum(i, nb - 1)
    prv = lambda i: jnp.maximum(jnp.minimum(i, nb - 1) - 1, 0)
    return pl.pallas_call(
        body,
        name="attn_bwd",
        grid=(nb + 1,),
        in_specs=[
            pl.BlockSpec(memory_space=pltpu.SMEM),
            blk(Q_W, lambda i: (cur(i), 0)),
            blk(2 * KV_W, lambda i: (prv(i), kvb)),
            blk(2 * KV_W, lambda i: (cur(i), kvb)),
            blk(LANES, lambda i: (cur(i), 0)),
            blk(LANES, lambda i: (cur(i), 0)),
            blk(LANES, lambda i: (prv(i), 0)),
            blk(LANES, lambda i: (prv(i), 0)),
            blk(Q_W, lambda i: (cur(i), 0)),
            blk(Q_W, lambda i: (cur(i), 0)),
        ],
        out_specs=[
            blk(Q_W, lambda i: (cur(i), 0)),
            blk(2 * KV_W, lambda i: (jnp.maximum(i - 1, 0), 0)),
            pl.BlockSpec((1, LANES), lambda i: (0, 0)),
        ],
        out_shape=[SDS((T, Q_W), BF16), SDS((T, 2 * KV_W), BF16), SDS((1, LANES), F32)],
        scratch_shapes=[pltpu.VMEM((ATT_BLOCK, 2 * KV_W), F32), pltpu.VMEM((ATT_BLOCK, Q_W), F32)],
        compiler_params=_cp(("arbitrary",)),
    )(sinks, za, za, za, cos, sin, cos, sin, o, do)


def _s5_discretize(lam_re, lam_im, log_dt, b_re, b_im):
    dt = jnp.exp(log_dt)[:, None]
    mag = jnp.exp(lam_re * dt)
    a_re, a_im = mag * jnp.cos(lam_im * dt), mag * jnp.sin(lam_im * dt)
    den = lam_re * lam_re + lam_im * lam_im
    nr, ni = a_re - 1.0, a_im
    coef_re = (nr * lam_re + ni * lam_im) / den
    coef_im = (ni * lam_re - nr * lam_im) / den
    bb_re = coef_re[..., None] * b_re - coef_im[..., None] * b_im
    bb_im = coef_re[..., None] * b_im + coef_im[..., None] * b_re
    return a_re, a_im, bb_re, bb_im


def _blockdiag_in(bb):
    x = bb.reshape(N_JB, 8, SSM_P, SSM_GC).transpose(0, 1, 3, 2)
    return (x[:, :, :, None, :] * jnp.eye(8, dtype=bb.dtype)[None, :, None, :, None]).reshape(N_JB, 128, 512)


def _blockdiag_in_extract(m):
    x = m.reshape(N_JB, 8, SSM_GC, 8, SSM_P)
    x = jnp.einsum('jgchp,gh->jgcp', x, jnp.eye(8, dtype=m.dtype))
    return x.transpose(0, 1, 3, 2).reshape(SSM_G, SSM_P, SSM_GC)


def _blockdiag_out(c):
    x = c.reshape(N_JB, 8, SSM_GC, SSM_P).transpose(0, 1, 3, 2)
    return (x[:, :, :, None, :] * jnp.eye(8, dtype=c.dtype)[None, :, None, :, None]).reshape(N_JB, 512, 128)


def _blockdiag_out_extract(m):
    x = m.reshape(N_JB, 8, SSM_P, 8, SSM_GC)
    x = jnp.einsum('jgphc,gh->jgpc', x, jnp.eye(8, dtype=m.dtype))
    return x.transpose(0, 1, 3, 2).reshape(SSM_G, SSM_GC, SSM_P)


def _s5_tables(a_re, a_im):
    ar, ai = a_re.reshape(N_LG, 1, LANES), a_im.reshape(N_LG, 1, LANES)
    pr, pi, n = ar, ai, 1
    while n < S5_SEG:
        pr, pi, n = pr * pr - pi * pi, 2.0 * pr * pi, 2 * n
    assert n == S5_SEG
    bc = lambda v: jnp.broadcast_to(v, (N_LG, SUBLANES, LANES))
    return bc(ar), bc(ai), pr, pi


def _s5_to_time_major(src_ref, dst_ref):
    for t in range(S5_SEG):
        dst_ref[t * SUBLANES:(t + 1) * SUBLANES, :] = src_ref[pl.ds(t, SUBLANES, stride=S5_SEG), :]


def _s5_from_time_major(val, dst_ref):
    for t in range(S5_SEG):
        dst_ref[pl.ds(t, SUBLANES, stride=S5_SEG), :] = val[t * SUBLANES:(t + 1) * SUBLANES, :]


def _tm_rows(t, row0=0):
    return pl.ds(pl.multiple_of(t * SUBLANES + row0, SUBLANES), SUBLANES)


def _s5_scan(src_re, src_im, ar, ai, reverse, start=None, dst=None, dst_row0=0):
    def step(n, carry):
        t = (S5_SEG - 1 - n) if reverse else n
        out = []
        for ll in range(LG_PER_JB):
            xr, xi = carry[2 * ll], carry[2 * ll + 1]
            idx = (ll, _tm_rows(t), slice(None))
            nr = ar[ll] * xr - ai[ll] * xi + src_re[idx]
            ni = ar[ll] * xi + ai[ll] * xr + src_im[idx]
            if dst is not None:
                odx = (ll, _tm_rows(t, dst_row0), slice(None))
                dst[0][odx] = nr
                dst[1][odx] = ni
            out += [nr, ni]
        return tuple(out)
    if start is None:
        init = (jnp.zeros((SUBLANES, LANES), F32),) * (2 * LG_PER_JB)
    else:
        init = tuple(s[ll] for ll in range(LG_PER_JB) for s in start)
    return lax.fori_loop(0, S5_SEG, step, init)


def _s5_fixup(ends, in_re, in_im, mr, mi, s_re, s_im, reverse):
    cr, ci = in_re, in_im
    order = range(SUBLANES - 1, -1, -1) if reverse else range(SUBLANES)
    for s in order:
        s_re[:, s:s + 1, :] = cr
        s_im[:, s:s + 1, :] = ci
        er = jnp.stack([ends[2 * ll][s:s + 1, :] for ll in range(LG_PER_JB)])
        ei = jnp.stack([ends[2 * ll + 1][s:s + 1, :] for ll in range(LG_PER_JB)])
        cr, ci = mr * cr - mi * ci + er, mr * ci + mi * cr + ei
    return cr, ci


def _s5_specs(nc, rev):
    cidx = (lambda c: nc - 1 - c) if rev else (lambda c: c)
    jb = lambda shape: pl.BlockSpec(shape, lambda j, c: (j, 0, 0))
    return cidx, [
        jb((1, LANES, 8 * LANES)),
        jb((1, 8 * LANES, LANES)),
        pl.BlockSpec((1, LANES), lambda j, c: (0, j)),
        jb((LG_PER_JB, SUBLANES, LANES)), jb((LG_PER_JB, SUBLANES, LANES)),
        jb((LG_PER_JB, 1, LANES)), jb((LG_PER_JB, 1, LANES)),
    ]


def _s5_fwd(za, prm, comm=None):
    T = za.shape[0]
    R = S5_CHUNK
    nc = T // R
    ub = (Q_W + 2 * KV_W) // LANES
    _, pspecs = _s5_specs(nc, False)

    def body(u_ref, b_ref, c_ref, d_ref, are_ref, aim_ref, alr_ref, ali_ref,
             yg_ref, x0r_ref, x0i_ref, bur, bui, xsr, xsi, sr, si, xcr, xci, utm, ynat):
        c = pl.program_id(1)

        @pl.when(c == 0)
        def _():
            xcr[...] = jnp.zeros(xcr.shape, F32)
            xci[...] = jnp.zeros(xci.shape, F32)

        _s5_to_time_major(u_ref, utm)
        u = utm[...]
        ub16 = u.astype(BF16)
        bu = jnp.dot(ub16, b_ref[0].astype(BF16), preferred_element_type=F32)
        for ll in range(LG_PER_JB):
            bur[ll] = bu[:, ll * LANES:(ll + 1) * LANES]
            bui[ll] = bu[:, (LG_PER_JB + ll) * LANES:(LG_PER_JB + ll + 1) * LANES]
        ar = [are_ref[ll] for ll in range(LG_PER_JB)]
        ai = [aim_ref[ll] for ll in range(LG_PER_JB)]
        ends = _s5_scan(bur, bui, ar, ai, False)
        in_r, in_i = xcr[...], xci[...]
        x0r_ref[0] = in_r
        x0i_ref[0] = in_i
        out_r, out_i = _s5_fixup(ends, in_r, in_i, alr_ref[...], ali_ref[...], sr, si, False)
        xcr[...] = out_r
        xci[...] = out_i
        _s5_scan(bur, bui, ar, ai, False, start=(sr, si), dst=(xsr, xsi))
        xcat =jnp.concatenate([xsr[ll].astype(BF16) for ll in range(LG_PER_JB)]
                               + [xsi[ll].astype(BF16) for ll in range(LG_PER_JB)], axis=1)
        y = d_ref[...] * u + jnp.dot(xcat, c_ref[0].astype(BF16), preferred_element_type=F32)
        _s5_from_time_major(_gelu(y), ynat)
        yg_ref[...] = ynat[...].astype(BF16)

    st = pl.BlockSpec((1, LG_PER_JB, 1, LANES), lambda j, c: (c, j, 0, 0))
    vm = lambda rows: pltpu.VMEM((LG_PER_JB, rows, LANES), F32)
    res = _call(
        "s5_fwd", body, (N_JB, nc),
        [pl.BlockSpec((R, LANES), lambda j, c: (c, ub + j))] + pspecs,
        [pl.BlockSpec((R, LANES), lambda j, c: (c, j)), st, st],
        [SDS((T, SSM_W), BF16), SDS((nc, N_LG, 1, LANES), F32), SDS((nc, N_LG, 1, LANES), F32)],
        [vm(R), vm(R), vm(R), vm(R), vm(SUBLANES), vm(SUBLANES), vm(1), vm(1),
         pltpu.VMEM((R, LANES), F32), pltpu.VMEM((R, LANES), F32)],
        ("parallel", "arbitrary"), (za, *prm), comm)
    return res if comm is None else (res[:3], res[3:])


def _s5_bwd(za, dyg, x0r, x0i, prm, comm=None):
    T = za.shape[0]
    R = S5_CHUNK
    nc = T // R
    ub = (Q_W + 2 * KV_W) // LANES
    cidx, pspecs = _s5_specs(nc, True)
    PAD = SUBLANES

    def body(u_ref, dyg_ref, x0r_ref, x0i_ref, b_ref, c_ref, d_ref, are_ref, aim_ref,
             alr_ref, ali_ref,
             du_ref, dar_ref, dai_ref, db_ref, dc_ref, dd_ref,
             bur, bui, xsr, xsi, sr, si, gcr, gci, utm, dtm, dunat):
        c = pl.program_id(1)

        @pl.when(c == 0)
        def _():
            gcr[...] = jnp.zeros(gcr.shape, F32)
            gci[...] = jnp.zeros(gci.shape, F32)
            dar_ref[...] = jnp.zeros(dar_ref.shape, F32)
            dai_ref[...] = jnp.zeros(dai_ref.shape, F32)
            db_ref[...] = jnp.zeros(db_ref.shape, F32)
            dc_ref[...] = jnp.zeros(dc_ref.shape, F32)
            dd_ref[...] = jnp.zeros(dd_ref.shape, F32)

        _s5_to_time_major(u_ref, utm)
        _s5_to_time_major(dyg_ref, dtm)
        u = utm[...]
        ub16 = u.astype(BF16)
        bcat, ccat = b_ref[0].astype(BF16), c_ref[0].astype(BF16)
        lanes = lambda v, ll: v[:, ll * LANES:(ll + 1) * LANES]
        bu = jnp.dot(ub16, bcat, preferred_element_type=F32)
        for ll in range(LG_PER_JB):
            bur[ll] = lanes(bu, ll)
            bui[ll] = lanes(bu, LG_PER_JB + ll)
        ar = [are_ref[ll] for ll in range(LG_PER_JB)]
        ai = [aim_ref[ll] for ll in range(LG_PER_JB)]
        ends = _s5_scan(bur, bui, ar, ai, False)
        in_r, in_i = x0r_ref[0], x0i_ref[0]
        _s5_fixup(ends, in_r, in_i, alr_ref[...], ali_ref[...], sr, si, False)
        _s5_scan(bur, bui, ar, ai, False, start=(sr, si), dst=(xsr, xsi), dst_row0=PAD)
        xsr[:, 0:PAD, :] = sr[...]
        xsi[:, 0:PAD, :] = si[...]
        xcat = jnp.concatenate([xsr[ll, PAD:, :].astype(BF16) for ll in range(LG_PER_JB)]
                               + [xsi[ll, PAD:, :].astype(BF16) for ll in range(LG_PER_JB)], axis=1)
        y = d_ref[...] * u + jnp.dot(xcat, ccat, preferred_element_type=F32)
        dy = dtm[...] * _gelu_grad(y)
        dyb = dy.astype(BF16)
        dd_ref[...] += jnp.sum(dy * u, axis=0, keepdims=True)
        du = d_ref[...] * dy
        dc_ref[0] += lax.dot_general(dyb, xcat, _TN, preferred_element_type=F32)
        g = lax.dot_general(dyb, ccat, _NT, preferred_element_type=F32)
        for ll in range(LG_PER_JB):
            bur[ll] = lanes(g, ll)
            bui[ll] = lanes(g, LG_PER_JB + ll)
        aic = [-v for v in ai]
        ends = _s5_scan(bur, bui, ar, aic, True)
        out_r, out_i = _s5_fixup(ends, gcr[...], gci[...], alr_ref[...], -ali_ref[...], sr, si, True)
        gcr[...] = out_r
        gci[...] = out_i
        _s5_scan(bur, bui, ar, aic, True, start=(sr, si), dst=(bur, bui))
        for ll in range(LG_PER_JB):
            gr, gi = bur[ll], bui[ll]
            xpr, xpi = xsr[ll, 0:R, :], xsi[ll, 0:R, :]
            red = lambda v: v.reshape(R // SUBLANES, SUBLANES, LANES).sum(axis=0)
            dar_ref[ll] += red(xpr * gr + xpi * gi)
            dai_ref[ll] += red(xpr * gi - xpi * gr)
        gcat = jnp.concatenate([bur[ll].astype(BF16) for ll in range(LG_PER_JB)]
                               + [bui[ll].astype(BF16) for ll in range(LG_PER_JB)], axis=1)
        db_ref[0] += lax.dot_general(ub16, gcat, _TN, preferred_element_type=F32)
        du = du + lax.dot_general(gcat, bcat, _NT, preferred_element_type=F32)
        _s5_from_time_major(du, dunat)
        du_ref[...] = dunat[...].astype(du_ref.dtype)

    st = pl.BlockSpec((1, LG_PER_JB, 1, LANES), lambda j, c: (cidx(c), j, 0, 0))
    jb = lambda shape: pl.BlockSpec(shape, lambda j, c: (j, 0, 0))
    vm = lambda rows: pltpu.VMEM((LG_PER_JB, rows, LANES), F32)
    res = _call(
        "s5_bwd", body, (N_JB, nc),
        [pl.BlockSpec((R, LANES), lambda j, c: (cidx(c), ub + j)),
         pl.BlockSpec((R, LANES), lambda j, c: (cidx(c), j)), st, st] + pspecs,
        [pl.BlockSpec((R, LANES), lambda j, c: (cidx(c), j)),
         jb((LG_PER_JB, SUBLANES, LANES)), jb((LG_PER_JB, SUBLANES, LANES)),
         jb((1, LANES, 8 * LANES)), jb((1, LANES, 8 * LANES)),
         pl.BlockSpec((1, LANES), lambda j, c: (0, j))],
        [SDS((T, SSM_W), BF16), SDS((N_LG, SUBLANES, LANES), F32), SDS((N_LG, SUBLANES, LANES), F32),
         SDS((N_JB, LANES, 8 * LANES), F32), SDS((N_JB, LANES, 8 * LANES), F32), SDS((1, SSM_W), F32)],
        [vm(R), vm(R), vm(R + PAD), vm(R + PAD), vm(SUBLANES), vm(SUBLANES), vm(1), vm(1)]
        + [pltpu.VMEM((R, LANES), F32)] * 3,
        ("parallel", "arbitrary"), (za, dyg, x0r, x0i, *prm), comm)
    return res if comm is None else (res[:6], res[6:])


def _assemble_w_a(wi):
    _, rows, cb = wi.shape
    tr = 256

    def body(w_ref, a_ref):
        a_ref[:, :cb] = w_ref[0]
        a_ref[:, cb:] = w_ref[1, :, :ZA_W - cb]

    return pl.pallas_call(
        body, name="assemble_w_a", grid=(rows // tr,),
        in_specs=[pl.BlockSpec((2, tr, cb), lambda i: (0, i, 0))],
        out_specs=pl.BlockSpec((tr, ZA_W), lambda i: (i, 0)),
        out_shape=SDS((rows, ZA_W), wi.dtype), compiler_params=_cp(("parallel",)))(wi)


def _assemble_w_g(wi):
    _, rows, cb = wi.shape
    tr = 256
    cut = ZA_W - cb

    def body(w_ref, g_ref):
        g_ref[:, :cb - cut] = w_ref[1, :, cut:]
        g_ref[:, cb - cut:2 * cb - cut] = w_ref[2]
        g_ref[:, 2 * cb - cut:] = w_ref[3]

    return pl.pallas_call(
        body, name="assemble_w_g", grid=(rows // tr,),
        in_specs=[pl.BlockSpec((4, tr, cb), lambda i: (0, i, 0))],
        out_specs=pl.BlockSpec((tr, 4 * cb - ZA_W), lambda i: (i, 0)),
        out_shape=SDS((rows, 4 * cb - ZA_W), wi.dtype), compiler_params=_cp(("parallel",)))(wi)


def _stack_w_in_grad(d_w_a, d_w_g):
    rows = d_w_a.shape[0]
    cb = (ZA_W + d_w_g.shape[1]) // 4
    cut = ZA_W - cb
    tr = 256

    def body(a_ref, g_ref, o_ref):
        o_ref[0] = a_ref[:, :cb]
        o_ref[1, :, :cut] = a_ref[:, cb:]
        o_ref[1, :, cut:] = g_ref[:, :cb - cut]
        o_ref[2] = g_ref[:, cb - cut:2 * cb - cut]
        o_ref[3] = g_ref[:, 2 * cb - cut:]

    return pl.pallas_call(
        body, name="stack_w_in_grad", grid=(rows // tr,),
        in_specs=[pl.BlockSpec((tr, ZA_W), lambda i: (i, 0)), pl.BlockSpec((tr, d_w_g.shape[1]), lambda i: (i, 0))],
        out_specs=pl.BlockSpec((4, tr, cb), lambda i: (0, i, 0)),
        out_shape=SDS((4, rows, cb), d_w_a.dtype), compiler_params=_cp(("parallel",)))(d_w_a, d_w_g)


def _local_step(x, target, gains, w_a, sinks, s5w, comms, late_g, late, red=None):
    T = x.shape[0]
    D = D_MODEL
    g1, g2, g3, g4 = gains
    cos, sin = _rope_tables(T)
    lam_re, lam_im, log_dt, b_re, b_im, c_re, c_im, d_skip = s5w
    (a_re, a_im, bb_re, bb_im), disc_vjp = jax.vjp(_s5_discretize, lam_re, lam_im, log_dt, b_re, b_im)
    abr, abi, al_re, al_im = _s5_tables(a_re, a_im)
    prm = (jnp.concatenate([_blockdiag_in(bb_re), _blockdiag_in(bb_im)], axis=2),
           jnp.concatenate([_blockdiag_out(c_re), -_blockdiag_out(c_im)], axis=1),
           d_skip.reshape(1, SSM_W), abr, abi, al_re, al_im)
    mm = functools.partial(_mm, tm=1024, tn=1024, tk=2048)

    h = _rowwise(lambda xv, g: ((_rms(xv)[0] * g,), ()), [(x, D, 0)], [g1], [(D, BF16)], [], tr=512, name="norm1")[0]
    unpack = lambda res, comm: (res, ()) if comm is None else res
    za, got_a = unpack(_mm(h, w_a, mode="nn", out_dtype=F32, tm=1024, tn=1152, tk=2048, name="mm_za", comm=comms[0]), comms[0])
    w_g = late_g(got_a)
    zg, got0 = unpack(mm(h, w_g, mode="nn", out_dtype=BF16, name="mm_zg", comm=comms[1]), comms[1])
    o_attn, got1 = unpack(_attn_fwd(za, cos, sin, sinks, comm=comms[2]), comms[2])
    (yg, x0r, x0i), got2 = unpack(_s5_fwd(za, prm, comm=comms[3]), comms[3])
    w_glu, w_ba, w_bs, w_out, w_up, w_down = late(got0, got1, got2)
    zglu = mm(yg, w_glu, mode="nn", out_dtype=BF16, name="mm_glu")
    o_ssm = _rowwise(lambda z1, z2: ((z1 * _sig(z2),), ()), [(zglu, SSM_W, 0), (zglu, SSM_W, 1)], [],
                     [(SSM_W, BF16)], [], tr=512, name="glu")[0]
    ya = mm(o_attn, w_ba, mode="nn", out_dtype=BF16, name="mm_ya")
    ys = mm(o_ssm, w_bs, mode="nn", out_dtype=BF16, name="mm_ys")
    mi = _rowwise(lambda ga, gs, a, s: ((_sig(ga) * a + _sig(gs) * s,), ()),
                  [(zg, D, 0), (zg, D, 1), (ya, D, 0), (ys, D, 0)], [], [(D, BF16)], [], tr=256, name="gate")[0]
    mixed = mm(mi, w_out, mode="nn", out_dtype=F32, name="mm_out")

    def f_post(xv, mv, g2v, g3v):
        x1v = xv + _rms(mv)[0] * g2v
        return (x1v, _rms(x1v)[0] * g3v), ()
    x1, h2 = _rowwise(f_post, [(x, D, 0), (mixed, D, 0)], [g2, g3], [(D, F32), (D, BF16)], [], tr=256, name="post_mix")
    act = mm(h2, w_up, mode="nn", out_dtype=BF16, name="mm_up", epi=lambda v: jnp.maximum(v, 0.0))
    f = mm(act, w_down, mode="nn", out_dtype=F32, name="mm_down", a_fn=lambda v: v * v, tk=4096)

    def f_final(x1v, fv, tv, g4v):
        fn, r = _rms(fv)
        e = x1v + fn * g4v - tv
        dx2v = e * (1.0 / D)
        dfv, dg4v = _rms_bwd(dx2v, fn, r, g4v)
        return (dfv, dx2v), (dg4v, jnp.zeros((SUBLANES, LANES), F32) + 0.5 * jnp.sum(e * e) * (1.0 / D))
    df, dx2, dg4, lossb = _rowwise(f_final, [(x1, D, 0), (f, D, 0), (target, D, 0)], [g4],
                                   [(D, BF16), (D, F32)], [(1, D), (SUBLANES, LANES)], tr=256, name="final")

    big = {}

    def add(k, g4):
        big[k] = g4
        if red is not None:
            red.add(k, g4)

    def hosted(fn, stage, names):
        if red is None:
            return fn(comm=None)
        out, got = fn(comm=getattr(red, stage)(names))
        getattr(red, stage + "_done")(names, got)
        return out

    dpre = mm(df, w_down, mode="nt", out_dtype=BF16, name="mm_dact", epi=lambda v, a: v * (2.0 * a.astype(F32)), extras=(act,))
    wg = functools.partial(_mm, mode="tn", out_dtype=F32, tm=1024, tn=1024, tk=4096)
    add("w_down", wg(act, df, name="wg_down", a_fn=lambda v: v * v).reshape(4, D_FF // 4, D))
    dh2 = hosted(functools.partial(mm, dpre, w_up, mode="nt", out_dtype=F32, name="mm_dh2", tk=4096),
                 "s1", ["w_down"])
    add("w_up", hosted(functools.partial(wg, h2, dpre, name="wg_up", shard_cols=D_FF // 4), "s3", ["w_down"]))

    def f_mid(dx2v, dh2v, x1v, mv, g2v, g3v):
        x1n, r3 = _rms(x1v)
        d3, dg3v = _rms_bwd(dh2v, x1n, r3, g3v)
        dx1v = dx2v + d3
        mn, r2 = _rms(mv)
        dmv, dg2v = _rms_bwd(dx1v, mn, r2, g2v)
        return (dx1v, dmv), (dg3v, dg2v)
    dx1, dmixed, dg3, dg2 = _rowwise(f_mid, [(dx2, D, 0), (dh2, D, 0), (x1, D, 0), (mixed, D, 0)], [g2, g3],
                                     [(D, F32), (D, BF16)], [(1, D), (1, D)], tr=256, name="mid")

    dmi = hosted(functools.partial(mm, dmixed, w_out, mode="nt", out_dtype=BF16, name="mm_dmi"), "s1", ["w_up"])
    add("w_out", wg(mi, dmixed, name="wg_out").reshape(4, D // 4, D))

    def f_gate(dv, ga, gs, a, s):
        sa, ss = _sig(ga), _sig(gs)
        return (dv * sa, dv * ss, jnp.concatenate([dv * a * sa * (1.0 - sa), dv * s * ss * (1.0 - ss)], axis=1)), ()
    dya, dys, dzg = _rowwise(f_gate, [(dmi, D, 0), (zg, D, 0), (zg, D, 1), (ya, D, 0), (ys, D, 0)], [],
                             [(D, BF16), (D, BF16), (2 * D, BF16)], [], tr=256, name="gate_bwd")
    do_attn = hosted(functools.partial(mm, dya, w_ba, mode="nt", out_dtype=BF16, name="mm_doa"), "s1", ["w_out"])
    d_w_ba = wg(o_attn, dya, name="wg_ba")
    do_ssm = mm(dys, w_bs, mode="nt", out_dtype=BF16, name="mm_dos")
    d_w_bs = wg(o_ssm, dys, name="wg_bs")
    add("w_branch", jnp.concatenate([d_w_ba.reshape(2, D // 4, D), d_w_bs.reshape(2, D // 4, D)], axis=0))

    def f_glu(dv, z1, z2):
        s2 = _sig(z2)
        return (jnp.concatenate([dv * s2, dv * z1 * s2 * (1.0 - s2)], axis=1),), ()
    dzglu = _rowwise(f_glu, [(do_ssm, SSM_W, 0), (zglu, SSM_W, 0), (zglu, SSM_W, 1)], [], [(2 * SSM_W, BF16)], [],
                     tr=512, name="glu_bwd")[0]
    dyg = hosted(functools.partial(mm, dzglu, w_glu, mode="nt", out_dtype=F32, name="mm_dyg"), "s1", ["w_branch"])
    add("w_glu", wg(yg, dzglu, name="wg_glu", tn=SSM_W // 2, shard_cols=SSM_W // 2))
    du, dar, dai, dbc, dcc, ddv = hosted(functools.partial(_s5_bwd, za, dyg, x0r, x0i, prm),
                                         "s3", ["w_up", "w_out", "w_branch"])
    dbr, dbi = dbc[:, :, :4 * LANES], dbc[:, :, 4 * LANES:]
    dcc = dcc.transpose(0, 2, 1)
    dcr, dci = dcc[:, :4 * LANES, :], -dcc[:, 4 * LANES:, :]
    dq, dkv, dsk = _attn_bwd(za, cos, sin, sinks, o_attn, do_attn)
    dza = jnp.concatenate([dq, dkv, du], axis=1)
    d_w_a = _mm(h, dza, mode="tn", out_dtype=F32, tm=1024, tn=ZA_W // 2, tk=2048, name="wg_a")
    d_w_g = wg(h, dzg, name="wg_g")
    add("w_in", _stack_w_in_grad(d_w_a, d_w_g))
    dh = hosted(functools.partial(mm, dza, w_a, mode="nt", out_dtype=F32, name="mm_dh_a", tk=ZA_W), "s1", ["w_in", "w_glu"])
    dh = hosted(functools.partial(mm, dzg, w_g, mode="nt", out_dtype=F32, name="mm_dh_g",
                                  epi=lambda v, p: v + p, extras=(dh,)), "s3", ["w_in", "w_glu"])

    def f_first(dx1v, dhv, xv, g1v):
        xn, r1 = _rms(xv)
        d1, dg1v = _rms_bwd(dhv, xn, r1, g1v)
        return (dx1v + d1,), (dg1v,)
    dx, dg1 = _rowwise(f_first, [(dx1, D, 0), (dh, D, 0), (x, D, 0)], [g1], [(D, F32)], [(1, D)], tr=256, name="first")

    da_re = dar.sum(axis=1).reshape(SSM_G, SSM_P)
    da_im = dai.sum(axis=1).reshape(SSM_G, SSM_P)
    d_lam_re, d_lam_im, d_log_dt, d_b_re, d_b_im = disc_vjp(
        (da_re, da_im, _blockdiag_in_extract(dbr), _blockdiag_in_extract(dbi)))
    small = dict(norm_mix_pre=dg1, norm_mix_post=dg2, norm_mlp_pre=dg3, norm_mlp_post=dg4,
                 sinks=dsk[:, :N_Q_HEADS], lam_re=d_lam_re, lam_im=d_lam_im, log_dt=d_log_dt,
                 b_re=d_b_re, b_im=d_b_im, c_re=_blockdiag_out_extract(dcr), c_im=_blockdiag_out_extract(dci),
                 d_skip=ddv.reshape(SSM_G, SSM_GC))
    return lossb[0, 0], dx, small, big


def _cast_into_slot(w, k_arr):
    rows, cols = w.shape
    tr = 256

    def body(k_ref, w_ref, o_ref):
        o_ref[0] = w_ref[...].astype(BF16)

    return pl.pallas_call(
        body,
        name="cast_into_slot",
        grid_spec=pltpu.PrefetchScalarGridSpec(
            num_scalar_prefetch=1,
            grid=(rows // tr,),
            in_specs=[pl.BlockSpec((tr, cols), lambda i, k: (i, 0))],
            out_specs=pl.BlockSpec((1, tr, cols), lambda i, k: (k[0], i, 0)),
        ),
        out_shape=SDS((4, rows, cols), BF16),
        compiler_params=_cp(("parallel",)),
    )(k_arr, w)


def _pair_sum(g, r, c_arr):
    _, _, hr, cols = g.shape
    tr = min(256, hr)

    def body(c_ref, g_ref, r_ref, o_ref):
        o_ref[0] = (g_ref[0, 0] + r_ref[0]).astype(BF16)

    return pl.pallas_call(
        body,
        name="pair_sum",
        grid_spec=pltpu.PrefetchScalarGridSpec(
            num_scalar_prefetch=1,
            grid=(3, hr // tr),
            in_specs=[pl.BlockSpec((1, 1, tr, cols), lambda k, i, c_ref: (c_ref[1 + k], c_ref[0], i, 0)),
                      pl.BlockSpec((1, tr, cols), lambda k, i, c_ref: (c_ref[1 + k], i, 0))],
            out_specs=pl.BlockSpec((1, tr, cols), lambda k, i, c_ref: (c_ref[1 + k], i, 0)),
        ),
        out_shape=SDS((4, hr, cols), BF16),
        compiler_params=_cp(("parallel", "parallel")),
    )(c_arr, g, r)


def _chip_sum(g, r, q, kc_arr):
    _, _, hr, cols = g.shape
    tr = min(256, hr)

    def body(kc_ref, g_ref, r_ref, q_ref, o_ref):
        s = g_ref[0, 0] + r_ref[0]
        for j in range(3):
            s = s + q_ref[j].astype(F32)
        o_ref[...] = s

    return pl.pallas_call(
        body,
        name="chip_sum",
        grid_spec=pltpu.PrefetchScalarGridSpec(
            num_scalar_prefetch=1,
            grid=(hr // tr,),
            in_specs=[pl.BlockSpec((1, 1, tr, cols), lambda i, kc: (kc[0], kc[1], i, 0)),
                      pl.BlockSpec((1, tr, cols), lambda i, kc: (kc[0], i, 0)),
                      pl.BlockSpec((3, tr, cols), lambda i, kc: (0, i, 0))],
            out_specs=pl.BlockSpec((tr, cols), lambda i, kc: (kc[1] * (hr // tr) + i, 0)),
        ),
        out_shape=SDS((2 * hr, cols), F32),
        compiler_params=_cp(("parallel",)),
    )(kc_arr, g, r, q)


def _pair_share(blocks):
    n = len(blocks)

    def body(*refs):
        ins, outs = refs[:n], refs[n:2 * n]
        ssem, rsem = refs[2 * n:]
        x, y, c, _ = _place()
        cps = []
        for w in range(n):
            hr = ins[w].shape[0] // 2
            rows = pl.ds(pl.multiple_of(c * hr, 8), hr)
            cp = _remote(ins[w].at[rows, :], outs[w].at[rows, :], ssem.at[w], rsem.at[w], (x, y, 1 - c))
            cp.start()
            cps.append(cp)
        for w in range(n):
            hr = ins[w].shape[0] // 2
            other = outs[w].at[pl.ds(pl.multiple_of((1 - c) * hr, 8), hr), :]
            _remote(other, other, ssem.at[w], rsem.at[w], (x, y, 1 - c)).wait_recv()
        for cp in cps:
            cp.wait_send()

    dma = pltpu.SemaphoreType.DMA
    return pl.pallas_call(
        body,
        name="pair_share",
        in_specs=[ANY] * n,
        out_specs=[ANY] * n,
        out_shape=[SDS(b.shape, b.dtype) for b in blocks],
        input_output_aliases={w: w for w in range(n)},
        scratch_shapes=[dma((n,)), dma((n,))],
    )(*blocks)


class _GradReducer:
    def __init__(self, c_arr, kc_arr):
        self.c_arr, self.kc_arr = c_arr, kc_arr
        self.g, self.r, self.ps, self.q = {}, {}, {}, {}

    def add(self, k, g4):
        self.g[k] = g4.reshape(4, 2, g4.shape[1] // 2, g4.shape[2])

    def s1(self, names):
        return _PairExchangeComm([self.g[k].reshape(4, -1, self.g[k].shape[3]) for k in names])

    def s1_done(self, names, got):
        for k, r in zip(names, got):
            self.r[k] = r
            self.ps[k] = _pair_sum(self.g[k], r, self.c_arr)

    def s3(self, names):
        return _ChipExchangeComm([self.ps[k] for k in names])

    def s3_done(self, names, got):
        self.q.update(zip(names, got))

    def finish(self, order):
        rest = [k for k in order if k not in self.r]
        if rest:
            self.s1_done(rest, _comm_only("pair_exchange", self.s1(rest)))
        rest = [k for k in order if k not in self.q]
        if rest:
            self.s3_done(rest, _comm_only("chip_exchange", self.s3(rest)))
        blocks = [_chip_sum(self.g[k], self.r[k], self.q[k], self.kc_arr) for k in order]
        return dict(zip(order, _pair_share(blocks)))


def _all_reduce_small(buf):
    rows = buf.shape[0]
    hr = rows // 2
    assert hr % SUBLANES == 0

    def body(in_ref, o_ref, sib, pair, slots, ssem, rsem):
        x, y, c, others = _place()
        me, sibling = 2 * x + y, (x, y, 1 - c)
        mine = pl.ds(pl.multiple_of(c * hr, SUBLANES), hr)
        theirs = pl.ds(pl.multiple_of((1 - c) * hr, SUBLANES), hr)
        first = _remote(in_ref, sib, ssem.at[0], rsem.at[0], sibling)
        first.start()
        first.wait()
        pair[...] = in_ref[...] + sib[...]
        slots[me] = pair[mine, :]
        cps = [_remote(pair.at[mine, :], slots.at[me], ssem.at[1 + r], rsem.at[1 + r], (ox, oy, c))
               for r, (ox, oy) in enumerate(others)]
        for cp in cps:
            cp.start()
        for r, (ox, oy) in enumerate(others):
            _remote(pair.at[mine, :], slots.at[2 * ox + oy], ssem.at[1 + r], rsem.at[1 + r], (ox, oy, c)).wait_recv()
        o_ref[mine, :] = (slots[0] + slots[1]) + (slots[2] + slots[3])
        last = _remote(o_ref.at[mine, :], o_ref.at[mine, :], ssem.at[4], rsem.at[4], sibling)
        last.start()
        _remote(o_ref.at[theirs, :], o_ref.at[theirs, :], ssem.at[4], rsem.at[4], sibling).wait_recv()
        last.wait_send()
        for cp in cps:
            cp.wait_send()

    dma = pltpu.SemaphoreType.DMA
    return pl.pallas_call(
        body,
        name="all_reduce_small",
        in_specs=[pl.BlockSpec(memory_space=pltpu.VMEM)],
        out_specs=pl.BlockSpec(memory_space=pltpu.VMEM),
        out_shape=SDS(buf.shape, F32),
        scratch_shapes=[pltpu.VMEM((rows, LANES), F32), pltpu.VMEM((rows, LANES), F32),
                        pltpu.VMEM((4, hr, LANES), F32), dma((5,)), dma((5,))],
        compiler_params=pltpu.CompilerParams(vmem_limit_bytes=VMEM_LIMIT),
    )(buf)


def _adam_fn(w, g, m, v):
    m2 = ADAM_B1 * m + (1.0 - ADAM_B1) * g
    v2 = ADAM_B2 * v + (1.0 - ADAM_B2) * (g * g)
    m_hat = m2 / (1.0 - ADAM_B1 ** ADAM_STEP)
    v_hat = v2 / (1.0 - ADAM_B2 ** ADAM_STEP)
    return (-ADAM_LR * (m_hat / (jnp.sqrt(v_hat) + ADAM_EPS) + ADAM_WD * w), m2, v2), ()


def _adamw(w, g, m, v, name, tr=256):
    cols = w.shape[1]
    return _rowwise(_adam_fn, [(w, cols, 0), (g, cols, 0), (m, cols, 0), (v, cols, 0)], [],
                    [(cols, F32)] * 3, [], tr=tr, name=name)


BIG = ("w_in", "w_glu", "w_branch", "w_out", "w_up", "w_down")
COL_SHARDED = ("w_in", "w_glu", "w_up")
SMALL = ("norm_mix_pre", "norm_mix_post", "norm_mlp_pre", "norm_mlp_post", "sinks", "lam_re", "lam_im", "log_dt",
         "b_re", "b_im", "c_re", "c_im", "d_skip")
WEIGHTS = ("norm_mix_pre", "norm_mix_post", "norm_mlp_pre", "norm_mlp_post", "w_in", "sinks", "lam_re", "lam_im",
           "log_dt", "b_re", "b_im", "c_re", "c_im", "d_skip", "w_glu", "w_branch", "w_out", "w_up", "w_down")


def _flat_small(vals, extra):
    flat = jnp.concatenate([vals[k].reshape(-1) for k in SMALL] + [extra.reshape(-1)])
    rows = -(-flat.shape[0] // (SUBLANES * LANES)) * SUBLANES
    return jnp.pad(flat, (0, rows * LANES - flat.shape[0])).reshape(rows, LANES)


def kernel(x, norm_mix_pre, norm_mix_post, norm_mlp_pre, norm_mlp_post, w_in, sinks, lam_re, lam_im, log_dt, b_re, b_im, c_re, c_im, d_skip, w_glu, w_branch, w_out, w_up, w_down, loss_target, m_norm_mix_pre, m_norm_mix_post, m_norm_mlp_pre, m_norm_mlp_post, m_w_in, m_sinks, m_lam_re, m_lam_im, m_log_dt, m_b_re, m_b_im, m_c_re, m_c_im, m_d_skip, m_w_glu, m_w_branch, m_w_out, m_w_up, m_w_down, v_norm_mix_pre, v_norm_mix_post, v_norm_mlp_pre, v_norm_mlp_post, v_w_in, v_sinks, v_lam_re, v_lam_im, v_log_dt, v_b_re, v_b_im, v_c_re, v_c_im, v_d_skip, v_w_glu, v_w_branch, v_w_out, v_w_up, v_w_down):
    w = dict(norm_mix_pre=norm_mix_pre, norm_mix_post=norm_mix_post, norm_mlp_pre=norm_mlp_pre, norm_mlp_post=norm_mlp_post,
             w_in=w_in, sinks=sinks, lam_re=lam_re, lam_im=lam_im, log_dt=log_dt, b_re=b_re, b_im=b_im, c_re=c_re,
             c_im=c_im, d_skip=d_skip, w_glu=w_glu, w_branch=w_branch, w_out=w_out, w_up=w_up, w_down=w_down)
    m = dict(norm_mix_pre=m_norm_mix_pre, norm_mix_post=m_norm_mix_post, norm_mlp_pre=m_norm_mlp_pre,
             norm_mlp_post=m_norm_mlp_post, w_in=m_w_in, sinks=m_sinks, lam_re=m_lam_re, lam_im=m_lam_im,
             log_dt=m_log_dt, b_re=m_b_re, b_im=m_b_im, c_re=m_c_re, c_im=m_c_im, d_skip=m_d_skip, w_glu=m_w_glu,
             w_branch=m_w_branch, w_out=m_w_out, w_up=m_w_up, w_down=m_w_down)
    v = dict(norm_mix_pre=v_norm_mix_pre, norm_mix_post=v_norm_mix_post, norm_mlp_pre=v_norm_mlp_pre,
             norm_mlp_post=v_norm_mlp_post, w_in=v_w_in, sinks=v_sinks, lam_re=v_lam_re, lam_im=v_lam_im,
             log_dt=v_log_dt, b_re=v_b_re, b_im=v_b_im, c_re=v_c_re, c_im=v_c_im, d_skip=v_d_skip, w_glu=v_w_glu,
             w_branch=v_w_branch, w_out=v_w_out, w_up=v_w_up, w_down=v_w_down)
    xi, yi, ci = lax.axis_index("x"), lax.axis_index("y"), lax.axis_index("c")

    k_arr = jnp.stack([2 * xi + yi]).astype(jnp.int32)
    slot = {k: _cast_into_slot(w[k][0], k_arr) for k in BIG}

    def whole(k, g4):
        if k in COL_SHARDED:
            return jnp.concatenate([g4[j] for j in range(4)], axis=1)
        return g4.reshape(4 * g4.shape[1], g4.shape[2])

    wi = _comm_only("gather_w_in", _GatherComm([slot["w_in"]]))[0]
    w_a = _assemble_w_a(wi)
    hosted = (("w_glu", "w_branch", "w_out"), ("w_up",), ("w_down",))
    comms = [None] + [_GatherComm([slot[k] for k in names]) for names in hosted]

    def late(*got):
        f = {k: whole(k, g4) for names, res in zip(hosted, got) for k, g4 in zip(names, res)}
        return f["w_glu"], f["w_branch"][:Q_W], f["w_branch"][Q_W:], f["w_out"], f["w_up"], f["w_down"]

    s5w = (lam_re[0], lam_im[0], log_dt[0], b_re[0], b_im[0], c_re[0], c_im[0], d_skip[0])
    reducer = _GradReducer(
        jnp.stack([ci, 2 * (1 - xi) + yi, 2 * xi + (1 - yi), 2 * (1 - xi) + (1 - yi)]).astype(jnp.int32),
        jnp.stack([2 * xi + yi, ci]).astype(jnp.int32))
    loss_part, dx, small, _ = _local_step(
        x[0], loss_target[0], (norm_mix_pre, norm_mix_post, norm_mlp_pre, norm_mlp_post),
        w_a, sinks, s5w, comms, lambda _: _assemble_w_g(wi), late, reducer)
    grads = reducer.finish(BIG)

    red = _all_reduce_small(_flat_small(small, loss_part)).reshape(-1)
    off = 0
    for k in SMALL:
        n = math.prod(w[k].shape)
        grads[k] = red[off:off + n].reshape(w[k].shape[1:])
        off += n
    loss = red[off]

    delta, new_m, new_v = {}, {}, {}
    for k in BIG:
        delta[k], new_m[k], new_v[k] = _adamw(w[k][0], grads[k], m[k][0], v[k][0], "adamw_" + k)
    zero = jnp.zeros((), F32)
    fw, fm, fv = (_flat_small({k: t[k] for k in SMALL}, zero) for t in (w, m, v))
    fg = _flat_small(grads, zero)
    sd, sm, sv = _adamw(fw, fg, fm, fv, "adamw_small", tr=fw.shape[0])
    off = 0
    for k in SMALL:
        n = math.prod(w[k].shape)
        delta[k], new_m[k], new_v[k] = (t.reshape(-1)[off:off + n].reshape(w[k].shape[1:]) for t in (sd, sm, sv))
        off += n

    lead = lambda t: t[None]
    return (loss, lead(dx), *[lead(grads[k]) for k in WEIGHTS], *[lead(delta[k]) for k in WEIGHTS],
            *[lead(new_m[k]) for k in WEIGHTS], *[lead(new_v[k]) for k in WEIGHTS])
```

```python
import functools
import math

import jax
import jax.numpy as jnp
from jax import lax
from jax.experimental import pallas as pl
from jax.experimental.pallas import tpu as pltpu

F32 = jnp.float32
BF16 = jnp.bfloat16
SDS = jax.ShapeDtypeStruct

D_MODEL = 2048
HEAD_DIM = 64
N_Q_HEADS = 16
ATT_BLOCK = 128
ROT_DIM = 16
ROPE_THETA = 500000.0
Q_W = 1024
KV_W = 128
SSM_W = 1024
SSM_G = 64
SSM_GC = 16
SSM_P = 64
N_STATE = SSM_G * SSM_P
LANES = 128
SUBLANES = 8
N_LG = N_STATE // LANES
N_JB = 8
LG_PER_JB = N_LG // N_JB
D_FF = 8192
ZA_W = Q_W + 2 * KV_W + SSM_W
EPS = 1e-6
S5_CHUNK = 2048
S5_SEG = S5_CHUNK // SUBLANES
VMEM_LIMIT = 56 * 1024 * 1024
NEG = -1e30

ADAM_LR = 0.001
ADAM_B1 = 0.9
ADAM_B2 = 0.999
ADAM_EPS = 1e-08
ADAM_WD = 0.01
ADAM_STEP = 10

MESH = pl.DeviceIdType.MESH


def _cp(sem):
    return pltpu.CompilerParams(dimension_semantics=sem, vmem_limit_bytes=VMEM_LIMIT)


ANY = pl.BlockSpec(memory_space=pl.ANY)


def _place():
    x, y, c = lax.axis_index("x"), lax.axis_index("y"), lax.axis_index("c")
    others = [(1 - x, y), (x, 1 - y), (1 - x, 1 - y)]
    return x, y, c, others


def _remote(src, dst, ssem, rsem, to):
    return pltpu.make_async_remote_copy(src_ref=src, dst_ref=dst, send_sem=ssem, recv_sem=rsem,
                                        device_id=to, device_id_type=MESH)


class _GatherComm:
    aliased = True

    def __init__(self, slotted):
        self.arrs = list(slotted)
        self.n = len(self.arrs)
        dma = pltpu.SemaphoreType.DMA
        self.scratch = [dma((3 * self.n,)) for _ in range(4)]
        self.out_shape = [SDS(s.shape, s.dtype) for s in self.arrs]

    @staticmethod
    def _half(ref, hc):
        hr = ref.shape[1] // 2
        return pl.ds(pl.multiple_of(hc * hr, 16), hr)

    def _sends(self, ins, outs, sems):
        ssem, rsem, _, _ = sems
        x, y, c, others = _place()
        me = 2 * x + y
        return [_remote(ins[w].at[me, self._half(ins[w], c), :], outs[w].at[me, self._half(ins[w], c), :],
                        ssem.at[3 * w + r], rsem.at[3 * w + r], (ox, oy, c))
                for w in range(self.n) for r, (ox, oy) in enumerate(others)]

    def start(self, ins, outs, sems):
        for cp in self._sends(ins, outs, sems):
            cp.start()

    def finish(self, ins, outs, sems):
        ssem, rsem, fs_sem, fr_sem = sems
        x, y, c, others = _place()
        sib = (x, y, 1 - c)
        passes = []
        for w in range(self.n):
            for r, (ox, oy) in enumerate(others):
                got = outs[w].at[2 * ox + oy, self._half(ins[w], c), :]
                _remote(got, got, ssem.at[3 * w + r], rsem.at[3 * w + r], (ox, oy, c)).wait_recv()
                cp = _remote(got, got, fs_sem.at[3 * w + r], fr_sem.at[3 * w + r], sib)
                cp.start()
                passes.append(cp)
        for w in range(self.n):
            for r, (ox, oy) in enumerate(others):
                got = outs[w].at[2 * ox + oy, self._half(ins[w], 1 - c), :]
                _remote(got, got, fs_sem.at[3 * w + r], fr_sem.at[3 * w + r], sib).wait_recv()
        for cp in self._sends(ins, outs, sems) + passes:
            cp.wait_send()


class _PairExchangeComm:
    aliased = False

    def __init__(self, grads):
        self.arrs = list(grads)
        self.n = len(self.arrs)
        dma = pltpu.SemaphoreType.DMA
        self.scratch = [dma((self.n,)), dma((self.n,))]
        self.out_shape = [SDS((4, g.shape[1] // 2, g.shape[2]), g.dtype) for g in self.arrs]

    def _copies(self, ins, outs, sems):
        ssem, rsem = sems
        x, y, c, _ = _place()
        cps = []
        for w in range(self.n):
            hr = ins[w].shape[1] // 2
            src = ins[w].at[:, pl.ds(pl.multiple_of((1 - c) * hr, 8), hr), :]
            cps.append(_remote(src, outs[w], ssem.at[w], rsem.at[w], (x, y, 1 - c)))
        return cps

    def start(self, ins, outs, sems):
        for cp in self._copies(ins, outs, sems):
            cp.start()

    def finish(self, ins, outs, sems):
        for cp in self._copies(ins, outs, sems):
            cp.wait()


class _ChipExchangeComm:
    aliased = False

    def __init__(self, psums):
        self.arrs = list(psums)
        self.n = len(self.arrs)
        dma = pltpu.SemaphoreType.DMA
        self.scratch = [dma((3 * self.n,)), dma((3 * self.n,))]
        self.out_shape = [SDS((3,) + p.shape[1:], p.dtype) for p in self.arrs]

    def _copies(self, ins, outs, sems):
        ssem, rsem = sems
        x, y, c, others = _place()
        return [_remote(ins[w].at[2 * ox + oy], outs[w].at[r], ssem.at[3 * w + r], rsem.at[3 * w + r], (ox, oy, c))
                for w in range(self.n) for r, (ox, oy) in enumerate(others)]

    def start(self, ins, outs, sems):
        for cp in self._copies(ins, outs, sems):
            cp.start()

    def finish(self, ins, outs, sems):
        for cp in self._copies(ins, outs, sems):
            cp.wait()


def _comm_only(name, comm):
    n = comm.n

    def body(*refs):
        ins, outs, sems = refs[:n], refs[n:2 * n], refs[2 * n:]
        comm.start(ins, outs, sems)
        comm.finish(ins, outs, sems)

    return pl.pallas_call(
        body, name=name, in_specs=[ANY] * n, out_specs=[ANY] * n, out_shape=comm.out_shape,
        input_output_aliases={w: w for w in range(n)} if comm.aliased else {},
        scratch_shapes=comm.scratch)(*comm.arrs)


def _call(name, body, grid, in_specs, out_specs, out_shape, scratch, dims, args, comm=None):
    if comm is None:
        return pl.pallas_call(body, name=name, grid=grid, in_specs=in_specs, out_specs=out_specs, out_shape=out_shape,
                              scratch_shapes=scratch, compiler_params=_cp(dims))(*args)
    ni, no, ns, n = len(in_specs), len(out_shape), len(scratch), comm.n

    def hosted(*refs):
        ins, cin = refs[:ni], refs[ni:ni + n]
        outs, cout = refs[ni + n:ni + n + no], refs[ni + n + no:ni + 2 * n + no]
        scr, sems = refs[ni + 2 * n + no:ni + 2 * n + no + ns], refs[ni + 2 * n + no + ns:]
        ids = [pl.program_id(d) for d in range(len(grid))]
        first = functools.reduce(jnp.logical_and, [i == 0 for i in ids])
        last = functools.reduce(jnp.logical_and, [i == g - 1 for i, g in zip(ids, grid)])

        @pl.when(first)
        def _():
            comm.start(cin, cout, sems)

        body(*ins, *outs, *scr)

        @pl.when(last)
        def _():
            comm.finish(cin, cout, sems)

    return pl.pallas_call(
        hosted, name=name, grid=grid, in_specs=list(in_specs) + [ANY] * n, out_specs=list(out_specs) + [ANY] * n,
        out_shape=list(out_shape) + comm.out_shape,
        input_output_aliases={ni + w: no + w for w in range(n)} if comm.aliased else {},
        scratch_shapes=list(scratch) + comm.scratch, compiler_params=_cp(("arbitrary",) * len(grid)))(*args, *comm.arrs)


def _mm(a, b, *, mode, out_dtype, tm, tn, tk, name, a_fn=None, epi=None, extras=(), comm=None, shard_cols=None):
    if mode == "nn":
        (M, K), (K2, N) = a.shape, b.shape
    elif mode == "nt":
        (M, K), (N, K2) = a.shape, b.shape
    else:
        (K, M), (K2, N) = a.shape, b.shape
    assert K == K2, (a.shape, b.shape, mode)
    tm, tn, tk = min(tm, M), min(tn, N), min(tk, K)
    assert M % tm == 0 and N % tn == 0 and K % tk == 0, (M, N, K, tm, tn, tk)
    nk = K // tk
    if mode == "tn":
        a_spec = pl.BlockSpec((tk, tm), lambda i, j, k: (k, i))
        ca = 0
    else:
        a_spec = pl.BlockSpec((tm, tk), lambda i, j, k: (i, k))
        ca = 1
    if mode == "nt":
        b_spec = pl.BlockSpec((tn, tk), lambda i, j, k: (j, k))
        cb = 1
    else:
        b_spec = pl.BlockSpec((tk, tn), lambda i, j, k: (k, j))
        cb = 0
    dims = (((ca,), (cb,)), ((), ()))
    ne = len(extras)

    def body(a_ref, b_ref, *rest):
        ex = rest[:ne]
        o_ref = rest[ne]
        av = a_ref[...]
        if a_fn is not None:
            av = a_fn(av.astype(F32))
        p = lax.dot_general(av.astype(BF16), b_ref[...].astype(BF16), dims, preferred_element_type=F32)

        def fin(v):
            if epi is not None:
                v = epi(v, *[e[...] for e in ex])
            o_ref[...] = v.astype(out_dtype).reshape(o_ref.shape)

        if nk == 1:
            fin(p)
        else:
            acc = rest[ne + 1]
            k = pl.program_id(2)

            @pl.when(k == 0)
            def _():
                acc[...] = p

            @pl.when(k > 0)
            def _():
                acc[...] += p

            @pl.when(k == nk - 1)
            def _():
                fin(acc[...])

    if shard_cols is None:
        o_spec, o_shape = pl.BlockSpec((tm, tn), lambda i, j, k: (i, j)), SDS((M, N), out_dtype)
    else:
        per = shard_cols // tn
        assert shard_cols % tn == 0 and N % shard_cols == 0
        o_spec = pl.BlockSpec((1, tm, tn), lambda i, j, k: (lax.div(j, per), i, lax.rem(j, per)))
        o_shape = SDS((N // shard_cols, M, shard_cols), out_dtype)
    res = _call(name, body, (M // tm, N // tn, nk),
                [a_spec, b_spec] + [pl.BlockSpec((tm, tn), lambda i, j, k: (i, j)) for _ in extras],
                [o_spec], [o_shape],
                [pltpu.VMEM((tm, tn), F32)] if nk > 1 else [], ("parallel", "parallel", "arbitrary"),
                (a, b, *extras), comm)
    return res[0] if comm is None else (res[0], res[1:])


def _rowwise(fn, rows, bcasts, outs, accs, *, tr, name):
    T = rows[0][0].shape[0]
    tr = min(tr, T)
    assert T % tr == 0
    nr, nb, no, na = len(rows), len(bcasts), len(outs), len(accs)
    in_specs = [pl.BlockSpec((tr, w), functools.partial(lambda i, c: (i, c), c=cb)) for (_, w, cb) in rows]
    in_specs += [pl.BlockSpec(b.shape, lambda i: (0, 0)) for b in bcasts]
    out_shape = [SDS((T, w), dt) for (w, dt) in outs] + [SDS(s, F32) for s in accs]
    out_specs = [pl.BlockSpec((tr, w), lambda i: (i, 0)) for (w, _) in outs]
    out_specs += [pl.BlockSpec(s, lambda i: (0, 0)) for s in accs]

    def body(*refs):
        ins = [r[...].astype(F32) for r in refs[:nr + nb]]
        o_refs = refs[nr + nb:nr + nb + no]
        a_refs = refs[nr + nb + no:]
        ro, ao = fn(*ins)
        for r, v in zip(o_refs, ro):
            r[...] = v.astype(r.dtype)
        if na:
            @pl.when(pl.program_id(0) == 0)
            def _():
                for r in a_refs:
                    r[...] = jnp.zeros(r.shape, F32)

            for r, v in zip(a_refs, ao):
                r[...] += v

    res = pl.pallas_call(
        body,
        name=name,
        grid=(T // tr,),
        in_specs=in_specs,
        out_specs=out_specs,
        out_shape=out_shape,
        compiler_params=_cp(("arbitrary",) if na else ("parallel",)),
    )(*[r[0] for r in rows], *bcasts)
    return res


def _rms(v):
    r = lax.rsqrt(jnp.mean(v * v, axis=-1, keepdims=True) + EPS)
    return v * r, r


def _rms_bwd(dy, xn, r, g):
    dxn = dy * g
    dv = r * (dxn - xn * jnp.mean(dxn * xn, axis=-1, keepdims=True))
    return dv, jnp.sum(dy * xn, axis=0, keepdims=True)


def _sig(v):
    return 1.0 / (1.0 + jnp.exp(-v))


_GELU_C = math.sqrt(2.0 / math.pi)


def _gelu(v):
    return 0.5 * v * (1.0 + jnp.tanh(_GELU_C * (v + 0.044715 * v * v * v)))


def _gelu_grad(v):
    t = jnp.tanh(_GELU_C * (v + 0.044715 * v * v * v))
    return 0.5 * (1.0 + t) + 0.5 * v * (1.0 - t * t) * _GELU_C * (1.0 + 3.0 * 0.044715 * v * v)


def _rope(v, c, s, sign):
    w = v.shape[1]
    m = lax.broadcasted_iota(jnp.int32, v.shape, 1) % HEAD_DIM
    p = jnp.where(m < ROT_DIM // 2, -pltpu.roll(v, w - ROT_DIM // 2, 1), pltpu.roll(v, ROT_DIM // 2, 1))
    return v * c + sign * (p * s)


def _rope_tables(T):
    half = ROT_DIM // 2
    inv = ROPE_THETA ** (-jnp.arange(half, dtype=F32) * 2.0 / ROT_DIM)
    ang = jnp.arange(T).astype(F32)[:, None] * inv[None, :]
    cos, sin = jnp.cos(ang), jnp.sin(ang)
    one = jnp.ones((T, HEAD_DIM - ROT_DIM), F32)
    c64 = jnp.concatenate([cos, cos, one], axis=1)
    s64 = jnp.concatenate([sin, sin, 0.0 * one], axis=1)
    return jnp.tile(c64, (1, 2)), jnp.tile(s64, (1, 2))


def _dup_half(m, lo):
    lane = lax.broadcasted_iota(jnp.int32, m.shape, 1)
    sw = pltpu.roll(m, HEAD_DIM, 1)
    return jnp.where(lane < HEAD_DIM, m, sw) if lo else jnp.where(lane >= HEAD_DIM, m, sw)


def _attn_mask(i):
    qi = lax.broadcasted_iota(jnp.int32, (ATT_BLOCK, 2 * ATT_BLOCK), 0)
    kj = lax.broadcasted_iota(jnp.int32, (ATT_BLOCK, 2 * ATT_BLOCK), 1)
    rel = qi + ATT_BLOCK - kj
    return (rel >= 0) & (rel < ATT_BLOCK) & ((kj >= ATT_BLOCK) | (i > 0))


_NT = (((1,), (1,)), ((), ()))
_TN = (((0,), (0,)), ((), ()))


def _stack_heads(m):
    lane = lax.broadcasted_iota(jnp.int32, m.shape, 1)
    zero = jnp.zeros_like(m)
    return jnp.concatenate([jnp.where(lane < HEAD_DIM, m, zero), jnp.where(lane >= HEAD_DIM, m, zero)], axis=0)


def _pair_probs(q2, k2, ok2, sink_lo, sink_hi):
    qs = _stack_heads(q2)
    s = lax.dot_general(qs, k2, _NT, preferred_element_type=F32)
    s = jnp.where(ok2, s, NEG)
    row = lax.broadcasted_iota(jnp.int32, (2 * ATT_BLOCK, 1), 0)
    sink = jnp.where(row < ATT_BLOCK, sink_lo, sink_hi)
    m = jnp.maximum(jnp.max(s, axis=1, keepdims=True), sink)
    e = jnp.exp(s - m)
    es = jnp.exp(sink - m)
    inv = 1.0 / (jnp.sum(e, axis=1, keepdims=True) + es)
    return e * inv, es * inv, qs


def _attn_fwd(za, cos, sin, sinks, comm=None):
    T = za.shape[0]
    nb = T // ATT_BLOCK
    kvb = Q_W // (2 * KV_W)

    def body(sink_ref, q_ref, kvp_ref, kvc_ref, cc_ref, sc_ref, cp_ref, sp_ref, o_ref):
        i = pl.program_id(0)
        cc, sc, cp, sp = cc_ref[...], sc_ref[...], cp_ref[...], sp_ref[...]
        q = (_rope(q_ref[...], jnp.tile(cc, (1, 8)), jnp.tile(sc, (1, 8)), 1.0) * 0.125).astype(BF16)
        kvp, kvc = kvp_ref[...], kvc_ref[...]
        k = jnp.concatenate([_rope(kvp[:, :KV_W], cp, sp, 1.0), _rope(kvc[:, :KV_W], cc, sc, 1.0)], axis=0).astype(BF16)
        v = jnp.concatenate([kvp[:, KV_W:], kvc[:, KV_W:]], axis=0).astype(BF16)
        ok = _attn_mask(i)
        ok2 = jnp.concatenate([ok, ok], axis=0)
        lane = lax.broadcasted_iota(jnp.int32, (ATT_BLOCK, LANES), 1)
        for kvh in range(2):
            k2 = _dup_half(k, kvh == 0)
            v2 = _dup_half(v, kvh == 0)
            for pair in range(4):
                c0 = (kvh * 4 + pair) * LANES
                q2 = q[:, c0:c0 + LANES]
                p, _, _ = _pair_probs(q2, k2, ok2, sink_ref[0, 2 * (kvh * 4 + pair)], sink_ref[0, 2 * (kvh * 4 + pair) + 1])
                o = jnp.dot(p.astype(BF16), v2, preferred_element_type=F32)
                o_ref[:, c0:c0 + LANES] = jnp.where(lane < HEAD_DIM, o[:ATT_BLOCK], o[ATT_BLOCK:]).astype(BF16)

    blk = lambda w, f: pl.BlockSpec((ATT_BLOCK, w), f)
    res = _call(
        "attn_fwd", body, (nb,),
        [
            pl.BlockSpec(memory_space=pltpu.SMEM),
            blk(Q_W, lambda i: (i, 0)),
            blk(2 * KV_W, lambda i: (jnp.maximum(i - 1, 0), kvb)),
            blk(2 * KV_W, lambda i: (i, kvb)),
            blk(LANES, lambda i: (i, 0)),
            blk(LANES, lambda i: (i, 0)),
            blk(LANES, lambda i: (jnp.maximum(i - 1, 0), 0)),
            blk(LANES, lambda i: (jnp.maximum(i - 1, 0), 0)),
        ],
        [blk(Q_W, lambda i: (i, 0))], [SDS((T, Q_W), BF16)], [], ("parallel",),
        (sinks, za, za, za, cos, sin, cos, sin), comm)
    return res[0] if comm is None else (res[0], res[1:])


def _attn_bwd(za, cos, sin, sinks, o, do, comm=None):
    T = za.shape[0]
    nb = T // ATT_BLOCK
    kvb = Q_W // (2 * KV_W)

    def body(sink_ref, q_ref, kvp_ref, kvc_ref, cc_ref, sc_ref, cp_ref, sp_ref, o_ref, do_ref,
             dq_ref, dkv_ref, dsk_ref, carry, dqs):
        i = pl.program_id(0)

        @pl.when(i == 0)
        def _():
            carry[...] = jnp.zeros(carry.shape, F32)
            dsk_ref[...] = jnp.zeros(dsk_ref.shape, F32)

        @pl.when(i < nb)
        def _():
            cc, sc, cp, sp = cc_ref[...], sc_ref[...], cp_ref[...], sp_ref[...]
            ccq, scq = jnp.tile(cc, (1, 8)), jnp.tile(sc, (1, 8))
            q = (_rope(q_ref[...], ccq, scq, 1.0) * 0.125).astype(BF16)
            kvp, kvc = kvp_ref[...], kvc_ref[...]
            k = jnp.concatenate([_rope(kvp[:, :KV_W], cp, sp, 1.0), _rope(kvc[:, :KV_W], cc, sc, 1.0)], axis=0).astype(BF16)
            v = jnp.concatenate([kvp[:, KV_W:], kvc[:, KV_W:]], axis=0).astype(BF16)
            ok = _attn_mask(i)
            ok2 = jnp.concatenate([ok, ok], axis=0)
            lane = lax.broadcasted_iota(jnp.int32, (ATT_BLOCK, LANES), 1)
            lane_s = lax.broadcasted_iota(jnp.int32, (1, LANES), 1)
            dsk = jnp.zeros((1, LANES), F32)
            dkt_h, dvt_h = [], []
            for kvh in range(2):
                k2 = _dup_half(k, kvh == 0)
                v2 = _dup_half(v, kvh == 0)
                dkt = jnp.zeros((LANES, 2 * ATT_BLOCK), F32)
                dvt = jnp.zeros((LANES, 2 * ATT_BLOCK), F32)
                for pair in range(4):
                    h = 2 * (kvh * 4 + pair)
                    c0 = (kvh * 4 + pair) * LANES
                    do2 = do_ref[:, c0:c0 + LANES]
                    prod = do2.astype(F32) * o_ref[:, c0:c0 + LANES].astype(F32)
                    d_lo = jnp.sum(jnp.where(lane < HEAD_DIM, prod, 0.0), axis=1, keepdims=True)
                    d_hi = jnp.sum(jnp.where(lane >= HEAD_DIM, prod, 0.0), axis=1, keepdims=True)
                    delta = jnp.concatenate([d_lo, d_hi], axis=0)
                    p, p_sink, qs = _pair_probs(q[:, c0:c0 + LANES], k2, ok2, sink_ref[0, h], sink_ref[0, h + 1])
                    dos = _stack_heads(do2)
                    t = p_sink * delta
                    dsk = dsk - jnp.where(lane_s == h, jnp.sum(t[:ATT_BLOCK]), 0.0) \
                              - jnp.where(lane_s == h + 1, jnp.sum(t[ATT_BLOCK:]), 0.0)
                    dp = lax.dot_general(dos, v2, _NT, preferred_element_type=F32)
                    ds = (p * (dp - delta)).astype(BF16)
                    dqp = jnp.dot(ds, k2, preferred_element_type=F32)
                    dqs[:, c0:c0 + LANES] = jnp.where(lane < HEAD_DIM, dqp[:ATT_BLOCK], dqp[ATT_BLOCK:]) * 0.125
                    dkt = dkt + lax.dot_general(qs, ds, _TN, preferred_element_type=F32)
                    dvt = dvt + lax.dot_general(dos, p.astype(BF16), _TN, preferred_element_type=F32)
                dkt_h.append(dkt[:HEAD_DIM] + dkt[HEAD_DIM:])
                dvt_h.append(dvt[:HEAD_DIM] + dvt[HEAD_DIM:])
            dk = jnp.concatenate(dkt_h, axis=0).T
            dv = jnp.concatenate(dvt_h, axis=0).T
            dq_ref[...] = _rope(dqs[...], ccq, scq, -1.0).astype(dq_ref.dtype)
            dkp = _rope(dk[:ATT_BLOCK], cp, sp, -1.0)
            dkc = _rope(dk[ATT_BLOCK:], cc, sc, -1.0)
            dkv_ref[...] = (carry[...] + jnp.concatenate([dkp, dv[:ATT_BLOCK]], axis=1)).astype(dkv_ref.dtype)
            carry[...] = jnp.concatenate([dkc, dv[ATT_BLOCK:]], axis=1)
            dsk_ref[...] += dsk

        @pl.when(i == nb)
        def _():
            dkv_ref[...] = carry[...].astype(dkv_ref.dtype)

    blk = lambda w, f: pl.BlockSpec((ATT_BLOCK, w), f)
    cur = lambda i: jnp.minimum(i, nb - 1)
    prv = lambda i: jnp.maximum(jnp.minimum(i, nb - 1) - 1, 0)
    res = _call(
        "attn_bwd", body, (nb + 1,),
        [
            pl.BlockSpec(memory_space=pltpu.SMEM),
            blk(Q_W, lambda i: (cur(i), 0)),
            blk(2 * KV_W, lambda i: (prv(i), kvb)),
            blk(2 * KV_W, lambda i: (cur(i), kvb)),
            blk(LANES, lambda i: (cur(i), 0)),
            blk(LANES, lambda i: (cur(i), 0)),
            blk(LANES, lambda i: (prv(i), 0)),
            blk(LANES, lambda i: (prv(i), 0)),
            blk(Q_W, lambda i: (cur(i), 0)),
            blk(Q_W, lambda i: (cur(i), 0)),
        ],
        [
            blk(Q_W, lambda i: (cur(i), 0)),
            blk(2 * KV_W, lambda i: (jnp.maximum(i - 1, 0), 0)),
            pl.BlockSpec((1, LANES), lambda i: (0, 0)),
        ],
        [SDS((T, Q_W), BF16), SDS((T, 2 * KV_W), BF16), SDS((1, LANES), F32)],
        [pltpu.VMEM((ATT_BLOCK, 2 * KV_W), F32), pltpu.VMEM((ATT_BLOCK, Q_W), F32)],
        ("arbitrary",), (sinks, za, za, za, cos, sin, cos, sin, o, do), comm)
    return res if comm is None else (res[:3], res[3:])


def _s5_discretize(lam_re, lam_im, log_dt, b_re, b_im):
    dt = jnp.exp(log_dt)[:, None]
    mag = jnp.exp(lam_re * dt)
    a_re, a_im = mag * jnp.cos(lam_im * dt), mag * jnp.sin(lam_im * dt)
    den = lam_re * lam_re + lam_im * lam_im
    nr, ni = a_re - 1.0, a_im
    coef_re = (nr * lam_re + ni * lam_im) / den
    coef_im = (ni * lam_re - nr * lam_im) / den
    bb_re = coef_re[..., None] * b_re - coef_im[..., None] * b_im
    bb_im = coef_re[..., None] * b_im + coef_im[..., None] * b_re
    return a_re, a_im, bb_re, bb_im


def _blockdiag_in(bb):
    x = bb.reshape(N_JB, 8, SSM_P, SSM_GC).transpose(0, 1, 3, 2)
    return (x[:, :, :, None, :] * jnp.eye(8, dtype=bb.dtype)[None, :, None, :, None]).reshape(N_JB, 128, 512)


def _blockdiag_in_extract(m):
    x = m.reshape(N_JB, 8, SSM_GC, 8, SSM_P)
    x = jnp.einsum('jgchp,gh->jgcp', x, jnp.eye(8, dtype=m.dtype))
    return x.transpose(0, 1, 3, 2).reshape(SSM_G, SSM_P, SSM_GC)


def _blockdiag_out(c):
    x = c.reshape(N_JB, 8, SSM_GC, SSM_P).transpose(0, 1, 3, 2)
    return (x[:, :, :, None, :] * jnp.eye(8, dtype=c.dtype)[None, :, None, :, None]).reshape(N_JB, 512, 128)


def _blockdiag_out_extract(m):
    x = m.reshape(N_JB, 8, SSM_P, 8, SSM_GC)
    x = jnp.einsum('jgphc,gh->jgpc', x, jnp.eye(8, dtype=m.dtype))
    return x.transpose(0, 1, 3, 2).reshape(SSM_G, SSM_GC, SSM_P)


def _s5_tables(a_re, a_im):
    ar, ai = a_re.reshape(N_LG, 1, LANES), a_im.reshape(N_LG, 1, LANES)
    pr, pi, n = ar, ai, 1
    while n < S5_SEG:
        pr, pi, n = pr * pr - pi * pi, 2.0 * pr * pi, 2 * n
    assert n == S5_SEG
    bc = lambda v: jnp.broadcast_to(v, (N_LG, SUBLANES, LANES))
    return bc(ar), bc(ai), pr, pi


def _s5_to_time_major(src_ref, dst_ref):
    for t in range(S5_SEG):
        dst_ref[t * SUBLANES:(t + 1) * SUBLANES, :] = src_ref[pl.ds(t, SUBLANES, stride=S5_SEG), :]


def _s5_from_time_major(val, dst_ref):
    for t in range(S5_SEG):
        dst_ref[pl.ds(t, SUBLANES, stride=S5_SEG), :] = val[t * SUBLANES:(t + 1) * SUBLANES, :]


def _tm_rows(t, row0=0):
    return pl.ds(pl.multiple_of(t * SUBLANES + row0, SUBLANES), SUBLANES)


def _s5_scan(src_re, src_im, ar, ai, reverse, start=None, dst=None, dst_row0=0):
    def step(n, carry):
        t = (S5_SEG - 1 - n) if reverse else n
        out = []
        for ll in range(LG_PER_JB):
            xr, xi = carry[2 * ll], carry[2 * ll + 1]
            idx = (ll, _tm_rows(t), slice(None))
            nr = ar[ll] * xr - ai[ll] * xi + src_re[idx]
            ni = ar[ll] * xi + ai[ll] * xr + src_im[idx]
            if dst is not None:
                odx = (ll, _tm_rows(t, dst_row0), slice(None))
                dst[0][odx] = nr
                dst[1][odx] = ni
            out += [nr, ni]
        return tuple(out)
    if start is None:
        init = (jnp.zeros((SUBLANES, LANES), F32),) * (2 * LG_PER_JB)
    else:
        init = tuple(s[ll] for ll in range(LG_PER_JB) for s in start)
    return lax.fori_loop(0, S5_SEG, step, init)


def _s5_fixup(ends, in_re, in_im, mr, mi, s_re, s_im, reverse):
    cr, ci = in_re, in_im
    order = range(SUBLANES - 1, -1, -1) if reverse else range(SUBLANES)
    for s in order:
        s_re[:, s:s + 1, :] = cr
        s_im[:, s:s + 1, :] = ci
        er = jnp.stack([ends[2 * ll][s:s + 1, :] for ll in range(LG_PER_JB)])
        ei = jnp.stack([ends[2 * ll + 1][s:s + 1, :] for ll in range(LG_PER_JB)])
        cr, ci = mr * cr - mi * ci + er, mr * ci + mi * cr + ei
    return cr, ci


def _s5_specs(nc, rev):
    cidx = (lambda c: nc - 1 - c) if rev else (lambda c: c)
    jb = lambda shape: pl.BlockSpec(shape, lambda j, c: (j, 0, 0))
    return cidx, [
        jb((1, LANES, 8 * LANES)),
        jb((1, 8 * LANES, LANES)),
        pl.BlockSpec((1, LANES), lambda j, c: (0, j)),
        jb((LG_PER_JB, SUBLANES, LANES)), jb((LG_PER_JB, SUBLANES, LANES)),
        jb((LG_PER_JB, 1, LANES)), jb((LG_PER_JB, 1, LANES)),
    ]


def _s5_fwd(za, prm, comm=None):
    T = za.shape[0]
    R = S5_CHUNK
    nc = T // R
    ub = (Q_W + 2 * KV_W) // LANES
    _, pspecs = _s5_specs(nc, False)

    def body(u_ref, b_ref, c_ref, d_ref, are_ref, aim_ref, alr_ref, ali_ref,
             yg_ref, x0r_ref, x0i_ref, bur, bui, xsr, xsi, sr, si, xcr, xci, utm, ynat):
        c = pl.program_id(1)

        @pl.when(c == 0)
        def _():
            xcr[...] = jnp.zeros(xcr.shape, F32)
            xci[...] = jnp.zeros(xci.shape, F32)

        _s5_to_time_major(u_ref, utm)
        u = utm[...]
        ub16 = u.astype(BF16)
        bu = jnp.dot(ub16, b_ref[0].astype(BF16), preferred_element_type=F32)
        for ll in range(LG_PER_JB):
            bur[ll] = bu[:, ll * LANES:(ll + 1) * LANES]
            bui[ll] = bu[:, (LG_PER_JB + ll) * LANES:(LG_PER_JB + ll + 1) * LANES]
        ar = [are_ref[ll] for ll in range(LG_PER_JB)]
        ai = [aim_ref[ll] for ll in range(LG_PER_JB)]
        ends = _s5_scan(bur, bui, ar, ai, False)
        in_r, in_i = xcr[...], xci[...]
        x0r_ref[0] = in_r
        x0i_ref[0] = in_i
        out_r, out_i = _s5_fixup(ends, in_r, in_i, alr_ref[...], ali_ref[...], sr, si, False)
        xcr[...] = out_r
        xci[...] = out_i
        _s5_scan(bur, bui, ar, ai, False, start=(sr, si), dst=(xsr, xsi))
        xcat =jnp.concatenate([xsr[ll].astype(BF16) for ll in range(LG_PER_JB)]
                               + [xsi[ll].astype(BF16) for ll in range(LG_PER_JB)], axis=1)
        y = d_ref[...] * u + jnp.dot(xcat, c_ref[0].astype(BF16), preferred_element_type=F32)
        _s5_from_time_major(_gelu(y), ynat)
        yg_ref[...] = ynat[...].astype(BF16)

    st = pl.BlockSpec((1, LG_PER_JB, 1, LANES), lambda j, c: (c, j, 0, 0))
    vm = lambda rows: pltpu.VMEM((LG_PER_JB, rows, LANES), F32)
    res = _call(
        "s5_fwd", body, (N_JB, nc),
        [pl.BlockSpec((R, LANES), lambda j, c: (c, ub + j))] + pspecs,
        [pl.BlockSpec((R, LANES), lambda j, c: (c, j)), st, st],
        [SDS((T, SSM_W), BF16), SDS((nc, N_LG, 1, LANES), F32), SDS((nc, N_LG, 1, LANES), F32)],
        [vm(R), vm(R), vm(R), vm(R), vm(SUBLANES), vm(SUBLANES), vm(1), vm(1),
         pltpu.VMEM((R, LANES), F32), pltpu.VMEM((R, LANES), F32)],
        ("parallel", "arbitrary"), (za, *prm), comm)
    return res if comm is None else (res[:3], res[3:])


def _s5_bwd(za, dyg, x0r, x0i, prm, comm=None):
    T = za.shape[0]
    R = S5_CHUNK
    nc = T // R
    ub = (Q_W + 2 * KV_W) // LANES
    cidx, pspecs = _s5_specs(nc, True)
    PAD = SUBLANES

    def body(u_ref, dyg_ref, x0r_ref, x0i_ref, b_ref, c_ref, d_ref, are_ref, aim_ref,
             alr_ref, ali_ref,
             du_ref, dar_ref, dai_ref, db_ref, dc_ref, dd_ref,
             bur, bui, xsr, xsi, sr, si, gcr, gci, utm, dtm, dunat):
        c = pl.program_id(1)

        @pl.when(c == 0)
        def _():
            gcr[...] = jnp.zeros(gcr.shape, F32)
            gci[...] = jnp.zeros(gci.shape, F32)
            dar_ref[...] = jnp.zeros(dar_ref.shape, F32)
            dai_ref[...] = jnp.zeros(dai_ref.shape, F32)
            db_ref[...] = jnp.zeros(db_ref.shape, F32)
            dc_ref[...] = jnp.zeros(dc_ref.shape, F32)
            dd_ref[...] = jnp.zeros(dd_ref.shape, F32)

        _s5_to_time_major(u_ref, utm)
        _s5_to_time_major(dyg_ref, dtm)
        u = utm[...]
        ub16 = u.astype(BF16)
        bcat, ccat = b_ref[0].astype(BF16), c_ref[0].astype(BF16)
        lanes = lambda v, ll: v[:, ll * LANES:(ll + 1) * LANES]
        bu = jnp.dot(ub16, bcat, preferred_element_type=F32)
        for ll in range(LG_PER_JB):
            bur[ll] = lanes(bu, ll)
            bui[ll] = lanes(bu, LG_PER_JB + ll)
        ar = [are_ref[ll] for ll in range(LG_PER_JB)]
        ai = [aim_ref[ll] for ll in range(LG_PER_JB)]
        ends = _s5_scan(bur, bui, ar, ai, False)
        in_r, in_i = x0r_ref[0], x0i_ref[0]
        _s5_fixup(ends, in_r, in_i, alr_ref[...], ali_ref[...], sr, si, False)
        _s5_scan(bur, bui, ar, ai, False, start=(sr, si), dst=(xsr, xsi), dst_row0=PAD)
        xsr[:, 0:PAD, :] = sr[...]
        xsi[:, 0:PAD, :] = si[...]
        xcat = jnp.concatenate([xsr[ll, PAD:, :].astype(BF16) for ll in range(LG_PER_JB)]
                               + [xsi[ll, PAD:, :].astype(BF16) for ll in range(LG_PER_JB)], axis=1)
        y = d_ref[...] * u + jnp.dot(xcat, ccat, preferred_element_type=F32)
        dy = dtm[...] * _gelu_grad(y)
        dyb = dy.astype(BF16)
        dd_ref[...] += jnp.sum(dy * u, axis=0, keepdims=True)
        du = d_ref[...] * dy
        dc_ref[0] += lax.dot_general(dyb, xcat, _TN, preferred_element_type=F32)
        g = lax.dot_general(dyb, ccat, _NT, preferred_element_type=F32)
        for ll in range(LG_PER_JB):
            bur[ll] = lanes(g, ll)
            bui[ll] = lanes(g, LG_PER_JB + ll)
        aic = [-v for v in ai]
        ends = _s5_scan(bur, bui, ar, aic, True)
        out_r, out_i = _s5_fixup(ends, gcr[...], gci[...], alr_ref[...], -ali_ref[...], sr, si, True)
        gcr[...] = out_r
        gci[...] = out_i
        _s5_scan(bur, bui, ar, aic, True, start=(sr, si), dst=(bur, bui))
        for ll in range(LG_PER_JB):
            gr, gi = bur[ll], bui[ll]
            xpr, xpi = xsr[ll, 0:R, :], xsi[ll, 0:R, :]
            red = lambda v: v.reshape(R // SUBLANES, SUBLANES, LANES).sum(axis=0)
            dar_ref[ll] += red(xpr * gr + xpi * gi)
            dai_ref[ll] += red(xpr * gi - xpi * gr)
        gcat = jnp.concatenate([bur[ll].astype(BF16) for ll in range(LG_PER_JB)]
                               + [bui[ll].astype(BF16) for ll in range(LG_PER_JB)], axis=1)
        db_ref[0] += lax.dot_general(ub16, gcat, _TN, preferred_element_type=F32)
        du = du + lax.dot_general(gcat, bcat, _NT, preferred_element_type=F32)
        _s5_from_time_major(du, dunat)
        du_ref[...] = dunat[...].astype(du_ref.dtype)

    st = pl.BlockSpec((1, LG_PER_JB, 1, LANES), lambda j, c: (cidx(c), j, 0, 0))
    jb = lambda shape: pl.BlockSpec(shape, lambda j, c: (j, 0, 0))
    vm = lambda rows: pltpu.VMEM((LG_PER_JB, rows, LANES), F32)
    res = _call(
        "s5_bwd", body, (N_JB, nc),
        [pl.BlockSpec((R, LANES), lambda j, c: (cidx(c), ub + j)),
         pl.BlockSpec((R, LANES), lambda j, c: (cidx(c), j)), st, st] + pspecs,
        [pl.BlockSpec((R, LANES), lambda j, c: (cidx(c), j)),
         jb((LG_PER_JB, SUBLANES, LANES)), jb((LG_PER_JB, SUBLANES, LANES)),
         jb((1, LANES, 8 * LANES)), jb((1, LANES, 8 * LANES)),
         pl.BlockSpec((1, LANES), lambda j, c: (0, j))],
        [SDS((T, SSM_W), BF16), SDS((N_LG, SUBLANES, LANES), F32), SDS((N_LG, SUBLANES, LANES), F32),
         SDS((N_JB, LANES, 8 * LANES), F32), SDS((N_JB, LANES, 8 * LANES), F32), SDS((1, SSM_W), F32)],
        [vm(R), vm(R), vm(R + PAD), vm(R + PAD), vm(SUBLANES), vm(SUBLANES), vm(1), vm(1)]
        + [pltpu.VMEM((R, LANES), F32)] * 3,
        ("parallel", "arbitrary"), (za, dyg, x0r, x0i, *prm), comm)
    return res if comm is None else (res[:6], res[6:])


def _assemble_w_a(wi):
    _, rows, cb = wi.shape
    tr = 256

    def body(w_ref, a_ref):
        a_ref[:, :cb] = w_ref[0]
        a_ref[:, cb:] = w_ref[1, :, :ZA_W - cb]

    return pl.pallas_call(
        body, name="assemble_w_a", grid=(rows // tr,),
        in_specs=[pl.BlockSpec((2, tr, cb), lambda i: (0, i, 0))],
        out_specs=pl.BlockSpec((tr, ZA_W), lambda i: (i, 0)),
        out_shape=SDS((rows, ZA_W), wi.dtype), compiler_params=_cp(("parallel",)))(wi)


def _assemble_w_g(wi):
    _, rows, cb = wi.shape
    tr = 256
    cut = ZA_W - cb

    def body(w_ref, g_ref):
        g_ref[:, :cb - cut] = w_ref[1, :, cut:]
        g_ref[:, cb - cut:2 * cb - cut] = w_ref[2]
        g_ref[:, 2 * cb - cut:] = w_ref[3]

    return pl.pallas_call(
        body, name="assemble_w_g", grid=(rows // tr,),
        in_specs=[pl.BlockSpec((4, tr, cb), lambda i: (0, i, 0))],
        out_specs=pl.BlockSpec((tr, 4 * cb - ZA_W), lambda i: (i, 0)),
        out_shape=SDS((rows, 4 * cb - ZA_W), wi.dtype), compiler_params=_cp(("parallel",)))(wi)


def _stack_w_in_grad(d_w_a, d_w_g):
    rows = d_w_a.shape[0]
    cb = (ZA_W + d_w_g.shape[1]) // 4
    cut = ZA_W - cb
    tr = 256

    def body(a_ref, g_ref, o_ref):
        o_ref[0] = a_ref[:, :cb]
        o_ref[1, :, :cut] = a_ref[:, cb:]
        o_ref[1, :, cut:] = g_ref[:, :cb - cut]
        o_ref[2] = g_ref[:, cb - cut:2 * cb - cut]
        o_ref[3] = g_ref[:, 2 * cb - cut:]

    return pl.pallas_call(
        body, name="stack_w_in_grad", grid=(rows // tr,),
        in_specs=[pl.BlockSpec((tr, ZA_W), lambda i: (i, 0)), pl.BlockSpec((tr, d_w_g.shape[1]), lambda i: (i, 0))],
        out_specs=pl.BlockSpec((4, tr, cb), lambda i: (0, i, 0)),
        out_shape=SDS((4, rows, cb), d_w_a.dtype), compiler_params=_cp(("parallel",)))(d_w_a, d_w_g)


def _local_step(x, target, gains, w_a, sinks, s5w, comms, late_g, late, red=None):
    T = x.shape[0]
    D = D_MODEL
    g1, g2, g3, g4 = gains
    cos, sin = _rope_tables(T)
    lam_re, lam_im, log_dt, b_re, b_im, c_re, c_im, d_skip = s5w
    (a_re, a_im, bb_re, bb_im), disc_vjp = jax.vjp(_s5_discretize, lam_re, lam_im, log_dt, b_re, b_im)
    abr, abi, al_re, al_im = _s5_tables(a_re, a_im)
    prm = (jnp.concatenate([_blockdiag_in(bb_re), _blockdiag_in(bb_im)], axis=2),
           jnp.concatenate([_blockdiag_out(c_re), -_blockdiag_out(c_im)], axis=1),
           d_skip.reshape(1, SSM_W), abr, abi, al_re, al_im)
    mm = functools.partial(_mm, tm=1024, tn=1024, tk=2048)

    h = _rowwise(lambda xv, g: ((_rms(xv)[0] * g,), ()), [(x, D, 0)], [g1], [(D, BF16)], [], tr=512, name="norm1")[0]
    unpack = lambda res, comm: (res, ()) if comm is None else res
    za, got_a = unpack(_mm(h, w_a, mode="nn", out_dtype=F32, tm=1024, tn=1152, tk=2048, name="mm_za", comm=comms[0]), comms[0])
    w_g = late_g(got_a)
    zg, got0 = unpack(mm(h, w_g, mode="nn", out_dtype=BF16, name="mm_zg", tn=2048, comm=comms[1]), comms[1])
    o_attn, got1 = unpack(_attn_fwd(za, cos, sin, sinks, comm=comms[2]), comms[2])
    (yg, x0r, x0i), got2 = unpack(_s5_fwd(za, prm, comm=comms[3]), comms[3])
    w_glu, w_ba, w_bs, w_out, w_up, w_down = late(got0, got1, got2)
    zglu = mm(yg, w_glu, mode="nn", out_dtype=BF16, name="mm_glu")
    o_ssm = _rowwise(lambda z1, z2: ((z1 * _sig(z2),), ()), [(zglu, SSM_W, 0), (zglu, SSM_W, 1)], [],
                     [(SSM_W, BF16)], [], tr=512, name="glu")[0]
    ya = mm(o_attn, w_ba, mode="nn", out_dtype=BF16, name="mm_ya")
    ys = mm(o_ssm, w_bs, mode="nn", out_dtype=BF16, name="mm_ys")
    mi = _rowwise(lambda ga, gs, a, s: ((_sig(ga) * a + _sig(gs) * s,), ()),
                  [(zg, D, 0), (zg, D, 1), (ya, D, 0), (ys, D, 0)], [], [(D, BF16)], [], tr=256, name="gate")[0]
    mixed = mm(mi, w_out, mode="nn", out_dtype=F32, name="mm_out")

    def f_post(xv, mv, g2v, g3v):
        x1v = xv + _rms(mv)[0] * g2v
        return (x1v, _rms(x1v)[0] * g3v), ()
    x1, h2 = _rowwise(f_post, [(x, D, 0), (mixed, D, 0)], [g2, g3], [(D, F32), (D, BF16)], [], tr=256, name="post_mix")
    act = mm(h2, w_up, mode="nn", out_dtype=BF16, name="mm_up", tn=2048, epi=lambda v: jnp.maximum(v, 0.0))
    f = mm(act, w_down, mode="nn", out_dtype=F32, name="mm_down", a_fn=lambda v: v * v, tk=4096)

    def f_final(x1v, fv, tv, g4v):
        fn, r = _rms(fv)
        e = x1v + fn * g4v - tv
        dx2v = e * (1.0 / D)
        dfv, dg4v = _rms_bwd(dx2v, fn, r, g4v)
        return (dfv, dx2v), (dg4v, jnp.zeros((SUBLANES, LANES), F32) + 0.5 * jnp.sum(e * e) * (1.0 / D))
    df, dx2, dg4, lossb = _rowwise(f_final, [(x1, D, 0), (f, D, 0), (target, D, 0)], [g4],
                                   [(D, BF16), (D, F32)], [(1, D), (SUBLANES, LANES)], tr=256, name="final")

    big = {}

    def add(k, g4):
        big[k] = g4
        if red is not None:
            red.add(k, g4)

    def hosted(fn, stage, names):
        if red is None:
            return fn(comm=None)
        out, got = fn(comm=getattr(red, stage)(names))
        getattr(red, stage + "_done")(names, got)
        return out

    dpre = mm(df, w_down, mode="nt", out_dtype=BF16, name="mm_dact", tn=2048,
              epi=lambda v, a: v * (2.0 * a.astype(F32)), extras=(act,))
    wg = functools.partial(_mm, mode="tn", out_dtype=F32, tm=1024, tn=1024, tk=4096)
    add("w_down", wg(act, df, name="wg_down", a_fn=lambda v: v * v).reshape(4, D_FF // 4, D))
    dh2 = hosted(functools.partial(mm, dpre, w_up, mode="nt", out_dtype=F32, name="mm_dh2", tk=4096),
                 "s1", ["w_down"])
    add("w_up", hosted(functools.partial(wg, h2, dpre, name="wg_up", shard_cols=D_FF // 4), "s3", ["w_down"]))

    def f_mid(dx2v, dh2v, x1v, mv, g2v, g3v):
        x1n, r3 = _rms(x1v)
        d3, dg3v = _rms_bwd(dh2v, x1n, r3, g3v)
        dx1v = dx2v + d3
        mn, r2 = _rms(mv)
        dmv, dg2v = _rms_bwd(dx1v, mn, r2, g2v)
        return (dx1v, dmv), (dg3v, dg2v)
    dx1, dmixed, dg3, dg2 = _rowwise(f_mid, [(dx2, D, 0), (dh2, D, 0), (x1, D, 0), (mixed, D, 0)], [g2, g3],
                                     [(D, F32), (D, BF16)], [(1, D), (1, D)], tr=256, name="mid")

    dmi = hosted(functools.partial(mm, dmixed, w_out, mode="nt", out_dtype=BF16, name="mm_dmi"), "s1", ["w_up"])
    add("w_out", wg(mi, dmixed, name="wg_out").reshape(4, D // 4, D))

    def f_gate(dv, ga, gs, a, s):
        sa, ss = _sig(ga), _sig(gs)
        return (dv * sa, dv * ss, jnp.concatenate([dv * a * sa * (1.0 - sa), dv * s * ss * (1.0 - ss)], axis=1)), ()
    dya, dys, dzg = _rowwise(f_gate, [(dmi, D, 0), (zg, D, 0), (zg, D, 1), (ya, D, 0), (ys, D, 0)], [],
                             [(D, BF16), (D, BF16), (2 * D, BF16)], [], tr=256, name="gate_bwd")
    do_attn = hosted(functools.partial(mm, dya, w_ba, mode="nt", out_dtype=BF16, name="mm_doa"), "s1", ["w_out"])
    d_w_ba = wg(o_attn, dya, name="wg_ba")
    do_ssm = mm(dys, w_bs, mode="nt", out_dtype=BF16, name="mm_dos")
    d_w_bs = wg(o_ssm, dys, name="wg_bs")
    add("w_branch", jnp.concatenate([d_w_ba.reshape(2, D // 4, D), d_w_bs.reshape(2, D // 4, D)], axis=0))

    def f_glu(dv, z1, z2):
        s2 = _sig(z2)
        return (jnp.concatenate([dv * s2, dv * z1 * s2 * (1.0 - s2)], axis=1),), ()
    dzglu = _rowwise(f_glu, [(do_ssm, SSM_W, 0), (zglu, SSM_W, 0), (zglu, SSM_W, 1)], [], [(2 * SSM_W, BF16)], [],
                     tr=512, name="glu_bwd")[0]
    dyg = hosted(functools.partial(mm, dzglu, w_glu, mode="nt", out_dtype=F32, name="mm_dyg"), "s1", ["w_branch"])
    add("w_glu", wg(yg, dzglu, name="wg_glu", tn=SSM_W // 2, shard_cols=SSM_W // 2))
    du, dar, dai, dbc, dcc, ddv = hosted(functools.partial(_s5_bwd, za, dyg, x0r, x0i, prm),
                                         "s3", ["w_up", "w_out", "w_branch"])
    dbr, dbi = dbc[:, :, :4 * LANES], dbc[:, :, 4 * LANES:]
    dcc = dcc.transpose(0, 2, 1)
    dcr, dci = dcc[:, :4 * LANES, :], -dcc[:, 4 * LANES:, :]
    dq, dkv, dsk = hosted(functools.partial(_attn_bwd, za, cos, sin, sinks, o_attn, do_attn),
                          "s5", ["w_down", "w_up", "w_out", "w_branch"])
    dza = jnp.concatenate([dq, dkv, du], axis=1)
    d_w_a = _mm(h, dza, mode="tn", out_dtype=F32, tm=1024, tn=ZA_W // 2, tk=2048, name="wg_a")
    d_w_g = wg(h, dzg, name="wg_g")
    add("w_in", _stack_w_in_grad(d_w_a, d_w_g))
    dh = hosted(functools.partial(mm, dza, w_a, mode="nt", out_dtype=F32, name="mm_dh_a", tk=ZA_W), "s1", ["w_in", "w_glu"])
    dh = hosted(functools.partial(mm, dzg, w_g, mode="nt", out_dtype=F32, name="mm_dh_g",
                                  epi=lambda v, p: v + p, extras=(dh,)), "s3", ["w_in", "w_glu"])

    def f_first(dx1v, dhv, xv, g1v):
        xn, r1 = _rms(xv)
        d1, dg1v = _rms_bwd(dhv, xn, r1, g1v)
        return (dx1v + d1,), (dg1v,)
    dx, dg1 = _rowwise(f_first, [(dx1, D, 0), (dh, D, 0), (x, D, 0)], [g1], [(D, F32)], [(1, D)], tr=256, name="first")

    da_re = dar.sum(axis=1).reshape(SSM_G, SSM_P)
    da_im = dai.sum(axis=1).reshape(SSM_G, SSM_P)
    d_lam_re, d_lam_im, d_log_dt, d_b_re, d_b_im = disc_vjp(
        (da_re, da_im, _blockdiag_in_extract(dbr), _blockdiag_in_extract(dbi)))
    small = dict(norm_mix_pre=dg1, norm_mix_post=dg2, norm_mlp_pre=dg3, norm_mlp_post=dg4,
                 sinks=dsk[:, :N_Q_HEADS], lam_re=d_lam_re, lam_im=d_lam_im, log_dt=d_log_dt,
                 b_re=d_b_re, b_im=d_b_im, c_re=_blockdiag_out_extract(dcr), c_im=_blockdiag_out_extract(dci),
                 d_skip=ddv.reshape(SSM_G, SSM_GC))
    return lossb[0, 0], dx, small, big


def _cast_into_slot(w, k_arr):
    rows, cols = w.shape
    tr = 256

    def body(k_ref, w_ref, o_ref):
        o_ref[0] = w_ref[...].astype(BF16)

    return pl.pallas_call(
        body,
        name="cast_into_slot",
        grid_spec=pltpu.PrefetchScalarGridSpec(
            num_scalar_prefetch=1,
            grid=(rows // tr,),
            in_specs=[pl.BlockSpec((tr, cols), lambda i, k: (i, 0))],
            out_specs=pl.BlockSpec((1, tr, cols), lambda i, k: (k[0], i, 0)),
        ),
        out_shape=SDS((4, rows, cols), BF16),
        compiler_params=_cp(("parallel",)),
    )(k_arr, w)


def _pair_sum(g, r, c_arr):
    _, _, hr, cols = g.shape
    tr = min(256, hr)

    def body(c_ref, g_ref, r_ref, o_ref):
        o_ref[0] = (g_ref[0, 0] + r_ref[0]).astype(BF16)

    return pl.pallas_call(
        body,
        name="pair_sum",
        grid_spec=pltpu.PrefetchScalarGridSpec(
            num_scalar_prefetch=1,
            grid=(3, hr // tr),
            in_specs=[pl.BlockSpec((1, 1, tr, cols), lambda k, i, c_ref: (c_ref[1 + k], c_ref[0], i, 0)),
                      pl.BlockSpec((1, tr, cols), lambda k, i, c_ref: (c_ref[1 + k], i, 0))],
            out_specs=pl.BlockSpec((1, tr, cols), lambda k, i, c_ref: (c_ref[1 + k], i, 0)),
        ),
        out_shape=SDS((4, hr, cols), BF16),
        compiler_params=_cp(("parallel", "parallel")),
    )(c_arr, g, r)


def _chip_sum(g, r, q, kc_arr):
    _, _, hr, cols = g.shape
    tr = min(256, hr)

    def body(kc_ref, g_ref, r_ref, q_ref, o_ref):
        s = g_ref[0, 0] + r_ref[0]
        for j in range(3):
            s = s + q_ref[j].astype(F32)
        o_ref[...] = s

    return pl.pallas_call(
        body,
        name="chip_sum",
        grid_spec=pltpu.PrefetchScalarGridSpec(
            num_scalar_prefetch=1,
            grid=(hr // tr,),
            in_specs=[pl.BlockSpec((1, 1, tr, cols), lambda i, kc: (kc[0], kc[1], i, 0)),
                      pl.BlockSpec((1, tr, cols), lambda i, kc: (kc[0], i, 0)),
                      pl.BlockSpec((3, tr, cols), lambda i, kc: (0, i, 0))],
            out_specs=pl.BlockSpec((tr, cols), lambda i, kc: (kc[1] * (hr // tr) + i, 0)),
        ),
        out_shape=SDS((2 * hr, cols), F32),
        compiler_params=_cp(("parallel",)),
    )(kc_arr, g, r, q)


class _PairShareComm:
    aliased = True

    def __init__(self, blocks):
        self.arrs = list(blocks)
        self.n = len(self.arrs)
        dma = pltpu.SemaphoreType.DMA
        self.scratch = [dma((self.n,)), dma((self.n,))]
        self.out_shape = [SDS(b.shape, b.dtype) for b in self.arrs]

    def _copies(self, ins, outs, sems, hc):
        ssem, rsem = sems
        x, y, c, _ = _place()
        cps = []
        for w in range(self.n):
            hr = ins[w].shape[0] // 2
            rows = pl.ds(pl.multiple_of((c if hc == 0 else 1 - c) * hr, 8), hr)
            cps.append(_remote(ins[w].at[rows, :], outs[w].at[rows, :], ssem.at[w], rsem.at[w], (x, y, 1 - c)))
        return cps

    def start(self, ins, outs, sems):
        for cp in self._copies(ins, outs, sems, 0):
            cp.start()

    def finish(self, ins, outs, sems):
        for cp in self._copies(ins, outs, sems, 1):
            cp.wait_recv()
        for cp in self._copies(ins, outs, sems, 0):
            cp.wait_send()


class _GradReducer:
    def __init__(self, c_arr, kc_arr):
        self.c_arr, self.kc_arr = c_arr, kc_arr
        self.g, self.r, self.ps, self.q, self.done = {}, {}, {}, {}, {}

    def add(self, k, g4):
        self.g[k] = g4.reshape(4, 2, g4.shape[1] // 2, g4.shape[2])

    def s1(self, names):
        return _PairExchangeComm([self.g[k].reshape(4, -1, self.g[k].shape[3]) for k in names])

    def s1_done(self, names, got):
        for k, r in zip(names, got):
            self.r[k] = r
            self.ps[k] = _pair_sum(self.g[k], r, self.c_arr)

    def s3(self, names):
        return _ChipExchangeComm([self.ps[k] for k in names])

    def s3_done(self, names, got):
        self.q.update(zip(names, got))

    def finish(self, order):
        rest = [k for k in order if k not in self.r]
        if rest:
            self.s1_done(rest, _comm_only("pair_exchange", self.s1(rest)))
        rest = [k for k in order if k not in self.q]
        if rest:
            self.s3_done(rest, _comm_only("chip_exchange", self.s3(rest)))
        rest = [k for k in order if k not in self.done]
        if rest:
            self.s5_done(rest, _comm_only("pair_share", self.s5(rest)))
        return self.done

    def s5(self, names):
        return _PairShareComm([_chip_sum(self.g[k], self.r[k], self.q[k], self.kc_arr) for k in names])

    def s5_done(self, names, got):
        self.done.update(zip(names, got))


def _all_reduce_small(buf):
    rows = buf.shape[0]
    hr = rows // 2
    assert hr % SUBLANES == 0

    def body(in_ref, o_ref, sib, pair, slots, ssem, rsem):
        x, y, c, others = _place()
        me, sibling = 2 * x + y, (x, y, 1 - c)
        mine = pl.ds(pl.multiple_of(c * hr, SUBLANES), hr)
        theirs = pl.ds(pl.multiple_of((1 - c) * hr, SUBLANES), hr)
        first = _remote(in_ref, sib, ssem.at[0], rsem.at[0], sibling)
        first.start()
        first.wait()
        pair[...] = in_ref[...] + sib[...]
        slots[me] = pair[mine, :]
        cps = [_remote(pair.at[mine, :], slots.at[me], ssem.at[1 + r], rsem.at[1 + r], (ox, oy, c))
               for r, (ox, oy) in enumerate(others)]
        for cp in cps:
            cp.start()
        for r, (ox, oy) in enumerate(others):
            _remote(pair.at[mine, :], slots.at[2 * ox + oy], ssem.at[1 + r], rsem.at[1 + r], (ox, oy, c)).wait_recv()
        o_ref[mine, :] = (slots[0] + slots[1]) + (slots[2] + slots[3])
        last = _remote(o_ref.at[mine, :], o_ref.at[mine, :], ssem.at[4], rsem.at[4], sibling)
        last.start()
        _remote(o_ref.at[theirs, :], o_ref.at[theirs, :], ssem.at[4], rsem.at[4], sibling).wait_recv()
        last.wait_send()
        for cp in cps:
            cp.wait_send()

    dma = pltpu.SemaphoreType.DMA
    return pl.pallas_call(
        body,
        name="all_reduce_small",
        in_specs=[pl.BlockSpec(memory_space=pltpu.VMEM)],
        out_specs=pl.BlockSpec(memory_space=pltpu.VMEM),
        out_shape=SDS(buf.shape, F32),
        scratch_shapes=[pltpu.VMEM((rows, LANES), F32), pltpu.VMEM((rows, LANES), F32),
                        pltpu.VMEM((4, hr, LANES), F32), dma((5,)), dma((5,))],
        compiler_params=pltpu.CompilerParams(vmem_limit_bytes=VMEM_LIMIT),
    )(buf)


def _adam_fn(w, g, m, v):
    m2 = ADAM_B1 * m + (1.0 - ADAM_B1) * g
    v2 = ADAM_B2 * v + (1.0 - ADAM_B2) * (g * g)
    m_hat = m2 / (1.0 - ADAM_B1 ** ADAM_STEP)
    v_hat = v2 / (1.0 - ADAM_B2 ** ADAM_STEP)
    return (-ADAM_LR * (m_hat / (jnp.sqrt(v_hat) + ADAM_EPS) + ADAM_WD * w), m2, v2), ()


def _adamw(w, g, m, v, name, tr=256):
    cols = w.shape[1]
    return _rowwise(_adam_fn, [(w, cols, 0), (g, cols, 0), (m, cols, 0), (v, cols, 0)], [],
                    [(cols, F32)] * 3, [], tr=tr, name=name)


BIG = ("w_in", "w_glu", "w_branch", "w_out", "w_up", "w_down")
COL_SHARDED = ("w_in", "w_glu", "w_up")
SMALL = ("norm_mix_pre", "norm_mix_post", "norm_mlp_pre", "norm_mlp_post", "sinks", "lam_re", "lam_im", "log_dt",
         "b_re", "b_im", "c_re", "c_im", "d_skip")
WEIGHTS = ("norm_mix_pre", "norm_mix_post", "norm_mlp_pre", "norm_mlp_post", "w_in", "sinks", "lam_re", "lam_im",
           "log_dt", "b_re", "b_im", "c_re", "c_im", "d_skip", "w_glu", "w_branch", "w_out", "w_up", "w_down")


def _flat_small(vals, extra):
    flat = jnp.concatenate([vals[k].reshape(-1) for k in SMALL] + [extra.reshape(-1)])
    rows = -(-flat.shape[0] // (SUBLANES * LANES)) * SUBLANES
    return jnp.pad(flat, (0, rows * LANES - flat.shape[0])).reshape(rows, LANES)


def kernel(x, norm_mix_pre, norm_mix_post, norm_mlp_pre, norm_mlp_post, w_in, sinks, lam_re, lam_im, log_dt, b_re, b_im, c_re, c_im, d_skip, w_glu, w_branch, w_out, w_up, w_down, loss_target, m_norm_mix_pre, m_norm_mix_post, m_norm_mlp_pre, m_norm_mlp_post, m_w_in, m_sinks, m_lam_re, m_lam_im, m_log_dt, m_b_re, m_b_im, m_c_re, m_c_im, m_d_skip, m_w_glu, m_w_branch, m_w_out, m_w_up, m_w_down, v_norm_mix_pre, v_norm_mix_post, v_norm_mlp_pre, v_norm_mlp_post, v_w_in, v_sinks, v_lam_re, v_lam_im, v_log_dt, v_b_re, v_b_im, v_c_re, v_c_im, v_d_skip, v_w_glu, v_w_branch, v_w_out, v_w_up, v_w_down):
    w = dict(norm_mix_pre=norm_mix_pre, norm_mix_post=norm_mix_post, norm_mlp_pre=norm_mlp_pre, norm_mlp_post=norm_mlp_post,
             w_in=w_in, sinks=sinks, lam_re=lam_re, lam_im=lam_im, log_dt=log_dt, b_re=b_re, b_im=b_im, c_re=c_re,
             c_im=c_im, d_skip=d_skip, w_glu=w_glu, w_branch=w_branch, w_out=w_out, w_up=w_up, w_down=w_down)
    m = dict(norm_mix_pre=m_norm_mix_pre, norm_mix_post=m_norm_mix_post, norm_mlp_pre=m_norm_mlp_pre,
             norm_mlp_post=m_norm_mlp_post, w_in=m_w_in, sinks=m_sinks, lam_re=m_lam_re, lam_im=m_lam_im,
             log_dt=m_log_dt, b_re=m_b_re, b_im=m_b_im, c_re=m_c_re, c_im=m_c_im, d_skip=m_d_skip, w_glu=m_w_glu,
             w_branch=m_w_branch, w_out=m_w_out, w_up=m_w_up, w_down=m_w_down)
    v = dict(norm_mix_pre=v_norm_mix_pre, norm_mix_post=v_norm_mix_post, norm_mlp_pre=v_norm_mlp_pre,
             norm_mlp_post=v_norm_mlp_post, w_in=v_w_in, sinks=v_sinks, lam_re=v_lam_re, lam_im=v_lam_im,
             log_dt=v_log_dt, b_re=v_b_re, b_im=v_b_im, c_re=v_c_re, c_im=v_c_im, d_skip=v_d_skip, w_glu=v_w_glu,
             w_branch=v_w_branch, w_out=v_w_out, w_up=v_w_up, w_down=v_w_down)
    xi, yi, ci = lax.axis_index("x"), lax.axis_index("y"), lax.axis_index("c")

    k_arr = jnp.stack([2 * xi + yi]).astype(jnp.int32)
    slot = {k: _cast_into_slot(w[k][0], k_arr) for k in BIG}

    def whole(k, g4):
        if k in COL_SHARDED:
            return jnp.concatenate([g4[j] for j in range(4)], axis=1)
        return g4.reshape(4 * g4.shape[1], g4.shape[2])

    wi = _comm_only("gather_w_in", _GatherComm([slot["w_in"]]))[0]
    w_a = _assemble_w_a(wi)
    hosted = (("w_glu", "w_branch", "w_out"), ("w_up",), ("w_down",))
    comms = [None] + [_GatherComm([slot[k] for k in names]) for names in hosted]

    def late(*got):
        f = {k: whole(k, g4) for names, res in zip(hosted, got) for k, g4 in zip(names, res)}
        return f["w_glu"], f["w_branch"][:Q_W], f["w_branch"][Q_W:], f["w_out"], f["w_up"], f["w_down"]

    s5w = (lam_re[0], lam_im[0], log_dt[0], b_re[0], b_im[0], c_re[0], c_im[0], d_skip[0])
    reducer = _GradReducer(
        jnp.stack([ci, 2 * (1 - xi) + yi, 2 * xi + (1 - yi), 2 * (1 - xi) + (1 - yi)]).astype(jnp.int32),
        jnp.stack([2 * xi + yi, ci]).astype(jnp.int32))
    loss_part, dx, small, _ = _local_step(
        x[0], loss_target[0], (norm_mix_pre, norm_mix_post, norm_mlp_pre, norm_mlp_post),
        w_a, sinks, s5w, comms, lambda _: _assemble_w_g(wi), late, reducer)
    grads = reducer.finish(BIG)

    red = _all_reduce_small(_flat_small(small, loss_part)).reshape(-1)
    off = 0
    for k in SMALL:
        n = math.prod(w[k].shape)
        grads[k] = red[off:off + n].reshape(w[k].shape[1:])
        off += n
    loss = red[off]

    delta, new_m, new_v = {}, {}, {}
    for k in BIG:
        delta[k], new_m[k], new_v[k] = _adamw(w[k][0], grads[k], m[k][0], v[k][0], "adamw_" + k)
    zero = jnp.zeros((), F32)
    fw, fm, fv = (_flat_small({k: t[k] for k in SMALL}, zero) for t in (w, m, v))
    fg = _flat_small(grads, zero)
    sd, sm, sv = _adamw(fw, fg, fm, fv, "adamw_small", tr=fw.shape[0])
    off = 0
    for k in SMALL:
        n = math.prod(w[k].shape)
        delta[k], new_m[k], new_v[k] = (t.reshape(-1)[off:off + n].reshape(w[k].shape[1:]) for t in (sd, sm, sv))
        off += n

    lead = lambda t: t[None]
    return (loss, lead(dx), *[lead(grads[k]) for k in WEIGHTS], *[lead(delta[k]) for k in WEIGHTS],
            *[lead(new_m[k]) for k in WEIGHTS], *[lead(new_v[k]) for k in WEIGHTS])
```

```python
import functools
import math

import jax
import jax.numpy as jnp
from jax import lax
from jax.experimental import pallas as pl
from jax.experimental.pallas import tpu as pltpu

F32 = jnp.float32
BF16 = jnp.bfloat16
SDS = jax.ShapeDtypeStruct

D_MODEL = 2048
HEAD_DIM = 64
N_Q_HEADS = 16
ATT_BLOCK = 128
ROT_DIM = 16
ROPE_THETA = 500000.0
Q_W = 1024
KV_W = 128
SSM_W = 1024
SSM_G = 64
SSM_GC = 16
SSM_P = 64
N_STATE = SSM_G * SSM_P
LANES = 128
SUBLANES = 8
N_LG = N_STATE // LANES
N_JB = 8
LG_PER_JB = N_LG // N_JB
D_FF = 8192
ZA_W = Q_W + 2 * KV_W + SSM_W
EPS = 1e-6
S5_CHUNK = 2048
S5_SEG = S5_CHUNK // SUBLANES
VMEM_LIMIT = 56 * 1024 * 1024
NEG = -1e30

ADAM_LR = 0.001
ADAM_B1 = 0.9
ADAM_B2 = 0.999
ADAM_EPS = 1e-08
ADAM_WD = 0.01
ADAM_STEP = 10

MESH = pl.DeviceIdType.MESH


def _cp(sem):
    return pltpu.CompilerParams(dimension_semantics=sem, vmem_limit_bytes=VMEM_LIMIT)


ANY = pl.BlockSpec(memory_space=pl.ANY)


def _place():
    x, y, c = lax.axis_index("x"), lax.axis_index("y"), lax.axis_index("c")
    others = [(1 - x, y), (x, 1 - y), (1 - x, 1 - y)]
    return x, y, c, others


def _remote(src, dst, ssem, rsem, to):
    return pltpu.make_async_remote_copy(src_ref=src, dst_ref=dst, send_sem=ssem, recv_sem=rsem,
                                        device_id=to, device_id_type=MESH)


class _GatherComm:
    aliased = True

    def __init__(self, slotted):
        self.arrs = list(slotted)
        self.n = len(self.arrs)
        dma = pltpu.SemaphoreType.DMA
        self.scratch = [dma((3 * self.n,)) for _ in range(4)]
        self.out_shape = [SDS(s.shape, s.dtype) for s in self.arrs]

    @staticmethod
    def _half(ref, hc):
        hr = ref.shape[1] // 2
        return pl.ds(pl.multiple_of(hc * hr, 16), hr)

    def _sends(self, ins, outs, sems):
        ssem, rsem, _, _ = sems
        x, y, c, others = _place()
        me = 2 * x + y
        return [_remote(ins[w].at[me, self._half(ins[w], c), :], outs[w].at[me, self._half(ins[w], c), :],
                        ssem.at[3 * w + r], rsem.at[3 * w + r], (ox, oy, c))
                for w in range(self.n) for r, (ox, oy) in enumerate(others)]

    def start(self, ins, outs, sems):
        for cp in self._sends(ins, outs, sems):
            cp.start()

    def finish(self, ins, outs, sems):
        ssem, rsem, fs_sem, fr_sem = sems
        x, y, c, others = _place()
        sib = (x, y, 1 - c)
        passes = []
        for w in range(self.n):
            for r, (ox, oy) in enumerate(others):
                got = outs[w].at[2 * ox + oy, self._half(ins[w], c), :]
                _remote(got, got, ssem.at[3 * w + r], rsem.at[3 * w + r], (ox, oy, c)).wait_recv()
                cp = _remote(got, got, fs_sem.at[3 * w + r], fr_sem.at[3 * w + r], sib)
                cp.start()
                passes.append(cp)
        for w in range(self.n):
            for r, (ox, oy) in enumerate(others):
                got = outs[w].at[2 * ox + oy, self._half(ins[w], 1 - c), :]
                _remote(got, got, fs_sem.at[3 * w + r], fr_sem.at[3 * w + r], sib).wait_recv()
        for cp in self._sends(ins, outs, sems) + passes:
            cp.wait_send()


class _PairExchangeComm:
    aliased = False

    def __init__(self, grads):
        self.arrs = list(grads)
        self.n = len(self.arrs)
        dma = pltpu.SemaphoreType.DMA
        self.scratch = [dma((self.n,)), dma((self.n,))]
        self.out_shape = [SDS((4, g.shape[1] // 2, g.shape[2]), g.dtype) for g in self.arrs]

    def _copies(self, ins, outs, sems):
        ssem, rsem = sems
        x, y, c, _ = _place()
        cps = []
        for w in range(self.n):
            hr = ins[w].shape[1] // 2
            src = ins[w].at[:, pl.ds(pl.multiple_of((1 - c) * hr, 8), hr), :]
            cps.append(_remote(src, outs[w], ssem.at[w], rsem.at[w], (x, y, 1 - c)))
        return cps

    def start(self, ins, outs, sems):
        for cp in self._copies(ins, outs, sems):
            cp.start()

    def finish(self, ins, outs, sems):
        for cp in self._copies(ins, outs, sems):
            cp.wait()


class _ChipExchangeComm:
    aliased = False

    def __init__(self, psums):
        self.arrs = list(psums)
        self.n = len(self.arrs)
        dma = pltpu.SemaphoreType.DMA
        self.scratch = [dma((3 * self.n,)), dma((3 * self.n,))]
        self.out_shape = [SDS((3,) + p.shape[1:], p.dtype) for p in self.arrs]

    def _copies(self, ins, outs, sems):
        ssem, rsem = sems
        x, y, c, others = _place()
        return [_remote(ins[w].at[2 * ox + oy], outs[w].at[r], ssem.at[3 * w + r], rsem.at[3 * w + r], (ox, oy, c))
                for w in range(self.n) for r, (ox, oy) in enumerate(others)]

    def start(self, ins, outs, sems):
        for cp in self._copies(ins, outs, sems):
            cp.start()

    def finish(self, ins, outs, sems):
        for cp in self._copies(ins, outs, sems):
            cp.wait()


def _comm_only(name, comm):
    n = comm.n

    def body(*refs):
        ins, outs, sems = refs[:n], refs[n:2 * n], refs[2 * n:]
        comm.start(ins, outs, sems)
        comm.finish(ins, outs, sems)

    return pl.pallas_call(
        body, name=name, in_specs=[ANY] * n, out_specs=[ANY] * n, out_shape=comm.out_shape,
        input_output_aliases={w: w for w in range(n)} if comm.aliased else {},
        scratch_shapes=comm.scratch)(*comm.arrs)


HBM = pl.BlockSpec(memory_space=pltpu.HBM)
SEM = pl.BlockSpec(memory_space=pltpu.SEMAPHORE)
_EFFECT = pltpu.SideEffectType.DATAFLOW_SIDE_EFFECTING


def _gather_copies(ref, sems):
    x, y, c, others = _place()
    me = 2 * x + y
    half = _GatherComm._half(ref, c)
    out = [_remote(ref.at[me, half, :], ref.at[me, half, :], sems[r], sems[3 + r], (ox, oy, c))
           for r, (ox, oy) in enumerate(others)]
    arriving = [_remote(ref.at[me, half, :], ref.at[2 * ox + oy, half, :], sems[r], sems[3 + r], (ox, oy, c))
                for r, (ox, oy) in enumerate(others)]
    return out, arriving


def _gather_start(slotted):
    def body(w_ref, *rest):
        for cp in _gather_copies(rest[6], rest[:6])[0]:
            cp.start()

    dma = pltpu.SemaphoreType.DMA(())
    return pl.pallas_call(
        body, name="gather_w_in_start",
        out_shape=(dma,) * 6 + (pltpu.HBM(slotted.shape, slotted.dtype),),
        in_specs=(HBM,), out_specs=(SEM,) * 6 + (HBM,), input_output_aliases={0: 6},
        compiler_params=pltpu.CompilerParams(has_side_effects=_EFFECT),
    )(pltpu.with_memory_space_constraint(slotted, pltpu.HBM))


def _gather_wait(sems, thru, after):
    def body(w_ref, *rest):
        out, arriving = _gather_copies(w_ref, rest[:6])
        for cp in out:
            cp.wait_send()
        for cp in arriving:
            cp.wait_recv()

    n = len(after)
    return pl.pallas_call(
        body, name="gather_w_in_wait", out_shape=(pltpu.HBM(thru.shape, thru.dtype),),
        in_specs=(HBM,) + (SEM,) * 6 + (ANY,) * n, out_specs=(HBM,), input_output_aliases={0: 0},
        compiler_params=pltpu.CompilerParams(has_side_effects=_EFFECT),
    )(thru, *sems, *after)[0]


class _PassOnComm:
    aliased = True

    def __init__(self, gathered):
        self.arrs = list(gathered)
        self.n = len(self.arrs)
        dma = pltpu.SemaphoreType.DMA
        self.scratch = [dma((3 * self.n,)), dma((3 * self.n,))]
        self.out_shape = [SDS(s.shape, s.dtype) for s in self.arrs]

    def _copies(self, ins, outs, sems, hc):
        ssem, rsem = sems
        x, y, c, others = _place()
        return [_remote(ins[w].at[2 * ox + oy, _GatherComm._half(ins[w], c if hc == 0 else 1 - c), :],
                        outs[w].at[2 * ox + oy, _GatherComm._half(ins[w], c if hc == 0 else 1 - c), :],
                        ssem.at[3 * w + r], rsem.at[3 * w + r], (x, y, 1 - c))
                for w in range(self.n) for r, (ox, oy) in enumerate(others)]

    def start(self, ins, outs, sems):
        for cp in self._copies(ins, outs, sems, 0):
            cp.start()

    def finish(self, ins, outs, sems):
        for cp in self._copies(ins, outs, sems, 1):
            cp.wait_recv()
        for cp in self._copies(ins, outs, sems, 0):
            cp.wait_send()


def _call(name, body, grid, in_specs, out_specs, out_shape, scratch, dims, args, comm=None):
    if comm is None:
        return pl.pallas_call(body, name=name, grid=grid, in_specs=in_specs, out_specs=out_specs, out_shape=out_shape,
                              scratch_shapes=scratch, compiler_params=_cp(dims))(*args)
    ni, no, ns, n = len(in_specs), len(out_shape), len(scratch), comm.n

    def hosted(*refs):
        ins, cin = refs[:ni], refs[ni:ni + n]
        outs, cout = refs[ni + n:ni + n + no], refs[ni + n + no:ni + 2 * n + no]
        scr, sems = refs[ni + 2 * n + no:ni + 2 * n + no + ns], refs[ni + 2 * n + no + ns:]
        ids = [pl.program_id(d) for d in range(len(grid))]
        first = functools.reduce(jnp.logical_and, [i == 0 for i in ids])
        last = functools.reduce(jnp.logical_and, [i == g - 1 for i, g in zip(ids, grid)])

        @pl.when(first)
        def _():
            comm.start(cin, cout, sems)

        body(*ins, *outs, *scr)

        @pl.when(last)
        def _():
            comm.finish(cin, cout, sems)

    return pl.pallas_call(
        hosted, name=name, grid=grid, in_specs=list(in_specs) + [ANY] * n, out_specs=list(out_specs) + [ANY] * n,
        out_shape=list(out_shape) + comm.out_shape,
        input_output_aliases={ni + w: no + w for w in range(n)} if comm.aliased else {},
        scratch_shapes=list(scratch) + comm.scratch, compiler_params=_cp(("arbitrary",) * len(grid)))(*args, *comm.arrs)


def _mm(a, b, *, mode, out_dtype, tm, tn, tk, name, a_fn=None, epi=None, extras=(), comm=None, shard_cols=None):
    if mode == "nn":
        (M, K), (K2, N) = a.shape, b.shape
    elif mode == "nt":
        (M, K), (N, K2) = a.shape, b.shape
    else:
        (K, M), (K2, N) = a.shape, b.shape
    assert K == K2, (a.shape, b.shape, mode)
    tm, tn, tk = min(tm, M), min(tn, N), min(tk, K)
    assert M % tm == 0 and N % tn == 0 and K % tk == 0, (M, N, K, tm, tn, tk)
    nk = K // tk
    if mode == "tn":
        a_spec = pl.BlockSpec((tk, tm), lambda i, j, k: (k, i))
        ca = 0
    else:
        a_spec = pl.BlockSpec((tm, tk), lambda i, j, k: (i, k))
        ca = 1
    if mode == "nt":
        b_spec = pl.BlockSpec((tn, tk), lambda i, j, k: (j, k))
        cb = 1
    else:
        b_spec = pl.BlockSpec((tk, tn), lambda i, j, k: (k, j))
        cb = 0
    dims = (((ca,), (cb,)), ((), ()))
    ne = len(extras)

    def body(a_ref, b_ref, *rest):
        ex = rest[:ne]
        o_ref = rest[ne]
        av = a_ref[...]
        if a_fn is not None:
            av = a_fn(av.astype(F32))
        p = lax.dot_general(av.astype(BF16), b_ref[...].astype(BF16), dims, preferred_element_type=F32)

        def fin(v):
            if epi is not None:
                v = epi(v, *[e[...] for e in ex])
            o_ref[...] = v.astype(out_dtype).reshape(o_ref.shape)

        if nk == 1:
            fin(p)
        else:
            acc = rest[ne + 1]
            k = pl.program_id(2)

            @pl.when(k == 0)
            def _():
                acc[...] = p

            @pl.when(k > 0)
            def _():
                acc[...] += p

            @pl.when(k == nk - 1)
            def _():
                fin(acc[...])

    if shard_cols is None:
        o_spec, o_shape = pl.BlockSpec((tm, tn), lambda i, j, k: (i, j)), SDS((M, N), out_dtype)
    else:
        per = shard_cols // tn
        assert shard_cols % tn == 0 and N % shard_cols == 0
        o_spec = pl.BlockSpec((1, tm, tn), lambda i, j, k: (lax.div(j, per), i, lax.rem(j, per)))
        o_shape = SDS((N // shard_cols, M, shard_cols), out_dtype)
    res = _call(name, body, (M // tm, N // tn, nk),
                [a_spec, b_spec] + [pl.BlockSpec((tm, tn), lambda i, j, k: (i, j)) for _ in extras],
                [o_spec], [o_shape],
                [pltpu.VMEM((tm, tn), F32)] if nk > 1 else [], ("parallel", "parallel", "arbitrary"),
                (a, b, *extras), comm)
    return res[0] if comm is None else (res[0], res[1:])


def _rowwise(fn, rows, bcasts, outs, accs, *, tr, name):
    T = rows[0][0].shape[0]
    tr = min(tr, T)
    assert T % tr == 0
    nr, nb, no, na = len(rows), len(bcasts), len(outs), len(accs)
    in_specs = [pl.BlockSpec((tr, w), functools.partial(lambda i, c: (i, c), c=cb)) for (_, w, cb) in rows]
    in_specs += [pl.BlockSpec(b.shape, lambda i: (0, 0)) for b in bcasts]
    out_shape = [SDS((T, w), dt) for (w, dt) in outs] + [SDS(s, F32) for s in accs]
    out_specs = [pl.BlockSpec((tr, w), lambda i: (i, 0)) for (w, _) in outs]
    out_specs += [pl.BlockSpec(s, lambda i: (0, 0)) for s in accs]

    def body(*refs):
        ins = [r[...].astype(F32) for r in refs[:nr + nb]]
        o_refs = refs[nr + nb:nr + nb + no]
        a_refs = refs[nr + nb + no:]
        ro, ao = fn(*ins)
        for r, v in zip(o_refs, ro):
            r[...] = v.astype(r.dtype)
        if na:
            @pl.when(pl.program_id(0) == 0)
            def _():
                for r in a_refs:
                    r[...] = jnp.zeros(r.shape, F32)

            for r, v in zip(a_refs, ao):
                r[...] += v

    res = pl.pallas_call(
        body,
        name=name,
        grid=(T // tr,),
        in_specs=in_specs,
        out_specs=out_specs,
        out_shape=out_shape,
        compiler_params=_cp(("arbitrary",) if na else ("parallel",)),
    )(*[r[0] for r in rows], *bcasts)
    return res


def _rms(v):
    r = lax.rsqrt(jnp.mean(v * v, axis=-1, keepdims=True) + EPS)
    return v * r, r


def _rms_bwd(dy, xn, r, g):
    dxn = dy * g
    dv = r * (dxn - xn * jnp.mean(dxn * xn, axis=-1, keepdims=True))
    return dv, jnp.sum(dy * xn, axis=0, keepdims=True)


def _sig(v):
    return 1.0 / (1.0 + jnp.exp(-v))


_GELU_C = math.sqrt(2.0 / math.pi)


def _gelu(v):
    return 0.5 * v * (1.0 + jnp.tanh(_GELU_C * (v + 0.044715 * v * v * v)))


def _gelu_grad(v):
    t = jnp.tanh(_GELU_C * (v + 0.044715 * v * v * v))
    return 0.5 * (1.0 + t) + 0.5 * v * (1.0 - t * t) * _GELU_C * (1.0 + 3.0 * 0.044715 * v * v)


def _rope(v, c, s, sign):
    w = v.shape[1]
    m = lax.broadcasted_iota(jnp.int32, v.shape, 1) % HEAD_DIM
    p = jnp.where(m < ROT_DIM // 2, -pltpu.roll(v, w - ROT_DIM // 2, 1), pltpu.roll(v, ROT_DIM // 2, 1))
    return v * c + sign * (p * s)


def _rope_tables(T):
    half = ROT_DIM // 2
    inv = ROPE_THETA ** (-jnp.arange(half, dtype=F32) * 2.0 / ROT_DIM)
    ang = jnp.arange(T).astype(F32)[:, None] * inv[None, :]
    cos, sin = jnp.cos(ang), jnp.sin(ang)
    one = jnp.ones((T, HEAD_DIM - ROT_DIM), F32)
    c64 = jnp.concatenate([cos, cos, one], axis=1)
    s64 = jnp.concatenate([sin, sin, 0.0 * one], axis=1)
    return jnp.tile(c64, (1, 2)), jnp.tile(s64, (1, 2))


def _dup_half(m, lo):
    lane = lax.broadcasted_iota(jnp.int32, m.shape, 1)
    sw = pltpu.roll(m, HEAD_DIM, 1)
    return jnp.where(lane < HEAD_DIM, m, sw) if lo else jnp.where(lane >= HEAD_DIM, m, sw)


def _attn_mask(i):
    qi = lax.broadcasted_iota(jnp.int32, (ATT_BLOCK, 2 * ATT_BLOCK), 0)
    kj = lax.broadcasted_iota(jnp.int32, (ATT_BLOCK, 2 * ATT_BLOCK), 1)
    rel = qi + ATT_BLOCK - kj
    return (rel >= 0) & (rel < ATT_BLOCK) & ((kj >= ATT_BLOCK) | (i > 0))


_NT = (((1,), (1,)), ((), ()))
_TN = (((0,), (0,)), ((), ()))


def _stack_heads(m):
    lane = lax.broadcasted_iota(jnp.int32, m.shape, 1)
    zero = jnp.zeros_like(m)
    return jnp.concatenate([jnp.where(lane < HEAD_DIM, m, zero), jnp.where(lane >= HEAD_DIM, m, zero)], axis=0)


def _pair_probs(q2, k2, ok2, sink_lo, sink_hi):
    qs = _stack_heads(q2)
    s = lax.dot_general(qs, k2, _NT, preferred_element_type=F32)
    s = jnp.where(ok2, s, NEG)
    row = lax.broadcasted_iota(jnp.int32, (2 * ATT_BLOCK, 1), 0)
    sink = jnp.where(row < ATT_BLOCK, sink_lo, sink_hi)
    m = jnp.maximum(jnp.max(s, axis=1, keepdims=True), sink)
    e = jnp.exp(s - m)
    es = jnp.exp(sink - m)
    inv = 1.0 / (jnp.sum(e, axis=1, keepdims=True) + es)
    return e * inv, es * inv, qs


def _attn_fwd(za, cos, sin, sinks, comm=None):
    T = za.shape[0]
    nb = T // ATT_BLOCK
    kvb = Q_W // (2 * KV_W)

    def body(sink_ref, q_ref, kvp_ref, kvc_ref, cc_ref, sc_ref, cp_ref, sp_ref, o_ref):
        i = pl.program_id(0)
        cc, sc, cp, sp = cc_ref[...], sc_ref[...], cp_ref[...], sp_ref[...]
        q = (_rope(q_ref[...], jnp.tile(cc, (1, 8)), jnp.tile(sc, (1, 8)), 1.0) * 0.125).astype(BF16)
        kvp, kvc = kvp_ref[...], kvc_ref[...]
        k = jnp.concatenate([_rope(kvp[:, :KV_W], cp, sp, 1.0), _rope(kvc[:, :KV_W], cc, sc, 1.0)], axis=0).astype(BF16)
        v = jnp.concatenate([kvp[:, KV_W:], kvc[:, KV_W:]], axis=0).astype(BF16)
        ok = _attn_mask(i)
        ok2 = jnp.concatenate([ok, ok], axis=0)
        lane = lax.broadcasted_iota(jnp.int32, (ATT_BLOCK, LANES), 1)
        for kvh in range(2):
            k2 = _dup_half(k, kvh == 0)
            v2 = _dup_half(v, kvh == 0)
            for pair in range(4):
                c0 = (kvh * 4 + pair) * LANES
                q2 = q[:, c0:c0 + LANES]
                p, _, _ = _pair_probs(q2, k2, ok2, sink_ref[0, 2 * (kvh * 4 + pair)], sink_ref[0, 2 * (kvh * 4 + pair) + 1])
                o = jnp.dot(p.astype(BF16), v2, preferred_element_type=F32)
                o_ref[:, c0:c0 + LANES] = jnp.where(lane < HEAD_DIM, o[:ATT_BLOCK], o[ATT_BLOCK:]).astype(BF16)

    blk = lambda w, f: pl.BlockSpec((ATT_BLOCK, w), f)
    res = _call(
        "attn_fwd", body, (nb,),
        [
            pl.BlockSpec(memory_space=pltpu.SMEM),
            blk(Q_W, lambda i: (i, 0)),
            blk(2 * KV_W, lambda i: (jnp.maximum(i - 1, 0), kvb)),
            blk(2 * KV_W, lambda i: (i, kvb)),
            blk(LANES, lambda i: (i, 0)),
            blk(LANES, lambda i: (i, 0)),
            blk(LANES, lambda i: (jnp.maximum(i - 1, 0), 0)),
            blk(LANES, lambda i: (jnp.maximum(i - 1, 0), 0)),
        ],
        [blk(Q_W, lambda i: (i, 0))], [SDS((T, Q_W), BF16)], [], ("parallel",),
        (sinks, za, za, za, cos, sin, cos, sin), comm)
    return res[0] if comm is None else (res[0], res[1:])


def _attn_bwd(za, cos, sin, sinks, o, do, comm=None):
    T = za.shape[0]
    nb = T // ATT_BLOCK
    kvb = Q_W // (2 * KV_W)

    def body(sink_ref, q_ref, kvp_ref, kvc_ref, cc_ref, sc_ref, cp_ref, sp_ref, o_ref, do_ref,
             dq_ref, dkv_ref, dsk_ref, carry, dqs):
        i = pl.program_id(0)

        @pl.when(i == 0)
        def _():
            carry[...] = jnp.zeros(carry.shape, F32)
            dsk_ref[...] = jnp.zeros(dsk_ref.shape, F32)

        @pl.when(i < nb)
        def _():
            cc, sc, cp, sp = cc_ref[...], sc_ref[...], cp_ref[...], sp_ref[...]
            ccq, scq = jnp.tile(cc, (1, 8)), jnp.tile(sc, (1, 8))
            q = (_rope(q_ref[...], ccq, scq, 1.0) * 0.125).astype(BF16)
            kvp, kvc = kvp_ref[...], kvc_ref[...]
            k = jnp.concatenate([_rope(kvp[:, :KV_W], cp, sp, 1.0), _rope(kvc[:, :KV_W], cc, sc, 1.0)], axis=0).astype(BF16)
            v = jnp.concatenate([kvp[:, KV_W:], kvc[:, KV_W:]], axis=0).astype(BF16)
            ok = _attn_mask(i)
            ok2 = jnp.concatenate([ok, ok], axis=0)
            lane = lax.broadcasted_iota(jnp.int32, (ATT_BLOCK, LANES), 1)
            lane_s = lax.broadcasted_iota(jnp.int32, (1, LANES), 1)
            dsk = jnp.zeros((1, LANES), F32)
            dkt_h, dvt_h = [], []
            for kvh in range(2):
                k2 = _dup_half(k, kvh == 0)
                v2 = _dup_half(v, kvh == 0)
                dkt = jnp.zeros((LANES, 2 * ATT_BLOCK), F32)
                dvt = jnp.zeros((LANES, 2 * ATT_BLOCK), F32)
                for pair in range(4):
                    h = 2 * (kvh * 4 + pair)
                    c0 = (kvh * 4 + pair) * LANES
                    do2 = do_ref[:, c0:c0 + LANES]
                    prod = do2.astype(F32) * o_ref[:, c0:c0 + LANES].astype(F32)
                    d_lo = jnp.sum(jnp.where(lane < HEAD_DIM, prod, 0.0), axis=1, keepdims=True)
                    d_hi = jnp.sum(jnp.where(lane >= HEAD_DIM, prod, 0.0), axis=1, keepdims=True)
                    delta = jnp.concatenate([d_lo, d_hi], axis=0)
                    p, p_sink, qs = _pair_probs(q[:, c0:c0 + LANES], k2, ok2, sink_ref[0, h], sink_ref[0, h + 1])
                    dos = _stack_heads(do2)
                    t = p_sink * delta
                    dsk = dsk - jnp.where(lane_s == h, jnp.sum(t[:ATT_BLOCK]), 0.0) \
                              - jnp.where(lane_s == h + 1, jnp.sum(t[ATT_BLOCK:]), 0.0)
                    dp = lax.dot_general(dos, v2, _NT, preferred_element_type=F32)
                    ds = (p * (dp - delta)).astype(BF16)
                    dqp = jnp.dot(ds, k2, preferred_element_type=F32)
                    dqs[:, c0:c0 + LANES] = jnp.where(lane < HEAD_DIM, dqp[:ATT_BLOCK], dqp[ATT_BLOCK:]) * 0.125
                    dkt = dkt + lax.dot_general(qs, ds, _TN, preferred_element_type=F32)
                    dvt = dvt + lax.dot_general(dos, p.astype(BF16), _TN, preferred_element_type=F32)
                dkt_h.append(dkt[:HEAD_DIM] + dkt[HEAD_DIM:])
                dvt_h.append(dvt[:HEAD_DIM] + dvt[HEAD_DIM:])
            dk = jnp.concatenate(dkt_h, axis=0).T
            dv = jnp.concatenate(dvt_h, axis=0).T
            dq_ref[...] = _rope(dqs[...], ccq, scq, -1.0).astype(dq_ref.dtype)
            dkp = _rope(dk[:ATT_BLOCK], cp, sp, -1.0)
            dkc = _rope(dk[ATT_BLOCK:], cc, sc, -1.0)
            dkv_ref[...] = (carry[...] + jnp.concatenate([dkp, dv[:ATT_BLOCK]], axis=1)).astype(dkv_ref.dtype)
            carry[...] = jnp.concatenate([dkc, dv[ATT_BLOCK:]], axis=1)
            dsk_ref[...] += dsk

        @pl.when(i == nb)
        def _():
            dkv_ref[...] = carry[...].astype(dkv_ref.dtype)

    blk = lambda w, f: pl.BlockSpec((ATT_BLOCK, w), f)
    cur = lambda i: jnp.minimum(i, nb - 1)
    prv = lambda i: jnp.maximum(jnp.minimum(i, nb - 1) - 1, 0)
    res = _call(
        "attn_bwd", body, (nb + 1,),
        [
            pl.BlockSpec(memory_space=pltpu.SMEM),
            blk(Q_W, lambda i: (cur(i), 0)),
            blk(2 * KV_W, lambda i: (prv(i), kvb)),
            blk(2 * KV_W, lambda i: (cur(i), kvb)),
            blk(LANES, lambda i: (cur(i), 0)),
            blk(LANES, lambda i: (cur(i), 0)),
            blk(LANES, lambda i: (prv(i), 0)),
            blk(LANES, lambda i: (prv(i), 0)),
            blk(Q_W, lambda i: (cur(i), 0)),
            blk(Q_W, lambda i: (cur(i), 0)),
        ],
        [
            blk(Q_W, lambda i: (cur(i), 0)),
            blk(2 * KV_W, lambda i: (jnp.maximum(i - 1, 0), 0)),
            pl.BlockSpec((1, LANES), lambda i: (0, 0)),
        ],
        [SDS((T, Q_W), BF16), SDS((T, 2 * KV_W), BF16), SDS((1, LANES), F32)],
        [pltpu.VMEM((ATT_BLOCK, 2 * KV_W), F32), pltpu.VMEM((ATT_BLOCK, Q_W), F32)],
        ("arbitrary",), (sinks, za, za, za, cos, sin, cos, sin, o, do), comm)
    return res if comm is None else (res[:3], res[3:])


def _s5_discretize(lam_re, lam_im, log_dt, b_re, b_im):
    dt = jnp.exp(log_dt)[:, None]
    mag = jnp.exp(lam_re * dt)
    a_re, a_im = mag * jnp.cos(lam_im * dt), mag * jnp.sin(lam_im * dt)
    den = lam_re * lam_re + lam_im * lam_im
    nr, ni = a_re - 1.0, a_im
    coef_re = (nr * lam_re + ni * lam_im) / den
    coef_im = (ni * lam_re - nr * lam_im) / den
    bb_re = coef_re[..., None] * b_re - coef_im[..., None] * b_im
    bb_im = coef_re[..., None] * b_im + coef_im[..., None] * b_re
    return a_re, a_im, bb_re, bb_im


def _blockdiag_in(bb):
    x = bb.reshape(N_JB, 8, SSM_P, SSM_GC).transpose(0, 1, 3, 2)
    return (x[:, :, :, None, :] * jnp.eye(8, dtype=bb.dtype)[None, :, None, :, None]).reshape(N_JB, 128, 512)


def _blockdiag_in_extract(m):
    x = m.reshape(N_JB, 8, SSM_GC, 8, SSM_P)
    x = jnp.einsum('jgchp,gh->jgcp', x, jnp.eye(8, dtype=m.dtype))
    return x.transpose(0, 1, 3, 2).reshape(SSM_G, SSM_P, SSM_GC)


def _blockdiag_out(c):
    x = c.reshape(N_JB, 8, SSM_GC, SSM_P).transpose(0, 1, 3, 2)
    return (x[:, :, :, None, :] * jnp.eye(8, dtype=c.dtype)[None, :, None, :, None]).reshape(N_JB, 512, 128)


def _blockdiag_out_extract(m):
    x = m.reshape(N_JB, 8, SSM_P, 8, SSM_GC)
    x = jnp.einsum('jgphc,gh->jgpc', x, jnp.eye(8, dtype=m.dtype))
    return x.transpose(0, 1, 3, 2).reshape(SSM_G, SSM_GC, SSM_P)


def _s5_tables(a_re, a_im):
    ar, ai = a_re.reshape(N_LG, 1, LANES), a_im.reshape(N_LG, 1, LANES)
    pr, pi, n = ar, ai, 1
    while n < S5_SEG:
        pr, pi, n = pr * pr - pi * pi, 2.0 * pr * pi, 2 * n
    assert n == S5_SEG
    bc = lambda v: jnp.broadcast_to(v, (N_LG, SUBLANES, LANES))
    return bc(ar), bc(ai), pr, pi


def _s5_to_time_major(src_ref, dst_ref):
    for t in range(S5_SEG):
        dst_ref[t * SUBLANES:(t + 1) * SUBLANES, :] = src_ref[pl.ds(t, SUBLANES, stride=S5_SEG), :]


def _s5_from_time_major(val, dst_ref):
    for t in range(S5_SEG):
        dst_ref[pl.ds(t, SUBLANES, stride=S5_SEG), :] = val[t * SUBLANES:(t + 1) * SUBLANES, :]


def _tm_rows(t, row0=0):
    return pl.ds(pl.multiple_of(t * SUBLANES + row0, SUBLANES), SUBLANES)


def _s5_scan(src_re, src_im, ar, ai, reverse, start=None, dst=None, dst_row0=0):
    def step(n, carry):
        t = (S5_SEG - 1 - n) if reverse else n
        out = []
        for ll in range(LG_PER_JB):
            xr, xi = carry[2 * ll], carry[2 * ll + 1]
            idx = (ll, _tm_rows(t), slice(None))
            nr = ar[ll] * xr - ai[ll] * xi + src_re[idx]
            ni = ar[ll] * xi + ai[ll] * xr + src_im[idx]
            if dst is not None:
                odx = (ll, _tm_rows(t, dst_row0), slice(None))
                dst[0][odx] = nr
                dst[1][odx] = ni
            out += [nr, ni]
        return tuple(out)
    if start is None:
        init = (jnp.zeros((SUBLANES, LANES), F32),) * (2 * LG_PER_JB)
    else:
        init = tuple(s[ll] for ll in range(LG_PER_JB) for s in start)
    return lax.fori_loop(0, S5_SEG, step, init)


def _s5_fixup(ends, in_re, in_im, mr, mi, s_re, s_im, reverse):
    cr, ci = in_re, in_im
    order = range(SUBLANES - 1, -1, -1) if reverse else range(SUBLANES)
    for s in order:
        s_re[:, s:s + 1, :] = cr
        s_im[:, s:s + 1, :] = ci
        er = jnp.stack([ends[2 * ll][s:s + 1, :] for ll in range(LG_PER_JB)])
        ei = jnp.stack([ends[2 * ll + 1][s:s + 1, :] for ll in range(LG_PER_JB)])
        cr, ci = mr * cr - mi * ci + er, mr * ci + mi * cr + ei
    return cr, ci


def _s5_specs(nc, rev):
    cidx = (lambda c: nc - 1 - c) if rev else (lambda c: c)
    jb = lambda shape: pl.BlockSpec(shape, lambda j, c: (j, 0, 0))
    return cidx, [
        jb((1, LANES, 8 * LANES)),
        jb((1, 8 * LANES, LANES)),
        pl.BlockSpec((1, LANES), lambda j, c: (0, j)),
        jb((LG_PER_JB, SUBLANES, LANES)), jb((LG_PER_JB, SUBLANES, LANES)),
        jb((LG_PER_JB, 1, LANES)), jb((LG_PER_JB, 1, LANES)),
    ]


def _s5_fwd(za, prm, comm=None):
    T = za.shape[0]
    R = S5_CHUNK
    nc = T // R
    ub = (Q_W + 2 * KV_W) // LANES
    _, pspecs = _s5_specs(nc, False)

    def body(u_ref, b_ref, c_ref, d_ref, are_ref, aim_ref, alr_ref, ali_ref,
             yg_ref, x0r_ref, x0i_ref, bur, bui, xsr, xsi, sr, si, xcr, xci, utm, ynat):
        c = pl.program_id(1)

        @pl.when(c == 0)
        def _():
            xcr[...] = jnp.zeros(xcr.shape, F32)
            xci[...] = jnp.zeros(xci.shape, F32)

        _s5_to_time_major(u_ref, utm)
        u = utm[...]
        ub16 = u.astype(BF16)
        bu = jnp.dot(ub16, b_ref[0].astype(BF16), preferred_element_type=F32)
        for ll in range(LG_PER_JB):
            bur[ll] = bu[:, ll * LANES:(ll + 1) * LANES]
            bui[ll] = bu[:, (LG_PER_JB + ll) * LANES:(LG_PER_JB + ll + 1) * LANES]
        ar = [are_ref[ll] for ll in range(LG_PER_JB)]
        ai = [aim_ref[ll] for ll in range(LG_PER_JB)]
        ends = _s5_scan(bur, bui, ar, ai, False)
        in_r, in_i = xcr[...], xci[...]
        x0r_ref[0] = in_r
        x0i_ref[0] = in_i
        out_r, out_i = _s5_fixup(ends, in_r, in_i, alr_ref[...], ali_ref[...], sr, si, False)
        xcr[...] = out_r
        xci[...] = out_i
        _s5_scan(bur, bui, ar, ai, False, start=(sr, si), dst=(xsr, xsi))
        xcat =jnp.concatenate([xsr[ll].astype(BF16) for ll in range(LG_PER_JB)]
                               + [xsi[ll].astype(BF16) for ll in range(LG_PER_JB)], axis=1)
        y = d_ref[...] * u + jnp.dot(xcat, c_ref[0].astype(BF16), preferred_element_type=F32)
        _s5_from_time_major(_gelu(y), ynat)
        yg_ref[...] = ynat[...].astype(BF16)

    st = pl.BlockSpec((1, LG_PER_JB, 1, LANES), lambda j, c: (c, j, 0, 0))
    vm = lambda rows: pltpu.VMEM((LG_PER_JB, rows, LANES), F32)
    res = _call(
        "s5_fwd", body, (N_JB, nc),
        [pl.BlockSpec((R, LANES), lambda j, c: (c, ub + j))] + pspecs,
        [pl.BlockSpec((R, LANES), lambda j, c: (c, j)), st, st],
        [SDS((T, SSM_W), BF16), SDS((nc, N_LG, 1, LANES), F32), SDS((nc, N_LG, 1, LANES), F32)],
        [vm(R), vm(R), vm(R), vm(R), vm(SUBLANES), vm(SUBLANES), vm(1), vm(1),
         pltpu.VMEM((R, LANES), F32), pltpu.VMEM((R, LANES), F32)],
        ("parallel", "arbitrary"), (za, *prm), comm)
    return res if comm is None else (res[:3], res[3:])


def _s5_bwd(za, dyg, x0r, x0i, prm, comm=None):
    T = za.shape[0]
    R = S5_CHUNK
    nc = T // R
    ub = (Q_W + 2 * KV_W) // LANES
    cidx, pspecs = _s5_specs(nc, True)
    PAD = SUBLANES

    def body(u_ref, dyg_ref, x0r_ref, x0i_ref, b_ref, c_ref, d_ref, are_ref, aim_ref,
             alr_ref, ali_ref,
             du_ref, dar_ref, dai_ref, db_ref, dc_ref, dd_ref,
             bur, bui, xsr, xsi, sr, si, gcr, gci, utm, dtm, dunat):
        c = pl.program_id(1)

        @pl.when(c == 0)
        def _():
            gcr[...] = jnp.zeros(gcr.shape, F32)
            gci[...] = jnp.zeros(gci.shape, F32)
            dar_ref[...] = jnp.zeros(dar_ref.shape, F32)
            dai_ref[...] = jnp.zeros(dai_ref.shape, F32)
            db_ref[...] = jnp.zeros(db_ref.shape, F32)
            dc_ref[...] = jnp.zeros(dc_ref.shape, F32)
            dd_ref[...] = jnp.zeros(dd_ref.shape, F32)

        _s5_to_time_major(u_ref, utm)
        _s5_to_time_major(dyg_ref, dtm)
        u = utm[...]
        ub16 = u.astype(BF16)
        bcat, ccat = b_ref[0].astype(BF16), c_ref[0].astype(BF16)
        lanes = lambda v, ll: v[:, ll * LANES:(ll + 1) * LANES]
        bu = jnp.dot(ub16, bcat, preferred_element_type=F32)
        for ll in range(LG_PER_JB):
            bur[ll] = lanes(bu, ll)
            bui[ll] = lanes(bu, LG_PER_JB + ll)
        ar = [are_ref[ll] for ll in range(LG_PER_JB)]
        ai = [aim_ref[ll] for ll in range(LG_PER_JB)]
        ends = _s5_scan(bur, bui, ar, ai, False)
        in_r, in_i = x0r_ref[0], x0i_ref[0]
        _s5_fixup(ends, in_r, in_i, alr_ref[...], ali_ref[...], sr, si, False)
        _s5_scan(bur, bui, ar, ai, False, start=(sr, si), dst=(xsr, xsi), dst_row0=PAD)
        xsr[:, 0:PAD, :] = sr[...]
        xsi[:, 0:PAD, :] = si[...]
        xcat = jnp.concatenate([xsr[ll, PAD:, :].astype(BF16) for ll in range(LG_PER_JB)]
                               + [xsi[ll, PAD:, :].astype(BF16) for ll in range(LG_PER_JB)], axis=1)
        y = d_ref[...] * u + jnp.dot(xcat, ccat, preferred_element_type=F32)
        dy = dtm[...] * _gelu_grad(y)
        dyb = dy.astype(BF16)
        dd_ref[...] += jnp.sum(dy * u, axis=0, keepdims=True)
        du = d_ref[...] * dy
        dc_ref[0] += lax.dot_general(dyb, xcat, _TN, preferred_element_type=F32)
        g = lax.dot_general(dyb, ccat, _NT, preferred_element_type=F32)
        for ll in range(LG_PER_JB):
            bur[ll] = lanes(g, ll)
            bui[ll] = lanes(g, LG_PER_JB + ll)
        aic = [-v for v in ai]
        ends = _s5_scan(bur, bui, ar, aic, True)
        out_r, out_i = _s5_fixup(ends, gcr[...], gci[...], alr_ref[...], -ali_ref[...], sr, si, True)
        gcr[...] = out_r
        gci[...] = out_i
        _s5_scan(bur, bui, ar, aic, True, start=(sr, si), dst=(bur, bui))
        for ll in range(LG_PER_JB):
            gr, gi = bur[ll], bui[ll]
            xpr, xpi = xsr[ll, 0:R, :], xsi[ll, 0:R, :]
            red = lambda v: v.reshape(R // SUBLANES, SUBLANES, LANES).sum(axis=0)
            dar_ref[ll] += red(xpr * gr + xpi * gi)
            dai_ref[ll] += red(xpr * gi - xpi * gr)
        gcat = jnp.concatenate([bur[ll].astype(BF16) for ll in range(LG_PER_JB)]
                               + [bui[ll].astype(BF16) for ll in range(LG_PER_JB)], axis=1)
        db_ref[0] += lax.dot_general(ub16, gcat, _TN, preferred_element_type=F32)
        du = du + lax.dot_general(gcat, bcat, _NT, preferred_element_type=F32)
        _s5_from_time_major(du, dunat)
        du_ref[...] = dunat[...].astype(du_ref.dtype)

    st = pl.BlockSpec((1, LG_PER_JB, 1, LANES), lambda j, c: (cidx(c), j, 0, 0))
    jb = lambda shape: pl.BlockSpec(shape, lambda j, c: (j, 0, 0))
    vm = lambda rows: pltpu.VMEM((LG_PER_JB, rows, LANES), F32)
    res = _call(
        "s5_bwd", body, (N_JB, nc),
        [pl.BlockSpec((R, LANES), lambda j, c: (cidx(c), ub + j)),
         pl.BlockSpec((R, LANES), lambda j, c: (cidx(c), j)), st, st] + pspecs,
        [pl.BlockSpec((R, LANES), lambda j, c: (cidx(c), j)),
         jb((LG_PER_JB, SUBLANES, LANES)), jb((LG_PER_JB, SUBLANES, LANES)),
         jb((1, LANES, 8 * LANES)), jb((1, LANES, 8 * LANES)),
         pl.BlockSpec((1, LANES), lambda j, c: (0, j))],
        [SDS((T, SSM_W), BF16), SDS((N_LG, SUBLANES, LANES), F32), SDS((N_LG, SUBLANES, LANES), F32),
         SDS((N_JB, LANES, 8 * LANES), F32), SDS((N_JB, LANES, 8 * LANES), F32), SDS((1, SSM_W), F32)],
        [vm(R), vm(R), vm(R + PAD), vm(R + PAD), vm(SUBLANES), vm(SUBLANES), vm(1), vm(1)]
        + [pltpu.VMEM((R, LANES), F32)] * 3,
        ("parallel", "arbitrary"), (za, dyg, x0r, x0i, *prm), comm)
    return res if comm is None else (res[:6], res[6:])


def _assemble_w_a(wi):
    _, rows, cb = wi.shape
    tr = 256

    def body(w_ref, a_ref):
        a_ref[:, :cb] = w_ref[0]
        a_ref[:, cb:] = w_ref[1, :, :ZA_W - cb]

    return pl.pallas_call(
        body, name="assemble_w_a", grid=(rows // tr,),
        in_specs=[pl.BlockSpec((2, tr, cb), lambda i: (0, i, 0))],
        out_specs=pl.BlockSpec((tr, ZA_W), lambda i: (i, 0)),
        out_shape=SDS((rows, ZA_W), wi.dtype), compiler_params=_cp(("parallel",)))(wi)


def _assemble_w_g(wi):
    _, rows, cb = wi.shape
    tr = 256
    cut = ZA_W - cb

    def body(w_ref, g_ref):
        g_ref[:, :cb - cut] = w_ref[1, :, cut:]
        g_ref[:, cb - cut:2 * cb - cut] = w_ref[2]
        g_ref[:, 2 * cb - cut:] = w_ref[3]

    return pl.pallas_call(
        body, name="assemble_w_g", grid=(rows // tr,),
        in_specs=[pl.BlockSpec((4, tr, cb), lambda i: (0, i, 0))],
        out_specs=pl.BlockSpec((tr, 4 * cb - ZA_W), lambda i: (i, 0)),
        out_shape=SDS((rows, 4 * cb - ZA_W), wi.dtype), compiler_params=_cp(("parallel",)))(wi)


def _stack_w_in_grad(d_w_a, d_w_g):
    rows = d_w_a.shape[0]
    cb = (ZA_W + d_w_g.shape[1]) // 4
    cut = ZA_W - cb
    tr = 256

    def body(a_ref, g_ref, o_ref):
        o_ref[0] = a_ref[:, :cb]
        o_ref[1, :, :cut] = a_ref[:, cb:]
        o_ref[1, :, cut:] = g_ref[:, :cb - cut]
        o_ref[2] = g_ref[:, cb - cut:2 * cb - cut]
        o_ref[3] = g_ref[:, 2 * cb - cut:]

    return pl.pallas_call(
        body, name="stack_w_in_grad", grid=(rows // tr,),
        in_specs=[pl.BlockSpec((tr, ZA_W), lambda i: (i, 0)), pl.BlockSpec((tr, d_w_g.shape[1]), lambda i: (i, 0))],
        out_specs=pl.BlockSpec((4, tr, cb), lambda i: (0, i, 0)),
        out_shape=SDS((4, rows, cb), d_w_a.dtype), compiler_params=_cp(("parallel",)))(d_w_a, d_w_g)


def _local_step(x, target, gains, w_a_of, sinks, s5w, comms, late_g, late, red=None):
    T = x.shape[0]
    D = D_MODEL
    g1, g2, g3, g4 = gains
    cos, sin = _rope_tables(T)
    lam_re, lam_im, log_dt, b_re, b_im, c_re, c_im, d_skip = s5w
    (a_re, a_im, bb_re, bb_im), disc_vjp = jax.vjp(_s5_discretize, lam_re, lam_im, log_dt, b_re, b_im)
    abr, abi, al_re, al_im = _s5_tables(a_re, a_im)
    prm = (jnp.concatenate([_blockdiag_in(bb_re), _blockdiag_in(bb_im)], axis=2),
           jnp.concatenate([_blockdiag_out(c_re), -_blockdiag_out(c_im)], axis=1),
           d_skip.reshape(1, SSM_W), abr, abi, al_re, al_im)
    mm = functools.partial(_mm, tm=1024, tn=1024, tk=2048)

    h = _rowwise(lambda xv, g: ((_rms(xv)[0] * g,), ()), [(x, D, 0)], [g1], [(D, BF16)], [], tr=512, name="norm1")[0]
    unpack = lambda res, comm: (res, ()) if comm is None else res
    w_a = w_a_of(h, *prm, cos, sin)
    za, got_a = unpack(_mm(h, w_a, mode="nn", out_dtype=F32, tm=1024, tn=1152, tk=2048, name="mm_za", comm=comms[0]), comms[0])
    w_g = late_g(got_a)
    zg, got0 = unpack(mm(h, w_g, mode="nn", out_dtype=BF16, name="mm_zg", tn=2048, comm=comms[1]), comms[1])
    o_attn, got1 = unpack(_attn_fwd(za, cos, sin, sinks, comm=comms[2]), comms[2])
    (yg, x0r, x0i), got2 = unpack(_s5_fwd(za, prm, comm=comms[3]), comms[3])
    w_glu, w_ba, w_bs, w_out, w_up, w_down = late(got0, got1, got2)
    zglu = mm(yg, w_glu, mode="nn", out_dtype=BF16, name="mm_glu")
    o_ssm = _rowwise(lambda z1, z2: ((z1 * _sig(z2),), ()), [(zglu, SSM_W, 0), (zglu, SSM_W, 1)], [],
                     [(SSM_W, BF16)], [], tr=512, name="glu")[0]
    ya = mm(o_attn, w_ba, mode="nn", out_dtype=BF16, name="mm_ya")
    ys = mm(o_ssm, w_bs, mode="nn", out_dtype=BF16, name="mm_ys")
    mi = _rowwise(lambda ga, gs, a, s: ((_sig(ga) * a + _sig(gs) * s,), ()),
                  [(zg, D, 0), (zg, D, 1), (ya, D, 0), (ys, D, 0)], [], [(D, BF16)], [], tr=256, name="gate")[0]
    mixed = mm(mi, w_out, mode="nn", out_dtype=F32, name="mm_out")

    def f_post(xv, mv, g2v, g3v):
        x1v = xv + _rms(mv)[0] * g2v
        return (x1v, _rms(x1v)[0] * g3v), ()
    x1, h2 = _rowwise(f_post, [(x, D, 0), (mixed, D, 0)], [g2, g3], [(D, F32), (D, BF16)], [], tr=256, name="post_mix")
    act = mm(h2, w_up, mode="nn", out_dtype=BF16, name="mm_up", tn=2048, epi=lambda v: jnp.maximum(v, 0.0))
    f = mm(act, w_down, mode="nn", out_dtype=F32, name="mm_down", a_fn=lambda v: v * v, tk=4096)

    def f_final(x1v, fv, tv, g4v):
        fn, r = _rms(fv)
        e = x1v + fn * g4v - tv
        dx2v = e * (1.0 / D)
        dfv, dg4v = _rms_bwd(dx2v, fn, r, g4v)
        return (dfv, dx2v), (dg4v, jnp.zeros((SUBLANES, LANES), F32) + 0.5 * jnp.sum(e * e) * (1.0 / D))
    df, dx2, dg4, lossb = _rowwise(f_final, [(x1, D, 0), (f, D, 0), (target, D, 0)], [g4],
                                   [(D, BF16), (D, F32)], [(1, D), (SUBLANES, LANES)], tr=256, name="final")

    big = {}

    def add(k, g4):
        big[k] = g4
        if red is not None:
            red.add(k, g4)

    def hosted(fn, stage, names):
        if red is None:
            return fn(comm=None)
        out, got = fn(comm=getattr(red, stage)(names))
        getattr(red, stage + "_done")(names, got)
        return out

    dpre = mm(df, w_down, mode="nt", out_dtype=BF16, name="mm_dact", tn=2048,
              epi=lambda v, a: v * (2.0 * a.astype(F32)), extras=(act,))
    wg = functools.partial(_mm, mode="tn", out_dtype=F32, tm=1024, tn=1024, tk=4096)
    add("w_down", wg(act, df, name="wg_down", a_fn=lambda v: v * v).reshape(4, D_FF // 4, D))
    dh2 = hosted(functools.partial(mm, dpre, w_up, mode="nt", out_dtype=F32, name="mm_dh2", tk=4096),
                 "s1", ["w_down"])
    add("w_up", hosted(functools.partial(wg, h2, dpre, name="wg_up", shard_cols=D_FF // 4), "s3", ["w_down"]))

    def f_mid(dx2v, dh2v, x1v, mv, g2v, g3v):
        x1n, r3 = _rms(x1v)
        d3, dg3v = _rms_bwd(dh2v, x1n, r3, g3v)
        dx1v = dx2v + d3
        mn, r2 = _rms(mv)
        dmv, dg2v = _rms_bwd(dx1v, mn, r2, g2v)
        return (dx1v, dmv), (dg3v, dg2v)
    dx1, dmixed, dg3, dg2 = _rowwise(f_mid, [(dx2, D, 0), (dh2, D, 0), (x1, D, 0), (mixed, D, 0)], [g2, g3],
                                     [(D, F32), (D, BF16)], [(1, D), (1, D)], tr=256, name="mid")

    dmi = hosted(functools.partial(mm, dmixed, w_out, mode="nt", out_dtype=BF16, name="mm_dmi"), "s1", ["w_up"])
    add("w_out", wg(mi, dmixed, name="wg_out").reshape(4, D // 4, D))

    def f_gate(dv, ga, gs, a, s):
        sa, ss = _sig(ga), _sig(gs)
        return (dv * sa, dv * ss, jnp.concatenate([dv * a * sa * (1.0 - sa), dv * s * ss * (1.0 - ss)], axis=1)), ()
    dya, dys, dzg = _rowwise(f_gate, [(dmi, D, 0), (zg, D, 0), (zg, D, 1), (ya, D, 0), (ys, D, 0)], [],
                             [(D, BF16), (D, BF16), (2 * D, BF16)], [], tr=256, name="gate_bwd")
    do_attn = hosted(functools.partial(mm, dya, w_ba, mode="nt", out_dtype=BF16, name="mm_doa"), "s1", ["w_out"])
    d_w_ba = wg(o_attn, dya, name="wg_ba")
    do_ssm = mm(dys, w_bs, mode="nt", out_dtype=BF16, name="mm_dos")
    d_w_bs = wg(o_ssm, dys, name="wg_bs")
    add("w_branch", jnp.concatenate([d_w_ba.reshape(2, D // 4, D), d_w_bs.reshape(2, D // 4, D)], axis=0))

    def f_glu(dv, z1, z2):
        s2 = _sig(z2)
        return (jnp.concatenate([dv * s2, dv * z1 * s2 * (1.0 - s2)], axis=1),), ()
    dzglu = _rowwise(f_glu, [(do_ssm, SSM_W, 0), (zglu, SSM_W, 0), (zglu, SSM_W, 1)], [], [(2 * SSM_W, BF16)], [],
                     tr=512, name="glu_bwd")[0]
    dyg = hosted(functools.partial(mm, dzglu, w_glu, mode="nt", out_dtype=F32, name="mm_dyg"), "s1", ["w_branch"])
    add("w_glu", wg(yg, dzglu, name="wg_glu", tn=SSM_W // 2, shard_cols=SSM_W // 2))
    du, dar, dai, dbc, dcc, ddv = hosted(functools.partial(_s5_bwd, za, dyg, x0r, x0i, prm),
                                         "s3", ["w_up", "w_out", "w_branch"])
    dbr, dbi = dbc[:, :, :4 * LANES], dbc[:, :, 4 * LANES:]
    dcc = dcc.transpose(0, 2, 1)
    dcr, dci = dcc[:, :4 * LANES, :], -dcc[:, 4 * LANES:, :]
    dq, dkv, dsk = hosted(functools.partial(_attn_bwd, za, cos, sin, sinks, o_attn, do_attn),
                          "s5", ["w_down", "w_up", "w_out", "w_branch"])
    dza = jnp.concatenate([dq, dkv, du], axis=1)
    d_w_a = _mm(h, dza, mode="tn", out_dtype=F32, tm=1024, tn=ZA_W // 2, tk=2048, name="wg_a")
    d_w_g = wg(h, dzg, name="wg_g")
    add("w_in", _stack_w_in_grad(d_w_a, d_w_g))
    dh = hosted(functools.partial(mm, dza, w_a, mode="nt", out_dtype=F32, name="mm_dh_a", tk=ZA_W), "s1", ["w_in", "w_glu"])
    dh = hosted(functools.partial(mm, dzg, w_g, mode="nt", out_dtype=F32, name="mm_dh_g",
                                  epi=lambda v, p: v + p, extras=(dh,)), "s3", ["w_in", "w_glu"])

    def f_first(dx1v, dhv, xv, g1v):
        xn, r1 = _rms(xv)
        d1, dg1v = _rms_bwd(dhv, xn, r1, g1v)
        return (dx1v + d1,), (dg1v,)
    dx, dg1 = _rowwise(f_first, [(dx1, D, 0), (dh, D, 0), (x, D, 0)], [g1], [(D, F32)], [(1, D)], tr=256, name="first")

    da_re = dar.sum(axis=1).reshape(SSM_G, SSM_P)
    da_im = dai.sum(axis=1).reshape(SSM_G, SSM_P)
    d_lam_re, d_lam_im, d_log_dt, d_b_re, d_b_im = disc_vjp(
        (da_re, da_im, _blockdiag_in_extract(dbr), _blockdiag_in_extract(dbi)))
    small = dict(norm_mix_pre=dg1, norm_mix_post=dg2, norm_mlp_pre=dg3, norm_mlp_post=dg4,
                 sinks=dsk[:, :N_Q_HEADS], lam_re=d_lam_re, lam_im=d_lam_im, log_dt=d_log_dt,
                 b_re=d_b_re, b_im=d_b_im, c_re=_blockdiag_out_extract(dcr), c_im=_blockdiag_out_extract(dci),
                 d_skip=ddv.reshape(SSM_G, SSM_GC))
    return lossb[0, 0], dx, small, big


def _cast_into_slot(w, k_arr):
    rows, cols = w.shape
    tr = 256

    def body(k_ref, w_ref, o_ref):
        o_ref[0] = w_ref[...].astype(BF16)

    return pl.pallas_call(
        body,
        name="cast_into_slot",
        grid_spec=pltpu.PrefetchScalarGridSpec(
            num_scalar_prefetch=1,
            grid=(rows // tr,),
            in_specs=[pl.BlockSpec((tr, cols), lambda i, k: (i, 0))],
            out_specs=pl.BlockSpec((1, tr, cols), lambda i, k: (k[0], i, 0)),
        ),
        out_shape=SDS((4, rows, cols), BF16),
        compiler_params=_cp(("parallel",)),
    )(k_arr, w)


def _pair_sum(g, r, c_arr):
    _, _, hr, cols = g.shape
    tr = min(256, hr)

    def body(c_ref, g_ref, r_ref, o_ref):
        o_ref[0] = (g_ref[0, 0] + r_ref[0]).astype(BF16)

    return pl.pallas_call(
        body,
        name="pair_sum",
        grid_spec=pltpu.PrefetchScalarGridSpec(
            num_scalar_prefetch=1,
            grid=(3, hr // tr),
            in_specs=[pl.BlockSpec((1, 1, tr, cols), lambda k, i, c_ref: (c_ref[1 + k], c_ref[0], i, 0)),
                      pl.BlockSpec((1, tr, cols), lambda k, i, c_ref: (c_ref[1 + k], i, 0))],
            out_specs=pl.BlockSpec((1, tr, cols), lambda k, i, c_ref: (c_ref[1 + k], i, 0)),
        ),
        out_shape=SDS((4, hr, cols), BF16),
        compiler_params=_cp(("parallel", "parallel")),
    )(c_arr, g, r)


def _chip_sum(g, r, q, kc_arr):
    _, _, hr, cols = g.shape
    tr = min(256, hr)

    def body(kc_ref, g_ref, r_ref, q_ref, o_ref):
        s = g_ref[0, 0] + r_ref[0]
        for j in range(3):
            s = s + q_ref[j].astype(F32)
        o_ref[...] = s

    return pl.pallas_call(
        body,
        name="chip_sum",
        grid_spec=pltpu.PrefetchScalarGridSpec(
            num_scalar_prefetch=1,
            grid=(hr // tr,),
            in_specs=[pl.BlockSpec((1, 1, tr, cols), lambda i, kc: (kc[0], kc[1], i, 0)),
                      pl.BlockSpec((1, tr, cols), lambda i, kc: (kc[0], i, 0)),
                      pl.BlockSpec((3, tr, cols), lambda i, kc: (0, i, 0))],
            out_specs=pl.BlockSpec((tr, cols), lambda i, kc: (kc[1] * (hr // tr) + i, 0)),
        ),
        out_shape=SDS((2 * hr, cols), F32),
        compiler_params=_cp(("parallel",)),
    )(kc_arr, g, r, q)


class _PairShareComm:
    aliased = True

    def __init__(self, blocks):
        self.arrs = list(blocks)
        self.n = len(self.arrs)
        dma = pltpu.SemaphoreType.DMA
        self.scratch = [dma((self.n,)), dma((self.n,))]
        self.out_shape = [SDS(b.shape, b.dtype) for b in self.arrs]

    def _copies(self, ins, outs, sems, hc):
        ssem, rsem = sems
        x, y, c, _ = _place()
        cps = []
        for w in range(self.n):
            hr = ins[w].shape[0] // 2
            rows = pl.ds(pl.multiple_of((c if hc == 0 else 1 - c) * hr, 8), hr)
            cps.append(_remote(ins[w].at[rows, :], outs[w].at[rows, :], ssem.at[w], rsem.at[w], (x, y, 1 - c)))
        return cps

    def start(self, ins, outs, sems):
        for cp in self._copies(ins, outs, sems, 0):
            cp.start()

    def finish(self, ins, outs, sems):
        for cp in self._copies(ins, outs, sems, 1):
            cp.wait_recv()
        for cp in self._copies(ins, outs, sems, 0):
            cp.wait_send()


class _GradReducer:
    def __init__(self, c_arr, kc_arr):
        self.c_arr, self.kc_arr = c_arr, kc_arr
        self.g, self.r, self.ps, self.q, self.done = {}, {}, {}, {}, {}

    def add(self, k, g4):
        self.g[k] = g4.reshape(4, 2, g4.shape[1] // 2, g4.shape[2])

    def s1(self, names):
        return _PairExchangeComm([self.g[k].reshape(4, -1, self.g[k].shape[3]) for k in names])

    def s1_done(self, names, got):
        for k, r in zip(names, got):
            self.r[k] = r
            self.ps[k] = _pair_sum(self.g[k], r, self.c_arr)

    def s3(self, names):
        return _ChipExchangeComm([self.ps[k] for k in names])

    def s3_done(self, names, got):
        self.q.update(zip(names, got))

    def finish(self, order):
        rest = [k for k in order if k not in self.r]
        if rest:
            self.s1_done(rest, _comm_only("pair_exchange", self.s1(rest)))
        rest = [k for k in order if k not in self.q]
        if rest:
            self.s3_done(rest, _comm_only("chip_exchange", self.s3(rest)))
        rest = [k for k in order if k not in self.done]
        if rest:
            self.s5_done(rest, _comm_only("pair_share", self.s5(rest)))
        return self.done

    def s5(self, names):
        return _PairShareComm([_chip_sum(self.g[k], self.r[k], self.q[k], self.kc_arr) for k in names])

    def s5_done(self, names, got):
        self.done.update(zip(names, got))


def _all_reduce_small(buf):
    rows = buf.shape[0]
    hr = rows // 2
    assert hr % SUBLANES == 0

    def body(in_ref, o_ref, sib, pair, slots, ssem, rsem):
        x, y, c, others = _place()
        me, sibling = 2 * x + y, (x, y, 1 - c)
        mine = pl.ds(pl.multiple_of(c * hr, SUBLANES), hr)
        theirs = pl.ds(pl.multiple_of((1 - c) * hr, SUBLANES), hr)
        first = _remote(in_ref, sib, ssem.at[0], rsem.at[0], sibling)
        first.start()
        first.wait()
        pair[...] = in_ref[...] + sib[...]
        slots[me] = pair[mine, :]
        cps = [_remote(pair.at[mine, :], slots.at[me], ssem.at[1 + r], rsem.at[1 + r], (ox, oy, c))
               for r, (ox, oy) in enumerate(others)]
        for cp in cps:
            cp.start()
        for r, (ox, oy) in enumerate(others):
            _remote(pair.at[mine, :], slots.at[2 * ox + oy], ssem.at[1 + r], rsem.at[1 + r], (ox, oy, c)).wait_recv()
        o_ref[mine, :] = (slots[0] + slots[1]) + (slots[2] + slots[3])
        last = _remote(o_ref.at[mine, :], o_ref.at[mine, :], ssem.at[4], rsem.at[4], sibling)
        last.start()
        _remote(o_ref.at[theirs, :], o_ref.at[theirs, :], ssem.at[4], rsem.at[4], sibling).wait_recv()
        last.wait_send()
        for cp in cps:
            cp.wait_send()

    dma = pltpu.SemaphoreType.DMA
    return pl.pallas_call(
        body,
        name="all_reduce_small",
        in_specs=[pl.BlockSpec(memory_space=pltpu.VMEM)],
        out_specs=pl.BlockSpec(memory_space=pltpu.VMEM),
        out_shape=SDS(buf.shape, F32),
        scratch_shapes=[pltpu.VMEM((rows, LANES), F32), pltpu.VMEM((rows, LANES), F32),
                        pltpu.VMEM((4, hr, LANES), F32), dma((5,)), dma((5,))],
        compiler_params=pltpu.CompilerParams(vmem_limit_bytes=VMEM_LIMIT),
    )(buf)


def _adam_fn(w, g, m, v):
    m2 = ADAM_B1 * m + (1.0 - ADAM_B1) * g
    v2 = ADAM_B2 * v + (1.0 - ADAM_B2) * (g * g)
    m_hat = m2 / (1.0 - ADAM_B1 ** ADAM_STEP)
    v_hat = v2 / (1.0 - ADAM_B2 ** ADAM_STEP)
    return (-ADAM_LR * (m_hat / (jnp.sqrt(v_hat) + ADAM_EPS) + ADAM_WD * w), m2, v2), ()


def _adamw(w, g, m, v, name, tr=256):
    cols = w.shape[1]
    return _rowwise(_adam_fn, [(w, cols, 0), (g, cols, 0), (m, cols, 0), (v, cols, 0)], [],
                    [(cols, F32)] * 3, [], tr=tr, name=name)


BIG = ("w_in", "w_glu", "w_branch", "w_out", "w_up", "w_down")
COL_SHARDED = ("w_in", "w_glu", "w_up")
SMALL = ("norm_mix_pre", "norm_mix_post", "norm_mlp_pre", "norm_mlp_post", "sinks", "lam_re", "lam_im", "log_dt",
         "b_re", "b_im", "c_re", "c_im", "d_skip")
WEIGHTS = ("norm_mix_pre", "norm_mix_post", "norm_mlp_pre", "norm_mlp_post", "w_in", "sinks", "lam_re", "lam_im",
           "log_dt", "b_re", "b_im", "c_re", "c_im", "d_skip", "w_glu", "w_branch", "w_out", "w_up", "w_down")


def _flat_small(vals, extra):
    flat = jnp.concatenate([vals[k].reshape(-1) for k in SMALL] + [extra.reshape(-1)])
    rows = -(-flat.shape[0] // (SUBLANES * LANES)) * SUBLANES
    return jnp.pad(flat, (0, rows * LANES - flat.shape[0])).reshape(rows, LANES)


def kernel(x, norm_mix_pre, norm_mix_post, norm_mlp_pre, norm_mlp_post, w_in, sinks, lam_re, lam_im, log_dt, b_re, b_im, c_re, c_im, d_skip, w_glu, w_branch, w_out, w_up, w_down, loss_target, m_norm_mix_pre, m_norm_mix_post, m_norm_mlp_pre, m_norm_mlp_post, m_w_in, m_sinks, m_lam_re, m_lam_im, m_log_dt, m_b_re, m_b_im, m_c_re, m_c_im, m_d_skip, m_w_glu, m_w_branch, m_w_out, m_w_up, m_w_down, v_norm_mix_pre, v_norm_mix_post, v_norm_mlp_pre, v_norm_mlp_post, v_w_in, v_sinks, v_lam_re, v_lam_im, v_log_dt, v_b_re, v_b_im, v_c_re, v_c_im, v_d_skip, v_w_glu, v_w_branch, v_w_out, v_w_up, v_w_down):
    w = dict(norm_mix_pre=norm_mix_pre, norm_mix_post=norm_mix_post, norm_mlp_pre=norm_mlp_pre, norm_mlp_post=norm_mlp_post,
             w_in=w_in, sinks=sinks, lam_re=lam_re, lam_im=lam_im, log_dt=log_dt, b_re=b_re, b_im=b_im, c_re=c_re,
             c_im=c_im, d_skip=d_skip, w_glu=w_glu, w_branch=w_branch, w_out=w_out, w_up=w_up, w_down=w_down)
    m = dict(norm_mix_pre=m_norm_mix_pre, norm_mix_post=m_norm_mix_post, norm_mlp_pre=m_norm_mlp_pre,
             norm_mlp_post=m_norm_mlp_post, w_in=m_w_in, sinks=m_sinks, lam_re=m_lam_re, lam_im=m_lam_im,
             log_dt=m_log_dt, b_re=m_b_re, b_im=m_b_im, c_re=m_c_re, c_im=m_c_im, d_skip=m_d_skip, w_glu=m_w_glu,
             w_branch=m_w_branch, w_out=m_w_out, w_up=m_w_up, w_down=m_w_down)
    v = dict(norm_mix_pre=v_norm_mix_pre, norm_mix_post=v_norm_mix_post, norm_mlp_pre=v_norm_mlp_pre,
             norm_mlp_post=v_norm_mlp_post, w_in=v_w_in, sinks=v_sinks, lam_re=v_lam_re, lam_im=v_lam_im,
             log_dt=v_log_dt, b_re=v_b_re, b_im=v_b_im, c_re=v_c_re, c_im=v_c_im, d_skip=v_d_skip, w_glu=v_w_glu,
             w_branch=v_w_branch, w_out=v_w_out, w_up=v_w_up, w_down=v_w_down)
    xi, yi, ci = lax.axis_index("x"), lax.axis_index("y"), lax.axis_index("c")

    k_arr = jnp.stack([2 * xi + yi]).astype(jnp.int32)
    *w_in_sems, w_in_thru = _gather_start(_cast_into_slot(w["w_in"][0], k_arr))
    slot = {k: _cast_into_slot(w[k][0], k_arr) for k in BIG if k != "w_in"}

    def whole(k, g4):
        if k in COL_SHARDED:
            return jnp.concatenate([g4[j] for j in range(4)], axis=1)
        return g4.reshape(4 * g4.shape[1], g4.shape[2])

    w_in = []

    def w_a_of(*after):
        arrived = _gather_wait(w_in_sems, w_in_thru, after + tuple(slot.values()))
        w_in.append(_comm_only("gather_w_in_pass", _PassOnComm([arrived]))[0])
        return _assemble_w_a(w_in[0])

    hosted = (("w_glu", "w_branch", "w_out"), ("w_up",), ("w_down",))
    comms = [None] + [_GatherComm([slot[k] for k in names]) for names in hosted]

    def late(*got):
        f = {k: whole(k, g4) for names, res in zip(hosted, got) for k, g4 in zip(names, res)}
        return f["w_glu"], f["w_branch"][:Q_W], f["w_branch"][Q_W:], f["w_out"], f["w_up"], f["w_down"]

    s5w = (lam_re[0], lam_im[0], log_dt[0], b_re[0], b_im[0], c_re[0], c_im[0], d_skip[0])
    reducer = _GradReducer(
        jnp.stack([ci, 2 * (1 - xi) + yi, 2 * xi + (1 - yi), 2 * (1 - xi) + (1 - yi)]).astype(jnp.int32),
        jnp.stack([2 * xi + yi, ci]).astype(jnp.int32))
    loss_part, dx, small, _ = _local_step(
        x[0], loss_target[0], (norm_mix_pre, norm_mix_post, norm_mlp_pre, norm_mlp_post),
        w_a_of, sinks, s5w, comms, lambda _: _assemble_w_g(w_in[0]), late, reducer)
    grads = reducer.finish(BIG)

    red = _all_reduce_small(_flat_small(small, loss_part)).reshape(-1)
    off = 0
    for k in SMALL:
        n = math.prod(w[k].shape)
        grads[k] = red[off:off + n].reshape(w[k].shape[1:])
        off += n
    loss = red[off]

    delta, new_m, new_v = {}, {}, {}
    for k in BIG:
        delta[k], new_m[k], new_v[k] = _adamw(w[k][0], grads[k], m[k][0], v[k][0], "adamw_" + k)
    zero = jnp.zeros((), F32)
    fw, fm, fv = (_flat_small({k: t[k] for k in SMALL}, zero) for t in (w, m, v))
    fg = _flat_small(grads, zero)
    sd, sm, sv = _adamw(fw, fg, fm, fv, "adamw_small", tr=fw.shape[0])
    off = 0
    for k in SMALL:
        n = math.prod(w[k].shape)
        delta[k], new_m[k], new_v[k] = (t.reshape(-1)[off:off + n].reshape(w[k].shape[1:]) for t in (sd, sm, sv))
        off += n

    lead = lambda t: t[None]
    return (loss, lead(dx), *[lead(grads[k]) for k in WEIGHTS], *[lead(delta[k]) for k in WEIGHTS],
            *[lead(new_m[k]) for k in WEIGHTS], *[lead(new_v[k]) for k in WEIGHTS])
```

```python
import functools
import math

import jax
import jax.numpy as jnp
from jax import lax
from jax.experimental import pallas as pl
from jax.experimental.pallas import tpu as pltpu

F32 = jnp.float32
BF16 = jnp.bfloat16
SDS = jax.ShapeDtypeStruct

D_MODEL = 2048
HEAD_DIM = 64
N_Q_HEADS = 16
ATT_BLOCK = 128
ROT_DIM = 16
ROPE_THETA = 500000.0
Q_W = 1024
KV_W = 128
SSM_W = 1024
SSM_G = 64
SSM_GC = 16
SSM_P = 64
N_STATE = SSM_G * SSM_P
LANES = 128
SUBLANES = 8
N_LG = N_STATE // LANES
N_JB = 8
LG_PER_JB = N_LG // N_JB
D_FF = 8192
ZA_W = Q_W + 2 * KV_W + SSM_W
EPS = 1e-6
S5_CHUNK = 2048
S5_SEG = S5_CHUNK // SUBLANES
VMEM_LIMIT = 56 * 1024 * 1024
NEG = -1e30

ADAM_LR = 0.001
ADAM_B1 = 0.9
ADAM_B2 = 0.999
ADAM_EPS = 1e-08
ADAM_WD = 0.01
ADAM_STEP = 10

MESH = pl.DeviceIdType.MESH


def _cp(sem):
    return pltpu.CompilerParams(dimension_semantics=sem, vmem_limit_bytes=VMEM_LIMIT)


ANY = pl.BlockSpec(memory_space=pl.ANY)


def _place():
    x, y, c = lax.axis_index("x"), lax.axis_index("y"), lax.axis_index("c")
    others = [(1 - x, y), (x, 1 - y), (1 - x, 1 - y)]
    return x, y, c, others


def _remote(src, dst, ssem, rsem, to):
    return pltpu.make_async_remote_copy(src_ref=src, dst_ref=dst, send_sem=ssem, recv_sem=rsem,
                                        device_id=to, device_id_type=MESH)


class _GatherComm:
    aliased = True

    def __init__(self, slotted):
        self.arrs = list(slotted)
        self.n = len(self.arrs)
        dma = pltpu.SemaphoreType.DMA
        self.scratch = [dma((3 * self.n,)) for _ in range(4)]
        self.out_shape = [SDS(s.shape, s.dtype) for s in self.arrs]

    @staticmethod
    def _half(ref, hc):
        hr = ref.shape[1] // 2
        return pl.ds(pl.multiple_of(hc * hr, 16), hr)

    def _sends(self, ins, outs, sems):
        ssem, rsem, _, _ = sems
        x, y, c, others = _place()
        me = 2 * x + y
        return [_remote(ins[w].at[me, self._half(ins[w], c), :], outs[w].at[me, self._half(ins[w], c), :],
                        ssem.at[3 * w + r], rsem.at[3 * w + r], (ox, oy, c))
                for w in range(self.n) for r, (ox, oy) in enumerate(others)]

    def start(self, ins, outs, sems):
        for cp in self._sends(ins, outs, sems):
            cp.start()

    def finish(self, ins, outs, sems):
        ssem, rsem, fs_sem, fr_sem = sems
        x, y, c, others = _place()
        sib = (x, y, 1 - c)
        passes = []
        for w in range(self.n):
            for r, (ox, oy) in enumerate(others):
                got = outs[w].at[2 * ox + oy, self._half(ins[w], c), :]
                _remote(got, got, ssem.at[3 * w + r], rsem.at[3 * w + r], (ox, oy, c)).wait_recv()
                cp = _remote(got, got, fs_sem.at[3 * w + r], fr_sem.at[3 * w + r], sib)
                cp.start()
                passes.append(cp)
        for w in range(self.n):
            for r, (ox, oy) in enumerate(others):
                got = outs[w].at[2 * ox + oy, self._half(ins[w], 1 - c), :]
                _remote(got, got, fs_sem.at[3 * w + r], fr_sem.at[3 * w + r], sib).wait_recv()
        for cp in self._sends(ins, outs, sems) + passes:
            cp.wait_send()


class _PairExchangeComm:
    aliased = False

    def __init__(self, grads):
        self.arrs = list(grads)
        self.n = len(self.arrs)
        dma = pltpu.SemaphoreType.DMA
        self.scratch = [dma((self.n,)), dma((self.n,))]
        self.out_shape = [SDS((4, g.shape[1] // 2, g.shape[2]), g.dtype) for g in self.arrs]

    def _copies(self, ins, outs, sems):
        ssem, rsem = sems
        x, y, c, _ = _place()
        cps = []
        for w in range(self.n):
            hr = ins[w].shape[1] // 2
            src = ins[w].at[:, pl.ds(pl.multiple_of((1 - c) * hr, 8), hr), :]
            cps.append(_remote(src, outs[w], ssem.at[w], rsem.at[w], (x, y, 1 - c)))
        return cps

    def start(self, ins, outs, sems):
        for cp in self._copies(ins, outs, sems):
            cp.start()

    def finish(self, ins, outs, sems):
        for cp in self._copies(ins, outs, sems):
            cp.wait()


class _ChipExchangeComm:
    aliased = False

    def __init__(self, psums):
        self.arrs = list(psums)
        self.n = len(self.arrs)
        dma = pltpu.SemaphoreType.DMA
        self.scratch = [dma((3 * self.n,)), dma((3 * self.n,))]
        self.out_shape = [SDS((3,) + p.shape[1:], p.dtype) for p in self.arrs]

    def _copies(self, ins, outs, sems):
        ssem, rsem = sems
        x, y, c, others = _place()
        return [_remote(ins[w].at[2 * ox + oy], outs[w].at[r], ssem.at[3 * w + r], rsem.at[3 * w + r], (ox, oy, c))
                for w in range(self.n) for r, (ox, oy) in enumerate(others)]

    def start(self, ins, outs, sems):
        for cp in self._copies(ins, outs, sems):
            cp.start()

    def finish(self, ins, outs, sems):
        for cp in self._copies(ins, outs, sems):
            cp.wait()


def _comm_only(name, comm):
    n = comm.n

    def body(*refs):
        ins, outs, sems = refs[:n], refs[n:2 * n], refs[2 * n:]
        comm.start(ins, outs, sems)
        comm.finish(ins, outs, sems)

    return pl.pallas_call(
        body, name=name, in_specs=[ANY] * n, out_specs=[ANY] * n, out_shape=comm.out_shape,
        input_output_aliases={w: w for w in range(n)} if comm.aliased else {},
        scratch_shapes=comm.scratch)(*comm.arrs)


HBM = pl.BlockSpec(memory_space=pltpu.HBM)
SEM = pl.BlockSpec(memory_space=pltpu.SEMAPHORE)
_EFFECT = pltpu.SideEffectType.DATAFLOW_SIDE_EFFECTING


def _gather_copies(ref, sems):
    x, y, c, others = _place()
    me = 2 * x + y
    half = _GatherComm._half(ref, c)
    out = [_remote(ref.at[me, half, :], ref.at[me, half, :], sems[r], sems[3 + r], (ox, oy, c))
           for r, (ox, oy) in enumerate(others)]
    arriving = [_remote(ref.at[me, half, :], ref.at[2 * ox + oy, half, :], sems[r], sems[3 + r], (ox, oy, c))
                for r, (ox, oy) in enumerate(others)]
    return out, arriving


def _gather_start(slotted):
    def body(w_ref, *rest):
        for cp in _gather_copies(rest[6], rest[:6])[0]:
            cp.start()

    dma = pltpu.SemaphoreType.DMA(())
    return pl.pallas_call(
        body, name="gather_w_in_start",
        out_shape=(dma,) * 6 + (pltpu.HBM(slotted.shape, slotted.dtype),),
        in_specs=(HBM,), out_specs=(SEM,) * 6 + (HBM,), input_output_aliases={0: 6},
        compiler_params=pltpu.CompilerParams(has_side_effects=_EFFECT),
    )(pltpu.with_memory_space_constraint(slotted, pltpu.HBM))


def _gather_wait(sems, thru, after):
    def body(w_ref, *rest):
        out, arriving = _gather_copies(w_ref, rest[:6])
        for cp in out:
            cp.wait_send()
        for cp in arriving:
            cp.wait_recv()

    n = len(after)
    return pl.pallas_call(
        body, name="gather_w_in_wait", out_shape=(pltpu.HBM(thru.shape, thru.dtype),),
        in_specs=(HBM,) + (SEM,) * 6 + (ANY,) * n, out_specs=(HBM,), input_output_aliases={0: 0},
        compiler_params=pltpu.CompilerParams(has_side_effects=_EFFECT),
    )(thru, *sems, *after)[0]


class _PassOnComm:
    aliased = True

    def __init__(self, gathered):
        self.arrs = list(gathered)
        self.n = len(self.arrs)
        dma = pltpu.SemaphoreType.DMA
        self.scratch = [dma((3 * self.n,)), dma((3 * self.n,))]
        self.out_shape = [SDS(s.shape, s.dtype) for s in self.arrs]

    def _copies(self, ins, outs, sems, hc):
        ssem, rsem = sems
        x, y, c, others = _place()
        return [_remote(ins[w].at[2 * ox + oy, _GatherComm._half(ins[w], c if hc == 0 else 1 - c), :],
                        outs[w].at[2 * ox + oy, _GatherComm._half(ins[w], c if hc == 0 else 1 - c), :],
                        ssem.at[3 * w + r], rsem.at[3 * w + r], (x, y, 1 - c))
                for w in range(self.n) for r, (ox, oy) in enumerate(others)]

    def start(self, ins, outs, sems):
        for cp in self._copies(ins, outs, sems, 0):
            cp.start()

    def finish(self, ins, outs, sems):
        for cp in self._copies(ins, outs, sems, 1):
            cp.wait_recv()
        for cp in self._copies(ins, outs, sems, 0):
            cp.wait_send()


def _call(name, body, grid, in_specs, out_specs, out_shape, scratch, dims, args, comm=None):
    if comm is None:
        return pl.pallas_call(body, name=name, grid=grid, in_specs=in_specs, out_specs=out_specs, out_shape=out_shape,
                              scratch_shapes=scratch, compiler_params=_cp(dims))(*args)
    ni, no, ns, n = len(in_specs), len(out_shape), len(scratch), comm.n

    def hosted(*refs):
        ins, cin = refs[:ni], refs[ni:ni + n]
        outs, cout = refs[ni + n:ni + n + no], refs[ni + n + no:ni + 2 * n + no]
        scr, sems = refs[ni + 2 * n + no:ni + 2 * n + no + ns], refs[ni + 2 * n + no + ns:]
        ids = [pl.program_id(d) for d in range(len(grid))]
        first = functools.reduce(jnp.logical_and, [i == 0 for i in ids])
        last = functools.reduce(jnp.logical_and, [i == g - 1 for i, g in zip(ids, grid)])

        @pl.when(first)
        def _():
            comm.start(cin, cout, sems)

        body(*ins, *outs, *scr)

        @pl.when(last)
        def _():
            comm.finish(cin, cout, sems)

    return pl.pallas_call(
        hosted, name=name, grid=grid, in_specs=list(in_specs) + [ANY] * n, out_specs=list(out_specs) + [ANY] * n,
        out_shape=list(out_shape) + comm.out_shape,
        input_output_aliases={ni + w: no + w for w in range(n)} if comm.aliased else {},
        scratch_shapes=list(scratch) + comm.scratch, compiler_params=_cp(("arbitrary",) * len(grid)))(*args, *comm.arrs)


def _mm(a, b, *, mode, out_dtype, tm, tn, tk, name, a_fn=None, epi=None, extras=(), comm=None, shard_cols=None):
    if mode == "nn":
        (M, K), (K2, N) = a.shape, b.shape
    elif mode == "nt":
        (M, K), (N, K2) = a.shape, b.shape
    else:
        (K, M), (K2, N) = a.shape, b.shape
    assert K == K2, (a.shape, b.shape, mode)
    tm, tn, tk = min(tm, M), min(tn, N), min(tk, K)
    assert M % tm == 0 and N % tn == 0 and K % tk == 0, (M, N, K, tm, tn, tk)
    nk = K // tk
    if mode == "tn":
        a_spec = pl.BlockSpec((tk, tm), lambda i, j, k: (k, i))
        ca = 0
    else:
        a_spec = pl.BlockSpec((tm, tk), lambda i, j, k: (i, k))
        ca = 1
    if mode == "nt":
        b_spec = pl.BlockSpec((tn, tk), lambda i, j, k: (j, k))
        cb = 1
    else:
        b_spec = pl.BlockSpec((tk, tn), lambda i, j, k: (k, j))
        cb = 0
    dims = (((ca,), (cb,)), ((), ()))
    ne = len(extras)

    def body(a_ref, b_ref, *rest):
        ex = rest[:ne]
        o_ref = rest[ne]
        av = a_ref[...]
        if a_fn is not None:
            av = a_fn(av.astype(F32))
        p = lax.dot_general(av.astype(BF16), b_ref[...].astype(BF16), dims, preferred_element_type=F32)

        def fin(v):
            if epi is not None:
                v = epi(v, *[e[...] for e in ex])
            o_ref[...] = v.astype(out_dtype).reshape(o_ref.shape)

        if nk == 1:
            fin(p)
        else:
            acc = rest[ne + 1]
            k = pl.program_id(2)

            @pl.when(k == 0)
            def _():
                acc[...] = p

            @pl.when(k > 0)
            def _():
                acc[...] += p

            @pl.when(k == nk - 1)
            def _():
                fin(acc[...])

    if shard_cols is None:
        o_spec, o_shape = pl.BlockSpec((tm, tn), lambda i, j, k: (i, j)), SDS((M, N), out_dtype)
    else:
        per = shard_cols // tn
        assert shard_cols % tn == 0 and N % shard_cols == 0
        o_spec = pl.BlockSpec((1, tm, tn), lambda i, j, k: (lax.div(j, per), i, lax.rem(j, per)))
        o_shape = SDS((N // shard_cols, M, shard_cols), out_dtype)
    res = _call(name, body, (M // tm, N // tn, nk),
                [a_spec, b_spec] + [pl.BlockSpec((tm, tn), lambda i, j, k: (i, j)) for _ in extras],
                [o_spec], [o_shape],
                [pltpu.VMEM((tm, tn), F32)] if nk > 1 else [], ("parallel", "parallel", "arbitrary"),
                (a, b, *extras), comm)
    return res[0] if comm is None else (res[0], res[1:])


def _rowwise(fn, rows, bcasts, outs, accs, *, tr, name):
    T = rows[0][0].shape[0]
    tr = min(tr, T)
    assert T % tr == 0
    nr, nb, no, na = len(rows), len(bcasts), len(outs), len(accs)
    in_specs = [pl.BlockSpec((tr, w), functools.partial(lambda i, c: (i, c), c=cb)) for (_, w, cb) in rows]
    in_specs += [pl.BlockSpec(b.shape, lambda i: (0, 0)) for b in bcasts]
    out_shape = [SDS((T, w), dt) for (w, dt) in outs] + [SDS(s, F32) for s in accs]
    out_specs = [pl.BlockSpec((tr, w), lambda i: (i, 0)) for (w, _) in outs]
    out_specs += [pl.BlockSpec(s, lambda i: (0, 0)) for s in accs]

    def body(*refs):
        ins = [r[...].astype(F32) for r in refs[:nr + nb]]
        o_refs = refs[nr + nb:nr + nb + no]
        a_refs = refs[nr + nb + no:]
        ro, ao = fn(*ins)
        for r, v in zip(o_refs, ro):
            r[...] = v.astype(r.dtype)
        if na:
            @pl.when(pl.program_id(0) == 0)
            def _():
                for r in a_refs:
                    r[...] = jnp.zeros(r.shape, F32)

            for r, v in zip(a_refs, ao):
                r[...] += v

    res = pl.pallas_call(
        body,
        name=name,
        grid=(T // tr,),
        in_specs=in_specs,
        out_specs=out_specs,
        out_shape=out_shape,
        compiler_params=_cp(("arbitrary",) if na else ("parallel",)),
    )(*[r[0] for r in rows], *bcasts)
    return res


def _rms(v):
    r = lax.rsqrt(jnp.mean(v * v, axis=-1, keepdims=True) + EPS)
    return v * r, r


def _rms_bwd(dy, xn, r, g):
    dxn = dy * g
    dv = r * (dxn - xn * jnp.mean(dxn * xn, axis=-1, keepdims=True))
    return dv, jnp.sum(dy * xn, axis=0, keepdims=True)


def _sig(v):
    return 1.0 / (1.0 + jnp.exp(-v))


_GELU_C = math.sqrt(2.0 / math.pi)


def _gelu(v):
    return 0.5 * v * (1.0 + jnp.tanh(_GELU_C * (v + 0.044715 * v * v * v)))


def _gelu_grad(v):
    t = jnp.tanh(_GELU_C * (v + 0.044715 * v * v * v))
    return 0.5 * (1.0 + t) + 0.5 * v * (1.0 - t * t) * _GELU_C * (1.0 + 3.0 * 0.044715 * v * v)


def _rope(v, c, s, sign):
    w = v.shape[1]
    m = lax.broadcasted_iota(jnp.int32, v.shape, 1) % HEAD_DIM
    p = jnp.where(m < ROT_DIM // 2, -pltpu.roll(v, w - ROT_DIM // 2, 1), pltpu.roll(v, ROT_DIM // 2, 1))
    return v * c + sign * (p * s)


def _rope_tables(T):
    half = ROT_DIM // 2
    inv = ROPE_THETA ** (-jnp.arange(half, dtype=F32) * 2.0 / ROT_DIM)
    ang = jnp.arange(T).astype(F32)[:, None] * inv[None, :]
    cos, sin = jnp.cos(ang), jnp.sin(ang)
    one = jnp.ones((T, HEAD_DIM - ROT_DIM), F32)
    c64 = jnp.concatenate([cos, cos, one], axis=1)
    s64 = jnp.concatenate([sin, sin, 0.0 * one], axis=1)
    return jnp.tile(c64, (1, 2)), jnp.tile(s64, (1, 2))


def _dup_half(m, lo):
    lane = lax.broadcasted_iota(jnp.int32, m.shape, 1)
    sw = pltpu.roll(m, HEAD_DIM, 1)
    return jnp.where(lane < HEAD_DIM, m, sw) if lo else jnp.where(lane >= HEAD_DIM, m, sw)


def _attn_mask(i):
    qi = lax.broadcasted_iota(jnp.int32, (ATT_BLOCK, 2 * ATT_BLOCK), 0)
    kj = lax.broadcasted_iota(jnp.int32, (ATT_BLOCK, 2 * ATT_BLOCK), 1)
    rel = qi + ATT_BLOCK - kj
    return (rel >= 0) & (rel < ATT_BLOCK) & ((kj >= ATT_BLOCK) | (i > 0))


_NT = (((1,), (1,)), ((), ()))
_TN = (((0,), (0,)), ((), ()))


def _stack_heads(m):
    lane = lax.broadcasted_iota(jnp.int32, m.shape, 1)
    zero = jnp.zeros_like(m)
    return jnp.concatenate([jnp.where(lane < HEAD_DIM, m, zero), jnp.where(lane >= HEAD_DIM, m, zero)], axis=0)


def _pair_probs(q2, k2, ok2, sink_lo, sink_hi):
    qs = _stack_heads(q2)
    s = lax.dot_general(qs, k2, _NT, preferred_element_type=F32)
    s = jnp.where(ok2, s, NEG)
    row = lax.broadcasted_iota(jnp.int32, (2 * ATT_BLOCK, 1), 0)
    sink = jnp.where(row < ATT_BLOCK, sink_lo, sink_hi)
    m = jnp.maximum(jnp.max(s, axis=1, keepdims=True), sink)
    e = jnp.exp(s - m)
    es = jnp.exp(sink - m)
    inv = 1.0 / (jnp.sum(e, axis=1, keepdims=True) + es)
    return e * inv, es * inv, qs


def _attn_fwd(za, cos, sin, sinks, comm=None):
    T = za.shape[0]
    nb = T // ATT_BLOCK
    kvb = Q_W // (2 * KV_W)

    def body(sink_ref, q_ref, kvp_ref, kvc_ref, cc_ref, sc_ref, cp_ref, sp_ref, o_ref):
        i = pl.program_id(0)
        cc, sc, cp, sp = cc_ref[...], sc_ref[...], cp_ref[...], sp_ref[...]
        q = (_rope(q_ref[...], jnp.tile(cc, (1, 8)), jnp.tile(sc, (1, 8)), 1.0) * 0.125).astype(BF16)
        kvp, kvc = kvp_ref[...], kvc_ref[...]
        k = jnp.concatenate([_rope(kvp[:, :KV_W], cp, sp, 1.0), _rope(kvc[:, :KV_W], cc, sc, 1.0)], axis=0).astype(BF16)
        v = jnp.concatenate([kvp[:, KV_W:], kvc[:, KV_W:]], axis=0).astype(BF16)
        ok = _attn_mask(i)
        ok2 = jnp.concatenate([ok, ok], axis=0)
        lane = lax.broadcasted_iota(jnp.int32, (ATT_BLOCK, LANES), 1)
        for kvh in range(2):
            k2 = _dup_half(k, kvh == 0)
            v2 = _dup_half(v, kvh == 0)
            for pair in range(4):
                c0 = (kvh * 4 + pair) * LANES
                q2 = q[:, c0:c0 + LANES]
                p, _, _ = _pair_probs(q2, k2, ok2, sink_ref[0, 2 * (kvh * 4 + pair)], sink_ref[0, 2 * (kvh * 4 + pair) + 1])
                o = jnp.dot(p.astype(BF16), v2, preferred_element_type=F32)
                o_ref[:, c0:c0 + LANES] = jnp.where(lane < HEAD_DIM, o[:ATT_BLOCK], o[ATT_BLOCK:]).astype(BF16)

    blk = lambda w, f: pl.BlockSpec((ATT_BLOCK, w), f)
    res = _call(
        "attn_fwd", body, (nb,),
        [
            pl.BlockSpec(memory_space=pltpu.SMEM),
            blk(Q_W, lambda i: (i, 0)),
            blk(2 * KV_W, lambda i: (jnp.maximum(i - 1, 0), kvb)),
            blk(2 * KV_W, lambda i: (i, kvb)),
            blk(LANES, lambda i: (i, 0)),
            blk(LANES, lambda i: (i, 0)),
            blk(LANES, lambda i: (jnp.maximum(i - 1, 0), 0)),
            blk(LANES, lambda i: (jnp.maximum(i - 1, 0), 0)),
        ],
        [blk(Q_W, lambda i: (i, 0))], [SDS((T, Q_W), BF16)], [], ("parallel",),
        (sinks, za, za, za, cos, sin, cos, sin), comm)
    return res[0] if comm is None else (res[0], res[1:])


def _attn_bwd(za, cos, sin, sinks, o, do, comm=None):
    T = za.shape[0]
    nb = T // ATT_BLOCK
    kvb = Q_W // (2 * KV_W)

    def body(sink_ref, q_ref, kvp_ref, kvc_ref, cc_ref, sc_ref, cp_ref, sp_ref, o_ref, do_ref,
             dq_ref, dkv_ref, dsk_ref, carry, dqs):
        i = pl.program_id(0)

        @pl.when(i == 0)
        def _():
            carry[...] = jnp.zeros(carry.shape, F32)
            dsk_ref[...] = jnp.zeros(dsk_ref.shape, F32)

        @pl.when(i < nb)
        def _():
            cc, sc, cp, sp = cc_ref[...], sc_ref[...], cp_ref[...], sp_ref[...]
            ccq, scq = jnp.tile(cc, (1, 8)), jnp.tile(sc, (1, 8))
            q = (_rope(q_ref[...], ccq, scq, 1.0) * 0.125).astype(BF16)
            kvp, kvc = kvp_ref[...], kvc_ref[...]
            k = jnp.concatenate([_rope(kvp[:, :KV_W], cp, sp, 1.0), _rope(kvc[:, :KV_W], cc, sc, 1.0)], axis=0).astype(BF16)
            v = jnp.concatenate([kvp[:, KV_W:], kvc[:, KV_W:]], axis=0).astype(BF16)
            ok = _attn_mask(i)
            ok2 = jnp.concatenate([ok, ok], axis=0)
            lane = lax.broadcasted_iota(jnp.int32, (ATT_BLOCK, LANES), 1)
            lane_s = lax.broadcasted_iota(jnp.int32, (1, LANES), 1)
            dsk = jnp.zeros((1, LANES), F32)
            dkt_h, dvt_h = [], []
            for kvh in range(2):
                k2 = _dup_half(k, kvh == 0)
                v2 = _dup_half(v, kvh == 0)
                dkt = jnp.zeros((LANES, 2 * ATT_BLOCK), F32)
                dvt = jnp.zeros((LANES, 2 * ATT_BLOCK), F32)
                for pair in range(4):
                    h = 2 * (kvh * 4 + pair)
                    c0 = (kvh * 4 + pair) * LANES
                    do2 = do_ref[:, c0:c0 + LANES]
                    prod = do2.astype(F32) * o_ref[:, c0:c0 + LANES].astype(F32)
                    d_lo = jnp.sum(jnp.where(lane < HEAD_DIM, prod, 0.0), axis=1, keepdims=True)
                    d_hi = jnp.sum(jnp.where(lane >= HEAD_DIM, prod, 0.0), axis=1, keepdims=True)
                    delta = jnp.concatenate([d_lo, d_hi], axis=0)
                    p, p_sink, qs = _pair_probs(q[:, c0:c0 + LANES], k2, ok2, sink_ref[0, h], sink_ref[0, h + 1])
                    dos = _stack_heads(do2)
                    t = p_sink * delta
                    dsk = dsk - jnp.where(lane_s == h, jnp.sum(t[:ATT_BLOCK]), 0.0) \
                              - jnp.where(lane_s == h + 1, jnp.sum(t[ATT_BLOCK:]), 0.0)
                    dp = lax.dot_general(dos, v2, _NT, preferred_element_type=F32)
                    ds = (p * (dp - delta)).astype(BF16)
                    dqp = jnp.dot(ds, k2, preferred_element_type=F32)
                    dqs[:, c0:c0 + LANES] = jnp.where(lane < HEAD_DIM, dqp[:ATT_BLOCK], dqp[ATT_BLOCK:]) * 0.125
                    dkt = dkt + lax.dot_general(qs, ds, _TN, preferred_element_type=F32)
                    dvt = dvt + lax.dot_general(dos, p.astype(BF16), _TN, preferred_element_type=F32)
                dkt_h.append(dkt[:HEAD_DIM] + dkt[HEAD_DIM:])
                dvt_h.append(dvt[:HEAD_DIM] + dvt[HEAD_DIM:])
            dk = jnp.concatenate(dkt_h, axis=0).T
            dv = jnp.concatenate(dvt_h, axis=0).T
            dq_ref[...] = _rope(dqs[...], ccq, scq, -1.0).astype(dq_ref.dtype)
            dkp = _rope(dk[:ATT_BLOCK], cp, sp, -1.0)
            dkc = _rope(dk[ATT_BLOCK:], cc, sc, -1.0)
            dkv_ref[...] = (carry[...] + jnp.concatenate([dkp, dv[:ATT_BLOCK]], axis=1)).astype(dkv_ref.dtype)
            carry[...] = jnp.concatenate([dkc, dv[ATT_BLOCK:]], axis=1)
            dsk_ref[...] += dsk

        @pl.when(i == nb)
        def _():
            dkv_ref[...] = carry[...].astype(dkv_ref.dtype)

    blk = lambda w, f: pl.BlockSpec((ATT_BLOCK, w), f)
    cur = lambda i: jnp.minimum(i, nb - 1)
    prv = lambda i: jnp.maximum(jnp.minimum(i, nb - 1) - 1, 0)
    res = _call(
        "attn_bwd", body, (nb + 1,),
        [
            pl.BlockSpec(memory_space=pltpu.SMEM),
            blk(Q_W, lambda i: (cur(i), 0)),
            blk(2 * KV_W, lambda i: (prv(i), kvb)),
            blk(2 * KV_W, lambda i: (cur(i), kvb)),
            blk(LANES, lambda i: (cur(i), 0)),
            blk(LANES, lambda i: (cur(i), 0)),
            blk(LANES, lambda i: (prv(i), 0)),
            blk(LANES, lambda i: (prv(i), 0)),
            blk(Q_W, lambda i: (cur(i), 0)),
            blk(Q_W, lambda i: (cur(i), 0)),
        ],
        [
            blk(Q_W, lambda i: (cur(i), 0)),
            blk(2 * KV_W, lambda i: (jnp.maximum(i - 1, 0), 0)),
            pl.BlockSpec((1, LANES), lambda i: (0, 0)),
        ],
        [SDS((T, Q_W), BF16), SDS((T, 2 * KV_W), BF16), SDS((1, LANES), F32)],
        [pltpu.VMEM((ATT_BLOCK, 2 * KV_W), F32), pltpu.VMEM((ATT_BLOCK, Q_W), F32)],
        ("arbitrary",), (sinks, za, za, za, cos, sin, cos, sin, o, do), comm)
    return res if comm is None else (res[:3], res[3:])


def _s5_discretize(lam_re, lam_im, log_dt, b_re, b_im):
    dt = jnp.exp(log_dt)[:, None]
    mag = jnp.exp(lam_re * dt)
    a_re, a_im = mag * jnp.cos(lam_im * dt), mag * jnp.sin(lam_im * dt)
    den = lam_re * lam_re + lam_im * lam_im
    nr, ni = a_re - 1.0, a_im
    coef_re = (nr * lam_re + ni * lam_im) / den
    coef_im = (ni * lam_re - nr * lam_im) / den
    bb_re = coef_re[..., None] * b_re - coef_im[..., None] * b_im
    bb_im = coef_re[..., None] * b_im + coef_im[..., None] * b_re
    return a_re, a_im, bb_re, bb_im


def _blockdiag_in(bb):
    x = bb.reshape(N_JB, 8, SSM_P, SSM_GC).transpose(0, 1, 3, 2)
    return (x[:, :, :, None, :] * jnp.eye(8, dtype=bb.dtype)[None, :, None, :, None]).reshape(N_JB, 128, 512)


def _blockdiag_in_extract(m):
    x = m.reshape(N_JB, 8, SSM_GC, 8, SSM_P)
    x = jnp.einsum('jgchp,gh->jgcp', x, jnp.eye(8, dtype=m.dtype))
    return x.transpose(0, 1, 3, 2).reshape(SSM_G, SSM_P, SSM_GC)


def _blockdiag_out(c):
    x = c.reshape(N_JB, 8, SSM_GC, SSM_P).transpose(0, 1, 3, 2)
    return (x[:, :, :, None, :] * jnp.eye(8, dtype=c.dtype)[None, :, None, :, None]).reshape(N_JB, 512, 128)


def _blockdiag_out_extract(m):
    x = m.reshape(N_JB, 8, SSM_P, 8, SSM_GC)
    x = jnp.einsum('jgphc,gh->jgpc', x, jnp.eye(8, dtype=m.dtype))
    return x.transpose(0, 1, 3, 2).reshape(SSM_G, SSM_GC, SSM_P)


def _s5_tables(a_re, a_im):
    ar, ai = a_re.reshape(N_LG, 1, LANES), a_im.reshape(N_LG, 1, LANES)
    pr, pi, n = ar, ai, 1
    while n < S5_SEG:
        pr, pi, n = pr * pr - pi * pi, 2.0 * pr * pi, 2 * n
    assert n == S5_SEG
    bc = lambda v: jnp.broadcast_to(v, (N_LG, SUBLANES, LANES))
    return bc(ar), bc(ai), pr, pi


def _s5_to_time_major(src_ref, dst_ref):
    for t in range(S5_SEG):
        dst_ref[t * SUBLANES:(t + 1) * SUBLANES, :] = src_ref[pl.ds(t, SUBLANES, stride=S5_SEG), :]


def _s5_from_time_major(val, dst_ref):
    for t in range(S5_SEG):
        dst_ref[pl.ds(t, SUBLANES, stride=S5_SEG), :] = val[t * SUBLANES:(t + 1) * SUBLANES, :]


def _tm_rows(t, row0=0):
    return pl.ds(pl.multiple_of(t * SUBLANES + row0, SUBLANES), SUBLANES)


def _s5_scan(src_re, src_im, ar, ai, reverse, start=None, dst=None, dst_row0=0):
    def step(n, carry):
        t = (S5_SEG - 1 - n) if reverse else n
        out = []
        for ll in range(LG_PER_JB):
            xr, xi = carry[2 * ll], carry[2 * ll + 1]
            idx = (ll, _tm_rows(t), slice(None))
            nr = ar[ll] * xr - ai[ll] * xi + src_re[idx]
            ni = ar[ll] * xi + ai[ll] * xr + src_im[idx]
            if dst is not None:
                odx = (ll, _tm_rows(t, dst_row0), slice(None))
                dst[0][odx] = nr
                dst[1][odx] = ni
            out += [nr, ni]
        return tuple(out)
    if start is None:
        init = (jnp.zeros((SUBLANES, LANES), F32),) * (2 * LG_PER_JB)
    else:
        init = tuple(s[ll] for ll in range(LG_PER_JB) for s in start)
    return lax.fori_loop(0, S5_SEG, step, init)


def _s5_fixup(ends, in_re, in_im, mr, mi, s_re, s_im, reverse):
    cr, ci = in_re, in_im
    order = range(SUBLANES - 1, -1, -1) if reverse else range(SUBLANES)
    for s in order:
        s_re[:, s:s + 1, :] = cr
        s_im[:, s:s + 1, :] = ci
        er = jnp.stack([ends[2 * ll][s:s + 1, :] for ll in range(LG_PER_JB)])
        ei = jnp.stack([ends[2 * ll + 1][s:s + 1, :] for ll in range(LG_PER_JB)])
        cr, ci = mr * cr - mi * ci + er, mr * ci + mi * cr + ei
    return cr, ci


def _s5_specs(nc, rev):
    cidx = (lambda c: nc - 1 - c) if rev else (lambda c: c)
    jb = lambda shape: pl.BlockSpec(shape, lambda j, c: (j, 0, 0))
    return cidx, [
        jb((1, LANES, 8 * LANES)),
        jb((1, 8 * LANES, LANES)),
        pl.BlockSpec((1, LANES), lambda j, c: (0, j)),
        jb((LG_PER_JB, SUBLANES, LANES)), jb((LG_PER_JB, SUBLANES, LANES)),
        jb((LG_PER_JB, 1, LANES)), jb((LG_PER_JB, 1, LANES)),
    ]


def _s5_fwd(za, prm, comm=None):
    T = za.shape[0]
    R = S5_CHUNK
    nc = T // R
    ub = (Q_W + 2 * KV_W) // LANES
    _, pspecs = _s5_specs(nc, False)

    def body(u_ref, b_ref, c_ref, d_ref, are_ref, aim_ref, alr_ref, ali_ref,
             yg_ref, x0r_ref, x0i_ref, bur, bui, xsr, xsi, sr, si, xcr, xci, utm, ynat):
        c = pl.program_id(1)

        @pl.when(c == 0)
        def _():
            xcr[...] = jnp.zeros(xcr.shape, F32)
            xci[...] = jnp.zeros(xci.shape, F32)

        _s5_to_time_major(u_ref, utm)
        u = utm[...]
        ub16 = u.astype(BF16)
        bu = jnp.dot(ub16, b_ref[0].astype(BF16), preferred_element_type=F32)
        for ll in range(LG_PER_JB):
            bur[ll] = bu[:, ll * LANES:(ll + 1) * LANES]
            bui[ll] = bu[:, (LG_PER_JB + ll) * LANES:(LG_PER_JB + ll + 1) * LANES]
        ar = [are_ref[ll] for ll in range(LG_PER_JB)]
        ai = [aim_ref[ll] for ll in range(LG_PER_JB)]
        ends = _s5_scan(bur, bui, ar, ai, False)
        in_r, in_i = xcr[...], xci[...]
        x0r_ref[0] = in_r
        x0i_ref[0] = in_i
        out_r, out_i = _s5_fixup(ends, in_r, in_i, alr_ref[...], ali_ref[...], sr, si, False)
        xcr[...] = out_r
        xci[...] = out_i
        _s5_scan(bur, bui, ar, ai, False, start=(sr, si), dst=(xsr, xsi))
        xcat =jnp.concatenate([xsr[ll].astype(BF16) for ll in range(LG_PER_JB)]
                               + [xsi[ll].astype(BF16) for ll in range(LG_PER_JB)], axis=1)
        y = d_ref[...] * u + jnp.dot(xcat, c_ref[0].astype(BF16), preferred_element_type=F32)
        _s5_from_time_major(_gelu(y), ynat)
        yg_ref[...] = ynat[...].astype(BF16)

    st = pl.BlockSpec((1, LG_PER_JB, 1, LANES), lambda j, c: (c, j, 0, 0))
    vm = lambda rows: pltpu.VMEM((LG_PER_JB, rows, LANES), F32)
    res = _call(
        "s5_fwd", body, (N_JB, nc),
        [pl.BlockSpec((R, LANES), lambda j, c: (c, ub + j))] + pspecs,
        [pl.BlockSpec((R, LANES), lambda j, c: (c, j)), st, st],
        [SDS((T, SSM_W), BF16), SDS((nc, N_LG, 1, LANES), F32), SDS((nc, N_LG, 1, LANES), F32)],
        [vm(R), vm(R), vm(R), vm(R), vm(SUBLANES), vm(SUBLANES), vm(1), vm(1),
         pltpu.VMEM((R, LANES), F32), pltpu.VMEM((R, LANES), F32)],
        ("parallel", "arbitrary"), (za, *prm), comm)
    return res if comm is None else (res[:3], res[3:])


def _s5_bwd(za, dyg, x0r, x0i, prm, comm=None):
    T = za.shape[0]
    R = S5_CHUNK
    nc = T // R
    ub = (Q_W + 2 * KV_W) // LANES
    cidx, pspecs = _s5_specs(nc, True)
    PAD = SUBLANES

    def body(u_ref, dyg_ref, x0r_ref, x0i_ref, b_ref, c_ref, d_ref, are_ref, aim_ref,
             alr_ref, ali_ref,
             du_ref, dar_ref, dai_ref, db_ref, dc_ref, dd_ref,
             bur, bui, xsr, xsi, sr, si, gcr, gci, utm, dtm, dunat):
        c = pl.program_id(1)

        @pl.when(c == 0)
        def _():
            gcr[...] = jnp.zeros(gcr.shape, F32)
            gci[...] = jnp.zeros(gci.shape, F32)
            dar_ref[...] = jnp.zeros(dar_ref.shape, F32)
            dai_ref[...] = jnp.zeros(dai_ref.shape, F32)
            db_ref[...] = jnp.zeros(db_ref.shape, F32)
            dc_ref[...] = jnp.zeros(dc_ref.shape, F32)
            dd_ref[...] = jnp.zeros(dd_ref.shape, F32)

        _s5_to_time_major(u_ref, utm)
        _s5_to_time_major(dyg_ref, dtm)
        u = utm[...]
        ub16 = u.astype(BF16)
        bcat, ccat = b_ref[0].astype(BF16), c_ref[0].astype(BF16)
        lanes = lambda v, ll: v[:, ll * LANES:(ll + 1) * LANES]
        bu = jnp.dot(ub16, bcat, preferred_element_type=F32)
        for ll in range(LG_PER_JB):
            bur[ll] = lanes(bu, ll)
            bui[ll] = lanes(bu, LG_PER_JB + ll)
        ar = [are_ref[ll] for ll in range(LG_PER_JB)]
        ai = [aim_ref[ll] for ll in range(LG_PER_JB)]
        ends = _s5_scan(bur, bui, ar, ai, False)
        in_r, in_i = x0r_ref[0], x0i_ref[0]
        _s5_fixup(ends, in_r, in_i, alr_ref[...], ali_ref[...], sr, si, False)
        _s5_scan(bur, bui, ar, ai, False, start=(sr, si), dst=(xsr, xsi), dst_row0=PAD)
        xsr[:, 0:PAD, :] = sr[...]
        xsi[:, 0:PAD, :] = si[...]
        xcat = jnp.concatenate([xsr[ll, PAD:, :].astype(BF16) for ll in range(LG_PER_JB)]
                               + [xsi[ll, PAD:, :].astype(BF16) for ll in range(LG_PER_JB)], axis=1)
        y = d_ref[...] * u + jnp.dot(xcat, ccat, preferred_element_type=F32)
        dy = dtm[...] * _gelu_grad(y)
        dyb = dy.astype(BF16)
        dd_ref[...] += jnp.sum(dy * u, axis=0, keepdims=True)
        du = d_ref[...] * dy
        dc_ref[0] += lax.dot_general(dyb, xcat, _TN, preferred_element_type=F32)
        g = lax.dot_general(dyb, ccat, _NT, preferred_element_type=F32)
        for ll in range(LG_PER_JB):
            bur[ll] = lanes(g, ll)
            bui[ll] = lanes(g, LG_PER_JB + ll)
        aic = [-v for v in ai]
        ends = _s5_scan(bur, bui, ar, aic, True)
        out_r, out_i = _s5_fixup(ends, gcr[...], gci[...], alr_ref[...], -ali_ref[...], sr, si, True)
        gcr[...] = out_r
        gci[...] = out_i
        _s5_scan(bur, bui, ar, aic, True, start=(sr, si), dst=(bur, bui))
        for ll in range(LG_PER_JB):
            gr, gi = bur[ll], bui[ll]
            xpr, xpi = xsr[ll, 0:R, :], xsi[ll, 0:R, :]
            red = lambda v: v.reshape(R // SUBLANES, SUBLANES, LANES).sum(axis=0)
            dar_ref[ll] += red(xpr * gr + xpi * gi)
            dai_ref[ll] += red(xpr * gi - xpi * gr)
        gcat = jnp.concatenate([bur[ll].astype(BF16) for ll in range(LG_PER_JB)]
                               + [bui[ll].astype(BF16) for ll in range(LG_PER_JB)], axis=1)
        db_ref[0] += lax.dot_general(ub16, gcat, _TN, preferred_element_type=F32)
        du = du + lax.dot_general(gcat, bcat, _NT, preferred_element_type=F32)
        _s5_from_time_major(du, dunat)
        du_ref[...] = dunat[...].astype(du_ref.dtype)

    st = pl.BlockSpec((1, LG_PER_JB, 1, LANES), lambda j, c: (cidx(c), j, 0, 0))
    jb = lambda shape: pl.BlockSpec(shape, lambda j, c: (j, 0, 0))
    vm = lambda rows: pltpu.VMEM((LG_PER_JB, rows, LANES), F32)
    res = _call(
        "s5_bwd", body, (N_JB, nc),
        [pl.BlockSpec((R, LANES), lambda j, c: (cidx(c), ub + j)),
         pl.BlockSpec((R, LANES), lambda j, c: (cidx(c), j)), st, st] + pspecs,
        [pl.BlockSpec((R, LANES), lambda j, c: (cidx(c), j)),
         jb((LG_PER_JB, SUBLANES, LANES)), jb((LG_PER_JB, SUBLANES, LANES)),
         jb((1, LANES, 8 * LANES)), jb((1, LANES, 8 * LANES)),
         pl.BlockSpec((1, LANES), lambda j, c: (0, j))],
        [SDS((T, SSM_W), BF16), SDS((N_LG, SUBLANES, LANES), F32), SDS((N_LG, SUBLANES, LANES), F32),
         SDS((N_JB, LANES, 8 * LANES), F32), SDS((N_JB, LANES, 8 * LANES), F32), SDS((1, SSM_W), F32)],
        [vm(R), vm(R), vm(R + PAD), vm(R + PAD), vm(SUBLANES), vm(SUBLANES), vm(1), vm(1)]
        + [pltpu.VMEM((R, LANES), F32)] * 3,
        ("parallel", "arbitrary"), (za, dyg, x0r, x0i, *prm), comm)
    return res if comm is None else (res[:6], res[6:])


def _assemble_w_a(wi):
    _, rows, cb = wi.shape
    tr = 256

    def body(w_ref, a_ref):
        a_ref[:, :cb] = w_ref[0]
        a_ref[:, cb:] = w_ref[1, :, :ZA_W - cb]

    return pl.pallas_call(
        body, name="assemble_w_a", grid=(rows // tr,),
        in_specs=[pl.BlockSpec((2, tr, cb), lambda i: (0, i, 0))],
        out_specs=pl.BlockSpec((tr, ZA_W), lambda i: (i, 0)),
        out_shape=SDS((rows, ZA_W), wi.dtype), compiler_params=_cp(("parallel",)))(wi)


def _assemble_w_g(wi):
    _, rows, cb = wi.shape
    tr = 256
    cut = ZA_W - cb

    def body(w_ref, g_ref):
        g_ref[:, :cb - cut] = w_ref[1, :, cut:]
        g_ref[:, cb - cut:2 * cb - cut] = w_ref[2]
        g_ref[:, 2 * cb - cut:] = w_ref[3]

    return pl.pallas_call(
        body, name="assemble_w_g", grid=(rows // tr,),
        in_specs=[pl.BlockSpec((4, tr, cb), lambda i: (0, i, 0))],
        out_specs=pl.BlockSpec((tr, 4 * cb - ZA_W), lambda i: (i, 0)),
        out_shape=SDS((rows, 4 * cb - ZA_W), wi.dtype), compiler_params=_cp(("parallel",)))(wi)


def _stack_w_in_grad(d_w_a, d_w_g):
    rows = d_w_a.shape[0]
    cb = (ZA_W + d_w_g.shape[1]) // 4
    cut = ZA_W - cb
    tr = 256

    def body(a_ref, g_ref, o_ref):
        o_ref[0] = a_ref[:, :cb]
        o_ref[1, :, :cut] = a_ref[:, cb:]
        o_ref[1, :, cut:] = g_ref[:, :cb - cut]
        o_ref[2] = g_ref[:, cb - cut:2 * cb - cut]
        o_ref[3] = g_ref[:, 2 * cb - cut:]

    return pl.pallas_call(
        body, name="stack_w_in_grad", grid=(rows // tr,),
        in_specs=[pl.BlockSpec((tr, ZA_W), lambda i: (i, 0)), pl.BlockSpec((tr, d_w_g.shape[1]), lambda i: (i, 0))],
        out_specs=pl.BlockSpec((4, tr, cb), lambda i: (0, i, 0)),
        out_shape=SDS((4, rows, cb), d_w_a.dtype), compiler_params=_cp(("parallel",)))(d_w_a, d_w_g)


def _local_step(x, target, gains, w_a_of, sinks, s5w, comms, late_g, late, red=None):
    T = x.shape[0]
    D = D_MODEL
    g1, g2, g3, g4 = gains
    cos, sin = _rope_tables(T)
    lam_re, lam_im, log_dt, b_re, b_im, c_re, c_im, d_skip = s5w
    (a_re, a_im, bb_re, bb_im), disc_vjp = jax.vjp(_s5_discretize, lam_re, lam_im, log_dt, b_re, b_im)
    abr, abi, al_re, al_im = _s5_tables(a_re, a_im)
    prm = (jnp.concatenate([_blockdiag_in(bb_re), _blockdiag_in(bb_im)], axis=2),
           jnp.concatenate([_blockdiag_out(c_re), -_blockdiag_out(c_im)], axis=1),
           d_skip.reshape(1, SSM_W), abr, abi, al_re, al_im)
    mm = functools.partial(_mm, tm=1024, tn=1024, tk=2048)

    h = _rowwise(lambda xv, g: ((_rms(xv)[0] * g,), ()), [(x, D, 0)], [g1], [(D, BF16)], [], tr=512, name="norm1")[0]
    unpack = lambda res, comm: (res, ()) if comm is None else res
    w_a = w_a_of(h, *prm, cos, sin)
    za, got_a = unpack(_mm(h, w_a, mode="nn", out_dtype=F32, tm=1024, tn=1152, tk=2048, name="mm_za", comm=comms[0]), comms[0])
    w_g = late_g(got_a)
    zg, got0 = unpack(mm(h, w_g, mode="nn", out_dtype=BF16, name="mm_zg", tn=2048, comm=comms[1]), comms[1])
    o_attn, got1 = unpack(_attn_fwd(za, cos, sin, sinks, comm=comms[2]), comms[2])
    (yg, x0r, x0i), got2 = unpack(_s5_fwd(za, prm, comm=comms[3]), comms[3])
    w_glu, w_ba, w_bs, w_out, w_up, w_down = late(got0, got1, got2)
    zglu = mm(yg, w_glu, mode="nn", out_dtype=BF16, name="mm_glu")
    o_ssm = _rowwise(lambda z1, z2: ((z1 * _sig(z2),), ()), [(zglu, SSM_W, 0), (zglu, SSM_W, 1)], [],
                     [(SSM_W, BF16)], [], tr=512, name="glu")[0]
    ya = mm(o_attn, w_ba, mode="nn", out_dtype=BF16, name="mm_ya")
    ys = mm(o_ssm, w_bs, mode="nn", out_dtype=BF16, name="mm_ys")
    mi = _rowwise(lambda ga, gs, a, s: ((_sig(ga) * a + _sig(gs) * s,), ()),
                  [(zg, D, 0), (zg, D, 1), (ya, D, 0), (ys, D, 0)], [], [(D, BF16)], [], tr=256, name="gate")[0]
    mixed = mm(mi, w_out, mode="nn", out_dtype=F32, name="mm_out")

    def f_post(xv, mv, g2v, g3v):
        x1v = xv + _rms(mv)[0] * g2v
        return (x1v, _rms(x1v)[0] * g3v), ()
    x1, h2 = _rowwise(f_post, [(x, D, 0), (mixed, D, 0)], [g2, g3], [(D, F32), (D, BF16)], [], tr=256, name="post_mix")
    act = mm(h2, w_up, mode="nn", out_dtype=BF16, name="mm_up", tn=2048, epi=lambda v: jnp.maximum(v, 0.0))
    f = mm(act, w_down, mode="nn", out_dtype=F32, name="mm_down", a_fn=lambda v: v * v, tk=4096)

    def f_final(x1v, fv, tv, g4v):
        fn, r = _rms(fv)
        e = x1v + fn * g4v - tv
        dx2v = e * (1.0 / D)
        dfv, dg4v = _rms_bwd(dx2v, fn, r, g4v)
        return (dfv, dx2v), (dg4v, jnp.zeros((SUBLANES, LANES), F32) + 0.5 * jnp.sum(e * e) * (1.0 / D))
    df, dx2, dg4, lossb = _rowwise(f_final, [(x1, D, 0), (f, D, 0), (target, D, 0)], [g4],
                                   [(D, BF16), (D, F32)], [(1, D), (SUBLANES, LANES)], tr=256, name="final")

    big = {}

    def add(k, g4):
        big[k] = g4
        if red is not None:
            red.add(k, g4)

    def hosted(fn, stage, names):
        if red is None:
            return fn(comm=None)
        out, got = fn(comm=getattr(red, stage)(names))
        getattr(red, stage + "_done")(names, got)
        return out

    dpre = mm(df, w_down, mode="nt", out_dtype=BF16, name="mm_dact", tn=2048,
              epi=lambda v, a: v * (2.0 * a.astype(F32)), extras=(act,))
    wg = functools.partial(_mm, mode="tn", out_dtype=F32, tm=1024, tn=1024, tk=4096)
    add("w_down", wg(act, df, name="wg_down", a_fn=lambda v: v * v).reshape(4, D_FF // 4, D))
    dh2 = hosted(functools.partial(mm, dpre, w_up, mode="nt", out_dtype=BF16, name="mm_dh2", tk=4096),
                 "s1", ["w_down"])
    add("w_up", hosted(functools.partial(wg, h2, dpre, name="wg_up", shard_cols=D_FF // 4), "s3", ["w_down"]))

    def f_mid(dx2v, dh2v, x1v, mv, g2v, g3v):
        x1n, r3 = _rms(x1v)
        d3, dg3v = _rms_bwd(dh2v, x1n, r3, g3v)
        dx1v = dx2v + d3
        mn, r2 = _rms(mv)
        dmv, dg2v = _rms_bwd(dx1v, mn, r2, g2v)
        return (dx1v, dmv), (dg3v, dg2v)
    dx1, dmixed, dg3, dg2 = _rowwise(f_mid, [(dx2, D, 0), (dh2, D, 0), (x1, D, 0), (mixed, D, 0)], [g2, g3],
                                     [(D, F32), (D, BF16)], [(1, D), (1, D)], tr=256, name="mid")

    dmi = hosted(functools.partial(mm, dmixed, w_out, mode="nt", out_dtype=BF16, name="mm_dmi"), "s1", ["w_up"])
    add("w_out", wg(mi, dmixed, name="wg_out").reshape(4, D // 4, D))

    def f_gate(dv, ga, gs, a, s):
        sa, ss = _sig(ga), _sig(gs)
        return (dv * sa, dv * ss, jnp.concatenate([dv * a * sa * (1.0 - sa), dv * s * ss * (1.0 - ss)], axis=1)), ()
    dya, dys, dzg = _rowwise(f_gate, [(dmi, D, 0), (zg, D, 0), (zg, D, 1), (ya, D, 0), (ys, D, 0)], [],
                             [(D, BF16), (D, BF16), (2 * D, BF16)], [], tr=256, name="gate_bwd")
    do_attn = hosted(functools.partial(mm, dya, w_ba, mode="nt", out_dtype=BF16, name="mm_doa"), "s1", ["w_out"])
    d_w_ba = wg(o_attn, dya, name="wg_ba")
    do_ssm = mm(dys, w_bs, mode="nt", out_dtype=BF16, name="mm_dos")
    d_w_bs = wg(o_ssm, dys, name="wg_bs")
    add("w_branch", jnp.concatenate([d_w_ba.reshape(2, D // 4, D), d_w_bs.reshape(2, D // 4, D)], axis=0))

    def f_glu(dv, z1, z2):
        s2 = _sig(z2)
        return (jnp.concatenate([dv * s2, dv * z1 * s2 * (1.0 - s2)], axis=1),), ()
    dzglu = _rowwise(f_glu, [(do_ssm, SSM_W, 0), (zglu, SSM_W, 0), (zglu, SSM_W, 1)], [], [(2 * SSM_W, BF16)], [],
                     tr=512, name="glu_bwd")[0]
    dyg = hosted(functools.partial(mm, dzglu, w_glu, mode="nt", out_dtype=F32, name="mm_dyg"), "s1", ["w_branch"])
    add("w_glu", wg(yg, dzglu, name="wg_glu", tn=SSM_W // 2, shard_cols=SSM_W // 2))
    du, dar, dai, dbc, dcc, ddv = hosted(functools.partial(_s5_bwd, za, dyg, x0r, x0i, prm),
                                         "s3", ["w_up", "w_out", "w_branch"])
    dbr, dbi = dbc[:, :, :4 * LANES], dbc[:, :, 4 * LANES:]
    dcc = dcc.transpose(0, 2, 1)
    dcr, dci = dcc[:, :4 * LANES, :], -dcc[:, 4 * LANES:, :]
    dq, dkv, dsk = hosted(functools.partial(_attn_bwd, za, cos, sin, sinks, o_attn, do_attn),
                          "s5", ["w_down", "w_up", "w_out", "w_branch"])
    dza = jnp.concatenate([dq, dkv, du], axis=1)
    d_w_a = _mm(h, dza, mode="tn", out_dtype=F32, tm=1024, tn=ZA_W // 2, tk=2048, name="wg_a")
    d_w_g = wg(h, dzg, name="wg_g")
    add("w_in", _stack_w_in_grad(d_w_a, d_w_g))
    dh = hosted(functools.partial(mm, dza, w_a, mode="nt", out_dtype=F32, name="mm_dh_a", tk=ZA_W), "s1", ["w_in", "w_glu"])
    dh = hosted(functools.partial(mm, dzg, w_g, mode="nt", out_dtype=BF16, name="mm_dh_g",
                                  epi=lambda v, p: v + p, extras=(dh,)), "s3", ["w_in", "w_glu"])

    def f_first(dx1v, dhv, xv, g1v):
        xn, r1 = _rms(xv)
        d1, dg1v = _rms_bwd(dhv, xn, r1, g1v)
        return (dx1v + d1,), (dg1v,)
    dx, dg1 = _rowwise(f_first, [(dx1, D, 0), (dh, D, 0), (x, D, 0)], [g1], [(D, F32)], [(1, D)], tr=256, name="first")

    da_re = dar.sum(axis=1).reshape(SSM_G, SSM_P)
    da_im = dai.sum(axis=1).reshape(SSM_G, SSM_P)
    d_lam_re, d_lam_im, d_log_dt, d_b_re, d_b_im = disc_vjp(
        (da_re, da_im, _blockdiag_in_extract(dbr), _blockdiag_in_extract(dbi)))
    small = dict(norm_mix_pre=dg1, norm_mix_post=dg2, norm_mlp_pre=dg3, norm_mlp_post=dg4,
                 sinks=dsk[:, :N_Q_HEADS], lam_re=d_lam_re, lam_im=d_lam_im, log_dt=d_log_dt,
                 b_re=d_b_re, b_im=d_b_im, c_re=_blockdiag_out_extract(dcr), c_im=_blockdiag_out_extract(dci),
                 d_skip=ddv.reshape(SSM_G, SSM_GC))
    return lossb[0, 0], dx, small, big


def _cast_into_slot(w, k_arr):
    rows, cols = w.shape
    tr = 256

    def body(k_ref, w_ref, o_ref):
        o_ref[0] = w_ref[...].astype(BF16)

    return pl.pallas_call(
        body,
        name="cast_into_slot",
        grid_spec=pltpu.PrefetchScalarGridSpec(
            num_scalar_prefetch=1,
            grid=(rows // tr,),
            in_specs=[pl.BlockSpec((tr, cols), lambda i, k: (i, 0))],
            out_specs=pl.BlockSpec((1, tr, cols), lambda i, k: (k[0], i, 0)),
        ),
        out_shape=SDS((4, rows, cols), BF16),
        compiler_params=_cp(("parallel",)),
    )(k_arr, w)


def _pair_sum(g, r, c_arr):
    _, _, hr, cols = g.shape
    tr = min(256, hr)

    def body(c_ref, g_ref, r_ref, o_ref):
        o_ref[0] = (g_ref[0, 0] + r_ref[0]).astype(BF16)

    return pl.pallas_call(
        body,
        name="pair_sum",
        grid_spec=pltpu.PrefetchScalarGridSpec(
            num_scalar_prefetch=1,
            grid=(3, hr // tr),
            in_specs=[pl.BlockSpec((1, 1, tr, cols), lambda k, i, c_ref: (c_ref[1 + k], c_ref[0], i, 0)),
                      pl.BlockSpec((1, tr, cols), lambda k, i, c_ref: (c_ref[1 + k], i, 0))],
            out_specs=pl.BlockSpec((1, tr, cols), lambda k, i, c_ref: (c_ref[1 + k], i, 0)),
        ),
        out_shape=SDS((4, hr, cols), BF16),
        compiler_params=_cp(("parallel", "parallel")),
    )(c_arr, g, r)


def _chip_sum(g, r, q, kc_arr):
    _, _, hr, cols = g.shape
    tr = min(256, hr)

    def body(kc_ref, g_ref, r_ref, q_ref, o_ref):
        s = g_ref[0, 0] + r_ref[0]
        for j in range(3):
            s = s + q_ref[j].astype(F32)
        o_ref[...] = s

    return pl.pallas_call(
        body,
        name="chip_sum",
        grid_spec=pltpu.PrefetchScalarGridSpec(
            num_scalar_prefetch=1,
            grid=(hr // tr,),
            in_specs=[pl.BlockSpec((1, 1, tr, cols), lambda i, kc: (kc[0], kc[1], i, 0)),
                      pl.BlockSpec((1, tr, cols), lambda i, kc: (kc[0], i, 0)),
                      pl.BlockSpec((3, tr, cols), lambda i, kc: (0, i, 0))],
            out_specs=pl.BlockSpec((tr, cols), lambda i, kc: (kc[1] * (hr // tr) + i, 0)),
        ),
        out_shape=SDS((2 * hr, cols), F32),
        compiler_params=_cp(("parallel",)),
    )(kc_arr, g, r, q)


class _PairShareComm:
    aliased = True

    def __init__(self, blocks):
        self.arrs = list(blocks)
        self.n = len(self.arrs)
        dma = pltpu.SemaphoreType.DMA
        self.scratch = [dma((self.n,)), dma((self.n,))]
        self.out_shape = [SDS(b.shape, b.dtype) for b in self.arrs]

    def _copies(self, ins, outs, sems, hc):
        ssem, rsem = sems
        x, y, c, _ = _place()
        cps = []
        for w in range(self.n):
            hr = ins[w].shape[0] // 2
            rows = pl.ds(pl.multiple_of((c if hc == 0 else 1 - c) * hr, 8), hr)
            cps.append(_remote(ins[w].at[rows, :], outs[w].at[rows, :], ssem.at[w], rsem.at[w], (x, y, 1 - c)))
        return cps

    def start(self, ins, outs, sems):
        for cp in self._copies(ins, outs, sems, 0):
            cp.start()

    def finish(self, ins, outs, sems):
        for cp in self._copies(ins, outs, sems, 1):
            cp.wait_recv()
        for cp in self._copies(ins, outs, sems, 0):
            cp.wait_send()


class _GradReducer:
    def __init__(self, c_arr, kc_arr):
        self.c_arr, self.kc_arr = c_arr, kc_arr
        self.g, self.r, self.ps, self.q, self.done = {}, {}, {}, {}, {}

    def add(self, k, g4):
        self.g[k] = g4.reshape(4, 2, g4.shape[1] // 2, g4.shape[2])

    def s1(self, names):
        return _PairExchangeComm([self.g[k].reshape(4, -1, self.g[k].shape[3]) for k in names])

    def s1_done(self, names, got):
        for k, r in zip(names, got):
            self.r[k] = r
            self.ps[k] = _pair_sum(self.g[k], r, self.c_arr)

    def s3(self, names):
        return _ChipExchangeComm([self.ps[k] for k in names])

    def s3_done(self, names, got):
        self.q.update(zip(names, got))

    def finish(self, order):
        rest = [k for k in order if k not in self.r]
        if rest:
            self.s1_done(rest, _comm_only("pair_exchange", self.s1(rest)))
        rest = [k for k in order if k not in self.q]
        if rest:
            self.s3_done(rest, _comm_only("chip_exchange", self.s3(rest)))
        rest = [k for k in order if k not in self.done]
        if rest:
            self.s5_done(rest, _comm_only("pair_share", self.s5(rest)))
        return self.done

    def s5(self, names):
        return _PairShareComm([_chip_sum(self.g[k], self.r[k], self.q[k], self.kc_arr) for k in names])

    def s5_done(self, names, got):
        self.done.update(zip(names, got))


def _all_reduce_small(buf):
    rows = buf.shape[0]
    hr = rows // 2
    assert hr % SUBLANES == 0

    def body(in_ref, o_ref, sib, pair, slots, ssem, rsem):
        x, y, c, others = _place()
        me, sibling = 2 * x + y, (x, y, 1 - c)
        mine = pl.ds(pl.multiple_of(c * hr, SUBLANES), hr)
        theirs = pl.ds(pl.multiple_of((1 - c) * hr, SUBLANES), hr)
        first = _remote(in_ref, sib, ssem.at[0], rsem.at[0], sibling)
        first.start()
        first.wait()
        pair[...] = in_ref[...] + sib[...]
        slots[me] = pair[mine, :]
        cps = [_remote(pair.at[mine, :], slots.at[me], ssem.at[1 + r], rsem.at[1 + r], (ox, oy, c))
               for r, (ox, oy) in enumerate(others)]
        for cp in cps:
            cp.start()
        for r, (ox, oy) in enumerate(others):
            _remote(pair.at[mine, :], slots.at[2 * ox + oy], ssem.at[1 + r], rsem.at[1 + r], (ox, oy, c)).wait_recv()
        o_ref[mine, :] = (slots[0] + slots[1]) + (slots[2] + slots[3])
        last = _remote(o_ref.at[mine, :], o_ref.at[mine, :], ssem.at[4], rsem.at[4], sibling)
        last.start()
        _remote(o_ref.at[theirs, :], o_ref.at[theirs, :], ssem.at[4], rsem.at[4], sibling).wait_recv()
        last.wait_send()
        for cp in cps:
            cp.wait_send()

    dma = pltpu.SemaphoreType.DMA
    return pl.pallas_call(
        body,
        name="all_reduce_small",
        in_specs=[pl.BlockSpec(memory_space=pltpu.VMEM)],
        out_specs=pl.BlockSpec(memory_space=pltpu.VMEM),
        out_shape=SDS(buf.shape, F32),
        scratch_shapes=[pltpu.VMEM((rows, LANES), F32), pltpu.VMEM((rows, LANES), F32),
                        pltpu.VMEM((4, hr, LANES), F32), dma((5,)), dma((5,))],
        compiler_params=pltpu.CompilerParams(vmem_limit_bytes=VMEM_LIMIT),
    )(buf)


def _adam_fn(w, g, m, v):
    m2 = ADAM_B1 * m + (1.0 - ADAM_B1) * g
    v2 = ADAM_B2 * v + (1.0 - ADAM_B2) * (g * g)
    m_hat = m2 / (1.0 - ADAM_B1 ** ADAM_STEP)
    v_hat = v2 / (1.0 - ADAM_B2 ** ADAM_STEP)
    return (-ADAM_LR * (m_hat / (jnp.sqrt(v_hat) + ADAM_EPS) + ADAM_WD * w), m2, v2), ()


def _adamw(w, g, m, v, name, tr=256):
    cols = w.shape[1]
    return _rowwise(_adam_fn, [(w, cols, 0), (g, cols, 0), (m, cols, 0), (v, cols, 0)], [],
                    [(cols, F32)] * 3, [], tr=tr, name=name)


BIG = ("w_in", "w_glu", "w_branch", "w_out", "w_up", "w_down")
COL_SHARDED = ("w_in", "w_glu", "w_up")
SMALL = ("norm_mix_pre", "norm_mix_post", "norm_mlp_pre", "norm_mlp_post", "sinks", "lam_re", "lam_im", "log_dt",
         "b_re", "b_im", "c_re", "c_im", "d_skip")
WEIGHTS = ("norm_mix_pre", "norm_mix_post", "norm_mlp_pre", "norm_mlp_post", "w_in", "sinks", "lam_re", "lam_im",
           "log_dt", "b_re", "b_im", "c_re", "c_im", "d_skip", "w_glu", "w_branch", "w_out", "w_up", "w_down")


def _flat_small(vals, extra):
    flat = jnp.concatenate([vals[k].reshape(-1) for k in SMALL] + [extra.reshape(-1)])
    rows = -(-flat.shape[0] // (SUBLANES * LANES)) * SUBLANES
    return jnp.pad(flat, (0, rows * LANES - flat.shape[0])).reshape(rows, LANES)


def kernel(x, norm_mix_pre, norm_mix_post, norm_mlp_pre, norm_mlp_post, w_in, sinks, lam_re, lam_im, log_dt, b_re, b_im, c_re, c_im, d_skip, w_glu, w_branch, w_out, w_up, w_down, loss_target, m_norm_mix_pre, m_norm_mix_post, m_norm_mlp_pre, m_norm_mlp_post, m_w_in, m_sinks, m_lam_re, m_lam_im, m_log_dt, m_b_re, m_b_im, m_c_re, m_c_im, m_d_skip, m_w_glu, m_w_branch, m_w_out, m_w_up, m_w_down, v_norm_mix_pre, v_norm_mix_post, v_norm_mlp_pre, v_norm_mlp_post, v_w_in, v_sinks, v_lam_re, v_lam_im, v_log_dt, v_b_re, v_b_im, v_c_re, v_c_im, v_d_skip, v_w_glu, v_w_branch, v_w_out, v_w_up, v_w_down):
    w = dict(norm_mix_pre=norm_mix_pre, norm_mix_post=norm_mix_post, norm_mlp_pre=norm_mlp_pre, norm_mlp_post=norm_mlp_post,
             w_in=w_in, sinks=sinks, lam_re=lam_re, lam_im=lam_im, log_dt=log_dt, b_re=b_re, b_im=b_im, c_re=c_re,
             c_im=c_im, d_skip=d_skip, w_glu=w_glu, w_branch=w_branch, w_out=w_out, w_up=w_up, w_down=w_down)
    m = dict(norm_mix_pre=m_norm_mix_pre, norm_mix_post=m_norm_mix_post, norm_mlp_pre=m_norm_mlp_pre,
             norm_mlp_post=m_norm_mlp_post, w_in=m_w_in, sinks=m_sinks, lam_re=m_lam_re, lam_im=m_lam_im,
             log_dt=m_log_dt, b_re=m_b_re, b_im=m_b_im, c_re=m_c_re, c_im=m_c_im, d_skip=m_d_skip, w_glu=m_w_glu,
             w_branch=m_w_branch, w_out=m_w_out, w_up=m_w_up, w_down=m_w_down)
    v = dict(norm_mix_pre=v_norm_mix_pre, norm_mix_post=v_norm_mix_post, norm_mlp_pre=v_norm_mlp_pre,
             norm_mlp_post=v_norm_mlp_post, w_in=v_w_in, sinks=v_sinks, lam_re=v_lam_re, lam_im=v_lam_im,
             log_dt=v_log_dt, b_re=v_b_re, b_im=v_b_im, c_re=v_c_re, c_im=v_c_im, d_skip=v_d_skip, w_glu=v_w_glu,
             w_branch=v_w_branch, w_out=v_w_out, w_up=v_w_up, w_down=v_w_down)
    xi, yi, ci = lax.axis_index("x"), lax.axis_index("y"), lax.axis_index("c")

    k_arr = jnp.stack([2 * xi + yi]).astype(jnp.int32)
    *w_in_sems, w_in_thru = _gather_start(_cast_into_slot(w["w_in"][0], k_arr))
    slot = {k: _cast_into_slot(w[k][0], k_arr) for k in BIG if k != "w_in"}

    def whole(k, g4):
        if k in COL_SHARDED:
            return jnp.concatenate([g4[j] for j in range(4)], axis=1)
        return g4.reshape(4 * g4.shape[1], g4.shape[2])

    w_in = []

    def w_a_of(*after):
        arrived = _gather_wait(w_in_sems, w_in_thru, after + tuple(slot.values()))
        w_in.append(_comm_only("gather_w_in_pass", _PassOnComm([arrived]))[0])
        return _assemble_w_a(w_in[0])

    hosted = (("w_glu", "w_branch", "w_out"), ("w_up",), ("w_down",))
    comms = [None] + [_GatherComm([slot[k] for k in names]) for names in hosted]

    def late(*got):
        f = {k: whole(k, g4) for names, res in zip(hosted, got) for k, g4 in zip(names, res)}
        return f["w_glu"], f["w_branch"][:Q_W], f["w_branch"][Q_W:], f["w_out"], f["w_up"], f["w_down"]

    s5w = (lam_re[0], lam_im[0], log_dt[0], b_re[0], b_im[0], c_re[0], c_im[0], d_skip[0])
    reducer = _GradReducer(
        jnp.stack([ci, 2 * (1 - xi) + yi, 2 * xi + (1 - yi), 2 * (1 - xi) + (1 - yi)]).astype(jnp.int32),
        jnp.stack([2 * xi + yi, ci]).astype(jnp.int32))
    loss_part, dx, small, _ = _local_step(
        x[0], loss_target[0], (norm_mix_pre, norm_mix_post, norm_mlp_pre, norm_mlp_post),
        w_a_of, sinks, s5w, comms, lambda _: _assemble_w_g(w_in[0]), late, reducer)
    grads = reducer.finish(BIG)

    red = _all_reduce_small(_flat_small(small, loss_part)).reshape(-1)
    off = 0
    for k in SMALL:
        n = math.prod(w[k].shape)
        grads[k] = red[off:off + n].reshape(w[k].shape[1:])
        off += n
    loss = red[off]

    delta, new_m, new_v = {}, {}, {}
    for k in BIG:
        delta[k], new_m[k], new_v[k] = _adamw(w[k][0], grads[k], m[k][0], v[k][0], "adamw_" + k)
    zero = jnp.zeros((), F32)
    fw, fm, fv = (_flat_small({k: t[k] for k in SMALL}, zero) for t in (w, m, v))
    fg = _flat_small(grads, zero)
    sd, sm, sv = _adamw(fw, fg, fm, fv, "adamw_small", tr=fw.shape[0])
    off = 0
    for k in SMALL:
        n = math.prod(w[k].shape)
        delta[k], new_m[k], new_v[k] = (t.reshape(-1)[off:off + n].reshape(w[k].shape[1:]) for t in (sd, sm, sv))
        off += n

    lead = lambda t: t[None]
    return (loss, lead(dx), *[lead(grads[k]) for k in WEIGHTS], *[lead(delta[k]) for k in WEIGHTS],
            *[lead(new_m[k]) for k in WEIGHTS], *[lead(new_v[k]) for k in WEIGHTS])
```

```python
import functools
import math

import jax
import jax.numpy as jnp
from jax import lax
from jax.experimental import pallas as pl
from jax.experimental.pallas import tpu as pltpu

F32 = jnp.float32
BF16 = jnp.bfloat16
SDS = jax.ShapeDtypeStruct

D_MODEL = 2048
HEAD_DIM = 64
N_Q_HEADS = 16
ATT_BLOCK = 128
ROT_DIM = 16
ROPE_THETA = 500000.0
Q_W = 1024
KV_W = 128
SSM_W = 1024
SSM_G = 64
SSM_GC = 16
SSM_P = 64
N_STATE = SSM_G * SSM_P
LANES = 128
SUBLANES = 8
N_LG = N_STATE // LANES
N_JB = 8
LG_PER_JB = N_LG // N_JB
D_FF = 8192
ZA_W = Q_W + 2 * KV_W + SSM_W
EPS = 1e-6
S5_CHUNK = 2048
S5_SEG = S5_CHUNK // SUBLANES
VMEM_LIMIT = 56 * 1024 * 1024
NEG = -1e30

ADAM_LR = 0.001
ADAM_B1 = 0.9
ADAM_B2 = 0.999
ADAM_EPS = 1e-08
ADAM_WD = 0.01
ADAM_STEP = 10

MESH = pl.DeviceIdType.MESH


def _cp(sem):
    return pltpu.CompilerParams(dimension_semantics=sem, vmem_limit_bytes=VMEM_LIMIT)


ANY = pl.BlockSpec(memory_space=pl.ANY)


def _place():
    x, y, c = lax.axis_index("x"), lax.axis_index("y"), lax.axis_index("c")
    others = [(1 - x, y), (x, 1 - y), (1 - x, 1 - y)]
    return x, y, c, others


def _remote(src, dst, ssem, rsem, to):
    return pltpu.make_async_remote_copy(src_ref=src, dst_ref=dst, send_sem=ssem, recv_sem=rsem,
                                        device_id=to, device_id_type=MESH)


class _GatherComm:
    aliased = True

    def __init__(self, slotted):
        self.arrs = list(slotted)
        self.n = len(self.arrs)
        dma = pltpu.SemaphoreType.DMA
        self.scratch = [dma((3 * self.n,)) for _ in range(4)]
        self.out_shape = [SDS(s.shape, s.dtype) for s in self.arrs]

    @staticmethod
    def _half(ref, hc):
        hr = ref.shape[1] // 2
        return pl.ds(pl.multiple_of(hc * hr, 16), hr)

    def _sends(self, ins, outs, sems):
        ssem, rsem, _, _ = sems
        x, y, c, others = _place()
        me = 2 * x + y
        return [_remote(ins[w].at[me, self._half(ins[w], c), :], outs[w].at[me, self._half(ins[w], c), :],
                        ssem.at[3 * w + r], rsem.at[3 * w + r], (ox, oy, c))
                for w in range(self.n) for r, (ox, oy) in enumerate(others)]

    def start(self, ins, outs, sems):
        for cp in self._sends(ins, outs, sems):
            cp.start()

    def finish(self, ins, outs, sems):
        ssem, rsem, fs_sem, fr_sem = sems
        x, y, c, others = _place()
        sib = (x, y, 1 - c)
        passes = []
        for w in range(self.n):
            for r, (ox, oy) in enumerate(others):
                got = outs[w].at[2 * ox + oy, self._half(ins[w], c), :]
                _remote(got, got, ssem.at[3 * w + r], rsem.at[3 * w + r], (ox, oy, c)).wait_recv()
                cp = _remote(got, got, fs_sem.at[3 * w + r], fr_sem.at[3 * w + r], sib)
                cp.start()
                passes.append(cp)
        for w in range(self.n):
            for r, (ox, oy) in enumerate(others):
                got = outs[w].at[2 * ox + oy, self._half(ins[w], 1 - c), :]
                _remote(got, got, fs_sem.at[3 * w + r], fr_sem.at[3 * w + r], sib).wait_recv()
        for cp in self._sends(ins, outs, sems) + passes:
            cp.wait_send()


class _PairExchangeComm:
    aliased = False

    def __init__(self, grads):
        self.arrs = list(grads)
        self.n = len(self.arrs)
        dma = pltpu.SemaphoreType.DMA
        self.scratch = [dma((self.n,)), dma((self.n,))]
        self.out_shape = [SDS((4, g.shape[1] // 2, g.shape[2]), g.dtype) for g in self.arrs]

    def _copies(self, ins, outs, sems):
        ssem, rsem = sems
        x, y, c, _ = _place()
        cps = []
        for w in range(self.n):
            hr = ins[w].shape[1] // 2
            src = ins[w].at[:, pl.ds(pl.multiple_of((1 - c) * hr, 8), hr), :]
            cps.append(_remote(src, outs[w], ssem.at[w], rsem.at[w], (x, y, 1 - c)))
        return cps

    def start(self, ins, outs, sems):
        for cp in self._copies(ins, outs, sems):
            cp.start()

    def finish(self, ins, outs, sems):
        for cp in self._copies(ins, outs, sems):
            cp.wait()


class _ChipExchangeComm:
    aliased = False

    def __init__(self, psums):
        self.arrs = list(psums)
        self.n = len(self.arrs)
        dma = pltpu.SemaphoreType.DMA
        self.scratch = [dma((3 * self.n,)), dma((3 * self.n,))]
        self.out_shape = [SDS((3,) + p.shape[1:], p.dtype) for p in self.arrs]

    def _copies(self, ins, outs, sems):
        ssem, rsem = sems
        x, y, c, others = _place()
        return [_remote(ins[w].at[2 * ox + oy], outs[w].at[r], ssem.at[3 * w + r], rsem.at[3 * w + r], (ox, oy, c))
                for w in range(self.n) for r, (ox, oy) in enumerate(others)]

    def start(self, ins, outs, sems):
        for cp in self._copies(ins, outs, sems):
            cp.start()

    def finish(self, ins, outs, sems):
        for cp in self._copies(ins, outs, sems):
            cp.wait()


def _comm_only(name, comm):
    n = comm.n

    def body(*refs):
        ins, outs, sems = refs[:n], refs[n:2 * n], refs[2 * n:]
        comm.start(ins, outs, sems)
        comm.finish(ins, outs, sems)

    return pl.pallas_call(
        body, name=name, in_specs=[ANY] * n, out_specs=[ANY] * n, out_shape=comm.out_shape,
        input_output_aliases={w: w for w in range(n)} if comm.aliased else {},
        scratch_shapes=comm.scratch)(*comm.arrs)


HBM = pl.BlockSpec(memory_space=pltpu.HBM)
SEM = pl.BlockSpec(memory_space=pltpu.SEMAPHORE)
_EFFECT = pltpu.SideEffectType.DATAFLOW_SIDE_EFFECTING


def _gather_copies(ref, sems):
    x, y, c, others = _place()
    me = 2 * x + y
    half = _GatherComm._half(ref, c)
    out = [_remote(ref.at[me, half, :], ref.at[me, half, :], sems[r], sems[3 + r], (ox, oy, c))
           for r, (ox, oy) in enumerate(others)]
    arriving = [_remote(ref.at[me, half, :], ref.at[2 * ox + oy, half, :], sems[r], sems[3 + r], (ox, oy, c))
                for r, (ox, oy) in enumerate(others)]
    return out, arriving


def _gather_start(slotted):
    def body(w_ref, *rest):
        for cp in _gather_copies(rest[6], rest[:6])[0]:
            cp.start()

    dma = pltpu.SemaphoreType.DMA(())
    return pl.pallas_call(
        body, name="gather_w_in_start",
        out_shape=(dma,) * 6 + (pltpu.HBM(slotted.shape, slotted.dtype),),
        in_specs=(HBM,), out_specs=(SEM,) * 6 + (HBM,), input_output_aliases={0: 6},
        compiler_params=pltpu.CompilerParams(has_side_effects=_EFFECT),
    )(pltpu.with_memory_space_constraint(slotted, pltpu.HBM))


def _gather_wait(sems, thru, after):
    def body(w_ref, *rest):
        out, arriving = _gather_copies(w_ref, rest[:6])
        for cp in out:
            cp.wait_send()
        for cp in arriving:
            cp.wait_recv()

    n = len(after)
    return pl.pallas_call(
        body, name="gather_w_in_wait", out_shape=(pltpu.HBM(thru.shape, thru.dtype),),
        in_specs=(HBM,) + (SEM,) * 6 + (ANY,) * n, out_specs=(HBM,), input_output_aliases={0: 0},
        compiler_params=pltpu.CompilerParams(has_side_effects=_EFFECT),
    )(thru, *sems, *after)[0]


class _PassOnComm:
    aliased = True

    def __init__(self, gathered):
        self.arrs = list(gathered)
        self.n = len(self.arrs)
        dma = pltpu.SemaphoreType.DMA
        self.scratch = [dma((3 * self.n,)), dma((3 * self.n,))]
        self.out_shape = [SDS(s.shape, s.dtype) for s in self.arrs]

    def _copies(self, ins, outs, sems, hc):
        ssem, rsem = sems
        x, y, c, others = _place()
        return [_remote(ins[w].at[2 * ox + oy, _GatherComm._half(ins[w], c if hc == 0 else 1 - c), :],
                        outs[w].at[2 * ox + oy, _GatherComm._half(ins[w], c if hc == 0 else 1 - c), :],
                        ssem.at[3 * w + r], rsem.at[3 * w + r], (x, y, 1 - c))
                for w in range(self.n) for r, (ox, oy) in enumerate(others)]

    def start(self, ins, outs, sems):
        for cp in self._copies(ins, outs, sems, 0):
            cp.start()

    def finish(self, ins, outs, sems):
        for cp in self._copies(ins, outs, sems, 1):
            cp.wait_recv()
        for cp in self._copies(ins, outs, sems, 0):
            cp.wait_send()


def _call(name, body, grid, in_specs, out_specs, out_shape, scratch, dims, args, comm=None):
    if comm is None:
        return pl.pallas_call(body, name=name, grid=grid, in_specs=in_specs, out_specs=out_specs, out_shape=out_shape,
                              scratch_shapes=scratch, compiler_params=_cp(dims))(*args)
    ni, no, ns, n = len(in_specs), len(out_shape), len(scratch), comm.n

    def hosted(*refs):
        ins, cin = refs[:ni], refs[ni:ni + n]
        outs, cout = refs[ni + n:ni + n + no], refs[ni + n + no:ni + 2 * n + no]
        scr, sems = refs[ni + 2 * n + no:ni + 2 * n + no + ns], refs[ni + 2 * n + no + ns:]
        ids = [pl.program_id(d) for d in range(len(grid))]
        first = functools.reduce(jnp.logical_and, [i == 0 for i in ids])
        last = functools.reduce(jnp.logical_and, [i == g - 1 for i, g in zip(ids, grid)])

        @pl.when(first)
        def _():
            comm.start(cin, cout, sems)

        body(*ins, *outs, *scr)

        @pl.when(last)
        def _():
            comm.finish(cin, cout, sems)

    return pl.pallas_call(
        hosted, name=name, grid=grid, in_specs=list(in_specs) + [ANY] * n, out_specs=list(out_specs) + [ANY] * n,
        out_shape=list(out_shape) + comm.out_shape,
        input_output_aliases={ni + w: no + w for w in range(n)} if comm.aliased else {},
        scratch_shapes=list(scratch) + comm.scratch, compiler_params=_cp(("arbitrary",) * len(grid)))(*args, *comm.arrs)


def _mm(a, b, *, mode, out_dtype, tm, tn, tk, name, a_fn=None, epi=None, extras=(), extra_cols=None, comm=None,
        shard_cols=None):
    if mode == "nn":
        (M, K), (K2, N) = a.shape, b.shape
    elif mode == "nt":
        (M, K), (N, K2) = a.shape, b.shape
    else:
        (K, M), (K2, N) = a.shape, b.shape
    assert K == K2, (a.shape, b.shape, mode)
    tm, tn, tk = min(tm, M), min(tn, N), min(tk, K)
    assert M % tm == 0 and N % tn == 0 and K % tk == 0, (M, N, K, tm, tn, tk)
    nk = K // tk
    if mode == "tn":
        a_spec = pl.BlockSpec((tk, tm), lambda i, j, k: (k, i))
        ca = 0
    else:
        a_spec = pl.BlockSpec((tm, tk), lambda i, j, k: (i, k))
        ca = 1
    if mode == "nt":
        b_spec = pl.BlockSpec((tn, tk), lambda i, j, k: (j, k))
        cb = 1
    else:
        b_spec = pl.BlockSpec((tk, tn), lambda i, j, k: (k, j))
        cb = 0
    dims = (((ca,), (cb,)), ((), ()))
    ne = len(extras)
    out_dtypes = out_dtype if isinstance(out_dtype, tuple) else (out_dtype,)
    no = len(out_dtypes)
    extra_cols = extra_cols or (0,) * ne

    def body(a_ref, b_ref, *rest):
        ex = rest[:ne]
        o_refs = rest[ne:ne + no]
        av = a_ref[...]
        if a_fn is not None:
            av = a_fn(av.astype(F32))
        p = lax.dot_general(av.astype(BF16), b_ref[...].astype(BF16), dims, preferred_element_type=F32)

        def fin(v):
            if epi is not None:
                v = epi(v, *[e[...] for e in ex])
            for o_ref, val in zip(o_refs, v if no > 1 else (v,)):
                o_ref[...] = val.astype(o_ref.dtype).reshape(o_ref.shape)

        if nk == 1:
            fin(p)
        else:
            acc = rest[ne + no]
            k = pl.program_id(2)

            @pl.when(k == 0)
            def _():
                acc[...] = p

            @pl.when(k > 0)
            def _():
                acc[...] += p

            @pl.when(k == nk - 1)
            def _():
                fin(acc[...])

    if shard_cols is None:
        o_spec, o_shape = pl.BlockSpec((tm, tn), lambda i, j, k: (i, j)), (M, N)
    else:
        per = shard_cols // tn
        assert shard_cols % tn == 0 and N % shard_cols == 0
        o_spec = pl.BlockSpec((1, tm, tn), lambda i, j, k: (lax.div(j, per), i, lax.rem(j, per)))
        o_shape = (N // shard_cols, M, shard_cols)
    ex_specs = [pl.BlockSpec((tm, tn), functools.partial(lambda i, j, k, off: (i, j + off), off=off)) for off in extra_cols]
    res = _call(name, body, (M // tm, N // tn, nk), [a_spec, b_spec] + ex_specs,
                [o_spec] * no, [SDS(o_shape, dt) for dt in out_dtypes],
                [pltpu.VMEM((tm, tn), F32)] if nk > 1 else [], ("parallel", "parallel", "arbitrary"),
                (a, b, *extras), comm)
    own = res[0] if no == 1 else tuple(res[:no])
    return own if comm is None else (own, res[no:])


def _rowwise(fn, rows, bcasts, outs, accs, *, tr, name):
    T = rows[0][0].shape[0]
    tr = min(tr, T)
    assert T % tr == 0
    nr, nb, no, na = len(rows), len(bcasts), len(outs), len(accs)
    in_specs = [pl.BlockSpec((tr, w), functools.partial(lambda i, c: (i, c), c=cb)) for (_, w, cb) in rows]
    in_specs += [pl.BlockSpec(b.shape, lambda i: (0, 0)) for b in bcasts]
    out_shape = [SDS((T, w), dt) for (w, dt) in outs] + [SDS(s, F32) for s in accs]
    out_specs = [pl.BlockSpec((tr, w), lambda i: (i, 0)) for (w, _) in outs]
    out_specs += [pl.BlockSpec(s, lambda i: (0, 0)) for s in accs]

    def body(*refs):
        ins = [r[...].astype(F32) for r in refs[:nr + nb]]
        o_refs = refs[nr + nb:nr + nb + no]
        a_refs = refs[nr + nb + no:]
        ro, ao = fn(*ins)
        for r, v in zip(o_refs, ro):
            r[...] = v.astype(r.dtype)
        if na:
            @pl.when(pl.program_id(0) == 0)
            def _():
                for r in a_refs:
                    r[...] = jnp.zeros(r.shape, F32)

            for r, v in zip(a_refs, ao):
                r[...] += v

    res = pl.pallas_call(
        body,
        name=name,
        grid=(T // tr,),
        in_specs=in_specs,
        out_specs=out_specs,
        out_shape=out_shape,
        compiler_params=_cp(("arbitrary",) if na else ("parallel",)),
    )(*[r[0] for r in rows], *bcasts)
    return res


def _rms(v):
    r = lax.rsqrt(jnp.mean(v * v, axis=-1, keepdims=True) + EPS)
    return v * r, r


def _rms_bwd(dy, xn, r, g):
    dxn = dy * g
    dv = r * (dxn - xn * jnp.mean(dxn * xn, axis=-1, keepdims=True))
    return dv, jnp.sum(dy * xn, axis=0, keepdims=True)


def _sig(v):
    return 1.0 / (1.0 + jnp.exp(-v))


_GELU_C = math.sqrt(2.0 / math.pi)


def _gelu(v):
    return 0.5 * v * (1.0 + jnp.tanh(_GELU_C * (v + 0.044715 * v * v * v)))


def _gelu_grad(v):
    t = jnp.tanh(_GELU_C * (v + 0.044715 * v * v * v))
    return 0.5 * (1.0 + t) + 0.5 * v * (1.0 - t * t) * _GELU_C * (1.0 + 3.0 * 0.044715 * v * v)


def _rope(v, c, s, sign):
    w = v.shape[1]
    m = lax.broadcasted_iota(jnp.int32, v.shape, 1) % HEAD_DIM
    p = jnp.where(m < ROT_DIM // 2, -pltpu.roll(v, w - ROT_DIM // 2, 1), pltpu.roll(v, ROT_DIM // 2, 1))
    return v * c + sign * (p * s)


def _rope_tables(T):
    half = ROT_DIM // 2
    inv = ROPE_THETA ** (-jnp.arange(half, dtype=F32) * 2.0 / ROT_DIM)
    ang = jnp.arange(T).astype(F32)[:, None] * inv[None, :]
    cos, sin = jnp.cos(ang), jnp.sin(ang)
    one = jnp.ones((T, HEAD_DIM - ROT_DIM), F32)
    c64 = jnp.concatenate([cos, cos, one], axis=1)
    s64 = jnp.concatenate([sin, sin, 0.0 * one], axis=1)
    return jnp.tile(c64, (1, 2)), jnp.tile(s64, (1, 2))


def _dup_half(m, lo):
    lane = lax.broadcasted_iota(jnp.int32, m.shape, 1)
    sw = pltpu.roll(m, HEAD_DIM, 1)
    return jnp.where(lane < HEAD_DIM, m, sw) if lo else jnp.where(lane >= HEAD_DIM, m, sw)


def _attn_mask(i):
    qi = lax.broadcasted_iota(jnp.int32, (ATT_BLOCK, 2 * ATT_BLOCK), 0)
    kj = lax.broadcasted_iota(jnp.int32, (ATT_BLOCK, 2 * ATT_BLOCK), 1)
    rel = qi + ATT_BLOCK - kj
    return (rel >= 0) & (rel < ATT_BLOCK) & ((kj >= ATT_BLOCK) | (i > 0))


_NT = (((1,), (1,)), ((), ()))
_TN = (((0,), (0,)), ((), ()))


def _stack_heads(m):
    lane = lax.broadcasted_iota(jnp.int32, m.shape, 1)
    zero = jnp.zeros_like(m)
    return jnp.concatenate([jnp.where(lane < HEAD_DIM, m, zero), jnp.where(lane >= HEAD_DIM, m, zero)], axis=0)


def _pair_probs(q2, k2, ok2, sink_lo, sink_hi):
    qs = _stack_heads(q2)
    s = lax.dot_general(qs, k2, _NT, preferred_element_type=F32)
    s = jnp.where(ok2, s, NEG)
    row = lax.broadcasted_iota(jnp.int32, (2 * ATT_BLOCK, 1), 0)
    sink = jnp.where(row < ATT_BLOCK, sink_lo, sink_hi)
    m = jnp.maximum(jnp.max(s, axis=1, keepdims=True), sink)
    e = jnp.exp(s - m)
    es = jnp.exp(sink - m)
    inv = 1.0 / (jnp.sum(e, axis=1, keepdims=True) + es)
    return e * inv, es * inv, qs


def _attn_fwd(za, cos, sin, sinks, comm=None):
    T = za.shape[0]
    nb = T // ATT_BLOCK
    kvb = Q_W // (2 * KV_W)

    def body(sink_ref, q_ref, kvp_ref, kvc_ref, cc_ref, sc_ref, cp_ref, sp_ref, o_ref):
        i = pl.program_id(0)
        cc, sc, cp, sp = cc_ref[...], sc_ref[...], cp_ref[...], sp_ref[...]
        q = (_rope(q_ref[...], jnp.tile(cc, (1, 8)), jnp.tile(sc, (1, 8)), 1.0) * 0.125).astype(BF16)
        kvp, kvc = kvp_ref[...], kvc_ref[...]
        k = jnp.concatenate([_rope(kvp[:, :KV_W], cp, sp, 1.0), _rope(kvc[:, :KV_W], cc, sc, 1.0)], axis=0).astype(BF16)
        v = jnp.concatenate([kvp[:, KV_W:], kvc[:, KV_W:]], axis=0).astype(BF16)
        ok = _attn_mask(i)
        ok2 = jnp.concatenate([ok, ok], axis=0)
        lane = lax.broadcasted_iota(jnp.int32, (ATT_BLOCK, LANES), 1)
        for kvh in range(2):
            k2 = _dup_half(k, kvh == 0)
            v2 = _dup_half(v, kvh == 0)
            for pair in range(4):
                c0 = (kvh * 4 + pair) * LANES
                q2 = q[:, c0:c0 + LANES]
                p, _, _ = _pair_probs(q2, k2, ok2, sink_ref[0, 2 * (kvh * 4 + pair)], sink_ref[0, 2 * (kvh * 4 + pair) + 1])
                o = jnp.dot(p.astype(BF16), v2, preferred_element_type=F32)
                o_ref[:, c0:c0 + LANES] = jnp.where(lane < HEAD_DIM, o[:ATT_BLOCK], o[ATT_BLOCK:]).astype(BF16)

    blk = lambda w, f: pl.BlockSpec((ATT_BLOCK, w), f)
    res = _call(
        "attn_fwd", body, (nb,),
        [
            pl.BlockSpec(memory_space=pltpu.SMEM),
            blk(Q_W, lambda i: (i, 0)),
            blk(2 * KV_W, lambda i: (jnp.maximum(i - 1, 0), kvb)),
            blk(2 * KV_W, lambda i: (i, kvb)),
            blk(LANES, lambda i: (i, 0)),
            blk(LANES, lambda i: (i, 0)),
            blk(LANES, lambda i: (jnp.maximum(i - 1, 0), 0)),
            blk(LANES, lambda i: (jnp.maximum(i - 1, 0), 0)),
        ],
        [blk(Q_W, lambda i: (i, 0))], [SDS((T, Q_W), BF16)], [], ("parallel",),
        (sinks, za, za, za, cos, sin, cos, sin), comm)
    return res[0] if comm is None else (res[0], res[1:])


def _attn_bwd(za, cos, sin, sinks, o, do, comm=None):
    T = za.shape[0]
    nb = T // ATT_BLOCK
    kvb = Q_W // (2 * KV_W)

    def body(sink_ref, q_ref, kvp_ref, kvc_ref, cc_ref, sc_ref, cp_ref, sp_ref, o_ref, do_ref,
             dq_ref, dkv_ref, dsk_ref, carry, dqs):
        i = pl.program_id(0)

        @pl.when(i == 0)
        def _():
            carry[...] = jnp.zeros(carry.shape, F32)
            dsk_ref[...] = jnp.zeros(dsk_ref.shape, F32)

        @pl.when(i < nb)
        def _():
            cc, sc, cp, sp = cc_ref[...], sc_ref[...], cp_ref[...], sp_ref[...]
            ccq, scq = jnp.tile(cc, (1, 8)), jnp.tile(sc, (1, 8))
            q = (_rope(q_ref[...], ccq, scq, 1.0) * 0.125).astype(BF16)
            kvp, kvc = kvp_ref[...], kvc_ref[...]
            k = jnp.concatenate([_rope(kvp[:, :KV_W], cp, sp, 1.0), _rope(kvc[:, :KV_W], cc, sc, 1.0)], axis=0).astype(BF16)
            v = jnp.concatenate([kvp[:, KV_W:], kvc[:, KV_W:]], axis=0).astype(BF16)
            ok = _attn_mask(i)
            ok2 = jnp.concatenate([ok, ok], axis=0)
            lane = lax.broadcasted_iota(jnp.int32, (ATT_BLOCK, LANES), 1)
            lane_s = lax.broadcasted_iota(jnp.int32, (1, LANES), 1)
            dsk = jnp.zeros((1, LANES), F32)
            dkt_h, dvt_h = [], []
            for kvh in range(2):
                k2 = _dup_half(k, kvh == 0)
                v2 = _dup_half(v, kvh == 0)
                dkt = jnp.zeros((LANES, 2 * ATT_BLOCK), F32)
                dvt = jnp.zeros((LANES, 2 * ATT_BLOCK), F32)
                for pair in range(4):
                    h = 2 * (kvh * 4 + pair)
                    c0 = (kvh * 4 + pair) * LANES
                    do2 = do_ref[:, c0:c0 + LANES]
                    prod = do2.astype(F32) * o_ref[:, c0:c0 + LANES].astype(F32)
                    d_lo = jnp.sum(jnp.where(lane < HEAD_DIM, prod, 0.0), axis=1, keepdims=True)
                    d_hi = jnp.sum(jnp.where(lane >= HEAD_DIM, prod, 0.0), axis=1, keepdims=True)
                    delta = jnp.concatenate([d_lo, d_hi], axis=0)
                    p, p_sink, qs = _pair_probs(q[:, c0:c0 + LANES], k2, ok2, sink_ref[0, h], sink_ref[0, h + 1])
                    dos = _stack_heads(do2)
                    t = p_sink * delta
                    dsk = dsk - jnp.where(lane_s == h, jnp.sum(t[:ATT_BLOCK]), 0.0) \
                              - jnp.where(lane_s == h + 1, jnp.sum(t[ATT_BLOCK:]), 0.0)
                    dp = lax.dot_general(dos, v2, _NT, preferred_element_type=F32)
                    ds = (p * (dp - delta)).astype(BF16)
                    dqp = jnp.dot(ds, k2, preferred_element_type=F32)
                    dqs[:, c0:c0 + LANES] = jnp.where(lane < HEAD_DIM, dqp[:ATT_BLOCK], dqp[ATT_BLOCK:]) * 0.125
                    dkt = dkt + lax.dot_general(qs, ds, _TN, preferred_element_type=F32)
                    dvt = dvt + lax.dot_general(dos, p.astype(BF16), _TN, preferred_element_type=F32)
                dkt_h.append(dkt[:HEAD_DIM] + dkt[HEAD_DIM:])
                dvt_h.append(dvt[:HEAD_DIM] + dvt[HEAD_DIM:])
            dk = jnp.concatenate(dkt_h, axis=0).T
            dv = jnp.concatenate(dvt_h, axis=0).T
            dq_ref[...] = _rope(dqs[...], ccq, scq, -1.0).astype(dq_ref.dtype)
            dkp = _rope(dk[:ATT_BLOCK], cp, sp, -1.0)
            dkc = _rope(dk[ATT_BLOCK:], cc, sc, -1.0)
            dkv_ref[...] = (carry[...] + jnp.concatenate([dkp, dv[:ATT_BLOCK]], axis=1)).astype(dkv_ref.dtype)
            carry[...] = jnp.concatenate([dkc, dv[ATT_BLOCK:]], axis=1)
            dsk_ref[...] += dsk

        @pl.when(i == nb)
        def _():
            dkv_ref[...] = carry[...].astype(dkv_ref.dtype)

    blk = lambda w, f: pl.BlockSpec((ATT_BLOCK, w), f)
    cur = lambda i: jnp.minimum(i, nb - 1)
    prv = lambda i: jnp.maximum(jnp.minimum(i, nb - 1) - 1, 0)
    res = _call(
        "attn_bwd", body, (nb + 1,),
        [
            pl.BlockSpec(memory_space=pltpu.SMEM),
            blk(Q_W, lambda i: (cur(i), 0)),
            blk(2 * KV_W, lambda i: (prv(i), kvb)),
            blk(2 * KV_W, lambda i: (cur(i), kvb)),
            blk(LANES, lambda i: (cur(i), 0)),
            blk(LANES, lambda i: (cur(i), 0)),
            blk(LANES, lambda i: (prv(i), 0)),
            blk(LANES, lambda i: (prv(i), 0)),
            blk(Q_W, lambda i: (cur(i), 0)),
            blk(Q_W, lambda i: (cur(i), 0)),
        ],
        [
            blk(Q_W, lambda i: (cur(i), 0)),
            blk(2 * KV_W, lambda i: (jnp.maximum(i - 1, 0), 0)),
            pl.BlockSpec((1, LANES), lambda i: (0, 0)),
        ],
        [SDS((T, Q_W), BF16), SDS((T, 2 * KV_W), BF16), SDS((1, LANES), F32)],
        [pltpu.VMEM((ATT_BLOCK, 2 * KV_W), F32), pltpu.VMEM((ATT_BLOCK, Q_W), F32)],
        ("arbitrary",), (sinks, za, za, za, cos, sin, cos, sin, o, do), comm)
    return res if comm is None else (res[:3], res[3:])


def _s5_discretize(lam_re, lam_im, log_dt, b_re, b_im):
    dt = jnp.exp(log_dt)[:, None]
    mag = jnp.exp(lam_re * dt)
    a_re, a_im = mag * jnp.cos(lam_im * dt), mag * jnp.sin(lam_im * dt)
    den = lam_re * lam_re + lam_im * lam_im
    nr, ni = a_re - 1.0, a_im
    coef_re = (nr * lam_re + ni * lam_im) / den
    coef_im = (ni * lam_re - nr * lam_im) / den
    bb_re = coef_re[..., None] * b_re - coef_im[..., None] * b_im
    bb_im = coef_re[..., None] * b_im + coef_im[..., None] * b_re
    return a_re, a_im, bb_re, bb_im


def _blockdiag_in(bb):
    x = bb.reshape(N_JB, 8, SSM_P, SSM_GC).transpose(0, 1, 3, 2)
    return (x[:, :, :, None, :] * jnp.eye(8, dtype=bb.dtype)[None, :, None, :, None]).reshape(N_JB, 128, 512)


def _blockdiag_in_extract(m):
    x = m.reshape(N_JB, 8, SSM_GC, 8, SSM_P)
    x = jnp.einsum('jgchp,gh->jgcp', x, jnp.eye(8, dtype=m.dtype))
    return x.transpose(0, 1, 3, 2).reshape(SSM_G, SSM_P, SSM_GC)


def _blockdiag_out(c):
    x = c.reshape(N_JB, 8, SSM_GC, SSM_P).transpose(0, 1, 3, 2)
    return (x[:, :, :, None, :] * jnp.eye(8, dtype=c.dtype)[None, :, None, :, None]).reshape(N_JB, 512, 128)


def _blockdiag_out_extract(m):
    x = m.reshape(N_JB, 8, SSM_P, 8, SSM_GC)
    x = jnp.einsum('jgphc,gh->jgpc', x, jnp.eye(8, dtype=m.dtype))
    return x.transpose(0, 1, 3, 2).reshape(SSM_G, SSM_GC, SSM_P)


def _s5_tables(a_re, a_im):
    ar, ai = a_re.reshape(N_LG, 1, LANES), a_im.reshape(N_LG, 1, LANES)
    pr, pi, n = ar, ai, 1
    while n < S5_SEG:
        pr, pi, n = pr * pr - pi * pi, 2.0 * pr * pi, 2 * n
    assert n == S5_SEG
    bc = lambda v: jnp.broadcast_to(v, (N_LG, SUBLANES, LANES))
    return bc(ar), bc(ai), pr, pi


def _s5_to_time_major(src_ref, dst_ref):
    for t in range(S5_SEG):
        dst_ref[t * SUBLANES:(t + 1) * SUBLANES, :] = src_ref[pl.ds(t, SUBLANES, stride=S5_SEG), :]


def _s5_from_time_major(val, dst_ref):
    for t in range(S5_SEG):
        dst_ref[pl.ds(t, SUBLANES, stride=S5_SEG), :] = val[t * SUBLANES:(t + 1) * SUBLANES, :]


def _tm_rows(t, row0=0):
    return pl.ds(pl.multiple_of(t * SUBLANES + row0, SUBLANES), SUBLANES)


def _s5_scan(src_re, src_im, ar, ai, reverse, start=None, dst=None, dst_row0=0):
    def step(n, carry):
        t = (S5_SEG - 1 - n) if reverse else n
        out = []
        for ll in range(LG_PER_JB):
            xr, xi = carry[2 * ll], carry[2 * ll + 1]
            idx = (ll, _tm_rows(t), slice(None))
            nr = ar[ll] * xr - ai[ll] * xi + src_re[idx]
            ni = ar[ll] * xi + ai[ll] * xr + src_im[idx]
            if dst is not None:
                odx = (ll, _tm_rows(t, dst_row0), slice(None))
                dst[0][odx] = nr
                dst[1][odx] = ni
            out += [nr, ni]
        return tuple(out)
    if start is None:
        init = (jnp.zeros((SUBLANES, LANES), F32),) * (2 * LG_PER_JB)
    else:
        init = tuple(s[ll] for ll in range(LG_PER_JB) for s in start)
    return lax.fori_loop(0, S5_SEG, step, init)


def _s5_fixup(ends, in_re, in_im, mr, mi, s_re, s_im, reverse):
    cr, ci = in_re, in_im
    order = range(SUBLANES - 1, -1, -1) if reverse else range(SUBLANES)
    for s in order:
        s_re[:, s:s + 1, :] = cr
        s_im[:, s:s + 1, :] = ci
        er = jnp.stack([ends[2 * ll][s:s + 1, :] for ll in range(LG_PER_JB)])
        ei = jnp.stack([ends[2 * ll + 1][s:s + 1, :] for ll in range(LG_PER_JB)])
        cr, ci = mr * cr - mi * ci + er, mr * ci + mi * cr + ei
    return cr, ci


def _s5_specs(nc, rev):
    cidx = (lambda c: nc - 1 - c) if rev else (lambda c: c)
    jb = lambda shape: pl.BlockSpec(shape, lambda j, c: (j, 0, 0))
    return cidx, [
        jb((1, LANES, 8 * LANES)),
        jb((1, 8 * LANES, LANES)),
        pl.BlockSpec((1, LANES), lambda j, c: (0, j)),
        jb((LG_PER_JB, SUBLANES, LANES)), jb((LG_PER_JB, SUBLANES, LANES)),
        jb((LG_PER_JB, 1, LANES)), jb((LG_PER_JB, 1, LANES)),
    ]


def _s5_fwd(za, prm, comm=None):
    T = za.shape[0]
    R = S5_CHUNK
    nc = T // R
    ub = (Q_W + 2 * KV_W) // LANES
    _, pspecs = _s5_specs(nc, False)

    def body(u_ref, b_ref, c_ref, d_ref, are_ref, aim_ref, alr_ref, ali_ref,
             yg_ref, x0r_ref, x0i_ref, bur, bui, xsr, xsi, sr, si, xcr, xci, utm, ynat):
        c = pl.program_id(1)

        @pl.when(c == 0)
        def _():
            xcr[...] = jnp.zeros(xcr.shape, F32)
            xci[...] = jnp.zeros(xci.shape, F32)

        _s5_to_time_major(u_ref, utm)
        u = utm[...]
        ub16 = u.astype(BF16)
        bu = jnp.dot(ub16, b_ref[0].astype(BF16), preferred_element_type=F32)
        for ll in range(LG_PER_JB):
            bur[ll] = bu[:, ll * LANES:(ll + 1) * LANES]
            bui[ll] = bu[:, (LG_PER_JB + ll) * LANES:(LG_PER_JB + ll + 1) * LANES]
        ar = [are_ref[ll] for ll in range(LG_PER_JB)]
        ai = [aim_ref[ll] for ll in range(LG_PER_JB)]
        ends = _s5_scan(bur, bui, ar, ai, False)
        in_r, in_i = xcr[...], xci[...]
        x0r_ref[0] = in_r
        x0i_ref[0] = in_i
        out_r, out_i = _s5_fixup(ends, in_r, in_i, alr_ref[...], ali_ref[...], sr, si, False)
        xcr[...] = out_r
        xci[...] = out_i
        _s5_scan(bur, bui, ar, ai, False, start=(sr, si), dst=(xsr, xsi))
        xcat =jnp.concatenate([xsr[ll].astype(BF16) for ll in range(LG_PER_JB)]
                               + [xsi[ll].astype(BF16) for ll in range(LG_PER_JB)], axis=1)
        y = d_ref[...] * u + jnp.dot(xcat, c_ref[0].astype(BF16), preferred_element_type=F32)
        _s5_from_time_major(_gelu(y), ynat)
        yg_ref[...] = ynat[...].astype(BF16)

    st = pl.BlockSpec((1, LG_PER_JB, 1, LANES), lambda j, c: (c, j, 0, 0))
    vm = lambda rows: pltpu.VMEM((LG_PER_JB, rows, LANES), F32)
    res = _call(
        "s5_fwd", body, (N_JB, nc),
        [pl.BlockSpec((R, LANES), lambda j, c: (c, ub + j))] + pspecs,
        [pl.BlockSpec((R, LANES), lambda j, c: (c, j)), st, st],
        [SDS((T, SSM_W), BF16), SDS((nc, N_LG, 1, LANES), F32), SDS((nc, N_LG, 1, LANES), F32)],
        [vm(R), vm(R), vm(R), vm(R), vm(SUBLANES), vm(SUBLANES), vm(1), vm(1),
         pltpu.VMEM((R, LANES), F32), pltpu.VMEM((R, LANES), F32)],
        ("parallel", "arbitrary"), (za, *prm), comm)
    return res if comm is None else (res[:3], res[3:])


def _s5_bwd(za, dyg, x0r, x0i, prm, comm=None):
    T = za.shape[0]
    R = S5_CHUNK
    nc = T // R
    ub = (Q_W + 2 * KV_W) // LANES
    cidx, pspecs = _s5_specs(nc, True)
    PAD = SUBLANES

    def body(u_ref, dyg_ref, x0r_ref, x0i_ref, b_ref, c_ref, d_ref, are_ref, aim_ref,
             alr_ref, ali_ref,
             du_ref, dar_ref, dai_ref, db_ref, dc_ref, dd_ref,
             bur, bui, xsr, xsi, sr, si, gcr, gci, utm, dtm, dunat):
        c = pl.program_id(1)

        @pl.when(c == 0)
        def _():
            gcr[...] = jnp.zeros(gcr.shape, F32)
            gci[...] = jnp.zeros(gci.shape, F32)
            dar_ref[...] = jnp.zeros(dar_ref.shape, F32)
            dai_ref[...] = jnp.zeros(dai_ref.shape, F32)
            db_ref[...] = jnp.zeros(db_ref.shape, F32)
            dc_ref[...] = jnp.zeros(dc_ref.shape, F32)
            dd_ref[...] = jnp.zeros(dd_ref.shape, F32)

        _s5_to_time_major(u_ref, utm)
        _s5_to_time_major(dyg_ref, dtm)
        u = utm[...]
        ub16 = u.astype(BF16)
        bcat, ccat = b_ref[0].astype(BF16), c_ref[0].astype(BF16)
        lanes = lambda v, ll: v[:, ll * LANES:(ll + 1) * LANES]
        bu = jnp.dot(ub16, bcat, preferred_element_type=F32)
        for ll in range(LG_PER_JB):
            bur[ll] = lanes(bu, ll)
            bui[ll] = lanes(bu, LG_PER_JB + ll)
        ar = [are_ref[ll] for ll in range(LG_PER_JB)]
        ai = [aim_ref[ll] for ll in range(LG_PER_JB)]
        ends = _s5_scan(bur, bui, ar, ai, False)
        in_r, in_i = x0r_ref[0], x0i_ref[0]
        _s5_fixup(ends, in_r, in_i, alr_ref[...], ali_ref[...], sr, si, False)
        _s5_scan(bur, bui, ar, ai, False, start=(sr, si), dst=(xsr, xsi), dst_row0=PAD)
        xsr[:, 0:PAD, :] = sr[...]
        xsi[:, 0:PAD, :] = si[...]
        xcat = jnp.concatenate([xsr[ll, PAD:, :].astype(BF16) for ll in range(LG_PER_JB)]
                               + [xsi[ll, PAD:, :].astype(BF16) for ll in range(LG_PER_JB)], axis=1)
        y = d_ref[...] * u + jnp.dot(xcat, ccat, preferred_element_type=F32)
        dy = dtm[...] * _gelu_grad(y)
        dyb = dy.astype(BF16)
        dd_ref[...] += jnp.sum(dy * u, axis=0, keepdims=True)
        du = d_ref[...] * dy
        dc_ref[0] += lax.dot_general(dyb, xcat, _TN, preferred_element_type=F32)
        g = lax.dot_general(dyb, ccat, _NT, preferred_element_type=F32)
        for ll in range(LG_PER_JB):
            bur[ll] = lanes(g, ll)
            bui[ll] = lanes(g, LG_PER_JB + ll)
        aic = [-v for v in ai]
        ends = _s5_scan(bur, bui, ar, aic, True)
        out_r, out_i = _s5_fixup(ends, gcr[...], gci[...], alr_ref[...], -ali_ref[...], sr, si, True)
        gcr[...] = out_r
        gci[...] = out_i
        _s5_scan(bur, bui, ar, aic, True, start=(sr, si), dst=(bur, bui))
        for ll in range(LG_PER_JB):
            gr, gi = bur[ll], bui[ll]
            xpr, xpi = xsr[ll, 0:R, :], xsi[ll, 0:R, :]
            red = lambda v: v.reshape(R // SUBLANES, SUBLANES, LANES).sum(axis=0)
            dar_ref[ll] += red(xpr * gr + xpi * gi)
            dai_ref[ll] += red(xpr * gi - xpi * gr)
        gcat = jnp.concatenate([bur[ll].astype(BF16) for ll in range(LG_PER_JB)]
                               + [bui[ll].astype(BF16) for ll in range(LG_PER_JB)], axis=1)
        db_ref[0] += lax.dot_general(ub16, gcat, _TN, preferred_element_type=F32)
        du = du + lax.dot_general(gcat, bcat, _NT, preferred_element_type=F32)
        _s5_from_time_major(du, dunat)
        du_ref[...] = dunat[...].astype(du_ref.dtype)

    st = pl.BlockSpec((1, LG_PER_JB, 1, LANES), lambda j, c: (cidx(c), j, 0, 0))
    jb = lambda shape: pl.BlockSpec(shape, lambda j, c: (j, 0, 0))
    vm = lambda rows: pltpu.VMEM((LG_PER_JB, rows, LANES), F32)
    res = _call(
        "s5_bwd", body, (N_JB, nc),
        [pl.BlockSpec((R, LANES), lambda j, c: (cidx(c), ub + j)),
         pl.BlockSpec((R, LANES), lambda j, c: (cidx(c), j)), st, st] + pspecs,
        [pl.BlockSpec((R, LANES), lambda j, c: (cidx(c), j)),
         jb((LG_PER_JB, SUBLANES, LANES)), jb((LG_PER_JB, SUBLANES, LANES)),
         jb((1, LANES, 8 * LANES)), jb((1, LANES, 8 * LANES)),
         pl.BlockSpec((1, LANES), lambda j, c: (0, j))],
        [SDS((T, SSM_W), BF16), SDS((N_LG, SUBLANES, LANES), F32), SDS((N_LG, SUBLANES, LANES), F32),
         SDS((N_JB, LANES, 8 * LANES), F32), SDS((N_JB, LANES, 8 * LANES), F32), SDS((1, SSM_W), F32)],
        [vm(R), vm(R), vm(R + PAD), vm(R + PAD), vm(SUBLANES), vm(SUBLANES), vm(1), vm(1)]
        + [pltpu.VMEM((R, LANES), F32)] * 3,
        ("parallel", "arbitrary"), (za, dyg, x0r, x0i, *prm), comm)
    return res if comm is None else (res[:6], res[6:])


def _assemble_w_a(wi):
    _, rows, cb = wi.shape
    tr = 256

    def body(w_ref, a_ref):
        a_ref[:, :cb] = w_ref[0]
        a_ref[:, cb:] = w_ref[1, :, :ZA_W - cb]

    return pl.pallas_call(
        body, name="assemble_w_a", grid=(rows // tr,),
        in_specs=[pl.BlockSpec((2, tr, cb), lambda i: (0, i, 0))],
        out_specs=pl.BlockSpec((tr, ZA_W), lambda i: (i, 0)),
        out_shape=SDS((rows, ZA_W), wi.dtype), compiler_params=_cp(("parallel",)))(wi)


def _assemble_w_g(wi):
    _, rows, cb = wi.shape
    tr = 256
    cut = ZA_W - cb

    def body(w_ref, g_ref):
        g_ref[:, :cb - cut] = w_ref[1, :, cut:]
        g_ref[:, cb - cut:2 * cb - cut] = w_ref[2]
        g_ref[:, 2 * cb - cut:] = w_ref[3]

    return pl.pallas_call(
        body, name="assemble_w_g", grid=(rows // tr,),
        in_specs=[pl.BlockSpec((4, tr, cb), lambda i: (0, i, 0))],
        out_specs=pl.BlockSpec((tr, 4 * cb - ZA_W), lambda i: (i, 0)),
        out_shape=SDS((rows, 4 * cb - ZA_W), wi.dtype), compiler_params=_cp(("parallel",)))(wi)


def _stack_w_in_grad(d_w_a, d_w_g):
    rows = d_w_a.shape[0]
    cb = (ZA_W + d_w_g.shape[1]) // 4
    cut = ZA_W - cb
    tr = 256

    def body(a_ref, g_ref, o_ref):
        o_ref[0] = a_ref[:, :cb]
        o_ref[1, :, :cut] = a_ref[:, cb:]
        o_ref[1, :, cut:] = g_ref[:, :cb - cut]
        o_ref[2] = g_ref[:, cb - cut:2 * cb - cut]
        o_ref[3] = g_ref[:, 2 * cb - cut:]

    return pl.pallas_call(
        body, name="stack_w_in_grad", grid=(rows // tr,),
        in_specs=[pl.BlockSpec((tr, ZA_W), lambda i: (i, 0)), pl.BlockSpec((tr, d_w_g.shape[1]), lambda i: (i, 0))],
        out_specs=pl.BlockSpec((4, tr, cb), lambda i: (0, i, 0)),
        out_shape=SDS((4, rows, cb), d_w_a.dtype), compiler_params=_cp(("parallel",)))(d_w_a, d_w_g)


def _local_step(x, target, gains, w_a_of, sinks, s5w, comms, late_g, late, red=None):
    T = x.shape[0]
    D = D_MODEL
    g1, g2, g3, g4 = gains
    cos, sin = _rope_tables(T)
    lam_re, lam_im, log_dt, b_re, b_im, c_re, c_im, d_skip = s5w
    (a_re, a_im, bb_re, bb_im), disc_vjp = jax.vjp(_s5_discretize, lam_re, lam_im, log_dt, b_re, b_im)
    abr, abi, al_re, al_im = _s5_tables(a_re, a_im)
    prm = (jnp.concatenate([_blockdiag_in(bb_re), _blockdiag_in(bb_im)], axis=2),
           jnp.concatenate([_blockdiag_out(c_re), -_blockdiag_out(c_im)], axis=1),
           d_skip.reshape(1, SSM_W), abr, abi, al_re, al_im)
    mm = functools.partial(_mm, tm=1024, tn=1024, tk=2048)

    h = _rowwise(lambda xv, g: ((_rms(xv)[0] * g,), ()), [(x, D, 0)], [g1], [(D, BF16)], [], tr=512, name="norm1")[0]
    unpack = lambda res, comm: (res, ()) if comm is None else res
    w_a = w_a_of(h, *prm, cos, sin)
    za, got_a = unpack(_mm(h, w_a, mode="nn", out_dtype=F32, tm=1024, tn=1152, tk=2048, name="mm_za", comm=comms[0]), comms[0])
    w_g = late_g(got_a)
    zg, got0 = unpack(mm(h, w_g, mode="nn", out_dtype=BF16, name="mm_zg", tn=2048, comm=comms[1]), comms[1])
    o_attn, got1 = unpack(_attn_fwd(za, cos, sin, sinks, comm=comms[2]), comms[2])
    (yg, x0r, x0i), got2 = unpack(_s5_fwd(za, prm, comm=comms[3]), comms[3])
    w_glu, w_ba, w_bs, w_out, w_up, w_down = late(got0, got1, got2)
    zglu = mm(yg, w_glu, mode="nn", out_dtype=BF16, name="mm_glu")
    o_ssm = _rowwise(lambda z1, z2: ((z1 * _sig(z2),), ()), [(zglu, SSM_W, 0), (zglu, SSM_W, 1)], [],
                     [(SSM_W, BF16)], [], tr=512, name="glu")[0]
    ya = mm(o_attn, w_ba, mode="nn", out_dtype=BF16, name="mm_ya")
    ys, mi = mm(o_ssm, w_bs, mode="nn", out_dtype=(BF16, BF16), name="mm_ys", extras=(zg, zg, ya),
                extra_cols=(0, D // 1024, 0),
                epi=lambda v, ga, gs, a: (v, _sig(ga.astype(F32)) * a.astype(F32) + _sig(gs.astype(F32)) * v))
    mixed = mm(mi, w_out, mode="nn", out_dtype=F32, name="mm_out")

    def f_post(xv, mv, g2v, g3v):
        x1v = xv + _rms(mv)[0] * g2v
        return (x1v, _rms(x1v)[0] * g3v), ()
    x1, h2 = _rowwise(f_post, [(x, D, 0), (mixed, D, 0)], [g2, g3], [(D, F32), (D, BF16)], [], tr=256, name="post_mix")
    act = mm(h2, w_up, mode="nn", out_dtype=BF16, name="mm_up", tn=2048, epi=lambda v: jnp.maximum(v, 0.0))
    f = mm(act, w_down, mode="nn", out_dtype=F32, name="mm_down", a_fn=lambda v: v * v, tk=4096)

    def f_final(x1v, fv, tv, g4v):
        fn, r = _rms(fv)
        e = x1v + fn * g4v - tv
        dx2v = e * (1.0 / D)
        dfv, dg4v = _rms_bwd(dx2v, fn, r, g4v)
        return (dfv, dx2v), (dg4v, jnp.zeros((SUBLANES, LANES), F32) + 0.5 * jnp.sum(e * e) * (1.0 / D))
    df, dx2, dg4, lossb = _rowwise(f_final, [(x1, D, 0), (f, D, 0), (target, D, 0)], [g4],
                                   [(D, BF16), (D, F32)], [(1, D), (SUBLANES, LANES)], tr=256, name="final")

    big = {}

    def add(k, g4):
        big[k] = g4
        if red is not None:
            red.add(k, g4)

    def hosted(fn, stage, names):
        if red is None:
            return fn(comm=None)
        out, got = fn(comm=getattr(red, stage)(names))
        getattr(red, stage + "_done")(names, got)
        return out

    dpre = mm(df, w_down, mode="nt", out_dtype=BF16, name="mm_dact", tn=2048,
              epi=lambda v, a: v * (2.0 * a.astype(F32)), extras=(act,))
    wg = functools.partial(_mm, mode="tn", out_dtype=F32, tm=1024, tn=1024, tk=4096)
    add("w_down", wg(act, df, name="wg_down", a_fn=lambda v: v * v).reshape(4, D_FF // 4, D))
    dh2 = hosted(functools.partial(mm, dpre, w_up, mode="nt", out_dtype=BF16, name="mm_dh2", tk=4096),
                 "s1", ["w_down"])
    add("w_up", hosted(functools.partial(wg, h2, dpre, name="wg_up", shard_cols=D_FF // 4), "s3", ["w_down"]))

    def f_mid(dx2v, dh2v, x1v, mv, g2v, g3v):
        x1n, r3 = _rms(x1v)
        d3, dg3v = _rms_bwd(dh2v, x1n, r3, g3v)
        dx1v = dx2v + d3
        mn, r2 = _rms(mv)
        dmv, dg2v = _rms_bwd(dx1v, mn, r2, g2v)
        return (dx1v, dmv), (dg3v, dg2v)
    dx1, dmixed, dg3, dg2 = _rowwise(f_mid, [(dx2, D, 0), (dh2, D, 0), (x1, D, 0), (mixed, D, 0)], [g2, g3],
                                     [(D, F32), (D, BF16)], [(1, D), (1, D)], tr=256, name="mid")

    dmi = hosted(functools.partial(mm, dmixed, w_out, mode="nt", out_dtype=BF16, name="mm_dmi"), "s1", ["w_up"])
    add("w_out", wg(mi, dmixed, name="wg_out").reshape(4, D // 4, D))

    def f_gate(dv, ga, gs, a, s):
        sa, ss = _sig(ga), _sig(gs)
        return (dv * sa, dv * ss, jnp.concatenate([dv * a * sa * (1.0 - sa), dv * s * ss * (1.0 - ss)], axis=1)), ()
    dya, dys, dzg = _rowwise(f_gate, [(dmi, D, 0), (zg, D, 0), (zg, D, 1), (ya, D, 0), (ys, D, 0)], [],
                             [(D, BF16), (D, BF16), (2 * D, BF16)], [], tr=256, name="gate_bwd")
    do_attn = hosted(functools.partial(mm, dya, w_ba, mode="nt", out_dtype=BF16, name="mm_doa"), "s1", ["w_out"])
    d_w_ba = wg(o_attn, dya, name="wg_ba")
    do_ssm = mm(dys, w_bs, mode="nt", out_dtype=BF16, name="mm_dos")
    d_w_bs = wg(o_ssm, dys, name="wg_bs")
    add("w_branch", jnp.concatenate([d_w_ba.reshape(2, D // 4, D), d_w_bs.reshape(2, D // 4, D)], axis=0))

    def f_glu(dv, z1, z2):
        s2 = _sig(z2)
        return (jnp.concatenate([dv * s2, dv * z1 * s2 * (1.0 - s2)], axis=1),), ()
    dzglu = _rowwise(f_glu, [(do_ssm, SSM_W, 0), (zglu, SSM_W, 0), (zglu, SSM_W, 1)], [], [(2 * SSM_W, BF16)], [],
                     tr=512, name="glu_bwd")[0]
    dyg = hosted(functools.partial(mm, dzglu, w_glu, mode="nt", out_dtype=F32, name="mm_dyg"), "s1", ["w_branch"])
    add("w_glu", wg(yg, dzglu, name="wg_glu", tn=SSM_W // 2, shard_cols=SSM_W // 2))
    du, dar, dai, dbc, dcc, ddv = hosted(functools.partial(_s5_bwd, za, dyg, x0r, x0i, prm),
                                         "s3", ["w_up", "w_out", "w_branch"])
    dbr, dbi = dbc[:, :, :4 * LANES], dbc[:, :, 4 * LANES:]
    dcc = dcc.transpose(0, 2, 1)
    dcr, dci = dcc[:, :4 * LANES, :], -dcc[:, 4 * LANES:, :]
    dq, dkv, dsk = hosted(functools.partial(_attn_bwd, za, cos, sin, sinks, o_attn, do_attn),
                          "s5", ["w_down", "w_up", "w_out", "w_branch"])
    dza = jnp.concatenate([dq, dkv, du], axis=1)
    d_w_a = _mm(h, dza, mode="tn", out_dtype=F32, tm=1024, tn=ZA_W // 2, tk=2048, name="wg_a")
    d_w_g = wg(h, dzg, name="wg_g")
    add("w_in", _stack_w_in_grad(d_w_a, d_w_g))
    dh = hosted(functools.partial(mm, dza, w_a, mode="nt", out_dtype=F32, name="mm_dh_a", tk=ZA_W), "s1", ["w_in", "w_glu"])
    dh = hosted(functools.partial(mm, dzg, w_g, mode="nt", out_dtype=BF16, name="mm_dh_g",
                                  epi=lambda v, p: v + p, extras=(dh,)), "s3", ["w_in", "w_glu"])

    def f_first(dx1v, dhv, xv, g1v):
        xn, r1 = _rms(xv)
        d1, dg1v = _rms_bwd(dhv, xn, r1, g1v)
        return (dx1v + d1,), (dg1v,)
    dx, dg1 = _rowwise(f_first, [(dx1, D, 0), (dh, D, 0), (x, D, 0)], [g1], [(D, F32)], [(1, D)], tr=256, name="first")

    da_re = dar.sum(axis=1).reshape(SSM_G, SSM_P)
    da_im = dai.sum(axis=1).reshape(SSM_G, SSM_P)
    d_lam_re, d_lam_im, d_log_dt, d_b_re, d_b_im = disc_vjp(
        (da_re, da_im, _blockdiag_in_extract(dbr), _blockdiag_in_extract(dbi)))
    small = dict(norm_mix_pre=dg1, norm_mix_post=dg2, norm_mlp_pre=dg3, norm_mlp_post=dg4,
                 sinks=dsk[:, :N_Q_HEADS], lam_re=d_lam_re, lam_im=d_lam_im, log_dt=d_log_dt,
                 b_re=d_b_re, b_im=d_b_im, c_re=_blockdiag_out_extract(dcr), c_im=_blockdiag_out_extract(dci),
                 d_skip=ddv.reshape(SSM_G, SSM_GC))
    return lossb[0, 0], dx, small, big


def _cast_into_slot(w, k_arr):
    rows, cols = w.shape
    tr = 256

    def body(k_ref, w_ref, o_ref):
        o_ref[0] = w_ref[...].astype(BF16)

    return pl.pallas_call(
        body,
        name="cast_into_slot",
        grid_spec=pltpu.PrefetchScalarGridSpec(
            num_scalar_prefetch=1,
            grid=(rows // tr,),
            in_specs=[pl.BlockSpec((tr, cols), lambda i, k: (i, 0))],
            out_specs=pl.BlockSpec((1, tr, cols), lambda i, k: (k[0], i, 0)),
        ),
        out_shape=SDS((4, rows, cols), BF16),
        compiler_params=_cp(("parallel",)),
    )(k_arr, w)


def _pair_sum(g, r, c_arr):
    _, _, hr, cols = g.shape
    tr = min(256, hr)

    def body(c_ref, g_ref, r_ref, o_ref):
        o_ref[0] = (g_ref[0, 0] + r_ref[0]).astype(BF16)

    return pl.pallas_call(
        body,
        name="pair_sum",
        grid_spec=pltpu.PrefetchScalarGridSpec(
            num_scalar_prefetch=1,
            grid=(3, hr // tr),
            in_specs=[pl.BlockSpec((1, 1, tr, cols), lambda k, i, c_ref: (c_ref[1 + k], c_ref[0], i, 0)),
                      pl.BlockSpec((1, tr, cols), lambda k, i, c_ref: (c_ref[1 + k], i, 0))],
            out_specs=pl.BlockSpec((1, tr, cols), lambda k, i, c_ref: (c_ref[1 + k], i, 0)),
        ),
        out_shape=SDS((4, hr, cols), BF16),
        compiler_params=_cp(("parallel", "parallel")),
    )(c_arr, g, r)


def _chip_sum(g, r, q, kc_arr):
    _, _, hr, cols = g.shape
    tr = min(256, hr)

    def body(kc_ref, g_ref, r_ref, q_ref, o_ref):
        s = g_ref[0, 0] + r_ref[0]
        for j in range(3):
            s = s + q_ref[j].astype(F32)
        o_ref[...] = s

    return pl.pallas_call(
        body,
        name="chip_sum",
        grid_spec=pltpu.PrefetchScalarGridSpec(
            num_scalar_prefetch=1,
            grid=(hr // tr,),
            in_specs=[pl.BlockSpec((1, 1, tr, cols), lambda i, kc: (kc[0], kc[1], i, 0)),
                      pl.BlockSpec((1, tr, cols), lambda i, kc: (kc[0], i, 0)),
                      pl.BlockSpec((3, tr, cols), lambda i, kc: (0, i, 0))],
            out_specs=pl.BlockSpec((tr, cols), lambda i, kc: (kc[1] * (hr // tr) + i, 0)),
        ),
        out_shape=SDS((2 * hr, cols), F32),
        compiler_params=_cp(("parallel",)),
    )(kc_arr, g, r, q)


class _PairShareComm:
    aliased = True

    def __init__(self, blocks):
        self.arrs = list(blocks)
        self.n = len(self.arrs)
        dma = pltpu.SemaphoreType.DMA
        self.scratch = [dma((self.n,)), dma((self.n,))]
        self.out_shape = [SDS(b.shape, b.dtype) for b in self.arrs]

    def _copies(self, ins, outs, sems, hc):
        ssem, rsem = sems
        x, y, c, _ = _place()
        cps = []
        for w in range(self.n):
            hr = ins[w].shape[0] // 2
            rows = pl.ds(pl.multiple_of((c if hc == 0 else 1 - c) * hr, 8), hr)
            cps.append(_remote(ins[w].at[rows, :], outs[w].at[rows, :], ssem.at[w], rsem.at[w], (x, y, 1 - c)))
        return cps

    def start(self, ins, outs, sems):
        for cp in self._copies(ins, outs, sems, 0):
            cp.start()

    def finish(self, ins, outs, sems):
        for cp in self._copies(ins, outs, sems, 1):
            cp.wait_recv()
        for cp in self._copies(ins, outs, sems, 0):
            cp.wait_send()


class _GradReducer:
    def __init__(self, c_arr, kc_arr):
        self.c_arr, self.kc_arr = c_arr, kc_arr
        self.g, self.r, self.ps, self.q, self.done = {}, {}, {}, {}, {}

    def add(self, k, g4):
        self.g[k] = g4.reshape(4, 2, g4.shape[1] // 2, g4.shape[2])

    def s1(self, names):
        return _PairExchangeComm([self.g[k].reshape(4, -1, self.g[k].shape[3]) for k in names])

    def s1_done(self, names, got):
        for k, r in zip(names, got):
            self.r[k] = r
            self.ps[k] = _pair_sum(self.g[k], r, self.c_arr)

    def s3(self, names):
        return _ChipExchangeComm([self.ps[k] for k in names])

    def s3_done(self, names, got):
        self.q.update(zip(names, got))

    def finish(self, order):
        rest = [k for k in order if k not in self.r]
        if rest:
            self.s1_done(rest, _comm_only("pair_exchange", self.s1(rest)))
        rest = [k for k in order if k not in self.q]
        if rest:
            self.s3_done(rest, _comm_only("chip_exchange", self.s3(rest)))
        rest = [k for k in order if k not in self.done]
        if rest:
            self.s5_done(rest, _comm_only("pair_share", self.s5(rest)))
        return self.done

    def s5(self, names):
        return _PairShareComm([_chip_sum(self.g[k], self.r[k], self.q[k], self.kc_arr) for k in names])

    def s5_done(self, names, got):
        self.done.update(zip(names, got))


def _all_reduce_small(buf):
    rows = buf.shape[0]
    hr = rows // 2
    assert hr % SUBLANES == 0

    def body(in_ref, o_ref, sib, pair, slots, ssem, rsem):
        x, y, c, others = _place()
        me, sibling = 2 * x + y, (x, y, 1 - c)
        mine = pl.ds(pl.multiple_of(c * hr, SUBLANES), hr)
        theirs = pl.ds(pl.multiple_of((1 - c) * hr, SUBLANES), hr)
        first = _remote(in_ref, sib, ssem.at[0], rsem.at[0], sibling)
        first.start()
        first.wait()
        pair[...] = in_ref[...] + sib[...]
        slots[me] = pair[mine, :]
        cps = [_remote(pair.at[mine, :], slots.at[me], ssem.at[1 + r], rsem.at[1 + r], (ox, oy, c))
               for r, (ox, oy) in enumerate(others)]
        for cp in cps:
            cp.start()
        for r, (ox, oy) in enumerate(others):
            _remote(pair.at[mine, :], slots.at[2 * ox + oy], ssem.at[1 + r], rsem.at[1 + r], (ox, oy, c)).wait_recv()
        o_ref[mine, :] = (slots[0] + slots[1]) + (slots[2] + slots[3])
        last = _remote(o_ref.at[mine, :], o_ref.at[mine, :], ssem.at[4], rsem.at[4], sibling)
        last.start()
        _remote(o_ref.at[theirs, :], o_ref.at[theirs, :], ssem.at[4], rsem.at[4], sibling).wait_recv()
        last.wait_send()
        for cp in cps:
            cp.wait_send()

    dma = pltpu.SemaphoreType.DMA
    return pl.pallas_call(
        body,
        name="all_reduce_small",
        in_specs=[pl.BlockSpec(memory_space=pltpu.VMEM)],
        out_specs=pl.BlockSpec(memory_space=pltpu.VMEM),
        out_shape=SDS(buf.shape, F32),
        scratch_shapes=[pltpu.VMEM((rows, LANES), F32), pltpu.VMEM((rows, LANES), F32),
                        pltpu.VMEM((4, hr, LANES), F32), dma((5,)), dma((5,))],
        compiler_params=pltpu.CompilerParams(vmem_limit_bytes=VMEM_LIMIT),
    )(buf)


def _adam_fn(w, g, m, v):
    m2 = ADAM_B1 * m + (1.0 - ADAM_B1) * g
    v2 = ADAM_B2 * v + (1.0 - ADAM_B2) * (g * g)
    m_hat = m2 / (1.0 - ADAM_B1 ** ADAM_STEP)
    v_hat = v2 / (1.0 - ADAM_B2 ** ADAM_STEP)
    return (-ADAM_LR * (m_hat / (jnp.sqrt(v_hat) + ADAM_EPS) + ADAM_WD * w), m2, v2), ()


def _adamw(w, g, m, v, name, tr=256):
    cols = w.shape[1]
    return _rowwise(_adam_fn, [(w, cols, 0), (g, cols, 0), (m, cols, 0), (v, cols, 0)], [],
                    [(cols, F32)] * 3, [], tr=tr, name=name)


BIG = ("w_in", "w_glu", "w_branch", "w_out", "w_up", "w_down")
COL_SHARDED = ("w_in", "w_glu", "w_up")
SMALL = ("norm_mix_pre", "norm_mix_post", "norm_mlp_pre", "norm_mlp_post", "sinks", "lam_re", "lam_im", "log_dt",
         "b_re", "b_im", "c_re", "c_im", "d_skip")
WEIGHTS = ("norm_mix_pre", "norm_mix_post", "norm_mlp_pre", "norm_mlp_post", "w_in", "sinks", "lam_re", "lam_im",
           "log_dt", "b_re", "b_im", "c_re", "c_im", "d_skip", "w_glu", "w_branch", "w_out", "w_up", "w_down")


def _flat_small(vals, extra):
    flat = jnp.concatenate([vals[k].reshape(-1) for k in SMALL] + [extra.reshape(-1)])
    rows = -(-flat.shape[0] // (SUBLANES * LANES)) * SUBLANES
    return jnp.pad(flat, (0, rows * LANES - flat.shape[0])).reshape(rows, LANES)


def kernel(x, norm_mix_pre, norm_mix_post, norm_mlp_pre, norm_mlp_post, w_in, sinks, lam_re, lam_im, log_dt, b_re, b_im, c_re, c_im, d_skip, w_glu, w_branch, w_out, w_up, w_down, loss_target, m_norm_mix_pre, m_norm_mix_post, m_norm_mlp_pre, m_norm_mlp_post, m_w_in, m_sinks, m_lam_re, m_lam_im, m_log_dt, m_b_re, m_b_im, m_c_re, m_c_im, m_d_skip, m_w_glu, m_w_branch, m_w_out, m_w_up, m_w_down, v_norm_mix_pre, v_norm_mix_post, v_norm_mlp_pre, v_norm_mlp_post, v_w_in, v_sinks, v_lam_re, v_lam_im, v_log_dt, v_b_re, v_b_im, v_c_re, v_c_im, v_d_skip, v_w_glu, v_w_branch, v_w_out, v_w_up, v_w_down):
    w = dict(norm_mix_pre=norm_mix_pre, norm_mix_post=norm_mix_post, norm_mlp_pre=norm_mlp_pre, norm_mlp_post=norm_mlp_post,
             w_in=w_in, sinks=sinks, lam_re=lam_re, lam_im=lam_im, log_dt=log_dt, b_re=b_re, b_im=b_im, c_re=c_re,
             c_im=c_im, d_skip=d_skip, w_glu=w_glu, w_branch=w_branch, w_out=w_out, w_up=w_up, w_down=w_down)
    m = dict(norm_mix_pre=m_norm_mix_pre, norm_mix_post=m_norm_mix_post, norm_mlp_pre=m_norm_mlp_pre,
             norm_mlp_post=m_norm_mlp_post, w_in=m_w_in, sinks=m_sinks, lam_re=m_lam_re, lam_im=m_lam_im,
             log_dt=m_log_dt, b_re=m_b_re, b_im=m_b_im, c_re=m_c_re, c_im=m_c_im, d_skip=m_d_skip, w_glu=m_w_glu,
             w_branch=m_w_branch, w_out=m_w_out, w_up=m_w_up, w_down=m_w_down)
    v = dict(norm_mix_pre=v_norm_mix_pre, norm_mix_post=v_norm_mix_post, norm_mlp_pre=v_norm_mlp_pre,
             norm_mlp_post=v_norm_mlp_post, w_in=v_w_in, sinks=v_sinks, lam_re=v_lam_re, lam_im=v_lam_im,
             log_dt=v_log_dt, b_re=v_b_re, b_im=v_b_im, c_re=v_c_re, c_im=v_c_im, d_skip=v_d_skip, w_glu=v_w_glu,
             w_branch=v_w_branch, w_out=v_w_out, w_up=v_w_up, w_down=v_w_down)
    xi, yi, ci = lax.axis_index("x"), lax.axis_index("y"), lax.axis_index("c")

    k_arr = jnp.stack([2 * xi + yi]).astype(jnp.int32)
    *w_in_sems, w_in_thru = _gather_start(_cast_into_slot(w["w_in"][0], k_arr))
    slot = {k: _cast_into_slot(w[k][0], k_arr) for k in BIG if k != "w_in"}

    def whole(k, g4):
        if k in COL_SHARDED:
            return jnp.concatenate([g4[j] for j in range(4)], axis=1)
        return g4.reshape(4 * g4.shape[1], g4.shape[2])

    w_in = []

    def w_a_of(*after):
        arrived = _gather_wait(w_in_sems, w_in_thru, after + tuple(slot.values()))
        w_in.append(_comm_only("gather_w_in_pass", _PassOnComm([arrived]))[0])
        return _assemble_w_a(w_in[0])

    hosted = (("w_glu", "w_branch", "w_out"), ("w_up",), ("w_down",))
    comms = [None] + [_GatherComm([slot[k] for k in names]) for names in hosted]

    def late(*got):
        f = {k: whole(k, g4) for names, res in zip(hosted, got) for k, g4 in zip(names, res)}
        return f["w_glu"], f["w_branch"][:Q_W], f["w_branch"][Q_W:], f["w_out"], f["w_up"], f["w_down"]

    s5w = (lam_re[0], lam_im[0], log_dt[0], b_re[0], b_im[0], c_re[0], c_im[0], d_skip[0])
    reducer = _GradReducer(
        jnp.stack([ci, 2 * (1 - xi) + yi, 2 * xi + (1 - yi), 2 * (1 - xi) + (1 - yi)]).astype(jnp.int32),
        jnp.stack([2 * xi + yi, ci]).astype(jnp.int32))
    loss_part, dx, small, _ = _local_step(
        x[0], loss_target[0], (norm_mix_pre, norm_mix_post, norm_mlp_pre, norm_mlp_post),
        w_a_of, sinks, s5w, comms, lambda _: _assemble_w_g(w_in[0]), late, reducer)
    grads = reducer.finish(BIG)

    red = _all_reduce_small(_flat_small(small, loss_part)).reshape(-1)
    off = 0
    for k in SMALL:
        n = math.prod(w[k].shape)
        grads[k] = red[off:off + n].reshape(w[k].shape[1:])
        off += n
    loss = red[off]

    delta, new_m, new_v = {}, {}, {}
    for k in BIG:
        delta[k], new_m[k], new_v[k] = _adamw(w[k][0], grads[k], m[k][0], v[k][0], "adamw_" + k)
    zero = jnp.zeros((), F32)
    fw, fm, fv = (_flat_small({k: t[k] for k in SMALL}, zero) for t in (w, m, v))
    fg = _flat_small(grads, zero)
    sd, sm, sv = _adamw(fw, fg, fm, fv, "adamw_small", tr=fw.shape[0])
    off = 0
    for k in SMALL:
        n = math.prod(w[k].shape)
        delta[k], new_m[k], new_v[k] = (t.reshape(-1)[off:off + n].reshape(w[k].shape[1:]) for t in (sd, sm, sv))
        off += n

    lead = lambda t: t[None]
    return (loss, lead(dx), *[lead(grads[k]) for k in WEIGHTS], *[lead(delta[k]) for k in WEIGHTS],
            *[lead(new_m[k]) for k in WEIGHTS], *[lead(new_v[k]) for k in WEIGHTS])
```

```python
import functools
import math

import jax
import jax.numpy as jnp
from jax import lax
from jax.experimental import pallas as pl
from jax.experimental.pallas import tpu as pltpu

F32 = jnp.float32
BF16 = jnp.bfloat16
SDS = jax.ShapeDtypeStruct

D_MODEL = 2048
HEAD_DIM = 64
N_Q_HEADS = 16
ATT_BLOCK = 128
ROT_DIM = 16
ROPE_THETA = 500000.0
Q_W = 1024
KV_W = 128
SSM_W = 1024
SSM_G = 64
SSM_GC = 16
SSM_P = 64
N_STATE = SSM_G * SSM_P
LANES = 128
SUBLANES = 8
N_LG = N_STATE // LANES
N_JB = 8
LG_PER_JB = N_LG // N_JB
D_FF = 8192
ZA_W = Q_W + 2 * KV_W + SSM_W
EPS = 1e-6
S5_CHUNK = 2048
S5_SEG = S5_CHUNK // SUBLANES
VMEM_LIMIT = 56 * 1024 * 1024
NEG = -1e30

ADAM_LR = 0.001
ADAM_B1 = 0.9
ADAM_B2 = 0.999
ADAM_EPS = 1e-08
ADAM_WD = 0.01
ADAM_STEP = 10

MESH = pl.DeviceIdType.MESH


def _cp(sem):
    return pltpu.CompilerParams(dimension_semantics=sem, vmem_limit_bytes=VMEM_LIMIT)


ANY = pl.BlockSpec(memory_space=pl.ANY)


def _place():
    x, y, c = lax.axis_index("x"), lax.axis_index("y"), lax.axis_index("c")
    others = [(1 - x, y), (x, 1 - y), (1 - x, 1 - y)]
    return x, y, c, others


def _remote(src, dst, ssem, rsem, to):
    return pltpu.make_async_remote_copy(src_ref=src, dst_ref=dst, send_sem=ssem, recv_sem=rsem,
                                        device_id=to, device_id_type=MESH)


class _GatherComm:
    aliased = True

    def __init__(self, slotted):
        self.arrs = list(slotted)
        self.n = len(self.arrs)
        dma = pltpu.SemaphoreType.DMA
        self.scratch = [dma((3 * self.n,)) for _ in range(4)]
        self.out_shape = [SDS(s.shape, s.dtype) for s in self.arrs]

    @staticmethod
    def _half(ref, hc):
        hr = ref.shape[1] // 2
        return pl.ds(pl.multiple_of(hc * hr, 16), hr)

    def _sends(self, ins, outs, sems):
        ssem, rsem, _, _ = sems
        x, y, c, others = _place()
        me = 2 * x + y
        return [_remote(ins[w].at[me, self._half(ins[w], c), :], outs[w].at[me, self._half(ins[w], c), :],
                        ssem.at[3 * w + r], rsem.at[3 * w + r], (ox, oy, c))
                for w in range(self.n) for r, (ox, oy) in enumerate(others)]

    def start(self, ins, outs, sems):
        for cp in self._sends(ins, outs, sems):
            cp.start()

    def finish(self, ins, outs, sems):
        ssem, rsem, fs_sem, fr_sem = sems
        x, y, c, others = _place()
        sib = (x, y, 1 - c)
        passes = []
        for w in range(self.n):
            for r, (ox, oy) in enumerate(others):
                got = outs[w].at[2 * ox + oy, self._half(ins[w], c), :]
                _remote(got, got, ssem.at[3 * w + r], rsem.at[3 * w + r], (ox, oy, c)).wait_recv()
                cp = _remote(got, got, fs_sem.at[3 * w + r], fr_sem.at[3 * w + r], sib)
                cp.start()
                passes.append(cp)
        for w in range(self.n):
            for r, (ox, oy) in enumerate(others):
                got = outs[w].at[2 * ox + oy, self._half(ins[w], 1 - c), :]
                _remote(got, got, fs_sem.at[3 * w + r], fr_sem.at[3 * w + r], sib).wait_recv()
        for cp in self._sends(ins, outs, sems) + passes:
            cp.wait_send()


class _PairExchangeComm:
    aliased = False

    def __init__(self, grads):
        self.arrs = list(grads)
        self.n = len(self.arrs)
        dma = pltpu.SemaphoreType.DMA
        self.scratch = [dma((self.n,)), dma((self.n,))]
        self.out_shape = [SDS((4, g.shape[1] // 2, g.shape[2]), g.dtype) for g in self.arrs]

    def _copies(self, ins, outs, sems):
        ssem, rsem = sems
        x, y, c, _ = _place()
        cps = []
        for w in range(self.n):
            hr = ins[w].shape[1] // 2
            src = ins[w].at[:, pl.ds(pl.multiple_of((1 - c) * hr, 8), hr), :]
            cps.append(_remote(src, outs[w], ssem.at[w], rsem.at[w], (x, y, 1 - c)))
        return cps

    def start(self, ins, outs, sems):
        for cp in self._copies(ins, outs, sems):
            cp.start()

    def finish(self, ins, outs, sems):
        for cp in self._copies(ins, outs, sems):
            cp.wait()


class _ChipExchangeComm:
    aliased = False

    def __init__(self, psums):
        self.arrs = list(psums)
        self.n = len(self.arrs)
        dma = pltpu.SemaphoreType.DMA
        self.scratch = [dma((3 * self.n,)), dma((3 * self.n,))]
        self.out_shape = [SDS((3,) + p.shape[1:], p.dtype) for p in self.arrs]

    def _copies(self, ins, outs, sems):
        ssem, rsem = sems
        x, y, c, others = _place()
        return [_remote(ins[w].at[2 * ox + oy], outs[w].at[r], ssem.at[3 * w + r], rsem.at[3 * w + r], (ox, oy, c))
                for w in range(self.n) for r, (ox, oy) in enumerate(others)]

    def start(self, ins, outs, sems):
        for cp in self._copies(ins, outs, sems):
            cp.start()

    def finish(self, ins, outs, sems):
        for cp in self._copies(ins, outs, sems):
            cp.wait()


def _comm_only(name, comm):
    n = comm.n

    def body(*refs):
        ins, outs, sems = refs[:n], refs[n:2 * n], refs[2 * n:]
        comm.start(ins, outs, sems)
        comm.finish(ins, outs, sems)

    return pl.pallas_call(
        body, name=name, in_specs=[ANY] * n, out_specs=[ANY] * n, out_shape=comm.out_shape,
        input_output_aliases={w: w for w in range(n)} if comm.aliased else {},
        scratch_shapes=comm.scratch)(*comm.arrs)


HBM = pl.BlockSpec(memory_space=pltpu.HBM)
SEM = pl.BlockSpec(memory_space=pltpu.SEMAPHORE)
_EFFECT = pltpu.SideEffectType.DATAFLOW_SIDE_EFFECTING


def _gather_copies(ref, sems):
    x, y, c, others = _place()
    me = 2 * x + y
    half = _GatherComm._half(ref, c)
    out = [_remote(ref.at[me, half, :], ref.at[me, half, :], sems[r], sems[3 + r], (ox, oy, c))
           for r, (ox, oy) in enumerate(others)]
    arriving = [_remote(ref.at[me, half, :], ref.at[2 * ox + oy, half, :], sems[r], sems[3 + r], (ox, oy, c))
                for r, (ox, oy) in enumerate(others)]
    return out, arriving


def _gather_start(slotted):
    def body(w_ref, *rest):
        for cp in _gather_copies(rest[6], rest[:6])[0]:
            cp.start()

    dma = pltpu.SemaphoreType.DMA(())
    return pl.pallas_call(
        body, name="gather_w_in_start",
        out_shape=(dma,) * 6 + (pltpu.HBM(slotted.shape, slotted.dtype),),
        in_specs=(HBM,), out_specs=(SEM,) * 6 + (HBM,), input_output_aliases={0: 6},
        compiler_params=pltpu.CompilerParams(has_side_effects=_EFFECT),
    )(pltpu.with_memory_space_constraint(slotted, pltpu.HBM))


def _gather_wait(sems, thru, after):
    def body(w_ref, *rest):
        out, arriving = _gather_copies(w_ref, rest[:6])
        for cp in out:
            cp.wait_send()
        for cp in arriving:
            cp.wait_recv()

    n = len(after)
    return pl.pallas_call(
        body, name="gather_w_in_wait", out_shape=(pltpu.HBM(thru.shape, thru.dtype),),
        in_specs=(HBM,) + (SEM,) * 6 + (ANY,) * n, out_specs=(HBM,), input_output_aliases={0: 0},
        compiler_params=pltpu.CompilerParams(has_side_effects=_EFFECT),
    )(thru, *sems, *after)[0]


class _PassOnComm:
    aliased = True

    def __init__(self, gathered):
        self.arrs = list(gathered)
        self.n = len(self.arrs)
        dma = pltpu.SemaphoreType.DMA
        self.scratch = [dma((3 * self.n,)), dma((3 * self.n,))]
        self.out_shape = [SDS(s.shape, s.dtype) for s in self.arrs]

    def _copies(self, ins, outs, sems, hc):
        ssem, rsem = sems
        x, y, c, others = _place()
        return [_remote(ins[w].at[2 * ox + oy, _GatherComm._half(ins[w], c if hc == 0 else 1 - c), :],
                        outs[w].at[2 * ox + oy, _GatherComm._half(ins[w], c if hc == 0 else 1 - c), :],
                        ssem.at[3 * w + r], rsem.at[3 * w + r], (x, y, 1 - c))
                for w in range(self.n) for r, (ox, oy) in enumerate(others)]

    def start(self, ins, outs, sems):
        for cp in self._copies(ins, outs, sems, 0):
            cp.start()

    def finish(self, ins, outs, sems):
        for cp in self._copies(ins, outs, sems, 1):
            cp.wait_recv()
        for cp in self._copies(ins, outs, sems, 0):
            cp.wait_send()


def _call(name, body, grid, in_specs, out_specs, out_shape, scratch, dims, args, comm=None):
    if comm is None:
        return pl.pallas_call(body, name=name, grid=grid, in_specs=in_specs, out_specs=out_specs, out_shape=out_shape,
                              scratch_shapes=scratch, compiler_params=_cp(dims))(*args)
    ni, no, ns, n = len(in_specs), len(out_shape), len(scratch), comm.n

    def hosted(*refs):
        ins, cin = refs[:ni], refs[ni:ni + n]
        outs, cout = refs[ni + n:ni + n + no], refs[ni + n + no:ni + 2 * n + no]
        scr, sems = refs[ni + 2 * n + no:ni + 2 * n + no + ns], refs[ni + 2 * n + no + ns:]
        ids = [pl.program_id(d) for d in range(len(grid))]
        first = functools.reduce(jnp.logical_and, [i == 0 for i in ids])
        last = functools.reduce(jnp.logical_and, [i == g - 1 for i, g in zip(ids, grid)])

        @pl.when(first)
        def _():
            comm.start(cin, cout, sems)

        body(*ins, *outs, *scr)

        @pl.when(last)
        def _():
            comm.finish(cin, cout, sems)

    return pl.pallas_call(
        hosted, name=name, grid=grid, in_specs=list(in_specs) + [ANY] * n, out_specs=list(out_specs) + [ANY] * n,
        out_shape=list(out_shape) + comm.out_shape,
        input_output_aliases={ni + w: no + w for w in range(n)} if comm.aliased else {},
        scratch_shapes=list(scratch) + comm.scratch, compiler_params=_cp(("arbitrary",) * len(grid)))(*args, *comm.arrs)


def _mm(a, b, *, mode, out_dtype, tm, tn, tk, name, a_fn=None, epi=None, extras=(), extra_cols=None, comm=None,
        shard_cols=None):
    if mode == "nn":
        (M, K), (K2, N) = a.shape, b.shape
    elif mode == "nt":
        (M, K), (N, K2) = a.shape, b.shape
    else:
        (K, M), (K2, N) = a.shape, b.shape
    assert K == K2, (a.shape, b.shape, mode)
    tm, tn, tk = min(tm, M), min(tn, N), min(tk, K)
    assert M % tm == 0 and N % tn == 0 and K % tk == 0, (M, N, K, tm, tn, tk)
    nk = K // tk
    if mode == "tn":
        a_spec = pl.BlockSpec((tk, tm), lambda i, j, k: (k, i))
        ca = 0
    else:
        a_spec = pl.BlockSpec((tm, tk), lambda i, j, k: (i, k))
        ca = 1
    if mode == "nt":
        b_spec = pl.BlockSpec((tn, tk), lambda i, j, k: (j, k))
        cb = 1
    else:
        b_spec = pl.BlockSpec((tk, tn), lambda i, j, k: (k, j))
        cb = 0
    dims = (((ca,), (cb,)), ((), ()))
    ne = len(extras)
    out_dtypes = out_dtype if isinstance(out_dtype, tuple) else (out_dtype,)
    no = len(out_dtypes)
    extra_cols = extra_cols or (0,) * ne

    def body(a_ref, b_ref, *rest):
        ex = rest[:ne]
        o_refs = rest[ne:ne + no]
        av = a_ref[...]
        if a_fn is not None:
            av = a_fn(av.astype(F32))
        p = lax.dot_general(av.astype(BF16), b_ref[...].astype(BF16), dims, preferred_element_type=F32)

        def fin(v):
            if epi is not None:
                v = epi(v, *[e[...] for e in ex])
            for o_ref, val in zip(o_refs, v if no > 1 else (v,)):
                o_ref[...] = val.astype(o_ref.dtype).reshape(o_ref.shape)

        if nk == 1:
            fin(p)
        else:
            acc = rest[ne + no]
            k = pl.program_id(2)

            @pl.when(k == 0)
            def _():
                acc[...] = p

            @pl.when(k > 0)
            def _():
                acc[...] += p

            @pl.when(k == nk - 1)
            def _():
                fin(acc[...])

    if shard_cols is None:
        o_spec, o_shape = pl.BlockSpec((tm, tn), lambda i, j, k: (i, j)), (M, N)
    else:
        per = shard_cols // tn
        assert shard_cols % tn == 0 and N % shard_cols == 0
        o_spec = pl.BlockSpec((1, tm, tn), lambda i, j, k: (lax.div(j, per), i, lax.rem(j, per)))
        o_shape = (N // shard_cols, M, shard_cols)
    ex_specs = [pl.BlockSpec((tm, tn), functools.partial(lambda i, j, k, off: (i, j + off), off=off)) for off in extra_cols]
    res = _call(name, body, (M // tm, N // tn, nk), [a_spec, b_spec] + ex_specs,
                [o_spec] * no, [SDS(o_shape, dt) for dt in out_dtypes],
                [pltpu.VMEM((tm, tn), F32)] if nk > 1 else [], ("parallel", "parallel", "arbitrary"),
                (a, b, *extras), comm)
    own = res[0] if no == 1 else tuple(res[:no])
    return own if comm is None else (own, res[no:])


def _rowwise(fn, rows, bcasts, outs, accs, *, tr, name):
    T = rows[0][0].shape[0]
    tr = min(tr, T)
    assert T % tr == 0
    nr, nb, no, na = len(rows), len(bcasts), len(outs), len(accs)
    in_specs = [pl.BlockSpec((tr, w), functools.partial(lambda i, c: (i, c), c=cb)) for (_, w, cb) in rows]
    in_specs += [pl.BlockSpec(b.shape, lambda i: (0, 0)) for b in bcasts]
    out_shape = [SDS((T, w), dt) for (w, dt) in outs] + [SDS(s, F32) for s in accs]
    out_specs = [pl.BlockSpec((tr, w), lambda i: (i, 0)) for (w, _) in outs]
    out_specs += [pl.BlockSpec(s, lambda i: (0, 0)) for s in accs]

    def body(*refs):
        ins = [r[...].astype(F32) for r in refs[:nr + nb]]
        o_refs = refs[nr + nb:nr + nb + no]
        a_refs = refs[nr + nb + no:]
        ro, ao = fn(*ins)
        for r, v in zip(o_refs, ro):
            r[...] = v.astype(r.dtype)
        if na:
            @pl.when(pl.program_id(0) == 0)
            def _():
                for r in a_refs:
                    r[...] = jnp.zeros(r.shape, F32)

            for r, v in zip(a_refs, ao):
                r[...] += v

    res = pl.pallas_call(
        body,
        name=name,
        grid=(T // tr,),
        in_specs=in_specs,
        out_specs=out_specs,
        out_shape=out_shape,
        compiler_params=_cp(("arbitrary",) if na else ("parallel",)),
    )(*[r[0] for r in rows], *bcasts)
    return res


def _rms(v):
    r = lax.rsqrt(jnp.mean(v * v, axis=-1, keepdims=True) + EPS)
    return v * r, r


def _rms_bwd(dy, xn, r, g):
    dxn = dy * g
    dv = r * (dxn - xn * jnp.mean(dxn * xn, axis=-1, keepdims=True))
    return dv, jnp.sum(dy * xn, axis=0, keepdims=True)


def _sig(v):
    return 1.0 / (1.0 + jnp.exp(-v))


_GELU_C = math.sqrt(2.0 / math.pi)


def _gelu(v):
    return 0.5 * v * (1.0 + jnp.tanh(_GELU_C * (v + 0.044715 * v * v * v)))


def _gelu_grad(v):
    t = jnp.tanh(_GELU_C * (v + 0.044715 * v * v * v))
    return 0.5 * (1.0 + t) + 0.5 * v * (1.0 - t * t) * _GELU_C * (1.0 + 3.0 * 0.044715 * v * v)


def _rope(v, c, s, sign):
    w = v.shape[1]
    m = lax.broadcasted_iota(jnp.int32, v.shape, 1) % HEAD_DIM
    p = jnp.where(m < ROT_DIM // 2, -pltpu.roll(v, w - ROT_DIM // 2, 1), pltpu.roll(v, ROT_DIM // 2, 1))
    return v * c + sign * (p * s)


def _rope_tables(T):
    half = ROT_DIM // 2
    inv = ROPE_THETA ** (-jnp.arange(half, dtype=F32) * 2.0 / ROT_DIM)
    ang = jnp.arange(T).astype(F32)[:, None] * inv[None, :]
    cos, sin = jnp.cos(ang), jnp.sin(ang)
    one = jnp.ones((T, HEAD_DIM - ROT_DIM), F32)
    c64 = jnp.concatenate([cos, cos, one], axis=1)
    s64 = jnp.concatenate([sin, sin, 0.0 * one], axis=1)
    return jnp.tile(c64, (1, 2)), jnp.tile(s64, (1, 2))


def _dup_half(m, lo):
    lane = lax.broadcasted_iota(jnp.int32, m.shape, 1)
    sw = pltpu.roll(m, HEAD_DIM, 1)
    return jnp.where(lane < HEAD_DIM, m, sw) if lo else jnp.where(lane >= HEAD_DIM, m, sw)


def _attn_mask(i):
    qi = lax.broadcasted_iota(jnp.int32, (ATT_BLOCK, 2 * ATT_BLOCK), 0)
    kj = lax.broadcasted_iota(jnp.int32, (ATT_BLOCK, 2 * ATT_BLOCK), 1)
    rel = qi + ATT_BLOCK - kj
    return (rel >= 0) & (rel < ATT_BLOCK) & ((kj >= ATT_BLOCK) | (i > 0))


_NT = (((1,), (1,)), ((), ()))
_TN = (((0,), (0,)), ((), ()))


def _stack_heads(m):
    lane = lax.broadcasted_iota(jnp.int32, m.shape, 1)
    zero = jnp.zeros_like(m)
    return jnp.concatenate([jnp.where(lane < HEAD_DIM, m, zero), jnp.where(lane >= HEAD_DIM, m, zero)], axis=0)


def _pair_probs(q2, k2, ok2, sink_lo, sink_hi):
    qs = _stack_heads(q2)
    s = lax.dot_general(qs, k2, _NT, preferred_element_type=F32)
    s = jnp.where(ok2, s, NEG)
    row = lax.broadcasted_iota(jnp.int32, (2 * ATT_BLOCK, 1), 0)
    sink = jnp.where(row < ATT_BLOCK, sink_lo, sink_hi)
    m = jnp.maximum(jnp.max(s, axis=1, keepdims=True), sink)
    e = jnp.exp(s - m)
    es = jnp.exp(sink - m)
    inv = 1.0 / (jnp.sum(e, axis=1, keepdims=True) + es)
    return e * inv, es * inv, qs


def _attn_fwd(za, cos, sin, sinks, comm=None):
    T = za.shape[0]
    nb = T // ATT_BLOCK
    kvb = Q_W // (2 * KV_W)

    def body(sink_ref, q_ref, kvp_ref, kvc_ref, cc_ref, sc_ref, cp_ref, sp_ref, o_ref):
        i = pl.program_id(0)
        cc, sc, cp, sp = cc_ref[...], sc_ref[...], cp_ref[...], sp_ref[...]
        q = (_rope(q_ref[...], jnp.tile(cc, (1, 8)), jnp.tile(sc, (1, 8)), 1.0) * 0.125).astype(BF16)
        kvp, kvc = kvp_ref[...], kvc_ref[...]
        k = jnp.concatenate([_rope(kvp[:, :KV_W], cp, sp, 1.0), _rope(kvc[:, :KV_W], cc, sc, 1.0)], axis=0).astype(BF16)
        v = jnp.concatenate([kvp[:, KV_W:], kvc[:, KV_W:]], axis=0).astype(BF16)
        ok = _attn_mask(i)
        ok2 = jnp.concatenate([ok, ok], axis=0)
        lane = lax.broadcasted_iota(jnp.int32, (ATT_BLOCK, LANES), 1)
        for kvh in range(2):
            k2 = _dup_half(k, kvh == 0)
            v2 = _dup_half(v, kvh == 0)
            for pair in range(4):
                c0 = (kvh * 4 + pair) * LANES
                q2 = q[:, c0:c0 + LANES]
                p, _, _ = _pair_probs(q2, k2, ok2, sink_ref[0, 2 * (kvh * 4 + pair)], sink_ref[0, 2 * (kvh * 4 + pair) + 1])
                o = jnp.dot(p.astype(BF16), v2, preferred_element_type=F32)
                o_ref[:, c0:c0 + LANES] = jnp.where(lane < HEAD_DIM, o[:ATT_BLOCK], o[ATT_BLOCK:]).astype(BF16)

    blk = lambda w, f: pl.BlockSpec((ATT_BLOCK, w), f)
    res = _call(
        "attn_fwd", body, (nb,),
        [
            pl.BlockSpec(memory_space=pltpu.SMEM),
            blk(Q_W, lambda i: (i, 0)),
            blk(2 * KV_W, lambda i: (jnp.maximum(i - 1, 0), kvb)),
            blk(2 * KV_W, lambda i: (i, kvb)),
            blk(LANES, lambda i: (i, 0)),
            blk(LANES, lambda i: (i, 0)),
            blk(LANES, lambda i: (jnp.maximum(i - 1, 0), 0)),
            blk(LANES, lambda i: (jnp.maximum(i - 1, 0), 0)),
        ],
        [blk(Q_W, lambda i: (i, 0))], [SDS((T, Q_W), BF16)], [], ("parallel",),
        (sinks, za, za, za, cos, sin, cos, sin), comm)
    return res[0] if comm is None else (res[0], res[1:])


def _attn_bwd(za, cos, sin, sinks, o, do, comm=None):
    T = za.shape[0]
    nb = T // ATT_BLOCK
    kvb = Q_W // (2 * KV_W)

    def body(sink_ref, q_ref, kvp_ref, kvc_ref, cc_ref, sc_ref, cp_ref, sp_ref, o_ref, do_ref,
             dq_ref, dkv_ref, dsk_ref, carry, dqs):
        i = pl.program_id(0)

        @pl.when(i == 0)
        def _():
            carry[...] = jnp.zeros(carry.shape, F32)
            dsk_ref[...] = jnp.zeros(dsk_ref.shape, F32)

        @pl.when(i < nb)
        def _():
            cc, sc, cp, sp = cc_ref[...], sc_ref[...], cp_ref[...], sp_ref[...]
            ccq, scq = jnp.tile(cc, (1, 8)), jnp.tile(sc, (1, 8))
            q = (_rope(q_ref[...], ccq, scq, 1.0) * 0.125).astype(BF16)
            kvp, kvc = kvp_ref[...], kvc_ref[...]
            k = jnp.concatenate([_rope(kvp[:, :KV_W], cp, sp, 1.0), _rope(kvc[:, :KV_W], cc, sc, 1.0)], axis=0).astype(BF16)
            v = jnp.concatenate([kvp[:, KV_W:], kvc[:, KV_W:]], axis=0).astype(BF16)
            ok = _attn_mask(i)
            ok2 = jnp.concatenate([ok, ok], axis=0)
            lane = lax.broadcasted_iota(jnp.int32, (ATT_BLOCK, LANES), 1)
            lane_s = lax.broadcasted_iota(jnp.int32, (1, LANES), 1)
            dsk = jnp.zeros((1, LANES), F32)
            dkt_h, dvt_h = [], []
            for kvh in range(2):
                k2 = _dup_half(k, kvh == 0)
                v2 = _dup_half(v, kvh == 0)
                dkt = jnp.zeros((LANES, 2 * ATT_BLOCK), F32)
                dvt = jnp.zeros((LANES, 2 * ATT_BLOCK), F32)
                for pair in range(4):
                    h = 2 * (kvh * 4 + pair)
                    c0 = (kvh * 4 + pair) * LANES
                    do2 = do_ref[:, c0:c0 + LANES]
                    prod = do2.astype(F32) * o_ref[:, c0:c0 + LANES].astype(F32)
                    d_lo = jnp.sum(jnp.where(lane < HEAD_DIM, prod, 0.0), axis=1, keepdims=True)
                    d_hi = jnp.sum(jnp.where(lane >= HEAD_DIM, prod, 0.0), axis=1, keepdims=True)
                    delta = jnp.concatenate([d_lo, d_hi], axis=0)
                    p, p_sink, qs = _pair_probs(q[:, c0:c0 + LANES], k2, ok2, sink_ref[0, h], sink_ref[0, h + 1])
                    dos = _stack_heads(do2)
                    t = p_sink * delta
                    dsk = dsk - jnp.where(lane_s == h, jnp.sum(t[:ATT_BLOCK]), 0.0) \
                              - jnp.where(lane_s == h + 1, jnp.sum(t[ATT_BLOCK:]), 0.0)
                    dp = lax.dot_general(dos, v2, _NT, preferred_element_type=F32)
                    ds = (p * (dp - delta)).astype(BF16)
                    dqp = jnp.dot(ds, k2, preferred_element_type=F32)
                    dqs[:, c0:c0 + LANES] = jnp.where(lane < HEAD_DIM, dqp[:ATT_BLOCK], dqp[ATT_BLOCK:]) * 0.125
                    dkt = dkt + lax.dot_general(qs, ds, _TN, preferred_element_type=F32)
                    dvt = dvt + lax.dot_general(dos, p.astype(BF16), _TN, preferred_element_type=F32)
                dkt_h.append(dkt[:HEAD_DIM] + dkt[HEAD_DIM:])
                dvt_h.append(dvt[:HEAD_DIM] + dvt[HEAD_DIM:])
            dk = jnp.concatenate(dkt_h, axis=0).T
            dv = jnp.concatenate(dvt_h, axis=0).T
            dq_ref[...] = _rope(dqs[...], ccq, scq, -1.0).astype(dq_ref.dtype)
            dkp = _rope(dk[:ATT_BLOCK], cp, sp, -1.0)
            dkc = _rope(dk[ATT_BLOCK:], cc, sc, -1.0)
            dkv_ref[...] = (carry[...] + jnp.concatenate([dkp, dv[:ATT_BLOCK]], axis=1)).astype(dkv_ref.dtype)
            carry[...] = jnp.concatenate([dkc, dv[ATT_BLOCK:]], axis=1)
            dsk_ref[...] += dsk

        @pl.when(i == nb)
        def _():
            dkv_ref[...] = carry[...].astype(dkv_ref.dtype)

    blk = lambda w, f: pl.BlockSpec((ATT_BLOCK, w), f)
    cur = lambda i: jnp.minimum(i, nb - 1)
    prv = lambda i: jnp.maximum(jnp.minimum(i, nb - 1) - 1, 0)
    res = _call(
        "attn_bwd", body, (nb + 1,),
        [
            pl.BlockSpec(memory_space=pltpu.SMEM),
            blk(Q_W, lambda i: (cur(i), 0)),
            blk(2 * KV_W, lambda i: (prv(i), kvb)),
            blk(2 * KV_W, lambda i: (cur(i), kvb)),
            blk(LANES, lambda i: (cur(i), 0)),
            blk(LANES, lambda i: (cur(i), 0)),
            blk(LANES, lambda i: (prv(i), 0)),
            blk(LANES, lambda i: (prv(i), 0)),
            blk(Q_W, lambda i: (cur(i), 0)),
            blk(Q_W, lambda i: (cur(i), 0)),
        ],
        [
            blk(Q_W, lambda i: (cur(i), 0)),
            blk(2 * KV_W, lambda i: (jnp.maximum(i - 1, 0), 0)),
            pl.BlockSpec((1, LANES), lambda i: (0, 0)),
        ],
        [SDS((T, Q_W), BF16), SDS((T, 2 * KV_W), BF16), SDS((1, LANES), F32)],
        [pltpu.VMEM((ATT_BLOCK, 2 * KV_W), F32), pltpu.VMEM((ATT_BLOCK, Q_W), F32)],
        ("arbitrary",), (sinks, za, za, za, cos, sin, cos, sin, o, do), comm)
    return res if comm is None else (res[:3], res[3:])


def _s5_discretize(lam_re, lam_im, log_dt, b_re, b_im):
    dt = jnp.exp(log_dt)[:, None]
    mag = jnp.exp(lam_re * dt)
    a_re, a_im = mag * jnp.cos(lam_im * dt), mag * jnp.sin(lam_im * dt)
    den = lam_re * lam_re + lam_im * lam_im
    nr, ni = a_re - 1.0, a_im
    coef_re = (nr * lam_re + ni * lam_im) / den
    coef_im = (ni * lam_re - nr * lam_im) / den
    bb_re = coef_re[..., None] * b_re - coef_im[..., None] * b_im
    bb_im = coef_re[..., None] * b_im + coef_im[..., None] * b_re
    return a_re, a_im, bb_re, bb_im


def _blockdiag_in(bb):
    x = bb.reshape(N_JB, 8, SSM_P, SSM_GC).transpose(0, 1, 3, 2)
    return (x[:, :, :, None, :] * jnp.eye(8, dtype=bb.dtype)[None, :, None, :, None]).reshape(N_JB, 128, 512)


def _blockdiag_in_extract(m):
    x = m.reshape(N_JB, 8, SSM_GC, 8, SSM_P)
    x = jnp.einsum('jgchp,gh->jgcp', x, jnp.eye(8, dtype=m.dtype))
    return x.transpose(0, 1, 3, 2).reshape(SSM_G, SSM_P, SSM_GC)


def _blockdiag_out(c):
    x = c.reshape(N_JB, 8, SSM_GC, SSM_P).transpose(0, 1, 3, 2)
    return (x[:, :, :, None, :] * jnp.eye(8, dtype=c.dtype)[None, :, None, :, None]).reshape(N_JB, 512, 128)


def _blockdiag_out_extract(m):
    x = m.reshape(N_JB, 8, SSM_P, 8, SSM_GC)
    x = jnp.einsum('jgphc,gh->jgpc', x, jnp.eye(8, dtype=m.dtype))
    return x.transpose(0, 1, 3, 2).reshape(SSM_G, SSM_GC, SSM_P)


def _s5_tables(a_re, a_im):
    ar, ai = a_re.reshape(N_LG, 1, LANES), a_im.reshape(N_LG, 1, LANES)
    pr, pi, n = ar, ai, 1
    while n < S5_SEG:
        pr, pi, n = pr * pr - pi * pi, 2.0 * pr * pi, 2 * n
    assert n == S5_SEG
    bc = lambda v: jnp.broadcast_to(v, (N_LG, SUBLANES, LANES))
    return bc(ar), bc(ai), pr, pi


def _s5_to_time_major(src_ref, dst_ref):
    for t in range(S5_SEG):
        dst_ref[t * SUBLANES:(t + 1) * SUBLANES, :] = src_ref[pl.ds(t, SUBLANES, stride=S5_SEG), :]


def _s5_from_time_major(val, dst_ref):
    for t in range(S5_SEG):
        dst_ref[pl.ds(t, SUBLANES, stride=S5_SEG), :] = val[t * SUBLANES:(t + 1) * SUBLANES, :]


def _tm_rows(t, row0=0):
    return pl.ds(pl.multiple_of(t * SUBLANES + row0, SUBLANES), SUBLANES)


def _s5_scan(src_re, src_im, ar, ai, reverse, start=None, dst=None, dst_row0=0):
    def step(n, carry):
        t = (S5_SEG - 1 - n) if reverse else n
        out = []
        for ll in range(LG_PER_JB):
            xr, xi = carry[2 * ll], carry[2 * ll + 1]
            idx = (ll, _tm_rows(t), slice(None))
            nr = ar[ll] * xr - ai[ll] * xi + src_re[idx]
            ni = ar[ll] * xi + ai[ll] * xr + src_im[idx]
            if dst is not None:
                odx = (ll, _tm_rows(t, dst_row0), slice(None))
                dst[0][odx] = nr
                dst[1][odx] = ni
            out += [nr, ni]
        return tuple(out)
    if start is None:
        init = (jnp.zeros((SUBLANES, LANES), F32),) * (2 * LG_PER_JB)
    else:
        init = tuple(s[ll] for ll in range(LG_PER_JB) for s in start)
    return lax.fori_loop(0, S5_SEG, step, init)


def _s5_fixup(ends, in_re, in_im, mr, mi, s_re, s_im, reverse):
    cr, ci = in_re, in_im
    order = range(SUBLANES - 1, -1, -1) if reverse else range(SUBLANES)
    for s in order:
        s_re[:, s:s + 1, :] = cr
        s_im[:, s:s + 1, :] = ci
        er = jnp.stack([ends[2 * ll][s:s + 1, :] for ll in range(LG_PER_JB)])
        ei = jnp.stack([ends[2 * ll + 1][s:s + 1, :] for ll in range(LG_PER_JB)])
        cr, ci = mr * cr - mi * ci + er, mr * ci + mi * cr + ei
    return cr, ci


def _s5_specs(nc, rev):
    cidx = (lambda c: nc - 1 - c) if rev else (lambda c: c)
    jb = lambda shape: pl.BlockSpec(shape, lambda j, c: (j, 0, 0))
    return cidx, [
        jb((1, LANES, 8 * LANES)),
        jb((1, 8 * LANES, LANES)),
        pl.BlockSpec((1, LANES), lambda j, c: (0, j)),
        jb((LG_PER_JB, SUBLANES, LANES)), jb((LG_PER_JB, SUBLANES, LANES)),
        jb((LG_PER_JB, 1, LANES)), jb((LG_PER_JB, 1, LANES)),
    ]


def _s5_fwd(za, prm, comm=None):
    T = za.shape[0]
    R = S5_CHUNK
    nc = T // R
    ub = (Q_W + 2 * KV_W) // LANES
    _, pspecs = _s5_specs(nc, False)

    def body(u_ref, b_ref, c_ref, d_ref, are_ref, aim_ref, alr_ref, ali_ref,
             yg_ref, x0r_ref, x0i_ref, bur, bui, xsr, xsi, sr, si, xcr, xci, utm, ynat):
        c = pl.program_id(1)

        @pl.when(c == 0)
        def _():
            xcr[...] = jnp.zeros(xcr.shape, F32)
            xci[...] = jnp.zeros(xci.shape, F32)

        _s5_to_time_major(u_ref, utm)
        u = utm[...]
        ub16 = u.astype(BF16)
        bu = jnp.dot(ub16, b_ref[0].astype(BF16), preferred_element_type=F32)
        for ll in range(LG_PER_JB):
            bur[ll] = bu[:, ll * LANES:(ll + 1) * LANES]
            bui[ll] = bu[:, (LG_PER_JB + ll) * LANES:(LG_PER_JB + ll + 1) * LANES]
        ar = [are_ref[ll] for ll in range(LG_PER_JB)]
        ai = [aim_ref[ll] for ll in range(LG_PER_JB)]
        ends = _s5_scan(bur, bui, ar, ai, False)
        in_r, in_i = xcr[...], xci[...]
        x0r_ref[0] = in_r
        x0i_ref[0] = in_i
        out_r, out_i = _s5_fixup(ends, in_r, in_i, alr_ref[...], ali_ref[...], sr, si, False)
        xcr[...] = out_r
        xci[...] = out_i
        _s5_scan(bur, bui, ar, ai, False, start=(sr, si), dst=(xsr, xsi))
        xcat =jnp.concatenate([xsr[ll].astype(BF16) for ll in range(LG_PER_JB)]
                               + [xsi[ll].astype(BF16) for ll in range(LG_PER_JB)], axis=1)
        y = d_ref[...] * u + jnp.dot(xcat, c_ref[0].astype(BF16), preferred_element_type=F32)
        _s5_from_time_major(_gelu(y), ynat)
        yg_ref[...] = ynat[...].astype(BF16)

    st = pl.BlockSpec((1, LG_PER_JB, 1, LANES), lambda j, c: (c, j, 0, 0))
    vm = lambda rows: pltpu.VMEM((LG_PER_JB, rows, LANES), F32)
    res = _call(
        "s5_fwd", body, (N_JB, nc),
        [pl.BlockSpec((R, LANES), lambda j, c: (c, ub + j))] + pspecs,
        [pl.BlockSpec((R, LANES), lambda j, c: (c, j)), st, st],
        [SDS((T, SSM_W), BF16), SDS((nc, N_LG, 1, LANES), F32), SDS((nc, N_LG, 1, LANES), F32)],
        [vm(R), vm(R), vm(R), vm(R), vm(SUBLANES), vm(SUBLANES), vm(1), vm(1),
         pltpu.VMEM((R, LANES), F32), pltpu.VMEM((R, LANES), F32)],
        ("parallel", "arbitrary"), (za, *prm), comm)
    return res if comm is None else (res[:3], res[3:])


def _s5_bwd(za, dyg, x0r, x0i, prm, comm=None):
    T = za.shape[0]
    R = S5_CHUNK
    nc = T // R
    ub = (Q_W + 2 * KV_W) // LANES
    cidx, pspecs = _s5_specs(nc, True)
    PAD = SUBLANES

    def body(u_ref, dyg_ref, x0r_ref, x0i_ref, b_ref, c_ref, d_ref, are_ref, aim_ref,
             alr_ref, ali_ref,
             du_ref, dar_ref, dai_ref, db_ref, dc_ref, dd_ref,
             bur, bui, xsr, xsi, sr, si, gcr, gci, utm, dtm, dunat):
        c = pl.program_id(1)

        @pl.when(c == 0)
        def _():
            gcr[...] = jnp.zeros(gcr.shape, F32)
            gci[...] = jnp.zeros(gci.shape, F32)
            dar_ref[...] = jnp.zeros(dar_ref.shape, F32)
            dai_ref[...] = jnp.zeros(dai_ref.shape, F32)
            db_ref[...] = jnp.zeros(db_ref.shape, F32)
            dc_ref[...] = jnp.zeros(dc_ref.shape, F32)
            dd_ref[...] = jnp.zeros(dd_ref.shape, F32)

        _s5_to_time_major(u_ref, utm)
        _s5_to_time_major(dyg_ref, dtm)
        u = utm[...]
        ub16 = u.astype(BF16)
        bcat, ccat = b_ref[0].astype(BF16), c_ref[0].astype(BF16)
        lanes = lambda v, ll: v[:, ll * LANES:(ll + 1) * LANES]
        bu = jnp.dot(ub16, bcat, preferred_element_type=F32)
        for ll in range(LG_PER_JB):
            bur[ll] = lanes(bu, ll)
            bui[ll] = lanes(bu, LG_PER_JB + ll)
        ar = [are_ref[ll] for ll in range(LG_PER_JB)]
        ai = [aim_ref[ll] for ll in range(LG_PER_JB)]
        ends = _s5_scan(bur, bui, ar, ai, False)
        in_r, in_i = x0r_ref[0], x0i_ref[0]
        _s5_fixup(ends, in_r, in_i, alr_ref[...], ali_ref[...], sr, si, False)
        _s5_scan(bur, bui, ar, ai, False, start=(sr, si), dst=(xsr, xsi), dst_row0=PAD)
        xsr[:, 0:PAD, :] = sr[...]
        xsi[:, 0:PAD, :] = si[...]
        xcat = jnp.concatenate([xsr[ll, PAD:, :].astype(BF16) for ll in range(LG_PER_JB)]
                               + [xsi[ll, PAD:, :].astype(BF16) for ll in range(LG_PER_JB)], axis=1)
        y = d_ref[...] * u + jnp.dot(xcat, ccat, preferred_element_type=F32)
        dy = dtm[...] * _gelu_grad(y)
        dyb = dy.astype(BF16)
        dd_ref[...] += jnp.sum(dy * u, axis=0, keepdims=True)
        du = d_ref[...] * dy
        dc_ref[0] += lax.dot_general(dyb, xcat, _TN, preferred_element_type=F32)
        g = lax.dot_general(dyb, ccat, _NT, preferred_element_type=F32)
        for ll in range(LG_PER_JB):
            bur[ll] = lanes(g, ll)
            bui[ll] = lanes(g, LG_PER_JB + ll)
        aic = [-v for v in ai]
        ends = _s5_scan(bur, bui, ar, aic, True)
        out_r, out_i = _s5_fixup(ends, gcr[...], gci[...], alr_ref[...], -ali_ref[...], sr, si, True)
        gcr[...] = out_r
        gci[...] = out_i
        _s5_scan(bur, bui, ar, aic, True, start=(sr, si), dst=(bur, bui))
        for ll in range(LG_PER_JB):
            gr, gi = bur[ll], bui[ll]
            xpr, xpi = xsr[ll, 0:R, :], xsi[ll, 0:R, :]
            red = lambda v: v.reshape(R // SUBLANES, SUBLANES, LANES).sum(axis=0)
            dar_ref[ll] += red(xpr * gr + xpi * gi)
            dai_ref[ll] += red(xpr * gi - xpi * gr)
        gcat = jnp.concatenate([bur[ll].astype(BF16) for ll in range(LG_PER_JB)]
                               + [bui[ll].astype(BF16) for ll in range(LG_PER_JB)], axis=1)
        db_ref[0] += lax.dot_general(ub16, gcat, _TN, preferred_element_type=F32)
        du = du + lax.dot_general(gcat, bcat, _NT, preferred_element_type=F32)
        _s5_from_time_major(du, dunat)
        du_ref[...] = dunat[...].astype(du_ref.dtype)

    st = pl.BlockSpec((1, LG_PER_JB, 1, LANES), lambda j, c: (cidx(c), j, 0, 0))
    jb = lambda shape: pl.BlockSpec(shape, lambda j, c: (j, 0, 0))
    vm = lambda rows: pltpu.VMEM((LG_PER_JB, rows, LANES), F32)
    res = _call(
        "s5_bwd", body, (N_JB, nc),
        [pl.BlockSpec((R, LANES), lambda j, c: (cidx(c), ub + j)),
         pl.BlockSpec((R, LANES), lambda j, c: (cidx(c), j)), st, st] + pspecs,
        [pl.BlockSpec((R, LANES), lambda j, c: (cidx(c), j)),
         jb((LG_PER_JB, SUBLANES, LANES)), jb((LG_PER_JB, SUBLANES, LANES)),
         jb((1, LANES, 8 * LANES)), jb((1, LANES, 8 * LANES)),
         pl.BlockSpec((1, LANES), lambda j, c: (0, j))],
        [SDS((T, SSM_W), BF16), SDS((N_LG, SUBLANES, LANES), F32), SDS((N_LG, SUBLANES, LANES), F32),
         SDS((N_JB, LANES, 8 * LANES), F32), SDS((N_JB, LANES, 8 * LANES), F32), SDS((1, SSM_W), F32)],
        [vm(R), vm(R), vm(R + PAD), vm(R + PAD), vm(SUBLANES), vm(SUBLANES), vm(1), vm(1)]
        + [pltpu.VMEM((R, LANES), F32)] * 3,
        ("parallel", "arbitrary"), (za, dyg, x0r, x0i, *prm), comm)
    return res if comm is None else (res[:6], res[6:])


def _assemble_w_a(wi):
    _, rows, cb = wi.shape
    tr = 256

    def body(w_ref, a_ref):
        a_ref[:, :cb] = w_ref[0]
        a_ref[:, cb:] = w_ref[1, :, :ZA_W - cb]

    return pl.pallas_call(
        body, name="assemble_w_a", grid=(rows // tr,),
        in_specs=[pl.BlockSpec((2, tr, cb), lambda i: (0, i, 0))],
        out_specs=pl.BlockSpec((tr, ZA_W), lambda i: (i, 0)),
        out_shape=SDS((rows, ZA_W), wi.dtype), compiler_params=_cp(("parallel",)))(wi)


def _assemble_w_g(wi):
    _, rows, cb = wi.shape
    tr = 256
    cut = ZA_W - cb

    def body(w_ref, g_ref):
        g_ref[:, :cb - cut] = w_ref[1, :, cut:]
        g_ref[:, cb - cut:2 * cb - cut] = w_ref[2]
        g_ref[:, 2 * cb - cut:] = w_ref[3]

    return pl.pallas_call(
        body, name="assemble_w_g", grid=(rows // tr,),
        in_specs=[pl.BlockSpec((4, tr, cb), lambda i: (0, i, 0))],
        out_specs=pl.BlockSpec((tr, 4 * cb - ZA_W), lambda i: (i, 0)),
        out_shape=SDS((rows, 4 * cb - ZA_W), wi.dtype), compiler_params=_cp(("parallel",)))(wi)


def _stack_w_in_grad(d_w_a, d_w_g):
    rows = d_w_a.shape[0]
    cb = (ZA_W + d_w_g.shape[1]) // 4
    cut = ZA_W - cb
    tr = 256

    def body(a_ref, g_ref, o_ref):
        o_ref[0] = a_ref[:, :cb]
        o_ref[1, :, :cut] = a_ref[:, cb:]
        o_ref[1, :, cut:] = g_ref[:, :cb - cut]
        o_ref[2] = g_ref[:, cb - cut:2 * cb - cut]
        o_ref[3] = g_ref[:, 2 * cb - cut:]

    return pl.pallas_call(
        body, name="stack_w_in_grad", grid=(rows // tr,),
        in_specs=[pl.BlockSpec((tr, ZA_W), lambda i: (i, 0)), pl.BlockSpec((tr, d_w_g.shape[1]), lambda i: (i, 0))],
        out_specs=pl.BlockSpec((4, tr, cb), lambda i: (0, i, 0)),
        out_shape=SDS((4, rows, cb), d_w_a.dtype), compiler_params=_cp(("parallel",)))(d_w_a, d_w_g)


def _local_step(x, target, gains, w_a_of, sinks, s5w, comms, late_g, late, red=None):
    T = x.shape[0]
    D = D_MODEL
    g1, g2, g3, g4 = gains
    cos, sin = _rope_tables(T)
    lam_re, lam_im, log_dt, b_re, b_im, c_re, c_im, d_skip = s5w
    (a_re, a_im, bb_re, bb_im), disc_vjp = jax.vjp(_s5_discretize, lam_re, lam_im, log_dt, b_re, b_im)
    abr, abi, al_re, al_im = _s5_tables(a_re, a_im)
    prm = (jnp.concatenate([_blockdiag_in(bb_re), _blockdiag_in(bb_im)], axis=2),
           jnp.concatenate([_blockdiag_out(c_re), -_blockdiag_out(c_im)], axis=1),
           d_skip.reshape(1, SSM_W), abr, abi, al_re, al_im)
    mm = functools.partial(_mm, tm=1024, tn=1024, tk=2048)

    h = _rowwise(lambda xv, g: ((_rms(xv)[0] * g,), ()), [(x, D, 0)], [g1], [(D, BF16)], [], tr=512, name="norm1")[0]
    unpack = lambda res, comm: (res, ()) if comm is None else res
    w_a = w_a_of(h, *prm, cos, sin)
    za, got_a = unpack(_mm(h, w_a, mode="nn", out_dtype=F32, tm=1024, tn=1152, tk=2048, name="mm_za", comm=comms[0]), comms[0])
    w_g = late_g(got_a)
    zg, got0 = unpack(mm(h, w_g, mode="nn", out_dtype=BF16, name="mm_zg", tn=2048, comm=comms[1]), comms[1])
    o_attn, got1 = unpack(_attn_fwd(za, cos, sin, sinks, comm=comms[2]), comms[2])
    (yg, x0r, x0i), got2 = unpack(_s5_fwd(za, prm, comm=comms[3]), comms[3])
    w_glu, w_ba, w_bs, w_out, w_up, w_down = late(got0, got1, got2)
    zglu2 = mm(yg, w_glu[:, SSM_W:], mode="nn", out_dtype=BF16, name="mm_glu_gate")
    zglu1, o_ssm = mm(yg, w_glu[:, :SSM_W], mode="nn", out_dtype=(BF16, BF16), name="mm_glu", extras=(zglu2,),
                      epi=lambda v, z2: (v, v * _sig(z2.astype(F32))))
    ya = mm(o_attn, w_ba, mode="nn", out_dtype=BF16, name="mm_ya")
    ys, mi = mm(o_ssm, w_bs, mode="nn", out_dtype=(BF16, BF16), name="mm_ys", extras=(zg, zg, ya),
                extra_cols=(0, D // 1024, 0),
                epi=lambda v, ga, gs, a: (v, _sig(ga.astype(F32)) * a.astype(F32) + _sig(gs.astype(F32)) * v))
    mixed = mm(mi, w_out, mode="nn", out_dtype=F32, name="mm_out")

    def f_post(xv, mv, g2v, g3v):
        x1v = xv + _rms(mv)[0] * g2v
        return (x1v, _rms(x1v)[0] * g3v), ()
    x1, h2 = _rowwise(f_post, [(x, D, 0), (mixed, D, 0)], [g2, g3], [(D, F32), (D, BF16)], [], tr=256, name="post_mix")
    act = mm(h2, w_up, mode="nn", out_dtype=BF16, name="mm_up", tn=2048, epi=lambda v: jnp.maximum(v, 0.0))
    f = mm(act, w_down, mode="nn", out_dtype=F32, name="mm_down", a_fn=lambda v: v * v, tk=4096)

    def f_final(x1v, fv, tv, g4v):
        fn, r = _rms(fv)
        e = x1v + fn * g4v - tv
        dx2v = e * (1.0 / D)
        dfv, dg4v = _rms_bwd(dx2v, fn, r, g4v)
        return (dfv, dx2v), (dg4v, jnp.zeros((SUBLANES, LANES), F32) + 0.5 * jnp.sum(e * e) * (1.0 / D))
    df, dx2, dg4, lossb = _rowwise(f_final, [(x1, D, 0), (f, D, 0), (target, D, 0)], [g4],
                                   [(D, BF16), (D, F32)], [(1, D), (SUBLANES, LANES)], tr=256, name="final")

    big = {}

    def add(k, g4):
        big[k] = g4
        if red is not None:
            red.add(k, g4)

    def hosted(fn, stage, names):
        if red is None:
            return fn(comm=None)
        out, got = fn(comm=getattr(red, stage)(names))
        getattr(red, stage + "_done")(names, got)
        return out

    dpre = mm(df, w_down, mode="nt", out_dtype=BF16, name="mm_dact", tn=2048,
              epi=lambda v, a: v * (2.0 * a.astype(F32)), extras=(act,))
    wg = functools.partial(_mm, mode="tn", out_dtype=F32, tm=1024, tn=1024, tk=4096)
    add("w_down", wg(act, df, name="wg_down", a_fn=lambda v: v * v).reshape(4, D_FF // 4, D))
    dh2 = hosted(functools.partial(mm, dpre, w_up, mode="nt", out_dtype=BF16, name="mm_dh2", tk=4096),
                 "s1", ["w_down"])
    add("w_up", hosted(functools.partial(wg, h2, dpre, name="wg_up", shard_cols=D_FF // 4), "s3", ["w_down"]))

    def f_mid(dx2v, dh2v, x1v, mv, g2v, g3v):
        x1n, r3 = _rms(x1v)
        d3, dg3v = _rms_bwd(dh2v, x1n, r3, g3v)
        dx1v = dx2v + d3
        mn, r2 = _rms(mv)
        dmv, dg2v = _rms_bwd(dx1v, mn, r2, g2v)
        return (dx1v, dmv), (dg3v, dg2v)
    dx1, dmixed, dg3, dg2 = _rowwise(f_mid, [(dx2, D, 0), (dh2, D, 0), (x1, D, 0), (mixed, D, 0)], [g2, g3],
                                     [(D, F32), (D, BF16)], [(1, D), (1, D)], tr=256, name="mid")

    dmi = hosted(functools.partial(mm, dmixed, w_out, mode="nt", out_dtype=BF16, name="mm_dmi"), "s1", ["w_up"])
    add("w_out", wg(mi, dmixed, name="wg_out").reshape(4, D // 4, D))

    def f_gate(dv, ga, gs, a, s):
        sa, ss = _sig(ga), _sig(gs)
        return (dv * sa, dv * ss, jnp.concatenate([dv * a * sa * (1.0 - sa), dv * s * ss * (1.0 - ss)], axis=1)), ()
    dya, dys, dzg = _rowwise(f_gate, [(dmi, D, 0), (zg, D, 0), (zg, D, 1), (ya, D, 0), (ys, D, 0)], [],
                             [(D, BF16), (D, BF16), (2 * D, BF16)], [], tr=256, name="gate_bwd")
    do_attn = hosted(functools.partial(mm, dya, w_ba, mode="nt", out_dtype=BF16, name="mm_doa"), "s1", ["w_out"])
    d_w_ba = wg(o_attn, dya, name="wg_ba")
    do_ssm = mm(dys, w_bs, mode="nt", out_dtype=BF16, name="mm_dos")
    d_w_bs = wg(o_ssm, dys, name="wg_bs")
    add("w_branch", jnp.concatenate([d_w_ba.reshape(2, D // 4, D), d_w_bs.reshape(2, D // 4, D)], axis=0))

    def f_glu(dv, z1, z2):
        s2 = _sig(z2)
        return (jnp.concatenate([dv * s2, dv * z1 * s2 * (1.0 - s2)], axis=1),), ()
    dzglu = _rowwise(f_glu, [(do_ssm, SSM_W, 0), (zglu1, SSM_W, 0), (zglu2, SSM_W, 0)], [], [(2 * SSM_W, BF16)], [],
                     tr=512, name="glu_bwd")[0]
    dyg = hosted(functools.partial(mm, dzglu, w_glu, mode="nt", out_dtype=F32, name="mm_dyg"), "s1", ["w_branch"])
    add("w_glu", wg(yg, dzglu, name="wg_glu", tn=SSM_W // 2, shard_cols=SSM_W // 2))
    du, dar, dai, dbc, dcc, ddv = hosted(functools.partial(_s5_bwd, za, dyg, x0r, x0i, prm),
                                         "s3", ["w_up", "w_out", "w_branch"])
    dbr, dbi = dbc[:, :, :4 * LANES], dbc[:, :, 4 * LANES:]
    dcc = dcc.transpose(0, 2, 1)
    dcr, dci = dcc[:, :4 * LANES, :], -dcc[:, 4 * LANES:, :]
    dq, dkv, dsk = hosted(functools.partial(_attn_bwd, za, cos, sin, sinks, o_attn, do_attn),
                          "s5", ["w_down", "w_up", "w_out", "w_branch"])
    dza = jnp.concatenate([dq, dkv, du], axis=1)
    d_w_a = _mm(h, dza, mode="tn", out_dtype=F32, tm=1024, tn=ZA_W // 2, tk=2048, name="wg_a")
    d_w_g = wg(h, dzg, name="wg_g")
    add("w_in", _stack_w_in_grad(d_w_a, d_w_g))
    dh = hosted(functools.partial(mm, dza, w_a, mode="nt", out_dtype=F32, name="mm_dh_a", tk=ZA_W), "s1", ["w_in", "w_glu"])
    dh = hosted(functools.partial(mm, dzg, w_g, mode="nt", out_dtype=BF16, name="mm_dh_g",
                                  epi=lambda v, p: v + p, extras=(dh,)), "s3", ["w_in", "w_glu"])

    def f_first(dx1v, dhv, xv, g1v):
        xn, r1 = _rms(xv)
        d1, dg1v = _rms_bwd(dhv, xn, r1, g1v)
        return (dx1v + d1,), (dg1v,)
    dx, dg1 = _rowwise(f_first, [(dx1, D, 0), (dh, D, 0), (x, D, 0)], [g1], [(D, F32)], [(1, D)], tr=256, name="first")

    da_re = dar.sum(axis=1).reshape(SSM_G, SSM_P)
    da_im = dai.sum(axis=1).reshape(SSM_G, SSM_P)
    d_lam_re, d_lam_im, d_log_dt, d_b_re, d_b_im = disc_vjp(
        (da_re, da_im, _blockdiag_in_extract(dbr), _blockdiag_in_extract(dbi)))
    small = dict(norm_mix_pre=dg1, norm_mix_post=dg2, norm_mlp_pre=dg3, norm_mlp_post=dg4,
                 sinks=dsk[:, :N_Q_HEADS], lam_re=d_lam_re, lam_im=d_lam_im, log_dt=d_log_dt,
                 b_re=d_b_re, b_im=d_b_im, c_re=_blockdiag_out_extract(dcr), c_im=_blockdiag_out_extract(dci),
                 d_skip=ddv.reshape(SSM_G, SSM_GC))
    return lossb[0, 0], dx, small, big


def _cast_into_slot(w, k_arr):
    rows, cols = w.shape
    tr = 256

    def body(k_ref, w_ref, o_ref):
        o_ref[0] = w_ref[...].astype(BF16)

    return pl.pallas_call(
        body,
        name="cast_into_slot",
        grid_spec=pltpu.PrefetchScalarGridSpec(
            num_scalar_prefetch=1,
            grid=(rows // tr,),
            in_specs=[pl.BlockSpec((tr, cols), lambda i, k: (i, 0))],
            out_specs=pl.BlockSpec((1, tr, cols), lambda i, k: (k[0], i, 0)),
        ),
        out_shape=SDS((4, rows, cols), BF16),
        compiler_params=_cp(("parallel",)),
    )(k_arr, w)


def _pair_sum(g, r, c_arr):
    _, _, hr, cols = g.shape
    tr = min(256, hr)

    def body(c_ref, g_ref, r_ref, o_ref):
        o_ref[0] = (g_ref[0, 0] + r_ref[0]).astype(BF16)

    return pl.pallas_call(
        body,
        name="pair_sum",
        grid_spec=pltpu.PrefetchScalarGridSpec(
            num_scalar_prefetch=1,
            grid=(3, hr // tr),
            in_specs=[pl.BlockSpec((1, 1, tr, cols), lambda k, i, c_ref: (c_ref[1 + k], c_ref[0], i, 0)),
                      pl.BlockSpec((1, tr, cols), lambda k, i, c_ref: (c_ref[1 + k], i, 0))],
            out_specs=pl.BlockSpec((1, tr, cols), lambda k, i, c_ref: (c_ref[1 + k], i, 0)),
        ),
        out_shape=SDS((4, hr, cols), BF16),
        compiler_params=_cp(("parallel", "parallel")),
    )(c_arr, g, r)


def _chip_sum(g, r, q, kc_arr):
    _, _, hr, cols = g.shape
    tr = min(256, hr)

    def body(kc_ref, g_ref, r_ref, q_ref, o_ref):
        s = g_ref[0, 0] + r_ref[0]
        for j in range(3):
            s = s + q_ref[j].astype(F32)
        o_ref[...] = s

    return pl.pallas_call(
        body,
        name="chip_sum",
        grid_spec=pltpu.PrefetchScalarGridSpec(
            num_scalar_prefetch=1,
            grid=(hr // tr,),
            in_specs=[pl.BlockSpec((1, 1, tr, cols), lambda i, kc: (kc[0], kc[1], i, 0)),
                      pl.BlockSpec((1, tr, cols), lambda i, kc: (kc[0], i, 0)),
                      pl.BlockSpec((3, tr, cols), lambda i, kc: (0, i, 0))],
            out_specs=pl.BlockSpec((tr, cols), lambda i, kc: (kc[1] * (hr // tr) + i, 0)),
        ),
        out_shape=SDS((2 * hr, cols), F32),
        compiler_params=_cp(("parallel",)),
    )(kc_arr, g, r, q)


class _PairShareComm:
    aliased = True

    def __init__(self, blocks):
        self.arrs = list(blocks)
        self.n = len(self.arrs)
        dma = pltpu.SemaphoreType.DMA
        self.scratch = [dma((self.n,)), dma((self.n,))]
        self.out_shape = [SDS(b.shape, b.dtype) for b in self.arrs]

    def _copies(self, ins, outs, sems, hc):
        ssem, rsem = sems
        x, y, c, _ = _place()
        cps = []
        for w in range(self.n):
            hr = ins[w].shape[0] // 2
            rows = pl.ds(pl.multiple_of((c if hc == 0 else 1 - c) * hr, 8), hr)
            cps.append(_remote(ins[w].at[rows, :], outs[w].at[rows, :], ssem.at[w], rsem.at[w], (x, y, 1 - c)))
        return cps

    def start(self, ins, outs, sems):
        for cp in self._copies(ins, outs, sems, 0):
            cp.start()

    def finish(self, ins, outs, sems):
        for cp in self._copies(ins, outs, sems, 1):
            cp.wait_recv()
        for cp in self._copies(ins, outs, sems, 0):
            cp.wait_send()


class _GradReducer:
    def __init__(self, c_arr, kc_arr):
        self.c_arr, self.kc_arr = c_arr, kc_arr
        self.g, self.r, self.ps, self.q, self.done = {}, {}, {}, {}, {}

    def add(self, k, g4):
        self.g[k] = g4.reshape(4, 2, g4.shape[1] // 2, g4.shape[2])

    def s1(self, names):
        return _PairExchangeComm([self.g[k].reshape(4, -1, self.g[k].shape[3]) for k in names])

    def s1_done(self, names, got):
        for k, r in zip(names, got):
            self.r[k] = r
            self.ps[k] = _pair_sum(self.g[k], r, self.c_arr)

    def s3(self, names):
        return _ChipExchangeComm([self.ps[k] for k in names])

    def s3_done(self, names, got):
        self.q.update(zip(names, got))

    def finish(self, order):
        rest = [k for k in order if k not in self.r]
        if rest:
            self.s1_done(rest, _comm_only("pair_exchange", self.s1(rest)))
        rest = [k for k in order if k not in self.q]
        if rest:
            self.s3_done(rest, _comm_only("chip_exchange", self.s3(rest)))
        rest = [k for k in order if k not in self.done]
        if rest:
            self.s5_done(rest, _comm_only("pair_share", self.s5(rest)))
        return self.done

    def s5(self, names):
        return _PairShareComm([_chip_sum(self.g[k], self.r[k], self.q[k], self.kc_arr) for k in names])

    def s5_done(self, names, got):
        self.done.update(zip(names, got))


def _all_reduce_small(buf):
    rows = buf.shape[0]
    hr = rows // 2
    assert hr % SUBLANES == 0

    def body(in_ref, o_ref, sib, pair, slots, ssem, rsem):
        x, y, c, others = _place()
        me, sibling = 2 * x + y, (x, y, 1 - c)
        mine = pl.ds(pl.multiple_of(c * hr, SUBLANES), hr)
        theirs = pl.ds(pl.multiple_of((1 - c) * hr, SUBLANES), hr)
        first = _remote(in_ref, sib, ssem.at[0], rsem.at[0], sibling)
        first.start()
        first.wait()
        pair[...] = in_ref[...] + sib[...]
        slots[me] = pair[mine, :]
        cps = [_remote(pair.at[mine, :], slots.at[me], ssem.at[1 + r], rsem.at[1 + r], (ox, oy, c))
               for r, (ox, oy) in enumerate(others)]
        for cp in cps:
            cp.start()
        for r, (ox, oy) in enumerate(others):
            _remote(pair.at[mine, :], slots.at[2 * ox + oy], ssem.at[1 + r], rsem.at[1 + r], (ox, oy, c)).wait_recv()
        o_ref[mine, :] = (slots[0] + slots[1]) + (slots[2] + slots[3])
        last = _remote(o_ref.at[mine, :], o_ref.at[mine, :], ssem.at[4], rsem.at[4], sibling)
        last.start()
        _remote(o_ref.at[theirs, :], o_ref.at[theirs, :], ssem.at[4], rsem.at[4], sibling).wait_recv()
        last.wait_send()
        for cp in cps:
            cp.wait_send()

    dma = pltpu.SemaphoreType.DMA
    return pl.pallas_call(
        body,
        name="all_reduce_small",
        in_specs=[pl.BlockSpec(memory_space=pltpu.VMEM)],
        out_specs=pl.BlockSpec(memory_space=pltpu.VMEM),
        out_shape=SDS(buf.shape, F32),
        scratch_shapes=[pltpu.VMEM((rows, LANES), F32), pltpu.VMEM((rows, LANES), F32),
                        pltpu.VMEM((4, hr, LANES), F32), dma((5,)), dma((5,))],
        compiler_params=pltpu.CompilerParams(vmem_limit_bytes=VMEM_LIMIT),
    )(buf)


def _adam_fn(w, g, m, v):
    m2 = ADAM_B1 * m + (1.0 - ADAM_B1) * g
    v2 = ADAM_B2 * v + (1.0 - ADAM_B2) * (g * g)
    m_hat = m2 / (1.0 - ADAM_B1 ** ADAM_STEP)
    v_hat = v2 / (1.0 - ADAM_B2 ** ADAM_STEP)
    return (-ADAM_LR * (m_hat / (jnp.sqrt(v_hat) + ADAM_EPS) + ADAM_WD * w), m2, v2), ()


def _adamw(w, g, m, v, name, tr=256):
    cols = w.shape[1]
    return _rowwise(_adam_fn, [(w, cols, 0), (g, cols, 0), (m, cols, 0), (v, cols, 0)], [],
                    [(cols, F32)] * 3, [], tr=tr, name=name)


BIG = ("w_in", "w_glu", "w_branch", "w_out", "w_up", "w_down")
COL_SHARDED = ("w_in", "w_glu", "w_up")
SMALL = ("norm_mix_pre", "norm_mix_post", "norm_mlp_pre", "norm_mlp_post", "sinks", "lam_re", "lam_im", "log_dt",
         "b_re", "b_im", "c_re", "c_im", "d_skip")
WEIGHTS = ("norm_mix_pre", "norm_mix_post", "norm_mlp_pre", "norm_mlp_post", "w_in", "sinks", "lam_re", "lam_im",
           "log_dt", "b_re", "b_im", "c_re", "c_im", "d_skip", "w_glu", "w_branch", "w_out", "w_up", "w_down")


def _flat_small(vals, extra):
    flat = jnp.concatenate([vals[k].reshape(-1) for k in SMALL] + [extra.reshape(-1)])
    rows = -(-flat.shape[0] // (SUBLANES * LANES)) * SUBLANES
    return jnp.pad(flat, (0, rows * LANES - flat.shape[0])).reshape(rows, LANES)


def kernel(x, norm_mix_pre, norm_mix_post, norm_mlp_pre, norm_mlp_post, w_in, sinks, lam_re, lam_im, log_dt, b_re, b_im, c_re, c_im, d_skip, w_glu, w_branch, w_out, w_up, w_down, loss_target, m_norm_mix_pre, m_norm_mix_post, m_norm_mlp_pre, m_norm_mlp_post, m_w_in, m_sinks, m_lam_re, m_lam_im, m_log_dt, m_b_re, m_b_im, m_c_re, m_c_im, m_d_skip, m_w_glu, m_w_branch, m_w_out, m_w_up, m_w_down, v_norm_mix_pre, v_norm_mix_post, v_norm_mlp_pre, v_norm_mlp_post, v_w_in, v_sinks, v_lam_re, v_lam_im, v_log_dt, v_b_re, v_b_im, v_c_re, v_c_im, v_d_skip, v_w_glu, v_w_branch, v_w_out, v_w_up, v_w_down):
    w = dict(norm_mix_pre=norm_mix_pre, norm_mix_post=norm_mix_post, norm_mlp_pre=norm_mlp_pre, norm_mlp_post=norm_mlp_post,
             w_in=w_in, sinks=sinks, lam_re=lam_re, lam_im=lam_im, log_dt=log_dt, b_re=b_re, b_im=b_im, c_re=c_re,
             c_im=c_im, d_skip=d_skip, w_glu=w_glu, w_branch=w_branch, w_out=w_out, w_up=w_up, w_down=w_down)
    m = dict(norm_mix_pre=m_norm_mix_pre, norm_mix_post=m_norm_mix_post, norm_mlp_pre=m_norm_mlp_pre,
             norm_mlp_post=m_norm_mlp_post, w_in=m_w_in, sinks=m_sinks, lam_re=m_lam_re, lam_im=m_lam_im,
             log_dt=m_log_dt, b_re=m_b_re, b_im=m_b_im, c_re=m_c_re, c_im=m_c_im, d_skip=m_d_skip, w_glu=m_w_glu,
             w_branch=m_w_branch, w_out=m_w_out, w_up=m_w_up, w_down=m_w_down)
    v = dict(norm_mix_pre=v_norm_mix_pre, norm_mix_post=v_norm_mix_post, norm_mlp_pre=v_norm_mlp_pre,
             norm_mlp_post=v_norm_mlp_post, w_in=v_w_in, sinks=v_sinks, lam_re=v_lam_re, lam_im=v_lam_im,
             log_dt=v_log_dt, b_re=v_b_re, b_im=v_b_im, c_re=v_c_re, c_im=v_c_im, d_skip=v_d_skip, w_glu=v_w_glu,
             w_branch=v_w_branch, w_out=v_w_out, w_up=v_w_up, w_down=v_w_down)
    xi, yi, ci = lax.axis_index("x"), lax.axis_index("y"), lax.axis_index("c")

    k_arr = jnp.stack([2 * xi + yi]).astype(jnp.int32)
    *w_in_sems, w_in_thru = _gather_start(_cast_into_slot(w["w_in"][0], k_arr))
    slot = {k: _cast_into_slot(w[k][0], k_arr) for k in BIG if k != "w_in"}

    def whole(k, g4):
        if k in COL_SHARDED:
            return jnp.concatenate([g4[j] for j in range(4)], axis=1)
        return g4.reshape(4 * g4.shape[1], g4.shape[2])

    w_in = []

    def w_a_of(*after):
        arrived = _gather_wait(w_in_sems, w_in_thru, after + tuple(slot.values()))
        w_in.append(_comm_only("gather_w_in_pass", _PassOnComm([arrived]))[0])
        return _assemble_w_a(w_in[0])

    hosted = (("w_glu", "w_branch", "w_out"), ("w_up",), ("w_down",))
    comms = [None] + [_GatherComm([slot[k] for k in names]) for names in hosted]

    def late(*got):
        f = {k: whole(k, g4) for names, res in zip(hosted, got) for k, g4 in zip(names, res)}
        return f["w_glu"], f["w_branch"][:Q_W], f["w_branch"][Q_W:], f["w_out"], f["w_up"], f["w_down"]

    s5w = (lam_re[0], lam_im[0], log_dt[0], b_re[0], b_im[0], c_re[0], c_im[0], d_skip[0])
    reducer = _GradReducer(
        jnp.stack([ci, 2 * (1 - xi) + yi, 2 * xi + (1 - yi), 2 * (1 - xi) + (1 - yi)]).astype(jnp.int32),
        jnp.stack([2 * xi + yi, ci]).astype(jnp.int32))
    loss_part, dx, small, _ = _local_step(
        x[0], loss_target[0], (norm_mix_pre, norm_mix_post, norm_mlp_pre, norm_mlp_post),
        w_a_of, sinks, s5w, comms, lambda _: _assemble_w_g(w_in[0]), late, reducer)
    grads = reducer.finish(BIG)

    red = _all_reduce_small(_flat_small(small, loss_part)).reshape(-1)
    off = 0
    for k in SMALL:
        n = math.prod(w[k].shape)
        grads[k] = red[off:off + n].reshape(w[k].shape[1:])
        off += n
    loss = red[off]

    delta, new_m, new_v = {}, {}, {}
    for k in BIG:
        delta[k], new_m[k], new_v[k] = _adamw(w[k][0], grads[k], m[k][0], v[k][0], "adamw_" + k)
    zero = jnp.zeros((), F32)
    fw, fm, fv = (_flat_small({k: t[k] for k in SMALL}, zero) for t in (w, m, v))
    fg = _flat_small(grads, zero)
    sd, sm, sv = _adamw(fw, fg, fm, fv, "adamw_small", tr=fw.shape[0])
    off = 0
    for k in SMALL:
        n = math.prod(w[k].shape)
        delta[k], new_m[k], new_v[k] = (t.reshape(-1)[off:off + n].reshape(w[k].shape[1:]) for t in (sd, sm, sv))
        off += n

    lead = lambda t: t[None]
    return (loss, lead(dx), *[lead(grads[k]) for k in WEIGHTS], *[lead(delta[k]) for k in WEIGHTS],
            *[lead(new_m[k]) for k in WEIGHTS], *[lead(new_v[k]) for k in WEIGHTS])
```
